```python
import jax, jax.numpy as jnp
from jax import lax
import numpy as np

D_MODEL = 1024
BATCH = 4
SEQ = 4096
DEPTH = 2

D_MIX = D_MODEL
D_LRU = D_MIX // 2
LRU_HEADS = 8
LRU_HEAD_DIM = D_LRU // LRU_HEADS
CONV_WIDTH = 4
LRU_C = 8.0
D_HGRN = D_MIX - D_LRU
HGRN_HEADS = 8
HGRN_HEAD_DIM = D_HGRN // HGRN_HEADS
HGRN_CHUNK = 64
N_GROUPS = 4
EXPERTS_PER_GROUP = 8
N_EXPERTS = N_GROUPS * EXPERTS_PER_GROUP
TOP_K = 2
D_EXPERT = D_MODEL // 2
MOE_BLOCK = 128
NORM_EPS = 1e-6
PROJ_WIDTH = 2 * D_LRU + 5 * D_HGRN

kernel_name = 'hybrid_rglru_hgrn2_hmoe_encoder'


def rms_norm(x, w):
    xf = x.astype(jnp.float32)
    y = xf * lax.rsqrt(jnp.mean(xf * xf, axis=-1, keepdims=True) + NORM_EPS)
    return (y * w.astype(jnp.float32)).astype(x.dtype)


def centred_depthwise_conv(x, w, b):
    pad_l = CONV_WIDTH // 2
    pad_r = CONV_WIDTH - 1 - pad_l
    y = lax.conv_general_dilated(x, w[:, None, :].astype(x.dtype), window_strides=(1,),
                                 padding=[(pad_l, pad_r)], dimension_numbers=('NWC', 'WIO', 'NWC'),
                                 feature_group_count=x.shape[-1])
    return y + b.astype(x.dtype)


def linear_scan(a, b, reverse):
    def combine(left, right):
        a_l, b_l = left
        a_r, b_r = right
        return a_l * a_r, a_r * b_l + b_r
    _, h = lax.associative_scan(combine, (a, b), reverse=reverse, axis=1)
    return h


def rg_lru_direction(xc, wa, ba, wx, bx, lam, reverse):
    bsz, seq, _ = xc.shape
    xh = xc.reshape(bsz, seq, LRU_HEADS, LRU_HEAD_DIM)
    r = jax.nn.sigmoid(jnp.einsum('bshi,hij->bshj', xh, wa).reshape(bsz, seq, D_LRU) + ba)
    i = jax.nn.sigmoid(jnp.einsum('bshi,hij->bshj', xh, wx).reshape(bsz, seq, D_LRU) + bx)
    log_a = -LRU_C * r * jax.nn.softplus(-lam)
    u = jnp.sqrt(-jnp.expm1(2.0 * log_a)) * (i * xc)
    return linear_scan(jnp.exp(log_a), u, reverse)


def hgrn2_chunkwise(q, k, v, log_f):
    bsz, seq, heads, dk = q.shape
    dv = v.shape[-1]
    n_chunks = seq // HGRN_CHUNK

    def to_chunks(t):
        return t.reshape(bsz, n_chunks, HGRN_CHUNK, heads, t.shape[-1]).transpose(1, 0, 3, 2, 4)

    incl = jnp.tril(jnp.ones((HGRN_CHUNK, HGRN_CHUNK), dtype=bool))[:, :, None]

    def step(state, chunk):
        qc, kc, vc, lf = chunk
        b = jnp.cumsum(lf, axis=2)
        inter = jnp.einsum('bhtd,bhde->bhte', qc * jnp.exp(b), state)
        diff = b[:, :, :, None, :] - b[:, :, None, :, :]
        decay = jnp.exp(jnp.where(incl, diff, -jnp.inf))
        scores = jnp.einsum('bhtd,bhsd,bhtsd->bhts', qc, kc, decay)
        intra = jnp.einsum('bhts,bhse->bhte', scores, vc)
        b_last = b[:, :, -1:, :]
        new_state = (jnp.exp(b_last[:, :, 0, :])[..., None] * state
                     + jnp.einsum('bhsd,bhse->bhde', kc * jnp.exp(b_last - b), vc))
        return new_state, inter + intra

    state0 = jnp.zeros((bsz, heads, dk, dv), jnp.float32)
    _, o = lax.scan(step, state0, (to_chunks(q), to_chunks(k), to_chunks(v), to_chunks(log_f)))
    return o.transpose(1, 0, 3, 2, 4).reshape(bsz, seq, heads, dv)


def hgrn2_direction(q, v, f_logit, lb, reverse):
    f = lb + (1.0 - lb) * jax.nn.sigmoid(f_logit)
    log_f = jnp.log(f)
    k = 1.0 - f
    if reverse:
        flip = lambda t: jnp.flip(t, axis=1)
        return flip(hgrn2_chunkwise(flip(q), flip(k), flip(v), flip(log_f)))
    return hgrn2_chunkwise(q, k, v, log_f)


def hybrid_mixer(h, w_in, conv_w, conv_b, lru_wa, lru_ba, lru_wx, lru_bx, lru_lambda,
                 norm_lru_w, lb, norm_hgrn_w, w_out):
    f32 = jnp.float32
    bsz, seq, _ = h.shape
    proj = h @ w_in
    cuts = [D_LRU, 2 * D_LRU, 2 * D_LRU + D_HGRN, 2 * D_LRU + 2 * D_HGRN,
            2 * D_LRU + 3 * D_HGRN, 2 * D_LRU + 4 * D_HGRN]
    x_lru, y_lru, q, f_fwd, f_bwd, v_in, g = jnp.split(proj, cuts, axis=-1)

    xc = centred_depthwise_conv(x_lru, conv_w, conv_b).astype(f32)
    lru = (rg_lru_direction(xc, lru_wa[0].astype(f32), lru_ba[0].astype(f32), lru_wx[0].astype(f32),
                            lru_bx[0].astype(f32), lru_lambda[0].astype(f32), False)
           + rg_lru_direction(xc, lru_wa[1].astype(f32), lru_ba[1].astype(f32), lru_wx[1].astype(f32),
                              lru_bx[1].astype(f32), lru_lambda[1].astype(f32), True))
    lru = rms_norm(lru * jax.nn.gelu(y_lru.astype(f32)), norm_lru_w)

    heads = lambda t: t.astype(f32).reshape(bsz, seq, HGRN_HEADS, HGRN_HEAD_DIM)
    qh, vh = heads(q), heads(v_in)
    lbh = lb.astype(f32).reshape(HGRN_HEADS, HGRN_HEAD_DIM)
    hg = hgrn2_direction(qh, vh, heads(f_fwd), lbh, False) + hgrn2_direction(qh, vh, heads(f_bwd), lbh, True)
    hg = rms_norm(hg, norm_hgrn_w.reshape(HGRN_HEADS, HGRN_HEAD_DIM)) * jax.nn.silu(heads(g))

    merged = jnp.concatenate([lru, hg.reshape(bsz, seq, D_HGRN)], axis=-1).astype(h.dtype)
    return merged @ w_out


def hierarchical_moe(h, wg_r, bg_r, we_r, be_r, w_gate, w_up, w_down):
    f32 = jnp.float32
    bsz, seq, d = h.shape
    xf = h.reshape(-1, d)
    m = xf.shape[0]
    g_logits = (xf @ wg_r + bg_r).astype(f32)
    g_idx = jnp.argmax(g_logits, axis=-1)
    p_group = jnp.take_along_axis(jax.nn.softmax(g_logits, axis=-1), g_idx[:, None], axis=1)[:, 0]
    e_logits = (xf @ we_r + be_r).astype(f32).reshape(m, N_GROUPS, EXPERTS_PER_GROUP)
    e_in_g = jnp.take_along_axis(e_logits, g_idx[:, None, None], axis=1)[:, 0]
    top_v, top_i = lax.top_k(e_in_g, TOP_K)
    w_tok = p_group[:, None] * jax.nn.softmax(top_v, axis=-1)
    expert = g_idx[:, None] * EXPERTS_PER_GROUP + top_i

    n_assign = m * TOP_K
    e_flat = expert.reshape(-1).astype(jnp.int32)
    tok_flat = jnp.repeat(jnp.arange(m, dtype=jnp.int32), TOP_K)
    w_flat = w_tok.reshape(-1)
    order = jnp.argsort(e_flat)
    e_s, tok_s, w_s = e_flat[order], tok_flat[order], w_flat[order]
    counts = jnp.bincount(e_flat, length=N_EXPERTS).astype(jnp.int32)
    start = jnp.cumsum(counts) - counts
    padded = ((counts + MOE_BLOCK - 1) // MOE_BLOCK) * MOE_BLOCK
    pend = jnp.cumsum(padded)
    pstart = pend - padded
    dest = pstart[e_s] + (jnp.arange(n_assign, dtype=jnp.int32) - start[e_s])
    n_rows = n_assign + N_EXPERTS * MOE_BLOCK
    n_blocks = n_rows // MOE_BLOCK
    tok_buf = jnp.zeros((n_rows,), jnp.int32).at[dest].set(tok_s)
    w_buf = jnp.zeros((n_rows,), f32).at[dest].set(w_s)
    blk_start = jnp.arange(n_blocks, dtype=jnp.int32) * MOE_BLOCK
    blk_expert = jnp.minimum(jnp.sum(pend[None, :] <= blk_start[:, None], axis=1), N_EXPERTS - 1)
    x_buf = xf[tok_buf].reshape(n_blocks, MOE_BLOCK, d)

    def expert_block(args):
        xb, e = args
        return (jax.nn.silu(xb @ w_gate[e]) * (xb @ w_up[e])) @ w_down[e]

    y_buf = lax.map(expert_block, (x_buf, blk_expert)).reshape(n_rows, d)
    y = jnp.zeros((m, d), h.dtype).at[tok_buf].add((y_buf * w_buf[:, None]).astype(h.dtype))
    return y.reshape(bsz, seq, d)


def setup_inputs(seed: int = 0) -> dict:
    key = jax.random.key(seed)
    ks = jax.random.split(key, 26)
    f32 = jnp.float32
    nrm = lambda k, shape, scale: jax.random.normal(k, shape, f32) * scale
    a_lo = 0.9 ** (1.0 / LRU_C)
    a_hi = 0.999 ** (1.0 / LRU_C)
    a0 = jax.random.uniform(ks[12], (DEPTH, 2, D_LRU), f32, minval=a_lo, maxval=a_hi)
    return {
        'x': nrm(ks[0], (BATCH, SEQ, D_MODEL), 1.0),
        'c': nrm(ks[1], (BATCH, D_MODEL), 1.0),
        'ada_w': nrm(ks[2], (DEPTH, D_MODEL, 6 * D_MODEL), 0.5 * D_MODEL ** -0.5),
        'ada_b': nrm(ks[3], (DEPTH, 6 * D_MODEL), 0.02),
        'norm_mix_w': 1.0 + nrm(ks[4], (DEPTH, D_MODEL), 0.02),
        'w_in': nrm(ks[5], (DEPTH, D_MODEL, PROJ_WIDTH), D_MODEL ** -0.5),
        'conv_w': nrm(ks[6], (DEPTH, CONV_WIDTH, D_LRU), 0.5),
        'conv_b': nrm(ks[7], (DEPTH, D_LRU), 0.02),
        'lru_wa': nrm(ks[8], (DEPTH, 2, LRU_HEADS, LRU_HEAD_DIM, LRU_HEAD_DIM), LRU_HEAD_DIM ** -0.5),
        'lru_ba': nrm(ks[9], (DEPTH, 2, D_LRU), 0.02),
        'lru_wx': nrm(ks[10], (DEPTH, 2, LRU_HEADS, LRU_HEAD_DIM, LRU_HEAD_DIM), LRU_HEAD_DIM ** -0.5),
        'lru_bx': nrm(ks[11], (DEPTH, 2, D_LRU), 0.02),
        'lru_lambda': jnp.log(a0) - jnp.log1p(-a0),
        'norm_lru_w': 1.0 + nrm(ks[13], (DEPTH, D_LRU), 0.02),
        'hgrn_lb': nrm(ks[14], (DEPTH, D_HGRN), 0.5),
        'norm_hgrn_w': 1.0 + nrm(ks[15], (DEPTH, D_HGRN), 0.02),
        'w_out': nrm(ks[16], (DEPTH, D_MIX, D_MODEL), D_MIX ** -0.5),
        'norm_ffn_w': 1.0 + nrm(ks[17], (DEPTH, D_MODEL), 0.02),
        'router_group_w': nrm(ks[18], (DEPTH, D_MODEL, N_GROUPS), D_MODEL ** -0.5),
        'router_group_b': nrm(ks[19], (DEPTH, N_GROUPS), 0.01),
        'router_expert_w': nrm(ks[20], (DEPTH, D_MODEL, N_EXPERTS), D_MODEL ** -0.5),
        'router_expert_b': nrm(ks[21], (DEPTH, N_EXPERTS), 0.01),
        'expert_w_gate': nrm(ks[22], (DEPTH, N_EXPERTS, D_MODEL, D_EXPERT), D_MODEL ** -0.5),
        'expert_w_up': nrm(ks[23], (DEPTH, N_EXPERTS, D_MODEL, D_EXPERT), D_MODEL ** -0.5),
        'expert_w_down': nrm(ks[24], (DEPTH, N_EXPERTS, D_EXPERT, D_MODEL), D_EXPERT ** -0.5),
        'final_norm_w': 1.0 + nrm(ks[25], (D_MODEL,), 0.02),
    }


def reference(x, c, ada_w, ada_b, norm_mix_w, w_in, conv_w, conv_b, lru_wa, lru_ba, lru_wx, lru_bx,
              lru_lambda, norm_lru_w, hgrn_lb, norm_hgrn_w, w_out, norm_ffn_w, router_group_w,
              router_group_b, router_expert_w, router_expert_b, expert_w_gate, expert_w_up,
              expert_w_down, final_norm_w):
    lb_cum = jnp.cumsum(jax.nn.softmax(hgrn_lb.astype(jnp.float32), axis=0), axis=0)
    lb_all = lb_cum - lb_cum[0:1]
    cond = jax.nn.silu(c)
    for l in range(DEPTH):
        mod = (cond @ ada_w[l] + ada_b[l])[:, None, :]
        sh_mix, sc_mix, g_mix, sh_ffn, sc_ffn, g_ffn = jnp.split(mod, 6, axis=-1)
        h = rms_norm(x, norm_mix_w[l]) * (1.0 + sc_mix) + sh_mix
        x = x + g_mix * hybrid_mixer(h, w_in[l], conv_w[l], conv_b[l], lru_wa[l], lru_ba[l], lru_wx[l],
                                     lru_bx[l], lru_lambda[l], norm_lru_w[l], lb_all[l], norm_hgrn_w[l],
                                     w_out[l])
        h = rms_norm(x, norm_ffn_w[l]) * (1.0 + sc_ffn) + sh_ffn
        x = x + g_ffn * hierarchical_moe(h, router_group_w[l], router_group_b[l], router_expert_w[l],
                                         router_expert_b[l], expert_w_gate[l], expert_w_up[l],
                                         expert_w_down[l])
    return rms_norm(x, final_norm_w)
```

```python
import functools

import jax
import jax.numpy as jnp
from jax import lax
from jax.experimental import pallas as pl
from jax.experimental.pallas import tpu as pltpu

F32 = jnp.float32
BF16 = jnp.bfloat16

LRU_HEADS = 8
HGRN_HEADS = 8
CONV_WIDTH = 4
LRU_C = 8.0
N_GROUPS = 4
EXPERTS_PER_GROUP = 8
N_EXPERTS = N_GROUPS * EXPERTS_PER_GROUP
NORM_EPS = 1e-6

LANES = 128
SUBLANES = 8
VMEM_LIMIT = 56 * 1024 * 1024

HGRN_CHUNK = 64
HGRN_SUB = 16
ROUTE_LANES = LANES
EXPERT_BLOCK = 256
NEG_BIG = -3.0e38


def _params(sem):
    return pltpu.CompilerParams(dimension_semantics=sem, vmem_limit_bytes=VMEM_LIMIT)


def _dot(a, b):
    return jnp.dot(a, b, preferred_element_type=F32)


def _dot_nt(a, b):
    return lax.dot_general(a, b, (((1,), (1,)), ((), ())), preferred_element_type=F32)


def _dot_tn(a, b):
    return lax.dot_general(a, b, (((0,), (0,)), ((), ())), preferred_element_type=F32)


def _dot01_exact(m01, x):
    hi = x.astype(BF16)
    r1 = x - hi.astype(F32)
    mid = r1.astype(BF16)
    lo = (r1 - mid.astype(F32)).astype(BF16)
    return _dot(m01, hi) + _dot(m01, mid) + _dot(m01, lo)


def _sigmoid(x):
    return 1.0 / (1.0 + jnp.exp(-x))


def _mod_kernel(c_ref, w_ref, b_ref, o_ref):
    c = c_ref[...]
    cond = c * _sigmoid(c)
    o_ref[0] = _dot(cond.astype(BF16), w_ref[0].astype(BF16)) + b_ref[0]


def _modulation(c, ada_w, ada_b):
    depth, d, n = ada_w.shape
    bsz = c.shape[0]
    rows = -(-bsz // SUBLANES) * SUBLANES
    c_pad = jnp.pad(c, ((0, rows - bsz), (0, 0)))
    tn = n // 6
    out = pl.pallas_call(
        _mod_kernel,
        out_shape=jax.ShapeDtypeStruct((depth, rows, n), F32),
        grid=(depth, n // tn),
        in_specs=[
            pl.BlockSpec((rows, d), lambda l, j: (0, 0)),
            pl.BlockSpec((1, d, tn), lambda l, j: (l, 0, j)),
            pl.BlockSpec((1, 1, tn), lambda l, j: (l, 0, j)),
        ],
        out_specs=pl.BlockSpec((1, rows, tn), lambda l, j: (l, 0, j)),
        compiler_params=_params(("arbitrary", "arbitrary")),
        name="adaln_mod",
    )(c_pad, ada_w, ada_b.reshape(depth, 1, n))
    return out[:, :bsz]


def _rms_mod(x, nw, sc, sh):
    ms = jnp.mean(x * x, axis=-1, keepdims=True)
    return (x * lax.rsqrt(ms + NORM_EPS) * nw) * (1.0 + sc) + sh


def _inproj_kernel(x_ref, nw_ref, sc_ref, sh_ref, w_ref, o_ref):
    h = _rms_mod(x_ref[0], nw_ref[...], sc_ref[0], sh_ref[0])
    o_ref[0] = _dot(h.astype(BF16), w_ref[...])


def _in_proj(x, nw, sc, sh, w_bf16, tm=512):
    bsz, seq, d = x.shape
    n = w_bf16.shape[1]
    return pl.pallas_call(
        _inproj_kernel,
        out_shape=jax.ShapeDtypeStruct((bsz, seq, n), F32),
        grid=(bsz, seq // tm),
        in_specs=[
            pl.BlockSpec((1, tm, d), lambda b, i: (b, i, 0)),
            pl.BlockSpec((1, d), lambda b, i: (0, 0)),
            pl.BlockSpec((1, 1, d), lambda b, i: (b, 0, 0)),
            pl.BlockSpec((1, 1, d), lambda b, i: (b, 0, 0)),
            pl.BlockSpec((d, n), lambda b, i: (0, 0)),
        ],
        out_specs=pl.BlockSpec((1, tm, n), lambda b, i: (b, i, 0)),
        compiler_params=_params(("arbitrary", "arbitrary")),
        name="in_proj",
    )(x, nw.reshape(1, d), sc, sh, w_bf16)


def _lru_kernel(x_ref, xp_ref, xn_ref, cw_ref, cb_ref, wa_ref, ba_ref, wx_ref, bx_ref, lam_ref,
                o_ref, carry_ref, *, reverse, n_chunks, rows):
    c = pl.program_id(1)
    chunk = (n_chunks - 1 - c) if reverse else c

    @pl.when(c == 0)
    def _():
        carry_ref[...] = jnp.zeros_like(carry_ref)

    x = x_ref[0]
    width = x.shape[1]
    row = lax.broadcasted_iota(jnp.int32, (rows, width), 0)
    has_prev = jnp.where(chunk > 0, 1.0, 0.0)
    has_next = jnp.where(chunk < n_chunks - 1, 1.0, 0.0)
    xp = xp_ref[0] * has_prev
    xn = xn_ref[0] * has_next
    p6, p7, n0 = xp[6:7], xp[7:8], xn[0:1]
    xm1 = jnp.where(row == 0, p7, pltpu.roll(x, 1, 0))
    xm2 = jnp.where(row == 0, p6, jnp.where(row == 1, p7, pltpu.roll(x, 2, 0)))
    xp1 = jnp.where(row == rows - 1, n0, pltpu.roll(x, rows - 1, 0))
    cw = cw_ref[...]
    xc = cw[0:1] * xm2 + cw[1:2] * xm1 + cw[2:3] * x + cw[3:4] * xp1 + cb_ref[...]

    xcb = xc.astype(BF16)
    r = _sigmoid(_dot(xcb, wa_ref[0]) + ba_ref[0])
    gate_i = _sigmoid(_dot(xcb, wx_ref[0]) + bx_ref[0])
    lam = lam_ref[0]
    softplus_neg_lam = jnp.maximum(-lam, 0.0) + jnp.log1p(jnp.exp(-jnp.abs(lam)))
    log_a = (-LRU_C) * r * softplus_neg_lam
    a = jnp.exp(log_a)
    t = jnp.tanh(-log_a)
    u = jnp.sqrt(2.0 * t / (1.0 + t)) * (gate_i * xc)

    acc_a, acc_b = a, u
    s = 1
    while s < rows:
        if reverse:
            valid = row < rows - s
            sh_a, sh_b = pltpu.roll(acc_a, rows - s, 0), pltpu.roll(acc_b, rows - s, 0)
        else:
            valid = row >= s
            sh_a, sh_b = pltpu.roll(acc_a, s, 0), pltpu.roll(acc_b, s, 0)
        acc_b = jnp.where(valid, acc_a * sh_b + acc_b, acc_b)
        acc_a = jnp.where(valid, acc_a * sh_a, acc_a)
        s *= 2
    h = acc_b + acc_a * carry_ref[...]
    o_ref[0] = h
    carry_ref[...] = h[0:1] if reverse else h[rows - 1:rows]


def _lru_scan(proj, conv_w, conv_b, wa_bd, ba, wx_bd, bx, lam, *, reverse, rows=256):
    bsz, seq, _ = proj.shape
    d_lru = conv_w.shape[1]
    n_chunks = seq // rows
    halo = rows // SUBLANES
    last_halo = seq // SUBLANES - 1
    dirn = 1 if reverse else 0

    def chunk_of(c):
        return (n_chunks - 1 - c) if reverse else c

    vec = lambda: pl.BlockSpec((1, 1, d_lru), lambda b, c: (dirn, 0, 0))
    mat = lambda: pl.BlockSpec((1, d_lru, d_lru), lambda b, c: (dirn, 0, 0))
    kern = functools.partial(_lru_kernel, reverse=reverse, n_chunks=n_chunks, rows=rows)
    return pl.pallas_call(
        kern,
        out_shape=jax.ShapeDtypeStruct((bsz, seq, d_lru), F32),
        grid=(bsz, n_chunks),
        in_specs=[
            pl.BlockSpec((1, rows, d_lru), lambda b, c: (b, chunk_of(c), 0)),
            pl.BlockSpec((1, SUBLANES, d_lru),
                         lambda b, c: (b, jnp.maximum(chunk_of(c) * halo - 1, 0), 0)),
            pl.BlockSpec((1, SUBLANES, d_lru),
                         lambda b, c: (b, jnp.minimum((chunk_of(c) + 1) * halo, last_halo), 0)),
            pl.BlockSpec((CONV_WIDTH, d_lru), lambda b, c: (0, 0)),
            pl.BlockSpec((1, d_lru), lambda b, c: (0, 0)),
            mat(), vec(), mat(), vec(), vec(),
        ],
        out_specs=pl.BlockSpec((1, rows, d_lru), lambda b, c: (b, chunk_of(c), 0)),
        scratch_shapes=[pltpu.VMEM((1, d_lru), F32)],
        compiler_params=_params(("arbitrary", "arbitrary")),
        name="lru_bwd" if reverse else "lru_fwd",
    )(proj, proj, proj, conv_w, conv_b.reshape(1, d_lru), wa_bd, ba.reshape(2, 1, d_lru),
      wx_bd, bx.reshape(2, 1, d_lru), lam.reshape(2, 1, d_lru))


def _hgrn_kernel(q_ref, f_ref, v_ref, lb_ref, o_ref, st_ref, *, rows):
    ck, sb = HGRN_CHUNK, HGRN_SUB
    n_blk = ck // sb
    sb_shift = sb.bit_length() - 1
    ck_shift = ck.bit_length() - 1
    rev = pl.program_id(0) == 1
    n_sub = rows // ck
    width = q_ref.shape[-1]
    n_pairs = width // LANES
    half = LANES // 2

    @pl.when(pl.program_id(2) == 0)
    def _():
        st_ref[...] = jnp.zeros_like(st_ref)

    def flip(idx, n):
        return jnp.where(rev, n - 1 - idx, idx)

    tf = flip(lax.broadcasted_iota(jnp.int32, (ck, ck), 0), ck)
    uf = flip(lax.broadcasted_iota(jnp.int32, (ck, ck), 1), ck)
    tb, ub = tf >> sb_shift, uf >> sb_shift
    m_loc = jnp.where((tb == ub) & (uf <= tf), 1.0, 0.0).astype(BF16)
    m_ref = jnp.where(ub < tb, 1.0, 0.0).astype(BF16)
    pi = lax.broadcasted_iota(jnp.int32, (SUBLANES, ck), 0)
    pu = flip(lax.broadcasted_iota(jnp.int32, (SUBLANES, ck), 1), ck)
    m_blk = jnp.where((pu >> sb_shift) < pi, 1.0, 0.0).astype(BF16)
    m_all = jnp.ones((SUBLANES, ck), BF16)
    row_blk = flip(lax.broadcasted_iota(jnp.int32, (ck, width), 0), ck) >> sb_shift
    pr = lax.broadcasted_iota(jnp.int32, (2 * ck, (n_blk - 1) * ck), 0)
    pc = lax.broadcasted_iota(jnp.int32, (2 * ck, (n_blk - 1) * ck), 1)
    score_mask = (flip(pr & (ck - 1), ck) >> sb_shift) == ((pc >> ck_shift) + 1)
    lane = lax.broadcasted_iota(jnp.int32, (1, LANES), 1)
    head0 = lane < half
    sr = lax.broadcasted_iota(jnp.int32, (LANES, LANES), 0)
    sc = lax.broadcasted_iota(jnp.int32, (LANES, LANES), 1)
    same_head = (sr < half) == (sc < half)
    head_sum = jnp.where(same_head, 1.0, 0.0).astype(BF16)
    sub_row = flip(lax.broadcasted_iota(jnp.int32, (sb, LANES), 0), sb)
    lbv = lb_ref[...]

    def chunk_body(j, carry):
        jj = jnp.where(rev, n_sub - 1 - j, j)
        r0 = pl.multiple_of(jj * ck, ck)
        q = q_ref[0, pl.ds(r0, ck), :]
        z = f_ref[0, pl.ds(r0, ck), :]
        v = v_ref[0, pl.ds(r0, ck), :]
        f = lbv + (1.0 - lbv) * _sigmoid(z)
        lf = jnp.log(f)
        k = 1.0 - f
        bl = _dot01_exact(m_loc, lf)
        bref = _dot01_exact(m_ref, lf)
        b = bl + bref
        bmat = _dot01_exact(m_blk, lf)
        btot = _dot01_exact(m_all, lf)[0:1]
        qe = q * jnp.exp(b)
        ke = k * jnp.exp(btot - b)
        qb = q * jnp.exp(bl)
        k_parts = []
        for i in range(1, n_blk):
            k_i = k * jnp.exp(jnp.minimum(bmat[i:i + 1] - b, 0.0))
            k_parts.append(jnp.where(row_blk < i, k_i, 0.0))
        k_stack = jnp.concatenate(k_parts, axis=0)

        for p in range(n_pairs):
            sl = slice(p * LANES, (p + 1) * LANES)
            st = st_ref[p]
            inter = _dot_nt(qe[:, sl].astype(BF16), st.astype(BF16))
            qb_p = qb[:, sl]
            q_heads = jnp.concatenate([jnp.where(head0, qb_p, 0.0), jnp.where(head0, 0.0, qb_p)], axis=0)
            scores = _dot_nt(q_heads.astype(BF16), k_stack[:, sl].astype(BF16))
            scores = jnp.where(score_mask, scores, 0.0)
            v_p = v[:, sl]
            v_stack = jnp.concatenate([v_p] * (n_blk - 1), axis=0)
            r_off = _dot(scores.astype(BF16), v_stack.astype(BF16))
            o_off = jnp.where(head0, r_off[:ck], r_off[ck:])

            diag_parts = []
            for jb in range(n_blk):
                rs = slice(jb * sb, (jb + 1) * sb)
                bl_b, q_b, k_b, v_b = bl[rs, sl], q[rs, sl], k[rs, sl], v_p[rs]
                terms = []
                for s in range(sb):
                    s_f = jnp.where(rev, sb - 1 - s, s)
                    dec = jnp.exp(jnp.minimum(bl_b - bl_b[s:s + 1], 0.0))
                    terms.append(jnp.where(sub_row >= s_f, q_b * k_b[s:s + 1] * dec, 0.0))
                w = _dot(jnp.concatenate(terms, axis=0).astype(BF16), head_sum)
                od = w[0:sb] * v_b[0:1]
                for s in range(1, sb):
                    od = od + w[s * sb:(s + 1) * sb] * v_b[s:s + 1]
                diag_parts.append(od)
            o_diag = jnp.concatenate(diag_parts, axis=0)

            o_ref[0, 0, pl.ds(r0, ck), sl] = inter + o_off + o_diag
            new_st = st * jnp.exp(btot[:, sl]) + _dot_tn(v_p.astype(BF16), ke[:, sl].astype(BF16))
            st_ref[p] = jnp.where(same_head, new_st, 0.0)
        return carry

    lax.fori_loop(0, n_sub, chunk_body, 0)


def _hgrn(proj, lb, *, d_lru, d_hgrn, rows=256):
    bsz, seq, _ = proj.shape
    n_chunks = seq // rows
    col0 = (2 * d_lru) // d_hgrn

    def chunk_of(d, c):
        return jnp.where(d == 1, n_chunks - 1 - c, c)

    kern = functools.partial(_hgrn_kernel, rows=rows)
    return pl.pallas_call(
        kern,
        out_shape=jax.ShapeDtypeStruct((2, bsz, seq, d_hgrn), F32),
        grid=(2, bsz, n_chunks),
        in_specs=[
            pl.BlockSpec((1, rows, d_hgrn), lambda d, b, c: (b, chunk_of(d, c), col0)),
            pl.BlockSpec((1, rows, d_hgrn), lambda d, b, c: (b, chunk_of(d, c), col0 + 1 + d)),
            pl.BlockSpec((1, rows, d_hgrn), lambda d, b, c: (b, chunk_of(d, c), col0 + 3)),
            pl.BlockSpec((1, d_hgrn), lambda d, b, c: (0, 0)),
        ],
        out_specs=pl.BlockSpec((1, 1, rows, d_hgrn), lambda d, b, c: (d, b, chunk_of(d, c), 0)),
        scratch_shapes=[pltpu.VMEM((d_hgrn // LANES, LANES, LANES), F32)],
        compiler_params=_params(("arbitrary", "arbitrary", "arbitrary")),
        name="hgrn2",
    )(proj, proj, proj, lb.reshape(1, d_hgrn))


def _gelu_tanh(y):
    return 0.5 * y * (1.0 + jnp.tanh(0.7978845608028654 * (y + 0.044715 * (y * y * y))))


def _post_kernel(lf_ref, lb_ref, y_ref, of_ref, ob_ref, g_ref, x_ref, nlw_ref, nhw_ref, wo_ref,
                 gm_ref, nfw_ref, scf_ref, shf_ref, wr_ref, br_ref,
                 xo_ref, h_ref, slab_ref, cnt_ref, carry_ref, *, tm, d_lru):
    first = (pl.program_id(0) == 0) & (pl.program_id(1) == 0)

    @pl.when(first)
    def _():
        carry_ref[...] = jnp.zeros_like(carry_ref)

    lru = (lf_ref[0] + lb_ref[0]) * _gelu_tanh(y_ref[0])
    ms = jnp.mean(lru * lru, axis=-1, keepdims=True)
    lru = lru * lax.rsqrt(ms + NORM_EPS) * nlw_ref[...]

    hg = of_ref[0, 0] + ob_ref[0, 0]
    width = hg.shape[1]
    hd = width // HGRN_HEADS
    hd_shift = hd.bit_length() - 1
    er = lax.broadcasted_iota(jnp.int32, (width, width), 0) >> hd_shift
    ec = lax.broadcasted_iota(jnp.int32, (width, width), 1) >> hd_shift
    head_sum = jnp.where(er == ec, 1.0, 0.0).astype(BF16)
    sq = hg * hg
    sq_hi = sq.astype(BF16)
    sq_lo = (sq - sq_hi.astype(F32)).astype(BF16)
    ms_h = (_dot(sq_hi, head_sum) + _dot(sq_lo, head_sum)) * (1.0 / hd)
    g = g_ref[0]
    hg = (hg * lax.rsqrt(ms_h + NORM_EPS) * nhw_ref[...]) * (g * _sigmoid(g))

    mixed = _dot(lru.astype(BF16), wo_ref[0:d_lru, :]) + _dot(hg.astype(BF16), wo_ref[d_lru:, :])
    x_new = x_ref[0] + gm_ref[0] * mixed
    xo_ref[0] = x_new

    h = _rms_mod(x_new, nfw_ref[...], scf_ref[0], shf_ref[0])
    h_ref[0] = h

    logits = _dot(h.astype(BF16), wr_ref[...]) + br_ref[...]
    lane = lax.broadcasted_iota(jnp.int32, (tm, ROUTE_LANES), 1)
    lane_f = lane.astype(F32)
    far = float(ROUTE_LANES)
    is_g = lane < N_GROUPS
    gl = jnp.where(is_g, logits, NEG_BIG)
    gmax = jnp.max(gl, axis=-1, keepdims=True)
    g_idx = jnp.min(jnp.where(gl == gmax, lane_f, far), axis=-1, keepdims=True)
    p_group = 1.0 / jnp.sum(jnp.where(is_g, jnp.exp(gl - gmax), 0.0), axis=-1, keepdims=True)
    e_lane = lane - N_GROUPS
    in_group = (e_lane >= 0) & (e_lane < N_EXPERTS) & ((e_lane >> (EXPERTS_PER_GROUP.bit_length() - 1)).astype(F32) == g_idx)
    ev = jnp.where(in_group, logits, NEG_BIG)
    top1 = jnp.max(ev, axis=-1, keepdims=True)
    i1 = jnp.min(jnp.where(in_group & (ev == top1), lane_f, far), axis=-1, keepdims=True)
    rest = in_group & (lane_f != i1)
    ev2 = jnp.where(rest, logits, NEG_BIG)
    top2 = jnp.max(ev2, axis=-1, keepdims=True)
    i2 = jnp.min(jnp.where(rest & (ev2 == top2), lane_f, far), axis=-1, keepdims=True)
    e1 = i1 - float(N_GROUPS)
    e2 = i2 - float(N_GROUPS)
    ex = jnp.exp(top2 - top1)
    w1 = p_group / (1.0 + ex)
    w2 = p_group * ex / (1.0 + ex)

    sel1 = lane_f == e1
    sel2 = lane_f == e2
    onehot = jnp.where(sel1 | sel2, 1.0, 0.0)
    tr = lax.broadcasted_iota(jnp.int32, (tm, tm), 0)
    tc = lax.broadcasted_iota(jnp.int32, (tm, tm), 1)
    before = jnp.where(tc < tr, 1.0, 0.0).astype(BF16)
    cnt = _dot(before, onehot.astype(BF16)) + carry_ref[0:1]
    rank1 = jnp.sum(jnp.where(sel1, cnt, 0.0), axis=-1, keepdims=True)
    rank2 = jnp.sum(jnp.where(sel2, cnt, 0.0), axis=-1, keepdims=True)
    total = carry_ref[0:1] + jnp.sum(onehot, axis=0, keepdims=True)
    carry_ref[...] = jnp.broadcast_to(total, carry_ref.shape)
    cnt_ref[...] = jnp.broadcast_to(total, cnt_ref.shape)

    slab = jnp.where(lane == 0, e1, 0.0)
    slab = jnp.where(lane == 1, e2, slab)
    slab = jnp.where(lane == 2, w1, slab)
    slab = jnp.where(lane == 3, w2, slab)
    slab = jnp.where(lane == 4, rank1, slab)
    slab = jnp.where(lane == 5, rank2, slab)
    slab_ref[0] = slab


def _post_mixer(lru_f, lru_b, proj, hg, x, nlw, nhw, wo_bf16, g_mix, nfw, sc_ffn, sh_ffn, wr_bf16, br,
                *, tm=256):
    bsz, seq, d = x.shape
    d_lru = lru_f.shape[-1]
    d_hgrn = hg.shape[-1]
    y_col = 1
    g_col = (2 * d_lru) // d_hgrn + 4
    row = lambda w: pl.BlockSpec((1, tm, w), lambda b, i: (b, i, 0))
    vec = lambda w: pl.BlockSpec((1, w), lambda b, i: (0, 0))
    per_b = lambda: pl.BlockSpec((1, 1, d), lambda b, i: (b, 0, 0))
    kern = functools.partial(_post_kernel, tm=tm, d_lru=d_lru)
    return pl.pallas_call(
        kern,
        out_shape=(
            jax.ShapeDtypeStruct((bsz, seq, d), F32),
            jax.ShapeDtypeStruct((bsz, seq, d), F32),
            jax.ShapeDtypeStruct((bsz, seq, ROUTE_LANES), F32),
            jax.ShapeDtypeStruct((SUBLANES, ROUTE_LANES), F32),
        ),
        grid=(bsz, seq // tm),
        in_specs=[
            row(d_lru), row(d_lru),
            pl.BlockSpec((1, tm, d_lru), lambda b, i: (b, i, y_col)),
            pl.BlockSpec((1, 1, tm, d_hgrn), lambda b, i: (0, b, i, 0)),
            pl.BlockSpec((1, 1, tm, d_hgrn), lambda b, i: (1, b, i, 0)),
            pl.BlockSpec((1, tm, d_hgrn), lambda b, i: (b, i, g_col)),
            row(d), vec(d_lru), vec(d_hgrn),
            pl.BlockSpec((d, d), lambda b, i: (0, 0)),
            per_b(), vec(d), per_b(), per_b(),
            pl.BlockSpec((d, ROUTE_LANES), lambda b, i: (0, 0)),
            vec(ROUTE_LANES),
        ],
        out_specs=(
            row(d), row(d), row(ROUTE_LANES),
            pl.BlockSpec((SUBLANES, ROUTE_LANES), lambda b, i: (0, 0)),
        ),
        scratch_shapes=[pltpu.VMEM((SUBLANES, ROUTE_LANES), F32)],
        compiler_params=_params(("arbitrary", "arbitrary")),
        name="post_mixer_router",
    )(lru_f, lru_b, proj, hg, hg, proj, x, nlw.reshape(1, d_lru), nhw.reshape(1, d_hgrn), wo_bf16,
      g_mix, nfw.reshape(1, d), sc_ffn, sh_ffn, wr_bf16, br)


def _dispatch_kernel(d1_ref, d2_ref, h_ref, z_ref, o_ref, sem, *, tb):
    del z_ref
    base = pl.program_id(0) * tb

    def issue(r, carry):
        t = base + r
        pltpu.make_async_copy(h_ref.at[pl.ds(t, 1)], o_ref.at[pl.ds(d1_ref[t], 1)], sem).start()
        pltpu.make_async_copy(h_ref.at[pl.ds(t, 1)], o_ref.at[pl.ds(d2_ref[t], 1)], sem).start()
        return carry

    lax.fori_loop(0, tb, issue, 0)
    for _ in range(2):
        pltpu.make_async_copy(h_ref.at[pl.ds(0, tb)], o_ref.at[pl.ds(0, tb)], sem).wait()


def _dispatch(dest1, dest2, h_flat, n_rows, *, tb=512):
    m, d = h_flat.shape
    kern = functools.partial(_dispatch_kernel, tb=tb)
    return pl.pallas_call(
        kern,
        out_shape=jax.ShapeDtypeStruct((n_rows, d), h_flat.dtype),
        grid_spec=pltpu.PrefetchScalarGridSpec(
            num_scalar_prefetch=2,
            grid=(m // tb,),
            in_specs=[pl.BlockSpec(memory_space=pl.ANY), pl.BlockSpec(memory_space=pl.ANY)],
            out_specs=pl.BlockSpec(memory_space=pl.ANY),
            scratch_shapes=[pltpu.SemaphoreType.DMA(())],
        ),
        input_output_aliases={3: 0},
        compiler_params=pltpu.CompilerParams(dimension_semantics=("arbitrary",), has_side_effects=True),
        name="moe_dispatch",
    )(dest1, dest2, h_flat, jnp.zeros((n_rows, d), h_flat.dtype))


def _expert_kernel(be_ref, x_ref, wg_ref, wu_ref, wd_ref, o_ref):
    del be_ref
    x = x_ref[...].astype(BF16)
    gate = _dot(x, wg_ref[0])
    up = _dot(x, wu_ref[0])
    act = (gate * _sigmoid(gate)) * up
    o_ref[...] = _dot(act.astype(BF16), wd_ref[0])


def _experts(blk_expert, x_buf, wg, wu, wd):
    n_rows, d = x_buf.shape
    de = wg.shape[-1]
    blk = EXPERT_BLOCK
    return pl.pallas_call(
        _expert_kernel,
        out_shape=jax.ShapeDtypeStruct((n_rows, d), F32),
        grid_spec=pltpu.PrefetchScalarGridSpec(
            num_scalar_prefetch=1,
            grid=(n_rows // blk,),
            in_specs=[
                pl.BlockSpec((blk, d), lambda i, be: (i, 0)),
                pl.BlockSpec((1, d, de), lambda i, be: (be[i], 0, 0)),
                pl.BlockSpec((1, d, de), lambda i, be: (be[i], 0, 0)),
                pl.BlockSpec((1, de, d), lambda i, be: (be[i], 0, 0)),
            ],
            out_specs=pl.BlockSpec((blk, d), lambda i, be: (i, 0)),
        ),
        compiler_params=_params(("arbitrary",)),
        name="moe_experts",
    )(blk_expert, x_buf, wg, wu, wd)


def _combine_kernel(d1_ref, d2_ref, y_ref, slab_ref, x_ref, g_ref, o_ref, r1_ref, r2_ref, sem, *, tm, tiles):
    base = (pl.program_id(0) * tiles + pl.program_id(1)) * tm

    def issue(r, carry):
        t = base + r
        pltpu.make_async_copy(y_ref.at[pl.ds(d1_ref[t], 1)], r1_ref.at[pl.ds(r, 1)], sem).start()
        pltpu.make_async_copy(y_ref.at[pl.ds(d2_ref[t], 1)], r2_ref.at[pl.ds(r, 1)], sem).start()
        return carry

    lax.fori_loop(0, tm, issue, 0)
    pltpu.make_async_copy(y_ref.at[pl.ds(0, tm)], r1_ref, sem).wait()
    pltpu.make_async_copy(y_ref.at[pl.ds(0, tm)], r2_ref, sem).wait()
    slab = slab_ref[0]
    y = slab[:, 2:3] * r1_ref[...] + slab[:, 3:4] * r2_ref[...]
    o_ref[0] = x_ref[0] + g_ref[0] * y


def _combine(dest1, dest2, y_buf, slab, x, g_ffn, *, tm=256):
    bsz, seq, d = x.shape
    tiles = seq // tm
    kern = functools.partial(_combine_kernel, tm=tm, tiles=tiles)
    return pl.pallas_call(
        kern,
        out_shape=jax.ShapeDtypeStruct((bsz, seq, d), F32),
        grid_spec=pltpu.PrefetchScalarGridSpec(
            num_scalar_prefetch=2,
            grid=(bsz, tiles),
            in_specs=[
                pl.BlockSpec(memory_space=pl.ANY),
                pl.BlockSpec((1, tm, ROUTE_LANES), lambda b, i, d1, d2: (b, i, 0)),
                pl.BlockSpec((1, tm, d), lambda b, i, d1, d2: (b, i, 0)),
                pl.BlockSpec((1, 1, d), lambda b, i, d1, d2: (b, 0, 0)),
            ],
            out_specs=pl.BlockSpec((1, tm, d), lambda b, i, d1, d2: (b, i, 0)),
            scratch_shapes=[pltpu.VMEM((tm, d), F32), pltpu.VMEM((tm, d), F32), pltpu.SemaphoreType.DMA(())],
        ),
        compiler_params=_params(("arbitrary", "arbitrary")),
        name="moe_combine",
    )(dest1, dest2, y_buf, slab, x, g_ffn)


def _final_norm_kernel(x_ref, w_ref, o_ref):
    x = x_ref[0]
    ms = jnp.mean(x * x, axis=-1, keepdims=True)
    o_ref[0] = x * lax.rsqrt(ms + NORM_EPS) * w_ref[...]


def _final_norm(x, w, tm=512):
    bsz, seq, d = x.shape
    return pl.pallas_call(
        _final_norm_kernel,
        out_shape=jax.ShapeDtypeStruct((bsz, seq, d), F32),
        grid=(bsz, seq // tm),
        in_specs=[pl.BlockSpec((1, tm, d), lambda b, i: (b, i, 0)), pl.BlockSpec((1, d), lambda b, i: (0, 0))],
        out_specs=pl.BlockSpec((1, tm, d), lambda b, i: (b, i, 0)),
        compiler_params=_params(("arbitrary", "arbitrary")),
        name="final_norm",
    )(x, w.reshape(1, d))


def _block_diag(w):
    heads, hd, _ = w.shape
    eye = jnp.eye(heads, dtype=w.dtype)
    return (w[:, :, None, :] * eye[:, None, :, None]).reshape(heads * hd, heads * hd)


def kernel(x, c, ada_w, ada_b, norm_mix_w, w_in, conv_w, conv_b, lru_wa, lru_ba, lru_wx, lru_bx, lru_lambda, norm_lru_w, hgrn_lb, norm_hgrn_w, w_out, norm_ffn_w, router_group_w, router_group_b, router_expert_w, router_expert_b, expert_w_gate, expert_w_up, expert_w_down, final_norm_w):
    bsz, seq, d = x.shape
    depth = ada_w.shape[0]
    d_lru = conv_w.shape[-1]
    d_hgrn = hgrn_lb.shape[-1]
    m = bsz * seq
    n_rows = m * 2 + N_EXPERTS * EXPERT_BLOCK
    n_blocks = n_rows // EXPERT_BLOCK

    mod = _modulation(c, ada_w, ada_b)
    lb_cum = jnp.cumsum(jax.nn.softmax(hgrn_lb.astype(F32), axis=0), axis=0)
    lb_all = lb_cum - lb_cum[0:1]

    for l in range(depth):
        sh_mix, sc_mix, g_mix, sh_ffn, sc_ffn, g_ffn = [
            mod[l, :, i * d:(i + 1) * d].reshape(bsz, 1, d) for i in range(6)]
        proj = _in_proj(x, norm_mix_w[l], sc_mix, sh_mix, w_in[l].astype(BF16))
        wa_bd = jnp.stack([_block_diag(lru_wa[l, 0]), _block_diag(lru_wa[l, 1])]).astype(BF16)
        wx_bd = jnp.stack([_block_diag(lru_wx[l, 0]), _block_diag(lru_wx[l, 1])]).astype(BF16)
        lru = [
            _lru_scan(proj, conv_w[l], conv_b[l], wa_bd, lru_ba[l], wx_bd, lru_bx[l], lru_lambda[l],
                      reverse=rv)
            for rv in (False, True)]
        hg = _hgrn(proj, lb_all[l], d_lru=d_lru, d_hgrn=d_hgrn)

        wr = jnp.zeros((d, ROUTE_LANES), F32)
        wr = wr.at[:, :N_GROUPS].set(router_group_w[l]).at[:, N_GROUPS:N_GROUPS + N_EXPERTS].set(router_expert_w[l])
        br = jnp.zeros((1, ROUTE_LANES), F32)
        br = br.at[0, :N_GROUPS].set(router_group_b[l]).at[0, N_GROUPS:N_GROUPS + N_EXPERTS].set(router_expert_b[l])
        x_mid, h_ffn, slab, counts = _post_mixer(
            lru[0], lru[1], proj, hg, x, norm_lru_w[l], norm_hgrn_w[l], w_out[l].astype(BF16), g_mix,
            norm_ffn_w[l], sc_ffn, sh_ffn, wr.astype(BF16), br)

        cnt = counts[0, :N_EXPERTS].astype(jnp.int32)
        padded = ((cnt + EXPERT_BLOCK - 1) // EXPERT_BLOCK) * EXPERT_BLOCK
        pend = jnp.cumsum(padded)
        pstart = pend - padded
        blk_start = jnp.arange(n_blocks, dtype=jnp.int32) * EXPERT_BLOCK
        blk_expert = jnp.minimum(jnp.sum(pend[None, :] <= blk_start[:, None], axis=1), N_EXPERTS - 1)
        blk_expert = blk_expert.astype(jnp.int32)
        slab_flat = slab.reshape(m, ROUTE_LANES)
        e1 = slab_flat[:, 0].astype(jnp.int32)
        e2 = slab_flat[:, 1].astype(jnp.int32)
        dest1 = pstart[e1] + slab_flat[:, 4].astype(jnp.int32)
        dest2 = pstart[e2] + slab_flat[:, 5].astype(jnp.int32)

        x_buf = _dispatch(dest1, dest2, h_ffn.reshape(m, d), n_rows)
        y_buf = _experts(blk_expert, x_buf, expert_w_gate[l].astype(BF16), expert_w_up[l].astype(BF16),
                         expert_w_down[l].astype(BF16))
        x = _combine(dest1, dest2, y_buf, slab, x_mid, g_ffn)

    return _final_norm(x, final_norm_w)
```

```python
import functools

import jax
import jax.numpy as jnp
from jax import lax
from jax.experimental import pallas as pl
from jax.experimental.pallas import tpu as pltpu

F32 = jnp.float32
BF16 = jnp.bfloat16

LRU_HEADS = 8
HGRN_HEADS = 8
CONV_WIDTH = 4
LRU_C = 8.0
N_GROUPS = 4
EXPERTS_PER_GROUP = 8
N_EXPERTS = N_GROUPS * EXPERTS_PER_GROUP
NORM_EPS = 1e-6

LANES = 128
SUBLANES = 8
VMEM_LIMIT = 56 * 1024 * 1024

HGRN_CHUNK = 64
HGRN_SUB = 16
LOG2E = 1.4426950408889634
ROUTE_LANES = LANES
EXPERT_BLOCK = 256
NEG_BIG = -3.0e38


def _params(sem):
    return pltpu.CompilerParams(dimension_semantics=sem, vmem_limit_bytes=VMEM_LIMIT)


def _dot(a, b):
    return jnp.dot(a, b, preferred_element_type=F32)


def _dot_nt(a, b):
    return lax.dot_general(a, b, (((1,), (1,)), ((), ())), preferred_element_type=F32)


def _dot_tn(a, b):
    return lax.dot_general(a, b, (((0,), (0,)), ((), ())), preferred_element_type=F32)


def _dot01_exact(m01, x):
    hi = x.astype(BF16)
    r1 = x - hi.astype(F32)
    mid = r1.astype(BF16)
    lo = (r1 - mid.astype(F32)).astype(BF16)
    return _dot(m01, hi) + _dot(m01, mid) + _dot(m01, lo)


def _sigmoid(x):
    return 1.0 / (1.0 + jnp.exp(-x))


def _mod_kernel(c_ref, w_ref, b_ref, o_ref):
    c = c_ref[...]
    cond = c * _sigmoid(c)
    o_ref[0] = _dot(cond.astype(BF16), w_ref[0].astype(BF16)) + b_ref[0]


def _modulation(c, ada_w, ada_b):
    depth, d, n = ada_w.shape
    bsz = c.shape[0]
    rows = -(-bsz // SUBLANES) * SUBLANES
    c_pad = jnp.pad(c, ((0, rows - bsz), (0, 0)))
    tn = n // 6
    out = pl.pallas_call(
        _mod_kernel,
        out_shape=jax.ShapeDtypeStruct((depth, rows, n), F32),
        grid=(depth, n // tn),
        in_specs=[
            pl.BlockSpec((rows, d), lambda l, j: (0, 0)),
            pl.BlockSpec((1, d, tn), lambda l, j: (l, 0, j)),
            pl.BlockSpec((1, 1, tn), lambda l, j: (l, 0, j)),
        ],
        out_specs=pl.BlockSpec((1, rows, tn), lambda l, j: (l, 0, j)),
        compiler_params=_params(("arbitrary", "arbitrary")),
        name="adaln_mod",
    )(c_pad, ada_w, ada_b.reshape(depth, 1, n))
    return out[:, :bsz]


def _rms_mod(x, nw, sc, sh):
    ms = jnp.mean(x * x, axis=-1, keepdims=True)
    return (x * lax.rsqrt(ms + NORM_EPS) * nw) * (1.0 + sc) + sh


def _inproj_kernel(x_ref, nw_ref, sc_ref, sh_ref, w_ref, o_ref):
    h = _rms_mod(x_ref[0], nw_ref[...], sc_ref[0], sh_ref[0])
    o_ref[0] = _dot(h.astype(BF16), w_ref[...])


def _in_proj(x, nw, sc, sh, w_bf16, tm=512):
    bsz, seq, d = x.shape
    n = w_bf16.shape[1]
    return pl.pallas_call(
        _inproj_kernel,
        out_shape=jax.ShapeDtypeStruct((bsz, seq, n), F32),
        grid=(bsz, seq // tm),
        in_specs=[
            pl.BlockSpec((1, tm, d), lambda b, i: (b, i, 0)),
            pl.BlockSpec((1, d), lambda b, i: (0, 0)),
            pl.BlockSpec((1, 1, d), lambda b, i: (b, 0, 0)),
            pl.BlockSpec((1, 1, d), lambda b, i: (b, 0, 0)),
            pl.BlockSpec((d, n), lambda b, i: (0, 0)),
        ],
        out_specs=pl.BlockSpec((1, tm, n), lambda b, i: (b, i, 0)),
        compiler_params=_params(("arbitrary", "arbitrary")),
        name="in_proj",
    )(x, nw.reshape(1, d), sc, sh, w_bf16)


def _lru_kernel(x_ref, xp_ref, xn_ref, cw_ref, cb_ref, wa_ref, ba_ref, wx_ref, bx_ref, lam_ref,
                o_ref, carry_ref, *, reverse, n_chunks, rows):
    c = pl.program_id(1)
    chunk = (n_chunks - 1 - c) if reverse else c

    @pl.when(c == 0)
    def _():
        carry_ref[...] = jnp.zeros_like(carry_ref)

    x = x_ref[0]
    width = x.shape[1]
    row = lax.broadcasted_iota(jnp.int32, (rows, width), 0)
    has_prev = jnp.where(chunk > 0, 1.0, 0.0)
    has_next = jnp.where(chunk < n_chunks - 1, 1.0, 0.0)
    xp = xp_ref[0] * has_prev
    xn = xn_ref[0] * has_next
    p6, p7, n0 = xp[6:7], xp[7:8], xn[0:1]
    xm1 = jnp.where(row == 0, p7, pltpu.roll(x, 1, 0))
    xm2 = jnp.where(row == 0, p6, jnp.where(row == 1, p7, pltpu.roll(x, 2, 0)))
    xp1 = jnp.where(row == rows - 1, n0, pltpu.roll(x, rows - 1, 0))
    cw = cw_ref[...]
    xc = cw[0:1] * xm2 + cw[1:2] * xm1 + cw[2:3] * x + cw[3:4] * xp1 + cb_ref[...]

    xcb = xc.astype(BF16)
    r = _sigmoid(_dot(xcb, wa_ref[0]) + ba_ref[0])
    gate_i = _sigmoid(_dot(xcb, wx_ref[0]) + bx_ref[0])
    lam = lam_ref[0]
    softplus_neg_lam = jnp.maximum(-lam, 0.0) + jnp.log1p(jnp.exp(-jnp.abs(lam)))
    log_a = (-LRU_C) * r * softplus_neg_lam
    a = jnp.exp(log_a)
    t = jnp.tanh(-log_a)
    u = jnp.sqrt(2.0 * t / (1.0 + t)) * (gate_i * xc)

    acc_a, acc_b = a, u
    s = 1
    while s < rows:
        if reverse:
            valid = row < rows - s
            sh_a, sh_b = pltpu.roll(acc_a, rows - s, 0), pltpu.roll(acc_b, rows - s, 0)
        else:
            valid = row >= s
            sh_a, sh_b = pltpu.roll(acc_a, s, 0), pltpu.roll(acc_b, s, 0)
        acc_b = jnp.where(valid, acc_a * sh_b + acc_b, acc_b)
        acc_a = jnp.where(valid, acc_a * sh_a, acc_a)
        s *= 2
    h = acc_b + acc_a * carry_ref[...]
    o_ref[0] = h
    carry_ref[...] = h[0:1] if reverse else h[rows - 1:rows]


def _lru_scan(proj, conv_w, conv_b, wa_bd, ba, wx_bd, bx, lam, *, reverse, rows=256):
    bsz, seq, _ = proj.shape
    d_lru = conv_w.shape[1]
    n_chunks = seq // rows
    halo = rows // SUBLANES
    last_halo = seq // SUBLANES - 1
    dirn = 1 if reverse else 0

    def chunk_of(c):
        return (n_chunks - 1 - c) if reverse else c

    vec = lambda: pl.BlockSpec((1, 1, d_lru), lambda b, c: (dirn, 0, 0))
    mat = lambda: pl.BlockSpec((1, d_lru, d_lru), lambda b, c: (dirn, 0, 0))
    kern = functools.partial(_lru_kernel, reverse=reverse, n_chunks=n_chunks, rows=rows)
    return pl.pallas_call(
        kern,
        out_shape=jax.ShapeDtypeStruct((bsz, seq, d_lru), F32),
        grid=(bsz, n_chunks),
        in_specs=[
            pl.BlockSpec((1, rows, d_lru), lambda b, c: (b, chunk_of(c), 0)),
            pl.BlockSpec((1, SUBLANES, d_lru),
                         lambda b, c: (b, jnp.maximum(chunk_of(c) * halo - 1, 0), 0)),
            pl.BlockSpec((1, SUBLANES, d_lru),
                         lambda b, c: (b, jnp.minimum((chunk_of(c) + 1) * halo, last_halo), 0)),
            pl.BlockSpec((CONV_WIDTH, d_lru), lambda b, c: (0, 0)),
            pl.BlockSpec((1, d_lru), lambda b, c: (0, 0)),
            mat(), vec(), mat(), vec(), vec(),
        ],
        out_specs=pl.BlockSpec((1, rows, d_lru), lambda b, c: (b, chunk_of(c), 0)),
        scratch_shapes=[pltpu.VMEM((1, d_lru), F32)],
        compiler_params=_params(("arbitrary", "arbitrary")),
        name="lru_bwd" if reverse else "lru_fwd",
    )(proj, proj, proj, conv_w, conv_b.reshape(1, d_lru), wa_bd, ba.reshape(2, 1, d_lru),
      wx_bd, bx.reshape(2, 1, d_lru), lam.reshape(2, 1, d_lru))


def _hgrn_kernel(q_ref, f_ref, v_ref, lb_ref, o_ref, st_ref, *, rows):
    ck, sb = HGRN_CHUNK, HGRN_SUB
    n_blk = ck // sb
    sb_shift = sb.bit_length() - 1
    ck_shift = ck.bit_length() - 1
    rev = pl.program_id(0) == 1
    n_sub = rows // ck
    width = q_ref.shape[-1]
    n_pairs = width // LANES
    half = LANES // 2

    @pl.when(pl.program_id(2) == 0)
    def _():
        st_ref[...] = jnp.zeros_like(st_ref)

    def flip(idx, n):
        return jnp.where(rev, n - 1 - idx, idx)

    tf = flip(lax.broadcasted_iota(jnp.int32, (ck, ck), 0), ck)
    uf = flip(lax.broadcasted_iota(jnp.int32, (ck, ck), 1), ck)
    tb, ub = tf >> sb_shift, uf >> sb_shift
    m_loc = jnp.where((tb == ub) & (uf <= tf), 1.0, 0.0).astype(BF16)
    m_ref = jnp.where(ub < tb, 1.0, 0.0).astype(BF16)
    pi = lax.broadcasted_iota(jnp.int32, (SUBLANES, ck), 0)
    pu = flip(lax.broadcasted_iota(jnp.int32, (SUBLANES, ck), 1), ck)
    m_blk = jnp.where((pu >> sb_shift) < pi, 1.0, 0.0).astype(BF16)
    m_all = jnp.ones((SUBLANES, ck), BF16)
    m_cum = jnp.concatenate([m_loc, m_ref, m_blk, m_all], axis=0)
    row_blk = flip(lax.broadcasted_iota(jnp.int32, (ck, width), 0), ck) >> sb_shift
    pr = lax.broadcasted_iota(jnp.int32, (2 * ck, (n_blk - 1) * ck), 0)
    pc = lax.broadcasted_iota(jnp.int32, (2 * ck, (n_blk - 1) * ck), 1)
    score_mask = (flip(pr & (ck - 1), ck) >> sb_shift) == ((pc >> ck_shift) + 1)
    lane = lax.broadcasted_iota(jnp.int32, (1, LANES), 1)
    head0 = lane < half
    sr = lax.broadcasted_iota(jnp.int32, (LANES, LANES), 0)
    sc = lax.broadcasted_iota(jnp.int32, (LANES, LANES), 1)
    same_head = (sr < half) == (sc < half)
    head_sum = jnp.where(same_head, 1.0, 0.0).astype(BF16)
    sub_row = flip(lax.broadcasted_iota(jnp.int32, (sb, LANES), 0), sb)
    lbv = lb_ref[...]

    def chunk_body(j, carry):
        jj = jnp.where(rev, n_sub - 1 - j, j)
        r0 = pl.multiple_of(jj * ck, ck)
        q = q_ref[0, pl.ds(r0, ck), :]
        z = f_ref[0, pl.ds(r0, ck), :]
        v = v_ref[0, pl.ds(r0, ck), :]
        f = lbv + (1.0 - lbv) * _sigmoid(z)
        lf = jnp.log(f)
        k = 1.0 - f
        cums = _dot01_exact(m_cum, lf)
        bl = cums[0:ck]
        b = bl + cums[ck:2 * ck]
        bmat = cums[2 * ck:2 * ck + SUBLANES]
        btot = cums[2 * ck + SUBLANES:2 * ck + SUBLANES + 1]
        qe = q * jnp.exp(b)
        ke = k * jnp.exp(btot - b)
        qb = q * jnp.exp(bl)
        log2_k = jnp.log(k) * LOG2E
        kb = b * LOG2E - log2_k
        kbl = bl * LOG2E - log2_k
        bl2 = bl * LOG2E
        bmat2 = bmat * LOG2E
        k_parts = []
        for i in range(1, n_blk):
            k_parts.append(jnp.exp2(jnp.where(row_blk < i, bmat2[i:i + 1] - kb, NEG_BIG)))
        k_stack = jnp.concatenate(k_parts, axis=0)

        for p in range(n_pairs):
            sl = slice(p * LANES, (p + 1) * LANES)
            st = st_ref[p]
            inter = _dot_nt(qe[:, sl].astype(BF16), st.astype(BF16))
            qb_p = qb[:, sl]
            q_heads = jnp.concatenate([jnp.where(head0, qb_p, 0.0), jnp.where(head0, 0.0, qb_p)], axis=0)
            scores = _dot_nt(q_heads.astype(BF16), k_stack[:, sl].astype(BF16))
            scores = jnp.where(score_mask, scores, 0.0)
            v_p = v[:, sl]
            v_stack = jnp.concatenate([v_p] * (n_blk - 1), axis=0)
            r_off = _dot(scores.astype(BF16), v_stack.astype(BF16))
            o_off = jnp.where(head0, r_off[:ck], r_off[ck:])

            diag_parts = []
            for jb in range(n_blk):
                rs = slice(jb * sb, (jb + 1) * sb)
                bl_b, kbl_b, q_b, v_b = bl2[rs, sl], kbl[rs, sl], q[rs, sl], v_p[rs]
                terms = []
                for s in range(sb):
                    s_f = jnp.where(rev, sb - 1 - s, s)
                    arg = jnp.where(sub_row >= s_f, bl_b - kbl_b[s:s + 1], NEG_BIG)
                    terms.append(q_b * jnp.exp2(arg))
                w = _dot(jnp.concatenate(terms, axis=0).astype(BF16), head_sum)
                od = w[0:sb] * v_b[0:1]
                for s in range(1, sb):
                    od = od + w[s * sb:(s + 1) * sb] * v_b[s:s + 1]
                diag_parts.append(od)
            o_diag = jnp.concatenate(diag_parts, axis=0)

            o_ref[0, 0, pl.ds(r0, ck), sl] = inter + o_off + o_diag
            new_st = st * jnp.exp(btot[:, sl]) + _dot_tn(v_p.astype(BF16), ke[:, sl].astype(BF16))
            st_ref[p] = jnp.where(same_head, new_st, 0.0)
        return carry

    lax.fori_loop(0, n_sub, chunk_body, 0)


def _hgrn(proj, lb, *, d_lru, d_hgrn, rows=256):
    bsz, seq, _ = proj.shape
    n_chunks = seq // rows
    col0 = (2 * d_lru) // d_hgrn

    def chunk_of(d, c):
        return jnp.where(d == 1, n_chunks - 1 - c, c)

    kern = functools.partial(_hgrn_kernel, rows=rows)
    return pl.pallas_call(
        kern,
        out_shape=jax.ShapeDtypeStruct((2, bsz, seq, d_hgrn), F32),
        grid=(2, bsz, n_chunks),
        in_specs=[
            pl.BlockSpec((1, rows, d_hgrn), lambda d, b, c: (b, chunk_of(d, c), col0)),
            pl.BlockSpec((1, rows, d_hgrn), lambda d, b, c: (b, chunk_of(d, c), col0 + 1 + d)),
            pl.BlockSpec((1, rows, d_hgrn), lambda d, b, c: (b, chunk_of(d, c), col0 + 3)),
            pl.BlockSpec((1, d_hgrn), lambda d, b, c: (0, 0)),
        ],
        out_specs=pl.BlockSpec((1, 1, rows, d_hgrn), lambda d, b, c: (d, b, chunk_of(d, c), 0)),
        scratch_shapes=[pltpu.VMEM((d_hgrn // LANES, LANES, LANES), F32)],
        compiler_params=_params(("arbitrary", "arbitrary", "arbitrary")),
        name="hgrn2",
    )(proj, proj, proj, lb.reshape(1, d_hgrn))


def _gelu_tanh(y):
    return 0.5 * y * (1.0 + jnp.tanh(0.7978845608028654 * (y + 0.044715 * (y * y * y))))


def _post_kernel(lf_ref, lb_ref, y_ref, of_ref, ob_ref, g_ref, x_ref, nlw_ref, nhw_ref, wo_ref,
                 gm_ref, nfw_ref, scf_ref, shf_ref, wr_ref, br_ref,
                 xo_ref, h_ref, slab_ref, cnt_ref, carry_ref, *, tm, d_lru):
    first = (pl.program_id(0) == 0) & (pl.program_id(1) == 0)

    @pl.when(first)
    def _():
        carry_ref[...] = jnp.zeros_like(carry_ref)

    lru = (lf_ref[0] + lb_ref[0]) * _gelu_tanh(y_ref[0])
    ms = jnp.mean(lru * lru, axis=-1, keepdims=True)
    lru = lru * lax.rsqrt(ms + NORM_EPS) * nlw_ref[...]

    hg = of_ref[0, 0] + ob_ref[0, 0]
    width = hg.shape[1]
    hd = width // HGRN_HEADS
    hd_shift = hd.bit_length() - 1
    er = lax.broadcasted_iota(jnp.int32, (width, width), 0) >> hd_shift
    ec = lax.broadcasted_iota(jnp.int32, (width, width), 1) >> hd_shift
    head_sum = jnp.where(er == ec, 1.0, 0.0).astype(BF16)
    sq = hg * hg
    sq_hi = sq.astype(BF16)
    sq_lo = (sq - sq_hi.astype(F32)).astype(BF16)
    ms_h = (_dot(sq_hi, head_sum) + _dot(sq_lo, head_sum)) * (1.0 / hd)
    g = g_ref[0]
    hg = (hg * lax.rsqrt(ms_h + NORM_EPS) * nhw_ref[...]) * (g * _sigmoid(g))

    mixed = _dot(lru.astype(BF16), wo_ref[0:d_lru, :]) + _dot(hg.astype(BF16), wo_ref[d_lru:, :])
    x_new = x_ref[0] + gm_ref[0] * mixed
    xo_ref[0] = x_new

    h = _rms_mod(x_new, nfw_ref[...], scf_ref[0], shf_ref[0])
    h_ref[0] = h

    logits = _dot(h.astype(BF16), wr_ref[...]) + br_ref[...]
    lane = lax.broadcasted_iota(jnp.int32, (tm, ROUTE_LANES), 1)
    lane_f = lane.astype(F32)
    far = float(ROUTE_LANES)
    is_g = lane < N_GROUPS
    gl = jnp.where(is_g, logits, NEG_BIG)
    gmax = jnp.max(gl, axis=-1, keepdims=True)
    g_idx = jnp.min(jnp.where(gl == gmax, lane_f, far), axis=-1, keepdims=True)
    p_group = 1.0 / jnp.sum(jnp.where(is_g, jnp.exp(gl - gmax), 0.0), axis=-1, keepdims=True)
    e_lane = lane - N_GROUPS
    in_group = (e_lane >= 0) & (e_lane < N_EXPERTS) & ((e_lane >> (EXPERTS_PER_GROUP.bit_length() - 1)).astype(F32) == g_idx)
    ev = jnp.where(in_group, logits, NEG_BIG)
    top1 = jnp.max(ev, axis=-1, keepdims=True)
    i1 = jnp.min(jnp.where(in_group & (ev == top1), lane_f, far), axis=-1, keepdims=True)
    rest = in_group & (lane_f != i1)
    ev2 = jnp.where(rest, logits, NEG_BIG)
    top2 = jnp.max(ev2, axis=-1, keepdims=True)
    i2 = jnp.min(jnp.where(rest & (ev2 == top2), lane_f, far), axis=-1, keepdims=True)
    e1 = i1 - float(N_GROUPS)
    e2 = i2 - float(N_GROUPS)
    ex = jnp.exp(top2 - top1)
    w1 = p_group / (1.0 + ex)
    w2 = p_group * ex / (1.0 + ex)

    sel1 = lane_f == e1
    sel2 = lane_f == e2
    onehot = jnp.where(sel1 | sel2, 1.0, 0.0)
    tr = lax.broadcasted_iota(jnp.int32, (tm, tm), 0)
    tc = lax.broadcasted_iota(jnp.int32, (tm, tm), 1)
    before = jnp.where(tc < tr, 1.0, 0.0).astype(BF16)
    cnt = _dot(before, onehot.astype(BF16)) + carry_ref[0:1]
    rank1 = jnp.sum(jnp.where(sel1, cnt, 0.0), axis=-1, keepdims=True)
    rank2 = jnp.sum(jnp.where(sel2, cnt, 0.0), axis=-1, keepdims=True)
    total = carry_ref[0:1] + jnp.sum(onehot, axis=0, keepdims=True)
    carry_ref[...] = jnp.broadcast_to(total, carry_ref.shape)
    cnt_ref[...] = jnp.broadcast_to(total, cnt_ref.shape)

    slab = jnp.where(lane == 0, e1, 0.0)
    slab = jnp.where(lane == 1, e2, slab)
    slab = jnp.where(lane == 2, w1, slab)
    slab = jnp.where(lane == 3, w2, slab)
    slab = jnp.where(lane == 4, rank1, slab)
    slab = jnp.where(lane == 5, rank2, slab)
    slab_ref[0] = slab


def _post_mixer(lru_f, lru_b, proj, hg, x, nlw, nhw, wo_bf16, g_mix, nfw, sc_ffn, sh_ffn, wr_bf16, br,
                *, tm=256):
    bsz, seq, d = x.shape
    d_lru = lru_f.shape[-1]
    d_hgrn = hg.shape[-1]
    y_col = 1
    g_col = (2 * d_lru) // d_hgrn + 4
    row = lambda w: pl.BlockSpec((1, tm, w), lambda b, i: (b, i, 0))
    vec = lambda w: pl.BlockSpec((1, w), lambda b, i: (0, 0))
    per_b = lambda: pl.BlockSpec((1, 1, d), lambda b, i: (b, 0, 0))
    kern = functools.partial(_post_kernel, tm=tm, d_lru=d_lru)
    return pl.pallas_call(
        kern,
        out_shape=(
            jax.ShapeDtypeStruct((bsz, seq, d), F32),
            jax.ShapeDtypeStruct((bsz, seq, d), F32),
            jax.ShapeDtypeStruct((bsz, seq, ROUTE_LANES), F32),
            jax.ShapeDtypeStruct((SUBLANES, ROUTE_LANES), F32),
        ),
        grid=(bsz, seq // tm),
        in_specs=[
            row(d_lru), row(d_lru),
            pl.BlockSpec((1, tm, d_lru), lambda b, i: (b, i, y_col)),
            pl.BlockSpec((1, 1, tm, d_hgrn), lambda b, i: (0, b, i, 0)),
            pl.BlockSpec((1, 1, tm, d_hgrn), lambda b, i: (1, b, i, 0)),
            pl.BlockSpec((1, tm, d_hgrn), lambda b, i: (b, i, g_col)),
            row(d), vec(d_lru), vec(d_hgrn),
            pl.BlockSpec((d, d), lambda b, i: (0, 0)),
            per_b(), vec(d), per_b(), per_b(),
            pl.BlockSpec((d, ROUTE_LANES), lambda b, i: (0, 0)),
            vec(ROUTE_LANES),
        ],
        out_specs=(
            row(d), row(d), row(ROUTE_LANES),
            pl.BlockSpec((SUBLANES, ROUTE_LANES), lambda b, i: (0, 0)),
        ),
        scratch_shapes=[pltpu.VMEM((SUBLANES, ROUTE_LANES), F32)],
        compiler_params=_params(("arbitrary", "arbitrary")),
        name="post_mixer_router",
    )(lru_f, lru_b, proj, hg, hg, proj, x, nlw.reshape(1, d_lru), nhw.reshape(1, d_hgrn), wo_bf16,
      g_mix, nfw.reshape(1, d), sc_ffn, sh_ffn, wr_bf16, br)


def _dispatch_kernel(d1_ref, d2_ref, h_ref, z_ref, o_ref, sem, *, tb):
    del z_ref
    base = pl.program_id(0) * tb

    def issue(r, carry):
        t = base + r
        pltpu.make_async_copy(h_ref.at[pl.ds(r, 1)], o_ref.at[pl.ds(d1_ref[t], 1)], sem).start()
        pltpu.make_async_copy(h_ref.at[pl.ds(r, 1)], o_ref.at[pl.ds(d2_ref[t], 1)], sem).start()
        return carry

    lax.fori_loop(0, tb, issue, 0)
    for _ in range(2):
        pltpu.make_async_copy(h_ref, o_ref.at[pl.ds(0, tb)], sem).wait()


def _dispatch(dest1, dest2, h_flat, n_rows, *, tb=256):
    m, d = h_flat.shape
    kern = functools.partial(_dispatch_kernel, tb=tb)
    return pl.pallas_call(
        kern,
        out_shape=jax.ShapeDtypeStruct((n_rows, d), h_flat.dtype),
        grid_spec=pltpu.PrefetchScalarGridSpec(
            num_scalar_prefetch=2,
            grid=(m // tb,),
            in_specs=[pl.BlockSpec((tb, d), lambda i, d1, d2: (i, 0)), pl.BlockSpec(memory_space=pl.ANY)],
            out_specs=pl.BlockSpec(memory_space=pl.ANY),
            scratch_shapes=[pltpu.SemaphoreType.DMA(())],
        ),
        input_output_aliases={3: 0},
        compiler_params=pltpu.CompilerParams(dimension_semantics=("arbitrary",), has_side_effects=True),
        name="moe_dispatch",
    )(dest1, dest2, h_flat, jnp.zeros((n_rows, d), h_flat.dtype))


def _expert_kernel(be_ref, x_ref, wg_ref, wu_ref, wd_ref, o_ref, wg_s, wu_s, wd_s):
    i = pl.program_id(0)
    prev_expert = be_ref[jnp.maximum(i - 1, 0)]

    @pl.when((i == 0) | (be_ref[i] != prev_expert))
    def _():
        wg_s[...] = wg_ref[0, 0].astype(BF16)
        wu_s[...] = wu_ref[0, 0].astype(BF16)
        wd_s[...] = wd_ref[0, 0].astype(BF16)

    x = x_ref[...].astype(BF16)
    gate = _dot(x, wg_s[...])
    up = _dot(x, wu_s[...])
    act = (gate * _sigmoid(gate)) * up
    o_ref[...] = _dot(act.astype(BF16), wd_s[...])


def _experts(blk_expert, x_buf, wg, wu, wd, layer):
    n_rows, d = x_buf.shape
    de = wg.shape[-1]
    blk = EXPERT_BLOCK
    return pl.pallas_call(
        _expert_kernel,
        out_shape=jax.ShapeDtypeStruct((n_rows, d), F32),
        grid_spec=pltpu.PrefetchScalarGridSpec(
            num_scalar_prefetch=1,
            grid=(n_rows // blk,),
            in_specs=[
                pl.BlockSpec((blk, d), lambda i, be: (i, 0)),
                pl.BlockSpec((1, 1, d, de), lambda i, be: (layer, be[i], 0, 0)),
                pl.BlockSpec((1, 1, d, de), lambda i, be: (layer, be[i], 0, 0)),
                pl.BlockSpec((1, 1, de, d), lambda i, be: (layer, be[i], 0, 0)),
            ],
            out_specs=pl.BlockSpec((blk, d), lambda i, be: (i, 0)),
            scratch_shapes=[pltpu.VMEM((d, de), BF16), pltpu.VMEM((d, de), BF16), pltpu.VMEM((de, d), BF16)],
        ),
        compiler_params=_params(("arbitrary",)),
        name="moe_experts",
    )(blk_expert, x_buf, wg, wu, wd)


def _combine_kernel(d1_ref, d2_ref, y_ref, slab_ref, x_ref, g_ref, o_ref, r1_ref, r2_ref, sem, *, tm, tiles):
    base = (pl.program_id(0) * tiles + pl.program_id(1)) * tm

    def issue(r, carry):
        t = base + r
        pltpu.make_async_copy(y_ref.at[pl.ds(d1_ref[t], 1)], r1_ref.at[pl.ds(r, 1)], sem).start()
        pltpu.make_async_copy(y_ref.at[pl.ds(d2_ref[t], 1)], r2_ref.at[pl.ds(r, 1)], sem).start()
        return carry

    lax.fori_loop(0, tm, issue, 0)
    pltpu.make_async_copy(y_ref.at[pl.ds(0, tm)], r1_ref, sem).wait()
    pltpu.make_async_copy(y_ref.at[pl.ds(0, tm)], r2_ref, sem).wait()
    slab = slab_ref[0]
    y = slab[:, 2:3] * r1_ref[...] + slab[:, 3:4] * r2_ref[...]
    o_ref[0] = x_ref[0] + g_ref[0] * y


def _combine(dest1, dest2, y_buf, slab, x, g_ffn, *, tm=256):
    bsz, seq, d = x.shape
    tiles = seq // tm
    kern = functools.partial(_combine_kernel, tm=tm, tiles=tiles)
    return pl.pallas_call(
        kern,
        out_shape=jax.ShapeDtypeStruct((bsz, seq, d), F32),
        grid_spec=pltpu.PrefetchScalarGridSpec(
            num_scalar_prefetch=2,
            grid=(bsz, tiles),
            in_specs=[
                pl.BlockSpec(memory_space=pl.ANY),
                pl.BlockSpec((1, tm, ROUTE_LANES), lambda b, i, d1, d2: (b, i, 0)),
                pl.BlockSpec((1, tm, d), lambda b, i, d1, d2: (b, i, 0)),
                pl.BlockSpec((1, 1, d), lambda b, i, d1, d2: (b, 0, 0)),
            ],
            out_specs=pl.BlockSpec((1, tm, d), lambda b, i, d1, d2: (b, i, 0)),
            scratch_shapes=[pltpu.VMEM((tm, d), F32), pltpu.VMEM((tm, d), F32), pltpu.SemaphoreType.DMA(())],
        ),
        compiler_params=_params(("arbitrary", "arbitrary")),
        name="moe_combine",
    )(dest1, dest2, y_buf, slab, x, g_ffn)


def _final_norm_kernel(x_ref, w_ref, o_ref):
    x = x_ref[0]
    ms = jnp.mean(x * x, axis=-1, keepdims=True)
    o_ref[0] = x * lax.rsqrt(ms + NORM_EPS) * w_ref[...]


def _final_norm(x, w, tm=512):
    bsz, seq, d = x.shape
    return pl.pallas_call(
        _final_norm_kernel,
        out_shape=jax.ShapeDtypeStruct((bsz, seq, d), F32),
        grid=(bsz, seq // tm),
        in_specs=[pl.BlockSpec((1, tm, d), lambda b, i: (b, i, 0)), pl.BlockSpec((1, d), lambda b, i: (0, 0))],
        out_specs=pl.BlockSpec((1, tm, d), lambda b, i: (b, i, 0)),
        compiler_params=_params(("arbitrary", "arbitrary")),
        name="final_norm",
    )(x, w.reshape(1, d))


def _block_diag(w):
    heads, hd, _ = w.shape
    eye = jnp.eye(heads, dtype=w.dtype)
    return (w[:, :, None, :] * eye[:, None, :, None]).reshape(heads * hd, heads * hd)


def kernel(x, c, ada_w, ada_b, norm_mix_w, w_in, conv_w, conv_b, lru_wa, lru_ba, lru_wx, lru_bx, lru_lambda, norm_lru_w, hgrn_lb, norm_hgrn_w, w_out, norm_ffn_w, router_group_w, router_group_b, router_expert_w, router_expert_b, expert_w_gate, expert_w_up, expert_w_down, final_norm_w):
    bsz, seq, d = x.shape
    depth = ada_w.shape[0]
    d_lru = conv_w.shape[-1]
    d_hgrn = hgrn_lb.shape[-1]
    m = bsz * seq
    n_rows = m * 2 + N_EXPERTS * EXPERT_BLOCK
    n_blocks = n_rows // EXPERT_BLOCK

    mod = _modulation(c, ada_w, ada_b)
    lb_cum = jnp.cumsum(jax.nn.softmax(hgrn_lb.astype(F32), axis=0), axis=0)
    lb_all = lb_cum - lb_cum[0:1]

    for l in range(depth):
        sh_mix, sc_mix, g_mix, sh_ffn, sc_ffn, g_ffn = [
            mod[l, :, i * d:(i + 1) * d].reshape(bsz, 1, d) for i in range(6)]
        proj = _in_proj(x, norm_mix_w[l], sc_mix, sh_mix, w_in[l].astype(BF16))
        wa_bd = jnp.stack([_block_diag(lru_wa[l, 0]), _block_diag(lru_wa[l, 1])]).astype(BF16)
        wx_bd = jnp.stack([_block_diag(lru_wx[l, 0]), _block_diag(lru_wx[l, 1])]).astype(BF16)
        lru = [
            _lru_scan(proj, conv_w[l], conv_b[l], wa_bd, lru_ba[l], wx_bd, lru_bx[l], lru_lambda[l],
                      reverse=rv)
            for rv in (False, True)]
        hg = _hgrn(proj, lb_all[l], d_lru=d_lru, d_hgrn=d_hgrn)

        wr = jnp.zeros((d, ROUTE_LANES), F32)
        wr = wr.at[:, :N_GROUPS].set(router_group_w[l]).at[:, N_GROUPS:N_GROUPS + N_EXPERTS].set(router_expert_w[l])
        br = jnp.zeros((1, ROUTE_LANES), F32)
        br = br.at[0, :N_GROUPS].set(router_group_b[l]).at[0, N_GROUPS:N_GROUPS + N_EXPERTS].set(router_expert_b[l])
        x_mid, h_ffn, slab, counts = _post_mixer(
            lru[0], lru[1], proj, hg, x, norm_lru_w[l], norm_hgrn_w[l], w_out[l].astype(BF16), g_mix,
            norm_ffn_w[l], sc_ffn, sh_ffn, wr.astype(BF16), br)

        cnt = counts[0, :N_EXPERTS].astype(jnp.int32)
        padded = ((cnt + EXPERT_BLOCK - 1) // EXPERT_BLOCK) * EXPERT_BLOCK
        pend = jnp.cumsum(padded)
        pstart = pend - padded
        blk_start = jnp.arange(n_blocks, dtype=jnp.int32) * EXPERT_BLOCK
        blk_expert = jnp.minimum(jnp.sum(pend[None, :] <= blk_start[:, None], axis=1), N_EXPERTS - 1)
        blk_expert = blk_expert.astype(jnp.int32)
        slab_flat = slab.reshape(m, ROUTE_LANES)
        e1 = slab_flat[:, 0].astype(jnp.int32)
        e2 = slab_flat[:, 1].astype(jnp.int32)
        dest1 = pstart[e1] + slab_flat[:, 4].astype(jnp.int32)
        dest2 = pstart[e2] + slab_flat[:, 5].astype(jnp.int32)

        x_buf = _dispatch(dest1, dest2, h_ffn.reshape(m, d), n_rows)
        y_buf = _experts(blk_expert, x_buf, expert_w_gate, expert_w_up, expert_w_down, l)
        x = _combine(dest1, dest2, y_buf, slab, x_mid, g_ffn)

    return _final_norm(x, final_norm_w)
```

```python
import functools

import jax
import jax.numpy as jnp
from jax import lax
from jax.experimental import pallas as pl
from jax.experimental.pallas import tpu as pltpu

F32 = jnp.float32
BF16 = jnp.bfloat16

LRU_HEADS = 8
HGRN_HEADS = 8
CONV_WIDTH = 4
LRU_C = 8.0
N_GROUPS = 4
EXPERTS_PER_GROUP = 8
N_EXPERTS = N_GROUPS * EXPERTS_PER_GROUP
NORM_EPS = 1e-6

LANES = 128
SUBLANES = 8
VMEM_LIMIT = 56 * 1024 * 1024

HGRN_CHUNK = 64
HGRN_SUB = 8
LOG2E = 1.4426950408889634
ROUTE_LANES = LANES
EXPERT_BLOCK = 256
DMA_ISSUE_UNROLL = 8
NEG_BIG = -3.0e38


def _params(sem):
    return pltpu.CompilerParams(dimension_semantics=sem, vmem_limit_bytes=VMEM_LIMIT)


def _dot(a, b):
    return jnp.dot(a, b, preferred_element_type=F32)


def _dot_nt(a, b):
    return lax.dot_general(a, b, (((1,), (1,)), ((), ())), preferred_element_type=F32)


def _dot_tn(a, b):
    return lax.dot_general(a, b, (((0,), (0,)), ((), ())), preferred_element_type=F32)


def _dot01_exact(m01, x):
    hi = x.astype(BF16)
    r1 = x - hi.astype(F32)
    mid = r1.astype(BF16)
    lo = (r1 - mid.astype(F32)).astype(BF16)
    return _dot(m01, hi) + _dot(m01, mid) + _dot(m01, lo)


def _sigmoid(x):
    return 1.0 / (1.0 + jnp.exp(-x))


def _mod_kernel(c_ref, w_ref, b_ref, o_ref):
    c = c_ref[...]
    cond = c * _sigmoid(c)
    o_ref[0] = _dot(cond.astype(BF16), w_ref[0].astype(BF16)) + b_ref[0]


def _modulation(c, ada_w, ada_b):
    depth, d, n = ada_w.shape
    bsz = c.shape[0]
    rows = -(-bsz // SUBLANES) * SUBLANES
    c_pad = jnp.pad(c, ((0, rows - bsz), (0, 0)))
    tn = n // 6
    out = pl.pallas_call(
        _mod_kernel,
        out_shape=jax.ShapeDtypeStruct((depth, rows, n), F32),
        grid=(depth, n // tn),
        in_specs=[
            pl.BlockSpec((rows, d), lambda l, j: (0, 0)),
            pl.BlockSpec((1, d, tn), lambda l, j: (l, 0, j)),
            pl.BlockSpec((1, 1, tn), lambda l, j: (l, 0, j)),
        ],
        out_specs=pl.BlockSpec((1, rows, tn), lambda l, j: (l, 0, j)),
        compiler_params=_params(("arbitrary", "arbitrary")),
        name="adaln_mod",
    )(c_pad, ada_w, ada_b.reshape(depth, 1, n))
    return out[:, :bsz]


def _rms_mod(x, nw, sc, sh):
    ms = jnp.mean(x * x, axis=-1, keepdims=True)
    return (x * lax.rsqrt(ms + NORM_EPS) * nw) * (1.0 + sc) + sh


def _inproj_kernel(x_ref, nw_ref, sc_ref, sh_ref, w_ref, o_ref):
    h = _rms_mod(x_ref[0], nw_ref[...], sc_ref[0], sh_ref[0])
    o_ref[0] = _dot(h.astype(BF16), w_ref[...])


def _in_proj(x, nw, sc, sh, w_bf16, tm=512):
    bsz, seq, d = x.shape
    n = w_bf16.shape[1]
    return pl.pallas_call(
        _inproj_kernel,
        out_shape=jax.ShapeDtypeStruct((bsz, seq, n), F32),
        grid=(bsz, seq // tm),
        in_specs=[
            pl.BlockSpec((1, tm, d), lambda b, i: (b, i, 0)),
            pl.BlockSpec((1, d), lambda b, i: (0, 0)),
            pl.BlockSpec((1, 1, d), lambda b, i: (b, 0, 0)),
            pl.BlockSpec((1, 1, d), lambda b, i: (b, 0, 0)),
            pl.BlockSpec((d, n), lambda b, i: (0, 0)),
        ],
        out_specs=pl.BlockSpec((1, tm, n), lambda b, i: (b, i, 0)),
        compiler_params=_params(("arbitrary", "arbitrary")),
        name="in_proj",
    )(x, nw.reshape(1, d), sc, sh, w_bf16)


def _lru_kernel(x_ref, xp_ref, xn_ref, cw_ref, cb_ref, wa_ref, ba_ref, wx_ref, bx_ref, lam_ref,
                o_ref, carry_ref, *, reverse, n_chunks, rows):
    c = pl.program_id(1)
    chunk = (n_chunks - 1 - c) if reverse else c

    @pl.when(c == 0)
    def _():
        carry_ref[...] = jnp.zeros_like(carry_ref)

    x = x_ref[0]
    width = x.shape[1]
    row = lax.broadcasted_iota(jnp.int32, (rows, width), 0)
    has_prev = jnp.where(chunk > 0, 1.0, 0.0)
    has_next = jnp.where(chunk < n_chunks - 1, 1.0, 0.0)
    xp = xp_ref[0] * has_prev
    xn = xn_ref[0] * has_next
    p6, p7, n0 = xp[6:7], xp[7:8], xn[0:1]
    xm1 = jnp.where(row == 0, p7, pltpu.roll(x, 1, 0))
    xm2 = jnp.where(row == 0, p6, jnp.where(row == 1, p7, pltpu.roll(x, 2, 0)))
    xp1 = jnp.where(row == rows - 1, n0, pltpu.roll(x, rows - 1, 0))
    cw = cw_ref[...]
    xc = cw[0:1] * xm2 + cw[1:2] * xm1 + cw[2:3] * x + cw[3:4] * xp1 + cb_ref[...]

    xcb = xc.astype(BF16)
    r = _sigmoid(_dot(xcb, wa_ref[0]) + ba_ref[0])
    gate_i = _sigmoid(_dot(xcb, wx_ref[0]) + bx_ref[0])
    lam = lam_ref[0]
    softplus_neg_lam = jnp.maximum(-lam, 0.0) + jnp.log1p(jnp.exp(-jnp.abs(lam)))
    log_a = (-LRU_C) * r * softplus_neg_lam
    a = jnp.exp(log_a)
    t = jnp.tanh(-log_a)
    u = jnp.sqrt(2.0 * t / (1.0 + t)) * (gate_i * xc)

    acc_a, acc_b = a, u
    s = 1
    while s < rows:
        if reverse:
            valid = row < rows - s
            sh_a, sh_b = pltpu.roll(acc_a, rows - s, 0), pltpu.roll(acc_b, rows - s, 0)
        else:
            valid = row >= s
            sh_a, sh_b = pltpu.roll(acc_a, s, 0), pltpu.roll(acc_b, s, 0)
        acc_b = jnp.where(valid, acc_a * sh_b + acc_b, acc_b)
        acc_a = jnp.where(valid, acc_a * sh_a, acc_a)
        s *= 2
    h = acc_b + acc_a * carry_ref[...]
    o_ref[0] = h
    carry_ref[...] = h[0:1] if reverse else h[rows - 1:rows]


def _lru_scan(proj, conv_w, conv_b, wa_bd, ba, wx_bd, bx, lam, *, reverse, rows=256):
    bsz, seq, _ = proj.shape
    d_lru = conv_w.shape[1]
    n_chunks = seq // rows
    halo = rows // SUBLANES
    last_halo = seq // SUBLANES - 1
    dirn = 1 if reverse else 0

    def chunk_of(c):
        return (n_chunks - 1 - c) if reverse else c

    vec = lambda: pl.BlockSpec((1, 1, d_lru), lambda b, c: (dirn, 0, 0))
    mat = lambda: pl.BlockSpec((1, d_lru, d_lru), lambda b, c: (dirn, 0, 0))
    kern = functools.partial(_lru_kernel, reverse=reverse, n_chunks=n_chunks, rows=rows)
    return pl.pallas_call(
        kern,
        out_shape=jax.ShapeDtypeStruct((bsz, seq, d_lru), F32),
        grid=(bsz, n_chunks),
        in_specs=[
            pl.BlockSpec((1, rows, d_lru), lambda b, c: (b, chunk_of(c), 0)),
            pl.BlockSpec((1, SUBLANES, d_lru),
                         lambda b, c: (b, jnp.maximum(chunk_of(c) * halo - 1, 0), 0)),
            pl.BlockSpec((1, SUBLANES, d_lru),
                         lambda b, c: (b, jnp.minimum((chunk_of(c) + 1) * halo, last_halo), 0)),
            pl.BlockSpec((CONV_WIDTH, d_lru), lambda b, c: (0, 0)),
            pl.BlockSpec((1, d_lru), lambda b, c: (0, 0)),
            mat(), vec(), mat(), vec(), vec(),
        ],
        out_specs=pl.BlockSpec((1, rows, d_lru), lambda b, c: (b, chunk_of(c), 0)),
        scratch_shapes=[pltpu.VMEM((1, d_lru), F32)],
        compiler_params=_params(("arbitrary", "arbitrary")),
        name="lru_bwd" if reverse else "lru_fwd",
    )(proj, proj, proj, conv_w, conv_b.reshape(1, d_lru), wa_bd, ba.reshape(2, 1, d_lru),
      wx_bd, bx.reshape(2, 1, d_lru), lam.reshape(2, 1, d_lru))


def _hgrn_kernel(q_ref, f_ref, v_ref, lb_ref, o_ref, st_ref, *, rows):
    ck, sb = HGRN_CHUNK, HGRN_SUB
    n_blk = ck // sb
    sb_shift = sb.bit_length() - 1
    ck_shift = ck.bit_length() - 1
    rev = pl.program_id(0) == 1
    n_sub = rows // ck
    width = q_ref.shape[-1]
    n_pairs = width // LANES
    half = LANES // 2

    @pl.when(pl.program_id(2) == 0)
    def _():
        st_ref[...] = jnp.zeros_like(st_ref)

    def flip(idx, n):
        return jnp.where(rev, n - 1 - idx, idx)

    n_lvl = n_blk.bit_length() - 1
    tf = flip(lax.broadcasted_iota(jnp.int32, (ck, ck), 0), ck)
    uf = flip(lax.broadcasted_iota(jnp.int32, (ck, ck), 1), ck)
    tb, ub = tf >> sb_shift, uf >> sb_shift
    pb = flip(lax.broadcasted_iota(jnp.int32, (n_blk, ck), 0), n_blk)
    pub = flip(lax.broadcasted_iota(jnp.int32, (n_blk, ck), 1), ck) >> sb_shift
    mats = [jnp.where((tb == ub) & (uf <= tf), 1.0, 0.0),
            jnp.where(pub < pb, 1.0, 0.0)]
    for lvl in range(n_lvl):
        mid = ((pb >> (lvl + 1)) << (lvl + 1)) + (1 << lvl)
        mats.append(jnp.where(pub < mid, 1.0, 0.0))
    mats.append(jnp.ones((SUBLANES, ck), F32))
    m_cum = jnp.concatenate(mats, axis=0).astype(BF16)

    def per_block(rows8):
        return jnp.concatenate(
            [jnp.broadcast_to(rows8[jb:jb + 1], (sb, rows8.shape[1])) for jb in range(n_blk)], axis=0)
    row_blk = flip(lax.broadcasted_iota(jnp.int32, (ck, width), 0), ck) >> sb_shift
    upper = [((row_blk >> lvl) & 1) == 1 for lvl in range(n_lvl)]
    pr = flip(lax.broadcasted_iota(jnp.int32, (ck, LANES), 0), ck) >> sb_shift
    pc = flip(lax.broadcasted_iota(jnp.int32, (ck, LANES), 1) & (ck - 1), ck) >> sb_shift
    group_mask = [(pr >> (lvl + 1)) == (pc >> (lvl + 1)) for lvl in range(n_lvl)]
    lane = lax.broadcasted_iota(jnp.int32, (1, LANES), 1)
    head0 = lane < half
    sr = lax.broadcasted_iota(jnp.int32, (LANES, LANES), 0)
    sc = lax.broadcasted_iota(jnp.int32, (LANES, LANES), 1)
    same_head = (sr < half) == (sc < half)
    er = lax.broadcasted_iota(jnp.int32, (sb * LANES, LANES), 0)
    ec = lax.broadcasted_iota(jnp.int32, (sb * LANES, LANES), 1)
    sel = jnp.where(ec == (((er & (LANES - 1)) >> (half.bit_length() - 1)) * half + (er >> (LANES.bit_length() - 1))),
                    1.0, 0.0).astype(BF16)
    sub_row = flip(lax.broadcasted_iota(jnp.int32, (sb, LANES), 0), sb)
    lbv = lb_ref[...]

    def chunk_body(j, carry):
        jj = jnp.where(rev, n_sub - 1 - j, j)
        r0 = pl.multiple_of(jj * ck, ck)
        q = q_ref[0, pl.ds(r0, ck), :]
        z = f_ref[0, pl.ds(r0, ck), :]
        v = v_ref[0, pl.ds(r0, ck), :]
        f = lbv + (1.0 - lbv) * _sigmoid(z)
        lf = jnp.log(f)
        k = 1.0 - f
        cums = _dot01_exact(m_cum, lf)
        bl = cums[0:ck]
        b = bl + per_block(cums[ck:ck + n_blk])
        tot_row = ck + (1 + n_lvl) * n_blk
        btot = cums[tot_row:tot_row + 1]
        qe = q * jnp.exp(b)
        ke = k * jnp.exp(btot - b)
        log2_k = jnp.log(k) * LOG2E
        b2 = b * LOG2E
        kb = b2 - log2_k
        bl2 = bl * LOG2E
        kbl = bl2 - log2_k
        q_lvl, k_lvl = [], []
        for lvl in range(n_lvl):
            split2 = per_block(cums[ck + (1 + lvl) * n_blk:ck + (2 + lvl) * n_blk] * LOG2E)
            q_lvl.append(q * jnp.exp2(jnp.where(upper[lvl], b2 - split2, NEG_BIG)))
            k_lvl.append(jnp.exp2(jnp.where(upper[lvl], NEG_BIG, split2 - kb)))

        issued = []
        for p in range(n_pairs):
            sl = slice(p * LANES, (p + 1) * LANES)
            diag_rows = []
            for jb in range(n_blk):
                rs = slice(jb * sb, (jb + 1) * sb)
                bl_b, kbl_b, q_b = bl2[rs, sl], kbl[rs, sl], q[rs, sl]
                terms = []
                for s in range(sb):
                    s_f = jnp.where(rev, sb - 1 - s, s)
                    arg = jnp.where(sub_row >= s_f, bl_b - kbl_b[s:s + 1], NEG_BIG)
                    terms.append(q_b * jnp.exp2(arg))
                diag_rows.append(jnp.concatenate(terms, axis=1))
            diag = _dot(jnp.concatenate(diag_rows, axis=0).astype(BF16), sel)
            s_lvls = []
            for lvl in range(n_lvl):
                k_p = k_lvl[lvl][:, sl]
                k_heads = jnp.concatenate([jnp.where(head0, k_p, 0.0), jnp.where(head0, 0.0, k_p)], axis=0)
                s_lvls.append(_dot_nt(q_lvl[lvl][:, sl].astype(BF16), k_heads.astype(BF16)))
            st = st_ref[p]
            inter = _dot_nt(qe[:, sl].astype(BF16), st.astype(BF16))
            v_p = v[:, sl]
            upd = _dot_tn(v_p.astype(BF16), ke[:, sl].astype(BF16))
            issued.append((diag, s_lvls, st, inter, upd))

        for p in range(n_pairs):
            sl = slice(p * LANES, (p + 1) * LANES)
            diag, s_lvls, st, inter, upd = issued[p]
            parts = []
            for jb in range(n_blk):
                blk = diag[jb * sb:(jb + 1) * sb]
                parts.append(pltpu.roll(blk, jb * sb, 1) if jb else blk)
            scores = jnp.concatenate(parts, axis=0)
            for lvl in range(n_lvl):
                scores = scores + jnp.where(group_mask[lvl], s_lvls[lvl], 0.0)
            v_p = v[:, sl]
            v_heads = jnp.concatenate([jnp.where(head0, v_p, 0.0), jnp.where(head0, 0.0, v_p)], axis=0)
            intra = _dot(scores.astype(BF16), v_heads.astype(BF16))
            o_ref[0, 0, pl.ds(r0, ck), sl] = inter + intra
            st_ref[p] = jnp.where(same_head, st * jnp.exp(btot[:, sl]) + upd, 0.0)
        return carry

    lax.fori_loop(0, n_sub, chunk_body, 0)


def _hgrn(proj, lb, *, d_lru, d_hgrn, rows=256):
    bsz, seq, _ = proj.shape
    n_chunks = seq // rows
    col0 = (2 * d_lru) // d_hgrn

    def chunk_of(d, c):
        return jnp.where(d == 1, n_chunks - 1 - c, c)

    kern = functools.partial(_hgrn_kernel, rows=rows)
    return pl.pallas_call(
        kern,
        out_shape=jax.ShapeDtypeStruct((2, bsz, seq, d_hgrn), F32),
        grid=(2, bsz, n_chunks),
        in_specs=[
            pl.BlockSpec((1, rows, d_hgrn), lambda d, b, c: (b, chunk_of(d, c), col0)),
            pl.BlockSpec((1, rows, d_hgrn), lambda d, b, c: (b, chunk_of(d, c), col0 + 1 + d)),
            pl.BlockSpec((1, rows, d_hgrn), lambda d, b, c: (b, chunk_of(d, c), col0 + 3)),
            pl.BlockSpec((1, d_hgrn), lambda d, b, c: (0, 0)),
        ],
        out_specs=pl.BlockSpec((1, 1, rows, d_hgrn), lambda d, b, c: (d, b, chunk_of(d, c), 0)),
        scratch_shapes=[pltpu.VMEM((d_hgrn // LANES, LANES, LANES), F32)],
        compiler_params=_params(("arbitrary", "arbitrary", "arbitrary")),
        name="hgrn2",
    )(proj, proj, proj, lb.reshape(1, d_hgrn))


def _gelu_tanh(y):
    return 0.5 * y * (1.0 + jnp.tanh(0.7978845608028654 * (y + 0.044715 * (y * y * y))))


def _post_kernel(lf_ref, lb_ref, y_ref, of_ref, ob_ref, g_ref, x_ref, nlw_ref, nhw_ref, wo_ref,
                 gm_ref, nfw_ref, scf_ref, shf_ref, wr_ref, br_ref,
                 xo_ref, h_ref, slab_ref, cnt_ref, carry_ref, *, tm, d_lru):
    first = (pl.program_id(0) == 0) & (pl.program_id(1) == 0)

    @pl.when(first)
    def _():
        carry_ref[...] = jnp.zeros_like(carry_ref)

    lru = (lf_ref[0] + lb_ref[0]) * _gelu_tanh(y_ref[0])
    ms = jnp.mean(lru * lru, axis=-1, keepdims=True)
    lru = lru * lax.rsqrt(ms + NORM_EPS) * nlw_ref[...]

    hg = of_ref[0, 0] + ob_ref[0, 0]
    width = hg.shape[1]
    hd = width // HGRN_HEADS
    hd_shift = hd.bit_length() - 1
    er = lax.broadcasted_iota(jnp.int32, (width, width), 0) >> hd_shift
    ec = lax.broadcasted_iota(jnp.int32, (width, width), 1) >> hd_shift
    head_sum = jnp.where(er == ec, 1.0, 0.0).astype(BF16)
    sq = hg * hg
    sq_hi = sq.astype(BF16)
    sq_lo = (sq - sq_hi.astype(F32)).astype(BF16)
    ms_h = (_dot(sq_hi, head_sum) + _dot(sq_lo, head_sum)) * (1.0 / hd)
    g = g_ref[0]
    hg = (hg * lax.rsqrt(ms_h + NORM_EPS) * nhw_ref[...]) * (g * _sigmoid(g))

    mixed = _dot(lru.astype(BF16), wo_ref[0:d_lru, :]) + _dot(hg.astype(BF16), wo_ref[d_lru:, :])
    x_new = x_ref[0] + gm_ref[0] * mixed
    xo_ref[0] = x_new

    h = _rms_mod(x_new, nfw_ref[...], scf_ref[0], shf_ref[0])
    h_ref[0] = h

    logits = _dot(h.astype(BF16), wr_ref[...]) + br_ref[...]
    lane = lax.broadcasted_iota(jnp.int32, (tm, ROUTE_LANES), 1)
    lane_f = lane.astype(F32)
    far = float(ROUTE_LANES)
    is_g = lane < N_GROUPS
    gl = jnp.where(is_g, logits, NEG_BIG)
    gmax = jnp.max(gl, axis=-1, keepdims=True)
    g_idx = jnp.min(jnp.where(gl == gmax, lane_f, far), axis=-1, keepdims=True)
    p_group = 1.0 / jnp.sum(jnp.where(is_g, jnp.exp(gl - gmax), 0.0), axis=-1, keepdims=True)
    e_lane = lane - N_GROUPS
    in_group = (e_lane >= 0) & (e_lane < N_EXPERTS) & ((e_lane >> (EXPERTS_PER_GROUP.bit_length() - 1)).astype(F32) == g_idx)
    ev = jnp.where(in_group, logits, NEG_BIG)
    top1 = jnp.max(ev, axis=-1, keepdims=True)
    i1 = jnp.min(jnp.where(in_group & (ev == top1), lane_f, far), axis=-1, keepdims=True)
    rest = in_group & (lane_f != i1)
    ev2 = jnp.where(rest, logits, NEG_BIG)
    top2 = jnp.max(ev2, axis=-1, keepdims=True)
    i2 = jnp.min(jnp.where(rest & (ev2 == top2), lane_f, far), axis=-1, keepdims=True)
    e1 = i1 - float(N_GROUPS)
    e2 = i2 - float(N_GROUPS)
    ex = jnp.exp(top2 - top1)
    w1 = p_group / (1.0 + ex)
    w2 = p_group * ex / (1.0 + ex)

    sel1 = lane_f == e1
    sel2 = lane_f == e2
    onehot = jnp.where(sel1 | sel2, 1.0, 0.0)
    tr = lax.broadcasted_iota(jnp.int32, (tm, tm), 0)
    tc = lax.broadcasted_iota(jnp.int32, (tm, tm), 1)
    before = jnp.where(tc < tr, 1.0, 0.0).astype(BF16)
    cnt = _dot(before, onehot.astype(BF16)) + carry_ref[0:1]
    rank1 = jnp.sum(jnp.where(sel1, cnt, 0.0), axis=-1, keepdims=True)
    rank2 = jnp.sum(jnp.where(sel2, cnt, 0.0), axis=-1, keepdims=True)
    total = carry_ref[0:1] + jnp.sum(onehot, axis=0, keepdims=True)
    carry_ref[...] = jnp.broadcast_to(total, carry_ref.shape)
    cnt_ref[...] = jnp.broadcast_to(total, cnt_ref.shape)

    slab = jnp.where(lane == 0, e1, 0.0)
    slab = jnp.where(lane == 1, e2, slab)
    slab = jnp.where(lane == 2, w1, slab)
    slab = jnp.where(lane == 3, w2, slab)
    slab = jnp.where(lane == 4, rank1, slab)
    slab = jnp.where(lane == 5, rank2, slab)
    slab_ref[0] = slab


def _post_mixer(lru_f, lru_b, proj, hg, x, nlw, nhw, wo_bf16, g_mix, nfw, sc_ffn, sh_ffn, wr_bf16, br,
                *, tm=256):
    bsz, seq, d = x.shape
    d_lru = lru_f.shape[-1]
    d_hgrn = hg.shape[-1]
    y_col = 1
    g_col = (2 * d_lru) // d_hgrn + 4
    row = lambda w: pl.BlockSpec((1, tm, w), lambda b, i: (b, i, 0))
    vec = lambda w: pl.BlockSpec((1, w), lambda b, i: (0, 0))
    per_b = lambda: pl.BlockSpec((1, 1, d), lambda b, i: (b, 0, 0))
    kern = functools.partial(_post_kernel, tm=tm, d_lru=d_lru)
    return pl.pallas_call(
        kern,
        out_shape=(
            jax.ShapeDtypeStruct((bsz, seq, d), F32),
            jax.ShapeDtypeStruct((bsz, seq, d), F32),
            jax.ShapeDtypeStruct((bsz, seq, ROUTE_LANES), F32),
            jax.ShapeDtypeStruct((SUBLANES, ROUTE_LANES), F32),
        ),
        grid=(bsz, seq // tm),
        in_specs=[
            row(d_lru), row(d_lru),
            pl.BlockSpec((1, tm, d_lru), lambda b, i: (b, i, y_col)),
            pl.BlockSpec((1, 1, tm, d_hgrn), lambda b, i: (0, b, i, 0)),
            pl.BlockSpec((1, 1, tm, d_hgrn), lambda b, i: (1, b, i, 0)),
            pl.BlockSpec((1, tm, d_hgrn), lambda b, i: (b, i, g_col)),
            row(d), vec(d_lru), vec(d_hgrn),
            pl.BlockSpec((d, d), lambda b, i: (0, 0)),
            per_b(), vec(d), per_b(), per_b(),
            pl.BlockSpec((d, ROUTE_LANES), lambda b, i: (0, 0)),
            vec(ROUTE_LANES),
        ],
        out_specs=(
            row(d), row(d), row(ROUTE_LANES),
            pl.BlockSpec((SUBLANES, ROUTE_LANES), lambda b, i: (0, 0)),
        ),
        scratch_shapes=[pltpu.VMEM((SUBLANES, ROUTE_LANES), F32)],
        compiler_params=_params(("arbitrary", "arbitrary")),
        name="post_mixer_router",
    )(lru_f, lru_b, proj, hg, hg, proj, x, nlw.reshape(1, d_lru), nhw.reshape(1, d_hgrn), wo_bf16,
      g_mix, nfw.reshape(1, d), sc_ffn, sh_ffn, wr_bf16, br)


def _dispatch_kernel(d1_ref, d2_ref, h_ref, z_ref, o_ref, sem, *, tb):
    del z_ref
    base = pl.program_id(0) * tb

    def issue(r, carry):
        t = base + r
        pltpu.make_async_copy(h_ref.at[pl.ds(r, 1)], o_ref.at[pl.ds(d1_ref[t], 1)], sem).start()
        pltpu.make_async_copy(h_ref.at[pl.ds(r, 1)], o_ref.at[pl.ds(d2_ref[t], 1)], sem).start()
        return carry

    lax.fori_loop(0, tb, issue, 0, unroll=DMA_ISSUE_UNROLL)
    for _ in range(2):
        pltpu.make_async_copy(h_ref, o_ref.at[pl.ds(0, tb)], sem).wait()


def _dispatch(dest1, dest2, h_flat, n_rows, *, tb=256):
    m, d = h_flat.shape
    kern = functools.partial(_dispatch_kernel, tb=tb)
    return pl.pallas_call(
        kern,
        out_shape=jax.ShapeDtypeStruct((n_rows, d), h_flat.dtype),
        grid_spec=pltpu.PrefetchScalarGridSpec(
            num_scalar_prefetch=2,
            grid=(m // tb,),
            in_specs=[pl.BlockSpec((tb, d), lambda i, d1, d2: (i, 0)), pl.BlockSpec(memory_space=pl.ANY)],
            out_specs=pl.BlockSpec(memory_space=pl.ANY),
            scratch_shapes=[pltpu.SemaphoreType.DMA(())],
        ),
        input_output_aliases={3: 0},
        compiler_params=pltpu.CompilerParams(dimension_semantics=("arbitrary",), has_side_effects=True),
        name="moe_dispatch",
    )(dest1, dest2, h_flat, jnp.zeros((n_rows, d), h_flat.dtype))


def _expert_kernel(be_ref, x_ref, wg_ref, wu_ref, wd_ref, o_ref, wg_s, wu_s, wd_s):
    i = pl.program_id(0)
    prev_expert = be_ref[jnp.maximum(i - 1, 0)]

    @pl.when((i == 0) | (be_ref[i] != prev_expert))
    def _():
        wg_s[...] = wg_ref[0, 0].astype(BF16)
        wu_s[...] = wu_ref[0, 0].astype(BF16)
        wd_s[...] = wd_ref[0, 0].astype(BF16)

    x = x_ref[...].astype(BF16)
    gate = _dot(x, wg_s[...])
    up = _dot(x, wu_s[...])
    act = (gate * _sigmoid(gate)) * up
    o_ref[...] = _dot(act.astype(BF16), wd_s[...])


def _experts(blk_expert, x_buf, wg, wu, wd, layer):
    n_rows, d = x_buf.shape
    de = wg.shape[-1]
    blk = EXPERT_BLOCK
    return pl.pallas_call(
        _expert_kernel,
        out_shape=jax.ShapeDtypeStruct((n_rows, d), F32),
        grid_spec=pltpu.PrefetchScalarGridSpec(
            num_scalar_prefetch=1,
            grid=(n_rows // blk,),
            in_specs=[
                pl.BlockSpec((blk, d), lambda i, be: (i, 0)),
                pl.BlockSpec((1, 1, d, de), lambda i, be: (layer, be[i], 0, 0)),
                pl.BlockSpec((1, 1, d, de), lambda i, be: (layer, be[i], 0, 0)),
                pl.BlockSpec((1, 1, de, d), lambda i, be: (layer, be[i], 0, 0)),
            ],
            out_specs=pl.BlockSpec((blk, d), lambda i, be: (i, 0)),
            scratch_shapes=[pltpu.VMEM((d, de), BF16), pltpu.VMEM((d, de), BF16), pltpu.VMEM((de, d), BF16)],
        ),
        compiler_params=_params(("arbitrary",)),
        name="moe_experts",
    )(blk_expert, x_buf, wg, wu, wd)


def _combine_kernel(d1_ref, d2_ref, y_ref, slab_ref, x_ref, g_ref, nw_ref, o_ref, r1_ref, r2_ref, sem,
                    *, tm, tiles, final_norm):
    base = (pl.program_id(0) * tiles + pl.program_id(1)) * tm

    def issue(r, carry):
        t = base + r
        pltpu.make_async_copy(y_ref.at[pl.ds(d1_ref[t], 1)], r1_ref.at[pl.ds(r, 1)], sem).start()
        pltpu.make_async_copy(y_ref.at[pl.ds(d2_ref[t], 1)], r2_ref.at[pl.ds(r, 1)], sem).start()
        return carry

    lax.fori_loop(0, tm, issue, 0, unroll=DMA_ISSUE_UNROLL)
    pltpu.make_async_copy(y_ref.at[pl.ds(0, tm)], r1_ref, sem).wait()
    pltpu.make_async_copy(y_ref.at[pl.ds(0, tm)], r2_ref, sem).wait()
    slab = slab_ref[0]
    y = slab[:, 2:3] * r1_ref[...] + slab[:, 3:4] * r2_ref[...]
    out = x_ref[0] + g_ref[0] * y
    if final_norm:
        ms = jnp.mean(out * out, axis=-1, keepdims=True)
        out = out * lax.rsqrt(ms + NORM_EPS) * nw_ref[...]
    o_ref[0] = out


def _combine(dest1, dest2, y_buf, slab, x, g_ffn, norm_w, *, final_norm, tm=256):
    bsz, seq, d = x.shape
    tiles = seq // tm
    kern = functools.partial(_combine_kernel, tm=tm, tiles=tiles, final_norm=final_norm)
    return pl.pallas_call(
        kern,
        out_shape=jax.ShapeDtypeStruct((bsz, seq, d), F32),
        grid_spec=pltpu.PrefetchScalarGridSpec(
            num_scalar_prefetch=2,
            grid=(bsz, tiles),
            in_specs=[
                pl.BlockSpec(memory_space=pl.ANY),
                pl.BlockSpec((1, tm, ROUTE_LANES), lambda b, i, d1, d2: (b, i, 0)),
                pl.BlockSpec((1, tm, d), lambda b, i, d1, d2: (b, i, 0)),
                pl.BlockSpec((1, 1, d), lambda b, i, d1, d2: (b, 0, 0)),
                pl.BlockSpec((1, d), lambda b, i, d1, d2: (0, 0)),
            ],
            out_specs=pl.BlockSpec((1, tm, d), lambda b, i, d1, d2: (b, i, 0)),
            scratch_shapes=[pltpu.VMEM((tm, d), F32), pltpu.VMEM((tm, d), F32), pltpu.SemaphoreType.DMA(())],
        ),
        compiler_params=_params(("arbitrary", "arbitrary")),
        name="moe_combine",
    )(dest1, dest2, y_buf, slab, x, g_ffn, norm_w.reshape(1, d))


def _block_diag(w):
    heads, hd, _ = w.shape
    eye = jnp.eye(heads, dtype=w.dtype)
    return (w[:, :, None, :] * eye[:, None, :, None]).reshape(heads * hd, heads * hd)


def kernel(x, c, ada_w, ada_b, norm_mix_w, w_in, conv_w, conv_b, lru_wa, lru_ba, lru_wx, lru_bx, lru_lambda, norm_lru_w, hgrn_lb, norm_hgrn_w, w_out, norm_ffn_w, router_group_w, router_group_b, router_expert_w, router_expert_b, expert_w_gate, expert_w_up, expert_w_down, final_norm_w):
    bsz, seq, d = x.shape
    depth = ada_w.shape[0]
    d_lru = conv_w.shape[-1]
    d_hgrn = hgrn_lb.shape[-1]
    m = bsz * seq
    n_rows = m * 2 + N_EXPERTS * EXPERT_BLOCK
    n_blocks = n_rows // EXPERT_BLOCK

    mod = _modulation(c, ada_w, ada_b)
    lb_cum = jnp.cumsum(jax.nn.softmax(hgrn_lb.astype(F32), axis=0), axis=0)
    lb_all = lb_cum - lb_cum[0:1]

    for l in range(depth):
        sh_mix, sc_mix, g_mix, sh_ffn, sc_ffn, g_ffn = [
            mod[l, :, i * d:(i + 1) * d].reshape(bsz, 1, d) for i in range(6)]
        proj = _in_proj(x, norm_mix_w[l], sc_mix, sh_mix, w_in[l].astype(BF16))
        wa_bd = jnp.stack([_block_diag(lru_wa[l, 0]), _block_diag(lru_wa[l, 1])]).astype(BF16)
        wx_bd = jnp.stack([_block_diag(lru_wx[l, 0]), _block_diag(lru_wx[l, 1])]).astype(BF16)
        lru = [
            _lru_scan(proj, conv_w[l], conv_b[l], wa_bd, lru_ba[l], wx_bd, lru_bx[l], lru_lambda[l],
                      reverse=rv)
            for rv in (False, True)]
        hg = _hgrn(proj, lb_all[l], d_lru=d_lru, d_hgrn=d_hgrn)

        wr = jnp.zeros((d, ROUTE_LANES), F32)
        wr = wr.at[:, :N_GROUPS].set(router_group_w[l]).at[:, N_GROUPS:N_GROUPS + N_EXPERTS].set(router_expert_w[l])
        br = jnp.zeros((1, ROUTE_LANES), F32)
        br = br.at[0, :N_GROUPS].set(router_group_b[l]).at[0, N_GROUPS:N_GROUPS + N_EXPERTS].set(router_expert_b[l])
        x_mid, h_ffn, slab, counts = _post_mixer(
            lru[0], lru[1], proj, hg, x, norm_lru_w[l], norm_hgrn_w[l], w_out[l].astype(BF16), g_mix,
            norm_ffn_w[l], sc_ffn, sh_ffn, wr.astype(BF16), br)

        cnt = counts[0, :N_EXPERTS].astype(jnp.int32)
        padded = ((cnt + EXPERT_BLOCK - 1) // EXPERT_BLOCK) * EXPERT_BLOCK
        pend = jnp.cumsum(padded)
        pstart = pend - padded
        blk_start = jnp.arange(n_blocks, dtype=jnp.int32) * EXPERT_BLOCK
        blk_expert = jnp.minimum(jnp.sum(pend[None, :] <= blk_start[:, None], axis=1), N_EXPERTS - 1)
        blk_expert = blk_expert.astype(jnp.int32)
        slab_flat = slab.reshape(m, ROUTE_LANES)
        e1 = slab_flat[:, 0].astype(jnp.int32)
        e2 = slab_flat[:, 1].astype(jnp.int32)
        dest1 = pstart[e1] + slab_flat[:, 4].astype(jnp.int32)
        dest2 = pstart[e2] + slab_flat[:, 5].astype(jnp.int32)

        x_buf = _dispatch(dest1, dest2, h_ffn.reshape(m, d), n_rows)
        y_buf = _experts(blk_expert, x_buf, expert_w_gate, expert_w_up, expert_w_down, l)
        x = _combine(dest1, dest2, y_buf, slab, x_mid, g_ffn, final_norm_w, final_norm=(l == depth - 1))

    return x
```

```python
import functools

import jax
import jax.numpy as jnp
from jax import lax
from jax.experimental import pallas as pl
from jax.experimental.pallas import tpu as pltpu

F32 = jnp.float32
BF16 = jnp.bfloat16

LRU_HEADS = 8
HGRN_HEADS = 8
CONV_WIDTH = 4
LRU_C = 8.0
N_GROUPS = 4
EXPERTS_PER_GROUP = 8
N_EXPERTS = N_GROUPS * EXPERTS_PER_GROUP
NORM_EPS = 1e-6

LANES = 128
SUBLANES = 8
VMEM_LIMIT = 56 * 1024 * 1024

HGRN_CHUNK = 64
HGRN_SUB = 8
LOG2E = 1.4426950408889634
ROUTE_LANES = LANES
EXPERT_BLOCK = 256
DMA_ISSUE_UNROLL = 8
NEG_BIG = -3.0e38


def _params(sem):
    return pltpu.CompilerParams(dimension_semantics=sem, vmem_limit_bytes=VMEM_LIMIT)


def _dot(a, b):
    return jnp.dot(a, b, preferred_element_type=F32)


def _dot_nt(a, b):
    return lax.dot_general(a, b, (((1,), (1,)), ((), ())), preferred_element_type=F32)


def _dot_tn(a, b):
    return lax.dot_general(a, b, (((0,), (0,)), ((), ())), preferred_element_type=F32)


def _dot01_exact(m01, x):
    hi = x.astype(BF16)
    r1 = x - hi.astype(F32)
    mid = r1.astype(BF16)
    lo = (r1 - mid.astype(F32)).astype(BF16)
    return _dot(m01, hi) + _dot(m01, mid) + _dot(m01, lo)


def _sigmoid(x):
    return 1.0 / (1.0 + jnp.exp(-x))


def _mod_kernel(c_ref, w_ref, b_ref, o_ref):
    c = c_ref[...]
    cond = c * _sigmoid(c)
    o_ref[0] = _dot(cond.astype(BF16), w_ref[0].astype(BF16)) + b_ref[0]


def _modulation(c, ada_w, ada_b):
    depth, d, n = ada_w.shape
    bsz = c.shape[0]
    rows = -(-bsz // SUBLANES) * SUBLANES
    c_pad = jnp.pad(c, ((0, rows - bsz), (0, 0)))
    tn = n // 6
    out = pl.pallas_call(
        _mod_kernel,
        out_shape=jax.ShapeDtypeStruct((depth, rows, n), F32),
        grid=(depth, n // tn),
        in_specs=[
            pl.BlockSpec((rows, d), lambda l, j: (0, 0)),
            pl.BlockSpec((1, d, tn), lambda l, j: (l, 0, j)),
            pl.BlockSpec((1, 1, tn), lambda l, j: (l, 0, j)),
        ],
        out_specs=pl.BlockSpec((1, rows, tn), lambda l, j: (l, 0, j)),
        compiler_params=_params(("arbitrary", "arbitrary")),
        name="adaln_mod",
    )(c_pad, ada_w, ada_b.reshape(depth, 1, n))
    return out[:, :bsz]


def _rms_mod(x, nw, sc, sh):
    ms = jnp.mean(x * x, axis=-1, keepdims=True)
    return (x * lax.rsqrt(ms + NORM_EPS) * nw) * (1.0 + sc) + sh


def _inproj_kernel(x_ref, nw_ref, sc_ref, sh_ref, w_ref, o_ref):
    h = _rms_mod(x_ref[0], nw_ref[...], sc_ref[0], sh_ref[0])
    o_ref[0] = _dot(h.astype(BF16), w_ref[...])


def _in_proj(x, nw, sc, sh, w_bf16, tm=512):
    bsz, seq, d = x.shape
    n = w_bf16.shape[1]
    return pl.pallas_call(
        _inproj_kernel,
        out_shape=jax.ShapeDtypeStruct((bsz, seq, n), F32),
        grid=(bsz, seq // tm),
        in_specs=[
            pl.BlockSpec((1, tm, d), lambda b, i: (b, i, 0)),
            pl.BlockSpec((1, d), lambda b, i: (0, 0)),
            pl.BlockSpec((1, 1, d), lambda b, i: (b, 0, 0)),
            pl.BlockSpec((1, 1, d), lambda b, i: (b, 0, 0)),
            pl.BlockSpec((d, n), lambda b, i: (0, 0)),
        ],
        out_specs=pl.BlockSpec((1, tm, n), lambda b, i: (b, i, 0)),
        compiler_params=_params(("arbitrary", "arbitrary")),
        name="in_proj",
    )(x, nw.reshape(1, d), sc, sh, w_bf16)


def _lru_kernel(x_ref, xp_ref, xn_ref, cw_ref, cb_ref, wa_ref, ba_ref, wx_ref, bx_ref, lam_ref,
                o_ref, carry_ref, sa_ref, sb_ref, cin_ref, *, reverse, n_chunks, rows):
    c = pl.program_id(1)
    chunk = (n_chunks - 1 - c) if reverse else c

    @pl.when(c == 0)
    def _():
        carry_ref[...] = jnp.zeros_like(carry_ref)

    x = x_ref[0]
    width = x.shape[1]
    row = lax.broadcasted_iota(jnp.int32, (rows, width), 0)
    has_prev = jnp.where(chunk > 0, 1.0, 0.0)
    has_next = jnp.where(chunk < n_chunks - 1, 1.0, 0.0)
    xp = xp_ref[0] * has_prev
    xn = xn_ref[0] * has_next
    xe = jnp.concatenate([xp, x, xn], axis=0)
    h8 = SUBLANES
    cw = cw_ref[...]
    xc = (cw[0:1] * xe[h8 - 2:h8 - 2 + rows] + cw[1:2] * xe[h8 - 1:h8 - 1 + rows] + cw[2:3] * x
          + cw[3:4] * xe[h8 + 1:h8 + 1 + rows] + cb_ref[...])

    xcb = xc.astype(BF16)
    r = _sigmoid(_dot(xcb, wa_ref[0]) + ba_ref[0])
    gate_i = _sigmoid(_dot(xcb, wx_ref[0]) + bx_ref[0])
    lam = lam_ref[0]
    softplus_neg_lam = jnp.maximum(-lam, 0.0) + jnp.log1p(jnp.exp(-jnp.abs(lam)))
    log_a = (-LRU_C) * r * softplus_neg_lam
    a = jnp.exp(log_a)
    t = jnp.tanh(-log_a)
    u = jnp.sqrt(2.0 * t / (1.0 + t)) * (gate_i * xc)

    groups = rows // SUBLANES
    acc_a = a.reshape(groups, SUBLANES, width)
    acc_b = u.reshape(groups, SUBLANES, width)
    sub = lax.broadcasted_iota(jnp.int32, (groups, SUBLANES, width), 1)
    s = 1
    while s < SUBLANES:
        if reverse:
            valid = sub < SUBLANES - s
            sh_a, sh_b = pltpu.roll(acc_a, SUBLANES - s, 1), pltpu.roll(acc_b, SUBLANES - s, 1)
        else:
            valid = sub >= s
            sh_a, sh_b = pltpu.roll(acc_a, s, 1), pltpu.roll(acc_b, s, 1)
        acc_b = jnp.where(valid, acc_a * sh_b + acc_b, acc_b)
        acc_a = jnp.where(valid, acc_a * sh_a, acc_a)
        s *= 2
    acc_a = acc_a.reshape(rows, width)
    acc_b = acc_b.reshape(rows, width)
    groups = rows // SUBLANES
    edge = 0 if reverse else SUBLANES - 1
    n_tiles = width // LANES
    for j in range(n_tiles):
        sa_ref[j] = acc_a[:, j * LANES:(j + 1) * LANES]
        sb_ref[j] = acc_b[:, j * LANES:(j + 1) * LANES]
    ea = jnp.concatenate([sa_ref[j, pl.ds(edge, groups, stride=SUBLANES), :] for j in range(n_tiles)], axis=1)
    eb = jnp.concatenate([sb_ref[j, pl.ds(edge, groups, stride=SUBLANES), :] for j in range(n_tiles)], axis=1)
    grow = lax.broadcasted_iota(jnp.int32, (groups, width), 0)
    s = 1
    while s < groups:
        if reverse:
            valid = grow < groups - s
            sh_a, sh_b = pltpu.roll(ea, groups - s, 0), pltpu.roll(eb, groups - s, 0)
        else:
            valid = grow >= s
            sh_a, sh_b = pltpu.roll(ea, s, 0), pltpu.roll(eb, s, 0)
        eb = jnp.where(valid, ea * sh_b + eb, eb)
        ea = jnp.where(valid, ea * sh_a, ea)
        s *= 2
    carry = carry_ref[...]
    group_out = eb + ea * carry
    if reverse:
        carry_in = jnp.where(grow == groups - 1, carry, pltpu.roll(group_out, groups - 1, 0))
        carry_ref[...] = group_out[0:1]
    else:
        carry_in = jnp.where(grow == 0, carry, pltpu.roll(group_out, 1, 0))
        carry_ref[...] = group_out[groups - 1:groups]
    cin_ref[...] = carry_in
    for g in range(groups):
        rs = slice(g * SUBLANES, (g + 1) * SUBLANES)
        o_ref[0, rs, :] = acc_b[rs] + acc_a[rs] * cin_ref[g:g + 1, :]


def _lru_scan(proj, conv_w, conv_b, wa_bd, ba, wx_bd, bx, lam, *, reverse, rows=256):
    bsz, seq, _ = proj.shape
    d_lru = conv_w.shape[1]
    n_chunks = seq // rows
    halo = rows // SUBLANES
    last_halo = seq // SUBLANES - 1
    dirn = 1 if reverse else 0

    def chunk_of(c):
        return (n_chunks - 1 - c) if reverse else c

    vec = lambda: pl.BlockSpec((1, 1, d_lru), lambda b, c: (dirn, 0, 0))
    mat = lambda: pl.BlockSpec((1, d_lru, d_lru), lambda b, c: (dirn, 0, 0))
    kern = functools.partial(_lru_kernel, reverse=reverse, n_chunks=n_chunks, rows=rows)
    return pl.pallas_call(
        kern,
        out_shape=jax.ShapeDtypeStruct((bsz, seq, d_lru), F32),
        grid=(bsz, n_chunks),
        in_specs=[
            pl.BlockSpec((1, rows, d_lru), lambda b, c: (b, chunk_of(c), 0)),
            pl.BlockSpec((1, SUBLANES, d_lru),
                         lambda b, c: (b, jnp.maximum(chunk_of(c) * halo - 1, 0), 0)),
            pl.BlockSpec((1, SUBLANES, d_lru),
                         lambda b, c: (b, jnp.minimum((chunk_of(c) + 1) * halo, last_halo), 0)),
            pl.BlockSpec((CONV_WIDTH, d_lru), lambda b, c: (0, 0)),
            pl.BlockSpec((1, d_lru), lambda b, c: (0, 0)),
            mat(), vec(), mat(), vec(), vec(),
        ],
        out_specs=pl.BlockSpec((1, rows, d_lru), lambda b, c: (b, chunk_of(c), 0)),
        scratch_shapes=[pltpu.VMEM((1, d_lru), F32), pltpu.VMEM((d_lru // LANES, rows, LANES), F32),
                        pltpu.VMEM((d_lru // LANES, rows, LANES), F32),
                        pltpu.VMEM((rows // SUBLANES, d_lru), F32)],
        compiler_params=_params(("arbitrary", "arbitrary")),
        name="lru_bwd" if reverse else "lru_fwd",
    )(proj, proj, proj, conv_w, conv_b.reshape(1, d_lru), wa_bd, ba.reshape(2, 1, d_lru),
      wx_bd, bx.reshape(2, 1, d_lru), lam.reshape(2, 1, d_lru))


def _hgrn_kernel(q_ref, f_ref, v_ref, lb_ref, o_ref, st_ref, diag_s, lvl_s, upd_s, qe_s, btot_s, *, rows):
    ck, sb = HGRN_CHUNK, HGRN_SUB
    n_blk = ck // sb
    sb_shift = sb.bit_length() - 1
    ck_shift = ck.bit_length() - 1
    rev = pl.program_id(0) == 1
    n_sub = rows // ck
    width = q_ref.shape[-1]
    n_pairs = width // LANES
    half = LANES // 2

    @pl.when(pl.program_id(2) == 0)
    def _():
        st_ref[...] = jnp.zeros_like(st_ref)

    def flip(idx, n):
        return jnp.where(rev, n - 1 - idx, idx)

    n_lvl = n_blk.bit_length() - 1
    tf = flip(lax.broadcasted_iota(jnp.int32, (ck, ck), 0), ck)
    uf = flip(lax.broadcasted_iota(jnp.int32, (ck, ck), 1), ck)
    tb, ub = tf >> sb_shift, uf >> sb_shift
    pb = flip(lax.broadcasted_iota(jnp.int32, (n_blk, ck), 0), n_blk)
    pub = flip(lax.broadcasted_iota(jnp.int32, (n_blk, ck), 1), ck) >> sb_shift
    mats = [jnp.where((tb == ub) & (uf <= tf), 1.0, 0.0),
            jnp.where(pub < pb, 1.0, 0.0)]
    for lvl in range(n_lvl):
        mid = ((pb >> (lvl + 1)) << (lvl + 1)) + (1 << lvl)
        mats.append(jnp.where(pub < mid, 1.0, 0.0))
    mats.append(jnp.ones((SUBLANES, ck), F32))
    m_cum = jnp.concatenate(mats, axis=0).astype(BF16)

    def per_block(rows8):
        return jnp.concatenate(
            [jnp.broadcast_to(rows8[jb:jb + 1], (sb, rows8.shape[1])) for jb in range(n_blk)], axis=0)
    row_blk = flip(lax.broadcasted_iota(jnp.int32, (ck, width), 0), ck) >> sb_shift
    upper = [((row_blk >> lvl) & 1) == 1 for lvl in range(n_lvl)]
    pr = flip(lax.broadcasted_iota(jnp.int32, (ck, LANES), 0), ck) >> sb_shift
    pc = flip(lax.broadcasted_iota(jnp.int32, (ck, LANES), 1) & (ck - 1), ck) >> sb_shift
    group_mask = [(pr >> (lvl + 1)) == (pc >> (lvl + 1)) for lvl in range(n_lvl)]
    lane = lax.broadcasted_iota(jnp.int32, (1, LANES), 1)
    head0 = lane < half
    sr = lax.broadcasted_iota(jnp.int32, (LANES, LANES), 0)
    sc = lax.broadcasted_iota(jnp.int32, (LANES, LANES), 1)
    same_head = (sr < half) == (sc < half)
    er = lax.broadcasted_iota(jnp.int32, (sb * LANES, LANES), 0)
    ec = lax.broadcasted_iota(jnp.int32, (sb * LANES, LANES), 1)
    sel = jnp.where(ec == (((er & (LANES - 1)) >> (half.bit_length() - 1)) * half + (er >> (LANES.bit_length() - 1))),
                    1.0, 0.0).astype(BF16)
    sub_row = flip(lax.broadcasted_iota(jnp.int32, (sb, LANES), 0), sb)
    lbv = lb_ref[...]

    def row_start(j):
        return pl.multiple_of(jnp.where(rev, n_sub - 1 - j, j) * ck, ck)

    def stage1(j):
        r0 = row_start(j)
        q = q_ref[0, pl.ds(r0, ck), :]
        z = f_ref[0, pl.ds(r0, ck), :]
        v = v_ref[0, pl.ds(r0, ck), :]
        f = lbv + (1.0 - lbv) * _sigmoid(z)
        lf = jnp.log(f)
        k = 1.0 - f
        cums = _dot01_exact(m_cum, lf)
        bl = cums[0:ck]
        b = bl + per_block(cums[ck:ck + n_blk])
        tot_row = ck + (1 + n_lvl) * n_blk
        btot = cums[tot_row:tot_row + 1]
        qe = q * jnp.exp(b)
        ke = k * jnp.exp(btot - b)
        log2_k = jnp.log(k) * LOG2E
        b2 = b * LOG2E
        kb = b2 - log2_k
        bl2 = bl * LOG2E
        kbl = bl2 - log2_k
        q_lvl, k_lvl = [], []
        for lvl in range(n_lvl):
            split2 = per_block(cums[ck + (1 + lvl) * n_blk:ck + (2 + lvl) * n_blk] * LOG2E)
            q_lvl.append(q * jnp.exp2(jnp.where(upper[lvl], b2 - split2, NEG_BIG)))
            k_lvl.append(jnp.exp2(jnp.where(upper[lvl], NEG_BIG, split2 - kb)))

        qe_s[...] = qe.astype(BF16)
        btot_s[...] = btot
        for p in range(n_pairs):
            sl = slice(p * LANES, (p + 1) * LANES)
            diag_rows = []
            for jb in range(n_blk):
                rs = slice(jb * sb, (jb + 1) * sb)
                bl_b, kbl_b, q_b = bl2[rs, sl], kbl[rs, sl], q[rs, sl]
                terms = []
                for s in range(sb):
                    s_f = jnp.where(rev, sb - 1 - s, s)
                    arg = jnp.where(sub_row >= s_f, bl_b - kbl_b[s:s + 1], NEG_BIG)
                    terms.append(q_b * jnp.exp2(arg))
                diag_rows.append(jnp.concatenate(terms, axis=1))
            diag_s[p] = _dot(jnp.concatenate(diag_rows, axis=0).astype(BF16), sel)
            for lvl in range(n_lvl):
                k_p = k_lvl[lvl][:, sl]
                k_heads = jnp.concatenate([jnp.where(head0, k_p, 0.0), jnp.where(head0, 0.0, k_p)], axis=0)
                lvl_s[p * n_lvl + lvl] = _dot_nt(q_lvl[lvl][:, sl].astype(BF16), k_heads.astype(BF16))
            upd_s[p] = _dot_tn(v[:, sl].astype(BF16), ke[:, sl].astype(BF16))

    def stage2_issue(j):
        r0 = row_start(j)
        v = v_ref[0, pl.ds(r0, ck), :]
        out = []
        for p in range(n_pairs):
            sl = slice(p * LANES, (p + 1) * LANES)
            parts = []
            for jb in range(n_blk):
                blk = diag_s[p, jb * sb:(jb + 1) * sb, :]
                parts.append(pltpu.roll(blk, jb * sb, 1) if jb else blk)
            scores = jnp.concatenate(parts, axis=0)
            for lvl in range(n_lvl):
                scores = scores + jnp.where(group_mask[lvl], lvl_s[p * n_lvl + lvl], 0.0)
            v_p = v[:, sl]
            v_heads = jnp.concatenate([jnp.where(head0, v_p, 0.0), jnp.where(head0, 0.0, v_p)], axis=0)
            intra = _dot(scores.astype(BF16), v_heads.astype(BF16))
            st = st_ref[p]
            inter = _dot_nt(qe_s[:, sl], st.astype(BF16))
            new_st = jnp.where(same_head, st * jnp.exp(btot_s[:, sl]) + upd_s[p], 0.0)
            out.append((inter + intra, new_st))
        return r0, out

    def stage2_finish(r0, out):
        for p in range(n_pairs):
            o_ref[0, 0, pl.ds(r0, ck), p * LANES:(p + 1) * LANES] = out[p][0]
            st_ref[p] = out[p][1]

    stage1(0)

    def pipelined(j, carry):
        r0, out = stage2_issue(j)
        stage1(j + 1)
        stage2_finish(r0, out)
        return carry

    lax.fori_loop(0, n_sub - 1, pipelined, 0)
    stage2_finish(*stage2_issue(n_sub - 1))


def _hgrn(proj, lb, *, d_lru, d_hgrn, rows=512):
    bsz, seq, _ = proj.shape
    n_chunks = seq // rows
    col0 = (2 * d_lru) // d_hgrn
    n_pairs = d_hgrn // LANES
    n_lvl = (HGRN_CHUNK // HGRN_SUB).bit_length() - 1

    def chunk_of(d, c):
        return jnp.where(d == 1, n_chunks - 1 - c, c)

    kern = functools.partial(_hgrn_kernel, rows=rows)
    return pl.pallas_call(
        kern,
        out_shape=jax.ShapeDtypeStruct((2, bsz, seq, d_hgrn), F32),
        grid=(2, bsz, n_chunks),
        in_specs=[
            pl.BlockSpec((1, rows, d_hgrn), lambda d, b, c: (b, chunk_of(d, c), col0)),
            pl.BlockSpec((1, rows, d_hgrn), lambda d, b, c: (b, chunk_of(d, c), col0 + 1 + d)),
            pl.BlockSpec((1, rows, d_hgrn), lambda d, b, c: (b, chunk_of(d, c), col0 + 3)),
            pl.BlockSpec((1, d_hgrn), lambda d, b, c: (0, 0)),
        ],
        out_specs=pl.BlockSpec((1, 1, rows, d_hgrn), lambda d, b, c: (d, b, chunk_of(d, c), 0)),
        scratch_shapes=[
            pltpu.VMEM((n_pairs, LANES, LANES), F32),
            pltpu.VMEM((n_pairs, HGRN_CHUNK, LANES), F32),
            pltpu.VMEM((n_pairs * n_lvl, HGRN_CHUNK, LANES), F32),
            pltpu.VMEM((n_pairs, LANES, LANES), F32),
            pltpu.VMEM((HGRN_CHUNK, d_hgrn), BF16),
            pltpu.VMEM((1, d_hgrn), F32),
        ],
        compiler_params=_params(("arbitrary", "arbitrary", "arbitrary")),
        name="hgrn2",
    )(proj, proj, proj, lb.reshape(1, d_hgrn))


def _gelu_tanh(y):
    return 0.5 * y * (1.0 + jnp.tanh(0.7978845608028654 * (y + 0.044715 * (y * y * y))))


def _post_kernel(lf_ref, lb_ref, y_ref, of_ref, ob_ref, g_ref, x_ref, nlw_ref, nhw_ref, wo_ref,
                 gm_ref, nfw_ref, scf_ref, shf_ref, wr_ref, br_ref,
                 xo_ref, h_ref, slab_ref, cnt_ref, carry_ref, *, tm, d_lru):
    first = (pl.program_id(0) == 0) & (pl.program_id(1) == 0)

    @pl.when(first)
    def _():
        carry_ref[...] = jnp.zeros_like(carry_ref)

    lru = (lf_ref[0] + lb_ref[0]) * _gelu_tanh(y_ref[0])
    ms = jnp.mean(lru * lru, axis=-1, keepdims=True)
    lru = lru * lax.rsqrt(ms + NORM_EPS) * nlw_ref[...]

    hg = of_ref[0, 0] + ob_ref[0, 0]
    width = hg.shape[1]
    hd = width // HGRN_HEADS
    hd_shift = hd.bit_length() - 1
    er = lax.broadcasted_iota(jnp.int32, (width, width), 0) >> hd_shift
    ec = lax.broadcasted_iota(jnp.int32, (width, width), 1) >> hd_shift
    head_sum = jnp.where(er == ec, 1.0, 0.0).astype(BF16)
    sq = hg * hg
    sq_hi = sq.astype(BF16)
    sq_lo = (sq - sq_hi.astype(F32)).astype(BF16)
    ms_h = (_dot(sq_hi, head_sum) + _dot(sq_lo, head_sum)) * (1.0 / hd)
    g = g_ref[0]
    hg = (hg * lax.rsqrt(ms_h + NORM_EPS) * nhw_ref[...]) * (g * _sigmoid(g))

    mixed = _dot(lru.astype(BF16), wo_ref[0:d_lru, :]) + _dot(hg.astype(BF16), wo_ref[d_lru:, :])
    x_new = x_ref[0] + gm_ref[0] * mixed
    xo_ref[0] = x_new

    h = _rms_mod(x_new, nfw_ref[...], scf_ref[0], shf_ref[0])
    h_ref[0] = h

    logits = _dot(h.astype(BF16), wr_ref[...]) + br_ref[...]
    lane = lax.broadcasted_iota(jnp.int32, (tm, ROUTE_LANES), 1)
    lane_f = lane.astype(F32)
    far = float(ROUTE_LANES)
    is_g = lane < N_GROUPS
    gl = jnp.where(is_g, logits, NEG_BIG)
    gmax = jnp.max(gl, axis=-1, keepdims=True)
    g_idx = jnp.min(jnp.where(gl == gmax, lane_f, far), axis=-1, keepdims=True)
    p_group = 1.0 / jnp.sum(jnp.where(is_g, jnp.exp(gl - gmax), 0.0), axis=-1, keepdims=True)
    e_lane = lane - N_GROUPS
    in_group = (e_lane >= 0) & (e_lane < N_EXPERTS) & ((e_lane >> (EXPERTS_PER_GROUP.bit_length() - 1)).astype(F32) == g_idx)
    ev = jnp.where(in_group, logits, NEG_BIG)
    top1 = jnp.max(ev, axis=-1, keepdims=True)
    i1 = jnp.min(jnp.where(in_group & (ev == top1), lane_f, far), axis=-1, keepdims=True)
    rest = in_group & (lane_f != i1)
    ev2 = jnp.where(rest, logits, NEG_BIG)
    top2 = jnp.max(ev2, axis=-1, keepdims=True)
    i2 = jnp.min(jnp.where(rest & (ev2 == top2), lane_f, far), axis=-1, keepdims=True)
    e1 = i1 - float(N_GROUPS)
    e2 = i2 - float(N_GROUPS)
    ex = jnp.exp(top2 - top1)
    w1 = p_group / (1.0 + ex)
    w2 = p_group * ex / (1.0 + ex)

    sel1 = lane_f == e1
    sel2 = lane_f == e2
    onehot = jnp.where(sel1 | sel2, 1.0, 0.0)
    tr = lax.broadcasted_iota(jnp.int32, (tm, tm), 0)
    tc = lax.broadcasted_iota(jnp.int32, (tm, tm), 1)
    before = jnp.where(tc < tr, 1.0, 0.0).astype(BF16)
    cnt = _dot(before, onehot.astype(BF16)) + carry_ref[0:1]
    rank1 = jnp.sum(jnp.where(sel1, cnt, 0.0), axis=-1, keepdims=True)
    rank2 = jnp.sum(jnp.where(sel2, cnt, 0.0), axis=-1, keepdims=True)
    total = carry_ref[0:1] + jnp.sum(onehot, axis=0, keepdims=True)
    carry_ref[...] = jnp.broadcast_to(total, carry_ref.shape)
    cnt_ref[...] = jnp.broadcast_to(total, cnt_ref.shape)

    slab = jnp.where(lane == 0, e1, 0.0)
    slab = jnp.where(lane == 1, e2, slab)
    slab = jnp.where(lane == 2, w1, slab)
    slab = jnp.where(lane == 3, w2, slab)
    slab = jnp.where(lane == 4, rank1, slab)
    slab = jnp.where(lane == 5, rank2, slab)
    slab_ref[0] = slab


def _post_mixer(lru_f, lru_b, proj, hg, x, nlw, nhw, wo_bf16, g_mix, nfw, sc_ffn, sh_ffn, wr_bf16, br,
                *, tm=256):
    bsz, seq, d = x.shape
    d_lru = lru_f.shape[-1]
    d_hgrn = hg.shape[-1]
    y_col = 1
    g_col = (2 * d_lru) // d_hgrn + 4
    row = lambda w: pl.BlockSpec((1, tm, w), lambda b, i: (b, i, 0))
    vec = lambda w: pl.BlockSpec((1, w), lambda b, i: (0, 0))
    per_b = lambda: pl.BlockSpec((1, 1, d), lambda b, i: (b, 0, 0))
    kern = functools.partial(_post_kernel, tm=tm, d_lru=d_lru)
    return pl.pallas_call(
        kern,
        out_shape=(
            jax.ShapeDtypeStruct((bsz, seq, d), F32),
            jax.ShapeDtypeStruct((bsz, seq, d), F32),
            jax.ShapeDtypeStruct((bsz, seq, ROUTE_LANES), F32),
            jax.ShapeDtypeStruct((SUBLANES, ROUTE_LANES), F32),
        ),
        grid=(bsz, seq // tm),
        in_specs=[
            row(d_lru), row(d_lru),
            pl.BlockSpec((1, tm, d_lru), lambda b, i: (b, i, y_col)),
            pl.BlockSpec((1, 1, tm, d_hgrn), lambda b, i: (0, b, i, 0)),
            pl.BlockSpec((1, 1, tm, d_hgrn), lambda b, i: (1, b, i, 0)),
            pl.BlockSpec((1, tm, d_hgrn), lambda b, i: (b, i, g_col)),
            row(d), vec(d_lru), vec(d_hgrn),
            pl.BlockSpec((d, d), lambda b, i: (0, 0)),
            per_b(), vec(d), per_b(), per_b(),
            pl.BlockSpec((d, ROUTE_LANES), lambda b, i: (0, 0)),
            vec(ROUTE_LANES),
        ],
        out_specs=(
            row(d), row(d), row(ROUTE_LANES),
            pl.BlockSpec((SUBLANES, ROUTE_LANES), lambda b, i: (0, 0)),
        ),
        scratch_shapes=[pltpu.VMEM((SUBLANES, ROUTE_LANES), F32)],
        compiler_params=_params(("arbitrary", "arbitrary")),
        name="post_mixer_router",
    )(lru_f, lru_b, proj, hg, hg, proj, x, nlw.reshape(1, d_lru), nhw.reshape(1, d_hgrn), wo_bf16,
      g_mix, nfw.reshape(1, d), sc_ffn, sh_ffn, wr_bf16, br)


def _dispatch_kernel(d1_ref, d2_ref, h_ref, z_ref, o_ref, sem, *, tb):
    del z_ref
    base = pl.program_id(0) * tb

    def issue(r, carry):
        t = base + r
        pltpu.make_async_copy(h_ref.at[pl.ds(r, 1)], o_ref.at[pl.ds(d1_ref[t], 1)], sem).start()
        pltpu.make_async_copy(h_ref.at[pl.ds(r, 1)], o_ref.at[pl.ds(d2_ref[t], 1)], sem).start()
        return carry

    lax.fori_loop(0, tb, issue, 0, unroll=DMA_ISSUE_UNROLL)
    for _ in range(2):
        pltpu.make_async_copy(h_ref, o_ref.at[pl.ds(0, tb)], sem).wait()


def _dispatch(dest1, dest2, h_flat, n_rows, *, tb=256):
    m, d = h_flat.shape
    kern = functools.partial(_dispatch_kernel, tb=tb)
    return pl.pallas_call(
        kern,
        out_shape=jax.ShapeDtypeStruct((n_rows, d), h_flat.dtype),
        grid_spec=pltpu.PrefetchScalarGridSpec(
            num_scalar_prefetch=2,
            grid=(m // tb,),
            in_specs=[pl.BlockSpec((tb, d), lambda i, d1, d2: (i, 0)), pl.BlockSpec(memory_space=pl.ANY)],
            out_specs=pl.BlockSpec(memory_space=pl.ANY),
            scratch_shapes=[pltpu.SemaphoreType.DMA(())],
        ),
        input_output_aliases={3: 0},
        compiler_params=pltpu.CompilerParams(dimension_semantics=("arbitrary",), has_side_effects=True),
        name="moe_dispatch",
    )(dest1, dest2, h_flat, jnp.zeros((n_rows, d), h_flat.dtype))


def _expert_kernel(be_ref, x_ref, wg_ref, wu_ref, wd_ref, o_ref, wg_s, wu_s, wd_s):
    i = pl.program_id(0)
    prev_expert = be_ref[jnp.maximum(i - 1, 0)]

    @pl.when((i == 0) | (be_ref[i] != prev_expert))
    def _():
        wg_s[...] = wg_ref[0, 0].astype(BF16)
        wu_s[...] = wu_ref[0, 0].astype(BF16)
        wd_s[...] = wd_ref[0, 0].astype(BF16)

    x = x_ref[...].astype(BF16)
    gate = _dot(x, wg_s[...])
    up = _dot(x, wu_s[...])
    act = (gate * _sigmoid(gate)) * up
    o_ref[...] = _dot(act.astype(BF16), wd_s[...])


def _experts(blk_expert, x_buf, wg, wu, wd, layer):
    n_rows, d = x_buf.shape
    de = wg.shape[-1]
    blk = EXPERT_BLOCK
    return pl.pallas_call(
        _expert_kernel,
        out_shape=jax.ShapeDtypeStruct((n_rows, d), F32),
        grid_spec=pltpu.PrefetchScalarGridSpec(
            num_scalar_prefetch=1,
            grid=(n_rows // blk,),
            in_specs=[
                pl.BlockSpec((blk, d), lambda i, be: (i, 0)),
                pl.BlockSpec((1, 1, d, de), lambda i, be: (layer, be[i], 0, 0)),
                pl.BlockSpec((1, 1, d, de), lambda i, be: (layer, be[i], 0, 0)),
                pl.BlockSpec((1, 1, de, d), lambda i, be: (layer, be[i], 0, 0)),
            ],
            out_specs=pl.BlockSpec((blk, d), lambda i, be: (i, 0)),
            scratch_shapes=[pltpu.VMEM((d, de), BF16), pltpu.VMEM((d, de), BF16), pltpu.VMEM((de, d), BF16)],
        ),
        compiler_params=_params(("arbitrary",)),
        name="moe_experts",
    )(blk_expert, x_buf, wg, wu, wd)


def _combine_kernel(d1_ref, d2_ref, y_ref, slab_ref, x_ref, g_ref, nw_ref, o_ref, r1_ref, r2_ref, sem,
                    *, tm, tiles, final_norm):
    base = (pl.program_id(0) * tiles + pl.program_id(1)) * tm

    def issue(r, carry):
        t = base + r
        pltpu.make_async_copy(y_ref.at[pl.ds(d1_ref[t], 1)], r1_ref.at[pl.ds(r, 1)], sem).start()
        pltpu.make_async_copy(y_ref.at[pl.ds(d2_ref[t], 1)], r2_ref.at[pl.ds(r, 1)], sem).start()
        return carry

    lax.fori_loop(0, tm, issue, 0, unroll=DMA_ISSUE_UNROLL)
    pltpu.make_async_copy(y_ref.at[pl.ds(0, tm)], r1_ref, sem).wait()
    pltpu.make_async_copy(y_ref.at[pl.ds(0, tm)], r2_ref, sem).wait()
    slab = slab_ref[0]
    y = slab[:, 2:3] * r1_ref[...] + slab[:, 3:4] * r2_ref[...]
    out = x_ref[0] + g_ref[0] * y
    if final_norm:
        ms = jnp.mean(out * out, axis=-1, keepdims=True)
        out = out * lax.rsqrt(ms + NORM_EPS) * nw_ref[...]
    o_ref[0] = out


def _combine(dest1, dest2, y_buf, slab, x, g_ffn, norm_w, *, final_norm, tm=256):
    bsz, seq, d = x.shape
    tiles = seq // tm
    kern = functools.partial(_combine_kernel, tm=tm, tiles=tiles, final_norm=final_norm)
    return pl.pallas_call(
        kern,
        out_shape=jax.ShapeDtypeStruct((bsz, seq, d), F32),
        grid_spec=pltpu.PrefetchScalarGridSpec(
            num_scalar_prefetch=2,
            grid=(bsz, tiles),
            in_specs=[
                pl.BlockSpec(memory_space=pl.ANY),
                pl.BlockSpec((1, tm, ROUTE_LANES), lambda b, i, d1, d2: (b, i, 0)),
                pl.BlockSpec((1, tm, d), lambda b, i, d1, d2: (b, i, 0)),
                pl.BlockSpec((1, 1, d), lambda b, i, d1, d2: (b, 0, 0)),
                pl.BlockSpec((1, d), lambda b, i, d1, d2: (0, 0)),
            ],
            out_specs=pl.BlockSpec((1, tm, d), lambda b, i, d1, d2: (b, i, 0)),
            scratch_shapes=[pltpu.VMEM((tm, d), F32), pltpu.VMEM((tm, d), F32), pltpu.SemaphoreType.DMA(())],
        ),
        compiler_params=_params(("arbitrary", "arbitrary")),
        name="moe_combine",
    )(dest1, dest2, y_buf, slab, x, g_ffn, norm_w.reshape(1, d))


def _block_diag(w):
    heads, hd, _ = w.shape
    eye = jnp.eye(heads, dtype=w.dtype)
    return (w[:, :, None, :] * eye[:, None, :, None]).reshape(heads * hd, heads * hd)


def kernel(x, c, ada_w, ada_b, norm_mix_w, w_in, conv_w, conv_b, lru_wa, lru_ba, lru_wx, lru_bx, lru_lambda, norm_lru_w, hgrn_lb, norm_hgrn_w, w_out, norm_ffn_w, router_group_w, router_group_b, router_expert_w, router_expert_b, expert_w_gate, expert_w_up, expert_w_down, final_norm_w):
    bsz, seq, d = x.shape
    depth = ada_w.shape[0]
    d_lru = conv_w.shape[-1]
    d_hgrn = hgrn_lb.shape[-1]
    m = bsz * seq
    n_rows = m * 2 + N_EXPERTS * EXPERT_BLOCK
    n_blocks = n_rows // EXPERT_BLOCK

    mod = _modulation(c, ada_w, ada_b)
    lb_cum = jnp.cumsum(jax.nn.softmax(hgrn_lb.astype(F32), axis=0), axis=0)
    lb_all = lb_cum - lb_cum[0:1]

    for l in range(depth):
        sh_mix, sc_mix, g_mix, sh_ffn, sc_ffn, g_ffn = [
            mod[l, :, i * d:(i + 1) * d].reshape(bsz, 1, d) for i in range(6)]
        proj = _in_proj(x, norm_mix_w[l], sc_mix, sh_mix, w_in[l].astype(BF16))
        wa_bd = jnp.stack([_block_diag(lru_wa[l, 0]), _block_diag(lru_wa[l, 1])]).astype(BF16)
        wx_bd = jnp.stack([_block_diag(lru_wx[l, 0]), _block_diag(lru_wx[l, 1])]).astype(BF16)
        lru = [
            _lru_scan(proj, conv_w[l], conv_b[l], wa_bd, lru_ba[l], wx_bd, lru_bx[l], lru_lambda[l],
                      reverse=rv)
            for rv in (False, True)]
        hg = _hgrn(proj, lb_all[l], d_lru=d_lru, d_hgrn=d_hgrn)

        wr = jnp.zeros((d, ROUTE_LANES), F32)
        wr = wr.at[:, :N_GROUPS].set(router_group_w[l]).at[:, N_GROUPS:N_GROUPS + N_EXPERTS].set(router_expert_w[l])
        br = jnp.zeros((1, ROUTE_LANES), F32)
        br = br.at[0, :N_GROUPS].set(router_group_b[l]).at[0, N_GROUPS:N_GROUPS + N_EXPERTS].set(router_expert_b[l])
        x_mid, h_ffn, slab, counts = _post_mixer(
            lru[0], lru[1], proj, hg, x, norm_lru_w[l], norm_hgrn_w[l], w_out[l].astype(BF16), g_mix,
            norm_ffn_w[l], sc_ffn, sh_ffn, wr.astype(BF16), br)

        cnt = counts[0, :N_EXPERTS].astype(jnp.int32)
        padded = ((cnt + EXPERT_BLOCK - 1) // EXPERT_BLOCK) * EXPERT_BLOCK
        pend = jnp.cumsum(padded)
        pstart = pend - padded
        blk_start = jnp.arange(n_blocks, dtype=jnp.int32) * EXPERT_BLOCK
        blk_expert = jnp.minimum(jnp.sum(pend[None, :] <= blk_start[:, None], axis=1), N_EXPERTS - 1)
        blk_expert = blk_expert.astype(jnp.int32)
        slab_flat = slab.reshape(m, ROUTE_LANES)
        e1 = slab_flat[:, 0].astype(jnp.int32)
        e2 = slab_flat[:, 1].astype(jnp.int32)
        dest1 = pstart[e1] + slab_flat[:, 4].astype(jnp.int32)
        dest2 = pstart[e2] + slab_flat[:, 5].astype(jnp.int32)

        x_buf = _dispatch(dest1, dest2, h_ffn.reshape(m, d), n_rows)
        y_buf = _experts(blk_expert, x_buf, expert_w_gate, expert_w_up, expert_w_down, l)
        x = _combine(dest1, dest2, y_buf, slab, x_mid, g_ffn, final_norm_w, final_norm=(l == depth - 1))

    return x
```

```python
import functools

import jax
import jax.numpy as jnp
from jax import lax
from jax.experimental import pallas as pl
from jax.experimental.pallas import tpu as pltpu

F32 = jnp.float32
BF16 = jnp.bfloat16

LRU_HEADS = 8
HGRN_HEADS = 8
CONV_WIDTH = 4
LRU_C = 8.0
N_GROUPS = 4
EXPERTS_PER_GROUP = 8
N_EXPERTS = N_GROUPS * EXPERTS_PER_GROUP
NORM_EPS = 1e-6

LANES = 128
SUBLANES = 8
VMEM_LIMIT = 56 * 1024 * 1024

HGRN_CHUNK = 64
HGRN_SUB = 8
LOG2E = 1.4426950408889634
ROUTE_LANES = LANES
EXPERT_BLOCK = 256
DMA_ISSUE_UNROLL = 8
NEG_BIG = -3.0e38


def _params(sem):
    return pltpu.CompilerParams(dimension_semantics=sem, vmem_limit_bytes=VMEM_LIMIT)


def _dot(a, b):
    return jnp.dot(a, b, preferred_element_type=F32)


def _dot_nt(a, b):
    return lax.dot_general(a, b, (((1,), (1,)), ((), ())), preferred_element_type=F32)


def _dot_tn(a, b):
    return lax.dot_general(a, b, (((0,), (0,)), ((), ())), preferred_element_type=F32)


def _dot01_exact(m01, x):
    hi = x.astype(BF16)
    r1 = x - hi.astype(F32)
    mid = r1.astype(BF16)
    lo = (r1 - mid.astype(F32)).astype(BF16)
    return _dot(m01, hi) + _dot(m01, mid) + _dot(m01, lo)


def _sigmoid(x):
    return 1.0 / (1.0 + jnp.exp(-x))


def _mod_kernel(c_ref, w_ref, b_ref, o_ref):
    c = c_ref[...]
    cond = c * _sigmoid(c)
    o_ref[0] = _dot(cond.astype(BF16), w_ref[0].astype(BF16)) + b_ref[0]


def _modulation(c, ada_w, ada_b):
    depth, d, n = ada_w.shape
    bsz = c.shape[0]
    rows = -(-bsz // SUBLANES) * SUBLANES
    c_pad = jnp.pad(c, ((0, rows - bsz), (0, 0)))
    tn = n // 6
    out = pl.pallas_call(
        _mod_kernel,
        out_shape=jax.ShapeDtypeStruct((depth, rows, n), F32),
        grid=(depth, n // tn),
        in_specs=[
            pl.BlockSpec((rows, d), lambda l, j: (0, 0)),
            pl.BlockSpec((1, d, tn), lambda l, j: (l, 0, j)),
            pl.BlockSpec((1, 1, tn), lambda l, j: (l, 0, j)),
        ],
        out_specs=pl.BlockSpec((1, rows, tn), lambda l, j: (l, 0, j)),
        compiler_params=_params(("arbitrary", "arbitrary")),
        name="adaln_mod",
    )(c_pad, ada_w, ada_b.reshape(depth, 1, n))
    return out[:, :bsz]


def _rms_mod(x, nw, sc, sh):
    ms = jnp.mean(x * x, axis=-1, keepdims=True)
    return (x * lax.rsqrt(ms + NORM_EPS) * nw) * (1.0 + sc) + sh


def _inproj_kernel(x_ref, nw_ref, sc_ref, sh_ref, w_ref, o_ref):
    h = _rms_mod(x_ref[0], nw_ref[...], sc_ref[0], sh_ref[0])
    o_ref[0] = _dot(h.astype(BF16), w_ref[...])


def _in_proj(x, nw, sc, sh, w_bf16, tm=512):
    bsz, seq, d = x.shape
    n = w_bf16.shape[1]
    return pl.pallas_call(
        _inproj_kernel,
        out_shape=jax.ShapeDtypeStruct((bsz, seq, n), F32),
        grid=(bsz, seq // tm),
        in_specs=[
            pl.BlockSpec((1, tm, d), lambda b, i: (b, i, 0)),
            pl.BlockSpec((1, d), lambda b, i: (0, 0)),
            pl.BlockSpec((1, 1, d), lambda b, i: (b, 0, 0)),
            pl.BlockSpec((1, 1, d), lambda b, i: (b, 0, 0)),
            pl.BlockSpec((d, n), lambda b, i: (0, 0)),
        ],
        out_specs=pl.BlockSpec((1, tm, n), lambda b, i: (b, i, 0)),
        compiler_params=_params(("arbitrary", "arbitrary")),
        name="in_proj",
    )(x, nw.reshape(1, d), sc, sh, w_bf16)


def _lru_kernel(x_ref, xp_ref, xn_ref, cw_ref, cb_ref, wa_ref, ba_ref, wx_ref, bx_ref, lam_ref,
                o_ref, carry_ref, sa_ref, sb_ref, cin_ref, *, reverse, n_chunks, rows):
    c = pl.program_id(1)
    chunk = (n_chunks - 1 - c) if reverse else c

    @pl.when(c == 0)
    def _():
        carry_ref[...] = jnp.zeros_like(carry_ref)

    x = x_ref[0]
    width = x.shape[1]
    row = lax.broadcasted_iota(jnp.int32, (rows, width), 0)
    has_prev = jnp.where(chunk > 0, 1.0, 0.0)
    has_next = jnp.where(chunk < n_chunks - 1, 1.0, 0.0)
    xp = xp_ref[0] * has_prev
    xn = xn_ref[0] * has_next
    xe = jnp.concatenate([xp, x, xn], axis=0)
    h8 = SUBLANES
    cw = cw_ref[...]
    xc = (cw[0:1] * xe[h8 - 2:h8 - 2 + rows] + cw[1:2] * xe[h8 - 1:h8 - 1 + rows] + cw[2:3] * x
          + cw[3:4] * xe[h8 + 1:h8 + 1 + rows] + cb_ref[...])

    xcb = xc.astype(BF16)
    r = _sigmoid(_dot(xcb, wa_ref[0]) + ba_ref[0])
    gate_i = _sigmoid(_dot(xcb, wx_ref[0]) + bx_ref[0])
    lam = lam_ref[0]
    softplus_neg_lam = jnp.maximum(-lam, 0.0) + jnp.log1p(jnp.exp(-jnp.abs(lam)))
    log_a = (-LRU_C) * r * softplus_neg_lam
    a = jnp.exp(log_a)
    t = jnp.tanh(-log_a)
    u = jnp.sqrt(2.0 * t / (1.0 + t)) * (gate_i * xc)

    groups = rows // SUBLANES
    acc_a = a.reshape(groups, SUBLANES, width)
    acc_b = u.reshape(groups, SUBLANES, width)
    sub = lax.broadcasted_iota(jnp.int32, (groups, SUBLANES, width), 1)
    s = 1
    while s < SUBLANES:
        if reverse:
            valid = sub < SUBLANES - s
            sh_a, sh_b = pltpu.roll(acc_a, SUBLANES - s, 1), pltpu.roll(acc_b, SUBLANES - s, 1)
        else:
            valid = sub >= s
            sh_a, sh_b = pltpu.roll(acc_a, s, 1), pltpu.roll(acc_b, s, 1)
        acc_b = jnp.where(valid, acc_a * sh_b + acc_b, acc_b)
        acc_a = jnp.where(valid, acc_a * sh_a, acc_a)
        s *= 2
    acc_a = acc_a.reshape(rows, width)
    acc_b = acc_b.reshape(rows, width)
    groups = rows // SUBLANES
    edge = 0 if reverse else SUBLANES - 1
    n_tiles = width // LANES
    for j in range(n_tiles):
        sa_ref[j] = acc_a[:, j * LANES:(j + 1) * LANES]
        sb_ref[j] = acc_b[:, j * LANES:(j + 1) * LANES]
    ea = jnp.concatenate([sa_ref[j, pl.ds(edge, groups, stride=SUBLANES), :] for j in range(n_tiles)], axis=1)
    eb = jnp.concatenate([sb_ref[j, pl.ds(edge, groups, stride=SUBLANES), :] for j in range(n_tiles)], axis=1)
    grow = lax.broadcasted_iota(jnp.int32, (groups, width), 0)
    s = 1
    while s < groups:
        if reverse:
            valid = grow < groups - s
            sh_a, sh_b = pltpu.roll(ea, groups - s, 0), pltpu.roll(eb, groups - s, 0)
        else:
            valid = grow >= s
            sh_a, sh_b = pltpu.roll(ea, s, 0), pltpu.roll(eb, s, 0)
        eb = jnp.where(valid, ea * sh_b + eb, eb)
        ea = jnp.where(valid, ea * sh_a, ea)
        s *= 2
    carry = carry_ref[...]
    group_out = eb + ea * carry
    if reverse:
        carry_in = jnp.where(grow == groups - 1, carry, pltpu.roll(group_out, groups - 1, 0))
        carry_ref[...] = group_out[0:1]
    else:
        carry_in = jnp.where(grow == 0, carry, pltpu.roll(group_out, 1, 0))
        carry_ref[...] = group_out[groups - 1:groups]
    cin_ref[...] = carry_in
    for g in range(groups):
        rs = slice(g * SUBLANES, (g + 1) * SUBLANES)
        o_ref[0, rs, :] = acc_b[rs] + acc_a[rs] * cin_ref[g:g + 1, :]


def _lru_scan(proj, conv_w, conv_b, wa_bd, ba, wx_bd, bx, lam, *, reverse, rows=256):
    bsz, seq, _ = proj.shape
    d_lru = conv_w.shape[1]
    n_chunks = seq // rows
    halo = rows // SUBLANES
    last_halo = seq // SUBLANES - 1
    dirn = 1 if reverse else 0

    def chunk_of(c):
        return (n_chunks - 1 - c) if reverse else c

    vec = lambda: pl.BlockSpec((1, 1, d_lru), lambda b, c: (dirn, 0, 0))
    mat = lambda: pl.BlockSpec((1, d_lru, d_lru), lambda b, c: (dirn, 0, 0))
    kern = functools.partial(_lru_kernel, reverse=reverse, n_chunks=n_chunks, rows=rows)
    return pl.pallas_call(
        kern,
        out_shape=jax.ShapeDtypeStruct((bsz, seq, d_lru), F32),
        grid=(bsz, n_chunks),
        in_specs=[
            pl.BlockSpec((1, rows, d_lru), lambda b, c: (b, chunk_of(c), 0)),
            pl.BlockSpec((1, SUBLANES, d_lru),
                         lambda b, c: (b, jnp.maximum(chunk_of(c) * halo - 1, 0), 0)),
            pl.BlockSpec((1, SUBLANES, d_lru),
                         lambda b, c: (b, jnp.minimum((chunk_of(c) + 1) * halo, last_halo), 0)),
            pl.BlockSpec((CONV_WIDTH, d_lru), lambda b, c: (0, 0)),
            pl.BlockSpec((1, d_lru), lambda b, c: (0, 0)),
            mat(), vec(), mat(), vec(), vec(),
        ],
        out_specs=pl.BlockSpec((1, rows, d_lru), lambda b, c: (b, chunk_of(c), 0)),
        scratch_shapes=[pltpu.VMEM((1, d_lru), F32), pltpu.VMEM((d_lru // LANES, rows, LANES), F32),
                        pltpu.VMEM((d_lru // LANES, rows, LANES), F32),
                        pltpu.VMEM((rows // SUBLANES, d_lru), F32)],
        compiler_params=_params(("arbitrary", "arbitrary")),
        name="lru_bwd" if reverse else "lru_fwd",
    )(proj, proj, proj, conv_w, conv_b.reshape(1, d_lru), wa_bd, ba.reshape(2, 1, d_lru),
      wx_bd, bx.reshape(2, 1, d_lru), lam.reshape(2, 1, d_lru))


def _hgrn_direction(rev, q_ref, f_ref, v_ref, lb_ref, o_ref, st_ref, diag_s, lvl_s, upd_s, qe_s, btot_s, *, rows):
    ck, sb = HGRN_CHUNK, HGRN_SUB
    n_blk = ck // sb
    sb_shift = sb.bit_length() - 1
    n_sub = rows // ck
    width = q_ref.shape[-1]
    n_pairs = width // LANES
    half = LANES // 2

    def flip(idx, n):
        return (n - 1 - idx) if rev else idx

    n_lvl = n_blk.bit_length() - 1
    tf = flip(lax.broadcasted_iota(jnp.int32, (ck, ck), 0), ck)
    uf = flip(lax.broadcasted_iota(jnp.int32, (ck, ck), 1), ck)
    tb, ub = tf >> sb_shift, uf >> sb_shift
    pb = flip(lax.broadcasted_iota(jnp.int32, (n_blk, ck), 0), n_blk)
    pub = flip(lax.broadcasted_iota(jnp.int32, (n_blk, ck), 1), ck) >> sb_shift
    mats = [jnp.where((tb == ub) & (uf <= tf), 1.0, 0.0),
            jnp.where(pub < pb, 1.0, 0.0)]
    for lvl in range(n_lvl):
        mid = ((pb >> (lvl + 1)) << (lvl + 1)) + (1 << lvl)
        mats.append(jnp.where(pub < mid, 1.0, 0.0))
    mats.append(jnp.ones((SUBLANES, ck), F32))
    m_cum = jnp.concatenate(mats, axis=0).astype(BF16)

    def per_block(rows8):
        return jnp.concatenate(
            [jnp.broadcast_to(rows8[jb:jb + 1], (sb, rows8.shape[1])) for jb in range(n_blk)], axis=0)
    row_blk = flip(lax.broadcasted_iota(jnp.int32, (ck, width), 0), ck) >> sb_shift
    upper = [((row_blk >> lvl) & 1) == 1 for lvl in range(n_lvl)]
    pr = flip(lax.broadcasted_iota(jnp.int32, (ck, LANES), 0), ck) >> sb_shift
    pc = flip(lax.broadcasted_iota(jnp.int32, (ck, LANES), 1) & (ck - 1), ck) >> sb_shift
    group_mask = [(pr >> (lvl + 1)) == (pc >> (lvl + 1)) for lvl in range(n_lvl)]
    lane = lax.broadcasted_iota(jnp.int32, (1, LANES), 1)
    head0 = lane < half
    sr = lax.broadcasted_iota(jnp.int32, (LANES, LANES), 0)
    sc = lax.broadcasted_iota(jnp.int32, (LANES, LANES), 1)
    same_head = (sr < half) == (sc < half)
    er = lax.broadcasted_iota(jnp.int32, (sb * LANES, LANES), 0)
    ec = lax.broadcasted_iota(jnp.int32, (sb * LANES, LANES), 1)
    sel = jnp.where(ec == (((er & (LANES - 1)) >> (half.bit_length() - 1)) * half + (er >> (LANES.bit_length() - 1))),
                    1.0, 0.0).astype(BF16)
    sub_row = flip(lax.broadcasted_iota(jnp.int32, (sb, LANES), 0), sb)
    lbv = lb_ref[...]

    def row_start(j):
        return pl.multiple_of(flip(j, n_sub) * ck, ck)

    def stage1a(j):
        r0 = row_start(j)
        q = q_ref[0, pl.ds(r0, ck), :]
        z = f_ref[0, pl.ds(r0, ck), :]
        v = v_ref[0, pl.ds(r0, ck), :]
        f = lbv + (1.0 - lbv) * _sigmoid(z)
        lf = jnp.log(f)
        k = 1.0 - f
        return q, v, k, _dot01_exact(m_cum, lf)

    def stage1b(q, v, k, cums):
        bl = cums[0:ck]
        b = bl + per_block(cums[ck:ck + n_blk])
        tot_row = ck + (1 + n_lvl) * n_blk
        btot = cums[tot_row:tot_row + 1]
        qe = q * jnp.exp(b)
        ke = k * jnp.exp(btot - b)
        log2_k = jnp.log(k) * LOG2E
        b2 = b * LOG2E
        kb = b2 - log2_k
        bl2 = bl * LOG2E
        kbl = bl2 - log2_k
        q_lvl, k_lvl = [], []
        for lvl in range(n_lvl):
            split2 = per_block(cums[ck + (1 + lvl) * n_blk:ck + (2 + lvl) * n_blk] * LOG2E)
            q_lvl.append(q * jnp.exp2(jnp.where(upper[lvl], b2 - split2, NEG_BIG)))
            k_lvl.append(jnp.exp2(jnp.where(upper[lvl], NEG_BIG, split2 - kb)))

        qe_s[...] = qe.astype(BF16)
        btot_s[...] = btot
        for p in range(n_pairs):
            sl = slice(p * LANES, (p + 1) * LANES)
            diag_rows = []
            for jb in range(n_blk):
                rs = slice(jb * sb, (jb + 1) * sb)
                bl_b, kbl_b, q_b = bl2[rs, sl], kbl[rs, sl], q[rs, sl]
                terms = []
                for s in range(sb):
                    arg = jnp.where(sub_row >= flip(s, sb), bl_b - kbl_b[s:s + 1], NEG_BIG)
                    terms.append(q_b * jnp.exp2(arg))
                diag_rows.append(jnp.concatenate(terms, axis=1))
            diag_s[p] = _dot(jnp.concatenate(diag_rows, axis=0).astype(BF16), sel)
            for lvl in range(n_lvl):
                k_p = k_lvl[lvl][:, sl]
                k_heads = jnp.concatenate([jnp.where(head0, k_p, 0.0), jnp.where(head0, 0.0, k_p)], axis=0)
                lvl_s[p * n_lvl + lvl] = _dot_nt(q_lvl[lvl][:, sl].astype(BF16), k_heads.astype(BF16))
            upd_s[p] = _dot_tn(v[:, sl].astype(BF16), ke[:, sl].astype(BF16))

    def stage2_issue(j):
        r0 = row_start(j)
        v = v_ref[0, pl.ds(r0, ck), :]
        out = []
        for p in range(n_pairs):
            sl = slice(p * LANES, (p + 1) * LANES)
            parts = []
            for jb in range(n_blk):
                blk = diag_s[p, jb * sb:(jb + 1) * sb, :]
                parts.append(pltpu.roll(blk, jb * sb, 1) if jb else blk)
            scores = jnp.concatenate(parts, axis=0)
            for lvl in range(n_lvl):
                scores = scores + jnp.where(group_mask[lvl], lvl_s[p * n_lvl + lvl], 0.0)
            v_p = v[:, sl]
            v_heads = jnp.concatenate([jnp.where(head0, v_p, 0.0), jnp.where(head0, 0.0, v_p)], axis=0)
            intra = _dot(scores.astype(BF16), v_heads.astype(BF16))
            st = st_ref[p]
            inter = _dot_nt(qe_s[:, sl], st.astype(BF16))
            new_st = jnp.where(same_head, st * jnp.exp(btot_s[:, sl]) + upd_s[p], 0.0)
            out.append((inter + intra, new_st))
        return r0, out

    def stage2_finish(r0, out):
        for p in range(n_pairs):
            o_ref[0, pl.ds(r0, ck), p * LANES:(p + 1) * LANES] = out[p][0]
            st_ref[p] = out[p][1]

    return stage1a, stage1b, stage2_issue, stage2_finish


N_HGRN_SCRATCH = 6


def _hgrn_kernel(qf_ref, ff_ref, vf_ref, qb_ref, fb_ref, vb_ref, lb_ref, of_ref, ob_ref, *scratch, rows):
    fwd_scratch, bwd_scratch = scratch[:N_HGRN_SCRATCH], scratch[N_HGRN_SCRATCH:]

    @pl.when(pl.program_id(1) == 0)
    def _():
        fwd_scratch[0][...] = jnp.zeros_like(fwd_scratch[0])
        bwd_scratch[0][...] = jnp.zeros_like(bwd_scratch[0])

    f1a, f1b, f2, f3 = _hgrn_direction(False, qf_ref, ff_ref, vf_ref, lb_ref, of_ref, *fwd_scratch, rows=rows)
    b1a, b1b, b2, b3 = _hgrn_direction(True, qb_ref, fb_ref, vb_ref, lb_ref, ob_ref, *bwd_scratch, rows=rows)
    n_sub = rows // HGRN_CHUNK

    def stage1_both(j):
        fa = f1a(j)
        ba = b1a(j)
        f1b(*fa)
        b1b(*ba)

    stage1_both(0)

    def pipelined(j, carry):
        fo = f2(j)
        bo = b2(j)
        stage1_both(j + 1)
        f3(*fo)
        b3(*bo)
        return carry

    lax.fori_loop(0, n_sub - 1, pipelined, 0)
    fo = f2(n_sub - 1)
    bo = b2(n_sub - 1)
    f3(*fo)
    b3(*bo)


def _hgrn(proj, lb, *, d_lru, d_hgrn, rows=512):
    bsz, seq, _ = proj.shape
    n_chunks = seq // rows
    col0 = (2 * d_lru) // d_hgrn
    n_pairs = d_hgrn // LANES
    n_lvl = (HGRN_CHUNK // HGRN_SUB).bit_length() - 1
    fwd = lambda col: pl.BlockSpec((1, rows, d_hgrn), lambda b, c: (b, c, col))
    bwd = lambda col: pl.BlockSpec((1, rows, d_hgrn), lambda b, c: (b, n_chunks - 1 - c, col))
    direction_scratch = [
        pltpu.VMEM((n_pairs, LANES, LANES), F32),
        pltpu.VMEM((n_pairs, HGRN_CHUNK, LANES), F32),
        pltpu.VMEM((n_pairs * n_lvl, HGRN_CHUNK, LANES), F32),
        pltpu.VMEM((n_pairs, LANES, LANES), F32),
        pltpu.VMEM((HGRN_CHUNK, d_hgrn), BF16),
        pltpu.VMEM((1, d_hgrn), F32),
    ]
    assert len(direction_scratch) == N_HGRN_SCRATCH
    kern = functools.partial(_hgrn_kernel, rows=rows)
    out = jax.ShapeDtypeStruct((bsz, seq, d_hgrn), F32)
    return pl.pallas_call(
        kern,
        out_shape=(out, out),
        grid=(bsz, n_chunks),
        in_specs=[fwd(col0), fwd(col0 + 1), fwd(col0 + 3), bwd(col0), bwd(col0 + 2), bwd(col0 + 3),
                  pl.BlockSpec((1, d_hgrn), lambda b, c: (0, 0))],
        out_specs=(pl.BlockSpec((1, rows, d_hgrn), lambda b, c: (b, c, 0)),
                   pl.BlockSpec((1, rows, d_hgrn), lambda b, c: (b, n_chunks - 1 - c, 0))),
        scratch_shapes=direction_scratch + direction_scratch,
        compiler_params=_params(("arbitrary", "arbitrary")),
        name="hgrn2",
    )(proj, proj, proj, proj, proj, proj, lb.reshape(1, d_hgrn))


def _gelu_tanh(y):
    return 0.5 * y * (1.0 + jnp.tanh(0.7978845608028654 * (y + 0.044715 * (y * y * y))))


def _post_kernel(lf_ref, lb_ref, y_ref, of_ref, ob_ref, g_ref, x_ref, nlw_ref, nhw_ref, wo_ref,
                 gm_ref, nfw_ref, scf_ref, shf_ref, wr_ref, br_ref,
                 xo_ref, h_ref, slab_ref, cnt_ref, carry_ref, *, tm, d_lru):
    first = (pl.program_id(0) == 0) & (pl.program_id(1) == 0)

    @pl.when(first)
    def _():
        carry_ref[...] = jnp.zeros_like(carry_ref)

    lru = (lf_ref[0] + lb_ref[0]) * _gelu_tanh(y_ref[0])
    ms = jnp.mean(lru * lru, axis=-1, keepdims=True)
    lru = lru * lax.rsqrt(ms + NORM_EPS) * nlw_ref[...]

    hg = of_ref[0] + ob_ref[0]
    width = hg.shape[1]
    hd = width // HGRN_HEADS
    hd_shift = hd.bit_length() - 1
    er = lax.broadcasted_iota(jnp.int32, (width, width), 0) >> hd_shift
    ec = lax.broadcasted_iota(jnp.int32, (width, width), 1) >> hd_shift
    head_sum = jnp.where(er == ec, 1.0, 0.0).astype(BF16)
    sq = hg * hg
    sq_hi = sq.astype(BF16)
    sq_lo = (sq - sq_hi.astype(F32)).astype(BF16)
    ms_h = (_dot(sq_hi, head_sum) + _dot(sq_lo, head_sum)) * (1.0 / hd)
    g = g_ref[0]
    hg = (hg * lax.rsqrt(ms_h + NORM_EPS) * nhw_ref[...]) * (g * _sigmoid(g))

    mixed = _dot(lru.astype(BF16), wo_ref[0:d_lru, :]) + _dot(hg.astype(BF16), wo_ref[d_lru:, :])
    x_new = x_ref[0] + gm_ref[0] * mixed
    xo_ref[0] = x_new

    h = _rms_mod(x_new, nfw_ref[...], scf_ref[0], shf_ref[0])
    h_ref[0] = h

    logits = _dot(h.astype(BF16), wr_ref[...]) + br_ref[...]
    lane = lax.broadcasted_iota(jnp.int32, (tm, ROUTE_LANES), 1)
    lane_f = lane.astype(F32)
    far = float(ROUTE_LANES)
    is_g = lane < N_GROUPS
    gl = jnp.where(is_g, logits, NEG_BIG)
    gmax = jnp.max(gl, axis=-1, keepdims=True)
    g_idx = jnp.min(jnp.where(gl == gmax, lane_f, far), axis=-1, keepdims=True)
    p_group = 1.0 / jnp.sum(jnp.where(is_g, jnp.exp(gl - gmax), 0.0), axis=-1, keepdims=True)
    e_lane = lane - N_GROUPS
    in_group = (e_lane >= 0) & (e_lane < N_EXPERTS) & ((e_lane >> (EXPERTS_PER_GROUP.bit_length() - 1)).astype(F32) == g_idx)
    ev = jnp.where(in_group, logits, NEG_BIG)
    top1 = jnp.max(ev, axis=-1, keepdims=True)
    i1 = jnp.min(jnp.where(in_group & (ev == top1), lane_f, far), axis=-1, keepdims=True)
    rest = in_group & (lane_f != i1)
    ev2 = jnp.where(rest, logits, NEG_BIG)
    top2 = jnp.max(ev2, axis=-1, keepdims=True)
    i2 = jnp.min(jnp.where(rest & (ev2 == top2), lane_f, far), axis=-1, keepdims=True)
    e1 = i1 - float(N_GROUPS)
    e2 = i2 - float(N_GROUPS)
    ex = jnp.exp(top2 - top1)
    w1 = p_group / (1.0 + ex)
    w2 = p_group * ex / (1.0 + ex)

    sel1 = lane_f == e1
    sel2 = lane_f == e2
    onehot = jnp.where(sel1 | sel2, 1.0, 0.0)
    tr = lax.broadcasted_iota(jnp.int32, (tm, tm), 0)
    tc = lax.broadcasted_iota(jnp.int32, (tm, tm), 1)
    before = jnp.where(tc < tr, 1.0, 0.0).astype(BF16)
    cnt = _dot(before, onehot.astype(BF16)) + carry_ref[0:1]
    rank1 = jnp.sum(jnp.where(sel1, cnt, 0.0), axis=-1, keepdims=True)
    rank2 = jnp.sum(jnp.where(sel2, cnt, 0.0), axis=-1, keepdims=True)
    total = carry_ref[0:1] + jnp.sum(onehot, axis=0, keepdims=True)
    carry_ref[...] = jnp.broadcast_to(total, carry_ref.shape)
    cnt_ref[...] = jnp.broadcast_to(total, cnt_ref.shape)

    slab = jnp.where(lane == 0, e1, 0.0)
    slab = jnp.where(lane == 1, e2, slab)
    slab = jnp.where(lane == 2, w1, slab)
    slab = jnp.where(lane == 3, w2, slab)
    slab = jnp.where(lane == 4, rank1, slab)
    slab = jnp.where(lane == 5, rank2, slab)
    slab_ref[0] = slab


def _post_mixer(lru_f, lru_b, proj, hg_f, hg_b, x, nlw, nhw, wo_bf16, g_mix, nfw, sc_ffn, sh_ffn, wr_bf16, br,
                *, tm=256):
    bsz, seq, d = x.shape
    d_lru = lru_f.shape[-1]
    d_hgrn = hg_f.shape[-1]
    y_col = 1
    g_col = (2 * d_lru) // d_hgrn + 4
    row = lambda w: pl.BlockSpec((1, tm, w), lambda b, i: (b, i, 0))
    vec = lambda w: pl.BlockSpec((1, w), lambda b, i: (0, 0))
    per_b = lambda: pl.BlockSpec((1, 1, d), lambda b, i: (b, 0, 0))
    kern = functools.partial(_post_kernel, tm=tm, d_lru=d_lru)
    return pl.pallas_call(
        kern,
        out_shape=(
            jax.ShapeDtypeStruct((bsz, seq, d), F32),
            jax.ShapeDtypeStruct((bsz, seq, d), F32),
            jax.ShapeDtypeStruct((bsz, seq, ROUTE_LANES), F32),
            jax.ShapeDtypeStruct((SUBLANES, ROUTE_LANES), F32),
        ),
        grid=(bsz, seq // tm),
        in_specs=[
            row(d_lru), row(d_lru),
            pl.BlockSpec((1, tm, d_lru), lambda b, i: (b, i, y_col)),
            row(d_hgrn), row(d_hgrn),
            pl.BlockSpec((1, tm, d_hgrn), lambda b, i: (b, i, g_col)),
            row(d), vec(d_lru), vec(d_hgrn),
            pl.BlockSpec((d, d), lambda b, i: (0, 0)),
            per_b(), vec(d), per_b(), per_b(),
            pl.BlockSpec((d, ROUTE_LANES), lambda b, i: (0, 0)),
            vec(ROUTE_LANES),
        ],
        out_specs=(
            row(d), row(d), row(ROUTE_LANES),
            pl.BlockSpec((SUBLANES, ROUTE_LANES), lambda b, i: (0, 0)),
        ),
        scratch_shapes=[pltpu.VMEM((SUBLANES, ROUTE_LANES), F32)],
        compiler_params=_params(("arbitrary", "arbitrary")),
        name="post_mixer_router",
    )(lru_f, lru_b, proj, hg_f, hg_b, proj, x, nlw.reshape(1, d_lru), nhw.reshape(1, d_hgrn), wo_bf16,
      g_mix, nfw.reshape(1, d), sc_ffn, sh_ffn, wr_bf16, br)


def _dispatch_kernel(d1_ref, d2_ref, h_ref, z_ref, o_ref, sem, *, tb):
    del z_ref
    base = pl.program_id(0) * tb

    def issue(r, carry):
        t = base + r
        pltpu.make_async_copy(h_ref.at[pl.ds(r, 1)], o_ref.at[pl.ds(d1_ref[t], 1)], sem).start()
        pltpu.make_async_copy(h_ref.at[pl.ds(r, 1)], o_ref.at[pl.ds(d2_ref[t], 1)], sem).start()
        return carry

    lax.fori_loop(0, tb, issue, 0, unroll=DMA_ISSUE_UNROLL)
    for _ in range(2):
        pltpu.make_async_copy(h_ref, o_ref.at[pl.ds(0, tb)], sem).wait()


def _dispatch(dest1, dest2, h_flat, n_rows, *, tb=256):
    m, d = h_flat.shape
    kern = functools.partial(_dispatch_kernel, tb=tb)
    return pl.pallas_call(
        kern,
        out_shape=jax.ShapeDtypeStruct((n_rows, d), h_flat.dtype),
        grid_spec=pltpu.PrefetchScalarGridSpec(
            num_scalar_prefetch=2,
            grid=(m // tb,),
            in_specs=[pl.BlockSpec((tb, d), lambda i, d1, d2: (i, 0)), pl.BlockSpec(memory_space=pl.ANY)],
            out_specs=pl.BlockSpec(memory_space=pl.ANY),
            scratch_shapes=[pltpu.SemaphoreType.DMA(())],
        ),
        input_output_aliases={3: 0},
        compiler_params=pltpu.CompilerParams(dimension_semantics=("arbitrary",), has_side_effects=True),
        name="moe_dispatch",
    )(dest1, dest2, h_flat, jnp.zeros((n_rows, d), h_flat.dtype))


def _expert_kernel(be_ref, x_ref, wg_ref, wu_ref, wd_ref, o_ref, wg_s, wu_s, wd_s):
    i = pl.program_id(0)
    prev_expert = be_ref[jnp.maximum(i - 1, 0)]

    @pl.when((i == 0) | (be_ref[i] != prev_expert))
    def _():
        wg_s[...] = wg_ref[0, 0].astype(BF16)
        wu_s[...] = wu_ref[0, 0].astype(BF16)
        wd_s[...] = wd_ref[0, 0].astype(BF16)

    x = x_ref[...].astype(BF16)
    gate = _dot(x, wg_s[...])
    up = _dot(x, wu_s[...])
    act = (gate * _sigmoid(gate)) * up
    o_ref[...] = _dot(act.astype(BF16), wd_s[...])


def _experts(blk_expert, x_buf, wg, wu, wd, layer):
    n_rows, d = x_buf.shape
    de = wg.shape[-1]
    blk = EXPERT_BLOCK
    return pl.pallas_call(
        _expert_kernel,
        out_shape=jax.ShapeDtypeStruct((n_rows, d), F32),
        grid_spec=pltpu.PrefetchScalarGridSpec(
            num_scalar_prefetch=1,
            grid=(n_rows // blk,),
            in_specs=[
                pl.BlockSpec((blk, d), lambda i, be: (i, 0)),
                pl.BlockSpec((1, 1, d, de), lambda i, be: (layer, be[i], 0, 0)),
                pl.BlockSpec((1, 1, d, de), lambda i, be: (layer, be[i], 0, 0)),
                pl.BlockSpec((1, 1, de, d), lambda i, be: (layer, be[i], 0, 0)),
            ],
            out_specs=pl.BlockSpec((blk, d), lambda i, be: (i, 0)),
            scratch_shapes=[pltpu.VMEM((d, de), BF16), pltpu.VMEM((d, de), BF16), pltpu.VMEM((de, d), BF16)],
        ),
        compiler_params=_params(("arbitrary",)),
        name="moe_experts",
    )(blk_expert, x_buf, wg, wu, wd)


def _combine_kernel(d1_ref, d2_ref, y_ref, slab_ref, x_ref, g_ref, nw_ref, o_ref, r1_ref, r2_ref, sem,
                    *, tm, tiles, final_norm):
    base = (pl.program_id(0) * tiles + pl.program_id(1)) * tm

    def issue(r, carry):
        t = base + r
        pltpu.make_async_copy(y_ref.at[pl.ds(d1_ref[t], 1)], r1_ref.at[pl.ds(r, 1)], sem).start()
        pltpu.make_async_copy(y_ref.at[pl.ds(d2_ref[t], 1)], r2_ref.at[pl.ds(r, 1)], sem).start()
        return carry

    lax.fori_loop(0, tm, issue, 0, unroll=DMA_ISSUE_UNROLL)
    pltpu.make_async_copy(y_ref.at[pl.ds(0, tm)], r1_ref, sem).wait()
    pltpu.make_async_copy(y_ref.at[pl.ds(0, tm)], r2_ref, sem).wait()
    slab = slab_ref[0]
    y = slab[:, 2:3] * r1_ref[...] + slab[:, 3:4] * r2_ref[...]
    out = x_ref[0] + g_ref[0] * y
    if final_norm:
        ms = jnp.mean(out * out, axis=-1, keepdims=True)
        out = out * lax.rsqrt(ms + NORM_EPS) * nw_ref[...]
    o_ref[0] = out


def _combine(dest1, dest2, y_buf, slab, x, g_ffn, norm_w, *, final_norm, tm=256):
    bsz, seq, d = x.shape
    tiles = seq // tm
    kern = functools.partial(_combine_kernel, tm=tm, tiles=tiles, final_norm=final_norm)
    return pl.pallas_call(
        kern,
        out_shape=jax.ShapeDtypeStruct((bsz, seq, d), F32),
        grid_spec=pltpu.PrefetchScalarGridSpec(
            num_scalar_prefetch=2,
            grid=(bsz, tiles),
            in_specs=[
                pl.BlockSpec(memory_space=pl.ANY),
                pl.BlockSpec((1, tm, ROUTE_LANES), lambda b, i, d1, d2: (b, i, 0)),
                pl.BlockSpec((1, tm, d), lambda b, i, d1, d2: (b, i, 0)),
                pl.BlockSpec((1, 1, d), lambda b, i, d1, d2: (b, 0, 0)),
                pl.BlockSpec((1, d), lambda b, i, d1, d2: (0, 0)),
            ],
            out_specs=pl.BlockSpec((1, tm, d), lambda b, i, d1, d2: (b, i, 0)),
            scratch_shapes=[pltpu.VMEM((tm, d), F32), pltpu.VMEM((tm, d), F32), pltpu.SemaphoreType.DMA(())],
        ),
        compiler_params=_params(("arbitrary", "arbitrary")),
        name="moe_combine",
    )(dest1, dest2, y_buf, slab, x, g_ffn, norm_w.reshape(1, d))


def _block_diag(w):
    heads, hd, _ = w.shape
    eye = jnp.eye(heads, dtype=w.dtype)
    return (w[:, :, None, :] * eye[:, None, :, None]).reshape(heads * hd, heads * hd)


def kernel(x, c, ada_w, ada_b, norm_mix_w, w_in, conv_w, conv_b, lru_wa, lru_ba, lru_wx, lru_bx, lru_lambda, norm_lru_w, hgrn_lb, norm_hgrn_w, w_out, norm_ffn_w, router_group_w, router_group_b, router_expert_w, router_expert_b, expert_w_gate, expert_w_up, expert_w_down, final_norm_w):
    bsz, seq, d = x.shape
    depth = ada_w.shape[0]
    d_lru = conv_w.shape[-1]
    d_hgrn = hgrn_lb.shape[-1]
    m = bsz * seq
    n_rows = m * 2 + N_EXPERTS * EXPERT_BLOCK
    n_blocks = n_rows // EXPERT_BLOCK

    mod = _modulation(c, ada_w, ada_b)
    lb_cum = jnp.cumsum(jax.nn.softmax(hgrn_lb.astype(F32), axis=0), axis=0)
    lb_all = lb_cum - lb_cum[0:1]

    for l in range(depth):
        sh_mix, sc_mix, g_mix, sh_ffn, sc_ffn, g_ffn = [
            mod[l, :, i * d:(i + 1) * d].reshape(bsz, 1, d) for i in range(6)]
        proj = _in_proj(x, norm_mix_w[l], sc_mix, sh_mix, w_in[l].astype(BF16))
        wa_bd = jnp.stack([_block_diag(lru_wa[l, 0]), _block_diag(lru_wa[l, 1])]).astype(BF16)
        wx_bd = jnp.stack([_block_diag(lru_wx[l, 0]), _block_diag(lru_wx[l, 1])]).astype(BF16)
        lru = [
            _lru_scan(proj, conv_w[l], conv_b[l], wa_bd, lru_ba[l], wx_bd, lru_bx[l], lru_lambda[l],
                      reverse=rv)
            for rv in (False, True)]
        hg_f, hg_b = _hgrn(proj, lb_all[l], d_lru=d_lru, d_hgrn=d_hgrn)

        wr = jnp.zeros((d, ROUTE_LANES), F32)
        wr = wr.at[:, :N_GROUPS].set(router_group_w[l]).at[:, N_GROUPS:N_GROUPS + N_EXPERTS].set(router_expert_w[l])
        br = jnp.zeros((1, ROUTE_LANES), F32)
        br = br.at[0, :N_GROUPS].set(router_group_b[l]).at[0, N_GROUPS:N_GROUPS + N_EXPERTS].set(router_expert_b[l])
        x_mid, h_ffn, slab, counts = _post_mixer(
            lru[0], lru[1], proj, hg_f, hg_b, x, norm_lru_w[l], norm_hgrn_w[l], w_out[l].astype(BF16), g_mix,
            norm_ffn_w[l], sc_ffn, sh_ffn, wr.astype(BF16), br)

        cnt = counts[0, :N_EXPERTS].astype(jnp.int32)
        padded = ((cnt + EXPERT_BLOCK - 1) // EXPERT_BLOCK) * EXPERT_BLOCK
        pend = jnp.cumsum(padded)
        pstart = pend - padded
        blk_start = jnp.arange(n_blocks, dtype=jnp.int32) * EXPERT_BLOCK
        blk_expert = jnp.minimum(jnp.sum(pend[None, :] <= blk_start[:, None], axis=1), N_EXPERTS - 1)
        blk_expert = blk_expert.astype(jnp.int32)
        slab_flat = slab.reshape(m, ROUTE_LANES)
        e1 = slab_flat[:, 0].astype(jnp.int32)
        e2 = slab_flat[:, 1].astype(jnp.int32)
        dest1 = pstart[e1] + slab_flat[:, 4].astype(jnp.int32)
        dest2 = pstart[e2] + slab_flat[:, 5].astype(jnp.int32)

        x_buf = _dispatch(dest1, dest2, h_ffn.reshape(m, d), n_rows)
        y_buf = _experts(blk_expert, x_buf, expert_w_gate, expert_w_up, expert_w_down, l)
        x = _combine(dest1, dest2, y_buf, slab, x_mid, g_ffn, final_norm_w, final_norm=(l == depth - 1))

    return x
```

```python
import functools

import jax
import jax.numpy as jnp
from jax import lax
from jax.experimental import pallas as pl
from jax.experimental.pallas import tpu as pltpu

F32 = jnp.float32
BF16 = jnp.bfloat16

LRU_HEADS = 8
HGRN_HEADS = 8
CONV_WIDTH = 4
LRU_C = 8.0
N_GROUPS = 4
EXPERTS_PER_GROUP = 8
N_EXPERTS = N_GROUPS * EXPERTS_PER_GROUP
NORM_EPS = 1e-6

LANES = 128
SUBLANES = 8
VMEM_LIMIT = 56 * 1024 * 1024

HGRN_CHUNK = 64
HGRN_SUB = 8
LOG2E = 1.4426950408889634
ROUTE_LANES = LANES
EXPERT_BLOCK = 256
DMA_ISSUE_UNROLL = 8
NEG_BIG = -3.0e38


def _params(sem):
    return pltpu.CompilerParams(dimension_semantics=sem, vmem_limit_bytes=VMEM_LIMIT)


def _dot(a, b):
    return jnp.dot(a, b, preferred_element_type=F32)


def _dot_nt(a, b):
    return lax.dot_general(a, b, (((1,), (1,)), ((), ())), preferred_element_type=F32)


def _dot_tn(a, b):
    return lax.dot_general(a, b, (((0,), (0,)), ((), ())), preferred_element_type=F32)


def _dot01_exact(m01, x):
    hi = x.astype(BF16)
    r1 = x - hi.astype(F32)
    mid = r1.astype(BF16)
    lo = (r1 - mid.astype(F32)).astype(BF16)
    return _dot(m01, hi) + _dot(m01, mid) + _dot(m01, lo)


def _sigmoid(x):
    return 1.0 / (1.0 + jnp.exp(-x))


def _mod_kernel(c_ref, w_ref, b_ref, o_ref):
    c = c_ref[...]
    cond = c * _sigmoid(c)
    o_ref[0] = _dot(cond.astype(BF16), w_ref[0].astype(BF16)) + b_ref[0]


def _modulation(c, ada_w, ada_b):
    depth, d, n = ada_w.shape
    bsz = c.shape[0]
    rows = -(-bsz // SUBLANES) * SUBLANES
    c_pad = jnp.pad(c, ((0, rows - bsz), (0, 0)))
    tn = n // 6
    out = pl.pallas_call(
        _mod_kernel,
        out_shape=jax.ShapeDtypeStruct((depth, rows, n), F32),
        grid=(depth, n // tn),
        in_specs=[
            pl.BlockSpec((rows, d), lambda l, j: (0, 0)),
            pl.BlockSpec((1, d, tn), lambda l, j: (l, 0, j)),
            pl.BlockSpec((1, 1, tn), lambda l, j: (l, 0, j)),
        ],
        out_specs=pl.BlockSpec((1, rows, tn), lambda l, j: (l, 0, j)),
        compiler_params=_params(("arbitrary", "arbitrary")),
        name="adaln_mod",
    )(c_pad, ada_w, ada_b.reshape(depth, 1, n))
    return out[:, :bsz]


def _rms_mod(x, nw, sc, sh):
    ms = jnp.mean(x * x, axis=-1, keepdims=True)
    return (x * lax.rsqrt(ms + NORM_EPS) * nw) * (1.0 + sc) + sh


def _inproj_kernel(x_ref, nw_ref, sc_ref, sh_ref, w_ref, o_ref):
    h = _rms_mod(x_ref[0], nw_ref[...], sc_ref[0], sh_ref[0])
    o_ref[0] = _dot(h.astype(BF16), w_ref[...])


def _in_proj(x, nw, sc, sh, w_bf16, tm=512):
    bsz, seq, d = x.shape
    n = w_bf16.shape[1]
    return pl.pallas_call(
        _inproj_kernel,
        out_shape=jax.ShapeDtypeStruct((bsz, seq, n), F32),
        grid=(bsz, seq // tm),
        in_specs=[
            pl.BlockSpec((1, tm, d), lambda b, i: (b, i, 0)),
            pl.BlockSpec((1, d), lambda b, i: (0, 0)),
            pl.BlockSpec((1, 1, d), lambda b, i: (b, 0, 0)),
            pl.BlockSpec((1, 1, d), lambda b, i: (b, 0, 0)),
            pl.BlockSpec((d, n), lambda b, i: (0, 0)),
        ],
        out_specs=pl.BlockSpec((1, tm, n), lambda b, i: (b, i, 0)),
        compiler_params=_params(("arbitrary", "arbitrary")),
        name="in_proj",
    )(x, nw.reshape(1, d), sc, sh, w_bf16)


def _lru_kernel(x_ref, xp_ref, xn_ref, cw_ref, cb_ref, wa_ref, ba_ref, wx_ref, bx_ref, lam_ref,
                o_ref, carry_ref, sa_ref, sb_ref, cin_ref, *, reverse, n_chunks, rows):
    c = pl.program_id(1)
    chunk = (n_chunks - 1 - c) if reverse else c

    @pl.when(c == 0)
    def _():
        carry_ref[...] = jnp.zeros_like(carry_ref)

    x = x_ref[0]
    width = x.shape[1]
    row = lax.broadcasted_iota(jnp.int32, (rows, width), 0)
    has_prev = jnp.where(chunk > 0, 1.0, 0.0)
    has_next = jnp.where(chunk < n_chunks - 1, 1.0, 0.0)
    xp = xp_ref[0] * has_prev
    xn = xn_ref[0] * has_next
    xe = jnp.concatenate([xp, x, xn], axis=0)
    h8 = SUBLANES
    cw = cw_ref[...]
    xc = (cw[0:1] * xe[h8 - 2:h8 - 2 + rows] + cw[1:2] * xe[h8 - 1:h8 - 1 + rows] + cw[2:3] * x
          + cw[3:4] * xe[h8 + 1:h8 + 1 + rows] + cb_ref[...])

    xcb = xc.astype(BF16)
    r = _sigmoid(_dot(xcb, wa_ref[0]) + ba_ref[0])
    gate_i = _sigmoid(_dot(xcb, wx_ref[0]) + bx_ref[0])
    lam = lam_ref[0]
    softplus_neg_lam = jnp.maximum(-lam, 0.0) + jnp.log1p(jnp.exp(-jnp.abs(lam)))
    log_a = (-LRU_C) * r * softplus_neg_lam
    a = jnp.exp(log_a)
    t = jnp.tanh(-log_a)
    u = jnp.sqrt(2.0 * t / (1.0 + t)) * (gate_i * xc)

    groups = rows // SUBLANES
    acc_a = a.reshape(groups, SUBLANES, width)
    acc_b = u.reshape(groups, SUBLANES, width)
    sub = lax.broadcasted_iota(jnp.int32, (groups, SUBLANES, width), 1)
    s = 1
    while s < SUBLANES:
        if reverse:
            valid = sub < SUBLANES - s
            sh_a, sh_b = pltpu.roll(acc_a, SUBLANES - s, 1), pltpu.roll(acc_b, SUBLANES - s, 1)
        else:
            valid = sub >= s
            sh_a, sh_b = pltpu.roll(acc_a, s, 1), pltpu.roll(acc_b, s, 1)
        acc_b = jnp.where(valid, acc_a * sh_b + acc_b, acc_b)
        acc_a = jnp.where(valid, acc_a * sh_a, acc_a)
        s *= 2
    acc_a = acc_a.reshape(rows, width)
    acc_b = acc_b.reshape(rows, width)
    groups = rows // SUBLANES
    edge = 0 if reverse else SUBLANES - 1
    n_tiles = width // LANES
    for j in range(n_tiles):
        sa_ref[j] = acc_a[:, j * LANES:(j + 1) * LANES]
        sb_ref[j] = acc_b[:, j * LANES:(j + 1) * LANES]
    ea = jnp.concatenate([sa_ref[j, pl.ds(edge, groups, stride=SUBLANES), :] for j in range(n_tiles)], axis=1)
    eb = jnp.concatenate([sb_ref[j, pl.ds(edge, groups, stride=SUBLANES), :] for j in range(n_tiles)], axis=1)
    grow = lax.broadcasted_iota(jnp.int32, (groups, width), 0)
    s = 1
    while s < groups:
        if reverse:
            valid = grow < groups - s
            sh_a, sh_b = pltpu.roll(ea, groups - s, 0), pltpu.roll(eb, groups - s, 0)
        else:
            valid = grow >= s
            sh_a, sh_b = pltpu.roll(ea, s, 0), pltpu.roll(eb, s, 0)
        eb = jnp.where(valid, ea * sh_b + eb, eb)
        ea = jnp.where(valid, ea * sh_a, ea)
        s *= 2
    carry = carry_ref[...]
    group_out = eb + ea * carry
    if reverse:
        carry_in = jnp.where(grow == groups - 1, carry, pltpu.roll(group_out, groups - 1, 0))
        carry_ref[...] = group_out[0:1]
    else:
        carry_in = jnp.where(grow == 0, carry, pltpu.roll(group_out, 1, 0))
        carry_ref[...] = group_out[groups - 1:groups]
    cin_ref[...] = carry_in
    for g in range(groups):
        rs = slice(g * SUBLANES, (g + 1) * SUBLANES)
        o_ref[0, rs, :] = acc_b[rs] + acc_a[rs] * cin_ref[g:g + 1, :]


def _lru_scan(proj, conv_w, conv_b, wa_bd, ba, wx_bd, bx, lam, *, reverse, rows=256):
    bsz, seq, _ = proj.shape
    d_lru = conv_w.shape[1]
    n_chunks = seq // rows
    halo = rows // SUBLANES
    last_halo = seq // SUBLANES - 1
    dirn = 1 if reverse else 0

    def chunk_of(c):
        return (n_chunks - 1 - c) if reverse else c

    vec = lambda: pl.BlockSpec((1, 1, d_lru), lambda b, c: (dirn, 0, 0))
    mat = lambda: pl.BlockSpec((1, d_lru, d_lru), lambda b, c: (dirn, 0, 0))
    kern = functools.partial(_lru_kernel, reverse=reverse, n_chunks=n_chunks, rows=rows)
    return pl.pallas_call(
        kern,
        out_shape=jax.ShapeDtypeStruct((bsz, seq, d_lru), F32),
        grid=(bsz, n_chunks),
        in_specs=[
            pl.BlockSpec((1, rows, d_lru), lambda b, c: (b, chunk_of(c), 0)),
            pl.BlockSpec((1, SUBLANES, d_lru),
                         lambda b, c: (b, jnp.maximum(chunk_of(c) * halo - 1, 0), 0)),
            pl.BlockSpec((1, SUBLANES, d_lru),
                         lambda b, c: (b, jnp.minimum((chunk_of(c) + 1) * halo, last_halo), 0)),
            pl.BlockSpec((CONV_WIDTH, d_lru), lambda b, c: (0, 0)),
            pl.BlockSpec((1, d_lru), lambda b, c: (0, 0)),
            mat(), vec(), mat(), vec(), vec(),
        ],
        out_specs=pl.BlockSpec((1, rows, d_lru), lambda b, c: (b, chunk_of(c), 0)),
        scratch_shapes=[pltpu.VMEM((1, d_lru), F32), pltpu.VMEM((d_lru // LANES, rows, LANES), F32),
                        pltpu.VMEM((d_lru // LANES, rows, LANES), F32),
                        pltpu.VMEM((rows // SUBLANES, d_lru), F32)],
        compiler_params=_params(("arbitrary", "arbitrary")),
        name="lru_bwd" if reverse else "lru_fwd",
    )(proj, proj, proj, conv_w, conv_b.reshape(1, d_lru), wa_bd, ba.reshape(2, 1, d_lru),
      wx_bd, bx.reshape(2, 1, d_lru), lam.reshape(2, 1, d_lru))


def _hgrn_direction(rev, q_ref, f_ref, v_ref, lb_ref, o_ref, st_ref, diag_s, lvl_s, upd_s, qe_s, btot_s, *, rows):
    ck, sb = HGRN_CHUNK, HGRN_SUB
    n_blk = ck // sb
    sb_shift = sb.bit_length() - 1
    n_sub = rows // ck
    width = q_ref.shape[-1]
    n_pairs = width // LANES
    half = LANES // 2

    def flip(idx, n):
        return (n - 1 - idx) if rev else idx

    n_lvl = n_blk.bit_length() - 1
    tf = flip(lax.broadcasted_iota(jnp.int32, (ck, ck), 0), ck)
    uf = flip(lax.broadcasted_iota(jnp.int32, (ck, ck), 1), ck)
    tb, ub = tf >> sb_shift, uf >> sb_shift
    pb = flip(lax.broadcasted_iota(jnp.int32, (n_blk, ck), 0), n_blk)
    pub = flip(lax.broadcasted_iota(jnp.int32, (n_blk, ck), 1), ck) >> sb_shift
    mats = [jnp.where((tb == ub) & (uf <= tf), 1.0, 0.0),
            jnp.where(pub < pb, 1.0, 0.0)]
    for lvl in range(n_lvl):
        mid = ((pb >> (lvl + 1)) << (lvl + 1)) + (1 << lvl)
        mats.append(jnp.where(pub < mid, 1.0, 0.0))
    mats.append(jnp.ones((SUBLANES, ck), F32))
    m_cum = jnp.concatenate(mats, axis=0).astype(BF16)

    def per_block(rows8):
        return jnp.concatenate(
            [jnp.broadcast_to(rows8[jb:jb + 1], (sb, rows8.shape[1])) for jb in range(n_blk)], axis=0)
    row_blk = flip(lax.broadcasted_iota(jnp.int32, (ck, width), 0), ck) >> sb_shift
    upper = [((row_blk >> lvl) & 1) == 1 for lvl in range(n_lvl)]
    pr = flip(lax.broadcasted_iota(jnp.int32, (ck, LANES), 0), ck) >> sb_shift
    pc = flip(lax.broadcasted_iota(jnp.int32, (ck, LANES), 1) & (ck - 1), ck) >> sb_shift
    group_mask = [(pr >> (lvl + 1)) == (pc >> (lvl + 1)) for lvl in range(n_lvl)]
    lane = lax.broadcasted_iota(jnp.int32, (1, LANES), 1)
    head0 = lane < half
    sr = lax.broadcasted_iota(jnp.int32, (LANES, LANES), 0)
    sc = lax.broadcasted_iota(jnp.int32, (LANES, LANES), 1)
    same_head = (sr < half) == (sc < half)
    er = lax.broadcasted_iota(jnp.int32, (sb * LANES, LANES), 0)
    ec = lax.broadcasted_iota(jnp.int32, (sb * LANES, LANES), 1)
    sel = jnp.where(ec == (((er & (LANES - 1)) >> (half.bit_length() - 1)) * half + (er >> (LANES.bit_length() - 1))),
                    1.0, 0.0).astype(BF16)
    sub_row = flip(lax.broadcasted_iota(jnp.int32, (sb, LANES), 0), sb)
    lbv = lb_ref[...]

    def row_start(j):
        return pl.multiple_of(flip(j, n_sub) * ck, ck)

    def stage1a(j):
        r0 = row_start(j)
        q = q_ref[0, pl.ds(r0, ck), :]
        z = f_ref[0, pl.ds(r0, ck), :]
        v = v_ref[0, pl.ds(r0, ck), :]
        f = lbv + (1.0 - lbv) * _sigmoid(z)
        lf = jnp.log(f)
        k = 1.0 - f
        return q, v, k, _dot01_exact(m_cum, lf)

    def stage1b(q, v, k, cums):
        bl = cums[0:ck]
        b = bl + per_block(cums[ck:ck + n_blk])
        tot_row = ck + (1 + n_lvl) * n_blk
        btot = cums[tot_row:tot_row + 1]
        qe = q * jnp.exp(b)
        ke = k * jnp.exp(btot - b)
        log2_k = jnp.log(k) * LOG2E
        b2 = b * LOG2E
        kb = b2 - log2_k
        bl2 = bl * LOG2E
        kbl = bl2 - log2_k
        q_lvl, k_lvl = [], []
        for lvl in range(n_lvl):
            split2 = per_block(cums[ck + (1 + lvl) * n_blk:ck + (2 + lvl) * n_blk] * LOG2E)
            q_lvl.append(q * jnp.exp2(jnp.where(upper[lvl], b2 - split2, NEG_BIG)))
            k_lvl.append(jnp.exp2(jnp.where(upper[lvl], NEG_BIG, split2 - kb)))

        qe_s[...] = qe.astype(BF16)
        btot_s[...] = btot
        for p in range(n_pairs):
            sl = slice(p * LANES, (p + 1) * LANES)
            diag_rows = []
            for jb in range(n_blk):
                rs = slice(jb * sb, (jb + 1) * sb)
                bl_b, kbl_b, q_b = bl2[rs, sl], kbl[rs, sl], q[rs, sl]
                terms = []
                for s in range(sb):
                    arg = jnp.where(sub_row >= flip(s, sb), bl_b - kbl_b[s:s + 1], NEG_BIG)
                    terms.append(q_b * jnp.exp2(arg))
                diag_rows.append(jnp.concatenate(terms, axis=1))
            diag_s[p] = _dot(jnp.concatenate(diag_rows, axis=0).astype(BF16), sel)
            for lvl in range(n_lvl):
                k_p = k_lvl[lvl][:, sl]
                k_heads = jnp.concatenate([jnp.where(head0, k_p, 0.0), jnp.where(head0, 0.0, k_p)], axis=0)
                lvl_s[p * n_lvl + lvl] = _dot_nt(q_lvl[lvl][:, sl].astype(BF16), k_heads.astype(BF16))
            upd_s[p] = _dot_tn(v[:, sl].astype(BF16), ke[:, sl].astype(BF16))

    def stage2_issue(j):
        r0 = row_start(j)
        v = v_ref[0, pl.ds(r0, ck), :]
        out = []
        for p in range(n_pairs):
            sl = slice(p * LANES, (p + 1) * LANES)
            parts = []
            for jb in range(n_blk):
                blk = diag_s[p, jb * sb:(jb + 1) * sb, :]
                parts.append(pltpu.roll(blk, jb * sb, 1) if jb else blk)
            scores = jnp.concatenate(parts, axis=0)
            for lvl in range(n_lvl):
                scores = scores + jnp.where(group_mask[lvl], lvl_s[p * n_lvl + lvl], 0.0)
            v_p = v[:, sl]
            v_heads = jnp.concatenate([jnp.where(head0, v_p, 0.0), jnp.where(head0, 0.0, v_p)], axis=0)
            intra = _dot(scores.astype(BF16), v_heads.astype(BF16))
            st = st_ref[p]
            inter = _dot_nt(qe_s[:, sl], st.astype(BF16))
            new_st = jnp.where(same_head, st * jnp.exp(btot_s[:, sl]) + upd_s[p], 0.0)
            out.append((inter + intra, new_st))
        return r0, out

    def stage2_finish(r0, out):
        for p in range(n_pairs):
            o_ref[0, pl.ds(r0, ck), p * LANES:(p + 1) * LANES] = out[p][0]
            st_ref[p] = out[p][1]

    return stage1a, stage1b, stage2_issue, stage2_finish


N_HGRN_SCRATCH = 6


def _hgrn_kernel(qf_ref, ff_ref, vf_ref, qb_ref, fb_ref, vb_ref, lb_ref, of_ref, ob_ref, *scratch, rows):
    fwd_scratch, bwd_scratch = scratch[:N_HGRN_SCRATCH], scratch[N_HGRN_SCRATCH:]

    @pl.when(pl.program_id(1) == 0)
    def _():
        fwd_scratch[0][...] = jnp.zeros_like(fwd_scratch[0])
        bwd_scratch[0][...] = jnp.zeros_like(bwd_scratch[0])

    f1a, f1b, f2, f3 = _hgrn_direction(False, qf_ref, ff_ref, vf_ref, lb_ref, of_ref, *fwd_scratch, rows=rows)
    b1a, b1b, b2, b3 = _hgrn_direction(True, qb_ref, fb_ref, vb_ref, lb_ref, ob_ref, *bwd_scratch, rows=rows)
    n_sub = rows // HGRN_CHUNK

    def stage1_both(j):
        fa = f1a(j)
        ba = b1a(j)
        f1b(*fa)
        b1b(*ba)

    stage1_both(0)

    def pipelined(j, carry):
        fo = f2(j)
        bo = b2(j)
        stage1_both(j + 1)
        f3(*fo)
        b3(*bo)
        return carry

    lax.fori_loop(0, n_sub - 1, pipelined, 0)
    fo = f2(n_sub - 1)
    bo = b2(n_sub - 1)
    f3(*fo)
    b3(*bo)


def _hgrn(proj, lb, *, d_lru, d_hgrn, rows=512):
    bsz, seq, _ = proj.shape
    n_chunks = seq // rows
    col0 = (2 * d_lru) // d_hgrn
    n_pairs = d_hgrn // LANES
    n_lvl = (HGRN_CHUNK // HGRN_SUB).bit_length() - 1
    fwd = lambda col: pl.BlockSpec((1, rows, d_hgrn), lambda b, c: (b, c, col))
    bwd = lambda col: pl.BlockSpec((1, rows, d_hgrn), lambda b, c: (b, n_chunks - 1 - c, col))
    direction_scratch = [
        pltpu.VMEM((n_pairs, LANES, LANES), F32),
        pltpu.VMEM((n_pairs, HGRN_CHUNK, LANES), F32),
        pltpu.VMEM((n_pairs * n_lvl, HGRN_CHUNK, LANES), F32),
        pltpu.VMEM((n_pairs, LANES, LANES), F32),
        pltpu.VMEM((HGRN_CHUNK, d_hgrn), BF16),
        pltpu.VMEM((1, d_hgrn), F32),
    ]
    assert len(direction_scratch) == N_HGRN_SCRATCH
    kern = functools.partial(_hgrn_kernel, rows=rows)
    out = jax.ShapeDtypeStruct((bsz, seq, d_hgrn), F32)
    return pl.pallas_call(
        kern,
        out_shape=(out, out),
        grid=(bsz, n_chunks),
        in_specs=[fwd(col0), fwd(col0 + 1), fwd(col0 + 3), bwd(col0), bwd(col0 + 2), bwd(col0 + 3),
                  pl.BlockSpec((1, d_hgrn), lambda b, c: (0, 0))],
        out_specs=(pl.BlockSpec((1, rows, d_hgrn), lambda b, c: (b, c, 0)),
                   pl.BlockSpec((1, rows, d_hgrn), lambda b, c: (b, n_chunks - 1 - c, 0))),
        scratch_shapes=direction_scratch + direction_scratch,
        compiler_params=_params(("arbitrary", "arbitrary")),
        name="hgrn2",
    )(proj, proj, proj, proj, proj, proj, lb.reshape(1, d_hgrn))


def _gelu_tanh(y):
    return 0.5 * y * (1.0 + jnp.tanh(0.7978845608028654 * (y + 0.044715 * (y * y * y))))


def _post_kernel(lf_ref, lb_ref, y_ref, of_ref, ob_ref, g_ref, x_ref, nlw_ref, nhw_ref, wo_ref,
                 gm_ref, nfw_ref, scf_ref, shf_ref, wr_ref, br_ref,
                 xo_ref, h_ref, slab_ref, cnt_ref, carry_ref, *, tm, d_lru):
    first = (pl.program_id(0) == 0) & (pl.program_id(1) == 0)

    @pl.when(first)
    def _():
        carry_ref[...] = jnp.zeros_like(carry_ref)

    lru = (lf_ref[0] + lb_ref[0]) * _gelu_tanh(y_ref[0])
    ms = jnp.mean(lru * lru, axis=-1, keepdims=True)
    lru = lru * lax.rsqrt(ms + NORM_EPS) * nlw_ref[...]

    hg = of_ref[0] + ob_ref[0]
    width = hg.shape[1]
    hd = width // HGRN_HEADS
    hd_shift = hd.bit_length() - 1
    er = lax.broadcasted_iota(jnp.int32, (width, width), 0) >> hd_shift
    ec = lax.broadcasted_iota(jnp.int32, (width, width), 1) >> hd_shift
    head_sum = jnp.where(er == ec, 1.0, 0.0).astype(BF16)
    sq = hg * hg
    sq_hi = sq.astype(BF16)
    sq_lo = (sq - sq_hi.astype(F32)).astype(BF16)
    ms_h = (_dot(sq_hi, head_sum) + _dot(sq_lo, head_sum)) * (1.0 / hd)
    g = g_ref[0]
    hg = (hg * lax.rsqrt(ms_h + NORM_EPS) * nhw_ref[...]) * (g * _sigmoid(g))

    mixed = _dot(lru.astype(BF16), wo_ref[0:d_lru, :]) + _dot(hg.astype(BF16), wo_ref[d_lru:, :])
    x_new = x_ref[0] + gm_ref[0] * mixed
    xo_ref[0] = x_new

    h = _rms_mod(x_new, nfw_ref[...], scf_ref[0], shf_ref[0])
    _tiles_store(h_ref, h, tm, lead=(0,))

    logits = _dot(h.astype(BF16), wr_ref[...]) + br_ref[...]
    lane = lax.broadcasted_iota(jnp.int32, (tm, ROUTE_LANES), 1)
    lane_f = lane.astype(F32)
    far = float(ROUTE_LANES)
    is_g = lane < N_GROUPS
    gl = jnp.where(is_g, logits, NEG_BIG)
    gmax = jnp.max(gl, axis=-1, keepdims=True)
    g_idx = jnp.min(jnp.where(gl == gmax, lane_f, far), axis=-1, keepdims=True)
    p_group = 1.0 / jnp.sum(jnp.where(is_g, jnp.exp(gl - gmax), 0.0), axis=-1, keepdims=True)
    e_lane = lane - N_GROUPS
    in_group = (e_lane >= 0) & (e_lane < N_EXPERTS) & ((e_lane >> (EXPERTS_PER_GROUP.bit_length() - 1)).astype(F32) == g_idx)
    ev = jnp.where(in_group, logits, NEG_BIG)
    top1 = jnp.max(ev, axis=-1, keepdims=True)
    i1 = jnp.min(jnp.where(in_group & (ev == top1), lane_f, far), axis=-1, keepdims=True)
    rest = in_group & (lane_f != i1)
    ev2 = jnp.where(rest, logits, NEG_BIG)
    top2 = jnp.max(ev2, axis=-1, keepdims=True)
    i2 = jnp.min(jnp.where(rest & (ev2 == top2), lane_f, far), axis=-1, keepdims=True)
    e1 = i1 - float(N_GROUPS)
    e2 = i2 - float(N_GROUPS)
    ex = jnp.exp(top2 - top1)
    w1 = p_group / (1.0 + ex)
    w2 = p_group * ex / (1.0 + ex)

    sel1 = lane_f == e1
    sel2 = lane_f == e2
    onehot = jnp.where(sel1 | sel2, 1.0, 0.0)
    tr = lax.broadcasted_iota(jnp.int32, (tm, tm), 0)
    tc = lax.broadcasted_iota(jnp.int32, (tm, tm), 1)
    before = jnp.where(tc < tr, 1.0, 0.0).astype(BF16)
    cnt = _dot(before, onehot.astype(BF16)) + carry_ref[0:1]
    rank1 = jnp.sum(jnp.where(sel1, cnt, 0.0), axis=-1, keepdims=True)
    rank2 = jnp.sum(jnp.where(sel2, cnt, 0.0), axis=-1, keepdims=True)
    total = carry_ref[0:1] + jnp.sum(onehot, axis=0, keepdims=True)
    carry_ref[...] = jnp.broadcast_to(total, carry_ref.shape)
    cnt_ref[...] = jnp.broadcast_to(total, cnt_ref.shape)

    slab = jnp.where(lane == 0, e1, 0.0)
    slab = jnp.where(lane == 1, e2, slab)
    slab = jnp.where(lane == 2, w1, slab)
    slab = jnp.where(lane == 3, w2, slab)
    slab = jnp.where(lane == 4, rank1, slab)
    slab = jnp.where(lane == 5, rank2, slab)
    slab_ref[0] = slab


def _post_mixer(lru_f, lru_b, proj, hg_f, hg_b, x, nlw, nhw, wo_bf16, g_mix, nfw, sc_ffn, sh_ffn, wr_bf16, br,
                *, tm=256):
    bsz, seq, d = x.shape
    d_lru = lru_f.shape[-1]
    d_hgrn = hg_f.shape[-1]
    y_col = 1
    g_col = (2 * d_lru) // d_hgrn + 4
    row = lambda w: pl.BlockSpec((1, tm, w), lambda b, i: (b, i, 0))
    vec = lambda w: pl.BlockSpec((1, w), lambda b, i: (0, 0))
    per_b = lambda: pl.BlockSpec((1, 1, d), lambda b, i: (b, 0, 0))
    kern = functools.partial(_post_kernel, tm=tm, d_lru=d_lru)
    return pl.pallas_call(
        kern,
        out_shape=(
            jax.ShapeDtypeStruct((bsz, seq, d), F32),
            jax.ShapeDtypeStruct((bsz, seq * SUBLANES, LANES), F32),
            jax.ShapeDtypeStruct((bsz, seq, ROUTE_LANES), F32),
            jax.ShapeDtypeStruct((SUBLANES, ROUTE_LANES), F32),
        ),
        grid=(bsz, seq // tm),
        in_specs=[
            row(d_lru), row(d_lru),
            pl.BlockSpec((1, tm, d_lru), lambda b, i: (b, i, y_col)),
            row(d_hgrn), row(d_hgrn),
            pl.BlockSpec((1, tm, d_hgrn), lambda b, i: (b, i, g_col)),
            row(d), vec(d_lru), vec(d_hgrn),
            pl.BlockSpec((d, d), lambda b, i: (0, 0)),
            per_b(), vec(d), per_b(), per_b(),
            pl.BlockSpec((d, ROUTE_LANES), lambda b, i: (0, 0)),
            vec(ROUTE_LANES),
        ],
        out_specs=(
            row(d), pl.BlockSpec((1, tm * SUBLANES, LANES), lambda b, i: (b, i, 0)), row(ROUTE_LANES),
            pl.BlockSpec((SUBLANES, ROUTE_LANES), lambda b, i: (0, 0)),
        ),
        scratch_shapes=[pltpu.VMEM((SUBLANES, ROUTE_LANES), F32)],
        compiler_params=_params(("arbitrary", "arbitrary")),
        name="post_mixer_router",
    )(lru_f, lru_b, proj, hg_f, hg_b, proj, x, nlw.reshape(1, d_lru), nhw.reshape(1, d_hgrn), wo_bf16,
      g_mix, nfw.reshape(1, d), sc_ffn, sh_ffn, wr_bf16, br)


def _tiles_load(ref, n, lead=()):
    return jnp.concatenate(
        [ref[(*lead, pl.ds(j, n, stride=SUBLANES), slice(None))] for j in range(SUBLANES)], axis=1)


def _tiles_store(ref, val, n, lead=()):
    for j in range(SUBLANES):
        ref[(*lead, pl.ds(j, n, stride=SUBLANES), slice(None))] = val[:, j * LANES:(j + 1) * LANES]


def _token_tile(ref, t):
    return ref.at[pl.ds(pl.multiple_of(t * SUBLANES, SUBLANES), SUBLANES)]


def _dispatch_kernel(d1_ref, d2_ref, h_ref, z_ref, o_ref, sem, *, tb):
    del z_ref
    base = pl.program_id(0) * tb

    def issue(r, carry):
        t = base + r
        pltpu.make_async_copy(_token_tile(h_ref, r), _token_tile(o_ref, d1_ref[t]), sem).start()
        pltpu.make_async_copy(_token_tile(h_ref, r), _token_tile(o_ref, d2_ref[t]), sem).start()
        return carry

    lax.fori_loop(0, tb, issue, 0, unroll=DMA_ISSUE_UNROLL)
    for _ in range(2):
        pltpu.make_async_copy(h_ref, o_ref.at[pl.ds(0, tb * SUBLANES)], sem).wait()


def _dispatch(dest1, dest2, h_tiles, n_rows, *, tb=256):
    m = h_tiles.shape[0] // SUBLANES
    kern = functools.partial(_dispatch_kernel, tb=tb)
    return pl.pallas_call(
        kern,
        out_shape=jax.ShapeDtypeStruct((n_rows * SUBLANES, LANES), h_tiles.dtype),
        grid_spec=pltpu.PrefetchScalarGridSpec(
            num_scalar_prefetch=2,
            grid=(m // tb,),
            in_specs=[pl.BlockSpec((tb * SUBLANES, LANES), lambda i, d1, d2: (i, 0)),
                      pl.BlockSpec(memory_space=pl.ANY)],
            out_specs=pl.BlockSpec(memory_space=pl.ANY),
            scratch_shapes=[pltpu.SemaphoreType.DMA(())],
        ),
        input_output_aliases={3: 0},
        compiler_params=pltpu.CompilerParams(dimension_semantics=("arbitrary",), has_side_effects=True),
        name="moe_dispatch",
    )(dest1, dest2, h_tiles, jnp.zeros((n_rows * SUBLANES, LANES), h_tiles.dtype))


def _expert_kernel(be_ref, x_ref, wg_ref, wu_ref, wd_ref, o_ref, wg_s, wu_s, wd_s):
    i = pl.program_id(0)
    prev_expert = be_ref[jnp.maximum(i - 1, 0)]

    @pl.when((i == 0) | (be_ref[i] != prev_expert))
    def _():
        wg_s[...] = wg_ref[0, 0].astype(BF16)
        wu_s[...] = wu_ref[0, 0].astype(BF16)
        wd_s[...] = wd_ref[0, 0].astype(BF16)

    blk = x_ref.shape[0] // SUBLANES
    x = _tiles_load(x_ref, blk).astype(BF16)
    gate = _dot(x, wg_s[...])
    up = _dot(x, wu_s[...])
    act = (gate * _sigmoid(gate)) * up
    _tiles_store(o_ref, _dot(act.astype(BF16), wd_s[...]), blk)


def _experts(blk_expert, x_tiles, wg, wu, wd, layer):
    n_rows = x_tiles.shape[0] // SUBLANES
    d, de = wg.shape[-2:]
    blk = EXPERT_BLOCK
    return pl.pallas_call(
        _expert_kernel,
        out_shape=jax.ShapeDtypeStruct((n_rows * SUBLANES, LANES), F32),
        grid_spec=pltpu.PrefetchScalarGridSpec(
            num_scalar_prefetch=1,
            grid=(n_rows // blk,),
            in_specs=[
                pl.BlockSpec((blk * SUBLANES, LANES), lambda i, be: (i, 0)),
                pl.BlockSpec((1, 1, d, de), lambda i, be: (layer, be[i], 0, 0)),
                pl.BlockSpec((1, 1, d, de), lambda i, be: (layer, be[i], 0, 0)),
                pl.BlockSpec((1, 1, de, d), lambda i, be: (layer, be[i], 0, 0)),
            ],
            out_specs=pl.BlockSpec((blk * SUBLANES, LANES), lambda i, be: (i, 0)),
            scratch_shapes=[pltpu.VMEM((d, de), BF16), pltpu.VMEM((d, de), BF16), pltpu.VMEM((de, d), BF16)],
        ),
        compiler_params=_params(("arbitrary",)),
        name="moe_experts",
    )(blk_expert, x_tiles, wg, wu, wd)


def _combine_kernel(d1_ref, d2_ref, y_ref, slab_ref, x_ref, g_ref, nw_ref, o_ref, r1_ref, r2_ref, sem,
                    *, tm, tiles, final_norm):
    base = (pl.program_id(0) * tiles + pl.program_id(1)) * tm

    def issue(r, carry):
        t = base + r
        pltpu.make_async_copy(_token_tile(y_ref, d1_ref[t]), _token_tile(r1_ref, r), sem).start()
        pltpu.make_async_copy(_token_tile(y_ref, d2_ref[t]), _token_tile(r2_ref, r), sem).start()
        return carry

    lax.fori_loop(0, tm, issue, 0, unroll=DMA_ISSUE_UNROLL)
    pltpu.make_async_copy(y_ref.at[pl.ds(0, tm * SUBLANES)], r1_ref, sem).wait()
    pltpu.make_async_copy(y_ref.at[pl.ds(0, tm * SUBLANES)], r2_ref, sem).wait()
    slab = slab_ref[0]
    y = slab[:, 2:3] * _tiles_load(r1_ref, tm) + slab[:, 3:4] * _tiles_load(r2_ref, tm)
    out = x_ref[0] + g_ref[0] * y
    if final_norm:
        ms = jnp.mean(out * out, axis=-1, keepdims=True)
        out = out * lax.rsqrt(ms + NORM_EPS) * nw_ref[...]
    o_ref[0] = out


def _combine(dest1, dest2, y_buf, slab, x, g_ffn, norm_w, *, final_norm, tm=256):
    bsz, seq, d = x.shape
    tiles = seq // tm
    kern = functools.partial(_combine_kernel, tm=tm, tiles=tiles, final_norm=final_norm)
    return pl.pallas_call(
        kern,
        out_shape=jax.ShapeDtypeStruct((bsz, seq, d), F32),
        grid_spec=pltpu.PrefetchScalarGridSpec(
            num_scalar_prefetch=2,
            grid=(bsz, tiles),
            in_specs=[
                pl.BlockSpec(memory_space=pl.ANY),
                pl.BlockSpec((1, tm, ROUTE_LANES), lambda b, i, d1, d2: (b, i, 0)),
                pl.BlockSpec((1, tm, d), lambda b, i, d1, d2: (b, i, 0)),
                pl.BlockSpec((1, 1, d), lambda b, i, d1, d2: (b, 0, 0)),
                pl.BlockSpec((1, d), lambda b, i, d1, d2: (0, 0)),
            ],
            out_specs=pl.BlockSpec((1, tm, d), lambda b, i, d1, d2: (b, i, 0)),
            scratch_shapes=[pltpu.VMEM((tm * SUBLANES, LANES), F32), pltpu.VMEM((tm * SUBLANES, LANES), F32),
                            pltpu.SemaphoreType.DMA(())],
        ),
        compiler_params=_params(("arbitrary", "arbitrary")),
        name="moe_combine",
    )(dest1, dest2, y_buf, slab, x, g_ffn, norm_w.reshape(1, d))


def _block_diag(w):
    heads, hd, _ = w.shape
    eye = jnp.eye(heads, dtype=w.dtype)
    return (w[:, :, None, :] * eye[:, None, :, None]).reshape(heads * hd, heads * hd)


def kernel(x, c, ada_w, ada_b, norm_mix_w, w_in, conv_w, conv_b, lru_wa, lru_ba, lru_wx, lru_bx, lru_lambda, norm_lru_w, hgrn_lb, norm_hgrn_w, w_out, norm_ffn_w, router_group_w, router_group_b, router_expert_w, router_expert_b, expert_w_gate, expert_w_up, expert_w_down, final_norm_w):
    bsz, seq, d = x.shape
    assert d == SUBLANES * LANES, "the MoE row movement keeps one (8, 128) tile per token"
    depth = ada_w.shape[0]
    d_lru = conv_w.shape[-1]
    d_hgrn = hgrn_lb.shape[-1]
    m = bsz * seq
    n_rows = m * 2 + N_EXPERTS * EXPERT_BLOCK
    n_blocks = n_rows // EXPERT_BLOCK

    mod = _modulation(c, ada_w, ada_b)
    lb_cum = jnp.cumsum(jax.nn.softmax(hgrn_lb.astype(F32), axis=0), axis=0)
    lb_all = lb_cum - lb_cum[0:1]

    for l in range(depth):
        sh_mix, sc_mix, g_mix, sh_ffn, sc_ffn, g_ffn = [
            mod[l, :, i * d:(i + 1) * d].reshape(bsz, 1, d) for i in range(6)]
        proj = _in_proj(x, norm_mix_w[l], sc_mix, sh_mix, w_in[l].astype(BF16))
        wa_bd = jnp.stack([_block_diag(lru_wa[l, 0]), _block_diag(lru_wa[l, 1])]).astype(BF16)
        wx_bd = jnp.stack([_block_diag(lru_wx[l, 0]), _block_diag(lru_wx[l, 1])]).astype(BF16)
        lru = [
            _lru_scan(proj, conv_w[l], conv_b[l], wa_bd, lru_ba[l], wx_bd, lru_bx[l], lru_lambda[l],
                      reverse=rv)
            for rv in (False, True)]
        hg_f, hg_b = _hgrn(proj, lb_all[l], d_lru=d_lru, d_hgrn=d_hgrn)

        wr = jnp.zeros((d, ROUTE_LANES), F32)
        wr = wr.at[:, :N_GROUPS].set(router_group_w[l]).at[:, N_GROUPS:N_GROUPS + N_EXPERTS].set(router_expert_w[l])
        br = jnp.zeros((1, ROUTE_LANES), F32)
        br = br.at[0, :N_GROUPS].set(router_group_b[l]).at[0, N_GROUPS:N_GROUPS + N_EXPERTS].set(router_expert_b[l])
        x_mid, h_ffn, slab, counts = _post_mixer(
            lru[0], lru[1], proj, hg_f, hg_b, x, norm_lru_w[l], norm_hgrn_w[l], w_out[l].astype(BF16), g_mix,
            norm_ffn_w[l], sc_ffn, sh_ffn, wr.astype(BF16), br)

        cnt = counts[0, :N_EXPERTS].astype(jnp.int32)
        padded = ((cnt + EXPERT_BLOCK - 1) // EXPERT_BLOCK) * EXPERT_BLOCK
        pend = jnp.cumsum(padded)
        pstart = pend - padded
        blk_start = jnp.arange(n_blocks, dtype=jnp.int32) * EXPERT_BLOCK
        blk_expert = jnp.minimum(jnp.sum(pend[None, :] <= blk_start[:, None], axis=1), N_EXPERTS - 1)
        blk_expert = blk_expert.astype(jnp.int32)
        slab_flat = slab.reshape(m, ROUTE_LANES)
        e1 = slab_flat[:, 0].astype(jnp.int32)
        e2 = slab_flat[:, 1].astype(jnp.int32)
        dest1 = pstart[e1] + slab_flat[:, 4].astype(jnp.int32)
        dest2 = pstart[e2] + slab_flat[:, 5].astype(jnp.int32)

        x_buf = _dispatch(dest1, dest2, h_ffn.reshape(m * SUBLANES, LANES), n_rows)
        y_buf = _experts(blk_expert, x_buf, expert_w_gate, expert_w_up, expert_w_down, l)
        x = _combine(dest1, dest2, y_buf, slab, x_mid, g_ffn, final_norm_w, final_norm=(l == depth - 1))

    return x
```

```python
import functools

import jax
import jax.numpy as jnp
from jax import lax
from jax.experimental import pallas as pl
from jax.experimental.pallas import tpu as pltpu

F32 = jnp.float32
BF16 = jnp.bfloat16

LRU_HEADS = 8
HGRN_HEADS = 8
CONV_WIDTH = 4
LRU_C = 8.0
N_GROUPS = 4
EXPERTS_PER_GROUP = 8
N_EXPERTS = N_GROUPS * EXPERTS_PER_GROUP
NORM_EPS = 1e-6

LANES = 128
SUBLANES = 8
VMEM_LIMIT = 56 * 1024 * 1024

HGRN_CHUNK = 64
HGRN_SUB = 8
LOG2E = 1.4426950408889634
ROUTE_LANES = LANES
EXPERT_BLOCK = 256
DMA_ISSUE_UNROLL = 8
NEG_BIG = -3.0e38


def _params(sem):
    return pltpu.CompilerParams(dimension_semantics=sem, vmem_limit_bytes=VMEM_LIMIT)


def _dot(a, b):
    return jnp.dot(a, b, preferred_element_type=F32)


def _dot_nt(a, b):
    return lax.dot_general(a, b, (((1,), (1,)), ((), ())), preferred_element_type=F32)


def _dot_tn(a, b):
    return lax.dot_general(a, b, (((0,), (0,)), ((), ())), preferred_element_type=F32)


def _dot01_exact(m01, x):
    hi = x.astype(BF16)
    r1 = x - hi.astype(F32)
    mid = r1.astype(BF16)
    lo = (r1 - mid.astype(F32)).astype(BF16)
    return _dot(m01, hi) + _dot(m01, mid) + _dot(m01, lo)


def _sigmoid(x):
    return 1.0 / (1.0 + jnp.exp(-x))


def _mod_kernel(c_ref, w_ref, b_ref, o_ref):
    c = c_ref[...]
    cond = c * _sigmoid(c)
    o_ref[0] = _dot(cond.astype(BF16), w_ref[0].astype(BF16)) + b_ref[0]


def _modulation(c, ada_w, ada_b):
    depth, d, n = ada_w.shape
    bsz = c.shape[0]
    rows = -(-bsz // SUBLANES) * SUBLANES
    c_pad = jnp.pad(c, ((0, rows - bsz), (0, 0)))
    tn = n // 6
    out = pl.pallas_call(
        _mod_kernel,
        out_shape=jax.ShapeDtypeStruct((depth, rows, n), F32),
        grid=(depth, n // tn),
        in_specs=[
            pl.BlockSpec((rows, d), lambda l, j: (0, 0)),
            pl.BlockSpec((1, d, tn), lambda l, j: (l, 0, j)),
            pl.BlockSpec((1, 1, tn), lambda l, j: (l, 0, j)),
        ],
        out_specs=pl.BlockSpec((1, rows, tn), lambda l, j: (l, 0, j)),
        compiler_params=_params(("arbitrary", "arbitrary")),
        name="adaln_mod",
    )(c_pad, ada_w, ada_b.reshape(depth, 1, n))
    return out[:, :bsz]


def _rms_mod(x, nw, sc, sh):
    ms = jnp.mean(x * x, axis=-1, keepdims=True)
    return (x * lax.rsqrt(ms + NORM_EPS) * nw) * (1.0 + sc) + sh


def _inproj_kernel(x_ref, nw_ref, sc_ref, sh_ref, w_ref, o_ref):
    h = _rms_mod(x_ref[0], nw_ref[...], sc_ref[0], sh_ref[0])
    o_ref[0] = _dot(h.astype(BF16), w_ref[...])


def _in_proj(x, nw, sc, sh, w_bf16, tm=512):
    bsz, seq, d = x.shape
    n = w_bf16.shape[1]
    return pl.pallas_call(
        _inproj_kernel,
        out_shape=jax.ShapeDtypeStruct((bsz, seq, n), F32),
        grid=(bsz, seq // tm),
        in_specs=[
            pl.BlockSpec((1, tm, d), lambda b, i: (b, i, 0)),
            pl.BlockSpec((1, d), lambda b, i: (0, 0)),
            pl.BlockSpec((1, 1, d), lambda b, i: (b, 0, 0)),
            pl.BlockSpec((1, 1, d), lambda b, i: (b, 0, 0)),
            pl.BlockSpec((d, n), lambda b, i: (0, 0)),
        ],
        out_specs=pl.BlockSpec((1, tm, n), lambda b, i: (b, i, 0)),
        compiler_params=_params(("arbitrary", "arbitrary")),
        name="in_proj",
    )(x, nw.reshape(1, d), sc, sh, w_bf16)


def _lru_kernel(x_ref, xp_ref, xn_ref, cw_ref, cb_ref, wa_ref, ba_ref, wx_ref, bx_ref, lam_ref,
                o_ref, carry_ref, sa_ref, sb_ref, cin_ref, *, reverse, n_chunks, rows):
    c = pl.program_id(1)
    chunk = (n_chunks - 1 - c) if reverse else c

    @pl.when(c == 0)
    def _():
        carry_ref[...] = jnp.zeros_like(carry_ref)

    x = x_ref[0]
    width = x.shape[1]
    row = lax.broadcasted_iota(jnp.int32, (rows, width), 0)
    has_prev = jnp.where(chunk > 0, 1.0, 0.0)
    has_next = jnp.where(chunk < n_chunks - 1, 1.0, 0.0)
    xp = xp_ref[0] * has_prev
    xn = xn_ref[0] * has_next
    xe = jnp.concatenate([xp, x, xn], axis=0)
    h8 = SUBLANES
    cw = cw_ref[...]
    xc = (cw[0:1] * xe[h8 - 2:h8 - 2 + rows] + cw[1:2] * xe[h8 - 1:h8 - 1 + rows] + cw[2:3] * x
          + cw[3:4] * xe[h8 + 1:h8 + 1 + rows] + cb_ref[...])

    xcb = xc.astype(BF16)
    r = _sigmoid(_dot(xcb, wa_ref[0]) + ba_ref[0])
    gate_i = _sigmoid(_dot(xcb, wx_ref[0]) + bx_ref[0])
    lam = lam_ref[0]
    softplus_neg_lam = jnp.maximum(-lam, 0.0) + jnp.log1p(jnp.exp(-jnp.abs(lam)))
    log_a = (-LRU_C) * r * softplus_neg_lam
    a = jnp.exp(log_a)
    t = jnp.tanh(-log_a)
    u = jnp.sqrt(2.0 * t / (1.0 + t)) * (gate_i * xc)

    groups = rows // SUBLANES
    acc_a = a.reshape(groups, SUBLANES, width)
    acc_b = u.reshape(groups, SUBLANES, width)
    sub = lax.broadcasted_iota(jnp.int32, (groups, SUBLANES, width), 1)
    s = 1
    while s < SUBLANES:
        if reverse:
            valid = sub < SUBLANES - s
            sh_a, sh_b = pltpu.roll(acc_a, SUBLANES - s, 1), pltpu.roll(acc_b, SUBLANES - s, 1)
        else:
            valid = sub >= s
            sh_a, sh_b = pltpu.roll(acc_a, s, 1), pltpu.roll(acc_b, s, 1)
        acc_b = jnp.where(valid, acc_a * sh_b + acc_b, acc_b)
        acc_a = jnp.where(valid, acc_a * sh_a, acc_a)
        s *= 2
    acc_a = acc_a.reshape(rows, width)
    acc_b = acc_b.reshape(rows, width)
    groups = rows // SUBLANES
    edge = 0 if reverse else SUBLANES - 1
    n_tiles = width // LANES
    for j in range(n_tiles):
        sa_ref[j] = acc_a[:, j * LANES:(j + 1) * LANES]
        sb_ref[j] = acc_b[:, j * LANES:(j + 1) * LANES]
    ea = jnp.concatenate([sa_ref[j, pl.ds(edge, groups, stride=SUBLANES), :] for j in range(n_tiles)], axis=1)
    eb = jnp.concatenate([sb_ref[j, pl.ds(edge, groups, stride=SUBLANES), :] for j in range(n_tiles)], axis=1)
    grow = lax.broadcasted_iota(jnp.int32, (groups, width), 0)
    s = 1
    while s < groups:
        if reverse:
            valid = grow < groups - s
            sh_a, sh_b = pltpu.roll(ea, groups - s, 0), pltpu.roll(eb, groups - s, 0)
        else:
            valid = grow >= s
            sh_a, sh_b = pltpu.roll(ea, s, 0), pltpu.roll(eb, s, 0)
        eb = jnp.where(valid, ea * sh_b + eb, eb)
        ea = jnp.where(valid, ea * sh_a, ea)
        s *= 2
    carry = carry_ref[...]
    group_out = eb + ea * carry
    if reverse:
        carry_in = jnp.where(grow == groups - 1, carry, pltpu.roll(group_out, groups - 1, 0))
        carry_ref[...] = group_out[0:1]
    else:
        carry_in = jnp.where(grow == 0, carry, pltpu.roll(group_out, 1, 0))
        carry_ref[...] = group_out[groups - 1:groups]
    cin_ref[...] = carry_in
    for g in range(groups):
        rs = slice(g * SUBLANES, (g + 1) * SUBLANES)
        o_ref[0, rs, :] = acc_b[rs] + acc_a[rs] * cin_ref[g:g + 1, :]


def _lru_scan(proj, conv_w, conv_b, wa_bd, ba, wx_bd, bx, lam, *, reverse, rows=256):
    bsz, seq, _ = proj.shape
    d_lru = conv_w.shape[1]
    n_chunks = seq // rows
    halo = rows // SUBLANES
    last_halo = seq // SUBLANES - 1
    dirn = 1 if reverse else 0

    def chunk_of(c):
        return (n_chunks - 1 - c) if reverse else c

    vec = lambda: pl.BlockSpec((1, 1, d_lru), lambda b, c: (dirn, 0, 0))
    mat = lambda: pl.BlockSpec((1, d_lru, d_lru), lambda b, c: (dirn, 0, 0))
    kern = functools.partial(_lru_kernel, reverse=reverse, n_chunks=n_chunks, rows=rows)
    return pl.pallas_call(
        kern,
        out_shape=jax.ShapeDtypeStruct((bsz, seq, d_lru), F32),
        grid=(bsz, n_chunks),
        in_specs=[
            pl.BlockSpec((1, rows, d_lru), lambda b, c: (b, chunk_of(c), 0)),
            pl.BlockSpec((1, SUBLANES, d_lru),
                         lambda b, c: (b, jnp.maximum(chunk_of(c) * halo - 1, 0), 0)),
            pl.BlockSpec((1, SUBLANES, d_lru),
                         lambda b, c: (b, jnp.minimum((chunk_of(c) + 1) * halo, last_halo), 0)),
            pl.BlockSpec((CONV_WIDTH, d_lru), lambda b, c: (0, 0)),
            pl.BlockSpec((1, d_lru), lambda b, c: (0, 0)),
            mat(), vec(), mat(), vec(), vec(),
        ],
        out_specs=pl.BlockSpec((1, rows, d_lru), lambda b, c: (b, chunk_of(c), 0)),
        scratch_shapes=[pltpu.VMEM((1, d_lru), F32), pltpu.VMEM((d_lru // LANES, rows, LANES), F32),
                        pltpu.VMEM((d_lru // LANES, rows, LANES), F32),
                        pltpu.VMEM((rows // SUBLANES, d_lru), F32)],
        compiler_params=_params(("arbitrary", "arbitrary")),
        name="lru_bwd" if reverse else "lru_fwd",
    )(proj, proj, proj, conv_w, conv_b.reshape(1, d_lru), wa_bd, ba.reshape(2, 1, d_lru),
      wx_bd, bx.reshape(2, 1, d_lru), lam.reshape(2, 1, d_lru))


def _hgrn_direction(rev, q_ref, f_ref, v_ref, lb_ref, o_ref, st_ref, diag_s, lvl_s, upd_s, qe_s, btot_s, *, rows):
    ck, sb = HGRN_CHUNK, HGRN_SUB
    n_blk = ck // sb
    sb_shift = sb.bit_length() - 1
    n_sub = rows // ck
    width = q_ref.shape[-1]
    n_pairs = width // LANES
    half = LANES // 2

    def flip(idx, n):
        return (n - 1 - idx) if rev else idx

    n_lvl = n_blk.bit_length() - 1
    tf = flip(lax.broadcasted_iota(jnp.int32, (ck, ck), 0), ck)
    uf = flip(lax.broadcasted_iota(jnp.int32, (ck, ck), 1), ck)
    tb, ub = tf >> sb_shift, uf >> sb_shift
    pb = flip(lax.broadcasted_iota(jnp.int32, (n_blk, ck), 0), n_blk)
    pub = flip(lax.broadcasted_iota(jnp.int32, (n_blk, ck), 1), ck) >> sb_shift
    mats = [jnp.where((tb == ub) & (uf <= tf), 1.0, 0.0),
            jnp.where(pub < pb, 1.0, 0.0)]
    for lvl in range(n_lvl):
        mid = ((pb >> (lvl + 1)) << (lvl + 1)) + (1 << lvl)
        mats.append(jnp.where(pub < mid, 1.0, 0.0))
    mats.append(jnp.ones((SUBLANES, ck), F32))
    m_cum = jnp.concatenate(mats, axis=0).astype(BF16)

    def per_block(rows8):
        return jnp.concatenate(
            [jnp.broadcast_to(rows8[jb:jb + 1], (sb, rows8.shape[1])) for jb in range(n_blk)], axis=0)
    row_blk = flip(lax.broadcasted_iota(jnp.int32, (ck, width), 0), ck) >> sb_shift
    upper = [((row_blk >> lvl) & 1) == 1 for lvl in range(n_lvl)]
    pr = flip(lax.broadcasted_iota(jnp.int32, (ck, LANES), 0), ck) >> sb_shift
    pc = flip(lax.broadcasted_iota(jnp.int32, (ck, LANES), 1) & (ck - 1), ck) >> sb_shift
    group_mask = [(pr >> (lvl + 1)) == (pc >> (lvl + 1)) for lvl in range(n_lvl)]
    lane = lax.broadcasted_iota(jnp.int32, (1, LANES), 1)
    head0 = lane < half
    sr = lax.broadcasted_iota(jnp.int32, (LANES, LANES), 0)
    sc = lax.broadcasted_iota(jnp.int32, (LANES, LANES), 1)
    same_head = (sr < half) == (sc < half)
    er = lax.broadcasted_iota(jnp.int32, (sb * LANES, LANES), 0)
    ec = lax.broadcasted_iota(jnp.int32, (sb * LANES, LANES), 1)
    sel = jnp.where(ec == (((er & (LANES - 1)) >> (half.bit_length() - 1)) * half + (er >> (LANES.bit_length() - 1))),
                    1.0, 0.0).astype(BF16)
    sub_row = flip(lax.broadcasted_iota(jnp.int32, (sb, LANES), 0), sb)
    lbv = lb_ref[...]

    def row_start(j):
        return pl.multiple_of(flip(j, n_sub) * ck, ck)

    def stage1a(j):
        r0 = row_start(j)
        q = q_ref[0, pl.ds(r0, ck), :]
        z = f_ref[0, pl.ds(r0, ck), :]
        v = v_ref[0, pl.ds(r0, ck), :]
        f = lbv + (1.0 - lbv) * _sigmoid(z)
        lf = jnp.log(f)
        k = 1.0 - f
        return q, v, k, _dot01_exact(m_cum, lf)

    def stage1b(q, v, k, cums):
        bl = cums[0:ck]
        b = bl + per_block(cums[ck:ck + n_blk])
        tot_row = ck + (1 + n_lvl) * n_blk
        btot = cums[tot_row:tot_row + 1]
        qe = q * jnp.exp(b)
        ke = k * jnp.exp(btot - b)
        log2_k = jnp.log(k) * LOG2E
        b2 = b * LOG2E
        kb = b2 - log2_k
        bl2 = bl * LOG2E
        kbl = bl2 - log2_k
        q_lvl, k_lvl = [], []
        for lvl in range(n_lvl):
            split2 = per_block(cums[ck + (1 + lvl) * n_blk:ck + (2 + lvl) * n_blk] * LOG2E)
            q_lvl.append(q * jnp.exp2(jnp.where(upper[lvl], b2 - split2, NEG_BIG)))
            k_lvl.append(jnp.exp2(jnp.where(upper[lvl], NEG_BIG, split2 - kb)))

        qe_s[...] = qe.astype(BF16)
        btot_s[...] = btot
        for p in range(n_pairs):
            sl = slice(p * LANES, (p + 1) * LANES)
            diag_rows = []
            for jb in range(n_blk):
                rs = slice(jb * sb, (jb + 1) * sb)
                bl_b, kbl_b, q_b = bl2[rs, sl], kbl[rs, sl], q[rs, sl]
                terms = []
                for s in range(sb):
                    arg = jnp.where(sub_row >= flip(s, sb), bl_b - kbl_b[s:s + 1], NEG_BIG)
                    terms.append(q_b * jnp.exp2(arg))
                diag_rows.append(jnp.concatenate(terms, axis=1))
            diag_s[p] = _dot(jnp.concatenate(diag_rows, axis=0).astype(BF16), sel)
            for lvl in range(n_lvl):
                k_p = k_lvl[lvl][:, sl]
                k_heads = jnp.concatenate([jnp.where(head0, k_p, 0.0), jnp.where(head0, 0.0, k_p)], axis=0)
                lvl_s[p * n_lvl + lvl] = _dot_nt(q_lvl[lvl][:, sl].astype(BF16), k_heads.astype(BF16))
            upd_s[p] = _dot_tn(v[:, sl].astype(BF16), ke[:, sl].astype(BF16))

    def stage2_issue(j):
        r0 = row_start(j)
        v = v_ref[0, pl.ds(r0, ck), :]
        out = []
        for p in range(n_pairs):
            sl = slice(p * LANES, (p + 1) * LANES)
            parts = []
            for jb in range(n_blk):
                blk = diag_s[p, jb * sb:(jb + 1) * sb, :]
                parts.append(pltpu.roll(blk, jb * sb, 1) if jb else blk)
            scores = jnp.concatenate(parts, axis=0)
            for lvl in range(n_lvl):
                scores = scores + jnp.where(group_mask[lvl], lvl_s[p * n_lvl + lvl], 0.0)
            v_p = v[:, sl]
            v_heads = jnp.concatenate([jnp.where(head0, v_p, 0.0), jnp.where(head0, 0.0, v_p)], axis=0)
            intra = _dot(scores.astype(BF16), v_heads.astype(BF16))
            st = st_ref[p]
            inter = _dot_nt(qe_s[:, sl], st.astype(BF16))
            new_st = jnp.where(same_head, st * jnp.exp(btot_s[:, sl]) + upd_s[p], 0.0)
            out.append((inter + intra, new_st))
        return r0, out

    def stage2_finish(r0, out):
        for p in range(n_pairs):
            o_ref[0, pl.ds(r0, ck), p * LANES:(p + 1) * LANES] = out[p][0]
            st_ref[p] = out[p][1]

    return stage1a, stage1b, stage2_issue, stage2_finish


N_HGRN_SCRATCH = 6


def _hgrn_kernel(qf_ref, ff_ref, vf_ref, qb_ref, fb_ref, vb_ref, lb_ref, of_ref, ob_ref, *scratch, rows):
    fwd_scratch, bwd_scratch = scratch[:N_HGRN_SCRATCH], scratch[N_HGRN_SCRATCH:]

    @pl.when(pl.program_id(1) == 0)
    def _():
        fwd_scratch[0][...] = jnp.zeros_like(fwd_scratch[0])
        bwd_scratch[0][...] = jnp.zeros_like(bwd_scratch[0])

    f1a, f1b, f2, f3 = _hgrn_direction(False, qf_ref, ff_ref, vf_ref, lb_ref, of_ref, *fwd_scratch, rows=rows)
    b1a, b1b, b2, b3 = _hgrn_direction(True, qb_ref, fb_ref, vb_ref, lb_ref, ob_ref, *bwd_scratch, rows=rows)
    n_sub = rows // HGRN_CHUNK

    def stage1_both(j):
        fa = f1a(j)
        ba = b1a(j)
        f1b(*fa)
        b1b(*ba)

    stage1_both(0)

    def pipelined(j, carry):
        fo = f2(j)
        bo = b2(j)
        stage1_both(j + 1)
        f3(*fo)
        b3(*bo)
        return carry

    lax.fori_loop(0, n_sub - 1, pipelined, 0)
    fo = f2(n_sub - 1)
    bo = b2(n_sub - 1)
    f3(*fo)
    b3(*bo)


def _hgrn(proj, lb, *, d_lru, d_hgrn, rows=512):
    bsz, seq, _ = proj.shape
    n_chunks = seq // rows
    col0 = (2 * d_lru) // d_hgrn
    n_pairs = d_hgrn // LANES
    n_lvl = (HGRN_CHUNK // HGRN_SUB).bit_length() - 1
    fwd = lambda col: pl.BlockSpec((1, rows, d_hgrn), lambda b, c: (b, c, col))
    bwd = lambda col: pl.BlockSpec((1, rows, d_hgrn), lambda b, c: (b, n_chunks - 1 - c, col))
    direction_scratch = [
        pltpu.VMEM((n_pairs, LANES, LANES), F32),
        pltpu.VMEM((n_pairs, HGRN_CHUNK, LANES), F32),
        pltpu.VMEM((n_pairs * n_lvl, HGRN_CHUNK, LANES), F32),
        pltpu.VMEM((n_pairs, LANES, LANES), F32),
        pltpu.VMEM((HGRN_CHUNK, d_hgrn), BF16),
        pltpu.VMEM((1, d_hgrn), F32),
    ]
    assert len(direction_scratch) == N_HGRN_SCRATCH
    kern = functools.partial(_hgrn_kernel, rows=rows)
    out = jax.ShapeDtypeStruct((bsz, seq, d_hgrn), F32)
    return pl.pallas_call(
        kern,
        out_shape=(out, out),
        grid=(bsz, n_chunks),
        in_specs=[fwd(col0), fwd(col0 + 1), fwd(col0 + 3), bwd(col0), bwd(col0 + 2), bwd(col0 + 3),
                  pl.BlockSpec((1, d_hgrn), lambda b, c: (0, 0))],
        out_specs=(pl.BlockSpec((1, rows, d_hgrn), lambda b, c: (b, c, 0)),
                   pl.BlockSpec((1, rows, d_hgrn), lambda b, c: (b, n_chunks - 1 - c, 0))),
        scratch_shapes=direction_scratch + direction_scratch,
        compiler_params=_params(("arbitrary", "arbitrary")),
        name="hgrn2",
    )(proj, proj, proj, proj, proj, proj, lb.reshape(1, d_hgrn))


def _gelu_tanh(y):
    return 0.5 * y * (1.0 + jnp.tanh(0.7978845608028654 * (y + 0.044715 * (y * y * y))))


def _post_kernel(lf_ref, lb_ref, y_ref, of_ref, ob_ref, g_ref, x_ref, nlw_ref, nhw_ref, wo_ref,
                 gm_ref, nfw_ref, scf_ref, shf_ref, wr_ref, br_ref,
                 xo_ref, h_ref, slab_ref, cnt_ref, carry_ref, *, tm, d_lru):
    first = (pl.program_id(0) == 0) & (pl.program_id(1) == 0)

    @pl.when(first)
    def _():
        carry_ref[...] = jnp.zeros_like(carry_ref)

    lru = (lf_ref[0] + lb_ref[0]) * _gelu_tanh(y_ref[0])
    ms = jnp.mean(lru * lru, axis=-1, keepdims=True)
    lru = lru * lax.rsqrt(ms + NORM_EPS) * nlw_ref[...]

    hg = of_ref[0] + ob_ref[0]
    width = hg.shape[1]
    hd = width // HGRN_HEADS
    hd_shift = hd.bit_length() - 1
    er = lax.broadcasted_iota(jnp.int32, (width, width), 0) >> hd_shift
    ec = lax.broadcasted_iota(jnp.int32, (width, width), 1) >> hd_shift
    head_sum = jnp.where(er == ec, 1.0, 0.0).astype(BF16)
    sq = hg * hg
    sq_hi = sq.astype(BF16)
    sq_lo = (sq - sq_hi.astype(F32)).astype(BF16)
    ms_h = (_dot(sq_hi, head_sum) + _dot(sq_lo, head_sum)) * (1.0 / hd)
    g = g_ref[0]
    hg = (hg * lax.rsqrt(ms_h + NORM_EPS) * nhw_ref[...]) * (g * _sigmoid(g))

    mixed = _dot(lru.astype(BF16), wo_ref[0:d_lru, :]) + _dot(hg.astype(BF16), wo_ref[d_lru:, :])
    x_new = x_ref[0] + gm_ref[0] * mixed
    xo_ref[0] = x_new

    h = _rms_mod(x_new, nfw_ref[...], scf_ref[0], shf_ref[0])
    _tiles_store(h_ref, h, tm, lead=(0,))

    logits = _dot(h.astype(BF16), wr_ref[...]) + br_ref[...]
    lane = lax.broadcasted_iota(jnp.int32, (tm, ROUTE_LANES), 1)
    lane_f = lane.astype(F32)
    far = float(ROUTE_LANES)
    is_g = lane < N_GROUPS
    gl = jnp.where(is_g, logits, NEG_BIG)
    gmax = jnp.max(gl, axis=-1, keepdims=True)
    g_idx = jnp.min(jnp.where(gl == gmax, lane_f, far), axis=-1, keepdims=True)
    p_group = 1.0 / jnp.sum(jnp.where(is_g, jnp.exp(gl - gmax), 0.0), axis=-1, keepdims=True)
    e_lane = lane - N_GROUPS
    in_group = (e_lane >= 0) & (e_lane < N_EXPERTS) & ((e_lane >> (EXPERTS_PER_GROUP.bit_length() - 1)).astype(F32) == g_idx)
    ev = jnp.where(in_group, logits, NEG_BIG)
    top1 = jnp.max(ev, axis=-1, keepdims=True)
    i1 = jnp.min(jnp.where(in_group & (ev == top1), lane_f, far), axis=-1, keepdims=True)
    rest = in_group & (lane_f != i1)
    ev2 = jnp.where(rest, logits, NEG_BIG)
    top2 = jnp.max(ev2, axis=-1, keepdims=True)
    i2 = jnp.min(jnp.where(rest & (ev2 == top2), lane_f, far), axis=-1, keepdims=True)
    e1 = i1 - float(N_GROUPS)
    e2 = i2 - float(N_GROUPS)
    ex = jnp.exp(top2 - top1)
    w1 = p_group / (1.0 + ex)
    w2 = p_group * ex / (1.0 + ex)

    sel1 = lane_f == e1
    sel2 = lane_f == e2
    onehot = jnp.where(sel1 | sel2, 1.0, 0.0)
    tr = lax.broadcasted_iota(jnp.int32, (tm, tm), 0)
    tc = lax.broadcasted_iota(jnp.int32, (tm, tm), 1)
    before = jnp.where(tc < tr, 1.0, 0.0).astype(BF16)
    cnt = _dot(before, onehot.astype(BF16)) + carry_ref[0:1]
    rank1 = jnp.sum(jnp.where(sel1, cnt, 0.0), axis=-1, keepdims=True)
    rank2 = jnp.sum(jnp.where(sel2, cnt, 0.0), axis=-1, keepdims=True)
    total = carry_ref[0:1] + jnp.sum(onehot, axis=0, keepdims=True)
    carry_ref[...] = jnp.broadcast_to(total, carry_ref.shape)
    cnt_ref[...] = jnp.broadcast_to(total, cnt_ref.shape)

    slab = jnp.where(lane == 0, e1, 0.0)
    slab = jnp.where(lane == 1, e2, slab)
    slab = jnp.where(lane == 2, w1, slab)
    slab = jnp.where(lane == 3, w2, slab)
    slab = jnp.where(lane == 4, rank1, slab)
    slab = jnp.where(lane == 5, rank2, slab)
    slab_ref[0] = slab


def _post_mixer(lru_f, lru_b, proj, hg_f, hg_b, x, nlw, nhw, wo_bf16, g_mix, nfw, sc_ffn, sh_ffn, wr_bf16, br,
                *, tm=256):
    bsz, seq, d = x.shape
    d_lru = lru_f.shape[-1]
    d_hgrn = hg_f.shape[-1]
    y_col = 1
    g_col = (2 * d_lru) // d_hgrn + 4
    row = lambda w: pl.BlockSpec((1, tm, w), lambda b, i: (b, i, 0))
    vec = lambda w: pl.BlockSpec((1, w), lambda b, i: (0, 0))
    per_b = lambda: pl.BlockSpec((1, 1, d), lambda b, i: (b, 0, 0))
    kern = functools.partial(_post_kernel, tm=tm, d_lru=d_lru)
    return pl.pallas_call(
        kern,
        out_shape=(
            jax.ShapeDtypeStruct((bsz, seq, d), F32),
            jax.ShapeDtypeStruct((bsz, seq * SUBLANES, LANES), F32),
            jax.ShapeDtypeStruct((bsz, seq, ROUTE_LANES), F32),
            jax.ShapeDtypeStruct((SUBLANES, ROUTE_LANES), F32),
        ),
        grid=(bsz, seq // tm),
        in_specs=[
            row(d_lru), row(d_lru),
            pl.BlockSpec((1, tm, d_lru), lambda b, i: (b, i, y_col)),
            row(d_hgrn), row(d_hgrn),
            pl.BlockSpec((1, tm, d_hgrn), lambda b, i: (b, i, g_col)),
            row(d), vec(d_lru), vec(d_hgrn),
            pl.BlockSpec((d, d), lambda b, i: (0, 0)),
            per_b(), vec(d), per_b(), per_b(),
            pl.BlockSpec((d, ROUTE_LANES), lambda b, i: (0, 0)),
            vec(ROUTE_LANES),
        ],
        out_specs=(
            row(d), pl.BlockSpec((1, tm * SUBLANES, LANES), lambda b, i: (b, i, 0)), row(ROUTE_LANES),
            pl.BlockSpec((SUBLANES, ROUTE_LANES), lambda b, i: (0, 0)),
        ),
        scratch_shapes=[pltpu.VMEM((SUBLANES, ROUTE_LANES), F32)],
        compiler_params=_params(("arbitrary", "arbitrary")),
        name="post_mixer_router",
    )(lru_f, lru_b, proj, hg_f, hg_b, proj, x, nlw.reshape(1, d_lru), nhw.reshape(1, d_hgrn), wo_bf16,
      g_mix, nfw.reshape(1, d), sc_ffn, sh_ffn, wr_bf16, br)


def _tiles_load(ref, n, lead=()):
    return jnp.concatenate(
        [ref[(*lead, pl.ds(j, n, stride=SUBLANES), slice(None))] for j in range(SUBLANES)], axis=1)


def _tiles_store(ref, val, n, lead=()):
    for j in range(SUBLANES):
        ref[(*lead, pl.ds(j, n, stride=SUBLANES), slice(None))] = val[:, j * LANES:(j + 1) * LANES]


def _token_tile(ref, t):
    return ref.at[pl.ds(pl.multiple_of(t * SUBLANES, SUBLANES), SUBLANES)]


def _dispatch_kernel(d1_ref, d2_ref, h_ref, z_ref, o_ref, sem, *, tb):
    del z_ref
    base = pl.program_id(0) * tb

    def issue(r, carry):
        t = base + r
        pltpu.make_async_copy(_token_tile(h_ref, r), _token_tile(o_ref, d1_ref[t]), sem).start(priority=0)
        pltpu.make_async_copy(_token_tile(h_ref, r), _token_tile(o_ref, d2_ref[t]), sem).start(priority=1)
        return carry

    lax.fori_loop(0, tb, issue, 0, unroll=DMA_ISSUE_UNROLL)
    for _ in range(2):
        pltpu.make_async_copy(h_ref, o_ref.at[pl.ds(0, tb * SUBLANES)], sem).wait()


def _dispatch(dest1, dest2, h_tiles, n_rows, *, tb=256):
    m = h_tiles.shape[0] // SUBLANES
    kern = functools.partial(_dispatch_kernel, tb=tb)
    return pl.pallas_call(
        kern,
        out_shape=jax.ShapeDtypeStruct((n_rows * SUBLANES, LANES), h_tiles.dtype),
        grid_spec=pltpu.PrefetchScalarGridSpec(
            num_scalar_prefetch=2,
            grid=(m // tb,),
            in_specs=[pl.BlockSpec((tb * SUBLANES, LANES), lambda i, d1, d2: (i, 0)),
                      pl.BlockSpec(memory_space=pl.ANY)],
            out_specs=pl.BlockSpec(memory_space=pl.ANY),
            scratch_shapes=[pltpu.SemaphoreType.DMA(())],
        ),
        input_output_aliases={3: 0},
        compiler_params=pltpu.CompilerParams(dimension_semantics=("arbitrary",), has_side_effects=True),
        name="moe_dispatch",
    )(dest1, dest2, h_tiles, jnp.zeros((n_rows * SUBLANES, LANES), h_tiles.dtype))


def _expert_kernel(be_ref, x_ref, wg_ref, wu_ref, wd_ref, o_ref, wg_s, wu_s, wd_s):
    i = pl.program_id(0)
    prev_expert = be_ref[jnp.maximum(i - 1, 0)]

    @pl.when((i == 0) | (be_ref[i] != prev_expert))
    def _():
        wg_s[...] = wg_ref[0, 0].astype(BF16)
        wu_s[...] = wu_ref[0, 0].astype(BF16)
        wd_s[...] = wd_ref[0, 0].astype(BF16)

    blk = x_ref.shape[0] // SUBLANES
    x = _tiles_load(x_ref, blk).astype(BF16)
    gate = _dot(x, wg_s[...])
    up = _dot(x, wu_s[...])
    act = (gate * _sigmoid(gate)) * up
    _tiles_store(o_ref, _dot(act.astype(BF16), wd_s[...]), blk)


def _experts(blk_expert, x_tiles, wg, wu, wd, layer):
    n_rows = x_tiles.shape[0] // SUBLANES
    d, de = wg.shape[-2:]
    blk = EXPERT_BLOCK
    return pl.pallas_call(
        _expert_kernel,
        out_shape=jax.ShapeDtypeStruct((n_rows * SUBLANES, LANES), F32),
        grid_spec=pltpu.PrefetchScalarGridSpec(
            num_scalar_prefetch=1,
            grid=(n_rows // blk,),
            in_specs=[
                pl.BlockSpec((blk * SUBLANES, LANES), lambda i, be: (i, 0)),
                pl.BlockSpec((1, 1, d, de), lambda i, be: (layer, be[i], 0, 0)),
                pl.BlockSpec((1, 1, d, de), lambda i, be: (layer, be[i], 0, 0)),
                pl.BlockSpec((1, 1, de, d), lambda i, be: (layer, be[i], 0, 0)),
            ],
            out_specs=pl.BlockSpec((blk * SUBLANES, LANES), lambda i, be: (i, 0)),
            scratch_shapes=[pltpu.VMEM((d, de), BF16), pltpu.VMEM((d, de), BF16), pltpu.VMEM((de, d), BF16)],
        ),
        compiler_params=_params(("arbitrary",)),
        name="moe_experts",
    )(blk_expert, x_tiles, wg, wu, wd)


def _combine_kernel(d1_ref, d2_ref, y_ref, slab_ref, x_ref, g_ref, nw_ref, o_ref, r1_ref, r2_ref, sem,
                    *, tm, tiles, final_norm):
    base = (pl.program_id(0) * tiles + pl.program_id(1)) * tm

    def issue(r, carry):
        t = base + r
        pltpu.make_async_copy(_token_tile(y_ref, d1_ref[t]), _token_tile(r1_ref, r), sem).start(priority=0)
        pltpu.make_async_copy(_token_tile(y_ref, d2_ref[t]), _token_tile(r2_ref, r), sem).start(priority=1)
        return carry

    lax.fori_loop(0, tm, issue, 0, unroll=DMA_ISSUE_UNROLL)
    pltpu.make_async_copy(y_ref.at[pl.ds(0, tm * SUBLANES)], r1_ref, sem).wait()
    pltpu.make_async_copy(y_ref.at[pl.ds(0, tm * SUBLANES)], r2_ref, sem).wait()
    slab = slab_ref[0]
    y = slab[:, 2:3] * _tiles_load(r1_ref, tm) + slab[:, 3:4] * _tiles_load(r2_ref, tm)
    out = x_ref[0] + g_ref[0] * y
    if final_norm:
        ms = jnp.mean(out * out, axis=-1, keepdims=True)
        out = out * lax.rsqrt(ms + NORM_EPS) * nw_ref[...]
    o_ref[0] = out


def _combine(dest1, dest2, y_buf, slab, x, g_ffn, norm_w, *, final_norm, tm=256):
    bsz, seq, d = x.shape
    tiles = seq // tm
    kern = functools.partial(_combine_kernel, tm=tm, tiles=tiles, final_norm=final_norm)
    return pl.pallas_call(
        kern,
        out_shape=jax.ShapeDtypeStruct((bsz, seq, d), F32),
        grid_spec=pltpu.PrefetchScalarGridSpec(
            num_scalar_prefetch=2,
            grid=(bsz, tiles),
            in_specs=[
                pl.BlockSpec(memory_space=pl.ANY),
                pl.BlockSpec((1, tm, ROUTE_LANES), lambda b, i, d1, d2: (b, i, 0)),
                pl.BlockSpec((1, tm, d), lambda b, i, d1, d2: (b, i, 0)),
                pl.BlockSpec((1, 1, d), lambda b, i, d1, d2: (b, 0, 0)),
                pl.BlockSpec((1, d), lambda b, i, d1, d2: (0, 0)),
            ],
            out_specs=pl.BlockSpec((1, tm, d), lambda b, i, d1, d2: (b, i, 0)),
            scratch_shapes=[pltpu.VMEM((tm * SUBLANES, LANES), F32), pltpu.VMEM((tm * SUBLANES, LANES), F32),
                            pltpu.SemaphoreType.DMA(())],
        ),
        compiler_params=_params(("arbitrary", "arbitrary")),
        name="moe_combine",
    )(dest1, dest2, y_buf, slab, x, g_ffn, norm_w.reshape(1, d))


def _block_diag(w):
    heads, hd, _ = w.shape
    eye = jnp.eye(heads, dtype=w.dtype)
    return (w[:, :, None, :] * eye[:, None, :, None]).reshape(heads * hd, heads * hd)


def kernel(x, c, ada_w, ada_b, norm_mix_w, w_in, conv_w, conv_b, lru_wa, lru_ba, lru_wx, lru_bx, lru_lambda, norm_lru_w, hgrn_lb, norm_hgrn_w, w_out, norm_ffn_w, router_group_w, router_group_b, router_expert_w, router_expert_b, expert_w_gate, expert_w_up, expert_w_down, final_norm_w):
    bsz, seq, d = x.shape
    assert d == SUBLANES * LANES, "the MoE row movement keeps one (8, 128) tile per token"
    depth = ada_w.shape[0]
    d_lru = conv_w.shape[-1]
    d_hgrn = hgrn_lb.shape[-1]
    m = bsz * seq
    n_rows = m * 2 + N_EXPERTS * EXPERT_BLOCK
    n_blocks = n_rows // EXPERT_BLOCK

    mod = _modulation(c, ada_w, ada_b)
    lb_cum = jnp.cumsum(jax.nn.softmax(hgrn_lb.astype(F32), axis=0), axis=0)
    lb_all = lb_cum - lb_cum[0:1]

    for l in range(depth):
        sh_mix, sc_mix, g_mix, sh_ffn, sc_ffn, g_ffn = [
            mod[l, :, i * d:(i + 1) * d].reshape(bsz, 1, d) for i in range(6)]
        proj = _in_proj(x, norm_mix_w[l], sc_mix, sh_mix, w_in[l].astype(BF16))
        wa_bd = jnp.stack([_block_diag(lru_wa[l, 0]), _block_diag(lru_wa[l, 1])]).astype(BF16)
        wx_bd = jnp.stack([_block_diag(lru_wx[l, 0]), _block_diag(lru_wx[l, 1])]).astype(BF16)
        lru = [
            _lru_scan(proj, conv_w[l], conv_b[l], wa_bd, lru_ba[l], wx_bd, lru_bx[l], lru_lambda[l],
                      reverse=rv)
            for rv in (False, True)]
        hg_f, hg_b = _hgrn(proj, lb_all[l], d_lru=d_lru, d_hgrn=d_hgrn)

        wr = jnp.zeros((d, ROUTE_LANES), F32)
        wr = wr.at[:, :N_GROUPS].set(router_group_w[l]).at[:, N_GROUPS:N_GROUPS + N_EXPERTS].set(router_expert_w[l])
        br = jnp.zeros((1, ROUTE_LANES), F32)
        br = br.at[0, :N_GROUPS].set(router_group_b[l]).at[0, N_GROUPS:N_GROUPS + N_EXPERTS].set(router_expert_b[l])
        x_mid, h_ffn, slab, counts = _post_mixer(
            lru[0], lru[1], proj, hg_f, hg_b, x, norm_lru_w[l], norm_hgrn_w[l], w_out[l].astype(BF16), g_mix,
            norm_ffn_w[l], sc_ffn, sh_ffn, wr.astype(BF16), br)

        cnt = counts[0, :N_EXPERTS].astype(jnp.int32)
        padded = ((cnt + EXPERT_BLOCK - 1) // EXPERT_BLOCK) * EXPERT_BLOCK
        pend = jnp.cumsum(padded)
        pstart = pend - padded
        blk_start = jnp.arange(n_blocks, dtype=jnp.int32) * EXPERT_BLOCK
        blk_expert = jnp.minimum(jnp.sum(pend[None, :] <= blk_start[:, None], axis=1), N_EXPERTS - 1)
        blk_expert = blk_expert.astype(jnp.int32)
        slab_flat = slab.reshape(m, ROUTE_LANES)
        e1 = slab_flat[:, 0].astype(jnp.int32)
        e2 = slab_flat[:, 1].astype(jnp.int32)
        dest1 = pstart[e1] + slab_flat[:, 4].astype(jnp.int32)
        dest2 = pstart[e2] + slab_flat[:, 5].astype(jnp.int32)

        x_buf = _dispatch(dest1, dest2, h_ffn.reshape(m * SUBLANES, LANES), n_rows)
        y_buf = _experts(blk_expert, x_buf, expert_w_gate, expert_w_up, expert_w_down, l)
        x = _combine(dest1, dest2, y_buf, slab, x_mid, g_ffn, final_norm_w, final_norm=(l == depth - 1))

    return x
```

```python
import functools

import jax
import jax.numpy as jnp
from jax import lax
from jax.experimental import pallas as pl
from jax.experimental.pallas import tpu as pltpu

F32 = jnp.float32
BF16 = jnp.bfloat16

LRU_HEADS = 8
HGRN_HEADS = 8
CONV_WIDTH = 4
LRU_C = 8.0
N_GROUPS = 4
EXPERTS_PER_GROUP = 8
N_EXPERTS = N_GROUPS * EXPERTS_PER_GROUP
NORM_EPS = 1e-6

LANES = 128
SUBLANES = 8
VMEM_LIMIT = 56 * 1024 * 1024

HGRN_CHUNK = 64
HGRN_SUB = 8
LOG2E = 1.4426950408889634
ROUTE_LANES = LANES
EXPERT_BLOCK = 256
DMA_ISSUE_UNROLL = 8
NEG_BIG = -3.0e38


def _params(sem):
    return pltpu.CompilerParams(dimension_semantics=sem, vmem_limit_bytes=VMEM_LIMIT)


def _dot(a, b):
    return jnp.dot(a, b, preferred_element_type=F32)


def _dot_nt(a, b):
    return lax.dot_general(a, b, (((1,), (1,)), ((), ())), preferred_element_type=F32)


def _dot_tn(a, b):
    return lax.dot_general(a, b, (((0,), (0,)), ((), ())), preferred_element_type=F32)


def _dot01_exact(m01, x):
    hi = x.astype(BF16)
    r1 = x - hi.astype(F32)
    mid = r1.astype(BF16)
    lo = (r1 - mid.astype(F32)).astype(BF16)
    return _dot(m01, hi) + _dot(m01, mid) + _dot(m01, lo)


def _sigmoid(x):
    return 1.0 / (1.0 + jnp.exp(-x))


def _mod_kernel(c_ref, w_ref, b_ref, o_ref):
    c = c_ref[...]
    cond = c * _sigmoid(c)
    o_ref[0] = _dot(cond.astype(BF16), w_ref[0].astype(BF16)) + b_ref[0]


def _modulation(c, ada_w, ada_b):
    depth, d, n = ada_w.shape
    bsz = c.shape[0]
    rows = -(-bsz // SUBLANES) * SUBLANES
    c_pad = jnp.pad(c, ((0, rows - bsz), (0, 0)))
    tn = n // 6
    out = pl.pallas_call(
        _mod_kernel,
        out_shape=jax.ShapeDtypeStruct((depth, rows, n), F32),
        grid=(depth, n // tn),
        in_specs=[
            pl.BlockSpec((rows, d), lambda l, j: (0, 0)),
            pl.BlockSpec((1, d, tn), lambda l, j: (l, 0, j)),
            pl.BlockSpec((1, 1, tn), lambda l, j: (l, 0, j)),
        ],
        out_specs=pl.BlockSpec((1, rows, tn), lambda l, j: (l, 0, j)),
        compiler_params=_params(("arbitrary", "arbitrary")),
        name="adaln_mod",
    )(c_pad, ada_w, ada_b.reshape(depth, 1, n))
    return out[:, :bsz]


def _rms_mod(x, nw, sc, sh):
    ms = jnp.mean(x * x, axis=-1, keepdims=True)
    return (x * lax.rsqrt(ms + NORM_EPS) * nw) * (1.0 + sc) + sh


def _inproj_kernel(x_ref, nw_ref, sc_ref, sh_ref, w_ref, o_ref):
    h = _rms_mod(x_ref[0], nw_ref[...], sc_ref[0], sh_ref[0])
    o_ref[0] = _dot(h.astype(BF16), w_ref[...])


def _in_proj(x, nw, sc, sh, w_bf16, tm=512):
    bsz, seq, d = x.shape
    n = w_bf16.shape[1]
    return pl.pallas_call(
        _inproj_kernel,
        out_shape=jax.ShapeDtypeStruct((bsz, seq, n), F32),
        grid=(bsz, seq // tm),
        in_specs=[
            pl.BlockSpec((1, tm, d), lambda b, i: (b, i, 0)),
            pl.BlockSpec((1, d), lambda b, i: (0, 0)),
            pl.BlockSpec((1, 1, d), lambda b, i: (b, 0, 0)),
            pl.BlockSpec((1, 1, d), lambda b, i: (b, 0, 0)),
            pl.BlockSpec((d, n), lambda b, i: (0, 0)),
        ],
        out_specs=pl.BlockSpec((1, tm, n), lambda b, i: (b, i, 0)),
        compiler_params=_params(("arbitrary", "arbitrary")),
        name="in_proj",
    )(x, nw.reshape(1, d), sc, sh, w_bf16)


def _lru_kernel(x_ref, xp_ref, xn_ref, cw_ref, cb_ref, wa_ref, ba_ref, wx_ref, bx_ref, lam_ref,
                o_ref, carry_ref, sa_ref, sb_ref, cin_ref, *, reverse, n_chunks, rows):
    c = pl.program_id(1)
    chunk = (n_chunks - 1 - c) if reverse else c

    @pl.when(c == 0)
    def _():
        carry_ref[...] = jnp.zeros_like(carry_ref)

    x = x_ref[0]
    width = x.shape[1]
    row = lax.broadcasted_iota(jnp.int32, (rows, width), 0)
    has_prev = jnp.where(chunk > 0, 1.0, 0.0)
    has_next = jnp.where(chunk < n_chunks - 1, 1.0, 0.0)
    xp = xp_ref[0] * has_prev
    xn = xn_ref[0] * has_next
    xe = jnp.concatenate([xp, x, xn], axis=0)
    h8 = SUBLANES
    cw = cw_ref[...]
    xc = (cw[0:1] * xe[h8 - 2:h8 - 2 + rows] + cw[1:2] * xe[h8 - 1:h8 - 1 + rows] + cw[2:3] * x
          + cw[3:4] * xe[h8 + 1:h8 + 1 + rows] + cb_ref[...])

    xcb = xc.astype(BF16)
    r = _sigmoid(_dot(xcb, wa_ref[0]) + ba_ref[0])
    gate_i = _sigmoid(_dot(xcb, wx_ref[0]) + bx_ref[0])
    lam = lam_ref[0]
    softplus_neg_lam = jnp.maximum(-lam, 0.0) + jnp.log1p(jnp.exp(-jnp.abs(lam)))
    log_a = (-LRU_C) * r * softplus_neg_lam
    a = jnp.exp(log_a)
    t = jnp.tanh(-log_a)
    u = jnp.sqrt(2.0 * t / (1.0 + t)) * (gate_i * xc)

    groups = rows // SUBLANES
    acc_a = a.reshape(groups, SUBLANES, width)
    acc_b = u.reshape(groups, SUBLANES, width)
    sub = lax.broadcasted_iota(jnp.int32, (groups, SUBLANES, width), 1)
    s = 1
    while s < SUBLANES:
        if reverse:
            valid = sub < SUBLANES - s
            sh_a, sh_b = pltpu.roll(acc_a, SUBLANES - s, 1), pltpu.roll(acc_b, SUBLANES - s, 1)
        else:
            valid = sub >= s
            sh_a, sh_b = pltpu.roll(acc_a, s, 1), pltpu.roll(acc_b, s, 1)
        acc_b = jnp.where(valid, acc_a * sh_b + acc_b, acc_b)
        acc_a = jnp.where(valid, acc_a * sh_a, acc_a)
        s *= 2
    acc_a = acc_a.reshape(rows, width)
    acc_b = acc_b.reshape(rows, width)
    groups = rows // SUBLANES
    edge = 0 if reverse else SUBLANES - 1
    n_tiles = width // LANES
    for j in range(n_tiles):
        sa_ref[j] = acc_a[:, j * LANES:(j + 1) * LANES]
        sb_ref[j] = acc_b[:, j * LANES:(j + 1) * LANES]
    ea = jnp.concatenate([sa_ref[j, pl.ds(edge, groups, stride=SUBLANES), :] for j in range(n_tiles)], axis=1)
    eb = jnp.concatenate([sb_ref[j, pl.ds(edge, groups, stride=SUBLANES), :] for j in range(n_tiles)], axis=1)
    grow = lax.broadcasted_iota(jnp.int32, (groups, width), 0)
    s = 1
    while s < groups:
        if reverse:
            valid = grow < groups - s
            sh_a, sh_b = pltpu.roll(ea, groups - s, 0), pltpu.roll(eb, groups - s, 0)
        else:
            valid = grow >= s
            sh_a, sh_b = pltpu.roll(ea, s, 0), pltpu.roll(eb, s, 0)
        eb = jnp.where(valid, ea * sh_b + eb, eb)
        ea = jnp.where(valid, ea * sh_a, ea)
        s *= 2
    carry = carry_ref[...]
    group_out = eb + ea * carry
    if reverse:
        carry_in = jnp.where(grow == groups - 1, carry, pltpu.roll(group_out, groups - 1, 0))
        carry_ref[...] = group_out[0:1]
    else:
        carry_in = jnp.where(grow == 0, carry, pltpu.roll(group_out, 1, 0))
        carry_ref[...] = group_out[groups - 1:groups]
    cin_ref[...] = carry_in
    for g in range(groups):
        rs = slice(g * SUBLANES, (g + 1) * SUBLANES)
        o_ref[0, rs, :] = acc_b[rs] + acc_a[rs] * cin_ref[g:g + 1, :]


def _lru_scan(proj, conv_w, conv_b, wa_bd, ba, wx_bd, bx, lam, *, reverse, rows=256):
    bsz, seq, _ = proj.shape
    d_lru = conv_w.shape[1]
    n_chunks = seq // rows
    halo = rows // SUBLANES
    last_halo = seq // SUBLANES - 1
    dirn = 1 if reverse else 0

    def chunk_of(c):
        return (n_chunks - 1 - c) if reverse else c

    vec = lambda: pl.BlockSpec((1, 1, d_lru), lambda b, c: (dirn, 0, 0))
    mat = lambda: pl.BlockSpec((1, d_lru, d_lru), lambda b, c: (dirn, 0, 0))
    kern = functools.partial(_lru_kernel, reverse=reverse, n_chunks=n_chunks, rows=rows)
    return pl.pallas_call(
        kern,
        out_shape=jax.ShapeDtypeStruct((bsz, seq, d_lru), F32),
        grid=(bsz, n_chunks),
        in_specs=[
            pl.BlockSpec((1, rows, d_lru), lambda b, c: (b, chunk_of(c), 0)),
            pl.BlockSpec((1, SUBLANES, d_lru),
                         lambda b, c: (b, jnp.maximum(chunk_of(c) * halo - 1, 0), 0)),
            pl.BlockSpec((1, SUBLANES, d_lru),
                         lambda b, c: (b, jnp.minimum((chunk_of(c) + 1) * halo, last_halo), 0)),
            pl.BlockSpec((CONV_WIDTH, d_lru), lambda b, c: (0, 0)),
            pl.BlockSpec((1, d_lru), lambda b, c: (0, 0)),
            mat(), vec(), mat(), vec(), vec(),
        ],
        out_specs=pl.BlockSpec((1, rows, d_lru), lambda b, c: (b, chunk_of(c), 0)),
        scratch_shapes=[pltpu.VMEM((1, d_lru), F32), pltpu.VMEM((d_lru // LANES, rows, LANES), F32),
                        pltpu.VMEM((d_lru // LANES, rows, LANES), F32),
                        pltpu.VMEM((rows // SUBLANES, d_lru), F32)],
        compiler_params=_params(("arbitrary", "arbitrary")),
        name="lru_bwd" if reverse else "lru_fwd",
    )(proj, proj, proj, conv_w, conv_b.reshape(1, d_lru), wa_bd, ba.reshape(2, 1, d_lru),
      wx_bd, bx.reshape(2, 1, d_lru), lam.reshape(2, 1, d_lru))


def _hgrn_direction(rev, q_ref, f_ref, v_ref, lb_ref, o_ref, st_ref, diag_s, lvl_s, upd_s, qe_s, btot_s, *, rows):
    ck, sb = HGRN_CHUNK, HGRN_SUB
    n_blk = ck // sb
    sb_shift = sb.bit_length() - 1
    n_sub = rows // ck
    width = q_ref.shape[-1]
    n_pairs = width // LANES
    half = LANES // 2

    def flip(idx, n):
        return (n - 1 - idx) if rev else idx

    n_lvl = n_blk.bit_length() - 1
    tf = flip(lax.broadcasted_iota(jnp.int32, (ck, ck), 0), ck)
    uf = flip(lax.broadcasted_iota(jnp.int32, (ck, ck), 1), ck)
    tb, ub = tf >> sb_shift, uf >> sb_shift
    pb = flip(lax.broadcasted_iota(jnp.int32, (n_blk, ck), 0), n_blk)
    pub = flip(lax.broadcasted_iota(jnp.int32, (n_blk, ck), 1), ck) >> sb_shift
    mats = [jnp.where((tb == ub) & (uf <= tf), 1.0, 0.0),
            jnp.where(pub < pb, 1.0, 0.0)]
    for lvl in range(n_lvl):
        mid = ((pb >> (lvl + 1)) << (lvl + 1)) + (1 << lvl)
        mats.append(jnp.where(pub < mid, 1.0, 0.0))
    mats.append(jnp.ones((SUBLANES, ck), F32))
    m_cum = jnp.concatenate(mats, axis=0).astype(BF16)

    def per_block(rows8):
        return jnp.concatenate(
            [jnp.broadcast_to(rows8[jb:jb + 1], (sb, rows8.shape[1])) for jb in range(n_blk)], axis=0)
    row_blk = flip(lax.broadcasted_iota(jnp.int32, (ck, width), 0), ck) >> sb_shift
    upper = [((row_blk >> lvl) & 1) == 1 for lvl in range(n_lvl)]
    pr = flip(lax.broadcasted_iota(jnp.int32, (ck, LANES), 0), ck) >> sb_shift
    pc = flip(lax.broadcasted_iota(jnp.int32, (ck, LANES), 1) & (ck - 1), ck) >> sb_shift
    group_mask = [(pr >> (lvl + 1)) == (pc >> (lvl + 1)) for lvl in range(n_lvl)]
    lane = lax.broadcasted_iota(jnp.int32, (1, LANES), 1)
    head0 = lane < half
    sr = lax.broadcasted_iota(jnp.int32, (LANES, LANES), 0)
    sc = lax.broadcasted_iota(jnp.int32, (LANES, LANES), 1)
    same_head = (sr < half) == (sc < half)
    er = lax.broadcasted_iota(jnp.int32, (sb * LANES, LANES), 0)
    ec = lax.broadcasted_iota(jnp.int32, (sb * LANES, LANES), 1)
    sel = jnp.where(ec == (((er & (LANES - 1)) >> (half.bit_length() - 1)) * half + (er >> (LANES.bit_length() - 1))),
                    1.0, 0.0).astype(BF16)
    sub_row = flip(lax.broadcasted_iota(jnp.int32, (sb, LANES), 0), sb)
    lbv = lb_ref[...]

    def row_start(j):
        return pl.multiple_of(flip(j, n_sub) * ck, ck)

    def stage1a(j):
        r0 = row_start(j)
        q = q_ref[0, pl.ds(r0, ck), :]
        z = f_ref[0, pl.ds(r0, ck), :]
        v = v_ref[0, pl.ds(r0, ck), :]
        f = lbv + (1.0 - lbv) * _sigmoid(z)
        lf = jnp.log(f)
        k = 1.0 - f
        return q, v, k, _dot01_exact(m_cum, lf)

    def stage1b(q, v, k, cums):
        bl = cums[0:ck]
        b = bl + per_block(cums[ck:ck + n_blk])
        tot_row = ck + (1 + n_lvl) * n_blk
        btot = cums[tot_row:tot_row + 1]
        qe = q * jnp.exp(b)
        ke = k * jnp.exp(btot - b)
        log2_k = jnp.log(k) * LOG2E
        b2 = b * LOG2E
        kb = b2 - log2_k
        bl2 = bl * LOG2E
        kbl = bl2 - log2_k
        q_lvl, k_lvl = [], []
        for lvl in range(n_lvl):
            split2 = per_block(cums[ck + (1 + lvl) * n_blk:ck + (2 + lvl) * n_blk] * LOG2E)
            q_lvl.append(q * jnp.exp2(jnp.where(upper[lvl], b2 - split2, NEG_BIG)))
            k_lvl.append(jnp.exp2(jnp.where(upper[lvl], NEG_BIG, split2 - kb)))

        qe_s[...] = qe.astype(BF16)
        btot_s[...] = btot
        for p in range(n_pairs):
            sl = slice(p * LANES, (p + 1) * LANES)
            diag_rows = []
            for jb in range(n_blk):
                rs = slice(jb * sb, (jb + 1) * sb)
                bl_b, kbl_b, q_b = bl2[rs, sl], kbl[rs, sl], q[rs, sl]
                terms = []
                for s in range(sb):
                    arg = jnp.where(sub_row >= flip(s, sb), bl_b - kbl_b[s:s + 1], NEG_BIG)
                    terms.append(q_b * jnp.exp2(arg))
                diag_rows.append(jnp.concatenate(terms, axis=1))
            diag_s[p] = _dot(jnp.concatenate(diag_rows, axis=0).astype(BF16), sel)
            for lvl in range(n_lvl):
                k_p = k_lvl[lvl][:, sl]
                k_heads = jnp.concatenate([jnp.where(head0, k_p, 0.0), jnp.where(head0, 0.0, k_p)], axis=0)
                lvl_s[p * n_lvl + lvl] = _dot_nt(q_lvl[lvl][:, sl].astype(BF16), k_heads.astype(BF16))
            upd_s[p] = _dot_tn(v[:, sl].astype(BF16), ke[:, sl].astype(BF16))

    def stage2_issue(j):
        r0 = row_start(j)
        v = v_ref[0, pl.ds(r0, ck), :]
        out = []
        for p in range(n_pairs):
            sl = slice(p * LANES, (p + 1) * LANES)
            parts = []
            for jb in range(n_blk):
                blk = diag_s[p, jb * sb:(jb + 1) * sb, :]
                parts.append(pltpu.roll(blk, jb * sb, 1) if jb else blk)
            scores = jnp.concatenate(parts, axis=0)
            for lvl in range(n_lvl):
                scores = scores + jnp.where(group_mask[lvl], lvl_s[p * n_lvl + lvl], 0.0)
            v_p = v[:, sl]
            v_heads = jnp.concatenate([jnp.where(head0, v_p, 0.0), jnp.where(head0, 0.0, v_p)], axis=0)
            intra = _dot(scores.astype(BF16), v_heads.astype(BF16))
            st = st_ref[p]
            inter = _dot_nt(qe_s[:, sl], st.astype(BF16))
            new_st = jnp.where(same_head, st * jnp.exp(btot_s[:, sl]) + upd_s[p], 0.0)
            out.append((inter + intra, new_st))
        return r0, out

    def stage2_finish(r0, out):
        for p in range(n_pairs):
            o_ref[0, pl.ds(r0, ck), p * LANES:(p + 1) * LANES] = out[p][0]
            st_ref[p] = out[p][1]

    return stage1a, stage1b, stage2_issue, stage2_finish


N_HGRN_SCRATCH = 6


def _hgrn_kernel(qf_ref, ff_ref, vf_ref, qb_ref, fb_ref, vb_ref, lb_ref, of_ref, ob_ref, *scratch, rows):
    fwd_scratch, bwd_scratch = scratch[:N_HGRN_SCRATCH], scratch[N_HGRN_SCRATCH:]

    @pl.when(pl.program_id(1) == 0)
    def _():
        fwd_scratch[0][...] = jnp.zeros_like(fwd_scratch[0])
        bwd_scratch[0][...] = jnp.zeros_like(bwd_scratch[0])

    f1a, f1b, f2, f3 = _hgrn_direction(False, qf_ref, ff_ref, vf_ref, lb_ref, of_ref, *fwd_scratch, rows=rows)
    b1a, b1b, b2, b3 = _hgrn_direction(True, qb_ref, fb_ref, vb_ref, lb_ref, ob_ref, *bwd_scratch, rows=rows)
    n_sub = rows // HGRN_CHUNK

    def stage1_both(j):
        fa = f1a(j)
        ba = b1a(j)
        f1b(*fa)
        b1b(*ba)

    stage1_both(0)

    def pipelined(j, carry):
        fo = f2(j)
        bo = b2(j)
        stage1_both(j + 1)
        f3(*fo)
        b3(*bo)
        return carry

    lax.fori_loop(0, n_sub - 1, pipelined, 0)
    fo = f2(n_sub - 1)
    bo = b2(n_sub - 1)
    f3(*fo)
    b3(*bo)


def _hgrn(proj, lb, *, d_lru, d_hgrn, rows=512):
    bsz, seq, _ = proj.shape
    n_chunks = seq // rows
    col0 = (2 * d_lru) // d_hgrn
    n_pairs = d_hgrn // LANES
    n_lvl = (HGRN_CHUNK // HGRN_SUB).bit_length() - 1
    fwd = lambda col: pl.BlockSpec((1, rows, d_hgrn), lambda b, c: (b, c, col))
    bwd = lambda col: pl.BlockSpec((1, rows, d_hgrn), lambda b, c: (b, n_chunks - 1 - c, col))
    direction_scratch = [
        pltpu.VMEM((n_pairs, LANES, LANES), F32),
        pltpu.VMEM((n_pairs, HGRN_CHUNK, LANES), F32),
        pltpu.VMEM((n_pairs * n_lvl, HGRN_CHUNK, LANES), F32),
        pltpu.VMEM((n_pairs, LANES, LANES), F32),
        pltpu.VMEM((HGRN_CHUNK, d_hgrn), BF16),
        pltpu.VMEM((1, d_hgrn), F32),
    ]
    assert len(direction_scratch) == N_HGRN_SCRATCH
    kern = functools.partial(_hgrn_kernel, rows=rows)
    out = jax.ShapeDtypeStruct((bsz, seq, d_hgrn), F32)
    return pl.pallas_call(
        kern,
        out_shape=(out, out),
        grid=(bsz, n_chunks),
        in_specs=[fwd(col0), fwd(col0 + 1), fwd(col0 + 3), bwd(col0), bwd(col0 + 2), bwd(col0 + 3),
                  pl.BlockSpec((1, d_hgrn), lambda b, c: (0, 0))],
        out_specs=(pl.BlockSpec((1, rows, d_hgrn), lambda b, c: (b, c, 0)),
                   pl.BlockSpec((1, rows, d_hgrn), lambda b, c: (b, n_chunks - 1 - c, 0))),
        scratch_shapes=direction_scratch + direction_scratch,
        compiler_params=_params(("arbitrary", "arbitrary")),
        name="hgrn2",
    )(proj, proj, proj, proj, proj, proj, lb.reshape(1, d_hgrn))


def _gelu_tanh(y):
    return 0.5 * y * (1.0 + jnp.tanh(0.7978845608028654 * (y + 0.044715 * (y * y * y))))


def _post_kernel(lf_ref, lb_ref, y_ref, of_ref, ob_ref, g_ref, x_ref, nlw_ref, nhw_ref, wo_ref,
                 gm_ref, nfw_ref, scf_ref, shf_ref, wr_ref, br_ref,
                 xo_ref, h_ref, slab_ref, cnt_ref, route_ref, carry_ref, *, tm, d_lru):
    first = (pl.program_id(0) == 0) & (pl.program_id(1) == 0)

    @pl.when(first)
    def _():
        carry_ref[...] = jnp.zeros_like(carry_ref)

    lru = (lf_ref[0] + lb_ref[0]) * _gelu_tanh(y_ref[0])
    ms = jnp.mean(lru * lru, axis=-1, keepdims=True)
    lru = lru * lax.rsqrt(ms + NORM_EPS) * nlw_ref[...]

    hg = of_ref[0] + ob_ref[0]
    width = hg.shape[1]
    hd = width // HGRN_HEADS
    hd_shift = hd.bit_length() - 1
    er = lax.broadcasted_iota(jnp.int32, (width, width), 0) >> hd_shift
    ec = lax.broadcasted_iota(jnp.int32, (width, width), 1) >> hd_shift
    head_sum = jnp.where(er == ec, 1.0, 0.0).astype(BF16)
    sq = hg * hg
    sq_hi = sq.astype(BF16)
    sq_lo = (sq - sq_hi.astype(F32)).astype(BF16)
    ms_h = (_dot(sq_hi, head_sum) + _dot(sq_lo, head_sum)) * (1.0 / hd)
    g = g_ref[0]
    hg = (hg * lax.rsqrt(ms_h + NORM_EPS) * nhw_ref[...]) * (g * _sigmoid(g))

    mixed = _dot(lru.astype(BF16), wo_ref[0:d_lru, :]) + _dot(hg.astype(BF16), wo_ref[d_lru:, :])
    x_new = x_ref[0] + gm_ref[0] * mixed
    xo_ref[0] = x_new

    h = _rms_mod(x_new, nfw_ref[...], scf_ref[0], shf_ref[0])
    _tiles_store(h_ref, h, tm, lead=(0,))

    logits = _dot(h.astype(BF16), wr_ref[...]) + br_ref[...]
    lane = lax.broadcasted_iota(jnp.int32, (tm, ROUTE_LANES), 1)
    lane_f = lane.astype(F32)
    far = float(ROUTE_LANES)
    is_g = lane < N_GROUPS
    gl = jnp.where(is_g, logits, NEG_BIG)
    gmax = jnp.max(gl, axis=-1, keepdims=True)
    g_idx = jnp.min(jnp.where(gl == gmax, lane_f, far), axis=-1, keepdims=True)
    p_group = 1.0 / jnp.sum(jnp.where(is_g, jnp.exp(gl - gmax), 0.0), axis=-1, keepdims=True)
    e_lane = lane - N_GROUPS
    in_group = (e_lane >= 0) & (e_lane < N_EXPERTS) & ((e_lane >> (EXPERTS_PER_GROUP.bit_length() - 1)).astype(F32) == g_idx)
    ev = jnp.where(in_group, logits, NEG_BIG)
    top1 = jnp.max(ev, axis=-1, keepdims=True)
    i1 = jnp.min(jnp.where(in_group & (ev == top1), lane_f, far), axis=-1, keepdims=True)
    rest = in_group & (lane_f != i1)
    ev2 = jnp.where(rest, logits, NEG_BIG)
    top2 = jnp.max(ev2, axis=-1, keepdims=True)
    i2 = jnp.min(jnp.where(rest & (ev2 == top2), lane_f, far), axis=-1, keepdims=True)
    e1 = i1 - float(N_GROUPS)
    e2 = i2 - float(N_GROUPS)
    ex = jnp.exp(top2 - top1)
    w1 = p_group / (1.0 + ex)
    w2 = p_group * ex / (1.0 + ex)

    sel1 = lane_f == e1
    sel2 = lane_f == e2
    onehot = jnp.where(sel1 | sel2, 1.0, 0.0)
    tr = lax.broadcasted_iota(jnp.int32, (tm, tm), 0)
    tc = lax.broadcasted_iota(jnp.int32, (tm, tm), 1)
    before = jnp.where(tc < tr, 1.0, 0.0).astype(BF16)
    cnt = _dot(before, onehot.astype(BF16)) + carry_ref[0:1]
    rank1 = jnp.sum(jnp.where(sel1, cnt, 0.0), axis=-1, keepdims=True)
    rank2 = jnp.sum(jnp.where(sel2, cnt, 0.0), axis=-1, keepdims=True)
    total = carry_ref[0:1] + jnp.sum(onehot, axis=0, keepdims=True)
    carry_ref[...] = jnp.broadcast_to(total, carry_ref.shape)
    cnt_ref[...] = jnp.broadcast_to(total, cnt_ref.shape)

    slab = jnp.where(lane == 0, e1, 0.0)
    slab = jnp.where(lane == 1, e2, slab)
    slab = jnp.where(lane == 2, w1, slab)
    slab = jnp.where(lane == 3, w2, slab)
    slab = jnp.where(lane == 4, rank1, slab)
    slab = jnp.where(lane == 5, rank2, slab)
    slab_ref[0] = slab
    route_ref[...] = slab.T[0:SUBLANES]


def _post_mixer(lru_f, lru_b, proj, hg_f, hg_b, x, nlw, nhw, wo_bf16, g_mix, nfw, sc_ffn, sh_ffn, wr_bf16, br,
                *, tm=256):
    bsz, seq, d = x.shape
    d_lru = lru_f.shape[-1]
    d_hgrn = hg_f.shape[-1]
    y_col = 1
    g_col = (2 * d_lru) // d_hgrn + 4
    row = lambda w: pl.BlockSpec((1, tm, w), lambda b, i: (b, i, 0))
    vec = lambda w: pl.BlockSpec((1, w), lambda b, i: (0, 0))
    per_b = lambda: pl.BlockSpec((1, 1, d), lambda b, i: (b, 0, 0))
    kern = functools.partial(_post_kernel, tm=tm, d_lru=d_lru)
    return pl.pallas_call(
        kern,
        out_shape=(
            jax.ShapeDtypeStruct((bsz, seq, d), F32),
            jax.ShapeDtypeStruct((bsz, seq * SUBLANES, LANES), F32),
            jax.ShapeDtypeStruct((bsz, seq, ROUTE_LANES), F32),
            jax.ShapeDtypeStruct((SUBLANES, ROUTE_LANES), F32),
            jax.ShapeDtypeStruct((SUBLANES, bsz * seq), F32),
        ),
        grid=(bsz, seq // tm),
        in_specs=[
            row(d_lru), row(d_lru),
            pl.BlockSpec((1, tm, d_lru), lambda b, i: (b, i, y_col)),
            row(d_hgrn), row(d_hgrn),
            pl.BlockSpec((1, tm, d_hgrn), lambda b, i: (b, i, g_col)),
            row(d), vec(d_lru), vec(d_hgrn),
            pl.BlockSpec((d, d), lambda b, i: (0, 0)),
            per_b(), vec(d), per_b(), per_b(),
            pl.BlockSpec((d, ROUTE_LANES), lambda b, i: (0, 0)),
            vec(ROUTE_LANES),
        ],
        out_specs=(
            row(d), pl.BlockSpec((1, tm * SUBLANES, LANES), lambda b, i: (b, i, 0)), row(ROUTE_LANES),
            pl.BlockSpec((SUBLANES, ROUTE_LANES), lambda b, i: (0, 0)),
            pl.BlockSpec((SUBLANES, tm), lambda b, i: (0, b * (seq // tm) + i)),
        ),
        scratch_shapes=[pltpu.VMEM((SUBLANES, ROUTE_LANES), F32)],
        compiler_params=_params(("arbitrary", "arbitrary")),
        name="post_mixer_router",
    )(lru_f, lru_b, proj, hg_f, hg_b, proj, x, nlw.reshape(1, d_lru), nhw.reshape(1, d_hgrn), wo_bf16,
      g_mix, nfw.reshape(1, d), sc_ffn, sh_ffn, wr_bf16, br)


def _tiles_load(ref, n, lead=()):
    return jnp.concatenate(
        [ref[(*lead, pl.ds(j, n, stride=SUBLANES), slice(None))] for j in range(SUBLANES)], axis=1)


def _tiles_store(ref, val, n, lead=()):
    for j in range(SUBLANES):
        ref[(*lead, pl.ds(j, n, stride=SUBLANES), slice(None))] = val[:, j * LANES:(j + 1) * LANES]


def _token_tile(ref, t):
    return ref.at[pl.ds(pl.multiple_of(t * SUBLANES, SUBLANES), SUBLANES)]


def _dispatch_kernel(d1_ref, d2_ref, h_ref, z_ref, o_ref, sem, *, tb):
    del z_ref
    base = pl.program_id(0) * tb

    def issue(r, carry):
        t = base + r
        pltpu.make_async_copy(_token_tile(h_ref, r), _token_tile(o_ref, d1_ref[t]), sem).start(priority=0)
        pltpu.make_async_copy(_token_tile(h_ref, r), _token_tile(o_ref, d2_ref[t]), sem).start(priority=1)
        return carry

    lax.fori_loop(0, tb, issue, 0, unroll=DMA_ISSUE_UNROLL)
    for _ in range(2):
        pltpu.make_async_copy(h_ref, o_ref.at[pl.ds(0, tb * SUBLANES)], sem).wait()


def _dispatch(dest1, dest2, h_tiles, n_rows, *, tb=256):
    m = h_tiles.shape[0] // SUBLANES
    kern = functools.partial(_dispatch_kernel, tb=tb)
    return pl.pallas_call(
        kern,
        out_shape=jax.ShapeDtypeStruct((n_rows * SUBLANES, LANES), h_tiles.dtype),
        grid_spec=pltpu.PrefetchScalarGridSpec(
            num_scalar_prefetch=2,
            grid=(m // tb,),
            in_specs=[pl.BlockSpec((tb * SUBLANES, LANES), lambda i, d1, d2: (i, 0)),
                      pl.BlockSpec(memory_space=pl.ANY)],
            out_specs=pl.BlockSpec(memory_space=pl.ANY),
            scratch_shapes=[pltpu.SemaphoreType.DMA(())],
        ),
        input_output_aliases={3: 0},
        compiler_params=pltpu.CompilerParams(dimension_semantics=("arbitrary",), has_side_effects=True),
        name="moe_dispatch",
    )(dest1, dest2, h_tiles, jnp.zeros((n_rows * SUBLANES, LANES), h_tiles.dtype))


def _expert_kernel(be_ref, x_ref, wg_ref, wu_ref, wd_ref, o_ref, wg_s, wu_s, wd_s):
    i = pl.program_id(0)
    prev_expert = be_ref[jnp.maximum(i - 1, 0)]

    @pl.when((i == 0) | (be_ref[i] != prev_expert))
    def _():
        wg_s[...] = wg_ref[0, 0].astype(BF16)
        wu_s[...] = wu_ref[0, 0].astype(BF16)
        wd_s[...] = wd_ref[0, 0].astype(BF16)

    blk = x_ref.shape[0] // SUBLANES
    n_used = be_ref[pl.num_programs(0)]

    @pl.when(i < n_used)
    def _():
        x = _tiles_load(x_ref, blk).astype(BF16)
        gate = _dot(x, wg_s[...])
        up = _dot(x, wu_s[...])
        act = (gate * _sigmoid(gate)) * up
        _tiles_store(o_ref, _dot(act.astype(BF16), wd_s[...]), blk)

    @pl.when(i >= n_used)
    def _():
        o_ref[...] = jnp.zeros_like(o_ref)


def _experts(blk_expert, x_tiles, wg, wu, wd, layer):
    n_rows = x_tiles.shape[0] // SUBLANES
    d, de = wg.shape[-2:]
    blk = EXPERT_BLOCK
    n_blocks = n_rows // blk
    return pl.pallas_call(
        _expert_kernel,
        out_shape=jax.ShapeDtypeStruct((n_rows * SUBLANES, LANES), F32),
        grid_spec=pltpu.PrefetchScalarGridSpec(
            num_scalar_prefetch=1,
            grid=(n_blocks,),
            in_specs=[
                pl.BlockSpec((blk * SUBLANES, LANES), lambda i, be: (jnp.minimum(i, be[n_blocks] - 1), 0)),
                pl.BlockSpec((1, 1, d, de), lambda i, be: (layer, be[i], 0, 0)),
                pl.BlockSpec((1, 1, d, de), lambda i, be: (layer, be[i], 0, 0)),
                pl.BlockSpec((1, 1, de, d), lambda i, be: (layer, be[i], 0, 0)),
            ],
            out_specs=pl.BlockSpec((blk * SUBLANES, LANES), lambda i, be: (i, 0)),
            scratch_shapes=[pltpu.VMEM((d, de), BF16), pltpu.VMEM((d, de), BF16), pltpu.VMEM((de, d), BF16)],
        ),
        compiler_params=_params(("arbitrary",)),
        name="moe_experts",
    )(blk_expert, x_tiles, wg, wu, wd)


def _combine_kernel(d1_ref, d2_ref, y_ref, slab_ref, x_ref, g_ref, nw_ref, o_ref, ra0, rb0, ra1, rb1, sem,
                    *, tm, tiles, n_steps, final_norm):
    step = pl.program_id(0) * tiles + pl.program_id(1)
    bufs = ((ra0, rb0), (ra1, rb1))

    def gather(tile, slot):
        base = tile * tm
        r1_ref, r2_ref = bufs[slot]

        def issue(r, carry):
            t = base + r
            pltpu.make_async_copy(_token_tile(y_ref, d1_ref[t]), _token_tile(r1_ref, r),
                                  sem.at[slot]).start(priority=0)
            pltpu.make_async_copy(_token_tile(y_ref, d2_ref[t]), _token_tile(r2_ref, r),
                                  sem.at[slot]).start(priority=1)
            return carry

        lax.fori_loop(0, tm, issue, 0, unroll=DMA_ISSUE_UNROLL)

    @pl.when(step == 0)
    def _():
        gather(0, 0)

    for slot in range(2):
        @pl.when((step & 1) == slot)
        def _():
            @pl.when(step + 1 < n_steps)
            def _():
                gather(step + 1, 1 - slot)

            r1_ref, r2_ref = bufs[slot]
            pltpu.make_async_copy(y_ref.at[pl.ds(0, tm * SUBLANES)], r1_ref, sem.at[slot]).wait()
            pltpu.make_async_copy(y_ref.at[pl.ds(0, tm * SUBLANES)], r2_ref, sem.at[slot]).wait()
            slab = slab_ref[0]
            y = slab[:, 2:3] * _tiles_load(r1_ref, tm) + slab[:, 3:4] * _tiles_load(r2_ref, tm)
            out = x_ref[0] + g_ref[0] * y
            if final_norm:
                ms = jnp.mean(out * out, axis=-1, keepdims=True)
                out = out * lax.rsqrt(ms + NORM_EPS) * nw_ref[...]
            o_ref[0] = out


def _combine(dest1, dest2, y_buf, slab, x, g_ffn, norm_w, *, final_norm, tm=256):
    bsz, seq, d = x.shape
    tiles = seq // tm
    kern = functools.partial(_combine_kernel, tm=tm, tiles=tiles, n_steps=bsz * tiles, final_norm=final_norm)
    row_buf = pltpu.VMEM((tm * SUBLANES, LANES), F32)
    return pl.pallas_call(
        kern,
        out_shape=jax.ShapeDtypeStruct((bsz, seq, d), F32),
        grid_spec=pltpu.PrefetchScalarGridSpec(
            num_scalar_prefetch=2,
            grid=(bsz, tiles),
            in_specs=[
                pl.BlockSpec(memory_space=pl.ANY),
                pl.BlockSpec((1, tm, ROUTE_LANES), lambda b, i, d1, d2: (b, i, 0)),
                pl.BlockSpec((1, tm, d), lambda b, i, d1, d2: (b, i, 0)),
                pl.BlockSpec((1, 1, d), lambda b, i, d1, d2: (b, 0, 0)),
                pl.BlockSpec((1, d), lambda b, i, d1, d2: (0, 0)),
            ],
            out_specs=pl.BlockSpec((1, tm, d), lambda b, i, d1, d2: (b, i, 0)),
            scratch_shapes=[row_buf, row_buf, row_buf, row_buf, pltpu.SemaphoreType.DMA((2,))],
        ),
        compiler_params=_params(("arbitrary", "arbitrary")),
        name="moe_combine",
    )(dest1, dest2, y_buf, slab, x, g_ffn, norm_w.reshape(1, d))


def _block_diag(w):
    heads, hd, _ = w.shape
    eye = jnp.eye(heads, dtype=w.dtype)
    return (w[:, :, None, :] * eye[:, None, :, None]).reshape(heads * hd, heads * hd)


def kernel(x, c, ada_w, ada_b, norm_mix_w, w_in, conv_w, conv_b, lru_wa, lru_ba, lru_wx, lru_bx, lru_lambda, norm_lru_w, hgrn_lb, norm_hgrn_w, w_out, norm_ffn_w, router_group_w, router_group_b, router_expert_w, router_expert_b, expert_w_gate, expert_w_up, expert_w_down, final_norm_w):
    bsz, seq, d = x.shape
    assert d == SUBLANES * LANES, "the MoE row movement keeps one (8, 128) tile per token"
    depth = ada_w.shape[0]
    d_lru = conv_w.shape[-1]
    d_hgrn = hgrn_lb.shape[-1]
    m = bsz * seq
    n_rows = m * 2 + N_EXPERTS * EXPERT_BLOCK
    n_blocks = n_rows // EXPERT_BLOCK

    mod = _modulation(c, ada_w, ada_b)
    lb_cum = jnp.cumsum(jax.nn.softmax(hgrn_lb.astype(F32), axis=0), axis=0)
    lb_all = lb_cum - lb_cum[0:1]

    for l in range(depth):
        sh_mix, sc_mix, g_mix, sh_ffn, sc_ffn, g_ffn = [
            mod[l, :, i * d:(i + 1) * d].reshape(bsz, 1, d) for i in range(6)]
        proj = _in_proj(x, norm_mix_w[l], sc_mix, sh_mix, w_in[l].astype(BF16))
        wa_bd = jnp.stack([_block_diag(lru_wa[l, 0]), _block_diag(lru_wa[l, 1])]).astype(BF16)
        wx_bd = jnp.stack([_block_diag(lru_wx[l, 0]), _block_diag(lru_wx[l, 1])]).astype(BF16)
        lru = [
            _lru_scan(proj, conv_w[l], conv_b[l], wa_bd, lru_ba[l], wx_bd, lru_bx[l], lru_lambda[l],
                      reverse=rv)
            for rv in (False, True)]
        hg_f, hg_b = _hgrn(proj, lb_all[l], d_lru=d_lru, d_hgrn=d_hgrn)

        wr = jnp.zeros((d, ROUTE_LANES), F32)
        wr = wr.at[:, :N_GROUPS].set(router_group_w[l]).at[:, N_GROUPS:N_GROUPS + N_EXPERTS].set(router_expert_w[l])
        br = jnp.zeros((1, ROUTE_LANES), F32)
        br = br.at[0, :N_GROUPS].set(router_group_b[l]).at[0, N_GROUPS:N_GROUPS + N_EXPERTS].set(router_expert_b[l])
        x_mid, h_ffn, slab, counts, route = _post_mixer(
            lru[0], lru[1], proj, hg_f, hg_b, x, norm_lru_w[l], norm_hgrn_w[l], w_out[l].astype(BF16), g_mix,
            norm_ffn_w[l], sc_ffn, sh_ffn, wr.astype(BF16), br)

        cnt = counts[0, :N_EXPERTS].astype(jnp.int32)
        padded = ((cnt + EXPERT_BLOCK - 1) // EXPERT_BLOCK) * EXPERT_BLOCK
        pend = jnp.cumsum(padded)
        pstart = pend - padded
        blk_start = jnp.arange(n_blocks, dtype=jnp.int32) * EXPERT_BLOCK
        blk_expert = jnp.minimum(jnp.sum(pend[None, :] <= blk_start[:, None], axis=1), N_EXPERTS - 1)
        blocks_used = (pend[N_EXPERTS - 1] // EXPERT_BLOCK).reshape(1)
        blk_expert = jnp.concatenate([blk_expert.astype(jnp.int32), blocks_used.astype(jnp.int32)])
        route = route.astype(jnp.int32)
        dest1 = pstart[route[0]] + route[4]
        dest2 = pstart[route[1]] + route[5]

        x_buf = _dispatch(dest1, dest2, h_ffn.reshape(m * SUBLANES, LANES), n_rows)
        y_buf = _experts(blk_expert, x_buf, expert_w_gate, expert_w_up, expert_w_down, l)
        x = _combine(dest1, dest2, y_buf, slab, x_mid, g_ffn, final_norm_w, final_norm=(l == depth - 1))

    return x
```

```python
import functools

import jax
import jax.numpy as jnp
from jax import lax
from jax.experimental import pallas as pl
from jax.experimental.pallas import tpu as pltpu

F32 = jnp.float32
BF16 = jnp.bfloat16

LRU_HEADS = 8
HGRN_HEADS = 8
CONV_WIDTH = 4
LRU_C = 8.0
N_GROUPS = 4
EXPERTS_PER_GROUP = 8
N_EXPERTS = N_GROUPS * EXPERTS_PER_GROUP
NORM_EPS = 1e-6

LANES = 128
SUBLANES = 8
VMEM_LIMIT = 56 * 1024 * 1024

HGRN_CHUNK = 64
HGRN_SUB = 8
LOG2E = 1.4426950408889634
ROUTE_LANES = LANES
EXPERT_BLOCK = 256
DMA_ISSUE_UNROLL = 8
NEG_BIG = -3.0e38


def _params(sem):
    return pltpu.CompilerParams(dimension_semantics=sem, vmem_limit_bytes=VMEM_LIMIT)


def _dot(a, b):
    return jnp.dot(a, b, preferred_element_type=F32)


def _dot_nt(a, b):
    return lax.dot_general(a, b, (((1,), (1,)), ((), ())), preferred_element_type=F32)


def _dot_tn(a, b):
    return lax.dot_general(a, b, (((0,), (0,)), ((), ())), preferred_element_type=F32)


def _dot01_exact(m01, x):
    hi = x.astype(BF16)
    r1 = x - hi.astype(F32)
    mid = r1.astype(BF16)
    lo = (r1 - mid.astype(F32)).astype(BF16)
    return _dot(m01, hi) + _dot(m01, mid) + _dot(m01, lo)


def _sigmoid(x):
    return 1.0 / (1.0 + jnp.exp(-x))


def _mod_kernel(c_ref, w_ref, b_ref, o_ref):
    c = c_ref[...]
    cond = c * _sigmoid(c)
    o_ref[0] = _dot(cond.astype(BF16), w_ref[0].astype(BF16)) + b_ref[0]


def _modulation(c, ada_w, ada_b):
    depth, d, n = ada_w.shape
    bsz = c.shape[0]
    rows = -(-bsz // SUBLANES) * SUBLANES
    c_pad = jnp.pad(c, ((0, rows - bsz), (0, 0)))
    tn = n // 6
    out = pl.pallas_call(
        _mod_kernel,
        out_shape=jax.ShapeDtypeStruct((depth, rows, n), F32),
        grid=(depth, n // tn),
        in_specs=[
            pl.BlockSpec((rows, d), lambda l, j: (0, 0)),
            pl.BlockSpec((1, d, tn), lambda l, j: (l, 0, j)),
            pl.BlockSpec((1, 1, tn), lambda l, j: (l, 0, j)),
        ],
        out_specs=pl.BlockSpec((1, rows, tn), lambda l, j: (l, 0, j)),
        compiler_params=_params(("arbitrary", "arbitrary")),
        name="adaln_mod",
    )(c_pad, ada_w, ada_b.reshape(depth, 1, n))
    return out[:, :bsz]


def _rms_mod(x, nw, sc, sh):
    ms = jnp.mean(x * x, axis=-1, keepdims=True)
    return (x * lax.rsqrt(ms + NORM_EPS) * nw) * (1.0 + sc) + sh


def _inproj_kernel(x_ref, nw_ref, sc_ref, sh_ref, w_ref, o_ref):
    h = _rms_mod(x_ref[0], nw_ref[...], sc_ref[0], sh_ref[0])
    o_ref[0] = _dot(h.astype(BF16), w_ref[...])


def _in_proj(x, nw, sc, sh, w_bf16, tm=512):
    bsz, seq, d = x.shape
    n = w_bf16.shape[1]
    return pl.pallas_call(
        _inproj_kernel,
        out_shape=jax.ShapeDtypeStruct((bsz, seq, n), F32),
        grid=(bsz, seq // tm),
        in_specs=[
            pl.BlockSpec((1, tm, d), lambda b, i: (b, i, 0)),
            pl.BlockSpec((1, d), lambda b, i: (0, 0)),
            pl.BlockSpec((1, 1, d), lambda b, i: (b, 0, 0)),
            pl.BlockSpec((1, 1, d), lambda b, i: (b, 0, 0)),
            pl.BlockSpec((d, n), lambda b, i: (0, 0)),
        ],
        out_specs=pl.BlockSpec((1, tm, n), lambda b, i: (b, i, 0)),
        compiler_params=_params(("arbitrary", "arbitrary")),
        name="in_proj",
    )(x, nw.reshape(1, d), sc, sh, w_bf16)


def _lru_kernel(x_ref, xp_ref, xn_ref, cw_ref, cb_ref, wa_ref, ba_ref, wx_ref, bx_ref, lam_ref,
                o_ref, carry_ref, sa_ref, sb_ref, cin_ref, *, reverse, n_chunks, rows):
    c = pl.program_id(1)
    chunk = (n_chunks - 1 - c) if reverse else c

    @pl.when(c == 0)
    def _():
        carry_ref[...] = jnp.zeros_like(carry_ref)

    x = x_ref[0]
    width = x.shape[1]
    row = lax.broadcasted_iota(jnp.int32, (rows, width), 0)
    has_prev = jnp.where(chunk > 0, 1.0, 0.0)
    has_next = jnp.where(chunk < n_chunks - 1, 1.0, 0.0)
    xp = xp_ref[0] * has_prev
    xn = xn_ref[0] * has_next
    xe = jnp.concatenate([xp, x, xn], axis=0)
    h8 = SUBLANES
    cw = cw_ref[...]
    xc = (cw[0:1] * xe[h8 - 2:h8 - 2 + rows] + cw[1:2] * xe[h8 - 1:h8 - 1 + rows] + cw[2:3] * x
          + cw[3:4] * xe[h8 + 1:h8 + 1 + rows] + cb_ref[...])

    xcb = xc.astype(BF16)
    r = _sigmoid(_dot(xcb, wa_ref[0]) + ba_ref[0])
    gate_i = _sigmoid(_dot(xcb, wx_ref[0]) + bx_ref[0])
    lam = lam_ref[0]
    softplus_neg_lam = jnp.maximum(-lam, 0.0) + jnp.log1p(jnp.exp(-jnp.abs(lam)))
    log_a = (-LRU_C) * r * softplus_neg_lam
    a = jnp.exp(log_a)
    t = jnp.tanh(-log_a)
    u = jnp.sqrt(2.0 * t / (1.0 + t)) * (gate_i * xc)

    groups = rows // SUBLANES
    acc_a = a.reshape(groups, SUBLANES, width)
    acc_b = u.reshape(groups, SUBLANES, width)
    sub = lax.broadcasted_iota(jnp.int32, (groups, SUBLANES, width), 1)
    s = 1
    while s < SUBLANES:
        if reverse:
            valid = sub < SUBLANES - s
            sh_a, sh_b = pltpu.roll(acc_a, SUBLANES - s, 1), pltpu.roll(acc_b, SUBLANES - s, 1)
        else:
            valid = sub >= s
            sh_a, sh_b = pltpu.roll(acc_a, s, 1), pltpu.roll(acc_b, s, 1)
        acc_b = jnp.where(valid, acc_a * sh_b + acc_b, acc_b)
        acc_a = jnp.where(valid, acc_a * sh_a, acc_a)
        s *= 2
    acc_a = acc_a.reshape(rows, width)
    acc_b = acc_b.reshape(rows, width)
    groups = rows // SUBLANES
    edge = 0 if reverse else SUBLANES - 1
    n_tiles = width // LANES
    for j in range(n_tiles):
        sa_ref[j] = acc_a[:, j * LANES:(j + 1) * LANES]
        sb_ref[j] = acc_b[:, j * LANES:(j + 1) * LANES]
    ea = jnp.concatenate([sa_ref[j, pl.ds(edge, groups, stride=SUBLANES), :] for j in range(n_tiles)], axis=1)
    eb = jnp.concatenate([sb_ref[j, pl.ds(edge, groups, stride=SUBLANES), :] for j in range(n_tiles)], axis=1)
    grow = lax.broadcasted_iota(jnp.int32, (groups, width), 0)
    s = 1
    while s < groups:
        if reverse:
            valid = grow < groups - s
            sh_a, sh_b = pltpu.roll(ea, groups - s, 0), pltpu.roll(eb, groups - s, 0)
        else:
            valid = grow >= s
            sh_a, sh_b = pltpu.roll(ea, s, 0), pltpu.roll(eb, s, 0)
        eb = jnp.where(valid, ea * sh_b + eb, eb)
        ea = jnp.where(valid, ea * sh_a, ea)
        s *= 2
    carry = carry_ref[...]
    group_out = eb + ea * carry
    if reverse:
        carry_in = jnp.where(grow == groups - 1, carry, pltpu.roll(group_out, groups - 1, 0))
        carry_ref[...] = group_out[0:1]
    else:
        carry_in = jnp.where(grow == 0, carry, pltpu.roll(group_out, 1, 0))
        carry_ref[...] = group_out[groups - 1:groups]
    cin_ref[...] = carry_in
    for g in range(groups):
        rs = slice(g * SUBLANES, (g + 1) * SUBLANES)
        o_ref[0, rs, :] = acc_b[rs] + acc_a[rs] * cin_ref[g:g + 1, :]


def _lru_scan(proj, conv_w, conv_b, wa_bd, ba, wx_bd, bx, lam, *, reverse, rows=256):
    bsz, seq, _ = proj.shape
    d_lru = conv_w.shape[1]
    n_chunks = seq // rows
    halo = rows // SUBLANES
    last_halo = seq // SUBLANES - 1
    dirn = 1 if reverse else 0

    def chunk_of(c):
        return (n_chunks - 1 - c) if reverse else c

    vec = lambda: pl.BlockSpec((1, 1, d_lru), lambda b, c: (dirn, 0, 0))
    mat = lambda: pl.BlockSpec((1, d_lru, d_lru), lambda b, c: (dirn, 0, 0))
    kern = functools.partial(_lru_kernel, reverse=reverse, n_chunks=n_chunks, rows=rows)
    return pl.pallas_call(
        kern,
        out_shape=jax.ShapeDtypeStruct((bsz, seq, d_lru), F32),
        grid=(bsz, n_chunks),
        in_specs=[
            pl.BlockSpec((1, rows, d_lru), lambda b, c: (b, chunk_of(c), 0)),
            pl.BlockSpec((1, SUBLANES, d_lru),
                         lambda b, c: (b, jnp.maximum(chunk_of(c) * halo - 1, 0), 0)),
            pl.BlockSpec((1, SUBLANES, d_lru),
                         lambda b, c: (b, jnp.minimum((chunk_of(c) + 1) * halo, last_halo), 0)),
            pl.BlockSpec((CONV_WIDTH, d_lru), lambda b, c: (0, 0)),
            pl.BlockSpec((1, d_lru), lambda b, c: (0, 0)),
            mat(), vec(), mat(), vec(), vec(),
        ],
        out_specs=pl.BlockSpec((1, rows, d_lru), lambda b, c: (b, chunk_of(c), 0)),
        scratch_shapes=[pltpu.VMEM((1, d_lru), F32), pltpu.VMEM((d_lru // LANES, rows, LANES), F32),
                        pltpu.VMEM((d_lru // LANES, rows, LANES), F32),
                        pltpu.VMEM((rows // SUBLANES, d_lru), F32)],
        compiler_params=_params(("arbitrary", "arbitrary")),
        name="lru_bwd" if reverse else "lru_fwd",
    )(proj, proj, proj, conv_w, conv_b.reshape(1, d_lru), wa_bd, ba.reshape(2, 1, d_lru),
      wx_bd, bx.reshape(2, 1, d_lru), lam.reshape(2, 1, d_lru))


def _hgrn_direction(rev, q_ref, f_ref, v_ref, lb_ref, o_ref, st_ref, diag_s, lvl_s, upd_s, qe_s, btot_s, *, rows):
    ck, sb = HGRN_CHUNK, HGRN_SUB
    n_blk = ck // sb
    sb_shift = sb.bit_length() - 1
    n_sub = rows // ck
    width = q_ref.shape[-1]
    n_pairs = width // LANES
    half = LANES // 2

    def flip(idx, n):
        return (n - 1 - idx) if rev else idx

    n_lvl = n_blk.bit_length() - 1
    tf = flip(lax.broadcasted_iota(jnp.int32, (ck, ck), 0), ck)
    uf = flip(lax.broadcasted_iota(jnp.int32, (ck, ck), 1), ck)
    tb, ub = tf >> sb_shift, uf >> sb_shift
    pb = flip(lax.broadcasted_iota(jnp.int32, (n_blk, ck), 0), n_blk)
    pub = flip(lax.broadcasted_iota(jnp.int32, (n_blk, ck), 1), ck) >> sb_shift
    mats = [jnp.where((tb == ub) & (uf <= tf), 1.0, 0.0),
            jnp.where(pub < pb, 1.0, 0.0)]
    for lvl in range(n_lvl):
        mid = ((pb >> (lvl + 1)) << (lvl + 1)) + (1 << lvl)
        mats.append(jnp.where(pub < mid, 1.0, 0.0))
    mats.append(jnp.ones((SUBLANES, ck), F32))
    m_cum = jnp.concatenate(mats, axis=0).astype(BF16)

    def per_block(rows8):
        return jnp.concatenate(
            [jnp.broadcast_to(rows8[jb:jb + 1], (sb, rows8.shape[1])) for jb in range(n_blk)], axis=0)
    row_blk = flip(lax.broadcasted_iota(jnp.int32, (ck, width), 0), ck) >> sb_shift
    upper = [((row_blk >> lvl) & 1) == 1 for lvl in range(n_lvl)]
    pr = flip(lax.broadcasted_iota(jnp.int32, (ck, LANES), 0), ck) >> sb_shift
    pc = flip(lax.broadcasted_iota(jnp.int32, (ck, LANES), 1) & (ck - 1), ck) >> sb_shift
    group_mask = [(pr >> (lvl + 1)) == (pc >> (lvl + 1)) for lvl in range(n_lvl)]
    lane = lax.broadcasted_iota(jnp.int32, (1, LANES), 1)
    head0 = lane < half
    sr = lax.broadcasted_iota(jnp.int32, (LANES, LANES), 0)
    sc = lax.broadcasted_iota(jnp.int32, (LANES, LANES), 1)
    same_head = (sr < half) == (sc < half)
    er = lax.broadcasted_iota(jnp.int32, (sb * LANES, LANES), 0)
    ec = lax.broadcasted_iota(jnp.int32, (sb * LANES, LANES), 1)
    sel = jnp.where(ec == (((er & (LANES - 1)) >> (half.bit_length() - 1)) * half + (er >> (LANES.bit_length() - 1))),
                    1.0, 0.0).astype(BF16)
    sub_row = flip(lax.broadcasted_iota(jnp.int32, (sb, LANES), 0), sb)
    lbv = lb_ref[...]

    def row_start(j):
        return pl.multiple_of(flip(j, n_sub) * ck, ck)

    def stage1a(j):
        r0 = row_start(j)
        q = q_ref[0, pl.ds(r0, ck), :]
        z = f_ref[0, pl.ds(r0, ck), :]
        v = v_ref[0, pl.ds(r0, ck), :]
        f = lbv + (1.0 - lbv) * _sigmoid(z)
        lf = jnp.log(f)
        k = 1.0 - f
        return q, v, k, _dot01_exact(m_cum, lf)

    def stage1b(q, v, k, cums):
        bl = cums[0:ck]
        b = bl + per_block(cums[ck:ck + n_blk])
        tot_row = ck + (1 + n_lvl) * n_blk
        btot = cums[tot_row:tot_row + 1]
        qe = q * jnp.exp(b)
        ke = k * jnp.exp(btot - b)
        log2_k = jnp.log(k) * LOG2E
        b2 = b * LOG2E
        kb = b2 - log2_k
        bl2 = bl * LOG2E
        kbl = bl2 - log2_k
        q_lvl, k_lvl = [], []
        for lvl in range(n_lvl):
            split2 = per_block(cums[ck + (1 + lvl) * n_blk:ck + (2 + lvl) * n_blk] * LOG2E)
            q_lvl.append(q * jnp.exp2(jnp.where(upper[lvl], b2 - split2, NEG_BIG)))
            k_lvl.append(jnp.exp2(jnp.where(upper[lvl], NEG_BIG, split2 - kb)))

        qe_s[...] = qe.astype(BF16)
        btot_s[...] = btot
        for p in range(n_pairs):
            sl = slice(p * LANES, (p + 1) * LANES)
            diag_rows = []
            for jb in range(n_blk):
                rs = slice(jb * sb, (jb + 1) * sb)
                bl_b, kbl_b, q_b = bl2[rs, sl], kbl[rs, sl], q[rs, sl]
                terms = []
                for s in range(sb):
                    arg = jnp.where(sub_row >= flip(s, sb), bl_b - kbl_b[s:s + 1], NEG_BIG)
                    terms.append(q_b * jnp.exp2(arg))
                diag_rows.append(jnp.concatenate(terms, axis=1))
            diag_s[p] = _dot(jnp.concatenate(diag_rows, axis=0).astype(BF16), sel)
            for lvl in range(n_lvl):
                k_p = k_lvl[lvl][:, sl]
                k_heads = jnp.concatenate([jnp.where(head0, k_p, 0.0), jnp.where(head0, 0.0, k_p)], axis=0)
                lvl_s[p * n_lvl + lvl] = _dot_nt(q_lvl[lvl][:, sl].astype(BF16), k_heads.astype(BF16))
            upd_s[p] = _dot_tn(v[:, sl].astype(BF16), ke[:, sl].astype(BF16))

    def stage2_issue(j):
        r0 = row_start(j)
        v = v_ref[0, pl.ds(r0, ck), :]
        out = []
        for p in range(n_pairs):
            sl = slice(p * LANES, (p + 1) * LANES)
            parts = []
            for jb in range(n_blk):
                blk = diag_s[p, jb * sb:(jb + 1) * sb, :]
                parts.append(pltpu.roll(blk, jb * sb, 1) if jb else blk)
            scores = jnp.concatenate(parts, axis=0)
            for lvl in range(n_lvl):
                scores = scores + jnp.where(group_mask[lvl], lvl_s[p * n_lvl + lvl], 0.0)
            v_p = v[:, sl]
            v_heads = jnp.concatenate([jnp.where(head0, v_p, 0.0), jnp.where(head0, 0.0, v_p)], axis=0)
            intra = _dot(scores.astype(BF16), v_heads.astype(BF16))
            st = st_ref[p]
            inter = _dot_nt(qe_s[:, sl], st.astype(BF16))
            new_st = jnp.where(same_head, st * jnp.exp(btot_s[:, sl]) + upd_s[p], 0.0)
            out.append((inter + intra, new_st))
        return r0, out

    def stage2_finish(r0, out):
        for p in range(n_pairs):
            o_ref[0, pl.ds(r0, ck), p * LANES:(p + 1) * LANES] = out[p][0]
            st_ref[p] = out[p][1]

    return stage1a, stage1b, stage2_issue, stage2_finish


N_HGRN_SCRATCH = 6


def _hgrn_kernel(qf_ref, ff_ref, vf_ref, qb_ref, fb_ref, vb_ref, lb_ref, of_ref, ob_ref, *scratch, rows):
    fwd_scratch, bwd_scratch = scratch[:N_HGRN_SCRATCH], scratch[N_HGRN_SCRATCH:]

    @pl.when(pl.program_id(1) == 0)
    def _():
        fwd_scratch[0][...] = jnp.zeros_like(fwd_scratch[0])
        bwd_scratch[0][...] = jnp.zeros_like(bwd_scratch[0])

    f1a, f1b, f2, f3 = _hgrn_direction(False, qf_ref, ff_ref, vf_ref, lb_ref, of_ref, *fwd_scratch, rows=rows)
    b1a, b1b, b2, b3 = _hgrn_direction(True, qb_ref, fb_ref, vb_ref, lb_ref, ob_ref, *bwd_scratch, rows=rows)
    n_sub = rows // HGRN_CHUNK

    def stage1_both(j):
        fa = f1a(j)
        ba = b1a(j)
        f1b(*fa)
        b1b(*ba)

    stage1_both(0)

    def pipelined(j, carry):
        fo = f2(j)
        bo = b2(j)
        stage1_both(j + 1)
        f3(*fo)
        b3(*bo)
        return carry

    lax.fori_loop(0, n_sub - 1, pipelined, 0)
    fo = f2(n_sub - 1)
    bo = b2(n_sub - 1)
    f3(*fo)
    b3(*bo)


def _hgrn(proj, lb, *, d_lru, d_hgrn, rows=512):
    bsz, seq, _ = proj.shape
    n_chunks = seq // rows
    col0 = (2 * d_lru) // d_hgrn
    n_pairs = d_hgrn // LANES
    n_lvl = (HGRN_CHUNK // HGRN_SUB).bit_length() - 1
    fwd = lambda col: pl.BlockSpec((1, rows, d_hgrn), lambda b, c: (b, c, col))
    bwd = lambda col: pl.BlockSpec((1, rows, d_hgrn), lambda b, c: (b, n_chunks - 1 - c, col))
    direction_scratch = [
        pltpu.VMEM((n_pairs, LANES, LANES), F32),
        pltpu.VMEM((n_pairs, HGRN_CHUNK, LANES), F32),
        pltpu.VMEM((n_pairs * n_lvl, HGRN_CHUNK, LANES), F32),
        pltpu.VMEM((n_pairs, LANES, LANES), F32),
        pltpu.VMEM((HGRN_CHUNK, d_hgrn), BF16),
        pltpu.VMEM((1, d_hgrn), F32),
    ]
    assert len(direction_scratch) == N_HGRN_SCRATCH
    kern = functools.partial(_hgrn_kernel, rows=rows)
    out = jax.ShapeDtypeStruct((bsz, seq, d_hgrn), F32)
    return pl.pallas_call(
        kern,
        out_shape=(out, out),
        grid=(bsz, n_chunks),
        in_specs=[fwd(col0), fwd(col0 + 1), fwd(col0 + 3), bwd(col0), bwd(col0 + 2), bwd(col0 + 3),
                  pl.BlockSpec((1, d_hgrn), lambda b, c: (0, 0))],
        out_specs=(pl.BlockSpec((1, rows, d_hgrn), lambda b, c: (b, c, 0)),
                   pl.BlockSpec((1, rows, d_hgrn), lambda b, c: (b, n_chunks - 1 - c, 0))),
        scratch_shapes=direction_scratch + direction_scratch,
        compiler_params=_params(("arbitrary", "arbitrary")),
        name="hgrn2",
    )(proj, proj, proj, proj, proj, proj, lb.reshape(1, d_hgrn))


def _gelu_tanh(y):
    return 0.5 * y * (1.0 + jnp.tanh(0.7978845608028654 * (y + 0.044715 * (y * y * y))))


def _post_kernel(lf_ref, lb_ref, y_ref, of_ref, ob_ref, g_ref, x_ref, nlw_ref, nhw_ref, wo_ref,
                 gm_ref, nfw_ref, scf_ref, shf_ref, wr_ref, br_ref,
                 xo_ref, h_ref, slab_ref, cnt_ref, route_ref, carry_ref, *, tm, d_lru):
    first = (pl.program_id(0) == 0) & (pl.program_id(1) == 0)

    @pl.when(first)
    def _():
        carry_ref[...] = jnp.zeros_like(carry_ref)

    lru = (lf_ref[0] + lb_ref[0]) * _gelu_tanh(y_ref[0])
    ms = jnp.mean(lru * lru, axis=-1, keepdims=True)
    lru = lru * lax.rsqrt(ms + NORM_EPS) * nlw_ref[...]

    hg = of_ref[0] + ob_ref[0]
    width = hg.shape[1]
    hd = width // HGRN_HEADS
    hd_shift = hd.bit_length() - 1
    er = lax.broadcasted_iota(jnp.int32, (width, width), 0) >> hd_shift
    ec = lax.broadcasted_iota(jnp.int32, (width, width), 1) >> hd_shift
    head_sum = jnp.where(er == ec, 1.0, 0.0).astype(BF16)
    sq = hg * hg
    sq_hi = sq.astype(BF16)
    sq_lo = (sq - sq_hi.astype(F32)).astype(BF16)
    ms_h = (_dot(sq_hi, head_sum) + _dot(sq_lo, head_sum)) * (1.0 / hd)
    g = g_ref[0]
    hg = (hg * lax.rsqrt(ms_h + NORM_EPS) * nhw_ref[...]) * (g * _sigmoid(g))

    mixed = _dot(lru.astype(BF16), wo_ref[0:d_lru, :]) + _dot(hg.astype(BF16), wo_ref[d_lru:, :])
    x_new = x_ref[0] + gm_ref[0] * mixed
    xo_ref[0] = x_new

    h = _rms_mod(x_new, nfw_ref[...], scf_ref[0], shf_ref[0])
    _tiles_store(h_ref, h, tm, lead=(0,))

    logits = _dot(h.astype(BF16), wr_ref[...]) + br_ref[...]
    lane = lax.broadcasted_iota(jnp.int32, (tm, ROUTE_LANES), 1)
    lane_f = lane.astype(F32)
    far = float(ROUTE_LANES)
    is_g = lane < N_GROUPS
    gl = jnp.where(is_g, logits, NEG_BIG)
    gmax = jnp.max(gl, axis=-1, keepdims=True)
    g_idx = jnp.argmax(gl, axis=-1, keepdims=True).astype(jnp.int32)
    p_group = 1.0 / jnp.sum(jnp.where(is_g, jnp.exp(gl - gmax), 0.0), axis=-1, keepdims=True)
    e_lane = lane - N_GROUPS
    in_group = (e_lane >= 0) & (e_lane < N_EXPERTS) & ((e_lane >> (EXPERTS_PER_GROUP.bit_length() - 1)) == g_idx)
    ev = jnp.where(in_group, logits, NEG_BIG)
    top1 = jnp.max(ev, axis=-1, keepdims=True)
    i1 = jnp.argmax(ev, axis=-1, keepdims=True).astype(jnp.int32)
    ev2 = jnp.where(in_group & (lane != i1), logits, NEG_BIG)
    top2 = jnp.max(ev2, axis=-1, keepdims=True)
    i2 = jnp.argmax(ev2, axis=-1, keepdims=True).astype(jnp.int32)
    e1 = (i1 - N_GROUPS).astype(F32)
    e2 = (i2 - N_GROUPS).astype(F32)
    ex = jnp.exp(top2 - top1)
    w1 = p_group / (1.0 + ex)
    w2 = p_group * ex / (1.0 + ex)

    sel1 = lane_f == e1
    sel2 = lane_f == e2
    onehot = jnp.where(sel1 | sel2, 1.0, 0.0)
    tr = lax.broadcasted_iota(jnp.int32, (tm, tm), 0)
    tc = lax.broadcasted_iota(jnp.int32, (tm, tm), 1)
    before = jnp.where(tc < tr, 1.0, 0.0).astype(BF16)
    cnt = _dot(before, onehot.astype(BF16)) + carry_ref[0:1]
    rank1 = jnp.sum(jnp.where(sel1, cnt, 0.0), axis=-1, keepdims=True)
    rank2 = jnp.sum(jnp.where(sel2, cnt, 0.0), axis=-1, keepdims=True)
    total = carry_ref[0:1] + jnp.sum(onehot, axis=0, keepdims=True)
    carry_ref[...] = jnp.broadcast_to(total, carry_ref.shape)
    cnt_ref[...] = jnp.broadcast_to(total, cnt_ref.shape)

    slab = jnp.where(lane == 0, e1, 0.0)
    slab = jnp.where(lane == 1, e2, slab)
    slab = jnp.where(lane == 2, w1, slab)
    slab = jnp.where(lane == 3, w2, slab)
    slab = jnp.where(lane == 4, rank1, slab)
    slab = jnp.where(lane == 5, rank2, slab)
    slab_ref[0] = slab
    route_ref[...] = slab.T[0:SUBLANES]


def _post_mixer(lru_f, lru_b, proj, hg_f, hg_b, x, nlw, nhw, wo_bf16, g_mix, nfw, sc_ffn, sh_ffn, wr_bf16, br,
                *, tm=512):
    bsz, seq, d = x.shape
    d_lru = lru_f.shape[-1]
    d_hgrn = hg_f.shape[-1]
    y_col = 1
    g_col = (2 * d_lru) // d_hgrn + 4
    row = lambda w: pl.BlockSpec((1, tm, w), lambda b, i: (b, i, 0))
    vec = lambda w: pl.BlockSpec((1, w), lambda b, i: (0, 0))
    per_b = lambda: pl.BlockSpec((1, 1, d), lambda b, i: (b, 0, 0))
    kern = functools.partial(_post_kernel, tm=tm, d_lru=d_lru)
    return pl.pallas_call(
        kern,
        out_shape=(
            jax.ShapeDtypeStruct((bsz, seq, d), F32),
            jax.ShapeDtypeStruct((bsz, seq * SUBLANES, LANES), F32),
            jax.ShapeDtypeStruct((bsz, seq, ROUTE_LANES), F32),
            jax.ShapeDtypeStruct((SUBLANES, ROUTE_LANES), F32),
            jax.ShapeDtypeStruct((SUBLANES, bsz * seq), F32),
        ),
        grid=(bsz, seq // tm),
        in_specs=[
            row(d_lru), row(d_lru),
            pl.BlockSpec((1, tm, d_lru), lambda b, i: (b, i, y_col)),
            row(d_hgrn), row(d_hgrn),
            pl.BlockSpec((1, tm, d_hgrn), lambda b, i: (b, i, g_col)),
            row(d), vec(d_lru), vec(d_hgrn),
            pl.BlockSpec((d, d), lambda b, i: (0, 0)),
            per_b(), vec(d), per_b(), per_b(),
            pl.BlockSpec((d, ROUTE_LANES), lambda b, i: (0, 0)),
            vec(ROUTE_LANES),
        ],
        out_specs=(
            row(d), pl.BlockSpec((1, tm * SUBLANES, LANES), lambda b, i: (b, i, 0)), row(ROUTE_LANES),
            pl.BlockSpec((SUBLANES, ROUTE_LANES), lambda b, i: (0, 0)),
            pl.BlockSpec((SUBLANES, tm), lambda b, i: (0, b * (seq // tm) + i)),
        ),
        scratch_shapes=[pltpu.VMEM((SUBLANES, ROUTE_LANES), F32)],
        compiler_params=_params(("arbitrary", "arbitrary")),
        name="post_mixer_router",
    )(lru_f, lru_b, proj, hg_f, hg_b, proj, x, nlw.reshape(1, d_lru), nhw.reshape(1, d_hgrn), wo_bf16,
      g_mix, nfw.reshape(1, d), sc_ffn, sh_ffn, wr_bf16, br)


def _tiles_load(ref, n, lead=()):
    return jnp.concatenate(
        [ref[(*lead, pl.ds(j, n, stride=SUBLANES), slice(None))] for j in range(SUBLANES)], axis=1)


def _tiles_store(ref, val, n, lead=()):
    for j in range(SUBLANES):
        ref[(*lead, pl.ds(j, n, stride=SUBLANES), slice(None))] = val[:, j * LANES:(j + 1) * LANES]


def _token_tile(ref, t):
    return ref.at[pl.ds(pl.multiple_of(t * SUBLANES, SUBLANES), SUBLANES)]


def _dispatch_kernel(d1_ref, d2_ref, h_ref, z_ref, o_ref, sem, *, tb):
    del z_ref
    base = pl.program_id(0) * tb

    def issue(r, carry):
        t = base + r
        pltpu.make_async_copy(_token_tile(h_ref, r), _token_tile(o_ref, d1_ref[t]), sem).start(priority=0)
        pltpu.make_async_copy(_token_tile(h_ref, r), _token_tile(o_ref, d2_ref[t]), sem).start(priority=1)
        return carry

    lax.fori_loop(0, tb, issue, 0, unroll=DMA_ISSUE_UNROLL)
    for _ in range(2):
        pltpu.make_async_copy(h_ref, o_ref.at[pl.ds(0, tb * SUBLANES)], sem).wait()


def _dispatch(dest1, dest2, h_tiles, n_rows, *, tb=256):
    m = h_tiles.shape[0] // SUBLANES
    kern = functools.partial(_dispatch_kernel, tb=tb)
    return pl.pallas_call(
        kern,
        out_shape=jax.ShapeDtypeStruct((n_rows * SUBLANES, LANES), h_tiles.dtype),
        grid_spec=pltpu.PrefetchScalarGridSpec(
            num_scalar_prefetch=2,
            grid=(m // tb,),
            in_specs=[pl.BlockSpec((tb * SUBLANES, LANES), lambda i, d1, d2: (i, 0)),
                      pl.BlockSpec(memory_space=pl.ANY)],
            out_specs=pl.BlockSpec(memory_space=pl.ANY),
            scratch_shapes=[pltpu.SemaphoreType.DMA(())],
        ),
        input_output_aliases={3: 0},
        compiler_params=pltpu.CompilerParams(dimension_semantics=("arbitrary",), has_side_effects=True),
        name="moe_dispatch",
    )(dest1, dest2, h_tiles, jnp.zeros((n_rows * SUBLANES, LANES), h_tiles.dtype))


def _expert_kernel(be_ref, x_ref, wg_ref, wu_ref, wd_ref, o_ref, wg_s, wu_s, wd_s):
    i = pl.program_id(0)
    prev_expert = be_ref[jnp.maximum(i - 1, 0)]

    @pl.when((i == 0) | (be_ref[i] != prev_expert))
    def _():
        wg_s[...] = wg_ref[0, 0].astype(BF16)
        wu_s[...] = wu_ref[0, 0].astype(BF16)
        wd_s[...] = wd_ref[0, 0].astype(BF16)

    blk = x_ref.shape[0] // SUBLANES
    n_used = be_ref[pl.num_programs(0)]

    @pl.when(i < n_used)
    def _():
        x = _tiles_load(x_ref, blk).astype(BF16)
        gate = _dot(x, wg_s[...])
        up = _dot(x, wu_s[...])
        act = (gate * _sigmoid(gate)) * up
        _tiles_store(o_ref, _dot(act.astype(BF16), wd_s[...]), blk)

    @pl.when(i >= n_used)
    def _():
        o_ref[...] = jnp.zeros_like(o_ref)


def _experts(blk_expert, x_tiles, wg, wu, wd, layer):
    n_rows = x_tiles.shape[0] // SUBLANES
    d, de = wg.shape[-2:]
    blk = EXPERT_BLOCK
    n_blocks = n_rows // blk
    return pl.pallas_call(
        _expert_kernel,
        out_shape=jax.ShapeDtypeStruct((n_rows * SUBLANES, LANES), F32),
        grid_spec=pltpu.PrefetchScalarGridSpec(
            num_scalar_prefetch=1,
            grid=(n_blocks,),
            in_specs=[
                pl.BlockSpec((blk * SUBLANES, LANES), lambda i, be: (jnp.minimum(i, be[n_blocks] - 1), 0)),
                pl.BlockSpec((1, 1, d, de), lambda i, be: (layer, be[i], 0, 0)),
                pl.BlockSpec((1, 1, d, de), lambda i, be: (layer, be[i], 0, 0)),
                pl.BlockSpec((1, 1, de, d), lambda i, be: (layer, be[i], 0, 0)),
            ],
            out_specs=pl.BlockSpec((blk * SUBLANES, LANES), lambda i, be: (i, 0)),
            scratch_shapes=[pltpu.VMEM((d, de), BF16), pltpu.VMEM((d, de), BF16), pltpu.VMEM((de, d), BF16)],
        ),
        compiler_params=_params(("arbitrary",)),
        name="moe_experts",
    )(blk_expert, x_tiles, wg, wu, wd)


def _combine_kernel(d1_ref, d2_ref, y_ref, slab_ref, x_ref, g_ref, nw_ref, o_ref, ra0, rb0, ra1, rb1, sem,
                    *, tm, tiles, n_steps, final_norm):
    step = pl.program_id(0) * tiles + pl.program_id(1)
    bufs = ((ra0, rb0), (ra1, rb1))

    def gather(tile, slot):
        base = tile * tm
        r1_ref, r2_ref = bufs[slot]

        def issue(r, carry):
            t = base + r
            pltpu.make_async_copy(_token_tile(y_ref, d1_ref[t]), _token_tile(r1_ref, r),
                                  sem.at[slot]).start(priority=0)
            pltpu.make_async_copy(_token_tile(y_ref, d2_ref[t]), _token_tile(r2_ref, r),
                                  sem.at[slot]).start(priority=1)
            return carry

        lax.fori_loop(0, tm, issue, 0, unroll=DMA_ISSUE_UNROLL)

    @pl.when(step == 0)
    def _():
        gather(0, 0)

    for slot in range(2):
        @pl.when((step & 1) == slot)
        def _():
            @pl.when(step + 1 < n_steps)
            def _():
                gather(step + 1, 1 - slot)

            r1_ref, r2_ref = bufs[slot]
            pltpu.make_async_copy(y_ref.at[pl.ds(0, tm * SUBLANES)], r1_ref, sem.at[slot]).wait()
            pltpu.make_async_copy(y_ref.at[pl.ds(0, tm * SUBLANES)], r2_ref, sem.at[slot]).wait()
            slab = slab_ref[0]
            y = slab[:, 2:3] * _tiles_load(r1_ref, tm) + slab[:, 3:4] * _tiles_load(r2_ref, tm)
            out = x_ref[0] + g_ref[0] * y
            if final_norm:
                ms = jnp.mean(out * out, axis=-1, keepdims=True)
                out = out * lax.rsqrt(ms + NORM_EPS) * nw_ref[...]
            o_ref[0] = out


def _combine(dest1, dest2, y_buf, slab, x, g_ffn, norm_w, *, final_norm, tm=256):
    bsz, seq, d = x.shape
    tiles = seq // tm
    kern = functools.partial(_combine_kernel, tm=tm, tiles=tiles, n_steps=bsz * tiles, final_norm=final_norm)
    row_buf = pltpu.VMEM((tm * SUBLANES, LANES), F32)
    return pl.pallas_call(
        kern,
        out_shape=jax.ShapeDtypeStruct((bsz, seq, d), F32),
        grid_spec=pltpu.PrefetchScalarGridSpec(
            num_scalar_prefetch=2,
            grid=(bsz, tiles),
            in_specs=[
                pl.BlockSpec(memory_space=pl.ANY),
                pl.BlockSpec((1, tm, ROUTE_LANES), lambda b, i, d1, d2: (b, i, 0)),
                pl.BlockSpec((1, tm, d), lambda b, i, d1, d2: (b, i, 0)),
                pl.BlockSpec((1, 1, d), lambda b, i, d1, d2: (b, 0, 0)),
                pl.BlockSpec((1, d), lambda b, i, d1, d2: (0, 0)),
            ],
            out_specs=pl.BlockSpec((1, tm, d), lambda b, i, d1, d2: (b, i, 0)),
            scratch_shapes=[row_buf, row_buf, row_buf, row_buf, pltpu.SemaphoreType.DMA((2,))],
        ),
        compiler_params=_params(("arbitrary", "arbitrary")),
        name="moe_combine",
    )(dest1, dest2, y_buf, slab, x, g_ffn, norm_w.reshape(1, d))


def _block_diag(w):
    heads, hd, _ = w.shape
    n = heads * hd
    tiled = jnp.tile(w.reshape(n, hd), (1, heads))
    blk_r = lax.broadcasted_iota(jnp.int32, (n, n), 0) // hd
    blk_c = lax.broadcasted_iota(jnp.int32, (n, n), 1) // hd
    return jnp.where(blk_r == blk_c, tiled, 0.0)


def kernel(x, c, ada_w, ada_b, norm_mix_w, w_in, conv_w, conv_b, lru_wa, lru_ba, lru_wx, lru_bx, lru_lambda, norm_lru_w, hgrn_lb, norm_hgrn_w, w_out, norm_ffn_w, router_group_w, router_group_b, router_expert_w, router_expert_b, expert_w_gate, expert_w_up, expert_w_down, final_norm_w):
    bsz, seq, d = x.shape
    assert d == SUBLANES * LANES, "the MoE row movement keeps one (8, 128) tile per token"
    depth = ada_w.shape[0]
    d_lru = conv_w.shape[-1]
    d_hgrn = hgrn_lb.shape[-1]
    m = bsz * seq
    n_rows = m * 2 + N_EXPERTS * EXPERT_BLOCK
    n_blocks = n_rows // EXPERT_BLOCK

    mod = _modulation(c, ada_w, ada_b)
    lb_cum = jnp.cumsum(jax.nn.softmax(hgrn_lb.astype(F32), axis=0), axis=0)
    lb_all = lb_cum - lb_cum[0:1]

    for l in range(depth):
        sh_mix, sc_mix, g_mix, sh_ffn, sc_ffn, g_ffn = [
            mod[l, :, i * d:(i + 1) * d].reshape(bsz, 1, d) for i in range(6)]
        proj = _in_proj(x, norm_mix_w[l], sc_mix, sh_mix, w_in[l].astype(BF16))
        wa_bd = jnp.stack([_block_diag(lru_wa[l, 0]), _block_diag(lru_wa[l, 1])]).astype(BF16)
        wx_bd = jnp.stack([_block_diag(lru_wx[l, 0]), _block_diag(lru_wx[l, 1])]).astype(BF16)
        lru = [
            _lru_scan(proj, conv_w[l], conv_b[l], wa_bd, lru_ba[l], wx_bd, lru_bx[l], lru_lambda[l],
                      reverse=rv)
            for rv in (False, True)]
        hg_f, hg_b = _hgrn(proj, lb_all[l], d_lru=d_lru, d_hgrn=d_hgrn)

        wr = jnp.zeros((d, ROUTE_LANES), F32)
        wr = wr.at[:, :N_GROUPS].set(router_group_w[l]).at[:, N_GROUPS:N_GROUPS + N_EXPERTS].set(router_expert_w[l])
        br = jnp.zeros((1, ROUTE_LANES), F32)
        br = br.at[0, :N_GROUPS].set(router_group_b[l]).at[0, N_GROUPS:N_GROUPS + N_EXPERTS].set(router_expert_b[l])
        x_mid, h_ffn, slab, counts, route = _post_mixer(
            lru[0], lru[1], proj, hg_f, hg_b, x, norm_lru_w[l], norm_hgrn_w[l], w_out[l].astype(BF16), g_mix,
            norm_ffn_w[l], sc_ffn, sh_ffn, wr.astype(BF16), br)

        cnt = counts[0, :N_EXPERTS].astype(jnp.int32)
        padded = ((cnt + EXPERT_BLOCK - 1) // EXPERT_BLOCK) * EXPERT_BLOCK
        pend = jnp.cumsum(padded)
        pstart = pend - padded
        blk_start = jnp.arange(n_blocks, dtype=jnp.int32) * EXPERT_BLOCK
        blk_expert = jnp.minimum(jnp.sum(pend[None, :] <= blk_start[:, None], axis=1), N_EXPERTS - 1)
        blocks_used = (pend[N_EXPERTS - 1] // EXPERT_BLOCK).reshape(1)
        blk_expert = jnp.concatenate([blk_expert.astype(jnp.int32), blocks_used.astype(jnp.int32)])
        route = route.astype(jnp.int32)
        dest1 = pstart[route[0]] + route[4]
        dest2 = pstart[route[1]] + route[5]

        x_buf = _dispatch(dest1, dest2, h_ffn.reshape(m * SUBLANES, LANES), n_rows)
        y_buf = _experts(blk_expert, x_buf, expert_w_gate, expert_w_up, expert_w_down, l)
        x = _combine(dest1, dest2, y_buf, slab, x_mid, g_ffn, final_norm_w, final_norm=(l == depth - 1))

    return x
```

```python
import functools

import jax
import jax.numpy as jnp
from jax import lax
from jax.experimental import pallas as pl
from jax.experimental.pallas import tpu as pltpu

F32 = jnp.float32
BF16 = jnp.bfloat16

LRU_HEADS = 8
HGRN_HEADS = 8
CONV_WIDTH = 4
LRU_C = 8.0
N_GROUPS = 4
EXPERTS_PER_GROUP = 8
N_EXPERTS = N_GROUPS * EXPERTS_PER_GROUP
NORM_EPS = 1e-6

LANES = 128
SUBLANES = 8
VMEM_LIMIT = 56 * 1024 * 1024

HGRN_CHUNK = 64
HGRN_SUB = 8
LOG2E = 1.4426950408889634
ROUTE_LANES = LANES
EXPERT_BLOCK = 512
DMA_ISSUE_UNROLL = 8
NEG_BIG = -3.0e38


def _params(sem):
    return pltpu.CompilerParams(dimension_semantics=sem, vmem_limit_bytes=VMEM_LIMIT)


def _dot(a, b):
    return jnp.dot(a, b, preferred_element_type=F32)


def _dot_nt(a, b):
    return lax.dot_general(a, b, (((1,), (1,)), ((), ())), preferred_element_type=F32)


def _dot_tn(a, b):
    return lax.dot_general(a, b, (((0,), (0,)), ((), ())), preferred_element_type=F32)


def _dot01_exact(m01, x):
    hi = x.astype(BF16)
    r1 = x - hi.astype(F32)
    mid = r1.astype(BF16)
    lo = (r1 - mid.astype(F32)).astype(BF16)
    return _dot(m01, hi) + _dot(m01, mid) + _dot(m01, lo)


def _sigmoid(x):
    return 1.0 / (1.0 + jnp.exp(-x))


def _mod_kernel(c_ref, w_ref, b_ref, o_ref):
    c = c_ref[...]
    cond = c * _sigmoid(c)
    o_ref[0] = _dot(cond.astype(BF16), w_ref[0].astype(BF16)) + b_ref[0]


def _modulation(c, ada_w, ada_b):
    depth, d, n = ada_w.shape
    bsz = c.shape[0]
    rows = -(-bsz // SUBLANES) * SUBLANES
    c_pad = jnp.pad(c, ((0, rows - bsz), (0, 0)))
    tn = n // 6
    out = pl.pallas_call(
        _mod_kernel,
        out_shape=jax.ShapeDtypeStruct((depth, rows, n), F32),
        grid=(depth, n // tn),
        in_specs=[
            pl.BlockSpec((rows, d), lambda l, j: (0, 0)),
            pl.BlockSpec((1, d, tn), lambda l, j: (l, 0, j)),
            pl.BlockSpec((1, 1, tn), lambda l, j: (l, 0, j)),
        ],
        out_specs=pl.BlockSpec((1, rows, tn), lambda l, j: (l, 0, j)),
        compiler_params=_params(("arbitrary", "arbitrary")),
        name="adaln_mod",
    )(c_pad, ada_w, ada_b.reshape(depth, 1, n))
    return out[:, :bsz]


def _rms_mod(x, nw, sc, sh):
    ms = jnp.mean(x * x, axis=-1, keepdims=True)
    return (x * lax.rsqrt(ms + NORM_EPS) * nw) * (1.0 + sc) + sh


def _inproj_kernel(x_ref, nw_ref, sc_ref, sh_ref, w_ref, o_ref):
    h = _rms_mod(x_ref[0], nw_ref[...], sc_ref[0], sh_ref[0])
    o_ref[0] = _dot(h.astype(BF16), w_ref[...])


def _in_proj(x, nw, sc, sh, w_bf16, tm=512):
    bsz, seq, d = x.shape
    n = w_bf16.shape[1]
    return pl.pallas_call(
        _inproj_kernel,
        out_shape=jax.ShapeDtypeStruct((bsz, seq, n), F32),
        grid=(bsz, seq // tm),
        in_specs=[
            pl.BlockSpec((1, tm, d), lambda b, i: (b, i, 0)),
            pl.BlockSpec((1, d), lambda b, i: (0, 0)),
            pl.BlockSpec((1, 1, d), lambda b, i: (b, 0, 0)),
            pl.BlockSpec((1, 1, d), lambda b, i: (b, 0, 0)),
            pl.BlockSpec((d, n), lambda b, i: (0, 0)),
        ],
        out_specs=pl.BlockSpec((1, tm, n), lambda b, i: (b, i, 0)),
        compiler_params=_params(("arbitrary", "arbitrary")),
        name="in_proj",
    )(x, nw.reshape(1, d), sc, sh, w_bf16)


def _lru_kernel(x_ref, xp_ref, xn_ref, cw_ref, cb_ref, wa_ref, ba_ref, wx_ref, bx_ref, lam_ref,
                o_ref, carry_ref, sa_ref, sb_ref, cin_ref, *, reverse, n_chunks, rows):
    c = pl.program_id(1)
    chunk = (n_chunks - 1 - c) if reverse else c

    @pl.when(c == 0)
    def _():
        carry_ref[...] = jnp.zeros_like(carry_ref)

    x = x_ref[0]
    width = x.shape[1]
    row = lax.broadcasted_iota(jnp.int32, (rows, width), 0)
    has_prev = jnp.where(chunk > 0, 1.0, 0.0)
    has_next = jnp.where(chunk < n_chunks - 1, 1.0, 0.0)
    xp = xp_ref[0] * has_prev
    xn = xn_ref[0] * has_next
    xe = jnp.concatenate([xp, x, xn], axis=0)
    h8 = SUBLANES
    cw = cw_ref[...]
    xc = (cw[0:1] * xe[h8 - 2:h8 - 2 + rows] + cw[1:2] * xe[h8 - 1:h8 - 1 + rows] + cw[2:3] * x
          + cw[3:4] * xe[h8 + 1:h8 + 1 + rows] + cb_ref[...])

    xcb = xc.astype(BF16)
    r = _sigmoid(_dot(xcb, wa_ref[0]) + ba_ref[0])
    gate_i = _sigmoid(_dot(xcb, wx_ref[0]) + bx_ref[0])
    lam = lam_ref[0]
    softplus_neg_lam = jnp.maximum(-lam, 0.0) + jnp.log1p(jnp.exp(-jnp.abs(lam)))
    log_a = (-LRU_C) * r * softplus_neg_lam
    a = jnp.exp(log_a)
    t = jnp.tanh(-log_a)
    u = jnp.sqrt(2.0 * t / (1.0 + t)) * (gate_i * xc)

    groups = rows // SUBLANES
    acc_a = a.reshape(groups, SUBLANES, width)
    acc_b = u.reshape(groups, SUBLANES, width)
    sub = lax.broadcasted_iota(jnp.int32, (groups, SUBLANES, width), 1)
    s = 1
    while s < SUBLANES:
        if reverse:
            valid = sub < SUBLANES - s
            sh_a, sh_b = pltpu.roll(acc_a, SUBLANES - s, 1), pltpu.roll(acc_b, SUBLANES - s, 1)
        else:
            valid = sub >= s
            sh_a, sh_b = pltpu.roll(acc_a, s, 1), pltpu.roll(acc_b, s, 1)
        acc_b = jnp.where(valid, acc_a * sh_b + acc_b, acc_b)
        acc_a = jnp.where(valid, acc_a * sh_a, acc_a)
        s *= 2
    acc_a = acc_a.reshape(rows, width)
    acc_b = acc_b.reshape(rows, width)
    groups = rows // SUBLANES
    edge = 0 if reverse else SUBLANES - 1
    n_tiles = width // LANES
    for j in range(n_tiles):
        sa_ref[j] = acc_a[:, j * LANES:(j + 1) * LANES]
        sb_ref[j] = acc_b[:, j * LANES:(j + 1) * LANES]
    ea = jnp.concatenate([sa_ref[j, pl.ds(edge, groups, stride=SUBLANES), :] for j in range(n_tiles)], axis=1)
    eb = jnp.concatenate([sb_ref[j, pl.ds(edge, groups, stride=SUBLANES), :] for j in range(n_tiles)], axis=1)
    grow = lax.broadcasted_iota(jnp.int32, (groups, width), 0)
    s = 1
    while s < groups:
        if reverse:
            valid = grow < groups - s
            sh_a, sh_b = pltpu.roll(ea, groups - s, 0), pltpu.roll(eb, groups - s, 0)
        else:
            valid = grow >= s
            sh_a, sh_b = pltpu.roll(ea, s, 0), pltpu.roll(eb, s, 0)
        eb = jnp.where(valid, ea * sh_b + eb, eb)
        ea = jnp.where(valid, ea * sh_a, ea)
        s *= 2
    carry = carry_ref[...]
    group_out = eb + ea * carry
    if reverse:
        carry_in = jnp.where(grow == groups - 1, carry, pltpu.roll(group_out, groups - 1, 0))
        carry_ref[...] = group_out[0:1]
    else:
        carry_in = jnp.where(grow == 0, carry, pltpu.roll(group_out, 1, 0))
        carry_ref[...] = group_out[groups - 1:groups]
    cin_ref[...] = carry_in
    for g in range(groups):
        rs = slice(g * SUBLANES, (g + 1) * SUBLANES)
        o_ref[0, rs, :] = acc_b[rs] + acc_a[rs] * cin_ref[g:g + 1, :]


def _lru_scan(proj, conv_w, conv_b, wa_bd, ba, wx_bd, bx, lam, *, reverse, rows=256):
    bsz, seq, _ = proj.shape
    d_lru = conv_w.shape[1]
    n_chunks = seq // rows
    halo = rows // SUBLANES
    last_halo = seq // SUBLANES - 1
    dirn = 1 if reverse else 0

    def chunk_of(c):
        return (n_chunks - 1 - c) if reverse else c

    vec = lambda: pl.BlockSpec((1, 1, d_lru), lambda b, c: (dirn, 0, 0))
    mat = lambda: pl.BlockSpec((1, d_lru, d_lru), lambda b, c: (dirn, 0, 0))
    kern = functools.partial(_lru_kernel, reverse=reverse, n_chunks=n_chunks, rows=rows)
    return pl.pallas_call(
        kern,
        out_shape=jax.ShapeDtypeStruct((bsz, seq, d_lru), F32),
        grid=(bsz, n_chunks),
        in_specs=[
            pl.BlockSpec((1, rows, d_lru), lambda b, c: (b, chunk_of(c), 0)),
            pl.BlockSpec((1, SUBLANES, d_lru),
                         lambda b, c: (b, jnp.maximum(chunk_of(c) * halo - 1, 0), 0)),
            pl.BlockSpec((1, SUBLANES, d_lru),
                         lambda b, c: (b, jnp.minimum((chunk_of(c) + 1) * halo, last_halo), 0)),
            pl.BlockSpec((CONV_WIDTH, d_lru), lambda b, c: (0, 0)),
            pl.BlockSpec((1, d_lru), lambda b, c: (0, 0)),
            mat(), vec(), mat(), vec(), vec(),
        ],
        out_specs=pl.BlockSpec((1, rows, d_lru), lambda b, c: (b, chunk_of(c), 0)),
        scratch_shapes=[pltpu.VMEM((1, d_lru), F32), pltpu.VMEM((d_lru // LANES, rows, LANES), F32),
                        pltpu.VMEM((d_lru // LANES, rows, LANES), F32),
                        pltpu.VMEM((rows // SUBLANES, d_lru), F32)],
        compiler_params=_params(("arbitrary", "arbitrary")),
        name="lru_bwd" if reverse else "lru_fwd",
    )(proj, proj, proj, conv_w, conv_b.reshape(1, d_lru), wa_bd, ba.reshape(2, 1, d_lru),
      wx_bd, bx.reshape(2, 1, d_lru), lam.reshape(2, 1, d_lru))


def _hgrn_direction(rev, q_ref, f_ref, v_ref, lb_ref, o_ref, st_ref, diag_s, lvl_s, upd_s, qe_s, btot_s, *, rows):
    ck, sb = HGRN_CHUNK, HGRN_SUB
    n_blk = ck // sb
    sb_shift = sb.bit_length() - 1
    n_sub = rows // ck
    width = q_ref.shape[-1]
    n_pairs = width // LANES
    half = LANES // 2

    def flip(idx, n):
        return (n - 1 - idx) if rev else idx

    n_lvl = n_blk.bit_length() - 1
    tf = flip(lax.broadcasted_iota(jnp.int32, (ck, ck), 0), ck)
    uf = flip(lax.broadcasted_iota(jnp.int32, (ck, ck), 1), ck)
    tb, ub = tf >> sb_shift, uf >> sb_shift
    pb = flip(lax.broadcasted_iota(jnp.int32, (n_blk, ck), 0), n_blk)
    pub = flip(lax.broadcasted_iota(jnp.int32, (n_blk, ck), 1), ck) >> sb_shift
    mats = [jnp.where((tb == ub) & (uf <= tf), 1.0, 0.0),
            jnp.where(pub < pb, 1.0, 0.0)]
    for lvl in range(n_lvl):
        mid = ((pb >> (lvl + 1)) << (lvl + 1)) + (1 << lvl)
        mats.append(jnp.where(pub < mid, 1.0, 0.0))
    mats.append(jnp.ones((SUBLANES, ck), F32))
    m_cum = jnp.concatenate(mats, axis=0).astype(BF16)

    def per_block(rows8):
        return jnp.concatenate(
            [jnp.broadcast_to(rows8[jb:jb + 1], (sb, rows8.shape[1])) for jb in range(n_blk)], axis=0)
    row_blk = flip(lax.broadcasted_iota(jnp.int32, (ck, width), 0), ck) >> sb_shift
    upper = [((row_blk >> lvl) & 1) == 1 for lvl in range(n_lvl)]
    pr = flip(lax.broadcasted_iota(jnp.int32, (ck, LANES), 0), ck) >> sb_shift
    pc = flip(lax.broadcasted_iota(jnp.int32, (ck, LANES), 1) & (ck - 1), ck) >> sb_shift
    group_mask = [(pr >> (lvl + 1)) == (pc >> (lvl + 1)) for lvl in range(n_lvl)]
    lane = lax.broadcasted_iota(jnp.int32, (1, LANES), 1)
    head0 = lane < half
    sr = lax.broadcasted_iota(jnp.int32, (LANES, LANES), 0)
    sc = lax.broadcasted_iota(jnp.int32, (LANES, LANES), 1)
    same_head = (sr < half) == (sc < half)
    er = lax.broadcasted_iota(jnp.int32, (sb * LANES, LANES), 0)
    ec = lax.broadcasted_iota(jnp.int32, (sb * LANES, LANES), 1)
    sel = jnp.where(ec == (((er & (LANES - 1)) >> (half.bit_length() - 1)) * half + (er >> (LANES.bit_length() - 1))),
                    1.0, 0.0).astype(BF16)
    sub_row = flip(lax.broadcasted_iota(jnp.int32, (sb, LANES), 0), sb)
    lbv = lb_ref[...]

    def row_start(j):
        return pl.multiple_of(flip(j, n_sub) * ck, ck)

    def stage1a(j):
        r0 = row_start(j)
        q = q_ref[0, pl.ds(r0, ck), :]
        z = f_ref[0, pl.ds(r0, ck), :]
        v = v_ref[0, pl.ds(r0, ck), :]
        f = lbv + (1.0 - lbv) * _sigmoid(z)
        lf = jnp.log(f)
        k = 1.0 - f
        return q, v, k, _dot01_exact(m_cum, lf)

    def stage1b(q, v, k, cums):
        bl = cums[0:ck]
        b = bl + per_block(cums[ck:ck + n_blk])
        tot_row = ck + (1 + n_lvl) * n_blk
        btot = cums[tot_row:tot_row + 1]
        qe = q * jnp.exp(b)
        ke = k * jnp.exp(btot - b)
        log2_k = jnp.log(k) * LOG2E
        b2 = b * LOG2E
        kb = b2 - log2_k
        bl2 = bl * LOG2E
        kbl = bl2 - log2_k
        q_lvl, k_lvl = [], []
        for lvl in range(n_lvl):
            split2 = per_block(cums[ck + (1 + lvl) * n_blk:ck + (2 + lvl) * n_blk] * LOG2E)
            q_lvl.append(q * jnp.exp2(jnp.where(upper[lvl], b2 - split2, NEG_BIG)))
            k_lvl.append(jnp.exp2(jnp.where(upper[lvl], NEG_BIG, split2 - kb)))

        qe_s[...] = qe.astype(BF16)
        btot_s[...] = btot
        for p in range(n_pairs):
            sl = slice(p * LANES, (p + 1) * LANES)
            diag_rows = []
            for jb in range(n_blk):
                rs = slice(jb * sb, (jb + 1) * sb)
                bl_b, kbl_b, q_b = bl2[rs, sl], kbl[rs, sl], q[rs, sl]
                terms = []
                for s in range(sb):
                    arg = jnp.where(sub_row >= flip(s, sb), bl_b - kbl_b[s:s + 1], NEG_BIG)
                    terms.append(q_b * jnp.exp2(arg))
                diag_rows.append(jnp.concatenate(terms, axis=1))
            diag_s[p] = _dot(jnp.concatenate(diag_rows, axis=0).astype(BF16), sel)
            for lvl in range(n_lvl):
                k_p = k_lvl[lvl][:, sl]
                k_heads = jnp.concatenate([jnp.where(head0, k_p, 0.0), jnp.where(head0, 0.0, k_p)], axis=0)
                lvl_s[p * n_lvl + lvl] = _dot_nt(q_lvl[lvl][:, sl].astype(BF16), k_heads.astype(BF16))
            upd_s[p] = _dot_tn(v[:, sl].astype(BF16), ke[:, sl].astype(BF16))

    def stage2_issue(j):
        r0 = row_start(j)
        v = v_ref[0, pl.ds(r0, ck), :]
        out = []
        for p in range(n_pairs):
            sl = slice(p * LANES, (p + 1) * LANES)
            parts = []
            for jb in range(n_blk):
                blk = diag_s[p, jb * sb:(jb + 1) * sb, :]
                parts.append(pltpu.roll(blk, jb * sb, 1) if jb else blk)
            scores = jnp.concatenate(parts, axis=0)
            for lvl in range(n_lvl):
                scores = scores + jnp.where(group_mask[lvl], lvl_s[p * n_lvl + lvl], 0.0)
            v_p = v[:, sl]
            v_heads = jnp.concatenate([jnp.where(head0, v_p, 0.0), jnp.where(head0, 0.0, v_p)], axis=0)
            intra = _dot(scores.astype(BF16), v_heads.astype(BF16))
            st = st_ref[p]
            inter = _dot_nt(qe_s[:, sl], st.astype(BF16))
            new_st = jnp.where(same_head, st * jnp.exp(btot_s[:, sl]) + upd_s[p], 0.0)
            out.append((inter + intra, new_st))
        return r0, out

    def stage2_finish(r0, out):
        for p in range(n_pairs):
            o_ref[0, pl.ds(r0, ck), p * LANES:(p + 1) * LANES] = out[p][0]
            st_ref[p] = out[p][1]

    return stage1a, stage1b, stage2_issue, stage2_finish


N_HGRN_SCRATCH = 6


def _hgrn_kernel(qf_ref, ff_ref, vf_ref, qb_ref, fb_ref, vb_ref, lb_ref, of_ref, ob_ref, *scratch, rows):
    fwd_scratch, bwd_scratch = scratch[:N_HGRN_SCRATCH], scratch[N_HGRN_SCRATCH:]

    @pl.when(pl.program_id(1) == 0)
    def _():
        fwd_scratch[0][...] = jnp.zeros_like(fwd_scratch[0])
        bwd_scratch[0][...] = jnp.zeros_like(bwd_scratch[0])

    f1a, f1b, f2, f3 = _hgrn_direction(False, qf_ref, ff_ref, vf_ref, lb_ref, of_ref, *fwd_scratch, rows=rows)
    b1a, b1b, b2, b3 = _hgrn_direction(True, qb_ref, fb_ref, vb_ref, lb_ref, ob_ref, *bwd_scratch, rows=rows)
    n_sub = rows // HGRN_CHUNK

    def stage1_both(j):
        fa = f1a(j)
        ba = b1a(j)
        f1b(*fa)
        b1b(*ba)

    stage1_both(0)

    def pipelined(j, carry):
        fo = f2(j)
        bo = b2(j)
        stage1_both(j + 1)
        f3(*fo)
        b3(*bo)
        return carry

    lax.fori_loop(0, n_sub - 1, pipelined, 0)
    fo = f2(n_sub - 1)
    bo = b2(n_sub - 1)
    f3(*fo)
    b3(*bo)


def _hgrn(proj, lb, *, d_lru, d_hgrn, rows=512):
    bsz, seq, _ = proj.shape
    n_chunks = seq // rows
    col0 = (2 * d_lru) // d_hgrn
    n_pairs = d_hgrn // LANES
    n_lvl = (HGRN_CHUNK // HGRN_SUB).bit_length() - 1
    fwd = lambda col: pl.BlockSpec((1, rows, d_hgrn), lambda b, c: (b, c, col))
    bwd = lambda col: pl.BlockSpec((1, rows, d_hgrn), lambda b, c: (b, n_chunks - 1 - c, col))
    direction_scratch = [
        pltpu.VMEM((n_pairs, LANES, LANES), F32),
        pltpu.VMEM((n_pairs, HGRN_CHUNK, LANES), F32),
        pltpu.VMEM((n_pairs * n_lvl, HGRN_CHUNK, LANES), F32),
        pltpu.VMEM((n_pairs, LANES, LANES), F32),
        pltpu.VMEM((HGRN_CHUNK, d_hgrn), BF16),
        pltpu.VMEM((1, d_hgrn), F32),
    ]
    assert len(direction_scratch) == N_HGRN_SCRATCH
    kern = functools.partial(_hgrn_kernel, rows=rows)
    out = jax.ShapeDtypeStruct((bsz, seq, d_hgrn), F32)
    return pl.pallas_call(
        kern,
        out_shape=(out, out),
        grid=(bsz, n_chunks),
        in_specs=[fwd(col0), fwd(col0 + 1), fwd(col0 + 3), bwd(col0), bwd(col0 + 2), bwd(col0 + 3),
                  pl.BlockSpec((1, d_hgrn), lambda b, c: (0, 0))],
        out_specs=(pl.BlockSpec((1, rows, d_hgrn), lambda b, c: (b, c, 0)),
                   pl.BlockSpec((1, rows, d_hgrn), lambda b, c: (b, n_chunks - 1 - c, 0))),
        scratch_shapes=direction_scratch + direction_scratch,
        compiler_params=_params(("arbitrary", "arbitrary")),
        name="hgrn2",
    )(proj, proj, proj, proj, proj, proj, lb.reshape(1, d_hgrn))


def _gelu_tanh(y):
    return 0.5 * y * (1.0 + jnp.tanh(0.7978845608028654 * (y + 0.044715 * (y * y * y))))


def _post_kernel(lf_ref, lb_ref, y_ref, of_ref, ob_ref, g_ref, x_ref, nlw_ref, nhw_ref, wo_ref,
                 gm_ref, nfw_ref, scf_ref, shf_ref, wr_ref, br_ref,
                 xo_ref, h_ref, slab_ref, cnt_ref, route_ref, carry_ref, *, tm, d_lru):
    first = (pl.program_id(0) == 0) & (pl.program_id(1) == 0)

    @pl.when(first)
    def _():
        carry_ref[...] = jnp.zeros_like(carry_ref)

    lru = (lf_ref[0] + lb_ref[0]) * _gelu_tanh(y_ref[0])
    ms = jnp.mean(lru * lru, axis=-1, keepdims=True)
    lru = lru * lax.rsqrt(ms + NORM_EPS) * nlw_ref[...]

    hg = of_ref[0] + ob_ref[0]
    width = hg.shape[1]
    hd = width // HGRN_HEADS
    hd_shift = hd.bit_length() - 1
    er = lax.broadcasted_iota(jnp.int32, (width, width), 0) >> hd_shift
    ec = lax.broadcasted_iota(jnp.int32, (width, width), 1) >> hd_shift
    head_sum = jnp.where(er == ec, 1.0, 0.0).astype(BF16)
    sq = hg * hg
    sq_hi = sq.astype(BF16)
    sq_lo = (sq - sq_hi.astype(F32)).astype(BF16)
    ms_h = (_dot(sq_hi, head_sum) + _dot(sq_lo, head_sum)) * (1.0 / hd)
    g = g_ref[0]
    hg = (hg * lax.rsqrt(ms_h + NORM_EPS) * nhw_ref[...]) * (g * _sigmoid(g))

    mixed = _dot(lru.astype(BF16), wo_ref[0:d_lru, :]) + _dot(hg.astype(BF16), wo_ref[d_lru:, :])
    x_new = x_ref[0] + gm_ref[0] * mixed
    xo_ref[0] = x_new

    h = _rms_mod(x_new, nfw_ref[...], scf_ref[0], shf_ref[0])
    _tiles_store(h_ref, h, tm, lead=(0,))

    logits = _dot(h.astype(BF16), wr_ref[...]) + br_ref[...]
    lane = lax.broadcasted_iota(jnp.int32, (tm, ROUTE_LANES), 1)
    lane_f = lane.astype(F32)
    far = float(ROUTE_LANES)
    is_g = lane < N_GROUPS
    gl = jnp.where(is_g, logits, NEG_BIG)
    gmax = jnp.max(gl, axis=-1, keepdims=True)
    g_idx = jnp.min(jnp.where(gl == gmax, lane_f, far), axis=-1, keepdims=True)
    p_group = 1.0 / jnp.sum(jnp.where(is_g, jnp.exp(gl - gmax), 0.0), axis=-1, keepdims=True)
    e_lane = lane - N_GROUPS
    in_group = (e_lane >= 0) & (e_lane < N_EXPERTS) & ((e_lane >> (EXPERTS_PER_GROUP.bit_length() - 1)).astype(F32) == g_idx)
    ev = jnp.where(in_group, logits, NEG_BIG)
    top1 = jnp.max(ev, axis=-1, keepdims=True)
    i1 = jnp.min(jnp.where(in_group & (ev == top1), lane_f, far), axis=-1, keepdims=True)
    rest = in_group & (lane_f != i1)
    ev2 = jnp.where(rest, logits, NEG_BIG)
    top2 = jnp.max(ev2, axis=-1, keepdims=True)
    i2 = jnp.min(jnp.where(rest & (ev2 == top2), lane_f, far), axis=-1, keepdims=True)
    e1 = i1 - float(N_GROUPS)
    e2 = i2 - float(N_GROUPS)
    ex = jnp.exp(top2 - top1)
    w1 = p_group / (1.0 + ex)
    w2 = p_group * ex / (1.0 + ex)

    sel1 = lane_f == e1
    sel2 = lane_f == e2
    onehot = jnp.where(sel1 | sel2, 1.0, 0.0)
    tr = lax.broadcasted_iota(jnp.int32, (tm, tm), 0)
    tc = lax.broadcasted_iota(jnp.int32, (tm, tm), 1)
    before = jnp.where(tc < tr, 1.0, 0.0).astype(BF16)
    cnt = _dot(before, onehot.astype(BF16)) + carry_ref[0:1]
    rank1 = jnp.sum(jnp.where(sel1, cnt, 0.0), axis=-1, keepdims=True)
    rank2 = jnp.sum(jnp.where(sel2, cnt, 0.0), axis=-1, keepdims=True)
    total = carry_ref[0:1] + jnp.sum(onehot, axis=0, keepdims=True)
    carry_ref[...] = jnp.broadcast_to(total, carry_ref.shape)
    cnt_ref[...] = jnp.broadcast_to(total, cnt_ref.shape)

    slab = jnp.where(lane == 0, e1, 0.0)
    slab = jnp.where(lane == 1, e2, slab)
    slab = jnp.where(lane == 2, w1, slab)
    slab = jnp.where(lane == 3, w2, slab)
    slab = jnp.where(lane == 4, rank1, slab)
    slab = jnp.where(lane == 5, rank2, slab)
    slab_ref[0] = slab
    route_ref[...] = slab.T[0:SUBLANES]


def _post_mixer(lru_f, lru_b, proj, hg_f, hg_b, x, nlw, nhw, wo_bf16, g_mix, nfw, sc_ffn, sh_ffn, wr_bf16, br,
                *, tm=512):
    bsz, seq, d = x.shape
    d_lru = lru_f.shape[-1]
    d_hgrn = hg_f.shape[-1]
    y_col = 1
    g_col = (2 * d_lru) // d_hgrn + 4
    row = lambda w: pl.BlockSpec((1, tm, w), lambda b, i: (b, i, 0))
    vec = lambda w: pl.BlockSpec((1, w), lambda b, i: (0, 0))
    per_b = lambda: pl.BlockSpec((1, 1, d), lambda b, i: (b, 0, 0))
    kern = functools.partial(_post_kernel, tm=tm, d_lru=d_lru)
    return pl.pallas_call(
        kern,
        out_shape=(
            jax.ShapeDtypeStruct((bsz, seq, d), F32),
            jax.ShapeDtypeStruct((bsz, seq * SUBLANES, LANES), F32),
            jax.ShapeDtypeStruct((bsz, seq, ROUTE_LANES), F32),
            jax.ShapeDtypeStruct((SUBLANES, ROUTE_LANES), F32),
            jax.ShapeDtypeStruct((SUBLANES, bsz * seq), F32),
        ),
        grid=(bsz, seq // tm),
        in_specs=[
            row(d_lru), row(d_lru),
            pl.BlockSpec((1, tm, d_lru), lambda b, i: (b, i, y_col)),
            row(d_hgrn), row(d_hgrn),
            pl.BlockSpec((1, tm, d_hgrn), lambda b, i: (b, i, g_col)),
            row(d), vec(d_lru), vec(d_hgrn),
            pl.BlockSpec((d, d), lambda b, i: (0, 0)),
            per_b(), vec(d), per_b(), per_b(),
            pl.BlockSpec((d, ROUTE_LANES), lambda b, i: (0, 0)),
            vec(ROUTE_LANES),
        ],
        out_specs=(
            row(d), pl.BlockSpec((1, tm * SUBLANES, LANES), lambda b, i: (b, i, 0)), row(ROUTE_LANES),
            pl.BlockSpec((SUBLANES, ROUTE_LANES), lambda b, i: (0, 0)),
            pl.BlockSpec((SUBLANES, tm), lambda b, i: (0, b * (seq // tm) + i)),
        ),
        scratch_shapes=[pltpu.VMEM((SUBLANES, ROUTE_LANES), F32)],
        compiler_params=_params(("arbitrary", "arbitrary")),
        name="post_mixer_router",
    )(lru_f, lru_b, proj, hg_f, hg_b, proj, x, nlw.reshape(1, d_lru), nhw.reshape(1, d_hgrn), wo_bf16,
      g_mix, nfw.reshape(1, d), sc_ffn, sh_ffn, wr_bf16, br)


def _tiles_load(ref, n, lead=()):
    return jnp.concatenate(
        [ref[(*lead, pl.ds(j, n, stride=SUBLANES), slice(None))] for j in range(SUBLANES)], axis=1)


def _tiles_store(ref, val, n, lead=()):
    for j in range(SUBLANES):
        ref[(*lead, pl.ds(j, n, stride=SUBLANES), slice(None))] = val[:, j * LANES:(j + 1) * LANES]


def _token_tile(ref, t):
    return ref.at[pl.ds(pl.multiple_of(t * SUBLANES, SUBLANES), SUBLANES)]


def _dispatch_kernel(d1_ref, d2_ref, h_ref, z_ref, o_ref, sem, *, tb):
    del z_ref
    base = pl.program_id(0) * tb

    def issue(r, carry):
        t = base + r
        pltpu.make_async_copy(_token_tile(h_ref, r), _token_tile(o_ref, d1_ref[t]), sem).start(priority=0)
        pltpu.make_async_copy(_token_tile(h_ref, r), _token_tile(o_ref, d2_ref[t]), sem).start(priority=1)
        return carry

    lax.fori_loop(0, tb, issue, 0, unroll=DMA_ISSUE_UNROLL)
    for _ in range(2):
        pltpu.make_async_copy(h_ref, o_ref.at[pl.ds(0, tb * SUBLANES)], sem).wait()


def _dispatch(dest1, dest2, h_tiles, n_rows, *, tb=256):
    m = h_tiles.shape[0] // SUBLANES
    kern = functools.partial(_dispatch_kernel, tb=tb)
    return pl.pallas_call(
        kern,
        out_shape=jax.ShapeDtypeStruct((n_rows * SUBLANES, LANES), h_tiles.dtype),
        grid_spec=pltpu.PrefetchScalarGridSpec(
            num_scalar_prefetch=2,
            grid=(m // tb,),
            in_specs=[pl.BlockSpec((tb * SUBLANES, LANES), lambda i, d1, d2: (i, 0)),
                      pl.BlockSpec(memory_space=pl.ANY)],
            out_specs=pl.BlockSpec(memory_space=pl.ANY),
            scratch_shapes=[pltpu.SemaphoreType.DMA(())],
        ),
        input_output_aliases={3: 0},
        compiler_params=pltpu.CompilerParams(dimension_semantics=("arbitrary",), has_side_effects=True),
        name="moe_dispatch",
    )(dest1, dest2, h_tiles, jnp.zeros((n_rows * SUBLANES, LANES), h_tiles.dtype))


def _expert_kernel(be_ref, x_ref, wg_ref, wu_ref, wd_ref, o_ref, wg_s, wu_s, wd_s):
    i = pl.program_id(0)
    prev_expert = be_ref[jnp.maximum(i - 1, 0)]

    @pl.when((i == 0) | (be_ref[i] != prev_expert))
    def _():
        wg_s[...] = wg_ref[0, 0].astype(BF16)
        wu_s[...] = wu_ref[0, 0].astype(BF16)
        wd_s[...] = wd_ref[0, 0].astype(BF16)

    blk = x_ref.shape[0] // SUBLANES
    n_used = be_ref[pl.num_programs(0)]

    @pl.when(i < n_used)
    def _():
        x = _tiles_load(x_ref, blk).astype(BF16)
        gate = _dot(x, wg_s[...])
        up = _dot(x, wu_s[...])
        act = (gate * _sigmoid(gate)) * up
        _tiles_store(o_ref, _dot(act.astype(BF16), wd_s[...]), blk)

    @pl.when(i >= n_used)
    def _():
        o_ref[...] = jnp.zeros_like(o_ref)


def _experts(blk_expert, x_tiles, wg, wu, wd, layer):
    n_rows = x_tiles.shape[0] // SUBLANES
    d, de = wg.shape[-2:]
    blk = EXPERT_BLOCK
    n_blocks = n_rows // blk
    return pl.pallas_call(
        _expert_kernel,
        out_shape=jax.ShapeDtypeStruct((n_rows * SUBLANES, LANES), F32),
        grid_spec=pltpu.PrefetchScalarGridSpec(
            num_scalar_prefetch=1,
            grid=(n_blocks,),
            in_specs=[
                pl.BlockSpec((blk * SUBLANES, LANES), lambda i, be: (jnp.minimum(i, be[n_blocks] - 1), 0)),
                pl.BlockSpec((1, 1, d, de), lambda i, be: (layer, be[i], 0, 0)),
                pl.BlockSpec((1, 1, d, de), lambda i, be: (layer, be[i], 0, 0)),
                pl.BlockSpec((1, 1, de, d), lambda i, be: (layer, be[i], 0, 0)),
            ],
            out_specs=pl.BlockSpec((blk * SUBLANES, LANES), lambda i, be: (i, 0)),
            scratch_shapes=[pltpu.VMEM((d, de), BF16), pltpu.VMEM((d, de), BF16), pltpu.VMEM((de, d), BF16)],
        ),
        compiler_params=_params(("arbitrary",)),
        name="moe_experts",
    )(blk_expert, x_tiles, wg, wu, wd)


def _combine_kernel(d1_ref, d2_ref, y_ref, slab_ref, x_ref, g_ref, nw_ref, o_ref, ra0, rb0, ra1, rb1, sem,
                    *, tm, tiles, n_steps, final_norm):
    step = pl.program_id(0) * tiles + pl.program_id(1)
    bufs = ((ra0, rb0), (ra1, rb1))

    def gather(tile, slot):
        base = tile * tm
        r1_ref, r2_ref = bufs[slot]

        def issue(r, carry):
            t = base + r
            pltpu.make_async_copy(_token_tile(y_ref, d1_ref[t]), _token_tile(r1_ref, r),
                                  sem.at[slot]).start(priority=0)
            pltpu.make_async_copy(_token_tile(y_ref, d2_ref[t]), _token_tile(r2_ref, r),
                                  sem.at[slot]).start(priority=1)
            return carry

        lax.fori_loop(0, tm, issue, 0, unroll=DMA_ISSUE_UNROLL)

    @pl.when(step == 0)
    def _():
        gather(0, 0)

    for slot in range(2):
        @pl.when((step & 1) == slot)
        def _():
            @pl.when(step + 1 < n_steps)
            def _():
                gather(step + 1, 1 - slot)

            r1_ref, r2_ref = bufs[slot]
            pltpu.make_async_copy(y_ref.at[pl.ds(0, tm * SUBLANES)], r1_ref, sem.at[slot]).wait()
            pltpu.make_async_copy(y_ref.at[pl.ds(0, tm * SUBLANES)], r2_ref, sem.at[slot]).wait()
            slab = slab_ref[0]
            y = slab[:, 2:3] * _tiles_load(r1_ref, tm) + slab[:, 3:4] * _tiles_load(r2_ref, tm)
            out = x_ref[0] + g_ref[0] * y
            if final_norm:
                ms = jnp.mean(out * out, axis=-1, keepdims=True)
                out = out * lax.rsqrt(ms + NORM_EPS) * nw_ref[...]
            o_ref[0] = out


def _combine(dest1, dest2, y_buf, slab, x, g_ffn, norm_w, *, final_norm, tm=256):
    bsz, seq, d = x.shape
    tiles = seq // tm
    kern = functools.partial(_combine_kernel, tm=tm, tiles=tiles, n_steps=bsz * tiles, final_norm=final_norm)
    row_buf = pltpu.VMEM((tm * SUBLANES, LANES), F32)
    return pl.pallas_call(
        kern,
        out_shape=jax.ShapeDtypeStruct((bsz, seq, d), F32),
        grid_spec=pltpu.PrefetchScalarGridSpec(
            num_scalar_prefetch=2,
            grid=(bsz, tiles),
            in_specs=[
                pl.BlockSpec(memory_space=pl.ANY),
                pl.BlockSpec((1, tm, ROUTE_LANES), lambda b, i, d1, d2: (b, i, 0)),
                pl.BlockSpec((1, tm, d), lambda b, i, d1, d2: (b, i, 0)),
                pl.BlockSpec((1, 1, d), lambda b, i, d1, d2: (b, 0, 0)),
                pl.BlockSpec((1, d), lambda b, i, d1, d2: (0, 0)),
            ],
            out_specs=pl.BlockSpec((1, tm, d), lambda b, i, d1, d2: (b, i, 0)),
            scratch_shapes=[row_buf, row_buf, row_buf, row_buf, pltpu.SemaphoreType.DMA((2,))],
        ),
        compiler_params=_params(("arbitrary", "arbitrary")),
        name="moe_combine",
    )(dest1, dest2, y_buf, slab, x, g_ffn, norm_w.reshape(1, d))


def _block_diag(w):
    heads, hd, _ = w.shape
    n = heads * hd
    tiled = jnp.tile(w.reshape(n, hd), (1, heads))
    blk_r = lax.broadcasted_iota(jnp.int32, (n, n), 0) // hd
    blk_c = lax.broadcasted_iota(jnp.int32, (n, n), 1) // hd
    return jnp.where(blk_r == blk_c, tiled, 0.0)


def kernel(x, c, ada_w, ada_b, norm_mix_w, w_in, conv_w, conv_b, lru_wa, lru_ba, lru_wx, lru_bx, lru_lambda, norm_lru_w, hgrn_lb, norm_hgrn_w, w_out, norm_ffn_w, router_group_w, router_group_b, router_expert_w, router_expert_b, expert_w_gate, expert_w_up, expert_w_down, final_norm_w):
    bsz, seq, d = x.shape
    assert d == SUBLANES * LANES, "the MoE row movement keeps one (8, 128) tile per token"
    depth = ada_w.shape[0]
    d_lru = conv_w.shape[-1]
    d_hgrn = hgrn_lb.shape[-1]
    m = bsz * seq
    n_rows = m * 2 + N_EXPERTS * EXPERT_BLOCK
    n_blocks = n_rows // EXPERT_BLOCK

    mod = _modulation(c, ada_w, ada_b)
    lb_cum = jnp.cumsum(jax.nn.softmax(hgrn_lb.astype(F32), axis=0), axis=0)
    lb_all = lb_cum - lb_cum[0:1]

    for l in range(depth):
        sh_mix, sc_mix, g_mix, sh_ffn, sc_ffn, g_ffn = [
            mod[l, :, i * d:(i + 1) * d].reshape(bsz, 1, d) for i in range(6)]
        proj = _in_proj(x, norm_mix_w[l], sc_mix, sh_mix, w_in[l].astype(BF16))
        wa_bd = jnp.stack([_block_diag(lru_wa[l, 0]), _block_diag(lru_wa[l, 1])]).astype(BF16)
        wx_bd = jnp.stack([_block_diag(lru_wx[l, 0]), _block_diag(lru_wx[l, 1])]).astype(BF16)
        lru = [
            _lru_scan(proj, conv_w[l], conv_b[l], wa_bd, lru_ba[l], wx_bd, lru_bx[l], lru_lambda[l],
                      reverse=rv)
            for rv in (False, True)]
        hg_f, hg_b = _hgrn(proj, lb_all[l], d_lru=d_lru, d_hgrn=d_hgrn)

        wr = jnp.zeros((d, ROUTE_LANES), F32)
        wr = wr.at[:, :N_GROUPS].set(router_group_w[l]).at[:, N_GROUPS:N_GROUPS + N_EXPERTS].set(router_expert_w[l])
        br = jnp.zeros((1, ROUTE_LANES), F32)
        br = br.at[0, :N_GROUPS].set(router_group_b[l]).at[0, N_GROUPS:N_GROUPS + N_EXPERTS].set(router_expert_b[l])
        x_mid, h_ffn, slab, counts, route = _post_mixer(
            lru[0], lru[1], proj, hg_f, hg_b, x, norm_lru_w[l], norm_hgrn_w[l], w_out[l].astype(BF16), g_mix,
            norm_ffn_w[l], sc_ffn, sh_ffn, wr.astype(BF16), br)

        cnt = counts[0, :N_EXPERTS].astype(jnp.int32)
        padded = ((cnt + EXPERT_BLOCK - 1) // EXPERT_BLOCK) * EXPERT_BLOCK
        pend = jnp.cumsum(padded)
        pstart = pend - padded
        blk_start = jnp.arange(n_blocks, dtype=jnp.int32) * EXPERT_BLOCK
        blk_expert = jnp.minimum(jnp.sum(pend[None, :] <= blk_start[:, None], axis=1), N_EXPERTS - 1)
        blocks_used = (pend[N_EXPERTS - 1] // EXPERT_BLOCK).reshape(1)
        blk_expert = jnp.concatenate([blk_expert.astype(jnp.int32), blocks_used.astype(jnp.int32)])
        route = route.astype(jnp.int32)
        dest1 = pstart[route[0]] + route[4]
        dest2 = pstart[route[1]] + route[5]

        x_buf = _dispatch(dest1, dest2, h_ffn.reshape(m * SUBLANES, LANES), n_rows)
        y_buf = _experts(blk_expert, x_buf, expert_w_gate, expert_w_up, expert_w_down, l)
        x = _combine(dest1, dest2, y_buf, slab, x_mid, g_ffn, final_norm_w, final_norm=(l == depth - 1))

    return x
```

```python
import functools

import jax
import jax.numpy as jnp
from jax import lax
from jax.experimental import pallas as pl
from jax.experimental.pallas import tpu as pltpu

F32 = jnp.float32
BF16 = jnp.bfloat16

LRU_HEADS = 8
HGRN_HEADS = 8
CONV_WIDTH = 4
LRU_C = 8.0
N_GROUPS = 4
EXPERTS_PER_GROUP = 8
N_EXPERTS = N_GROUPS * EXPERTS_PER_GROUP
NORM_EPS = 1e-6

LANES = 128
SUBLANES = 8
VMEM_LIMIT = 56 * 1024 * 1024

HGRN_CHUNK = 64
HGRN_SUB = 8
LOG2E = 1.4426950408889634
ROUTE_LANES = LANES
EXPERT_BLOCK = 512
DMA_ISSUE_UNROLL = 8
NEG_BIG = -3.0e38


def _params(sem):
    return pltpu.CompilerParams(dimension_semantics=sem, vmem_limit_bytes=VMEM_LIMIT)


def _dot(a, b):
    return jnp.dot(a, b, preferred_element_type=F32)


def _dot_nt(a, b):
    return lax.dot_general(a, b, (((1,), (1,)), ((), ())), preferred_element_type=F32)


def _dot_tn(a, b):
    return lax.dot_general(a, b, (((0,), (0,)), ((), ())), preferred_element_type=F32)


def _dot01_exact(m01, x):
    hi = x.astype(BF16)
    r1 = x - hi.astype(F32)
    mid = r1.astype(BF16)
    lo = (r1 - mid.astype(F32)).astype(BF16)
    return _dot(m01, hi) + _dot(m01, mid) + _dot(m01, lo)


def _sigmoid(x):
    return 1.0 / (1.0 + jnp.exp(-x))


def _mod_kernel(c_ref, w_ref, b_ref, o_ref):
    c = c_ref[...]
    cond = c * _sigmoid(c)
    o_ref[0] = _dot(cond.astype(BF16), w_ref[0].astype(BF16)) + b_ref[0]


def _modulation(c, ada_w, ada_b):
    depth, d, n = ada_w.shape
    bsz = c.shape[0]
    rows = -(-bsz // SUBLANES) * SUBLANES
    c_pad = jnp.pad(c, ((0, rows - bsz), (0, 0)))
    tn = n // 6
    out = pl.pallas_call(
        _mod_kernel,
        out_shape=jax.ShapeDtypeStruct((depth, rows, n), F32),
        grid=(depth, n // tn),
        in_specs=[
            pl.BlockSpec((rows, d), lambda l, j: (0, 0)),
            pl.BlockSpec((1, d, tn), lambda l, j: (l, 0, j)),
            pl.BlockSpec((1, 1, tn), lambda l, j: (l, 0, j)),
        ],
        out_specs=pl.BlockSpec((1, rows, tn), lambda l, j: (l, 0, j)),
        compiler_params=_params(("arbitrary", "arbitrary")),
        name="adaln_mod",
    )(c_pad, ada_w, ada_b.reshape(depth, 1, n))
    return out[:, :bsz]


def _rms_mod(x, nw, sc, sh):
    ms = jnp.mean(x * x, axis=-1, keepdims=True)
    return (x * lax.rsqrt(ms + NORM_EPS) * nw) * (1.0 + sc) + sh


def _inproj_kernel(x_ref, nw_ref, sc_ref, sh_ref, w_ref, o_ref, w_s):
    @pl.when((pl.program_id(0) == 0) & (pl.program_id(1) == 0))
    def _():
        w_s[...] = w_ref[0].astype(BF16)

    h = _rms_mod(x_ref[0], nw_ref[...], sc_ref[0], sh_ref[0])
    o_ref[0] = _dot(h.astype(BF16), w_s[...])


def _in_proj(x, nw, sc, sh, w_in, layer, tm=512):
    bsz, seq, d = x.shape
    n = w_in.shape[-1]
    return pl.pallas_call(
        _inproj_kernel,
        out_shape=jax.ShapeDtypeStruct((bsz, seq, n), F32),
        grid=(bsz, seq // tm),
        in_specs=[
            pl.BlockSpec((1, tm, d), lambda b, i: (b, i, 0)),
            pl.BlockSpec((1, d), lambda b, i: (0, 0)),
            pl.BlockSpec((1, 1, d), lambda b, i: (b, 0, 0)),
            pl.BlockSpec((1, 1, d), lambda b, i: (b, 0, 0)),
            pl.BlockSpec((1, d, n), lambda b, i: (layer, 0, 0), pipeline_mode=pl.Buffered(1)),
        ],
        out_specs=pl.BlockSpec((1, tm, n), lambda b, i: (b, i, 0)),
        scratch_shapes=[pltpu.VMEM((d, n), BF16)],
        compiler_params=_params(("arbitrary", "arbitrary")),
        name="in_proj",
    )(x, nw.reshape(1, d), sc, sh, w_in)


def _lru_kernel(x_ref, xp_ref, xn_ref, cw_ref, cb_ref, wa_ref, ba_ref, wx_ref, bx_ref, lam_ref,
                o_ref, carry_ref, sa_ref, sb_ref, cin_ref, *, reverse, n_chunks, rows):
    c = pl.program_id(1)
    chunk = (n_chunks - 1 - c) if reverse else c

    @pl.when(c == 0)
    def _():
        carry_ref[...] = jnp.zeros_like(carry_ref)

    x = x_ref[0]
    width = x.shape[1]
    row = lax.broadcasted_iota(jnp.int32, (rows, width), 0)
    has_prev = jnp.where(chunk > 0, 1.0, 0.0)
    has_next = jnp.where(chunk < n_chunks - 1, 1.0, 0.0)
    xp = xp_ref[0] * has_prev
    xn = xn_ref[0] * has_next
    xe = jnp.concatenate([xp, x, xn], axis=0)
    h8 = SUBLANES
    cw = cw_ref[...]
    xc = (cw[0:1] * xe[h8 - 2:h8 - 2 + rows] + cw[1:2] * xe[h8 - 1:h8 - 1 + rows] + cw[2:3] * x
          + cw[3:4] * xe[h8 + 1:h8 + 1 + rows] + cb_ref[...])

    xcb = xc.astype(BF16)
    r = _sigmoid(_dot(xcb, wa_ref[0]) + ba_ref[0])
    gate_i = _sigmoid(_dot(xcb, wx_ref[0]) + bx_ref[0])
    lam = lam_ref[0]
    softplus_neg_lam = jnp.maximum(-lam, 0.0) + jnp.log1p(jnp.exp(-jnp.abs(lam)))
    log_a = (-LRU_C) * r * softplus_neg_lam
    a = jnp.exp(log_a)
    t = jnp.tanh(-log_a)
    u = jnp.sqrt(2.0 * t / (1.0 + t)) * (gate_i * xc)

    groups = rows // SUBLANES
    acc_a = a.reshape(groups, SUBLANES, width)
    acc_b = u.reshape(groups, SUBLANES, width)
    sub = lax.broadcasted_iota(jnp.int32, (groups, SUBLANES, width), 1)
    s = 1
    while s < SUBLANES:
        if reverse:
            valid = sub < SUBLANES - s
            sh_a, sh_b = pltpu.roll(acc_a, SUBLANES - s, 1), pltpu.roll(acc_b, SUBLANES - s, 1)
        else:
            valid = sub >= s
            sh_a, sh_b = pltpu.roll(acc_a, s, 1), pltpu.roll(acc_b, s, 1)
        acc_b = jnp.where(valid, acc_a * sh_b + acc_b, acc_b)
        acc_a = jnp.where(valid, acc_a * sh_a, acc_a)
        s *= 2
    acc_a = acc_a.reshape(rows, width)
    acc_b = acc_b.reshape(rows, width)
    groups = rows // SUBLANES
    edge = 0 if reverse else SUBLANES - 1
    n_tiles = width // LANES
    for j in range(n_tiles):
        sa_ref[j] = acc_a[:, j * LANES:(j + 1) * LANES]
        sb_ref[j] = acc_b[:, j * LANES:(j + 1) * LANES]
    ea = jnp.concatenate([sa_ref[j, pl.ds(edge, groups, stride=SUBLANES), :] for j in range(n_tiles)], axis=1)
    eb = jnp.concatenate([sb_ref[j, pl.ds(edge, groups, stride=SUBLANES), :] for j in range(n_tiles)], axis=1)
    grow = lax.broadcasted_iota(jnp.int32, (groups, width), 0)
    s = 1
    while s < groups:
        if reverse:
            valid = grow < groups - s
            sh_a, sh_b = pltpu.roll(ea, groups - s, 0), pltpu.roll(eb, groups - s, 0)
        else:
            valid = grow >= s
            sh_a, sh_b = pltpu.roll(ea, s, 0), pltpu.roll(eb, s, 0)
        eb = jnp.where(valid, ea * sh_b + eb, eb)
        ea = jnp.where(valid, ea * sh_a, ea)
        s *= 2
    carry = carry_ref[...]
    group_out = eb + ea * carry
    if reverse:
        carry_in = jnp.where(grow == groups - 1, carry, pltpu.roll(group_out, groups - 1, 0))
        carry_ref[...] = group_out[0:1]
    else:
        carry_in = jnp.where(grow == 0, carry, pltpu.roll(group_out, 1, 0))
        carry_ref[...] = group_out[groups - 1:groups]
    cin_ref[...] = carry_in
    for g in range(groups):
        rs = slice(g * SUBLANES, (g + 1) * SUBLANES)
        o_ref[0, rs, :] = acc_b[rs] + acc_a[rs] * cin_ref[g:g + 1, :]


def _lru_scan(proj, conv_w, conv_b, wa_bd, ba, wx_bd, bx, lam, *, reverse, rows=256):
    bsz, seq, _ = proj.shape
    d_lru = conv_w.shape[1]
    n_chunks = seq // rows
    halo = rows // SUBLANES
    last_halo = seq // SUBLANES - 1
    dirn = 1 if reverse else 0

    def chunk_of(c):
        return (n_chunks - 1 - c) if reverse else c

    vec = lambda: pl.BlockSpec((1, 1, d_lru), lambda b, c: (dirn, 0, 0))
    mat = lambda: pl.BlockSpec((1, d_lru, d_lru), lambda b, c: (dirn, 0, 0))
    kern = functools.partial(_lru_kernel, reverse=reverse, n_chunks=n_chunks, rows=rows)
    return pl.pallas_call(
        kern,
        out_shape=jax.ShapeDtypeStruct((bsz, seq, d_lru), F32),
        grid=(bsz, n_chunks),
        in_specs=[
            pl.BlockSpec((1, rows, d_lru), lambda b, c: (b, chunk_of(c), 0)),
            pl.BlockSpec((1, SUBLANES, d_lru),
                         lambda b, c: (b, jnp.maximum(chunk_of(c) * halo - 1, 0), 0)),
            pl.BlockSpec((1, SUBLANES, d_lru),
                         lambda b, c: (b, jnp.minimum((chunk_of(c) + 1) * halo, last_halo), 0)),
            pl.BlockSpec((CONV_WIDTH, d_lru), lambda b, c: (0, 0)),
            pl.BlockSpec((1, d_lru), lambda b, c: (0, 0)),
            mat(), vec(), mat(), vec(), vec(),
        ],
        out_specs=pl.BlockSpec((1, rows, d_lru), lambda b, c: (b, chunk_of(c), 0)),
        scratch_shapes=[pltpu.VMEM((1, d_lru), F32), pltpu.VMEM((d_lru // LANES, rows, LANES), F32),
                        pltpu.VMEM((d_lru // LANES, rows, LANES), F32),
                        pltpu.VMEM((rows // SUBLANES, d_lru), F32)],
        compiler_params=_params(("arbitrary", "arbitrary")),
        name="lru_bwd" if reverse else "lru_fwd",
    )(proj, proj, proj, conv_w, conv_b.reshape(1, d_lru), wa_bd, ba.reshape(2, 1, d_lru),
      wx_bd, bx.reshape(2, 1, d_lru), lam.reshape(2, 1, d_lru))


def _hgrn_direction(rev, q_ref, f_ref, v_ref, lb_ref, o_ref, st_ref, diag_s, lvl_s, upd_s, qe_s, btot_s, *, rows):
    ck, sb = HGRN_CHUNK, HGRN_SUB
    n_blk = ck // sb
    sb_shift = sb.bit_length() - 1
    n_sub = rows // ck
    width = q_ref.shape[-1]
    n_pairs = width // LANES
    half = LANES // 2

    def flip(idx, n):
        return (n - 1 - idx) if rev else idx

    n_lvl = n_blk.bit_length() - 1
    tf = flip(lax.broadcasted_iota(jnp.int32, (ck, ck), 0), ck)
    uf = flip(lax.broadcasted_iota(jnp.int32, (ck, ck), 1), ck)
    tb, ub = tf >> sb_shift, uf >> sb_shift
    pb = flip(lax.broadcasted_iota(jnp.int32, (n_blk, ck), 0), n_blk)
    pub = flip(lax.broadcasted_iota(jnp.int32, (n_blk, ck), 1), ck) >> sb_shift
    mats = [jnp.where((tb == ub) & (uf <= tf), 1.0, 0.0),
            jnp.where(pub < pb, 1.0, 0.0)]
    for lvl in range(n_lvl):
        mid = ((pb >> (lvl + 1)) << (lvl + 1)) + (1 << lvl)
        mats.append(jnp.where(pub < mid, 1.0, 0.0))
    mats.append(jnp.ones((SUBLANES, ck), F32))
    m_cum = jnp.concatenate(mats, axis=0).astype(BF16)

    def per_block(rows8):
        return jnp.concatenate(
            [jnp.broadcast_to(rows8[jb:jb + 1], (sb, rows8.shape[1])) for jb in range(n_blk)], axis=0)
    row_blk = flip(lax.broadcasted_iota(jnp.int32, (ck, width), 0), ck) >> sb_shift
    upper = [((row_blk >> lvl) & 1) == 1 for lvl in range(n_lvl)]
    pr = flip(lax.broadcasted_iota(jnp.int32, (ck, LANES), 0), ck) >> sb_shift
    pc = flip(lax.broadcasted_iota(jnp.int32, (ck, LANES), 1) & (ck - 1), ck) >> sb_shift
    group_mask = [(pr >> (lvl + 1)) == (pc >> (lvl + 1)) for lvl in range(n_lvl)]
    lane = lax.broadcasted_iota(jnp.int32, (1, LANES), 1)
    head0 = lane < half
    sr = lax.broadcasted_iota(jnp.int32, (LANES, LANES), 0)
    sc = lax.broadcasted_iota(jnp.int32, (LANES, LANES), 1)
    same_head = (sr < half) == (sc < half)
    er = lax.broadcasted_iota(jnp.int32, (sb * LANES, LANES), 0)
    ec = lax.broadcasted_iota(jnp.int32, (sb * LANES, LANES), 1)
    sel = jnp.where(ec == (((er & (LANES - 1)) >> (half.bit_length() - 1)) * half + (er >> (LANES.bit_length() - 1))),
                    1.0, 0.0).astype(BF16)
    sub_row = flip(lax.broadcasted_iota(jnp.int32, (sb, LANES), 0), sb)
    lbv = lb_ref[...]

    def row_start(j):
        return pl.multiple_of(flip(j, n_sub) * ck, ck)

    def stage1a(j):
        r0 = row_start(j)
        q = q_ref[0, pl.ds(r0, ck), :]
        z = f_ref[0, pl.ds(r0, ck), :]
        v = v_ref[0, pl.ds(r0, ck), :]
        f = lbv + (1.0 - lbv) * _sigmoid(z)
        lf = jnp.log(f)
        k = 1.0 - f
        return q, v, k, _dot01_exact(m_cum, lf)

    def stage1b(q, v, k, cums):
        bl = cums[0:ck]
        b = bl + per_block(cums[ck:ck + n_blk])
        tot_row = ck + (1 + n_lvl) * n_blk
        btot = cums[tot_row:tot_row + 1]
        qe = q * jnp.exp(b)
        ke = k * jnp.exp(btot - b)
        log2_k = jnp.log(k) * LOG2E
        b2 = b * LOG2E
        kb = b2 - log2_k
        bl2 = bl * LOG2E
        kbl = bl2 - log2_k
        q_lvl, k_lvl = [], []
        for lvl in range(n_lvl):
            split2 = per_block(cums[ck + (1 + lvl) * n_blk:ck + (2 + lvl) * n_blk] * LOG2E)
            q_lvl.append(q * jnp.exp2(jnp.where(upper[lvl], b2 - split2, NEG_BIG)))
            k_lvl.append(jnp.exp2(jnp.where(upper[lvl], NEG_BIG, split2 - kb)))

        qe_s[...] = qe.astype(BF16)
        btot_s[...] = btot
        for p in range(n_pairs):
            sl = slice(p * LANES, (p + 1) * LANES)
            diag_rows = []
            for jb in range(n_blk):
                rs = slice(jb * sb, (jb + 1) * sb)
                bl_b, kbl_b, q_b = bl2[rs, sl], kbl[rs, sl], q[rs, sl]
                terms = []
                for s in range(sb):
                    arg = jnp.where(sub_row >= flip(s, sb), bl_b - kbl_b[s:s + 1], NEG_BIG)
                    terms.append(q_b * jnp.exp2(arg))
                diag_rows.append(jnp.concatenate(terms, axis=1))
            diag_s[p] = _dot(jnp.concatenate(diag_rows, axis=0).astype(BF16), sel)
            for lvl in range(n_lvl):
                k_p = k_lvl[lvl][:, sl]
                k_heads = jnp.concatenate([jnp.where(head0, k_p, 0.0), jnp.where(head0, 0.0, k_p)], axis=0)
                lvl_s[p * n_lvl + lvl] = _dot_nt(q_lvl[lvl][:, sl].astype(BF16), k_heads.astype(BF16))
            upd_s[p] = _dot_tn(v[:, sl].astype(BF16), ke[:, sl].astype(BF16))

    def stage2_issue(j):
        r0 = row_start(j)
        v = v_ref[0, pl.ds(r0, ck), :]
        out = []
        for p in range(n_pairs):
            sl = slice(p * LANES, (p + 1) * LANES)
            parts = []
            for jb in range(n_blk):
                blk = diag_s[p, jb * sb:(jb + 1) * sb, :]
                parts.append(pltpu.roll(blk, jb * sb, 1) if jb else blk)
            scores = jnp.concatenate(parts, axis=0)
            for lvl in range(n_lvl):
                scores = scores + jnp.where(group_mask[lvl], lvl_s[p * n_lvl + lvl], 0.0)
            v_p = v[:, sl]
            v_heads = jnp.concatenate([jnp.where(head0, v_p, 0.0), jnp.where(head0, 0.0, v_p)], axis=0)
            intra = _dot(scores.astype(BF16), v_heads.astype(BF16))
            st = st_ref[p]
            inter = _dot_nt(qe_s[:, sl], st.astype(BF16))
            new_st = jnp.where(same_head, st * jnp.exp(btot_s[:, sl]) + upd_s[p], 0.0)
            out.append((inter + intra, new_st))
        return r0, out

    def stage2_finish(r0, out):
        for p in range(n_pairs):
            o_ref[0, pl.ds(r0, ck), p * LANES:(p + 1) * LANES] = out[p][0]
            st_ref[p] = out[p][1]

    return stage1a, stage1b, stage2_issue, stage2_finish


N_HGRN_SCRATCH = 6


def _hgrn_kernel(qf_ref, ff_ref, vf_ref, qb_ref, fb_ref, vb_ref, lb_ref, of_ref, ob_ref, *scratch, rows):
    fwd_scratch, bwd_scratch = scratch[:N_HGRN_SCRATCH], scratch[N_HGRN_SCRATCH:]

    @pl.when(pl.program_id(1) == 0)
    def _():
        fwd_scratch[0][...] = jnp.zeros_like(fwd_scratch[0])
        bwd_scratch[0][...] = jnp.zeros_like(bwd_scratch[0])

    f1a, f1b, f2, f3 = _hgrn_direction(False, qf_ref, ff_ref, vf_ref, lb_ref, of_ref, *fwd_scratch, rows=rows)
    b1a, b1b, b2, b3 = _hgrn_direction(True, qb_ref, fb_ref, vb_ref, lb_ref, ob_ref, *bwd_scratch, rows=rows)
    n_sub = rows // HGRN_CHUNK

    def stage1_both(j):
        fa = f1a(j)
        ba = b1a(j)
        f1b(*fa)
        b1b(*ba)

    stage1_both(0)

    def pipelined(j, carry):
        fo = f2(j)
        bo = b2(j)
        stage1_both(j + 1)
        f3(*fo)
        b3(*bo)
        return carry

    lax.fori_loop(0, n_sub - 1, pipelined, 0)
    fo = f2(n_sub - 1)
    bo = b2(n_sub - 1)
    f3(*fo)
    b3(*bo)


def _hgrn(proj, lb, *, d_lru, d_hgrn, rows=512):
    bsz, seq, _ = proj.shape
    n_chunks = seq // rows
    col0 = (2 * d_lru) // d_hgrn
    n_pairs = d_hgrn // LANES
    n_lvl = (HGRN_CHUNK // HGRN_SUB).bit_length() - 1
    fwd = lambda col: pl.BlockSpec((1, rows, d_hgrn), lambda b, c: (b, c, col))
    bwd = lambda col: pl.BlockSpec((1, rows, d_hgrn), lambda b, c: (b, n_chunks - 1 - c, col))
    direction_scratch = [
        pltpu.VMEM((n_pairs, LANES, LANES), F32),
        pltpu.VMEM((n_pairs, HGRN_CHUNK, LANES), F32),
        pltpu.VMEM((n_pairs * n_lvl, HGRN_CHUNK, LANES), F32),
        pltpu.VMEM((n_pairs, LANES, LANES), F32),
        pltpu.VMEM((HGRN_CHUNK, d_hgrn), BF16),
        pltpu.VMEM((1, d_hgrn), F32),
    ]
    assert len(direction_scratch) == N_HGRN_SCRATCH
    kern = functools.partial(_hgrn_kernel, rows=rows)
    out = jax.ShapeDtypeStruct((bsz, seq, d_hgrn), F32)
    return pl.pallas_call(
        kern,
        out_shape=(out, out),
        grid=(bsz, n_chunks),
        in_specs=[fwd(col0), fwd(col0 + 1), fwd(col0 + 3), bwd(col0), bwd(col0 + 2), bwd(col0 + 3),
                  pl.BlockSpec((1, d_hgrn), lambda b, c: (0, 0))],
        out_specs=(pl.BlockSpec((1, rows, d_hgrn), lambda b, c: (b, c, 0)),
                   pl.BlockSpec((1, rows, d_hgrn), lambda b, c: (b, n_chunks - 1 - c, 0))),
        scratch_shapes=direction_scratch + direction_scratch,
        compiler_params=_params(("arbitrary", "arbitrary")),
        name="hgrn2",
    )(proj, proj, proj, proj, proj, proj, lb.reshape(1, d_hgrn))


def _gelu_tanh(y):
    return 0.5 * y * (1.0 + jnp.tanh(0.7978845608028654 * (y + 0.044715 * (y * y * y))))


def _post_kernel(lf_ref, lb_ref, y_ref, of_ref, ob_ref, g_ref, x_ref, nlw_ref, nhw_ref, wo_ref,
                 gm_ref, nfw_ref, scf_ref, shf_ref, wr_ref, br_ref,
                 xo_ref, h_ref, slab_ref, cnt_ref, route_ref, carry_ref, *, tm, d_lru):
    first = (pl.program_id(0) == 0) & (pl.program_id(1) == 0)

    @pl.when(first)
    def _():
        carry_ref[...] = jnp.zeros_like(carry_ref)

    lru = (lf_ref[0] + lb_ref[0]) * _gelu_tanh(y_ref[0])
    ms = jnp.mean(lru * lru, axis=-1, keepdims=True)
    lru = lru * lax.rsqrt(ms + NORM_EPS) * nlw_ref[...]

    hg = of_ref[0] + ob_ref[0]
    width = hg.shape[1]
    hd = width // HGRN_HEADS
    hd_shift = hd.bit_length() - 1
    er = lax.broadcasted_iota(jnp.int32, (width, width), 0) >> hd_shift
    ec = lax.broadcasted_iota(jnp.int32, (width, width), 1) >> hd_shift
    head_sum = jnp.where(er == ec, 1.0, 0.0).astype(BF16)
    sq = hg * hg
    sq_hi = sq.astype(BF16)
    sq_lo = (sq - sq_hi.astype(F32)).astype(BF16)
    ms_h = (_dot(sq_hi, head_sum) + _dot(sq_lo, head_sum)) * (1.0 / hd)
    g = g_ref[0]
    hg = (hg * lax.rsqrt(ms_h + NORM_EPS) * nhw_ref[...]) * (g * _sigmoid(g))

    mixed = _dot(lru.astype(BF16), wo_ref[0:d_lru, :]) + _dot(hg.astype(BF16), wo_ref[d_lru:, :])
    x_new = x_ref[0] + gm_ref[0] * mixed
    xo_ref[0] = x_new

    h = _rms_mod(x_new, nfw_ref[...], scf_ref[0], shf_ref[0])
    _tiles_store(h_ref, h, tm, lead=(0,))

    logits = _dot(h.astype(BF16), wr_ref[...]) + br_ref[...]
    lane = lax.broadcasted_iota(jnp.int32, (tm, ROUTE_LANES), 1)
    lane_f = lane.astype(F32)
    far = float(ROUTE_LANES)
    is_g = lane < N_GROUPS
    gl = jnp.where(is_g, logits, NEG_BIG)
    gmax = jnp.max(gl, axis=-1, keepdims=True)
    g_idx = jnp.min(jnp.where(gl == gmax, lane_f, far), axis=-1, keepdims=True)
    p_group = 1.0 / jnp.sum(jnp.where(is_g, jnp.exp(gl - gmax), 0.0), axis=-1, keepdims=True)
    e_lane = lane - N_GROUPS
    in_group = (e_lane >= 0) & (e_lane < N_EXPERTS) & ((e_lane >> (EXPERTS_PER_GROUP.bit_length() - 1)).astype(F32) == g_idx)
    ev = jnp.where(in_group, logits, NEG_BIG)
    top1 = jnp.max(ev, axis=-1, keepdims=True)
    i1 = jnp.min(jnp.where(in_group & (ev == top1), lane_f, far), axis=-1, keepdims=True)
    rest = in_group & (lane_f != i1)
    ev2 = jnp.where(rest, logits, NEG_BIG)
    top2 = jnp.max(ev2, axis=-1, keepdims=True)
    i2 = jnp.min(jnp.where(rest & (ev2 == top2), lane_f, far), axis=-1, keepdims=True)
    e1 = i1 - float(N_GROUPS)
    e2 = i2 - float(N_GROUPS)
    ex = jnp.exp(top2 - top1)
    w1 = p_group / (1.0 + ex)
    w2 = p_group * ex / (1.0 + ex)

    sel1 = lane_f == e1
    sel2 = lane_f == e2
    onehot = jnp.where(sel1 | sel2, 1.0, 0.0)
    tr = lax.broadcasted_iota(jnp.int32, (tm, tm), 0)
    tc = lax.broadcasted_iota(jnp.int32, (tm, tm), 1)
    before = jnp.where(tc < tr, 1.0, 0.0).astype(BF16)
    cnt = _dot(before, onehot.astype(BF16)) + carry_ref[0:1]
    rank1 = jnp.sum(jnp.where(sel1, cnt, 0.0), axis=-1, keepdims=True)
    rank2 = jnp.sum(jnp.where(sel2, cnt, 0.0), axis=-1, keepdims=True)
    total = carry_ref[0:1] + jnp.sum(onehot, axis=0, keepdims=True)
    carry_ref[...] = jnp.broadcast_to(total, carry_ref.shape)
    cnt_ref[...] = jnp.broadcast_to(total, cnt_ref.shape)

    slab = jnp.where(lane == 0, e1, 0.0)
    slab = jnp.where(lane == 1, e2, slab)
    slab = jnp.where(lane == 2, w1, slab)
    slab = jnp.where(lane == 3, w2, slab)
    slab = jnp.where(lane == 4, rank1, slab)
    slab = jnp.where(lane == 5, rank2, slab)
    slab_ref[0] = slab
    route_ref[...] = slab.T[0:SUBLANES]


def _post_mixer(lru_f, lru_b, proj, hg_f, hg_b, x, nlw, nhw, wo_bf16, g_mix, nfw, sc_ffn, sh_ffn, wr_bf16, br,
                *, tm=512):
    bsz, seq, d = x.shape
    d_lru = lru_f.shape[-1]
    d_hgrn = hg_f.shape[-1]
    y_col = 1
    g_col = (2 * d_lru) // d_hgrn + 4
    row = lambda w: pl.BlockSpec((1, tm, w), lambda b, i: (b, i, 0))
    vec = lambda w: pl.BlockSpec((1, w), lambda b, i: (0, 0))
    per_b = lambda: pl.BlockSpec((1, 1, d), lambda b, i: (b, 0, 0))
    kern = functools.partial(_post_kernel, tm=tm, d_lru=d_lru)
    return pl.pallas_call(
        kern,
        out_shape=(
            jax.ShapeDtypeStruct((bsz, seq, d), F32),
            jax.ShapeDtypeStruct((bsz, seq * SUBLANES, LANES), F32),
            jax.ShapeDtypeStruct((bsz, seq, ROUTE_LANES), F32),
            jax.ShapeDtypeStruct((SUBLANES, ROUTE_LANES), F32),
            jax.ShapeDtypeStruct((SUBLANES, bsz * seq), F32),
        ),
        grid=(bsz, seq // tm),
        in_specs=[
            row(d_lru), row(d_lru),
            pl.BlockSpec((1, tm, d_lru), lambda b, i: (b, i, y_col)),
            row(d_hgrn), row(d_hgrn),
            pl.BlockSpec((1, tm, d_hgrn), lambda b, i: (b, i, g_col)),
            row(d), vec(d_lru), vec(d_hgrn),
            pl.BlockSpec((d, d), lambda b, i: (0, 0)),
            per_b(), vec(d), per_b(), per_b(),
            pl.BlockSpec((d, ROUTE_LANES), lambda b, i: (0, 0)),
            vec(ROUTE_LANES),
        ],
        out_specs=(
            row(d), pl.BlockSpec((1, tm * SUBLANES, LANES), lambda b, i: (b, i, 0)), row(ROUTE_LANES),
            pl.BlockSpec((SUBLANES, ROUTE_LANES), lambda b, i: (0, 0)),
            pl.BlockSpec((SUBLANES, tm), lambda b, i: (0, b * (seq // tm) + i)),
        ),
        scratch_shapes=[pltpu.VMEM((SUBLANES, ROUTE_LANES), F32)],
        compiler_params=_params(("arbitrary", "arbitrary")),
        name="post_mixer_router",
    )(lru_f, lru_b, proj, hg_f, hg_b, proj, x, nlw.reshape(1, d_lru), nhw.reshape(1, d_hgrn), wo_bf16,
      g_mix, nfw.reshape(1, d), sc_ffn, sh_ffn, wr_bf16, br)


def _tiles_load(ref, n, lead=()):
    return jnp.concatenate(
        [ref[(*lead, pl.ds(j, n, stride=SUBLANES), slice(None))] for j in range(SUBLANES)], axis=1)


def _tiles_store(ref, val, n, lead=()):
    for j in range(SUBLANES):
        ref[(*lead, pl.ds(j, n, stride=SUBLANES), slice(None))] = val[:, j * LANES:(j + 1) * LANES]


def _token_tile(ref, t):
    return ref.at[pl.ds(pl.multiple_of(t * SUBLANES, SUBLANES), SUBLANES)]


def _dispatch_kernel(d1_ref, d2_ref, h_ref, z_ref, o_ref, sem, *, tb):
    del z_ref
    base = pl.program_id(0) * tb

    def issue(r, carry):
        t = base + r
        pltpu.make_async_copy(_token_tile(h_ref, r), _token_tile(o_ref, d1_ref[t]), sem).start(priority=0)
        pltpu.make_async_copy(_token_tile(h_ref, r), _token_tile(o_ref, d2_ref[t]), sem).start(priority=1)
        return carry

    lax.fori_loop(0, tb, issue, 0, unroll=DMA_ISSUE_UNROLL)
    for _ in range(2):
        pltpu.make_async_copy(h_ref, o_ref.at[pl.ds(0, tb * SUBLANES)], sem).wait()


def _dispatch(dest1, dest2, h_tiles, n_rows, *, tb=256):
    m = h_tiles.shape[0] // SUBLANES
    kern = functools.partial(_dispatch_kernel, tb=tb)
    return pl.pallas_call(
        kern,
        out_shape=jax.ShapeDtypeStruct((n_rows * SUBLANES, LANES), h_tiles.dtype),
        grid_spec=pltpu.PrefetchScalarGridSpec(
            num_scalar_prefetch=2,
            grid=(m // tb,),
            in_specs=[pl.BlockSpec((tb * SUBLANES, LANES), lambda i, d1, d2: (i, 0)),
                      pl.BlockSpec(memory_space=pl.ANY)],
            out_specs=pl.BlockSpec(memory_space=pl.ANY),
            scratch_shapes=[pltpu.SemaphoreType.DMA(())],
        ),
        input_output_aliases={3: 0},
        compiler_params=pltpu.CompilerParams(dimension_semantics=("arbitrary",), has_side_effects=True),
        name="moe_dispatch",
    )(dest1, dest2, h_tiles, jnp.zeros((n_rows * SUBLANES, LANES), h_tiles.dtype))


def _expert_kernel(be_ref, x_ref, wg_ref, wu_ref, wd_ref, o_ref, wg_s, wu_s, wd_s):
    i = pl.program_id(0)
    prev_expert = be_ref[jnp.maximum(i - 1, 0)]

    @pl.when((i == 0) | (be_ref[i] != prev_expert))
    def _():
        wg_s[...] = wg_ref[0, 0].astype(BF16)
        wu_s[...] = wu_ref[0, 0].astype(BF16)
        wd_s[...] = wd_ref[0, 0].astype(BF16)

    blk = x_ref.shape[0] // SUBLANES
    n_used = be_ref[pl.num_programs(0)]

    @pl.when(i < n_used)
    def _():
        x = _tiles_load(x_ref, blk).astype(BF16)
        gate = _dot(x, wg_s[...])
        up = _dot(x, wu_s[...])
        act = (gate * _sigmoid(gate)) * up
        _tiles_store(o_ref, _dot(act.astype(BF16), wd_s[...]), blk)

    @pl.when(i >= n_used)
    def _():
        o_ref[...] = jnp.zeros_like(o_ref)


def _experts(blk_expert, x_tiles, wg, wu, wd, layer):
    n_rows = x_tiles.shape[0] // SUBLANES
    d, de = wg.shape[-2:]
    blk = EXPERT_BLOCK
    n_blocks = n_rows // blk
    return pl.pallas_call(
        _expert_kernel,
        out_shape=jax.ShapeDtypeStruct((n_rows * SUBLANES, LANES), F32),
        grid_spec=pltpu.PrefetchScalarGridSpec(
            num_scalar_prefetch=1,
            grid=(n_blocks,),
            in_specs=[
                pl.BlockSpec((blk * SUBLANES, LANES), lambda i, be: (jnp.minimum(i, be[n_blocks] - 1), 0)),
                pl.BlockSpec((1, 1, d, de), lambda i, be: (layer, be[i], 0, 0)),
                pl.BlockSpec((1, 1, d, de), lambda i, be: (layer, be[i], 0, 0)),
                pl.BlockSpec((1, 1, de, d), lambda i, be: (layer, be[i], 0, 0)),
            ],
            out_specs=pl.BlockSpec((blk * SUBLANES, LANES), lambda i, be: (i, 0)),
            scratch_shapes=[pltpu.VMEM((d, de), BF16), pltpu.VMEM((d, de), BF16), pltpu.VMEM((de, d), BF16)],
        ),
        compiler_params=_params(("arbitrary",)),
        name="moe_experts",
    )(blk_expert, x_tiles, wg, wu, wd)


def _combine_kernel(d1_ref, d2_ref, y_ref, slab_ref, x_ref, g_ref, nw_ref, o_ref, ra0, rb0, ra1, rb1, sem,
                    *, tm, tiles, n_steps, final_norm):
    step = pl.program_id(0) * tiles + pl.program_id(1)
    bufs = ((ra0, rb0), (ra1, rb1))

    def gather(tile, slot):
        base = tile * tm
        r1_ref, r2_ref = bufs[slot]

        def issue(r, carry):
            t = base + r
            pltpu.make_async_copy(_token_tile(y_ref, d1_ref[t]), _token_tile(r1_ref, r),
                                  sem.at[slot]).start(priority=0)
            pltpu.make_async_copy(_token_tile(y_ref, d2_ref[t]), _token_tile(r2_ref, r),
                                  sem.at[slot]).start(priority=1)
            return carry

        lax.fori_loop(0, tm, issue, 0, unroll=DMA_ISSUE_UNROLL)

    @pl.when(step == 0)
    def _():
        gather(0, 0)

    for slot in range(2):
        @pl.when((step & 1) == slot)
        def _():
            @pl.when(step + 1 < n_steps)
            def _():
                gather(step + 1, 1 - slot)

            r1_ref, r2_ref = bufs[slot]
            pltpu.make_async_copy(y_ref.at[pl.ds(0, tm * SUBLANES)], r1_ref, sem.at[slot]).wait()
            pltpu.make_async_copy(y_ref.at[pl.ds(0, tm * SUBLANES)], r2_ref, sem.at[slot]).wait()
            slab = slab_ref[0]
            y = slab[:, 2:3] * _tiles_load(r1_ref, tm) + slab[:, 3:4] * _tiles_load(r2_ref, tm)
            out = x_ref[0] + g_ref[0] * y
            if final_norm:
                ms = jnp.mean(out * out, axis=-1, keepdims=True)
                out = out * lax.rsqrt(ms + NORM_EPS) * nw_ref[...]
            o_ref[0] = out


def _combine(dest1, dest2, y_buf, slab, x, g_ffn, norm_w, *, final_norm, tm=256):
    bsz, seq, d = x.shape
    tiles = seq // tm
    kern = functools.partial(_combine_kernel, tm=tm, tiles=tiles, n_steps=bsz * tiles, final_norm=final_norm)
    row_buf = pltpu.VMEM((tm * SUBLANES, LANES), F32)
    return pl.pallas_call(
        kern,
        out_shape=jax.ShapeDtypeStruct((bsz, seq, d), F32),
        grid_spec=pltpu.PrefetchScalarGridSpec(
            num_scalar_prefetch=2,
            grid=(bsz, tiles),
            in_specs=[
                pl.BlockSpec(memory_space=pl.ANY),
                pl.BlockSpec((1, tm, ROUTE_LANES), lambda b, i, d1, d2: (b, i, 0)),
                pl.BlockSpec((1, tm, d), lambda b, i, d1, d2: (b, i, 0)),
                pl.BlockSpec((1, 1, d), lambda b, i, d1, d2: (b, 0, 0)),
                pl.BlockSpec((1, d), lambda b, i, d1, d2: (0, 0)),
            ],
            out_specs=pl.BlockSpec((1, tm, d), lambda b, i, d1, d2: (b, i, 0)),
            scratch_shapes=[row_buf, row_buf, row_buf, row_buf, pltpu.SemaphoreType.DMA((2,))],
        ),
        compiler_params=_params(("arbitrary", "arbitrary")),
        name="moe_combine",
    )(dest1, dest2, y_buf, slab, x, g_ffn, norm_w.reshape(1, d))


def _block_diag(w):
    heads, hd, _ = w.shape
    n = heads * hd
    tiled = jnp.tile(w.reshape(n, hd), (1, heads))
    blk_r = lax.broadcasted_iota(jnp.int32, (n, n), 0) // hd
    blk_c = lax.broadcasted_iota(jnp.int32, (n, n), 1) // hd
    return jnp.where(blk_r == blk_c, tiled, 0.0)


def kernel(x, c, ada_w, ada_b, norm_mix_w, w_in, conv_w, conv_b, lru_wa, lru_ba, lru_wx, lru_bx, lru_lambda, norm_lru_w, hgrn_lb, norm_hgrn_w, w_out, norm_ffn_w, router_group_w, router_group_b, router_expert_w, router_expert_b, expert_w_gate, expert_w_up, expert_w_down, final_norm_w):
    bsz, seq, d = x.shape
    assert d == SUBLANES * LANES, "the MoE row movement keeps one (8, 128) tile per token"
    depth = ada_w.shape[0]
    d_lru = conv_w.shape[-1]
    d_hgrn = hgrn_lb.shape[-1]
    m = bsz * seq
    n_rows = m * 2 + N_EXPERTS * EXPERT_BLOCK
    n_blocks = n_rows // EXPERT_BLOCK

    mod = _modulation(c, ada_w, ada_b)
    lb_cum = jnp.cumsum(jax.nn.softmax(hgrn_lb.astype(F32), axis=0), axis=0)
    lb_all = lb_cum - lb_cum[0:1]

    for l in range(depth):
        sh_mix, sc_mix, g_mix, sh_ffn, sc_ffn, g_ffn = [
            mod[l, :, i * d:(i + 1) * d].reshape(bsz, 1, d) for i in range(6)]
        proj = _in_proj(x, norm_mix_w[l], sc_mix, sh_mix, w_in, l)
        wa_bd = jnp.stack([_block_diag(lru_wa[l, 0]), _block_diag(lru_wa[l, 1])]).astype(BF16)
        wx_bd = jnp.stack([_block_diag(lru_wx[l, 0]), _block_diag(lru_wx[l, 1])]).astype(BF16)
        lru = [
            _lru_scan(proj, conv_w[l], conv_b[l], wa_bd, lru_ba[l], wx_bd, lru_bx[l], lru_lambda[l],
                      reverse=rv)
            for rv in (False, True)]
        hg_f, hg_b = _hgrn(proj, lb_all[l], d_lru=d_lru, d_hgrn=d_hgrn)

        lane_pad = ROUTE_LANES - N_GROUPS - N_EXPERTS
        wr = jnp.pad(jnp.concatenate([router_group_w[l], router_expert_w[l]], axis=1), ((0, 0), (0, lane_pad)))
        br = jnp.pad(jnp.concatenate([router_group_b[l], router_expert_b[l]]), (0, lane_pad)).reshape(1, ROUTE_LANES)
        x_mid, h_ffn, slab, counts, route = _post_mixer(
            lru[0], lru[1], proj, hg_f, hg_b, x, norm_lru_w[l], norm_hgrn_w[l], w_out[l].astype(BF16), g_mix,
            norm_ffn_w[l], sc_ffn, sh_ffn, wr.astype(BF16), br)

        cnt = counts[0, :N_EXPERTS].astype(jnp.int32)
        padded = ((cnt + EXPERT_BLOCK - 1) // EXPERT_BLOCK) * EXPERT_BLOCK
        pend = jnp.cumsum(padded)
        pstart = pend - padded
        blk_start = jnp.arange(n_blocks, dtype=jnp.int32) * EXPERT_BLOCK
        blk_expert = jnp.minimum(jnp.sum(pend[None, :] <= blk_start[:, None], axis=1), N_EXPERTS - 1)
        blocks_used = (pend[N_EXPERTS - 1] // EXPERT_BLOCK).reshape(1)
        blk_expert = jnp.concatenate([blk_expert.astype(jnp.int32), blocks_used.astype(jnp.int32)])
        route = route.astype(jnp.int32)
        dest1 = pstart[route[0]] + route[4]
        dest2 = pstart[route[1]] + route[5]

        x_buf = _dispatch(dest1, dest2, h_ffn.reshape(m * SUBLANES, LANES), n_rows)
        y_buf = _experts(blk_expert, x_buf, expert_w_gate, expert_w_up, expert_w_down, l)
        x = _combine(dest1, dest2, y_buf, slab, x_mid, g_ffn, final_norm_w, final_norm=(l == depth - 1))

    return x
```

```python
import functools

import jax
import jax.numpy as jnp
from jax import lax
from jax.experimental import pallas as pl
from jax.experimental.pallas import tpu as pltpu

F32 = jnp.float32
BF16 = jnp.bfloat16

LRU_HEADS = 8
HGRN_HEADS = 8
CONV_WIDTH = 4
LRU_C = 8.0
N_GROUPS = 4
EXPERTS_PER_GROUP = 8
N_EXPERTS = N_GROUPS * EXPERTS_PER_GROUP
NORM_EPS = 1e-6

LANES = 128
SUBLANES = 8
VMEM_LIMIT = 56 * 1024 * 1024

HGRN_CHUNK = 64
HGRN_SUB = 8
LOG2E = 1.4426950408889634
ROUTE_LANES = LANES
EXPERT_BLOCK = 512
DMA_ISSUE_UNROLL = 8
NEG_BIG = -3.0e38


def _params(sem):
    return pltpu.CompilerParams(dimension_semantics=sem, vmem_limit_bytes=VMEM_LIMIT)


def _dot(a, b):
    return jnp.dot(a, b, preferred_element_type=F32)


def _dot_nt(a, b):
    return lax.dot_general(a, b, (((1,), (1,)), ((), ())), preferred_element_type=F32)


def _dot_tn(a, b):
    return lax.dot_general(a, b, (((0,), (0,)), ((), ())), preferred_element_type=F32)


def _dot01_exact(m01, x):
    hi = x.astype(BF16)
    r1 = x - hi.astype(F32)
    mid = r1.astype(BF16)
    lo = (r1 - mid.astype(F32)).astype(BF16)
    return _dot(m01, hi) + _dot(m01, mid) + _dot(m01, lo)


def _sigmoid(x):
    return 1.0 / (1.0 + jnp.exp(-x))


def _mod_kernel(c_ref, w_ref, b_ref, o_ref):
    c = c_ref[...]
    cond = c * _sigmoid(c)
    o_ref[0] = _dot(cond.astype(BF16), w_ref[0].astype(BF16)) + b_ref[0]


def _modulation(c, ada_w, ada_b):
    depth, d, n = ada_w.shape
    bsz = c.shape[0]
    rows = -(-bsz // SUBLANES) * SUBLANES
    c_pad = jnp.pad(c, ((0, rows - bsz), (0, 0)))
    tn = n // 6
    out = pl.pallas_call(
        _mod_kernel,
        out_shape=jax.ShapeDtypeStruct((depth, rows, n), F32),
        grid=(depth, n // tn),
        in_specs=[
            pl.BlockSpec((rows, d), lambda l, j: (0, 0)),
            pl.BlockSpec((1, d, tn), lambda l, j: (l, 0, j)),
            pl.BlockSpec((1, 1, tn), lambda l, j: (l, 0, j)),
        ],
        out_specs=pl.BlockSpec((1, rows, tn), lambda l, j: (l, 0, j)),
        compiler_params=_params(("arbitrary", "arbitrary")),
        name="adaln_mod",
    )(c_pad, ada_w, ada_b.reshape(depth, 1, n))
    return out[:, :bsz]


def _rms_mod(x, nw, sc, sh):
    ms = jnp.mean(x * x, axis=-1, keepdims=True)
    return (x * lax.rsqrt(ms + NORM_EPS) * nw) * (1.0 + sc) + sh


def _inproj_kernel(x_ref, nw_ref, sc_ref, sh_ref, w_ref, o_ref, w_s):
    @pl.when((pl.program_id(0) == 0) & (pl.program_id(1) == 0))
    def _():
        w_s[...] = w_ref[0].astype(BF16)

    h = _rms_mod(x_ref[0], nw_ref[...], sc_ref[0], sh_ref[0])
    o_ref[0] = _dot(h.astype(BF16), w_s[...])


def _in_proj(x, nw, sc, sh, w_in, layer, tm=512):
    bsz, seq, d = x.shape
    n = w_in.shape[-1]
    return pl.pallas_call(
        _inproj_kernel,
        out_shape=jax.ShapeDtypeStruct((bsz, seq, n), F32),
        grid=(bsz, seq // tm),
        in_specs=[
            pl.BlockSpec((1, tm, d), lambda b, i: (b, i, 0)),
            pl.BlockSpec((1, d), lambda b, i: (0, 0)),
            pl.BlockSpec((1, 1, d), lambda b, i: (b, 0, 0)),
            pl.BlockSpec((1, 1, d), lambda b, i: (b, 0, 0)),
            pl.BlockSpec((1, d, n), lambda b, i: (layer, 0, 0), pipeline_mode=pl.Buffered(1)),
        ],
        out_specs=pl.BlockSpec((1, tm, n), lambda b, i: (b, i, 0)),
        scratch_shapes=[pltpu.VMEM((d, n), BF16)],
        compiler_params=_params(("arbitrary", "arbitrary")),
        name="in_proj",
    )(x, nw.reshape(1, d), sc, sh, w_in)


def _lru_kernel(x_ref, xp_ref, xn_ref, cw_ref, cb_ref, wa_ref, ba_ref, wx_ref, bx_ref, lam_ref,
                o_ref, carry_ref, sa_ref, sb_ref, cin_ref, *, reverse, n_chunks, rows):
    c = pl.program_id(1)
    chunk = (n_chunks - 1 - c) if reverse else c

    @pl.when(c == 0)
    def _():
        carry_ref[...] = jnp.zeros_like(carry_ref)

    x = x_ref[0]
    width = x.shape[1]
    row = lax.broadcasted_iota(jnp.int32, (rows, width), 0)
    has_prev = jnp.where(chunk > 0, 1.0, 0.0)
    has_next = jnp.where(chunk < n_chunks - 1, 1.0, 0.0)
    xp = xp_ref[0] * has_prev
    xn = xn_ref[0] * has_next
    xe = jnp.concatenate([xp, x, xn], axis=0)
    h8 = SUBLANES
    cw = cw_ref[...]
    xc = (cw[0:1] * xe[h8 - 2:h8 - 2 + rows] + cw[1:2] * xe[h8 - 1:h8 - 1 + rows] + cw[2:3] * x
          + cw[3:4] * xe[h8 + 1:h8 + 1 + rows] + cb_ref[...])

    xcb = xc.astype(BF16)
    r = _sigmoid(_dot(xcb, wa_ref[0]) + ba_ref[0])
    gate_i = _sigmoid(_dot(xcb, wx_ref[0]) + bx_ref[0])
    lam = lam_ref[0]
    softplus_neg_lam = jnp.maximum(-lam, 0.0) + jnp.log1p(jnp.exp(-jnp.abs(lam)))
    log_a = (-LRU_C) * r * softplus_neg_lam
    a = jnp.exp(log_a)
    t = jnp.tanh(-log_a)
    u = jnp.sqrt(2.0 * t / (1.0 + t)) * (gate_i * xc)

    groups = rows // SUBLANES
    acc_a = a.reshape(groups, SUBLANES, width)
    acc_b = u.reshape(groups, SUBLANES, width)
    sub = lax.broadcasted_iota(jnp.int32, (groups, SUBLANES, width), 1)
    s = 1
    while s < SUBLANES:
        if reverse:
            valid = sub < SUBLANES - s
            sh_a, sh_b = pltpu.roll(acc_a, SUBLANES - s, 1), pltpu.roll(acc_b, SUBLANES - s, 1)
        else:
            valid = sub >= s
            sh_a, sh_b = pltpu.roll(acc_a, s, 1), pltpu.roll(acc_b, s, 1)
        acc_b = jnp.where(valid, acc_a * sh_b + acc_b, acc_b)
        acc_a = jnp.where(valid, acc_a * sh_a, acc_a)
        s *= 2
    acc_a = acc_a.reshape(rows, width)
    acc_b = acc_b.reshape(rows, width)
    groups = rows // SUBLANES
    edge = 0 if reverse else SUBLANES - 1
    n_tiles = width // LANES
    for j in range(n_tiles):
        sa_ref[j] = acc_a[:, j * LANES:(j + 1) * LANES]
        sb_ref[j] = acc_b[:, j * LANES:(j + 1) * LANES]
    ea = jnp.concatenate([sa_ref[j, pl.ds(edge, groups, stride=SUBLANES), :] for j in range(n_tiles)], axis=1)
    eb = jnp.concatenate([sb_ref[j, pl.ds(edge, groups, stride=SUBLANES), :] for j in range(n_tiles)], axis=1)
    grow = lax.broadcasted_iota(jnp.int32, (groups, width), 0)
    s = 1
    while s < groups:
        if reverse:
            valid = grow < groups - s
            sh_a, sh_b = pltpu.roll(ea, groups - s, 0), pltpu.roll(eb, groups - s, 0)
        else:
            valid = grow >= s
            sh_a, sh_b = pltpu.roll(ea, s, 0), pltpu.roll(eb, s, 0)
        eb = jnp.where(valid, ea * sh_b + eb, eb)
        ea = jnp.where(valid, ea * sh_a, ea)
        s *= 2
    carry = carry_ref[...]
    group_out = eb + ea * carry
    if reverse:
        carry_in = jnp.where(grow == groups - 1, carry, pltpu.roll(group_out, groups - 1, 0))
        carry_ref[...] = group_out[0:1]
    else:
        carry_in = jnp.where(grow == 0, carry, pltpu.roll(group_out, 1, 0))
        carry_ref[...] = group_out[groups - 1:groups]
    cin_ref[...] = carry_in
    for g in range(groups):
        rs = slice(g * SUBLANES, (g + 1) * SUBLANES)
        o_ref[0, rs, :] = acc_b[rs] + acc_a[rs] * cin_ref[g:g + 1, :]


def _lru_scan(proj, conv_w, conv_b, wa_bd, ba, wx_bd, bx, lam, *, reverse, rows=256):
    bsz, seq, _ = proj.shape
    d_lru = conv_w.shape[1]
    n_chunks = seq // rows
    halo = rows // SUBLANES
    last_halo = seq // SUBLANES - 1
    dirn = 1 if reverse else 0

    def chunk_of(c):
        return (n_chunks - 1 - c) if reverse else c

    vec = lambda: pl.BlockSpec((1, 1, d_lru), lambda b, c: (dirn, 0, 0))
    mat = lambda: pl.BlockSpec((1, d_lru, d_lru), lambda b, c: (dirn, 0, 0))
    kern = functools.partial(_lru_kernel, reverse=reverse, n_chunks=n_chunks, rows=rows)
    return pl.pallas_call(
        kern,
        out_shape=jax.ShapeDtypeStruct((bsz, seq, d_lru), F32),
        grid=(bsz, n_chunks),
        in_specs=[
            pl.BlockSpec((1, rows, d_lru), lambda b, c: (b, chunk_of(c), 0)),
            pl.BlockSpec((1, SUBLANES, d_lru),
                         lambda b, c: (b, jnp.maximum(chunk_of(c) * halo - 1, 0), 0)),
            pl.BlockSpec((1, SUBLANES, d_lru),
                         lambda b, c: (b, jnp.minimum((chunk_of(c) + 1) * halo, last_halo), 0)),
            pl.BlockSpec((CONV_WIDTH, d_lru), lambda b, c: (0, 0)),
            pl.BlockSpec((1, d_lru), lambda b, c: (0, 0)),
            mat(), vec(), mat(), vec(), vec(),
        ],
        out_specs=pl.BlockSpec((1, rows, d_lru), lambda b, c: (b, chunk_of(c), 0)),
        scratch_shapes=[pltpu.VMEM((1, d_lru), F32), pltpu.VMEM((d_lru // LANES, rows, LANES), F32),
                        pltpu.VMEM((d_lru // LANES, rows, LANES), F32),
                        pltpu.VMEM((rows // SUBLANES, d_lru), F32)],
        compiler_params=_params(("arbitrary", "arbitrary")),
        name="lru_bwd" if reverse else "lru_fwd",
    )(proj, proj, proj, conv_w, conv_b.reshape(1, d_lru), wa_bd, ba.reshape(2, 1, d_lru),
      wx_bd, bx.reshape(2, 1, d_lru), lam.reshape(2, 1, d_lru))


def _hgrn_direction(rev, q_ref, f_ref, v_ref, lb_ref, o_ref, st_ref, diag_s, lvl_s, upd_s, qe_s, btot_s, *, rows):
    ck, sb = HGRN_CHUNK, HGRN_SUB
    n_blk = ck // sb
    sb_shift = sb.bit_length() - 1
    n_sub = rows // ck
    width = q_ref.shape[-1]
    n_pairs = width // LANES
    half = LANES // 2

    def flip(idx, n):
        return (n - 1 - idx) if rev else idx

    n_lvl = n_blk.bit_length() - 1
    tf = flip(lax.broadcasted_iota(jnp.int32, (ck, ck), 0), ck)
    uf = flip(lax.broadcasted_iota(jnp.int32, (ck, ck), 1), ck)
    tb, ub = tf >> sb_shift, uf >> sb_shift
    pb = flip(lax.broadcasted_iota(jnp.int32, (n_blk, ck), 0), n_blk)
    pub = flip(lax.broadcasted_iota(jnp.int32, (n_blk, ck), 1), ck) >> sb_shift
    mats = [jnp.where((tb == ub) & (uf <= tf), 1.0, 0.0),
            jnp.where(pub < pb, 1.0, 0.0)]
    for lvl in range(n_lvl):
        mid = ((pb >> (lvl + 1)) << (lvl + 1)) + (1 << lvl)
        mats.append(jnp.where(pub < mid, 1.0, 0.0))
    mats.append(jnp.ones((SUBLANES, ck), F32))
    m_cum = jnp.concatenate(mats, axis=0).astype(BF16)

    def per_block(rows8):
        return jnp.concatenate(
            [jnp.broadcast_to(rows8[jb:jb + 1], (sb, rows8.shape[1])) for jb in range(n_blk)], axis=0)
    row_blk = flip(lax.broadcasted_iota(jnp.int32, (ck, width), 0), ck) >> sb_shift
    upper = [((row_blk >> lvl) & 1) == 1 for lvl in range(n_lvl)]
    pr = flip(lax.broadcasted_iota(jnp.int32, (ck, LANES), 0), ck) >> sb_shift
    pc = flip(lax.broadcasted_iota(jnp.int32, (ck, LANES), 1) & (ck - 1), ck) >> sb_shift
    group_mask = [(pr >> (lvl + 1)) == (pc >> (lvl + 1)) for lvl in range(n_lvl)]
    lane = lax.broadcasted_iota(jnp.int32, (1, LANES), 1)
    head0 = lane < half
    sr = lax.broadcasted_iota(jnp.int32, (LANES, LANES), 0)
    sc = lax.broadcasted_iota(jnp.int32, (LANES, LANES), 1)
    same_head = (sr < half) == (sc < half)
    er = lax.broadcasted_iota(jnp.int32, (sb * LANES, LANES), 0)
    ec = lax.broadcasted_iota(jnp.int32, (sb * LANES, LANES), 1)
    sel = jnp.where(ec == (((er & (LANES - 1)) >> (half.bit_length() - 1)) * half + (er >> (LANES.bit_length() - 1))),
                    1.0, 0.0).astype(BF16)
    sub_row = flip(lax.broadcasted_iota(jnp.int32, (sb, LANES), 0), sb)
    lbv = lb_ref[...]

    def row_start(j):
        return pl.multiple_of(flip(j, n_sub) * ck, ck)

    def stage1a(j):
        r0 = row_start(j)
        q = q_ref[0, pl.ds(r0, ck), :]
        z = f_ref[0, pl.ds(r0, ck), :]
        v = v_ref[0, pl.ds(r0, ck), :]
        f = lbv + (1.0 - lbv) * _sigmoid(z)
        lf = jnp.log(f)
        k = 1.0 - f
        return q, v, k, _dot01_exact(m_cum, lf)

    def stage1b(q, v, k, cums):
        bl = cums[0:ck]
        b = bl + per_block(cums[ck:ck + n_blk])
        tot_row = ck + (1 + n_lvl) * n_blk
        btot = cums[tot_row:tot_row + 1]
        qe = q * jnp.exp(b)
        ke = k * jnp.exp(btot - b)
        log2_k = jnp.log(k) * LOG2E
        b2 = b * LOG2E
        kb = b2 - log2_k
        bl2 = bl * LOG2E
        kbl = bl2 - log2_k
        q_lvl, k_lvl = [], []
        for lvl in range(n_lvl):
            split2 = per_block(cums[ck + (1 + lvl) * n_blk:ck + (2 + lvl) * n_blk] * LOG2E)
            q_lvl.append(q * jnp.exp2(jnp.where(upper[lvl], b2 - split2, NEG_BIG)))
            k_lvl.append(jnp.exp2(jnp.where(upper[lvl], NEG_BIG, split2 - kb)))

        qe_s[...] = qe.astype(BF16)
        btot_s[...] = btot
        for p in range(n_pairs):
            sl = slice(p * LANES, (p + 1) * LANES)
            diag_rows = []
            for jb in range(n_blk):
                rs = slice(jb * sb, (jb + 1) * sb)
                bl_b, kbl_b, q_b = bl2[rs, sl], kbl[rs, sl], q[rs, sl]
                terms = []
                for s in range(sb):
                    arg = jnp.where(sub_row >= flip(s, sb), bl_b - kbl_b[s:s + 1], NEG_BIG)
                    terms.append(q_b * jnp.exp2(arg))
                diag_rows.append(jnp.concatenate(terms, axis=1))
            diag_s[p] = _dot(jnp.concatenate(diag_rows, axis=0).astype(BF16), sel)
            for lvl in range(n_lvl):
                k_p = k_lvl[lvl][:, sl]
                k_heads = jnp.concatenate([jnp.where(head0, k_p, 0.0), jnp.where(head0, 0.0, k_p)], axis=0)
                lvl_s[p * n_lvl + lvl] = _dot_nt(q_lvl[lvl][:, sl].astype(BF16), k_heads.astype(BF16))
            upd_s[p] = _dot_tn(v[:, sl].astype(BF16), ke[:, sl].astype(BF16))

    def stage2_issue(j):
        r0 = row_start(j)
        v = v_ref[0, pl.ds(r0, ck), :]
        out = []
        for p in range(n_pairs):
            sl = slice(p * LANES, (p + 1) * LANES)
            parts = []
            for jb in range(n_blk):
                blk = diag_s[p, jb * sb:(jb + 1) * sb, :]
                parts.append(pltpu.roll(blk, jb * sb, 1) if jb else blk)
            scores = jnp.concatenate(parts, axis=0)
            for lvl in range(n_lvl):
                scores = scores + jnp.where(group_mask[lvl], lvl_s[p * n_lvl + lvl], 0.0)
            v_p = v[:, sl]
            v_heads = jnp.concatenate([jnp.where(head0, v_p, 0.0), jnp.where(head0, 0.0, v_p)], axis=0)
            intra = _dot(scores.astype(BF16), v_heads.astype(BF16))
            st = st_ref[p]
            inter = _dot_nt(qe_s[:, sl], st.astype(BF16))
            new_st = jnp.where(same_head, st * jnp.exp(btot_s[:, sl]) + upd_s[p], 0.0)
            out.append((inter + intra, new_st))
        return r0, out

    def stage2_finish(r0, out):
        for p in range(n_pairs):
            o_ref[0, pl.ds(r0, ck), p * LANES:(p + 1) * LANES] = out[p][0]
            st_ref[p] = out[p][1]

    return stage1a, stage1b, stage2_issue, stage2_finish


N_HGRN_SCRATCH = 6


def _hgrn_kernel(qf_ref, ff_ref, vf_ref, qb_ref, fb_ref, vb_ref, lb_ref, of_ref, ob_ref, *scratch, rows):
    fwd_scratch, bwd_scratch = scratch[:N_HGRN_SCRATCH], scratch[N_HGRN_SCRATCH:]

    @pl.when(pl.program_id(1) == 0)
    def _():
        fwd_scratch[0][...] = jnp.zeros_like(fwd_scratch[0])
        bwd_scratch[0][...] = jnp.zeros_like(bwd_scratch[0])

    f1a, f1b, f2, f3 = _hgrn_direction(False, qf_ref, ff_ref, vf_ref, lb_ref, of_ref, *fwd_scratch, rows=rows)
    b1a, b1b, b2, b3 = _hgrn_direction(True, qb_ref, fb_ref, vb_ref, lb_ref, ob_ref, *bwd_scratch, rows=rows)
    n_sub = rows // HGRN_CHUNK

    def stage1_both(j):
        fa = f1a(j)
        ba = b1a(j)
        f1b(*fa)
        b1b(*ba)

    stage1_both(0)

    def pipelined(j, carry):
        fo = f2(j)
        bo = b2(j)
        stage1_both(j + 1)
        f3(*fo)
        b3(*bo)
        return carry

    lax.fori_loop(0, n_sub - 1, pipelined, 0)
    fo = f2(n_sub - 1)
    bo = b2(n_sub - 1)
    f3(*fo)
    b3(*bo)


def _hgrn(proj, lb, *, d_lru, d_hgrn, rows=512):
    bsz, seq, _ = proj.shape
    n_chunks = seq // rows
    col0 = (2 * d_lru) // d_hgrn
    n_pairs = d_hgrn // LANES
    n_lvl = (HGRN_CHUNK // HGRN_SUB).bit_length() - 1
    fwd = lambda col: pl.BlockSpec((1, rows, d_hgrn), lambda b, c: (b, c, col))
    bwd = lambda col: pl.BlockSpec((1, rows, d_hgrn), lambda b, c: (b, n_chunks - 1 - c, col))
    direction_scratch = [
        pltpu.VMEM((n_pairs, LANES, LANES), F32),
        pltpu.VMEM((n_pairs, HGRN_CHUNK, LANES), F32),
        pltpu.VMEM((n_pairs * n_lvl, HGRN_CHUNK, LANES), F32),
        pltpu.VMEM((n_pairs, LANES, LANES), F32),
        pltpu.VMEM((HGRN_CHUNK, d_hgrn), BF16),
        pltpu.VMEM((1, d_hgrn), F32),
    ]
    assert len(direction_scratch) == N_HGRN_SCRATCH
    kern = functools.partial(_hgrn_kernel, rows=rows)
    out = jax.ShapeDtypeStruct((bsz, seq, d_hgrn), F32)
    return pl.pallas_call(
        kern,
        out_shape=(out, out),
        grid=(bsz, n_chunks),
        in_specs=[fwd(col0), fwd(col0 + 1), fwd(col0 + 3), bwd(col0), bwd(col0 + 2), bwd(col0 + 3),
                  pl.BlockSpec((1, d_hgrn), lambda b, c: (0, 0))],
        out_specs=(pl.BlockSpec((1, rows, d_hgrn), lambda b, c: (b, c, 0)),
                   pl.BlockSpec((1, rows, d_hgrn), lambda b, c: (b, n_chunks - 1 - c, 0))),
        scratch_shapes=direction_scratch + direction_scratch,
        compiler_params=_params(("arbitrary", "arbitrary")),
        name="hgrn2",
    )(proj, proj, proj, proj, proj, proj, lb.reshape(1, d_hgrn))


def _gelu_tanh(y):
    return 0.5 * y * (1.0 + jnp.tanh(0.7978845608028654 * (y + 0.044715 * (y * y * y))))


def _post_kernel(lf_ref, lb_ref, y_ref, of_ref, ob_ref, g_ref, x_ref, nlw_ref, nhw_ref, wo_ref,
                 gm_ref, nfw_ref, scf_ref, shf_ref, wr_ref, br_ref,
                 xo_ref, h_ref, slab_ref, cnt_ref, route_ref, carry_ref, *, tm, d_lru):
    first = (pl.program_id(0) == 0) & (pl.program_id(1) == 0)

    @pl.when(first)
    def _():
        carry_ref[...] = jnp.zeros_like(carry_ref)

    lru = (lf_ref[0] + lb_ref[0]) * _gelu_tanh(y_ref[0])
    ms = jnp.mean(lru * lru, axis=-1, keepdims=True)
    lru = lru * lax.rsqrt(ms + NORM_EPS) * nlw_ref[...]

    hg = of_ref[0] + ob_ref[0]
    width = hg.shape[1]
    hd = width // HGRN_HEADS
    hd_shift = hd.bit_length() - 1
    er = lax.broadcasted_iota(jnp.int32, (width, width), 0) >> hd_shift
    ec = lax.broadcasted_iota(jnp.int32, (width, width), 1) >> hd_shift
    head_sum = jnp.where(er == ec, 1.0, 0.0).astype(BF16)
    sq = hg * hg
    sq_hi = sq.astype(BF16)
    sq_lo = (sq - sq_hi.astype(F32)).astype(BF16)
    ms_h = (_dot(sq_hi, head_sum) + _dot(sq_lo, head_sum)) * (1.0 / hd)
    g = g_ref[0]
    hg = (hg * lax.rsqrt(ms_h + NORM_EPS) * nhw_ref[...]) * (g * _sigmoid(g))

    mixed = _dot(lru.astype(BF16), wo_ref[0:d_lru, :]) + _dot(hg.astype(BF16), wo_ref[d_lru:, :])
    x_new = x_ref[0] + gm_ref[0] * mixed
    xo_ref[0] = x_new

    h = _rms_mod(x_new, nfw_ref[...], scf_ref[0], shf_ref[0])
    _tiles_store(h_ref, h, tm, lead=(0,))

    logits = _dot(h.astype(BF16), wr_ref[...]) + br_ref[...]
    lane = lax.broadcasted_iota(jnp.int32, (tm, ROUTE_LANES), 1)
    lane_f = lane.astype(F32)
    far = float(ROUTE_LANES)
    is_g = lane < N_GROUPS
    gl = jnp.where(is_g, logits, NEG_BIG)
    gmax = jnp.max(gl, axis=-1, keepdims=True)
    g_idx = jnp.min(jnp.where(gl == gmax, lane_f, far), axis=-1, keepdims=True)
    p_group = 1.0 / jnp.sum(jnp.where(is_g, jnp.exp(gl - gmax), 0.0), axis=-1, keepdims=True)
    e_lane = lane - N_GROUPS
    in_group = (e_lane >= 0) & (e_lane < N_EXPERTS) & ((e_lane >> (EXPERTS_PER_GROUP.bit_length() - 1)).astype(F32) == g_idx)
    ev = jnp.where(in_group, logits, NEG_BIG)
    top1 = jnp.max(ev, axis=-1, keepdims=True)
    i1 = jnp.min(jnp.where(in_group & (ev == top1), lane_f, far), axis=-1, keepdims=True)
    rest = in_group & (lane_f != i1)
    ev2 = jnp.where(rest, logits, NEG_BIG)
    top2 = jnp.max(ev2, axis=-1, keepdims=True)
    i2 = jnp.min(jnp.where(rest & (ev2 == top2), lane_f, far), axis=-1, keepdims=True)
    e1 = i1 - float(N_GROUPS)
    e2 = i2 - float(N_GROUPS)
    ex = jnp.exp(top2 - top1)
    w1 = p_group / (1.0 + ex)
    w2 = p_group * ex / (1.0 + ex)

    sel1 = lane_f == e1
    sel2 = lane_f == e2
    onehot = jnp.where(sel1 | sel2, 1.0, 0.0)
    tr = lax.broadcasted_iota(jnp.int32, (tm, tm), 0)
    tc = lax.broadcasted_iota(jnp.int32, (tm, tm), 1)
    before = jnp.where(tc < tr, 1.0, 0.0).astype(BF16)
    cnt = _dot(before, onehot.astype(BF16)) + carry_ref[0:1]
    rank1 = jnp.sum(jnp.where(sel1, cnt, 0.0), axis=-1, keepdims=True)
    rank2 = jnp.sum(jnp.where(sel2, cnt, 0.0), axis=-1, keepdims=True)
    total = carry_ref[0:1] + jnp.sum(onehot, axis=0, keepdims=True)
    carry_ref[...] = jnp.broadcast_to(total, carry_ref.shape)
    cnt_ref[...] = jnp.broadcast_to(total, cnt_ref.shape)

    slab = jnp.where(lane == 0, e1, 0.0)
    slab = jnp.where(lane == 1, e2, slab)
    slab = jnp.where(lane == 2, w1, slab)
    slab = jnp.where(lane == 3, w2, slab)
    slab = jnp.where(lane == 4, rank1, slab)
    slab = jnp.where(lane == 5, rank2, slab)
    slab_ref[0] = slab
    route_ref[...] = slab.T[0:SUBLANES]


def _post_mixer(lru_f, lru_b, proj, hg_f, hg_b, x, nlw, nhw, wo_bf16, g_mix, nfw, sc_ffn, sh_ffn, wr_bf16, br,
                *, tm=512):
    bsz, seq, d = x.shape
    d_lru = lru_f.shape[-1]
    d_hgrn = hg_f.shape[-1]
    y_col = 1
    g_col = (2 * d_lru) // d_hgrn + 4
    row = lambda w: pl.BlockSpec((1, tm, w), lambda b, i: (b, i, 0))
    vec = lambda w: pl.BlockSpec((1, w), lambda b, i: (0, 0))
    per_b = lambda: pl.BlockSpec((1, 1, d), lambda b, i: (b, 0, 0))
    kern = functools.partial(_post_kernel, tm=tm, d_lru=d_lru)
    return pl.pallas_call(
        kern,
        out_shape=(
            jax.ShapeDtypeStruct((bsz, seq, d), F32),
            jax.ShapeDtypeStruct((bsz, seq * SUBLANES, LANES), F32),
            jax.ShapeDtypeStruct((bsz, seq, ROUTE_LANES), F32),
            jax.ShapeDtypeStruct((SUBLANES, ROUTE_LANES), F32),
            jax.ShapeDtypeStruct((SUBLANES, bsz * seq), F32),
        ),
        grid=(bsz, seq // tm),
        in_specs=[
            row(d_lru), row(d_lru),
            pl.BlockSpec((1, tm, d_lru), lambda b, i: (b, i, y_col)),
            row(d_hgrn), row(d_hgrn),
            pl.BlockSpec((1, tm, d_hgrn), lambda b, i: (b, i, g_col)),
            row(d), vec(d_lru), vec(d_hgrn),
            pl.BlockSpec((d, d), lambda b, i: (0, 0)),
            per_b(), vec(d), per_b(), per_b(),
            pl.BlockSpec((d, ROUTE_LANES), lambda b, i: (0, 0)),
            vec(ROUTE_LANES),
        ],
        out_specs=(
            row(d), pl.BlockSpec((1, tm * SUBLANES, LANES), lambda b, i: (b, i, 0)), row(ROUTE_LANES),
            pl.BlockSpec((SUBLANES, ROUTE_LANES), lambda b, i: (0, 0)),
            pl.BlockSpec((SUBLANES, tm), lambda b, i: (0, b * (seq // tm) + i)),
        ),
        scratch_shapes=[pltpu.VMEM((SUBLANES, ROUTE_LANES), F32)],
        compiler_params=_params(("arbitrary", "arbitrary")),
        name="post_mixer_router",
    )(lru_f, lru_b, proj, hg_f, hg_b, proj, x, nlw.reshape(1, d_lru), nhw.reshape(1, d_hgrn), wo_bf16,
      g_mix, nfw.reshape(1, d), sc_ffn, sh_ffn, wr_bf16, br)


def _tiles_load(ref, n, lead=()):
    return jnp.concatenate(
        [ref[(*lead, pl.ds(j, n, stride=SUBLANES), slice(None))] for j in range(SUBLANES)], axis=1)


def _tiles_store(ref, val, n, lead=()):
    for j in range(SUBLANES):
        ref[(*lead, pl.ds(j, n, stride=SUBLANES), slice(None))] = val[:, j * LANES:(j + 1) * LANES]


def _token_tile(ref, t):
    return ref.at[pl.ds(pl.multiple_of(t * SUBLANES, SUBLANES), SUBLANES)]


def _dest_kernel(start_ref, route_ref, o_ref):
    route = route_ref[...].astype(jnp.int32)
    start = jnp.zeros_like(route)
    for e in range(N_EXPERTS):
        start = jnp.where(route == e, start_ref[e], start)
    o_ref[...] = start + pltpu.roll(route, SUBLANES // 2, 0)


def _dest_rows(expert_start, route):
    return pl.pallas_call(
        _dest_kernel,
        out_shape=jax.ShapeDtypeStruct(route.shape, jnp.int32),
        grid_spec=pltpu.PrefetchScalarGridSpec(
            num_scalar_prefetch=1,
            grid=(1,),
            in_specs=[pl.BlockSpec(route.shape, lambda i, s: (0, 0))],
            out_specs=pl.BlockSpec(route.shape, lambda i, s: (0, 0)),
        ),
        compiler_params=pltpu.CompilerParams(dimension_semantics=("arbitrary",)),
        name="moe_dest_rows",
    )(expert_start, route)


def _dispatch_kernel(d1_ref, d2_ref, h_ref, z_ref, o_ref, sem, *, tb):
    del z_ref
    base = pl.program_id(0) * tb

    def issue(r, carry):
        t = base + r
        pltpu.make_async_copy(_token_tile(h_ref, r), _token_tile(o_ref, d1_ref[t]), sem).start(priority=0)
        pltpu.make_async_copy(_token_tile(h_ref, r), _token_tile(o_ref, d2_ref[t]), sem).start(priority=1)
        return carry

    lax.fori_loop(0, tb, issue, 0, unroll=DMA_ISSUE_UNROLL)
    for _ in range(2):
        pltpu.make_async_copy(h_ref, o_ref.at[pl.ds(0, tb * SUBLANES)], sem).wait()


def _dispatch(dest1, dest2, h_tiles, n_rows, *, tb=512):
    m = h_tiles.shape[0] // SUBLANES
    kern = functools.partial(_dispatch_kernel, tb=tb)
    return pl.pallas_call(
        kern,
        out_shape=jax.ShapeDtypeStruct((n_rows * SUBLANES, LANES), h_tiles.dtype),
        grid_spec=pltpu.PrefetchScalarGridSpec(
            num_scalar_prefetch=2,
            grid=(m // tb,),
            in_specs=[pl.BlockSpec((tb * SUBLANES, LANES), lambda i, d1, d2: (i, 0)),
                      pl.BlockSpec(memory_space=pl.ANY)],
            out_specs=pl.BlockSpec(memory_space=pl.ANY),
            scratch_shapes=[pltpu.SemaphoreType.DMA(())],
        ),
        input_output_aliases={3: 0},
        compiler_params=pltpu.CompilerParams(dimension_semantics=("arbitrary",), has_side_effects=True),
        name="moe_dispatch",
    )(dest1, dest2, h_tiles, jnp.zeros((n_rows * SUBLANES, LANES), h_tiles.dtype))


def _expert_kernel(be_ref, x_ref, wg_ref, wu_ref, wd_ref, o_ref, wg_s, wu_s, wd_s):
    i = pl.program_id(0)
    prev_expert = be_ref[jnp.maximum(i - 1, 0)]

    @pl.when((i == 0) | (be_ref[i] != prev_expert))
    def _():
        wg_s[...] = wg_ref[0, 0].astype(BF16)
        wu_s[...] = wu_ref[0, 0].astype(BF16)
        wd_s[...] = wd_ref[0, 0].astype(BF16)

    blk = x_ref.shape[0] // SUBLANES
    n_used = be_ref[pl.num_programs(0)]

    @pl.when(i < n_used)
    def _():
        x = _tiles_load(x_ref, blk).astype(BF16)
        gate = _dot(x, wg_s[...])
        up = _dot(x, wu_s[...])
        act = (gate * _sigmoid(gate)) * up
        _tiles_store(o_ref, _dot(act.astype(BF16), wd_s[...]), blk)

    @pl.when(i >= n_used)
    def _():
        o_ref[...] = jnp.zeros_like(o_ref)


def _experts(blk_expert, x_tiles, wg, wu, wd, layer):
    n_rows = x_tiles.shape[0] // SUBLANES
    d, de = wg.shape[-2:]
    blk = EXPERT_BLOCK
    n_blocks = n_rows // blk
    return pl.pallas_call(
        _expert_kernel,
        out_shape=jax.ShapeDtypeStruct((n_rows * SUBLANES, LANES), F32),
        grid_spec=pltpu.PrefetchScalarGridSpec(
            num_scalar_prefetch=1,
            grid=(n_blocks,),
            in_specs=[
                pl.BlockSpec((blk * SUBLANES, LANES), lambda i, be: (jnp.minimum(i, be[n_blocks] - 1), 0)),
                pl.BlockSpec((1, 1, d, de), lambda i, be: (layer, be[i], 0, 0)),
                pl.BlockSpec((1, 1, d, de), lambda i, be: (layer, be[i], 0, 0)),
                pl.BlockSpec((1, 1, de, d), lambda i, be: (layer, be[i], 0, 0)),
            ],
            out_specs=pl.BlockSpec((blk * SUBLANES, LANES), lambda i, be: (i, 0)),
            scratch_shapes=[pltpu.VMEM((d, de), BF16), pltpu.VMEM((d, de), BF16), pltpu.VMEM((de, d), BF16)],
        ),
        compiler_params=_params(("arbitrary",)),
        name="moe_experts",
    )(blk_expert, x_tiles, wg, wu, wd)


def _combine_kernel(d1_ref, d2_ref, y_ref, slab_ref, x_ref, g_ref, nw_ref, o_ref, ra0, rb0, ra1, rb1, sem,
                    *, tm, tiles, n_steps, final_norm):
    step = pl.program_id(0) * tiles + pl.program_id(1)
    bufs = ((ra0, rb0), (ra1, rb1))

    def gather(tile, slot):
        base = tile * tm
        r1_ref, r2_ref = bufs[slot]

        def issue(r, carry):
            t = base + r
            pltpu.make_async_copy(_token_tile(y_ref, d1_ref[t]), _token_tile(r1_ref, r),
                                  sem.at[slot]).start(priority=0)
            pltpu.make_async_copy(_token_tile(y_ref, d2_ref[t]), _token_tile(r2_ref, r),
                                  sem.at[slot]).start(priority=1)
            return carry

        lax.fori_loop(0, tm, issue, 0, unroll=DMA_ISSUE_UNROLL)

    @pl.when(step == 0)
    def _():
        gather(0, 0)

    for slot in range(2):
        @pl.when((step & 1) == slot)
        def _():
            @pl.when(step + 1 < n_steps)
            def _():
                gather(step + 1, 1 - slot)

            r1_ref, r2_ref = bufs[slot]
            pltpu.make_async_copy(y_ref.at[pl.ds(0, tm * SUBLANES)], r1_ref, sem.at[slot]).wait()
            pltpu.make_async_copy(y_ref.at[pl.ds(0, tm * SUBLANES)], r2_ref, sem.at[slot]).wait()
            slab = slab_ref[0]
            y = slab[:, 2:3] * _tiles_load(r1_ref, tm) + slab[:, 3:4] * _tiles_load(r2_ref, tm)
            out = x_ref[0] + g_ref[0] * y
            if final_norm:
                ms = jnp.mean(out * out, axis=-1, keepdims=True)
                out = out * lax.rsqrt(ms + NORM_EPS) * nw_ref[...]
            o_ref[0] = out


def _combine(dest1, dest2, y_buf, slab, x, g_ffn, norm_w, *, final_norm, tm=512):
    bsz, seq, d = x.shape
    tiles = seq // tm
    kern = functools.partial(_combine_kernel, tm=tm, tiles=tiles, n_steps=bsz * tiles, final_norm=final_norm)
    row_buf = pltpu.VMEM((tm * SUBLANES, LANES), F32)
    return pl.pallas_call(
        kern,
        out_shape=jax.ShapeDtypeStruct((bsz, seq, d), F32),
        grid_spec=pltpu.PrefetchScalarGridSpec(
            num_scalar_prefetch=2,
            grid=(bsz, tiles),
            in_specs=[
                pl.BlockSpec(memory_space=pl.ANY),
                pl.BlockSpec((1, tm, ROUTE_LANES), lambda b, i, d1, d2: (b, i, 0)),
                pl.BlockSpec((1, tm, d), lambda b, i, d1, d2: (b, i, 0)),
                pl.BlockSpec((1, 1, d), lambda b, i, d1, d2: (b, 0, 0)),
                pl.BlockSpec((1, d), lambda b, i, d1, d2: (0, 0)),
            ],
            out_specs=pl.BlockSpec((1, tm, d), lambda b, i, d1, d2: (b, i, 0)),
            scratch_shapes=[row_buf, row_buf, row_buf, row_buf, pltpu.SemaphoreType.DMA((2,))],
        ),
        compiler_params=_params(("arbitrary", "arbitrary")),
        name="moe_combine",
    )(dest1, dest2, y_buf, slab, x, g_ffn, norm_w.reshape(1, d))


def _block_diag(w):
    heads, hd, _ = w.shape
    n = heads * hd
    tiled = jnp.tile(w.reshape(n, hd), (1, heads))
    blk_r = lax.broadcasted_iota(jnp.int32, (n, n), 0) // hd
    blk_c = lax.broadcasted_iota(jnp.int32, (n, n), 1) // hd
    return jnp.where(blk_r == blk_c, tiled, 0.0)


def kernel(x, c, ada_w, ada_b, norm_mix_w, w_in, conv_w, conv_b, lru_wa, lru_ba, lru_wx, lru_bx, lru_lambda, norm_lru_w, hgrn_lb, norm_hgrn_w, w_out, norm_ffn_w, router_group_w, router_group_b, router_expert_w, router_expert_b, expert_w_gate, expert_w_up, expert_w_down, final_norm_w):
    bsz, seq, d = x.shape
    assert d == SUBLANES * LANES, "the MoE row movement keeps one (8, 128) tile per token"
    depth = ada_w.shape[0]
    d_lru = conv_w.shape[-1]
    d_hgrn = hgrn_lb.shape[-1]
    m = bsz * seq
    n_rows = m * 2 + N_EXPERTS * EXPERT_BLOCK
    n_blocks = n_rows // EXPERT_BLOCK

    mod = _modulation(c, ada_w, ada_b)
    lb_cum = jnp.cumsum(jax.nn.softmax(hgrn_lb.astype(F32), axis=0), axis=0)
    lb_all = lb_cum - lb_cum[0:1]

    for l in range(depth):
        sh_mix, sc_mix, g_mix, sh_ffn, sc_ffn, g_ffn = [
            mod[l, :, i * d:(i + 1) * d].reshape(bsz, 1, d) for i in range(6)]
        proj = _in_proj(x, norm_mix_w[l], sc_mix, sh_mix, w_in, l)
        wa_bd = jnp.stack([_block_diag(lru_wa[l, 0]), _block_diag(lru_wa[l, 1])]).astype(BF16)
        wx_bd = jnp.stack([_block_diag(lru_wx[l, 0]), _block_diag(lru_wx[l, 1])]).astype(BF16)
        lru = [
            _lru_scan(proj, conv_w[l], conv_b[l], wa_bd, lru_ba[l], wx_bd, lru_bx[l], lru_lambda[l],
                      reverse=rv)
            for rv in (False, True)]
        hg_f, hg_b = _hgrn(proj, lb_all[l], d_lru=d_lru, d_hgrn=d_hgrn)

        lane_pad = ROUTE_LANES - N_GROUPS - N_EXPERTS
        wr = jnp.pad(jnp.concatenate([router_group_w[l], router_expert_w[l]], axis=1), ((0, 0), (0, lane_pad)))
        br = jnp.pad(jnp.concatenate([router_group_b[l], router_expert_b[l]]), (0, lane_pad)).reshape(1, ROUTE_LANES)
        x_mid, h_ffn, slab, counts, route = _post_mixer(
            lru[0], lru[1], proj, hg_f, hg_b, x, norm_lru_w[l], norm_hgrn_w[l], w_out[l].astype(BF16), g_mix,
            norm_ffn_w[l], sc_ffn, sh_ffn, wr.astype(BF16), br)

        cnt = counts[0, :N_EXPERTS].astype(jnp.int32)
        padded = ((cnt + EXPERT_BLOCK - 1) // EXPERT_BLOCK) * EXPERT_BLOCK
        pend = jnp.cumsum(padded)
        pstart = pend - padded
        blk_start = jnp.arange(n_blocks, dtype=jnp.int32) * EXPERT_BLOCK
        blk_expert = jnp.minimum(jnp.sum(pend[None, :] <= blk_start[:, None], axis=1), N_EXPERTS - 1)
        blocks_used = (pend[N_EXPERTS - 1] // EXPERT_BLOCK).reshape(1)
        blk_expert = jnp.concatenate([blk_expert.astype(jnp.int32), blocks_used.astype(jnp.int32)])
        dest = _dest_rows(pstart.astype(jnp.int32), route)
        dest1, dest2 = dest[0], dest[1]

        x_buf = _dispatch(dest1, dest2, h_ffn.reshape(m * SUBLANES, LANES), n_rows)
        y_buf = _experts(blk_expert, x_buf, expert_w_gate, expert_w_up, expert_w_down, l)
        x = _combine(dest1, dest2, y_buf, slab, x_mid, g_ffn, final_norm_w, final_norm=(l == depth - 1))

    return x
```

```python
import functools

import jax
import jax.numpy as jnp
from jax import lax
from jax.experimental import pallas as pl
from jax.experimental.pallas import tpu as pltpu

F32 = jnp.float32
BF16 = jnp.bfloat16

LRU_HEADS = 8
HGRN_HEADS = 8
CONV_WIDTH = 4
LRU_C = 8.0
N_GROUPS = 4
EXPERTS_PER_GROUP = 8
N_EXPERTS = N_GROUPS * EXPERTS_PER_GROUP
NORM_EPS = 1e-6

LANES = 128
SUBLANES = 8
VMEM_LIMIT = 56 * 1024 * 1024

HGRN_CHUNK = 64
HGRN_SUB = 8
LOG2E = 1.4426950408889634
ROUTE_LANES = LANES
EXPERT_BLOCK = 512
DMA_ISSUE_UNROLL = 8
NEG_BIG = -3.0e38


def _params(sem):
    return pltpu.CompilerParams(dimension_semantics=sem, vmem_limit_bytes=VMEM_LIMIT)


def _dot(a, b):
    return jnp.dot(a, b, preferred_element_type=F32)


def _dot_nt(a, b):
    return lax.dot_general(a, b, (((1,), (1,)), ((), ())), preferred_element_type=F32)


def _dot_tn(a, b):
    return lax.dot_general(a, b, (((0,), (0,)), ((), ())), preferred_element_type=F32)


def _dot01_exact(m01, x):
    hi = x.astype(BF16)
    r1 = x - hi.astype(F32)
    mid = r1.astype(BF16)
    lo = (r1 - mid.astype(F32)).astype(BF16)
    return _dot(m01, hi) + _dot(m01, mid) + _dot(m01, lo)


def _sigmoid(x):
    return 1.0 / (1.0 + jnp.exp(-x))


def _mod_kernel(c_ref, w_ref, b_ref, o_ref):
    c = c_ref[...]
    cond = c * _sigmoid(c)
    o_ref[0] = _dot(cond.astype(BF16), w_ref[0].astype(BF16)) + b_ref[0]


def _modulation(c, ada_w, ada_b):
    depth, d, n = ada_w.shape
    bsz = c.shape[0]
    rows = -(-bsz // SUBLANES) * SUBLANES
    c_pad = jnp.pad(c, ((0, rows - bsz), (0, 0)))
    tn = n // 6
    out = pl.pallas_call(
        _mod_kernel,
        out_shape=jax.ShapeDtypeStruct((depth, rows, n), F32),
        grid=(depth, n // tn),
        in_specs=[
            pl.BlockSpec((rows, d), lambda l, j: (0, 0)),
            pl.BlockSpec((1, d, tn), lambda l, j: (l, 0, j)),
            pl.BlockSpec((1, 1, tn), lambda l, j: (l, 0, j)),
        ],
        out_specs=pl.BlockSpec((1, rows, tn), lambda l, j: (l, 0, j)),
        compiler_params=_params(("arbitrary", "arbitrary")),
        name="adaln_mod",
    )(c_pad, ada_w, ada_b.reshape(depth, 1, n))
    return out[:, :bsz]


def _rms_mod(x, nw, sc, sh):
    ms = jnp.mean(x * x, axis=-1, keepdims=True)
    return (x * lax.rsqrt(ms + NORM_EPS) * nw) * (1.0 + sc) + sh


def _inproj_kernel(x_ref, nw_ref, sc_ref, sh_ref, w_ref, o_ref, w_s):
    @pl.when((pl.program_id(0) == 0) & (pl.program_id(1) == 0))
    def _():
        w_s[...] = w_ref[0].astype(BF16)

    h = _rms_mod(x_ref[0], nw_ref[...], sc_ref[0], sh_ref[0])
    o_ref[0] = _dot(h.astype(BF16), w_s[...])


def _in_proj(x, nw, sc, sh, w_in, layer, tm=512):
    bsz, seq, d = x.shape
    n = w_in.shape[-1]
    return pl.pallas_call(
        _inproj_kernel,
        out_shape=jax.ShapeDtypeStruct((bsz, seq, n), F32),
        grid=(bsz, seq // tm),
        in_specs=[
            pl.BlockSpec((1, tm, d), lambda b, i: (b, i, 0)),
            pl.BlockSpec((1, d), lambda b, i: (0, 0)),
            pl.BlockSpec((1, 1, d), lambda b, i: (b, 0, 0)),
            pl.BlockSpec((1, 1, d), lambda b, i: (b, 0, 0)),
            pl.BlockSpec((1, d, n), lambda b, i: (layer, 0, 0), pipeline_mode=pl.Buffered(1)),
        ],
        out_specs=pl.BlockSpec((1, tm, n), lambda b, i: (b, i, 0)),
        scratch_shapes=[pltpu.VMEM((d, n), BF16)],
        compiler_params=_params(("arbitrary", "arbitrary")),
        name="in_proj",
    )(x, nw.reshape(1, d), sc, sh, w_in)


def _lru_kernel(x_ref, xp_ref, xn_ref, cw_ref, cb_ref, wa_ref, ba_ref, wx_ref, bx_ref, lam_ref,
                o_ref, carry_ref, sa_ref, sb_ref, cin_ref, *, reverse, n_chunks, rows):
    c = pl.program_id(1)
    chunk = (n_chunks - 1 - c) if reverse else c

    @pl.when(c == 0)
    def _():
        carry_ref[...] = jnp.zeros_like(carry_ref)

    x = x_ref[0]
    width = x.shape[1]
    row = lax.broadcasted_iota(jnp.int32, (rows, width), 0)
    has_prev = jnp.where(chunk > 0, 1.0, 0.0)
    has_next = jnp.where(chunk < n_chunks - 1, 1.0, 0.0)
    xp = xp_ref[0] * has_prev
    xn = xn_ref[0] * has_next
    xe = jnp.concatenate([xp, x, xn], axis=0)
    h8 = SUBLANES
    cw = cw_ref[...]
    xc = (cw[0:1] * xe[h8 - 2:h8 - 2 + rows] + cw[1:2] * xe[h8 - 1:h8 - 1 + rows] + cw[2:3] * x
          + cw[3:4] * xe[h8 + 1:h8 + 1 + rows] + cb_ref[...])

    xcb = xc.astype(BF16)
    r = _sigmoid(_dot(xcb, wa_ref[0]) + ba_ref[0])
    gate_i = _sigmoid(_dot(xcb, wx_ref[0]) + bx_ref[0])
    lam = lam_ref[0]
    softplus_neg_lam = jnp.maximum(-lam, 0.0) + jnp.log1p(jnp.exp(-jnp.abs(lam)))
    log_a = (-LRU_C) * r * softplus_neg_lam
    a = jnp.exp(log_a)
    t = jnp.tanh(-log_a)
    u = jnp.sqrt(2.0 * t / (1.0 + t)) * (gate_i * xc)

    groups = rows // SUBLANES
    acc_a = a.reshape(groups, SUBLANES, width)
    acc_b = u.reshape(groups, SUBLANES, width)
    sub = lax.broadcasted_iota(jnp.int32, (groups, SUBLANES, width), 1)
    s = 1
    while s < SUBLANES:
        if reverse:
            valid = sub < SUBLANES - s
            sh_a, sh_b = pltpu.roll(acc_a, SUBLANES - s, 1), pltpu.roll(acc_b, SUBLANES - s, 1)
        else:
            valid = sub >= s
            sh_a, sh_b = pltpu.roll(acc_a, s, 1), pltpu.roll(acc_b, s, 1)
        acc_b = jnp.where(valid, acc_a * sh_b + acc_b, acc_b)
        acc_a = jnp.where(valid, acc_a * sh_a, acc_a)
        s *= 2
    acc_a = acc_a.reshape(rows, width)
    acc_b = acc_b.reshape(rows, width)
    groups = rows // SUBLANES
    edge = 0 if reverse else SUBLANES - 1
    n_tiles = width // LANES
    for j in range(n_tiles):
        sa_ref[j] = acc_a[:, j * LANES:(j + 1) * LANES]
        sb_ref[j] = acc_b[:, j * LANES:(j + 1) * LANES]
    ea = jnp.concatenate([sa_ref[j, pl.ds(edge, groups, stride=SUBLANES), :] for j in range(n_tiles)], axis=1)
    eb = jnp.concatenate([sb_ref[j, pl.ds(edge, groups, stride=SUBLANES), :] for j in range(n_tiles)], axis=1)
    grow = lax.broadcasted_iota(jnp.int32, (groups, width), 0)
    s = 1
    while s < groups:
        if reverse:
            valid = grow < groups - s
            sh_a, sh_b = pltpu.roll(ea, groups - s, 0), pltpu.roll(eb, groups - s, 0)
        else:
            valid = grow >= s
            sh_a, sh_b = pltpu.roll(ea, s, 0), pltpu.roll(eb, s, 0)
        eb = jnp.where(valid, ea * sh_b + eb, eb)
        ea = jnp.where(valid, ea * sh_a, ea)
        s *= 2
    carry = carry_ref[...]
    group_out = eb + ea * carry
    if reverse:
        carry_in = jnp.where(grow == groups - 1, carry, pltpu.roll(group_out, groups - 1, 0))
        carry_ref[...] = group_out[0:1]
    else:
        carry_in = jnp.where(grow == 0, carry, pltpu.roll(group_out, 1, 0))
        carry_ref[...] = group_out[groups - 1:groups]
    cin_ref[...] = carry_in
    for g in range(groups):
        rs = slice(g * SUBLANES, (g + 1) * SUBLANES)
        o_ref[0, rs, :] = acc_b[rs] + acc_a[rs] * cin_ref[g:g + 1, :]


def _lru_scan(proj, conv_w, conv_b, wa_bd, ba, wx_bd, bx, lam, *, reverse, rows=512):
    bsz, seq, _ = proj.shape
    d_lru = conv_w.shape[1]
    n_chunks = seq // rows
    halo = rows // SUBLANES
    last_halo = seq // SUBLANES - 1
    dirn = 1 if reverse else 0

    def chunk_of(c):
        return (n_chunks - 1 - c) if reverse else c

    vec = lambda: pl.BlockSpec((1, 1, d_lru), lambda b, c: (dirn, 0, 0))
    mat = lambda: pl.BlockSpec((1, d_lru, d_lru), lambda b, c: (dirn, 0, 0))
    kern = functools.partial(_lru_kernel, reverse=reverse, n_chunks=n_chunks, rows=rows)
    return pl.pallas_call(
        kern,
        out_shape=jax.ShapeDtypeStruct((bsz, seq, d_lru), F32),
        grid=(bsz, n_chunks),
        in_specs=[
            pl.BlockSpec((1, rows, d_lru), lambda b, c: (b, chunk_of(c), 0)),
            pl.BlockSpec((1, SUBLANES, d_lru),
                         lambda b, c: (b, jnp.maximum(chunk_of(c) * halo - 1, 0), 0)),
            pl.BlockSpec((1, SUBLANES, d_lru),
                         lambda b, c: (b, jnp.minimum((chunk_of(c) + 1) * halo, last_halo), 0)),
            pl.BlockSpec((CONV_WIDTH, d_lru), lambda b, c: (0, 0)),
            pl.BlockSpec((1, d_lru), lambda b, c: (0, 0)),
            mat(), vec(), mat(), vec(), vec(),
        ],
        out_specs=pl.BlockSpec((1, rows, d_lru), lambda b, c: (b, chunk_of(c), 0)),
        scratch_shapes=[pltpu.VMEM((1, d_lru), F32), pltpu.VMEM((d_lru // LANES, rows, LANES), F32),
                        pltpu.VMEM((d_lru // LANES, rows, LANES), F32),
                        pltpu.VMEM((rows // SUBLANES, d_lru), F32)],
        compiler_params=_params(("arbitrary", "arbitrary")),
        name="lru_bwd" if reverse else "lru_fwd",
    )(proj, proj, proj, conv_w, conv_b.reshape(1, d_lru), wa_bd, ba.reshape(2, 1, d_lru),
      wx_bd, bx.reshape(2, 1, d_lru), lam.reshape(2, 1, d_lru))


def _hgrn_direction(rev, q_ref, f_ref, v_ref, lb_ref, o_ref, st_ref, diag_s, lvl_s, upd_s, qe_s, btot_s, *, rows):
    ck, sb = HGRN_CHUNK, HGRN_SUB
    n_blk = ck // sb
    sb_shift = sb.bit_length() - 1
    n_sub = rows // ck
    width = q_ref.shape[-1]
    n_pairs = width // LANES
    half = LANES // 2

    def flip(idx, n):
        return (n - 1 - idx) if rev else idx

    n_lvl = n_blk.bit_length() - 1
    tf = flip(lax.broadcasted_iota(jnp.int32, (ck, ck), 0), ck)
    uf = flip(lax.broadcasted_iota(jnp.int32, (ck, ck), 1), ck)
    tb, ub = tf >> sb_shift, uf >> sb_shift
    pb = flip(lax.broadcasted_iota(jnp.int32, (n_blk, ck), 0), n_blk)
    pub = flip(lax.broadcasted_iota(jnp.int32, (n_blk, ck), 1), ck) >> sb_shift
    mats = [jnp.where((tb == ub) & (uf <= tf), 1.0, 0.0),
            jnp.where(pub < pb, 1.0, 0.0)]
    for lvl in range(n_lvl):
        mid = ((pb >> (lvl + 1)) << (lvl + 1)) + (1 << lvl)
        mats.append(jnp.where(pub < mid, 1.0, 0.0))
    mats.append(jnp.ones((SUBLANES, ck), F32))
    m_cum = jnp.concatenate(mats, axis=0).astype(BF16)

    def per_block(rows8):
        return jnp.concatenate(
            [jnp.broadcast_to(rows8[jb:jb + 1], (sb, rows8.shape[1])) for jb in range(n_blk)], axis=0)
    row_blk = flip(lax.broadcasted_iota(jnp.int32, (ck, width), 0), ck) >> sb_shift
    upper = [((row_blk >> lvl) & 1) == 1 for lvl in range(n_lvl)]
    pr = flip(lax.broadcasted_iota(jnp.int32, (ck, LANES), 0), ck) >> sb_shift
    pc = flip(lax.broadcasted_iota(jnp.int32, (ck, LANES), 1) & (ck - 1), ck) >> sb_shift
    group_mask = [(pr >> (lvl + 1)) == (pc >> (lvl + 1)) for lvl in range(n_lvl)]
    lane = lax.broadcasted_iota(jnp.int32, (1, LANES), 1)
    head0 = lane < half

    def split_heads(x):
        xb = x.astype(BF16)
        zero = jnp.zeros_like(xb)
        return jnp.concatenate([jnp.where(head0, xb, zero), jnp.where(head0, zero, xb)], axis=0)

    sr = lax.broadcasted_iota(jnp.int32, (LANES, LANES), 0)
    sc = lax.broadcasted_iota(jnp.int32, (LANES, LANES), 1)
    same_head = (sr < half) == (sc < half)
    er = lax.broadcasted_iota(jnp.int32, (sb * LANES, LANES), 0)
    ec = lax.broadcasted_iota(jnp.int32, (sb * LANES, LANES), 1)
    sel = jnp.where(ec == (((er & (LANES - 1)) >> (half.bit_length() - 1)) * half + (er >> (LANES.bit_length() - 1))),
                    1.0, 0.0).astype(BF16)
    sub_row = flip(lax.broadcasted_iota(jnp.int32, (sb, LANES), 0), sb)
    lbv = lb_ref[...]

    def row_start(j):
        return pl.multiple_of(flip(j, n_sub) * ck, ck)

    def stage1a(j):
        r0 = row_start(j)
        q = q_ref[0, pl.ds(r0, ck), :]
        z = f_ref[0, pl.ds(r0, ck), :]
        v = v_ref[0, pl.ds(r0, ck), :]
        f = lbv + (1.0 - lbv) * _sigmoid(z)
        lf2 = jnp.log(f) * LOG2E
        k = 1.0 - f
        return q, v, k, _dot01_exact(m_cum, lf2)

    def stage1b(q, v, k, cums):
        bl2 = cums[0:ck]
        b2 = bl2 + per_block(cums[ck:ck + n_blk])
        tot_row = ck + (1 + n_lvl) * n_blk
        btot2 = cums[tot_row:tot_row + 1]
        log2_k = jnp.log(k) * LOG2E
        kb = b2 - log2_k
        kbl = bl2 - log2_k
        qe = q * jnp.exp2(b2)
        ke = jnp.exp2(btot2 - kb)
        q_lvl, k_lvl = [], []
        for lvl in range(n_lvl):
            split2 = per_block(cums[ck + (1 + lvl) * n_blk:ck + (2 + lvl) * n_blk])
            q_lvl.append(q * jnp.exp2(jnp.where(upper[lvl], b2 - split2, NEG_BIG)))
            k_lvl.append(jnp.exp2(jnp.where(upper[lvl], NEG_BIG, split2 - kb)))

        qe_s[...] = qe.astype(BF16)
        btot_s[...] = btot2
        for p in range(n_pairs):
            sl = slice(p * LANES, (p + 1) * LANES)
            diag_rows = []
            for jb in range(n_blk):
                rs = slice(jb * sb, (jb + 1) * sb)
                bl_b, kbl_b, q_b = bl2[rs, sl], kbl[rs, sl], q[rs, sl]
                terms = []
                for s in range(sb):
                    arg = jnp.where(sub_row >= flip(s, sb), bl_b - kbl_b[s:s + 1], NEG_BIG)
                    terms.append(q_b * jnp.exp2(arg))
                diag_rows.append(jnp.concatenate(terms, axis=1))
            diag_s[p] = _dot(jnp.concatenate(diag_rows, axis=0).astype(BF16), sel)
            for lvl in range(n_lvl):
                k_p = k_lvl[lvl][:, sl]
                lvl_s[p * n_lvl + lvl] = _dot_nt(q_lvl[lvl][:, sl].astype(BF16), split_heads(k_p))
            upd_s[p] = _dot_tn(v[:, sl].astype(BF16), ke[:, sl].astype(BF16))

    def stage2_issue(j):
        r0 = row_start(j)
        v = v_ref[0, pl.ds(r0, ck), :]
        out = []
        for p in range(n_pairs):
            sl = slice(p * LANES, (p + 1) * LANES)
            parts = []
            for jb in range(n_blk):
                blk = diag_s[p, jb * sb:(jb + 1) * sb, :]
                parts.append(pltpu.roll(blk, jb * sb, 1) if jb else blk)
            scores = jnp.concatenate(parts, axis=0)
            for lvl in range(n_lvl):
                s_lvl = lvl_s[p * n_lvl + lvl]
                scores = scores + (s_lvl if lvl == n_lvl - 1 else jnp.where(group_mask[lvl], s_lvl, 0.0))
            intra = _dot(scores.astype(BF16), split_heads(v[:, sl]))
            st = st_ref[p]
            inter = _dot_nt(qe_s[:, sl], st.astype(BF16))
            new_st = jnp.where(same_head, st * jnp.exp2(btot_s[:, sl]) + upd_s[p], 0.0)
            out.append((inter + intra, new_st))
        return r0, out

    def stage2_finish(r0, out):
        for p in range(n_pairs):
            o_ref[0, pl.ds(r0, ck), p * LANES:(p + 1) * LANES] = out[p][0]
            st_ref[p] = out[p][1]

    return stage1a, stage1b, stage2_issue, stage2_finish


N_HGRN_SCRATCH = 6


def _hgrn_kernel(qf_ref, ff_ref, vf_ref, qb_ref, fb_ref, vb_ref, lb_ref, of_ref, ob_ref, *scratch, rows):
    fwd_scratch, bwd_scratch = scratch[:N_HGRN_SCRATCH], scratch[N_HGRN_SCRATCH:]

    @pl.when(pl.program_id(1) == 0)
    def _():
        fwd_scratch[0][...] = jnp.zeros_like(fwd_scratch[0])
        bwd_scratch[0][...] = jnp.zeros_like(bwd_scratch[0])

    f1a, f1b, f2, f3 = _hgrn_direction(False, qf_ref, ff_ref, vf_ref, lb_ref, of_ref, *fwd_scratch, rows=rows)
    b1a, b1b, b2, b3 = _hgrn_direction(True, qb_ref, fb_ref, vb_ref, lb_ref, ob_ref, *bwd_scratch, rows=rows)
    n_sub = rows // HGRN_CHUNK

    def stage1_both(j):
        fa = f1a(j)
        ba = b1a(j)
        f1b(*fa)
        b1b(*ba)

    stage1_both(0)

    def pipelined(j, carry):
        fo = f2(j)
        bo = b2(j)
        stage1_both(j + 1)
        f3(*fo)
        b3(*bo)
        return carry

    lax.fori_loop(0, n_sub - 1, pipelined, 0)
    fo = f2(n_sub - 1)
    bo = b2(n_sub - 1)
    f3(*fo)
    b3(*bo)


def _hgrn(proj, lb, *, d_lru, d_hgrn, rows=512):
    bsz, seq, _ = proj.shape
    n_chunks = seq // rows
    col0 = (2 * d_lru) // d_hgrn
    n_pairs = d_hgrn // LANES
    n_lvl = (HGRN_CHUNK // HGRN_SUB).bit_length() - 1
    fwd = lambda col: pl.BlockSpec((1, rows, d_hgrn), lambda b, c: (b, c, col))
    bwd = lambda col: pl.BlockSpec((1, rows, d_hgrn), lambda b, c: (b, n_chunks - 1 - c, col))
    direction_scratch = [
        pltpu.VMEM((n_pairs, LANES, LANES), F32),
        pltpu.VMEM((n_pairs, HGRN_CHUNK, LANES), F32),
        pltpu.VMEM((n_pairs * n_lvl, HGRN_CHUNK, LANES), F32),
        pltpu.VMEM((n_pairs, LANES, LANES), F32),
        pltpu.VMEM((HGRN_CHUNK, d_hgrn), BF16),
        pltpu.VMEM((1, d_hgrn), F32),
    ]
    assert len(direction_scratch) == N_HGRN_SCRATCH
    kern = functools.partial(_hgrn_kernel, rows=rows)
    out = jax.ShapeDtypeStruct((bsz, seq, d_hgrn), F32)
    return pl.pallas_call(
        kern,
        out_shape=(out, out),
        grid=(bsz, n_chunks),
        in_specs=[fwd(col0), fwd(col0 + 1), fwd(col0 + 3), bwd(col0), bwd(col0 + 2), bwd(col0 + 3),
                  pl.BlockSpec((1, d_hgrn), lambda b, c: (0, 0))],
        out_specs=(pl.BlockSpec((1, rows, d_hgrn), lambda b, c: (b, c, 0)),
                   pl.BlockSpec((1, rows, d_hgrn), lambda b, c: (b, n_chunks - 1 - c, 0))),
        scratch_shapes=direction_scratch + direction_scratch,
        compiler_params=_params(("arbitrary", "arbitrary")),
        name="hgrn2",
    )(proj, proj, proj, proj, proj, proj, lb.reshape(1, d_hgrn))


def _gelu_tanh(y):
    return 0.5 * y * (1.0 + jnp.tanh(0.7978845608028654 * (y + 0.044715 * (y * y * y))))


def _post_kernel(lf_ref, lb_ref, y_ref, of_ref, ob_ref, g_ref, x_ref, nlw_ref, nhw_ref, wo_ref,
                 gm_ref, nfw_ref, scf_ref, shf_ref, wr_ref, br_ref,
                 xo_ref, h_ref, slab_ref, cnt_ref, route_ref, carry_ref, *, tm, d_lru):
    first = (pl.program_id(0) == 0) & (pl.program_id(1) == 0)

    @pl.when(first)
    def _():
        carry_ref[...] = jnp.zeros_like(carry_ref)

    lru = (lf_ref[0] + lb_ref[0]) * _gelu_tanh(y_ref[0])
    ms = jnp.mean(lru * lru, axis=-1, keepdims=True)
    lru = lru * lax.rsqrt(ms + NORM_EPS) * nlw_ref[...]

    hg = of_ref[0] + ob_ref[0]
    width = hg.shape[1]
    hd = width // HGRN_HEADS
    hd_shift = hd.bit_length() - 1
    er = lax.broadcasted_iota(jnp.int32, (width, width), 0) >> hd_shift
    ec = lax.broadcasted_iota(jnp.int32, (width, width), 1) >> hd_shift
    head_sum = jnp.where(er == ec, 1.0, 0.0).astype(BF16)
    sq = hg * hg
    sq_hi = sq.astype(BF16)
    sq_lo = (sq - sq_hi.astype(F32)).astype(BF16)
    ms_h = (_dot(sq_hi, head_sum) + _dot(sq_lo, head_sum)) * (1.0 / hd)
    g = g_ref[0]
    hg = (hg * lax.rsqrt(ms_h + NORM_EPS) * nhw_ref[...]) * (g * _sigmoid(g))

    mixed = _dot(lru.astype(BF16), wo_ref[0:d_lru, :]) + _dot(hg.astype(BF16), wo_ref[d_lru:, :])
    x_new = x_ref[0] + gm_ref[0] * mixed
    xo_ref[0] = x_new

    h = _rms_mod(x_new, nfw_ref[...], scf_ref[0], shf_ref[0])
    _tiles_store(h_ref, h, tm, lead=(0,))

    logits = _dot(h.astype(BF16), wr_ref[...]) + br_ref[...]
    lane = lax.broadcasted_iota(jnp.int32, (tm, ROUTE_LANES), 1)
    lane_f = lane.astype(F32)
    far = float(ROUTE_LANES)
    is_g = lane < N_GROUPS
    gl = jnp.where(is_g, logits, NEG_BIG)
    gmax = jnp.max(gl, axis=-1, keepdims=True)
    g_idx = jnp.min(jnp.where(gl == gmax, lane_f, far), axis=-1, keepdims=True)
    p_group = 1.0 / jnp.sum(jnp.where(is_g, jnp.exp(gl - gmax), 0.0), axis=-1, keepdims=True)
    e_lane = lane - N_GROUPS
    in_group = (e_lane >= 0) & (e_lane < N_EXPERTS) & ((e_lane >> (EXPERTS_PER_GROUP.bit_length() - 1)).astype(F32) == g_idx)
    ev = jnp.where(in_group, logits, NEG_BIG)
    top1 = jnp.max(ev, axis=-1, keepdims=True)
    i1 = jnp.min(jnp.where(in_group & (ev == top1), lane_f, far), axis=-1, keepdims=True)
    rest = in_group & (lane_f != i1)
    ev2 = jnp.where(rest, logits, NEG_BIG)
    top2 = jnp.max(ev2, axis=-1, keepdims=True)
    i2 = jnp.min(jnp.where(rest & (ev2 == top2), lane_f, far), axis=-1, keepdims=True)
    e1 = i1 - float(N_GROUPS)
    e2 = i2 - float(N_GROUPS)
    ex = jnp.exp(top2 - top1)
    w1 = p_group / (1.0 + ex)
    w2 = p_group * ex / (1.0 + ex)

    sel1 = lane_f == e1
    sel2 = lane_f == e2
    onehot = jnp.where(sel1 | sel2, 1.0, 0.0)
    tr = lax.broadcasted_iota(jnp.int32, (tm, tm), 0)
    tc = lax.broadcasted_iota(jnp.int32, (tm, tm), 1)
    before = jnp.where(tc < tr, 1.0, 0.0).astype(BF16)
    cnt = _dot(before, onehot.astype(BF16)) + carry_ref[0:1]
    rank1 = jnp.sum(jnp.where(sel1, cnt, 0.0), axis=-1, keepdims=True)
    rank2 = jnp.sum(jnp.where(sel2, cnt, 0.0), axis=-1, keepdims=True)
    total = carry_ref[0:1] + jnp.sum(onehot, axis=0, keepdims=True)
    carry_ref[...] = jnp.broadcast_to(total, carry_ref.shape)
    cnt_ref[...] = jnp.broadcast_to(total, cnt_ref.shape)

    slab = jnp.where(lane == 0, e1, 0.0)
    slab = jnp.where(lane == 1, e2, slab)
    slab = jnp.where(lane == 2, w1, slab)
    slab = jnp.where(lane == 3, w2, slab)
    slab = jnp.where(lane == 4, rank1, slab)
    slab = jnp.where(lane == 5, rank2, slab)
    slab_ref[0] = slab
    route_ref[...] = slab.T[0:SUBLANES]


def _post_mixer(lru_f, lru_b, proj, hg_f, hg_b, x, nlw, nhw, wo_bf16, g_mix, nfw, sc_ffn, sh_ffn, wr_bf16, br,
                *, tm=512):
    bsz, seq, d = x.shape
    d_lru = lru_f.shape[-1]
    d_hgrn = hg_f.shape[-1]
    y_col = 1
    g_col = (2 * d_lru) // d_hgrn + 4
    row = lambda w: pl.BlockSpec((1, tm, w), lambda b, i: (b, i, 0))
    vec = lambda w: pl.BlockSpec((1, w), lambda b, i: (0, 0))
    per_b = lambda: pl.BlockSpec((1, 1, d), lambda b, i: (b, 0, 0))
    kern = functools.partial(_post_kernel, tm=tm, d_lru=d_lru)
    return pl.pallas_call(
        kern,
        out_shape=(
            jax.ShapeDtypeStruct((bsz, seq, d), F32),
            jax.ShapeDtypeStruct((bsz, seq * SUBLANES, LANES), F32),
            jax.ShapeDtypeStruct((bsz, seq, ROUTE_LANES), F32),
            jax.ShapeDtypeStruct((SUBLANES, ROUTE_LANES), F32),
            jax.ShapeDtypeStruct((SUBLANES, bsz * seq), F32),
        ),
        grid=(bsz, seq // tm),
        in_specs=[
            row(d_lru), row(d_lru),
            pl.BlockSpec((1, tm, d_lru), lambda b, i: (b, i, y_col)),
            row(d_hgrn), row(d_hgrn),
            pl.BlockSpec((1, tm, d_hgrn), lambda b, i: (b, i, g_col)),
            row(d), vec(d_lru), vec(d_hgrn),
            pl.BlockSpec((d, d), lambda b, i: (0, 0)),
            per_b(), vec(d), per_b(), per_b(),
            pl.BlockSpec((d, ROUTE_LANES), lambda b, i: (0, 0)),
            vec(ROUTE_LANES),
        ],
        out_specs=(
            row(d), pl.BlockSpec((1, tm * SUBLANES, LANES), lambda b, i: (b, i, 0)), row(ROUTE_LANES),
            pl.BlockSpec((SUBLANES, ROUTE_LANES), lambda b, i: (0, 0)),
            pl.BlockSpec((SUBLANES, tm), lambda b, i: (0, b * (seq // tm) + i)),
        ),
        scratch_shapes=[pltpu.VMEM((SUBLANES, ROUTE_LANES), F32)],
        compiler_params=_params(("arbitrary", "arbitrary")),
        name="post_mixer_router",
    )(lru_f, lru_b, proj, hg_f, hg_b, proj, x, nlw.reshape(1, d_lru), nhw.reshape(1, d_hgrn), wo_bf16,
      g_mix, nfw.reshape(1, d), sc_ffn, sh_ffn, wr_bf16, br)


def _tiles_load(ref, n, lead=()):
    return jnp.concatenate(
        [ref[(*lead, pl.ds(j, n, stride=SUBLANES), slice(None))] for j in range(SUBLANES)], axis=1)


def _tiles_store(ref, val, n, lead=()):
    for j in range(SUBLANES):
        ref[(*lead, pl.ds(j, n, stride=SUBLANES), slice(None))] = val[:, j * LANES:(j + 1) * LANES]


def _token_tile(ref, t):
    return ref.at[pl.ds(pl.multiple_of(t * SUBLANES, SUBLANES), SUBLANES)]


def _dest_kernel(start_ref, route_ref, o_ref):
    route = route_ref[...].astype(jnp.int32)
    start = jnp.zeros_like(route)
    for e in range(N_EXPERTS):
        start = jnp.where(route == e, start_ref[e], start)
    o_ref[...] = start + pltpu.roll(route, SUBLANES // 2, 0)


def _dest_rows(expert_start, route):
    return pl.pallas_call(
        _dest_kernel,
        out_shape=jax.ShapeDtypeStruct(route.shape, jnp.int32),
        grid_spec=pltpu.PrefetchScalarGridSpec(
            num_scalar_prefetch=1,
            grid=(1,),
            in_specs=[pl.BlockSpec(route.shape, lambda i, s: (0, 0))],
            out_specs=pl.BlockSpec(route.shape, lambda i, s: (0, 0)),
        ),
        compiler_params=pltpu.CompilerParams(dimension_semantics=("arbitrary",)),
        name="moe_dest_rows",
    )(expert_start, route)


def _dispatch_kernel(d1_ref, d2_ref, h_ref, z_ref, o_ref, sem, *, tb):
    del z_ref
    base = pl.program_id(0) * tb

    def issue(r, carry):
        t = base + r
        pltpu.make_async_copy(_token_tile(h_ref, r), _token_tile(o_ref, d1_ref[t]), sem).start(priority=0)
        pltpu.make_async_copy(_token_tile(h_ref, r), _token_tile(o_ref, d2_ref[t]), sem).start(priority=1)
        return carry

    lax.fori_loop(0, tb, issue, 0, unroll=DMA_ISSUE_UNROLL)
    for _ in range(2):
        pltpu.make_async_copy(h_ref, o_ref.at[pl.ds(0, tb * SUBLANES)], sem).wait()


def _dispatch(dest1, dest2, h_tiles, n_rows, *, tb=512):
    m = h_tiles.shape[0] // SUBLANES
    kern = functools.partial(_dispatch_kernel, tb=tb)
    return pl.pallas_call(
        kern,
        out_shape=jax.ShapeDtypeStruct((n_rows * SUBLANES, LANES), h_tiles.dtype),
        grid_spec=pltpu.PrefetchScalarGridSpec(
            num_scalar_prefetch=2,
            grid=(m // tb,),
            in_specs=[pl.BlockSpec((tb * SUBLANES, LANES), lambda i, d1, d2: (i, 0)),
                      pl.BlockSpec(memory_space=pl.ANY)],
            out_specs=pl.BlockSpec(memory_space=pl.ANY),
            scratch_shapes=[pltpu.SemaphoreType.DMA(())],
        ),
        input_output_aliases={3: 0},
        compiler_params=pltpu.CompilerParams(dimension_semantics=("arbitrary",), has_side_effects=True),
        name="moe_dispatch",
    )(dest1, dest2, h_tiles, jnp.zeros((n_rows * SUBLANES, LANES), h_tiles.dtype))


def _expert_kernel(be_ref, x_ref, wg_ref, wu_ref, wd_ref, o_ref, wg_s, wu_s, wd_s):
    i = pl.program_id(0)
    prev_expert = be_ref[jnp.maximum(i - 1, 0)]

    @pl.when((i == 0) | (be_ref[i] != prev_expert))
    def _():
        wg_s[...] = wg_ref[0, 0].astype(BF16)
        wu_s[...] = wu_ref[0, 0].astype(BF16)
        wd_s[...] = wd_ref[0, 0].astype(BF16)

    blk = x_ref.shape[0] // SUBLANES
    n_used = be_ref[pl.num_programs(0)]

    @pl.when(i < n_used)
    def _():
        x = _tiles_load(x_ref, blk).astype(BF16)
        gate = _dot(x, wg_s[...])
        up = _dot(x, wu_s[...])
        act = (gate * _sigmoid(gate)) * up
        _tiles_store(o_ref, _dot(act.astype(BF16), wd_s[...]), blk)

    @pl.when(i >= n_used)
    def _():
        o_ref[...] = jnp.zeros_like(o_ref)


def _experts(blk_expert, x_tiles, wg, wu, wd, layer):
    n_rows = x_tiles.shape[0] // SUBLANES
    d, de = wg.shape[-2:]
    blk = EXPERT_BLOCK
    n_blocks = n_rows // blk
    return pl.pallas_call(
        _expert_kernel,
        out_shape=jax.ShapeDtypeStruct((n_rows * SUBLANES, LANES), F32),
        grid_spec=pltpu.PrefetchScalarGridSpec(
            num_scalar_prefetch=1,
            grid=(n_blocks,),
            in_specs=[
                pl.BlockSpec((blk * SUBLANES, LANES), lambda i, be: (jnp.minimum(i, be[n_blocks] - 1), 0)),
                pl.BlockSpec((1, 1, d, de), lambda i, be: (layer, be[i], 0, 0)),
                pl.BlockSpec((1, 1, d, de), lambda i, be: (layer, be[i], 0, 0)),
                pl.BlockSpec((1, 1, de, d), lambda i, be: (layer, be[i], 0, 0)),
            ],
            out_specs=pl.BlockSpec((blk * SUBLANES, LANES), lambda i, be: (i, 0)),
            scratch_shapes=[pltpu.VMEM((d, de), BF16), pltpu.VMEM((d, de), BF16), pltpu.VMEM((de, d), BF16)],
        ),
        compiler_params=_params(("arbitrary",)),
        name="moe_experts",
    )(blk_expert, x_tiles, wg, wu, wd)


def _combine_kernel(d1_ref, d2_ref, y_ref, slab_ref, x_ref, g_ref, nw_ref, o_ref, ra0, rb0, ra1, rb1, sem,
                    *, tm, tiles, n_steps, final_norm):
    step = pl.program_id(0) * tiles + pl.program_id(1)
    bufs = ((ra0, rb0), (ra1, rb1))

    def gather(tile, slot):
        base = tile * tm
        r1_ref, r2_ref = bufs[slot]

        def issue(r, carry):
            t = base + r
            pltpu.make_async_copy(_token_tile(y_ref, d1_ref[t]), _token_tile(r1_ref, r),
                                  sem.at[slot]).start(priority=0)
            pltpu.make_async_copy(_token_tile(y_ref, d2_ref[t]), _token_tile(r2_ref, r),
                                  sem.at[slot]).start(priority=1)
            return carry

        lax.fori_loop(0, tm, issue, 0, unroll=DMA_ISSUE_UNROLL)

    @pl.when(step == 0)
    def _():
        gather(0, 0)

    for slot in range(2):
        @pl.when((step & 1) == slot)
        def _():
            @pl.when(step + 1 < n_steps)
            def _():
                gather(step + 1, 1 - slot)

            r1_ref, r2_ref = bufs[slot]
            pltpu.make_async_copy(y_ref.at[pl.ds(0, tm * SUBLANES)], r1_ref, sem.at[slot]).wait()
            pltpu.make_async_copy(y_ref.at[pl.ds(0, tm * SUBLANES)], r2_ref, sem.at[slot]).wait()
            slab = slab_ref[0]
            y = slab[:, 2:3] * _tiles_load(r1_ref, tm) + slab[:, 3:4] * _tiles_load(r2_ref, tm)
            out = x_ref[0] + g_ref[0] * y
            if final_norm:
                ms = jnp.mean(out * out, axis=-1, keepdims=True)
                out = out * lax.rsqrt(ms + NORM_EPS) * nw_ref[...]
            o_ref[0] = out


def _combine(dest1, dest2, y_buf, slab, x, g_ffn, norm_w, *, final_norm, tm=512):
    bsz, seq, d = x.shape
    tiles = seq // tm
    kern = functools.partial(_combine_kernel, tm=tm, tiles=tiles, n_steps=bsz * tiles, final_norm=final_norm)
    row_buf = pltpu.VMEM((tm * SUBLANES, LANES), F32)
    return pl.pallas_call(
        kern,
        out_shape=jax.ShapeDtypeStruct((bsz, seq, d), F32),
        grid_spec=pltpu.PrefetchScalarGridSpec(
            num_scalar_prefetch=2,
            grid=(bsz, tiles),
            in_specs=[
                pl.BlockSpec(memory_space=pl.ANY),
                pl.BlockSpec((1, tm, ROUTE_LANES), lambda b, i, d1, d2: (b, i, 0)),
                pl.BlockSpec((1, tm, d), lambda b, i, d1, d2: (b, i, 0)),
                pl.BlockSpec((1, 1, d), lambda b, i, d1, d2: (b, 0, 0)),
                pl.BlockSpec((1, d), lambda b, i, d1, d2: (0, 0)),
            ],
            out_specs=pl.BlockSpec((1, tm, d), lambda b, i, d1, d2: (b, i, 0)),
            scratch_shapes=[row_buf, row_buf, row_buf, row_buf, pltpu.SemaphoreType.DMA((2,))],
        ),
        compiler_params=_params(("arbitrary", "arbitrary")),
        name="moe_combine",
    )(dest1, dest2, y_buf, slab, x, g_ffn, norm_w.reshape(1, d))


def _block_diag(w):
    heads, hd, _ = w.shape
    n = heads * hd
    tiled = jnp.tile(w.reshape(n, hd), (1, heads))
    blk_r = lax.broadcasted_iota(jnp.int32, (n, n), 0) // hd
    blk_c = lax.broadcasted_iota(jnp.int32, (n, n), 1) // hd
    return jnp.where(blk_r == blk_c, tiled, 0.0)


def kernel(x, c, ada_w, ada_b, norm_mix_w, w_in, conv_w, conv_b, lru_wa, lru_ba, lru_wx, lru_bx, lru_lambda, norm_lru_w, hgrn_lb, norm_hgrn_w, w_out, norm_ffn_w, router_group_w, router_group_b, router_expert_w, router_expert_b, expert_w_gate, expert_w_up, expert_w_down, final_norm_w):
    bsz, seq, d = x.shape
    assert d == SUBLANES * LANES, "the MoE row movement keeps one (8, 128) tile per token"
    depth = ada_w.shape[0]
    d_lru = conv_w.shape[-1]
    d_hgrn = hgrn_lb.shape[-1]
    m = bsz * seq
    n_rows = m * 2 + N_EXPERTS * EXPERT_BLOCK
    n_blocks = n_rows // EXPERT_BLOCK

    mod = _modulation(c, ada_w, ada_b)
    lb_cum = jnp.cumsum(jax.nn.softmax(hgrn_lb.astype(F32), axis=0), axis=0)
    lb_all = lb_cum - lb_cum[0:1]

    for l in range(depth):
        sh_mix, sc_mix, g_mix, sh_ffn, sc_ffn, g_ffn = [
            mod[l, :, i * d:(i + 1) * d].reshape(bsz, 1, d) for i in range(6)]
        proj = _in_proj(x, norm_mix_w[l], sc_mix, sh_mix, w_in, l)
        wa_bd = jnp.stack([_block_diag(lru_wa[l, 0]), _block_diag(lru_wa[l, 1])]).astype(BF16)
        wx_bd = jnp.stack([_block_diag(lru_wx[l, 0]), _block_diag(lru_wx[l, 1])]).astype(BF16)
        lru = [
            _lru_scan(proj, conv_w[l], conv_b[l], wa_bd, lru_ba[l], wx_bd, lru_bx[l], lru_lambda[l],
                      reverse=rv)
            for rv in (False, True)]
        hg_f, hg_b = _hgrn(proj, lb_all[l], d_lru=d_lru, d_hgrn=d_hgrn)

        lane_pad = ROUTE_LANES - N_GROUPS - N_EXPERTS
        wr = jnp.pad(jnp.concatenate([router_group_w[l], router_expert_w[l]], axis=1), ((0, 0), (0, lane_pad)))
        br = jnp.pad(jnp.concatenate([router_group_b[l], router_expert_b[l]]), (0, lane_pad)).reshape(1, ROUTE_LANES)
        x_mid, h_ffn, slab, counts, route = _post_mixer(
            lru[0], lru[1], proj, hg_f, hg_b, x, norm_lru_w[l], norm_hgrn_w[l], w_out[l].astype(BF16), g_mix,
            norm_ffn_w[l], sc_ffn, sh_ffn, wr.astype(BF16), br)

        cnt = counts[0, :N_EXPERTS].astype(jnp.int32)
        padded = ((cnt + EXPERT_BLOCK - 1) // EXPERT_BLOCK) * EXPERT_BLOCK
        pend = jnp.cumsum(padded)
        pstart = pend - padded
        blk_start = jnp.arange(n_blocks, dtype=jnp.int32) * EXPERT_BLOCK
        blk_expert = jnp.minimum(jnp.sum(pend[None, :] <= blk_start[:, None], axis=1), N_EXPERTS - 1)
        blocks_used = (pend[N_EXPERTS - 1] // EXPERT_BLOCK).reshape(1)
        blk_expert = jnp.concatenate([blk_expert.astype(jnp.int32), blocks_used.astype(jnp.int32)])
        dest = _dest_rows(pstart.astype(jnp.int32), route)
        dest1, dest2 = dest[0], dest[1]

        x_buf = _dispatch(dest1, dest2, h_ffn.reshape(m * SUBLANES, LANES), n_rows)
        y_buf = _experts(blk_expert, x_buf, expert_w_gate, expert_w_up, expert_w_down, l)
        x = _combine(dest1, dest2, y_buf, slab, x_mid, g_ffn, final_norm_w, final_norm=(l == depth - 1))

    return x
```

```python
import functools

import jax
import jax.numpy as jnp
from jax import lax
from jax.experimental import pallas as pl
from jax.experimental.pallas import tpu as pltpu

F32 = jnp.float32
BF16 = jnp.bfloat16

LRU_HEADS = 8
HGRN_HEADS = 8
CONV_WIDTH = 4
LRU_C = 8.0
N_GROUPS = 4
EXPERTS_PER_GROUP = 8
N_EXPERTS = N_GROUPS * EXPERTS_PER_GROUP
NORM_EPS = 1e-6

LANES = 128
SUBLANES = 8
VMEM_LIMIT = 56 * 1024 * 1024

HGRN_CHUNK = 64
HGRN_SUB = 8
LOG2E = 1.4426950408889634
ROUTE_LANES = LANES
EXPERT_BLOCK = 512
DMA_ISSUE_UNROLL = 8
NEG_BIG = -3.0e38


def _params(sem):
    return pltpu.CompilerParams(dimension_semantics=sem, vmem_limit_bytes=VMEM_LIMIT)


def _dot(a, b):
    return jnp.dot(a, b, preferred_element_type=F32)


def _dot_nt(a, b):
    return lax.dot_general(a, b, (((1,), (1,)), ((), ())), preferred_element_type=F32)


def _dot_tn(a, b):
    return lax.dot_general(a, b, (((0,), (0,)), ((), ())), preferred_element_type=F32)


def _dot01_exact(m01, x):
    hi = x.astype(BF16)
    r1 = x - hi.astype(F32)
    mid = r1.astype(BF16)
    lo = (r1 - mid.astype(F32)).astype(BF16)
    return _dot(m01, hi) + _dot(m01, mid) + _dot(m01, lo)


def _sigmoid(x):
    return 1.0 / (1.0 + jnp.exp(-x))


def _mod_kernel(c_ref, w_ref, b_ref, o_ref):
    c = c_ref[...]
    cond = c * _sigmoid(c)
    o_ref[0] = _dot(cond.astype(BF16), w_ref[0].astype(BF16)) + b_ref[0]


def _modulation(c, ada_w, ada_b):
    depth, d, n = ada_w.shape
    bsz = c.shape[0]
    rows = -(-bsz // SUBLANES) * SUBLANES
    c_pad = jnp.pad(c, ((0, rows - bsz), (0, 0)))
    tn = n // 6
    out = pl.pallas_call(
        _mod_kernel,
        out_shape=jax.ShapeDtypeStruct((depth, rows, n), F32),
        grid=(depth, n // tn),
        in_specs=[
            pl.BlockSpec((rows, d), lambda l, j: (0, 0)),
            pl.BlockSpec((1, d, tn), lambda l, j: (l, 0, j)),
            pl.BlockSpec((1, 1, tn), lambda l, j: (l, 0, j)),
        ],
        out_specs=pl.BlockSpec((1, rows, tn), lambda l, j: (l, 0, j)),
        compiler_params=_params(("arbitrary", "arbitrary")),
        name="adaln_mod",
    )(c_pad, ada_w, ada_b.reshape(depth, 1, n))
    return out[:, :bsz]


def _rms_mod(x, nw, sc, sh):
    ms = jnp.mean(x * x, axis=-1, keepdims=True)
    return (x * lax.rsqrt(ms + NORM_EPS) * nw) * (1.0 + sc) + sh


def _inproj_kernel(x_ref, nw_ref, sc_ref, sh_ref, w_ref, o_ref, w_s):
    @pl.when((pl.program_id(0) == 0) & (pl.program_id(1) == 0))
    def _():
        w_s[...] = w_ref[0].astype(BF16)

    h = _rms_mod(x_ref[0], nw_ref[...], sc_ref[0], sh_ref[0])
    o_ref[0] = _dot(h.astype(BF16), w_s[...])


def _in_proj(x, nw, sc, sh, w_in, layer, tm=512):
    bsz, seq, d = x.shape
    n = w_in.shape[-1]
    return pl.pallas_call(
        _inproj_kernel,
        out_shape=jax.ShapeDtypeStruct((bsz, seq, n), F32),
        grid=(bsz, seq // tm),
        in_specs=[
            pl.BlockSpec((1, tm, d), lambda b, i: (b, i, 0)),
            pl.BlockSpec((1, d), lambda b, i: (0, 0)),
            pl.BlockSpec((1, 1, d), lambda b, i: (b, 0, 0)),
            pl.BlockSpec((1, 1, d), lambda b, i: (b, 0, 0)),
            pl.BlockSpec((1, d, n), lambda b, i: (layer, 0, 0), pipeline_mode=pl.Buffered(1)),
        ],
        out_specs=pl.BlockSpec((1, tm, n), lambda b, i: (b, i, 0)),
        scratch_shapes=[pltpu.VMEM((d, n), BF16)],
        compiler_params=_params(("arbitrary", "arbitrary")),
        name="in_proj",
    )(x, nw.reshape(1, d), sc, sh, w_in)


def _lru_kernel(x_ref, xp_ref, xn_ref, cw_ref, cb_ref, wa_ref, ba_ref, wx_ref, bx_ref, lam_ref,
                o_ref, carry_ref, sa_ref, sb_ref, cin_ref, *, reverse, n_chunks, rows):
    c = pl.program_id(1)
    chunk = (n_chunks - 1 - c) if reverse else c

    @pl.when(c == 0)
    def _():
        carry_ref[...] = jnp.zeros_like(carry_ref)

    x = x_ref[0]
    width = x.shape[1]
    row = lax.broadcasted_iota(jnp.int32, (rows, width), 0)
    has_prev = jnp.where(chunk > 0, 1.0, 0.0)
    has_next = jnp.where(chunk < n_chunks - 1, 1.0, 0.0)
    xp = xp_ref[0] * has_prev
    xn = xn_ref[0] * has_next
    xe = jnp.concatenate([xp, x, xn], axis=0)
    h8 = SUBLANES
    cw = cw_ref[...]
    xc = (cw[0:1] * xe[h8 - 2:h8 - 2 + rows] + cw[1:2] * xe[h8 - 1:h8 - 1 + rows] + cw[2:3] * x
          + cw[3:4] * xe[h8 + 1:h8 + 1 + rows] + cb_ref[...])

    xcb = xc.astype(BF16)
    r = _sigmoid(_dot(xcb, wa_ref[0]) + ba_ref[0])
    gate_i = _sigmoid(_dot(xcb, wx_ref[0]) + bx_ref[0])
    lam = lam_ref[0]
    softplus_neg_lam = jnp.maximum(-lam, 0.0) + jnp.log1p(jnp.exp(-jnp.abs(lam)))
    log_a = (-LRU_C) * r * softplus_neg_lam
    a = jnp.exp(log_a)
    t = jnp.tanh(-log_a)
    u = jnp.sqrt(2.0 * t / (1.0 + t)) * (gate_i * xc)

    groups = rows // SUBLANES
    acc_a = a.reshape(groups, SUBLANES, width)
    acc_b = u.reshape(groups, SUBLANES, width)
    sub = lax.broadcasted_iota(jnp.int32, (groups, SUBLANES, width), 1)
    s = 1
    while s < SUBLANES:
        if reverse:
            valid = sub < SUBLANES - s
            sh_a, sh_b = pltpu.roll(acc_a, SUBLANES - s, 1), pltpu.roll(acc_b, SUBLANES - s, 1)
        else:
            valid = sub >= s
            sh_a, sh_b = pltpu.roll(acc_a, s, 1), pltpu.roll(acc_b, s, 1)
        acc_b = jnp.where(valid, acc_a * sh_b + acc_b, acc_b)
        acc_a = jnp.where(valid, acc_a * sh_a, acc_a)
        s *= 2
    acc_a = acc_a.reshape(rows, width)
    acc_b = acc_b.reshape(rows, width)
    groups = rows // SUBLANES
    edge = 0 if reverse else SUBLANES - 1
    n_tiles = width // LANES
    for j in range(n_tiles):
        sa_ref[j] = acc_a[:, j * LANES:(j + 1) * LANES]
        sb_ref[j] = acc_b[:, j * LANES:(j + 1) * LANES]
    ea = jnp.concatenate([sa_ref[j, pl.ds(edge, groups, stride=SUBLANES), :] for j in range(n_tiles)], axis=1)
    eb = jnp.concatenate([sb_ref[j, pl.ds(edge, groups, stride=SUBLANES), :] for j in range(n_tiles)], axis=1)
    grow = lax.broadcasted_iota(jnp.int32, (groups, width), 0)
    s = 1
    while s < groups:
        if reverse:
            valid = grow < groups - s
            sh_a, sh_b = pltpu.roll(ea, groups - s, 0), pltpu.roll(eb, groups - s, 0)
        else:
            valid = grow >= s
            sh_a, sh_b = pltpu.roll(ea, s, 0), pltpu.roll(eb, s, 0)
        eb = jnp.where(valid, ea * sh_b + eb, eb)
        ea = jnp.where(valid, ea * sh_a, ea)
        s *= 2
    carry = carry_ref[...]
    group_out = eb + ea * carry
    if reverse:
        carry_in = jnp.where(grow == groups - 1, carry, pltpu.roll(group_out, groups - 1, 0))
        carry_ref[...] = group_out[0:1]
    else:
        carry_in = jnp.where(grow == 0, carry, pltpu.roll(group_out, 1, 0))
        carry_ref[...] = group_out[groups - 1:groups]
    cin_ref[...] = carry_in
    for g in range(groups):
        rs = slice(g * SUBLANES, (g + 1) * SUBLANES)
        o_ref[0, rs, :] = acc_b[rs] + acc_a[rs] * cin_ref[g:g + 1, :]


def _lru_scan(proj, conv_w, conv_b, wa_bd, ba, wx_bd, bx, lam, *, reverse, rows=512):
    bsz, seq, _ = proj.shape
    d_lru = conv_w.shape[1]
    n_chunks = seq // rows
    halo = rows // SUBLANES
    last_halo = seq // SUBLANES - 1
    dirn = 1 if reverse else 0

    def chunk_of(c):
        return (n_chunks - 1 - c) if reverse else c

    vec = lambda: pl.BlockSpec((1, 1, d_lru), lambda b, c: (dirn, 0, 0))
    mat = lambda: pl.BlockSpec((1, d_lru, d_lru), lambda b, c: (dirn, 0, 0))
    kern = functools.partial(_lru_kernel, reverse=reverse, n_chunks=n_chunks, rows=rows)
    return pl.pallas_call(
        kern,
        out_shape=jax.ShapeDtypeStruct((bsz, seq, d_lru), F32),
        grid=(bsz, n_chunks),
        in_specs=[
            pl.BlockSpec((1, rows, d_lru), lambda b, c: (b, chunk_of(c), 0)),
            pl.BlockSpec((1, SUBLANES, d_lru),
                         lambda b, c: (b, jnp.maximum(chunk_of(c) * halo - 1, 0), 0)),
            pl.BlockSpec((1, SUBLANES, d_lru),
                         lambda b, c: (b, jnp.minimum((chunk_of(c) + 1) * halo, last_halo), 0)),
            pl.BlockSpec((CONV_WIDTH, d_lru), lambda b, c: (0, 0)),
            pl.BlockSpec((1, d_lru), lambda b, c: (0, 0)),
            mat(), vec(), mat(), vec(), vec(),
        ],
        out_specs=pl.BlockSpec((1, rows, d_lru), lambda b, c: (b, chunk_of(c), 0)),
        scratch_shapes=[pltpu.VMEM((1, d_lru), F32), pltpu.VMEM((d_lru // LANES, rows, LANES), F32),
                        pltpu.VMEM((d_lru // LANES, rows, LANES), F32),
                        pltpu.VMEM((rows // SUBLANES, d_lru), F32)],
        compiler_params=_params(("arbitrary", "arbitrary")),
        name="lru_bwd" if reverse else "lru_fwd",
    )(proj, proj, proj, conv_w, conv_b.reshape(1, d_lru), wa_bd, ba.reshape(2, 1, d_lru),
      wx_bd, bx.reshape(2, 1, d_lru), lam.reshape(2, 1, d_lru))


def _hgrn_direction(rev, q_ref, f_ref, v_ref, lb_ref, o_ref, st_ref, diag_s, lvl_s, upd_s, qe_s, btot_s, *, rows):
    ck, sb = HGRN_CHUNK, HGRN_SUB
    n_blk = ck // sb
    sb_shift = sb.bit_length() - 1
    n_sub = rows // ck
    width = q_ref.shape[-1]
    n_pairs = width // LANES
    half = LANES // 2

    def flip(idx, n):
        return (n - 1 - idx) if rev else idx

    n_lvl = n_blk.bit_length() - 1
    tf = flip(lax.broadcasted_iota(jnp.int32, (ck, ck), 0), ck)
    uf = flip(lax.broadcasted_iota(jnp.int32, (ck, ck), 1), ck)
    tb, ub = tf >> sb_shift, uf >> sb_shift
    pb = flip(lax.broadcasted_iota(jnp.int32, (n_blk, ck), 0), n_blk)
    pub = flip(lax.broadcasted_iota(jnp.int32, (n_blk, ck), 1), ck) >> sb_shift
    mats = [jnp.where((tb == ub) & (uf <= tf), 1.0, 0.0),
            jnp.where(pub < pb, 1.0, 0.0)]
    for lvl in range(n_lvl):
        mid = ((pb >> (lvl + 1)) << (lvl + 1)) + (1 << lvl)
        mats.append(jnp.where(pub < mid, 1.0, 0.0))
    mats.append(jnp.ones((SUBLANES, ck), F32))
    m_cum = jnp.concatenate(mats, axis=0).astype(BF16)

    def per_block(rows8):
        return jnp.concatenate(
            [jnp.broadcast_to(rows8[jb:jb + 1], (sb, rows8.shape[1])) for jb in range(n_blk)], axis=0)
    row_blk = flip(lax.broadcasted_iota(jnp.int32, (ck, width), 0), ck) >> sb_shift
    upper = [((row_blk >> lvl) & 1) == 1 for lvl in range(n_lvl)]
    pr = flip(lax.broadcasted_iota(jnp.int32, (ck, LANES), 0), ck) >> sb_shift
    pc = flip(lax.broadcasted_iota(jnp.int32, (ck, LANES), 1) & (ck - 1), ck) >> sb_shift
    group_mask = [(pr >> (lvl + 1)) == (pc >> (lvl + 1)) for lvl in range(n_lvl)]
    lane = lax.broadcasted_iota(jnp.int32, (1, LANES), 1)
    head0 = lane < half

    def split_heads(x):
        xb = x.astype(BF16)
        zero = jnp.zeros_like(xb)
        return jnp.concatenate([jnp.where(head0, xb, zero), jnp.where(head0, zero, xb)], axis=0)

    sr = lax.broadcasted_iota(jnp.int32, (LANES, LANES), 0)
    sc = lax.broadcasted_iota(jnp.int32, (LANES, LANES), 1)
    same_head = (sr < half) == (sc < half)
    er = lax.broadcasted_iota(jnp.int32, (sb * LANES, LANES), 0)
    ec = lax.broadcasted_iota(jnp.int32, (sb * LANES, LANES), 1)
    sel = jnp.where(ec == (((er & (LANES - 1)) >> (half.bit_length() - 1)) * half + (er >> (LANES.bit_length() - 1))),
                    1.0, 0.0).astype(BF16)
    sub_row = flip(lax.broadcasted_iota(jnp.int32, (sb, LANES), 0), sb)
    lbv = lb_ref[...]

    def row_start(j):
        return pl.multiple_of(flip(j, n_sub) * ck, ck)

    def stage1a(j):
        r0 = row_start(j)
        q = q_ref[0, pl.ds(r0, ck), :]
        z = f_ref[0, pl.ds(r0, ck), :]
        v = v_ref[0, pl.ds(r0, ck), :]
        f = lbv + (1.0 - lbv) * _sigmoid(z)
        lf2 = jnp.log(f) * LOG2E
        k = 1.0 - f
        return q, v, k, _dot01_exact(m_cum, lf2)

    def stage1b(q, v, k, cums):
        bl2 = cums[0:ck]
        b2 = bl2 + per_block(cums[ck:ck + n_blk])
        tot_row = ck + (1 + n_lvl) * n_blk
        btot2 = cums[tot_row:tot_row + 1]
        log2_k = jnp.log(k) * LOG2E
        kb = b2 - log2_k
        kbl = bl2 - log2_k
        qe = q * jnp.exp2(b2)
        ke = jnp.exp2(btot2 - kb)
        q_lvl, k_lvl = [], []
        for lvl in range(n_lvl):
            split2 = per_block(cums[ck + (1 + lvl) * n_blk:ck + (2 + lvl) * n_blk])
            q_lvl.append(q * jnp.exp2(jnp.where(upper[lvl], b2 - split2, NEG_BIG)))
            k_lvl.append(jnp.exp2(jnp.where(upper[lvl], NEG_BIG, split2 - kb)))

        qe_s[...] = qe.astype(BF16)
        btot_s[...] = btot2
        for p in range(n_pairs):
            sl = slice(p * LANES, (p + 1) * LANES)
            diag_rows = []
            for jb in range(n_blk):
                rs = slice(jb * sb, (jb + 1) * sb)
                bl_b, kbl_b, q_b = bl2[rs, sl], kbl[rs, sl], q[rs, sl]
                terms = []
                for s in range(sb):
                    arg = jnp.where(sub_row >= flip(s, sb), bl_b - kbl_b[s:s + 1], NEG_BIG)
                    terms.append(q_b * jnp.exp2(arg))
                diag_rows.append(jnp.concatenate(terms, axis=1))
            diag_s[p] = _dot(jnp.concatenate(diag_rows, axis=0).astype(BF16), sel)
            for lvl in range(n_lvl):
                k_p = k_lvl[lvl][:, sl]
                lvl_s[p * n_lvl + lvl] = _dot_nt(q_lvl[lvl][:, sl].astype(BF16), split_heads(k_p))
            upd_s[p] = _dot_tn(v[:, sl].astype(BF16), ke[:, sl].astype(BF16))

    def stage2_issue(j):
        r0 = row_start(j)
        v = v_ref[0, pl.ds(r0, ck), :]
        out = []
        for p in range(n_pairs):
            sl = slice(p * LANES, (p + 1) * LANES)
            parts = []
            for jb in range(n_blk):
                blk = diag_s[p, jb * sb:(jb + 1) * sb, :]
                parts.append(pltpu.roll(blk, jb * sb, 1) if jb else blk)
            scores = jnp.concatenate(parts, axis=0)
            for lvl in range(n_lvl):
                s_lvl = lvl_s[p * n_lvl + lvl]
                scores = scores + (s_lvl if lvl == n_lvl - 1 else jnp.where(group_mask[lvl], s_lvl, 0.0))
            intra = _dot(scores.astype(BF16), split_heads(v[:, sl]))
            st = st_ref[p]
            inter = _dot_nt(qe_s[:, sl], st.astype(BF16))
            new_st = jnp.where(same_head, st * jnp.exp2(btot_s[:, sl]) + upd_s[p], 0.0)
            out.append((inter + intra, new_st))
        return r0, out

    def stage2_finish(r0, out):
        for p in range(n_pairs):
            o_ref[0, pl.ds(r0, ck), p * LANES:(p + 1) * LANES] = out[p][0]
            st_ref[p] = out[p][1]

    return stage1a, stage1b, stage2_issue, stage2_finish


N_HGRN_SCRATCH = 6


def _hgrn_kernel(qf_ref, ff_ref, vf_ref, qb_ref, fb_ref, vb_ref, lb_ref, of_ref, ob_ref, zero_ref, *scratch,
                 rows):
    fwd_scratch, bwd_scratch = scratch[:N_HGRN_SCRATCH], scratch[N_HGRN_SCRATCH:]
    zero_ref[...] = jnp.zeros_like(zero_ref)

    @pl.when(pl.program_id(1) == 0)
    def _():
        fwd_scratch[0][...] = jnp.zeros_like(fwd_scratch[0])
        bwd_scratch[0][...] = jnp.zeros_like(bwd_scratch[0])

    f1a, f1b, f2, f3 = _hgrn_direction(False, qf_ref, ff_ref, vf_ref, lb_ref, of_ref, *fwd_scratch, rows=rows)
    b1a, b1b, b2, b3 = _hgrn_direction(True, qb_ref, fb_ref, vb_ref, lb_ref, ob_ref, *bwd_scratch, rows=rows)
    n_sub = rows // HGRN_CHUNK

    def stage1_both(j):
        fa = f1a(j)
        ba = b1a(j)
        f1b(*fa)
        b1b(*ba)

    stage1_both(0)

    def pipelined(j, carry):
        fo = f2(j)
        bo = b2(j)
        stage1_both(j + 1)
        f3(*fo)
        b3(*bo)
        return carry

    lax.fori_loop(0, n_sub - 1, pipelined, 0)
    fo = f2(n_sub - 1)
    bo = b2(n_sub - 1)
    f3(*fo)
    b3(*bo)


def _hgrn(proj, lb, zero_rows, *, d_lru, d_hgrn, rows=512):
    bsz, seq, _ = proj.shape
    n_chunks = seq // rows
    zero_blk = zero_rows // (bsz * n_chunks)
    assert zero_blk * bsz * n_chunks == zero_rows and zero_blk % SUBLANES == 0
    col0 = (2 * d_lru) // d_hgrn
    n_pairs = d_hgrn // LANES
    n_lvl = (HGRN_CHUNK // HGRN_SUB).bit_length() - 1
    fwd = lambda col: pl.BlockSpec((1, rows, d_hgrn), lambda b, c: (b, c, col))
    bwd = lambda col: pl.BlockSpec((1, rows, d_hgrn), lambda b, c: (b, n_chunks - 1 - c, col))
    direction_scratch = [
        pltpu.VMEM((n_pairs, LANES, LANES), F32),
        pltpu.VMEM((n_pairs, HGRN_CHUNK, LANES), F32),
        pltpu.VMEM((n_pairs * n_lvl, HGRN_CHUNK, LANES), F32),
        pltpu.VMEM((n_pairs, LANES, LANES), F32),
        pltpu.VMEM((HGRN_CHUNK, d_hgrn), BF16),
        pltpu.VMEM((1, d_hgrn), F32),
    ]
    assert len(direction_scratch) == N_HGRN_SCRATCH
    kern = functools.partial(_hgrn_kernel, rows=rows)
    out = jax.ShapeDtypeStruct((bsz, seq, d_hgrn), F32)
    return pl.pallas_call(
        kern,
        out_shape=(out, out, jax.ShapeDtypeStruct((zero_rows, LANES), F32)),
        grid=(bsz, n_chunks),
        in_specs=[fwd(col0), fwd(col0 + 1), fwd(col0 + 3), bwd(col0), bwd(col0 + 2), bwd(col0 + 3),
                  pl.BlockSpec((1, d_hgrn), lambda b, c: (0, 0))],
        out_specs=(pl.BlockSpec((1, rows, d_hgrn), lambda b, c: (b, c, 0)),
                   pl.BlockSpec((1, rows, d_hgrn), lambda b, c: (b, n_chunks - 1 - c, 0)),
                   pl.BlockSpec((zero_blk, LANES), lambda b, c: (b * n_chunks + c, 0))),
        scratch_shapes=direction_scratch + direction_scratch,
        compiler_params=_params(("arbitrary", "arbitrary")),
        name="hgrn2",
    )(proj, proj, proj, proj, proj, proj, lb.reshape(1, d_hgrn))


def _gelu_tanh(y):
    return 0.5 * y * (1.0 + jnp.tanh(0.7978845608028654 * (y + 0.044715 * (y * y * y))))


def _post_kernel(lf_ref, lb_ref, y_ref, of_ref, ob_ref, g_ref, x_ref, nlw_ref, nhw_ref, wo_ref,
                 gm_ref, nfw_ref, scf_ref, shf_ref, wr_ref, br_ref,
                 xo_ref, h_ref, slab_ref, cnt_ref, route_ref, carry_ref, *, tm, d_lru):
    first = (pl.program_id(0) == 0) & (pl.program_id(1) == 0)

    @pl.when(first)
    def _():
        carry_ref[...] = jnp.zeros_like(carry_ref)

    lru = (lf_ref[0] + lb_ref[0]) * _gelu_tanh(y_ref[0])
    ms = jnp.mean(lru * lru, axis=-1, keepdims=True)
    lru = lru * lax.rsqrt(ms + NORM_EPS) * nlw_ref[...]

    hg = of_ref[0] + ob_ref[0]
    width = hg.shape[1]
    hd = width // HGRN_HEADS
    hd_shift = hd.bit_length() - 1
    er = lax.broadcasted_iota(jnp.int32, (width, width), 0) >> hd_shift
    ec = lax.broadcasted_iota(jnp.int32, (width, width), 1) >> hd_shift
    head_sum = jnp.where(er == ec, 1.0, 0.0).astype(BF16)
    sq = hg * hg
    sq_hi = sq.astype(BF16)
    sq_lo = (sq - sq_hi.astype(F32)).astype(BF16)
    ms_h = (_dot(sq_hi, head_sum) + _dot(sq_lo, head_sum)) * (1.0 / hd)
    g = g_ref[0]
    hg = (hg * lax.rsqrt(ms_h + NORM_EPS) * nhw_ref[...]) * (g * _sigmoid(g))

    mixed = _dot(lru.astype(BF16), wo_ref[0:d_lru, :]) + _dot(hg.astype(BF16), wo_ref[d_lru:, :])
    x_new = x_ref[0] + gm_ref[0] * mixed
    xo_ref[0] = x_new

    h = _rms_mod(x_new, nfw_ref[...], scf_ref[0], shf_ref[0])
    _tiles_store(h_ref, h, tm, lead=(0,))

    logits = _dot(h.astype(BF16), wr_ref[...]) + br_ref[...]
    lane = lax.broadcasted_iota(jnp.int32, (tm, ROUTE_LANES), 1)
    lane_f = lane.astype(F32)
    far = float(ROUTE_LANES)
    is_g = lane < N_GROUPS
    gl = jnp.where(is_g, logits, NEG_BIG)
    gmax = jnp.max(gl, axis=-1, keepdims=True)
    g_idx = jnp.min(jnp.where(gl == gmax, lane_f, far), axis=-1, keepdims=True)
    p_group = 1.0 / jnp.sum(jnp.where(is_g, jnp.exp(gl - gmax), 0.0), axis=-1, keepdims=True)
    e_lane = lane - N_GROUPS
    in_group = (e_lane >= 0) & (e_lane < N_EXPERTS) & ((e_lane >> (EXPERTS_PER_GROUP.bit_length() - 1)).astype(F32) == g_idx)
    ev = jnp.where(in_group, logits, NEG_BIG)
    top1 = jnp.max(ev, axis=-1, keepdims=True)
    i1 = jnp.min(jnp.where(in_group & (ev == top1), lane_f, far), axis=-1, keepdims=True)
    rest = in_group & (lane_f != i1)
    ev2 = jnp.where(rest, logits, NEG_BIG)
    top2 = jnp.max(ev2, axis=-1, keepdims=True)
    i2 = jnp.min(jnp.where(rest & (ev2 == top2), lane_f, far), axis=-1, keepdims=True)
    e1 = i1 - float(N_GROUPS)
    e2 = i2 - float(N_GROUPS)
    ex = jnp.exp(top2 - top1)
    w1 = p_group / (1.0 + ex)
    w2 = p_group * ex / (1.0 + ex)

    sel1 = lane_f == e1
    sel2 = lane_f == e2
    onehot = jnp.where(sel1 | sel2, 1.0, 0.0)
    tr = lax.broadcasted_iota(jnp.int32, (tm, tm), 0)
    tc = lax.broadcasted_iota(jnp.int32, (tm, tm), 1)
    before = jnp.where(tc < tr, 1.0, 0.0).astype(BF16)
    cnt = _dot(before, onehot.astype(BF16)) + carry_ref[0:1]
    rank1 = jnp.sum(jnp.where(sel1, cnt, 0.0), axis=-1, keepdims=True)
    rank2 = jnp.sum(jnp.where(sel2, cnt, 0.0), axis=-1, keepdims=True)
    total = carry_ref[0:1] + jnp.sum(onehot, axis=0, keepdims=True)
    carry_ref[...] = jnp.broadcast_to(total, carry_ref.shape)
    cnt_ref[...] = jnp.broadcast_to(total, cnt_ref.shape)

    slab = jnp.where(lane == 0, e1, 0.0)
    slab = jnp.where(lane == 1, e2, slab)
    slab = jnp.where(lane == 2, w1, slab)
    slab = jnp.where(lane == 3, w2, slab)
    slab = jnp.where(lane == 4, rank1, slab)
    slab = jnp.where(lane == 5, rank2, slab)
    slab_ref[0] = slab
    route_ref[...] = slab.T[0:SUBLANES]


def _post_mixer(lru_f, lru_b, proj, hg_f, hg_b, x, nlw, nhw, wo_bf16, g_mix, nfw, sc_ffn, sh_ffn, wr_bf16, br,
                *, tm=512):
    bsz, seq, d = x.shape
    d_lru = lru_f.shape[-1]
    d_hgrn = hg_f.shape[-1]
    y_col = 1
    g_col = (2 * d_lru) // d_hgrn + 4
    row = lambda w: pl.BlockSpec((1, tm, w), lambda b, i: (b, i, 0))
    vec = lambda w: pl.BlockSpec((1, w), lambda b, i: (0, 0))
    per_b = lambda: pl.BlockSpec((1, 1, d), lambda b, i: (b, 0, 0))
    kern = functools.partial(_post_kernel, tm=tm, d_lru=d_lru)
    return pl.pallas_call(
        kern,
        out_shape=(
            jax.ShapeDtypeStruct((bsz, seq, d), F32),
            jax.ShapeDtypeStruct((bsz, seq * SUBLANES, LANES), F32),
            jax.ShapeDtypeStruct((bsz, seq, ROUTE_LANES), F32),
            jax.ShapeDtypeStruct((SUBLANES, ROUTE_LANES), F32),
            jax.ShapeDtypeStruct((SUBLANES, bsz * seq), F32),
        ),
        grid=(bsz, seq // tm),
        in_specs=[
            row(d_lru), row(d_lru),
            pl.BlockSpec((1, tm, d_lru), lambda b, i: (b, i, y_col)),
            row(d_hgrn), row(d_hgrn),
            pl.BlockSpec((1, tm, d_hgrn), lambda b, i: (b, i, g_col)),
            row(d), vec(d_lru), vec(d_hgrn),
            pl.BlockSpec((d, d), lambda b, i: (0, 0)),
            per_b(), vec(d), per_b(), per_b(),
            pl.BlockSpec((d, ROUTE_LANES), lambda b, i: (0, 0)),
            vec(ROUTE_LANES),
        ],
        out_specs=(
            row(d), pl.BlockSpec((1, tm * SUBLANES, LANES), lambda b, i: (b, i, 0)), row(ROUTE_LANES),
            pl.BlockSpec((SUBLANES, ROUTE_LANES), lambda b, i: (0, 0)),
            pl.BlockSpec((SUBLANES, tm), lambda b, i: (0, b * (seq // tm) + i)),
        ),
        scratch_shapes=[pltpu.VMEM((SUBLANES, ROUTE_LANES), F32)],
        compiler_params=_params(("arbitrary", "arbitrary")),
        name="post_mixer_router",
    )(lru_f, lru_b, proj, hg_f, hg_b, proj, x, nlw.reshape(1, d_lru), nhw.reshape(1, d_hgrn), wo_bf16,
      g_mix, nfw.reshape(1, d), sc_ffn, sh_ffn, wr_bf16, br)


def _tiles_load(ref, n, lead=()):
    return jnp.concatenate(
        [ref[(*lead, pl.ds(j, n, stride=SUBLANES), slice(None))] for j in range(SUBLANES)], axis=1)


def _tiles_store(ref, val, n, lead=()):
    for j in range(SUBLANES):
        ref[(*lead, pl.ds(j, n, stride=SUBLANES), slice(None))] = val[:, j * LANES:(j + 1) * LANES]


def _token_tile(ref, t):
    return ref.at[pl.ds(pl.multiple_of(t * SUBLANES, SUBLANES), SUBLANES)]


def _dest_kernel(start_ref, route_ref, o_ref):
    route = route_ref[...].astype(jnp.int32)
    start = jnp.zeros_like(route)
    for e in range(N_EXPERTS):
        start = jnp.where(route == e, start_ref[e], start)
    o_ref[...] = start + pltpu.roll(route, SUBLANES // 2, 0)


def _dest_rows(expert_start, route):
    return pl.pallas_call(
        _dest_kernel,
        out_shape=jax.ShapeDtypeStruct(route.shape, jnp.int32),
        grid_spec=pltpu.PrefetchScalarGridSpec(
            num_scalar_prefetch=1,
            grid=(1,),
            in_specs=[pl.BlockSpec(route.shape, lambda i, s: (0, 0))],
            out_specs=pl.BlockSpec(route.shape, lambda i, s: (0, 0)),
        ),
        compiler_params=pltpu.CompilerParams(dimension_semantics=("arbitrary",)),
        name="moe_dest_rows",
    )(expert_start, route)


def _dispatch_kernel(d1_ref, d2_ref, h_ref, z_ref, o_ref, sem, *, tb):
    del z_ref
    base = pl.program_id(0) * tb

    def issue(r, carry):
        t = base + r
        pltpu.make_async_copy(_token_tile(h_ref, r), _token_tile(o_ref, d1_ref[t]), sem).start(priority=0)
        pltpu.make_async_copy(_token_tile(h_ref, r), _token_tile(o_ref, d2_ref[t]), sem).start(priority=1)
        return carry

    lax.fori_loop(0, tb, issue, 0, unroll=DMA_ISSUE_UNROLL)
    for _ in range(2):
        pltpu.make_async_copy(h_ref, o_ref.at[pl.ds(0, tb * SUBLANES)], sem).wait()


def _dispatch(dest1, dest2, h_tiles, zero_tiles, *, tb=512):
    m = h_tiles.shape[0] // SUBLANES
    n_rows = zero_tiles.shape[0] // SUBLANES
    kern = functools.partial(_dispatch_kernel, tb=tb)
    return pl.pallas_call(
        kern,
        out_shape=jax.ShapeDtypeStruct((n_rows * SUBLANES, LANES), h_tiles.dtype),
        grid_spec=pltpu.PrefetchScalarGridSpec(
            num_scalar_prefetch=2,
            grid=(m // tb,),
            in_specs=[pl.BlockSpec((tb * SUBLANES, LANES), lambda i, d1, d2: (i, 0)),
                      pl.BlockSpec(memory_space=pl.ANY)],
            out_specs=pl.BlockSpec(memory_space=pl.ANY),
            scratch_shapes=[pltpu.SemaphoreType.DMA(())],
        ),
        input_output_aliases={3: 0},
        compiler_params=pltpu.CompilerParams(dimension_semantics=("arbitrary",), has_side_effects=True),
        name="moe_dispatch",
    )(dest1, dest2, h_tiles, zero_tiles)


def _expert_kernel(be_ref, x_ref, wg_ref, wu_ref, wd_ref, o_ref, wg_s, wu_s, wd_s):
    i = pl.program_id(0)
    prev_expert = be_ref[jnp.maximum(i - 1, 0)]

    @pl.when((i == 0) | (be_ref[i] != prev_expert))
    def _():
        wg_s[...] = wg_ref[0, 0].astype(BF16)
        wu_s[...] = wu_ref[0, 0].astype(BF16)
        wd_s[...] = wd_ref[0, 0].astype(BF16)

    blk = x_ref.shape[0] // SUBLANES
    n_used = be_ref[pl.num_programs(0)]

    @pl.when(i < n_used)
    def _():
        x = _tiles_load(x_ref, blk).astype(BF16)
        gate = _dot(x, wg_s[...])
        up = _dot(x, wu_s[...])
        act = (gate * _sigmoid(gate)) * up
        _tiles_store(o_ref, _dot(act.astype(BF16), wd_s[...]), blk)

    @pl.when(i >= n_used)
    def _():
        o_ref[...] = jnp.zeros_like(o_ref)


def _experts(blk_expert, x_tiles, wg, wu, wd, layer):
    n_rows = x_tiles.shape[0] // SUBLANES
    d, de = wg.shape[-2:]
    blk = EXPERT_BLOCK
    n_blocks = n_rows // blk
    return pl.pallas_call(
        _expert_kernel,
        out_shape=jax.ShapeDtypeStruct((n_rows * SUBLANES, LANES), F32),
        grid_spec=pltpu.PrefetchScalarGridSpec(
            num_scalar_prefetch=1,
            grid=(n_blocks,),
            in_specs=[
                pl.BlockSpec((blk * SUBLANES, LANES), lambda i, be: (jnp.minimum(i, be[n_blocks] - 1), 0)),
                pl.BlockSpec((1, 1, d, de), lambda i, be: (layer, be[i], 0, 0)),
                pl.BlockSpec((1, 1, d, de), lambda i, be: (layer, be[i], 0, 0)),
                pl.BlockSpec((1, 1, de, d), lambda i, be: (layer, be[i], 0, 0)),
            ],
            out_specs=pl.BlockSpec((blk * SUBLANES, LANES), lambda i, be: (i, 0)),
            scratch_shapes=[pltpu.VMEM((d, de), BF16), pltpu.VMEM((d, de), BF16), pltpu.VMEM((de, d), BF16)],
        ),
        compiler_params=_params(("arbitrary",)),
        name="moe_experts",
    )(blk_expert, x_tiles, wg, wu, wd)


def _combine_kernel(d1_ref, d2_ref, y_ref, slab_ref, x_ref, g_ref, nw_ref, o_ref, ra0, rb0, ra1, rb1, sem,
                    *, tm, tiles, n_steps, final_norm):
    step = pl.program_id(0) * tiles + pl.program_id(1)
    bufs = ((ra0, rb0), (ra1, rb1))

    def gather(tile, slot):
        base = tile * tm
        r1_ref, r2_ref = bufs[slot]

        def issue(r, carry):
            t = base + r
            pltpu.make_async_copy(_token_tile(y_ref, d1_ref[t]), _token_tile(r1_ref, r),
                                  sem.at[slot]).start(priority=0)
            pltpu.make_async_copy(_token_tile(y_ref, d2_ref[t]), _token_tile(r2_ref, r),
                                  sem.at[slot]).start(priority=1)
            return carry

        lax.fori_loop(0, tm, issue, 0, unroll=DMA_ISSUE_UNROLL)

    @pl.when(step == 0)
    def _():
        gather(0, 0)

    for slot in range(2):
        @pl.when((step & 1) == slot)
        def _():
            @pl.when(step + 1 < n_steps)
            def _():
                gather(step + 1, 1 - slot)

            r1_ref, r2_ref = bufs[slot]
            pltpu.make_async_copy(y_ref.at[pl.ds(0, tm * SUBLANES)], r1_ref, sem.at[slot]).wait()
            pltpu.make_async_copy(y_ref.at[pl.ds(0, tm * SUBLANES)], r2_ref, sem.at[slot]).wait()
            slab = slab_ref[0]
            y = slab[:, 2:3] * _tiles_load(r1_ref, tm) + slab[:, 3:4] * _tiles_load(r2_ref, tm)
            out = x_ref[0] + g_ref[0] * y
            if final_norm:
                ms = jnp.mean(out * out, axis=-1, keepdims=True)
                out = out * lax.rsqrt(ms + NORM_EPS) * nw_ref[...]
            o_ref[0] = out


def _combine(dest1, dest2, y_buf, slab, x, g_ffn, norm_w, *, final_norm, tm=512):
    bsz, seq, d = x.shape
    tiles = seq // tm
    kern = functools.partial(_combine_kernel, tm=tm, tiles=tiles, n_steps=bsz * tiles, final_norm=final_norm)
    row_buf = pltpu.VMEM((tm * SUBLANES, LANES), F32)
    return pl.pallas_call(
        kern,
        out_shape=jax.ShapeDtypeStruct((bsz, seq, d), F32),
        grid_spec=pltpu.PrefetchScalarGridSpec(
            num_scalar_prefetch=2,
            grid=(bsz, tiles),
            in_specs=[
                pl.BlockSpec(memory_space=pl.ANY),
                pl.BlockSpec((1, tm, ROUTE_LANES), lambda b, i, d1, d2: (b, i, 0)),
                pl.BlockSpec((1, tm, d), lambda b, i, d1, d2: (b, i, 0)),
                pl.BlockSpec((1, 1, d), lambda b, i, d1, d2: (b, 0, 0)),
                pl.BlockSpec((1, d), lambda b, i, d1, d2: (0, 0)),
            ],
            out_specs=pl.BlockSpec((1, tm, d), lambda b, i, d1, d2: (b, i, 0)),
            scratch_shapes=[row_buf, row_buf, row_buf, row_buf, pltpu.SemaphoreType.DMA((2,))],
        ),
        compiler_params=_params(("arbitrary", "arbitrary")),
        name="moe_combine",
    )(dest1, dest2, y_buf, slab, x, g_ffn, norm_w.reshape(1, d))


def _block_diag(w):
    heads, hd, _ = w.shape
    n = heads * hd
    tiled = jnp.tile(w.reshape(n, hd), (1, heads))
    blk_r = lax.broadcasted_iota(jnp.int32, (n, n), 0) // hd
    blk_c = lax.broadcasted_iota(jnp.int32, (n, n), 1) // hd
    return jnp.where(blk_r == blk_c, tiled, 0.0)


def kernel(x, c, ada_w, ada_b, norm_mix_w, w_in, conv_w, conv_b, lru_wa, lru_ba, lru_wx, lru_bx, lru_lambda, norm_lru_w, hgrn_lb, norm_hgrn_w, w_out, norm_ffn_w, router_group_w, router_group_b, router_expert_w, router_expert_b, expert_w_gate, expert_w_up, expert_w_down, final_norm_w):
    bsz, seq, d = x.shape
    assert d == SUBLANES * LANES, "the MoE row movement keeps one (8, 128) tile per token"
    depth = ada_w.shape[0]
    d_lru = conv_w.shape[-1]
    d_hgrn = hgrn_lb.shape[-1]
    m = bsz * seq
    n_rows = m * 2 + N_EXPERTS * EXPERT_BLOCK
    n_blocks = n_rows // EXPERT_BLOCK

    mod = _modulation(c, ada_w, ada_b)
    lb_cum = jnp.cumsum(jax.nn.softmax(hgrn_lb.astype(F32), axis=0), axis=0)
    lb_all = lb_cum - lb_cum[0:1]

    for l in range(depth):
        sh_mix, sc_mix, g_mix, sh_ffn, sc_ffn, g_ffn = [
            mod[l, :, i * d:(i + 1) * d].reshape(bsz, 1, d) for i in range(6)]
        proj = _in_proj(x, norm_mix_w[l], sc_mix, sh_mix, w_in, l)
        wa_bd = jnp.stack([_block_diag(lru_wa[l, 0]), _block_diag(lru_wa[l, 1])]).astype(BF16)
        wx_bd = jnp.stack([_block_diag(lru_wx[l, 0]), _block_diag(lru_wx[l, 1])]).astype(BF16)
        lru = [
            _lru_scan(proj, conv_w[l], conv_b[l], wa_bd, lru_ba[l], wx_bd, lru_bx[l], lru_lambda[l],
                      reverse=rv)
            for rv in (False, True)]
        hg_f, hg_b, zero_tiles = _hgrn(proj, lb_all[l], n_rows * SUBLANES, d_lru=d_lru, d_hgrn=d_hgrn)

        lane_pad = ROUTE_LANES - N_GROUPS - N_EXPERTS
        wr = jnp.pad(jnp.concatenate([router_group_w[l], router_expert_w[l]], axis=1), ((0, 0), (0, lane_pad)))
        br = jnp.pad(jnp.concatenate([router_group_b[l], router_expert_b[l]]), (0, lane_pad)).reshape(1, ROUTE_LANES)
        x_mid, h_ffn, slab, counts, route = _post_mixer(
            lru[0], lru[1], proj, hg_f, hg_b, x, norm_lru_w[l], norm_hgrn_w[l], w_out[l].astype(BF16), g_mix,
            norm_ffn_w[l], sc_ffn, sh_ffn, wr.astype(BF16), br)

        cnt = counts[0, :N_EXPERTS].astype(jnp.int32)
        padded = ((cnt + EXPERT_BLOCK - 1) // EXPERT_BLOCK) * EXPERT_BLOCK
        pend = jnp.cumsum(padded)
        pstart = pend - padded
        blk_start = jnp.arange(n_blocks, dtype=jnp.int32) * EXPERT_BLOCK
        blk_expert = jnp.minimum(jnp.sum(pend[None, :] <= blk_start[:, None], axis=1), N_EXPERTS - 1)
        blocks_used = (pend[N_EXPERTS - 1] // EXPERT_BLOCK).reshape(1)
        blk_expert = jnp.concatenate([blk_expert.astype(jnp.int32), blocks_used.astype(jnp.int32)])
        dest = _dest_rows(pstart.astype(jnp.int32), route)
        dest1, dest2 = dest[0], dest[1]

        x_buf = _dispatch(dest1, dest2, h_ffn.reshape(m * SUBLANES, LANES), zero_tiles)
        y_buf = _experts(blk_expert, x_buf, expert_w_gate, expert_w_up, expert_w_down, l)
        x = _combine(dest1, dest2, y_buf, slab, x_mid, g_ffn, final_norm_w, final_norm=(l == depth - 1))

    return x
```

```python
import functools

import jax
import jax.numpy as jnp
from jax import lax
from jax.experimental import pallas as pl
from jax.experimental.pallas import tpu as pltpu

F32 = jnp.float32
BF16 = jnp.bfloat16

LRU_HEADS = 8
HGRN_HEADS = 8
CONV_WIDTH = 4
LRU_C = 8.0
N_GROUPS = 4
EXPERTS_PER_GROUP = 8
N_EXPERTS = N_GROUPS * EXPERTS_PER_GROUP
NORM_EPS = 1e-6

LANES = 128
SUBLANES = 8
VMEM_LIMIT = 56 * 1024 * 1024

HGRN_CHUNK = 64
HGRN_SUB = 8
LOG2E = 1.4426950408889634
ROUTE_LANES = LANES
EXPERT_BLOCK = 512
DMA_ISSUE_UNROLL = 8
NEG_BIG = -3.0e38


def _params(sem):
    return pltpu.CompilerParams(dimension_semantics=sem, vmem_limit_bytes=VMEM_LIMIT)


def _dot(a, b):
    return jnp.dot(a, b, preferred_element_type=F32)


def _dot_nt(a, b):
    return lax.dot_general(a, b, (((1,), (1,)), ((), ())), preferred_element_type=F32)


def _dot_tn(a, b):
    return lax.dot_general(a, b, (((0,), (0,)), ((), ())), preferred_element_type=F32)


def _dot01_exact(m01, x):
    hi = x.astype(BF16)
    r1 = x - hi.astype(F32)
    mid = r1.astype(BF16)
    lo = (r1 - mid.astype(F32)).astype(BF16)
    return _dot(m01, hi) + _dot(m01, mid) + _dot(m01, lo)


def _sigmoid(x):
    return 1.0 / (1.0 + jnp.exp(-x))


def _mod_kernel(c_ref, w_ref, b_ref, o_ref):
    c = c_ref[...]
    cond = c * _sigmoid(c)
    o_ref[0] = _dot(cond.astype(BF16), w_ref[0].astype(BF16)) + b_ref[0]


def _modulation(c, ada_w, ada_b):
    depth, d, n = ada_w.shape
    bsz = c.shape[0]
    rows = -(-bsz // SUBLANES) * SUBLANES
    c_pad = jnp.pad(c, ((0, rows - bsz), (0, 0)))
    tn = n // 6
    out = pl.pallas_call(
        _mod_kernel,
        out_shape=jax.ShapeDtypeStruct((depth, rows, n), F32),
        grid=(depth, n // tn),
        in_specs=[
            pl.BlockSpec((rows, d), lambda l, j: (0, 0)),
            pl.BlockSpec((1, d, tn), lambda l, j: (l, 0, j)),
            pl.BlockSpec((1, 1, tn), lambda l, j: (l, 0, j)),
        ],
        out_specs=pl.BlockSpec((1, rows, tn), lambda l, j: (l, 0, j)),
        compiler_params=_params(("arbitrary", "arbitrary")),
        name="adaln_mod",
    )(c_pad, ada_w, ada_b.reshape(depth, 1, n))
    return out[:, :bsz]


def _rms_mod(x, nw, sc, sh):
    ms = jnp.mean(x * x, axis=-1, keepdims=True)
    return (x * lax.rsqrt(ms + NORM_EPS) * nw) * (1.0 + sc) + sh


def _inproj_kernel(x_ref, nw_ref, sc_ref, sh_ref, w_ref, o_ref, w_s):
    @pl.when((pl.program_id(0) == 0) & (pl.program_id(1) == 0))
    def _():
        w_s[...] = w_ref[0].astype(BF16)

    h = _rms_mod(x_ref[0], nw_ref[...], sc_ref[0], sh_ref[0])
    o_ref[0] = _dot(h.astype(BF16), w_s[...])


def _in_proj(x, nw, sc, sh, w_in, layer, tm=512):
    bsz, seq, d = x.shape
    n = w_in.shape[-1]
    return pl.pallas_call(
        _inproj_kernel,
        out_shape=jax.ShapeDtypeStruct((bsz, seq, n), F32),
        grid=(bsz, seq // tm),
        in_specs=[
            pl.BlockSpec((1, tm, d), lambda b, i: (b, i, 0)),
            pl.BlockSpec((1, d), lambda b, i: (0, 0)),
            pl.BlockSpec((1, 1, d), lambda b, i: (b, 0, 0)),
            pl.BlockSpec((1, 1, d), lambda b, i: (b, 0, 0)),
            pl.BlockSpec((1, d, n), lambda b, i: (layer, 0, 0), pipeline_mode=pl.Buffered(1)),
        ],
        out_specs=pl.BlockSpec((1, tm, n), lambda b, i: (b, i, 0)),
        scratch_shapes=[pltpu.VMEM((d, n), BF16)],
        compiler_params=_params(("arbitrary", "arbitrary")),
        name="in_proj",
    )(x, nw.reshape(1, d), sc, sh, w_in)


def _lru_kernel(x_ref, xp_ref, xn_ref, cw_ref, cb_ref, wa_ref, ba_ref, wx_ref, bx_ref, lam_ref,
                o_ref, carry_ref, sa_ref, sb_ref, cin_ref, *, reverse, n_chunks, rows):
    c = pl.program_id(1)
    chunk = (n_chunks - 1 - c) if reverse else c

    @pl.when(c == 0)
    def _():
        carry_ref[...] = jnp.zeros_like(carry_ref)

    x = x_ref[0]
    width = x.shape[1]
    row = lax.broadcasted_iota(jnp.int32, (rows, width), 0)
    has_prev = jnp.where(chunk > 0, 1.0, 0.0)
    has_next = jnp.where(chunk < n_chunks - 1, 1.0, 0.0)
    xp = xp_ref[0] * has_prev
    xn = xn_ref[0] * has_next
    xe = jnp.concatenate([xp, x, xn], axis=0)
    h8 = SUBLANES
    cw = cw_ref[...]
    xc = (cw[0:1] * xe[h8 - 2:h8 - 2 + rows] + cw[1:2] * xe[h8 - 1:h8 - 1 + rows] + cw[2:3] * x
          + cw[3:4] * xe[h8 + 1:h8 + 1 + rows] + cb_ref[...])

    xcb = xc.astype(BF16)
    r = _sigmoid(_dot(xcb, wa_ref[0]) + ba_ref[0])
    gate_i = _sigmoid(_dot(xcb, wx_ref[0]) + bx_ref[0])
    lam = lam_ref[0]
    softplus_neg_lam = jnp.maximum(-lam, 0.0) + jnp.log1p(jnp.exp(-jnp.abs(lam)))
    log_a = (-LRU_C) * r * softplus_neg_lam
    a = jnp.exp(log_a)
    t = jnp.tanh(-log_a)
    u = jnp.sqrt(2.0 * t / (1.0 + t)) * (gate_i * xc)

    groups = rows // SUBLANES
    acc_a = a.reshape(groups, SUBLANES, width)
    acc_b = u.reshape(groups, SUBLANES, width)
    sub = lax.broadcasted_iota(jnp.int32, (groups, SUBLANES, width), 1)
    s = 1
    while s < SUBLANES:
        if reverse:
            valid = sub < SUBLANES - s
            sh_a, sh_b = pltpu.roll(acc_a, SUBLANES - s, 1), pltpu.roll(acc_b, SUBLANES - s, 1)
        else:
            valid = sub >= s
            sh_a, sh_b = pltpu.roll(acc_a, s, 1), pltpu.roll(acc_b, s, 1)
        acc_b = jnp.where(valid, acc_a * sh_b + acc_b, acc_b)
        acc_a = jnp.where(valid, acc_a * sh_a, acc_a)
        s *= 2
    acc_a = acc_a.reshape(rows, width)
    acc_b = acc_b.reshape(rows, width)
    groups = rows // SUBLANES
    edge = 0 if reverse else SUBLANES - 1
    n_tiles = width // LANES
    for j in range(n_tiles):
        sa_ref[j] = acc_a[:, j * LANES:(j + 1) * LANES]
        sb_ref[j] = acc_b[:, j * LANES:(j + 1) * LANES]
    ea = jnp.concatenate([sa_ref[j, pl.ds(edge, groups, stride=SUBLANES), :] for j in range(n_tiles)], axis=1)
    eb = jnp.concatenate([sb_ref[j, pl.ds(edge, groups, stride=SUBLANES), :] for j in range(n_tiles)], axis=1)
    grow = lax.broadcasted_iota(jnp.int32, (groups, width), 0)
    s = 1
    while s < groups:
        if reverse:
            valid = grow < groups - s
            sh_a, sh_b = pltpu.roll(ea, groups - s, 0), pltpu.roll(eb, groups - s, 0)
        else:
            valid = grow >= s
            sh_a, sh_b = pltpu.roll(ea, s, 0), pltpu.roll(eb, s, 0)
        eb = jnp.where(valid, ea * sh_b + eb, eb)
        ea = jnp.where(valid, ea * sh_a, ea)
        s *= 2
    carry = carry_ref[...]
    group_out = eb + ea * carry
    if reverse:
        carry_in = jnp.where(grow == groups - 1, carry, pltpu.roll(group_out, groups - 1, 0))
        carry_ref[...] = group_out[0:1]
    else:
        carry_in = jnp.where(grow == 0, carry, pltpu.roll(group_out, 1, 0))
        carry_ref[...] = group_out[groups - 1:groups]
    cin_ref[...] = carry_in
    for g in range(groups):
        rs = slice(g * SUBLANES, (g + 1) * SUBLANES)
        o_ref[0, rs, :] = acc_b[rs] + acc_a[rs] * cin_ref[g:g + 1, :]


def _lru_scan(proj, conv_w, conv_b, wa_bd, ba, wx_bd, bx, lam, *, reverse, rows=512):
    bsz, seq, _ = proj.shape
    d_lru = conv_w.shape[1]
    n_chunks = seq // rows
    halo = rows // SUBLANES
    last_halo = seq // SUBLANES - 1
    dirn = 1 if reverse else 0

    def chunk_of(c):
        return (n_chunks - 1 - c) if reverse else c

    vec = lambda: pl.BlockSpec((1, 1, d_lru), lambda b, c: (dirn, 0, 0))
    mat = lambda: pl.BlockSpec((1, d_lru, d_lru), lambda b, c: (dirn, 0, 0))
    kern = functools.partial(_lru_kernel, reverse=reverse, n_chunks=n_chunks, rows=rows)
    return pl.pallas_call(
        kern,
        out_shape=jax.ShapeDtypeStruct((bsz, seq, d_lru), F32),
        grid=(bsz, n_chunks),
        in_specs=[
            pl.BlockSpec((1, rows, d_lru), lambda b, c: (b, chunk_of(c), 0)),
            pl.BlockSpec((1, SUBLANES, d_lru),
                         lambda b, c: (b, jnp.maximum(chunk_of(c) * halo - 1, 0), 0)),
            pl.BlockSpec((1, SUBLANES, d_lru),
                         lambda b, c: (b, jnp.minimum((chunk_of(c) + 1) * halo, last_halo), 0)),
            pl.BlockSpec((CONV_WIDTH, d_lru), lambda b, c: (0, 0)),
            pl.BlockSpec((1, d_lru), lambda b, c: (0, 0)),
            mat(), vec(), mat(), vec(), vec(),
        ],
        out_specs=pl.BlockSpec((1, rows, d_lru), lambda b, c: (b, chunk_of(c), 0)),
        scratch_shapes=[pltpu.VMEM((1, d_lru), F32), pltpu.VMEM((d_lru // LANES, rows, LANES), F32),
                        pltpu.VMEM((d_lru // LANES, rows, LANES), F32),
                        pltpu.VMEM((rows // SUBLANES, d_lru), F32)],
        compiler_params=_params(("arbitrary", "arbitrary")),
        name="lru_bwd" if reverse else "lru_fwd",
    )(proj, proj, proj, conv_w, conv_b.reshape(1, d_lru), wa_bd, ba.reshape(2, 1, d_lru),
      wx_bd, bx.reshape(2, 1, d_lru), lam.reshape(2, 1, d_lru))


def _hgrn_direction(rev, q_ref, f_ref, v_ref, lb_ref, o_ref, st_ref, diag_s, lvl_s, upd_s, qe_s, btot_s, *, rows):
    ck, sb = HGRN_CHUNK, HGRN_SUB
    n_blk = ck // sb
    sb_shift = sb.bit_length() - 1
    n_sub = rows // ck
    width = q_ref.shape[-1]
    n_pairs = width // LANES
    half = LANES // 2

    def flip(idx, n):
        return (n - 1 - idx) if rev else idx

    n_lvl = n_blk.bit_length() - 1
    tf = flip(lax.broadcasted_iota(jnp.int32, (ck, ck), 0), ck)
    uf = flip(lax.broadcasted_iota(jnp.int32, (ck, ck), 1), ck)
    tb, ub = tf >> sb_shift, uf >> sb_shift
    pb = flip(lax.broadcasted_iota(jnp.int32, (n_blk, ck), 0), n_blk)
    pub = flip(lax.broadcasted_iota(jnp.int32, (n_blk, ck), 1), ck) >> sb_shift
    mats = [jnp.where((tb == ub) & (uf <= tf), 1.0, 0.0),
            jnp.where(pub < pb, 1.0, 0.0)]
    for lvl in range(n_lvl):
        mid = ((pb >> (lvl + 1)) << (lvl + 1)) + (1 << lvl)
        mats.append(jnp.where(pub < mid, 1.0, 0.0))
    mats.append(jnp.ones((SUBLANES, ck), F32))
    m_cum = jnp.concatenate(mats, axis=0).astype(BF16)

    def per_block(rows8):
        return jnp.concatenate(
            [jnp.broadcast_to(rows8[jb:jb + 1], (sb, rows8.shape[1])) for jb in range(n_blk)], axis=0)
    row_blk = flip(lax.broadcasted_iota(jnp.int32, (ck, width), 0), ck) >> sb_shift
    upper = [((row_blk >> lvl) & 1) == 1 for lvl in range(n_lvl)]
    pr = flip(lax.broadcasted_iota(jnp.int32, (ck, LANES), 0), ck) >> sb_shift
    pc = flip(lax.broadcasted_iota(jnp.int32, (ck, LANES), 1) & (ck - 1), ck) >> sb_shift
    group_mask = [(pr >> (lvl + 1)) == (pc >> (lvl + 1)) for lvl in range(n_lvl)]
    lane = lax.broadcasted_iota(jnp.int32, (1, LANES), 1)
    head0 = lane < half

    def split_heads(x):
        xb = x.astype(BF16)
        zero = jnp.zeros_like(xb)
        return jnp.concatenate([jnp.where(head0, xb, zero), jnp.where(head0, zero, xb)], axis=0)

    sr = lax.broadcasted_iota(jnp.int32, (LANES, LANES), 0)
    sc = lax.broadcasted_iota(jnp.int32, (LANES, LANES), 1)
    same_head = (sr < half) == (sc < half)
    er = lax.broadcasted_iota(jnp.int32, (sb * LANES, LANES), 0)
    ec = lax.broadcasted_iota(jnp.int32, (sb * LANES, LANES), 1)
    sel = jnp.where(ec == (((er & (LANES - 1)) >> (half.bit_length() - 1)) * half + (er >> (LANES.bit_length() - 1))),
                    1.0, 0.0).astype(BF16)
    sub_row = flip(lax.broadcasted_iota(jnp.int32, (sb, LANES), 0), sb)
    lbv = lb_ref[...]

    def row_start(j):
        return pl.multiple_of(flip(j, n_sub) * ck, ck)

    def stage1a(j):
        r0 = row_start(j)
        q = q_ref[0, pl.ds(r0, ck), :]
        z = f_ref[0, pl.ds(r0, ck), :]
        v = v_ref[0, pl.ds(r0, ck), :]
        f = lbv + (1.0 - lbv) * _sigmoid(z)
        lf2 = jnp.log(f) * LOG2E
        k = 1.0 - f
        return q, v, k, _dot01_exact(m_cum, lf2)

    def stage1b(q, v, k, cums):
        bl2 = cums[0:ck]
        b2 = bl2 + per_block(cums[ck:ck + n_blk])
        tot_row = ck + (1 + n_lvl) * n_blk
        btot2 = cums[tot_row:tot_row + 1]
        log2_k = jnp.log(k) * LOG2E
        kb = b2 - log2_k
        kbl = bl2 - log2_k
        qe = q * jnp.exp2(b2)
        ke = jnp.exp2(btot2 - kb)
        q_lvl, k_lvl = [], []
        for lvl in range(n_lvl):
            split2 = per_block(cums[ck + (1 + lvl) * n_blk:ck + (2 + lvl) * n_blk])
            q_lvl.append(q * jnp.exp2(jnp.where(upper[lvl], b2 - split2, NEG_BIG)))
            k_lvl.append(jnp.exp2(jnp.where(upper[lvl], NEG_BIG, split2 - kb)))

        qe_s[...] = qe.astype(BF16)
        btot_s[...] = btot2
        for p in range(n_pairs):
            sl = slice(p * LANES, (p + 1) * LANES)
            diag_rows = []
            for jb in range(n_blk):
                rs = slice(jb * sb, (jb + 1) * sb)
                bl_b, kbl_b, q_b = bl2[rs, sl], kbl[rs, sl], q[rs, sl]
                terms = []
                for s in range(sb):
                    arg = jnp.where(sub_row >= flip(s, sb), bl_b - kbl_b[s:s + 1], NEG_BIG)
                    terms.append(q_b * jnp.exp2(arg))
                diag_rows.append(jnp.concatenate(terms, axis=1))
            diag_s[p] = _dot(jnp.concatenate(diag_rows, axis=0).astype(BF16), sel)
            for lvl in range(n_lvl):
                k_p = k_lvl[lvl][:, sl]
                lvl_s[p * n_lvl + lvl] = _dot_nt(q_lvl[lvl][:, sl].astype(BF16), split_heads(k_p))
            upd_s[p] = _dot_tn(v[:, sl].astype(BF16), ke[:, sl].astype(BF16))

    def stage2_issue(j):
        r0 = row_start(j)
        v = v_ref[0, pl.ds(r0, ck), :]
        out = []
        for p in range(n_pairs):
            sl = slice(p * LANES, (p + 1) * LANES)
            parts = []
            for jb in range(n_blk):
                blk = diag_s[p, jb * sb:(jb + 1) * sb, :]
                parts.append(pltpu.roll(blk, jb * sb, 1) if jb else blk)
            scores = jnp.concatenate(parts, axis=0)
            for lvl in range(n_lvl):
                s_lvl = lvl_s[p * n_lvl + lvl]
                scores = scores + (s_lvl if lvl == n_lvl - 1 else jnp.where(group_mask[lvl], s_lvl, 0.0))
            intra = _dot(scores.astype(BF16), split_heads(v[:, sl]))
            st = st_ref[p]
            inter = _dot_nt(qe_s[:, sl], st.astype(BF16))
            new_st = jnp.where(same_head, st * jnp.exp2(btot_s[:, sl]) + upd_s[p], 0.0)
            out.append((inter + intra, new_st))
        return r0, out

    def stage2_finish(r0, out):
        for p in range(n_pairs):
            o_ref[0, pl.ds(r0, ck), p * LANES:(p + 1) * LANES] = out[p][0]
            st_ref[p] = out[p][1]

    return stage1a, stage1b, stage2_issue, stage2_finish


N_HGRN_SCRATCH = 6


def _hgrn_kernel(qf_ref, ff_ref, vf_ref, qb_ref, fb_ref, vb_ref, lb_ref, of_ref, ob_ref, zero_ref, *scratch,
                 rows):
    fwd_scratch, bwd_scratch = scratch[:N_HGRN_SCRATCH], scratch[N_HGRN_SCRATCH:]
    zero_ref[...] = jnp.zeros_like(zero_ref)

    @pl.when(pl.program_id(1) == 0)
    def _():
        fwd_scratch[0][...] = jnp.zeros_like(fwd_scratch[0])
        bwd_scratch[0][...] = jnp.zeros_like(bwd_scratch[0])

    f1a, f1b, f2, f3 = _hgrn_direction(False, qf_ref, ff_ref, vf_ref, lb_ref, of_ref, *fwd_scratch, rows=rows)
    b1a, b1b, b2, b3 = _hgrn_direction(True, qb_ref, fb_ref, vb_ref, lb_ref, ob_ref, *bwd_scratch, rows=rows)
    n_sub = rows // HGRN_CHUNK

    def stage1_both(j):
        fa = f1a(j)
        ba = b1a(j)
        f1b(*fa)
        b1b(*ba)

    stage1_both(0)

    def pipelined(j, carry):
        fo = f2(j)
        bo = b2(j)
        stage1_both(j + 1)
        f3(*fo)
        b3(*bo)
        return carry

    lax.fori_loop(0, n_sub - 1, pipelined, 0)
    fo = f2(n_sub - 1)
    bo = b2(n_sub - 1)
    f3(*fo)
    b3(*bo)


def _hgrn(proj, lb, zero_rows, *, d_lru, d_hgrn, rows=512):
    bsz, seq, _ = proj.shape
    n_chunks = seq // rows
    zero_blk = zero_rows // (bsz * n_chunks)
    assert zero_blk * bsz * n_chunks == zero_rows and zero_blk % SUBLANES == 0
    col0 = (2 * d_lru) // d_hgrn
    n_pairs = d_hgrn // LANES
    n_lvl = (HGRN_CHUNK // HGRN_SUB).bit_length() - 1
    fwd = lambda col: pl.BlockSpec((1, rows, d_hgrn), lambda b, c: (b, c, col))
    bwd = lambda col: pl.BlockSpec((1, rows, d_hgrn), lambda b, c: (b, n_chunks - 1 - c, col))
    direction_scratch = [
        pltpu.VMEM((n_pairs, LANES, LANES), F32),
        pltpu.VMEM((n_pairs, HGRN_CHUNK, LANES), F32),
        pltpu.VMEM((n_pairs * n_lvl, HGRN_CHUNK, LANES), F32),
        pltpu.VMEM((n_pairs, LANES, LANES), F32),
        pltpu.VMEM((HGRN_CHUNK, d_hgrn), BF16),
        pltpu.VMEM((1, d_hgrn), F32),
    ]
    assert len(direction_scratch) == N_HGRN_SCRATCH
    kern = functools.partial(_hgrn_kernel, rows=rows)
    out = jax.ShapeDtypeStruct((bsz, seq, d_hgrn), F32)
    return pl.pallas_call(
        kern,
        out_shape=(out, out, jax.ShapeDtypeStruct((zero_rows, LANES), F32)),
        grid=(bsz, n_chunks),
        in_specs=[fwd(col0), fwd(col0 + 1), fwd(col0 + 3), bwd(col0), bwd(col0 + 2), bwd(col0 + 3),
                  pl.BlockSpec((1, d_hgrn), lambda b, c: (0, 0))],
        out_specs=(pl.BlockSpec((1, rows, d_hgrn), lambda b, c: (b, c, 0)),
                   pl.BlockSpec((1, rows, d_hgrn), lambda b, c: (b, n_chunks - 1 - c, 0)),
                   pl.BlockSpec((zero_blk, LANES), lambda b, c: (b * n_chunks + c, 0))),
        scratch_shapes=direction_scratch + direction_scratch,
        compiler_params=_params(("arbitrary", "arbitrary")),
        name="hgrn2",
    )(proj, proj, proj, proj, proj, proj, lb.reshape(1, d_hgrn))


def _gelu_tanh(y):
    return 0.5 * y * (1.0 + jnp.tanh(0.7978845608028654 * (y + 0.044715 * (y * y * y))))


def _post_kernel(lf_ref, lb_ref, y_ref, of_ref, ob_ref, g_ref, x_ref, nlw_ref, nhw_ref, wo_ref,
                 gm_ref, nfw_ref, scf_ref, shf_ref, wr_ref, br_ref,
                 xo_ref, h_ref, slab_ref, cnt_ref, route_ref, carry_ref, *, tm, d_lru):
    first = (pl.program_id(0) == 0) & (pl.program_id(1) == 0)

    @pl.when(first)
    def _():
        carry_ref[...] = jnp.zeros_like(carry_ref)

    lru = (lf_ref[0] + lb_ref[0]) * _gelu_tanh(y_ref[0])
    ms = jnp.mean(lru * lru, axis=-1, keepdims=True)
    lru = lru * lax.rsqrt(ms + NORM_EPS) * nlw_ref[...]

    hg = of_ref[0] + ob_ref[0]
    width = hg.shape[1]
    hd = width // HGRN_HEADS
    hd_shift = hd.bit_length() - 1
    er = lax.broadcasted_iota(jnp.int32, (width, width), 0) >> hd_shift
    ec = lax.broadcasted_iota(jnp.int32, (width, width), 1) >> hd_shift
    head_sum = jnp.where(er == ec, 1.0, 0.0).astype(BF16)
    sq = hg * hg
    sq_hi = sq.astype(BF16)
    sq_lo = (sq - sq_hi.astype(F32)).astype(BF16)
    ms_h = (_dot(sq_hi, head_sum) + _dot(sq_lo, head_sum)) * (1.0 / hd)
    g = g_ref[0]
    hg = (hg * lax.rsqrt(ms_h + NORM_EPS) * nhw_ref[...]) * (g * _sigmoid(g))

    mixed = _dot(lru.astype(BF16), wo_ref[0:d_lru, :]) + _dot(hg.astype(BF16), wo_ref[d_lru:, :])
    x_new = x_ref[0] + gm_ref[0] * mixed
    xo_ref[0] = x_new

    h = _rms_mod(x_new, nfw_ref[...], scf_ref[0], shf_ref[0])
    _tiles_store(h_ref, h, tm, lead=(0,))

    logits = _dot(h.astype(BF16), wr_ref[...]) + br_ref[...]
    lane = lax.broadcasted_iota(jnp.int32, (tm, ROUTE_LANES), 1)
    lane_f = lane.astype(F32)
    far = float(ROUTE_LANES)
    is_g = lane < N_GROUPS
    gl = jnp.where(is_g, logits, NEG_BIG)
    gmax = jnp.max(gl, axis=-1, keepdims=True)
    g_idx = jnp.min(jnp.where(gl == gmax, lane_f, far), axis=-1, keepdims=True)
    p_group = 1.0 / jnp.sum(jnp.where(is_g, jnp.exp(gl - gmax), 0.0), axis=-1, keepdims=True)
    e_lane = lane - N_GROUPS
    in_group = (e_lane >= 0) & (e_lane < N_EXPERTS) & ((e_lane >> (EXPERTS_PER_GROUP.bit_length() - 1)).astype(F32) == g_idx)
    ev = jnp.where(in_group, logits, NEG_BIG)
    top1 = jnp.max(ev, axis=-1, keepdims=True)
    i1 = jnp.min(jnp.where(in_group & (ev == top1), lane_f, far), axis=-1, keepdims=True)
    rest = in_group & (lane_f != i1)
    ev2 = jnp.where(rest, logits, NEG_BIG)
    top2 = jnp.max(ev2, axis=-1, keepdims=True)
    i2 = jnp.min(jnp.where(rest & (ev2 == top2), lane_f, far), axis=-1, keepdims=True)
    e1 = i1 - float(N_GROUPS)
    e2 = i2 - float(N_GROUPS)
    ex = jnp.exp(top2 - top1)
    w1 = p_group / (1.0 + ex)
    w2 = p_group * ex / (1.0 + ex)

    sel1 = lane_f == e1
    sel2 = lane_f == e2
    onehot = jnp.where(sel1 | sel2, 1.0, 0.0)
    tr = lax.broadcasted_iota(jnp.int32, (tm, tm), 0)
    tc = lax.broadcasted_iota(jnp.int32, (tm, tm), 1)
    before = jnp.where(tc < tr, 1.0, 0.0).astype(BF16)
    cnt = _dot(before, onehot.astype(BF16)) + carry_ref[0:1]
    rank1 = jnp.sum(jnp.where(sel1, cnt, 0.0), axis=-1, keepdims=True)
    rank2 = jnp.sum(jnp.where(sel2, cnt, 0.0), axis=-1, keepdims=True)
    total = carry_ref[0:1] + jnp.sum(onehot, axis=0, keepdims=True)
    carry_ref[...] = jnp.broadcast_to(total, carry_ref.shape)
    cnt_ref[...] = jnp.broadcast_to(total, cnt_ref.shape)

    slab = jnp.where(lane == 0, e1, 0.0)
    slab = jnp.where(lane == 1, e2, slab)
    slab = jnp.where(lane == 2, w1, slab)
    slab = jnp.where(lane == 3, w2, slab)
    slab = jnp.where(lane == 4, rank1, slab)
    slab = jnp.where(lane == 5, rank2, slab)
    slab_ref[0] = slab
    route_ref[...] = slab.T[0:SUBLANES]


def _post_mixer(lru_f, lru_b, proj, hg_f, hg_b, x, nlw, nhw, wo_bf16, g_mix, nfw, sc_ffn, sh_ffn, wr_bf16, br,
                *, tm=512):
    bsz, seq, d = x.shape
    d_lru = lru_f.shape[-1]
    d_hgrn = hg_f.shape[-1]
    y_col = 1
    g_col = (2 * d_lru) // d_hgrn + 4
    row = lambda w: pl.BlockSpec((1, tm, w), lambda b, i: (b, i, 0))
    vec = lambda w: pl.BlockSpec((1, w), lambda b, i: (0, 0))
    per_b = lambda: pl.BlockSpec((1, 1, d), lambda b, i: (b, 0, 0))
    kern = functools.partial(_post_kernel, tm=tm, d_lru=d_lru)
    return pl.pallas_call(
        kern,
        out_shape=(
            jax.ShapeDtypeStruct((bsz, seq, d), F32),
            jax.ShapeDtypeStruct((bsz, seq * SUBLANES, LANES), F32),
            jax.ShapeDtypeStruct((bsz, seq, ROUTE_LANES), F32),
            jax.ShapeDtypeStruct((SUBLANES, ROUTE_LANES), F32),
            jax.ShapeDtypeStruct((SUBLANES, bsz * seq), F32),
        ),
        grid=(bsz, seq // tm),
        in_specs=[
            row(d_lru), row(d_lru),
            pl.BlockSpec((1, tm, d_lru), lambda b, i: (b, i, y_col)),
            row(d_hgrn), row(d_hgrn),
            pl.BlockSpec((1, tm, d_hgrn), lambda b, i: (b, i, g_col)),
            row(d), vec(d_lru), vec(d_hgrn),
            pl.BlockSpec((d, d), lambda b, i: (0, 0)),
            per_b(), vec(d), per_b(), per_b(),
            pl.BlockSpec((d, ROUTE_LANES), lambda b, i: (0, 0)),
            vec(ROUTE_LANES),
        ],
        out_specs=(
            row(d), pl.BlockSpec((1, tm * SUBLANES, LANES), lambda b, i: (b, i, 0)), row(ROUTE_LANES),
            pl.BlockSpec((SUBLANES, ROUTE_LANES), lambda b, i: (0, 0)),
            pl.BlockSpec((SUBLANES, tm), lambda b, i: (0, b * (seq // tm) + i)),
        ),
        scratch_shapes=[pltpu.VMEM((SUBLANES, ROUTE_LANES), F32)],
        compiler_params=_params(("arbitrary", "arbitrary")),
        name="post_mixer_router",
    )(lru_f, lru_b, proj, hg_f, hg_b, proj, x, nlw.reshape(1, d_lru), nhw.reshape(1, d_hgrn), wo_bf16,
      g_mix, nfw.reshape(1, d), sc_ffn, sh_ffn, wr_bf16, br)


def _tiles_load(ref, n, lead=()):
    return jnp.concatenate(
        [ref[(*lead, pl.ds(j, n, stride=SUBLANES), slice(None))] for j in range(SUBLANES)], axis=1)


def _tiles_store(ref, val, n, lead=()):
    for j in range(SUBLANES):
        ref[(*lead, pl.ds(j, n, stride=SUBLANES), slice(None))] = val[:, j * LANES:(j + 1) * LANES]


def _token_tile(ref, t):
    return ref.at[pl.ds(pl.multiple_of(t * SUBLANES, SUBLANES), SUBLANES)]


def _dest_kernel(start_ref, route_ref, o_ref):
    route = route_ref[...].astype(jnp.int32)
    start = jnp.zeros_like(route)
    for e in range(N_EXPERTS):
        start = jnp.where(route == e, start_ref[e], start)
    o_ref[...] = start + pltpu.roll(route, SUBLANES // 2, 0)


def _dest_rows(expert_start, route):
    return pl.pallas_call(
        _dest_kernel,
        out_shape=jax.ShapeDtypeStruct(route.shape, jnp.int32),
        grid_spec=pltpu.PrefetchScalarGridSpec(
            num_scalar_prefetch=1,
            grid=(1,),
            in_specs=[pl.BlockSpec(route.shape, lambda i, s: (0, 0))],
            out_specs=pl.BlockSpec(route.shape, lambda i, s: (0, 0)),
        ),
        compiler_params=pltpu.CompilerParams(dimension_semantics=("arbitrary",)),
        name="moe_dest_rows",
    )(expert_start, route)


def _dispatch_kernel(d1_ref, d2_ref, h_ref, z_ref, o_ref, sem, *, tb):
    del z_ref
    base = pl.program_id(0) * tb

    def issue(r, carry):
        t = base + r
        pltpu.make_async_copy(_token_tile(h_ref, r), _token_tile(o_ref, d1_ref[t]), sem).start(priority=0)
        pltpu.make_async_copy(_token_tile(h_ref, r), _token_tile(o_ref, d2_ref[t]), sem).start(priority=1)
        return carry

    lax.fori_loop(0, tb, issue, 0, unroll=DMA_ISSUE_UNROLL)
    for _ in range(2):
        pltpu.make_async_copy(h_ref, o_ref.at[pl.ds(0, tb * SUBLANES)], sem).wait()


def _dispatch(dest1, dest2, h_tiles, zero_tiles, *, tb=512):
    m = h_tiles.shape[0] // SUBLANES
    n_rows = zero_tiles.shape[0] // SUBLANES
    kern = functools.partial(_dispatch_kernel, tb=tb)
    return pl.pallas_call(
        kern,
        out_shape=jax.ShapeDtypeStruct((n_rows * SUBLANES, LANES), h_tiles.dtype),
        grid_spec=pltpu.PrefetchScalarGridSpec(
            num_scalar_prefetch=2,
            grid=(m // tb,),
            in_specs=[pl.BlockSpec((tb * SUBLANES, LANES), lambda i, d1, d2: (i, 0)),
                      pl.BlockSpec(memory_space=pl.ANY)],
            out_specs=pl.BlockSpec(memory_space=pl.ANY),
            scratch_shapes=[pltpu.SemaphoreType.DMA(())],
        ),
        input_output_aliases={3: 0},
        compiler_params=pltpu.CompilerParams(dimension_semantics=("arbitrary",), has_side_effects=True),
        name="moe_dispatch",
    )(dest1, dest2, h_tiles, zero_tiles)


def _expert_kernel(plan_ref, x_ref, wg_hbm, wu_hbm, wd_hbm, o_ref, wg_f, wu_f, wd_f, wg_s, wu_s, wd_s, sem,
                   *, layer):
    i = pl.program_id(0)
    n_blocks = pl.num_programs(0)
    n_used = plan_ref[n_blocks]
    expert = plan_ref[i]
    next_expert = plan_ref[n_blocks + 1 + i]
    slot = plan_ref[2 * n_blocks + 1 + i]
    first_block = ((i == 0) | (plan_ref[jnp.maximum(i - 1, 0)] != expert)) & (i < n_used)

    def copies(e, s):
        return (pltpu.make_async_copy(wg_hbm.at[layer, e], wg_f.at[s], sem.at[s]),
                pltpu.make_async_copy(wu_hbm.at[layer, e], wu_f.at[s], sem.at[s]),
                pltpu.make_async_copy(wd_hbm.at[layer, e], wd_f.at[s], sem.at[s]))

    @pl.when(i == 0)
    def _():
        for c in copies(expert, slot):
            c.start()

    @pl.when(first_block)
    def _():
        for c in copies(expert, slot):
            c.wait()
        wg_s[...] = wg_f[slot].astype(BF16)
        wu_s[...] = wu_f[slot].astype(BF16)
        wd_s[...] = wd_f[slot].astype(BF16)

        @pl.when(next_expert >= 0)
        def _():
            for c in copies(next_expert, 1 - slot):
                c.start()

    blk = x_ref.shape[0] // SUBLANES

    @pl.when(i < n_used)
    def _():
        x = _tiles_load(x_ref, blk).astype(BF16)
        gate = _dot(x, wg_s[...])
        up = _dot(x, wu_s[...])
        act = (gate * _sigmoid(gate)) * up
        _tiles_store(o_ref, _dot(act.astype(BF16), wd_s[...]), blk)

    @pl.when(i >= n_used)
    def _():
        o_ref[...] = jnp.zeros_like(o_ref)


def _expert_plan(blk_expert, blocks_used):
    n_blocks = blk_expert.shape[0]
    idx = jnp.arange(n_blocks, dtype=jnp.int32)
    change = jnp.concatenate([jnp.ones((1,), bool), blk_expert[1:] != blk_expert[:-1]])
    slot = (jnp.cumsum(change.astype(jnp.int32)) - 1) & 1
    change_at = jnp.where(change, idx, n_blocks)
    from_here = lax.cummin(change_at[::-1])[::-1]
    next_change = jnp.concatenate([from_here[1:], jnp.full((1,), n_blocks, jnp.int32)])
    next_expert = jnp.where(next_change < blocks_used, blk_expert[jnp.minimum(next_change, n_blocks - 1)], -1)
    return jnp.concatenate([blk_expert, blocks_used.reshape(1), next_expert, slot]).astype(jnp.int32)


def _experts(plan, x_tiles, wg, wu, wd, layer):
    n_rows = x_tiles.shape[0] // SUBLANES
    d, de = wg.shape[-2:]
    blk = EXPERT_BLOCK
    n_blocks = n_rows // blk
    kern = functools.partial(_expert_kernel, layer=layer)
    return pl.pallas_call(
        kern,
        out_shape=jax.ShapeDtypeStruct((n_rows * SUBLANES, LANES), F32),
        grid_spec=pltpu.PrefetchScalarGridSpec(
            num_scalar_prefetch=1,
            grid=(n_blocks,),
            in_specs=[
                pl.BlockSpec((blk * SUBLANES, LANES), lambda i, plan: (jnp.minimum(i, plan[n_blocks] - 1), 0)),
                pl.BlockSpec(memory_space=pl.ANY),
                pl.BlockSpec(memory_space=pl.ANY),
                pl.BlockSpec(memory_space=pl.ANY),
            ],
            out_specs=pl.BlockSpec((blk * SUBLANES, LANES), lambda i, plan: (i, 0)),
            scratch_shapes=[
                pltpu.VMEM((2, d, de), F32), pltpu.VMEM((2, d, de), F32), pltpu.VMEM((2, de, d), F32),
                pltpu.VMEM((d, de), BF16), pltpu.VMEM((d, de), BF16), pltpu.VMEM((de, d), BF16),
                pltpu.SemaphoreType.DMA((2,)),
            ],
        ),
        compiler_params=_params(("arbitrary",)),
        name="moe_experts",
    )(plan, x_tiles, wg, wu, wd)


def _combine_kernel(d1_ref, d2_ref, y_ref, slab_ref, x_ref, g_ref, nw_ref, o_ref, ra0, rb0, ra1, rb1, sem,
                    *, tm, tiles, n_steps, final_norm):
    step = pl.program_id(0) * tiles + pl.program_id(1)
    bufs = ((ra0, rb0), (ra1, rb1))

    def gather(tile, slot):
        base = tile * tm
        r1_ref, r2_ref = bufs[slot]

        def issue(r, carry):
            t = base + r
            pltpu.make_async_copy(_token_tile(y_ref, d1_ref[t]), _token_tile(r1_ref, r),
                                  sem.at[slot]).start(priority=0)
            pltpu.make_async_copy(_token_tile(y_ref, d2_ref[t]), _token_tile(r2_ref, r),
                                  sem.at[slot]).start(priority=1)
            return carry

        lax.fori_loop(0, tm, issue, 0, unroll=DMA_ISSUE_UNROLL)

    @pl.when(step == 0)
    def _():
        gather(0, 0)

    for slot in range(2):
        @pl.when((step & 1) == slot)
        def _():
            @pl.when(step + 1 < n_steps)
            def _():
                gather(step + 1, 1 - slot)

            r1_ref, r2_ref = bufs[slot]
            pltpu.make_async_copy(y_ref.at[pl.ds(0, tm * SUBLANES)], r1_ref, sem.at[slot]).wait()
            pltpu.make_async_copy(y_ref.at[pl.ds(0, tm * SUBLANES)], r2_ref, sem.at[slot]).wait()
            slab = slab_ref[0]
            y = slab[:, 2:3] * _tiles_load(r1_ref, tm) + slab[:, 3:4] * _tiles_load(r2_ref, tm)
            out = x_ref[0] + g_ref[0] * y
            if final_norm:
                ms = jnp.mean(out * out, axis=-1, keepdims=True)
                out = out * lax.rsqrt(ms + NORM_EPS) * nw_ref[...]
            o_ref[0] = out


def _combine(dest1, dest2, y_buf, slab, x, g_ffn, norm_w, *, final_norm, tm=512):
    bsz, seq, d = x.shape
    tiles = seq // tm
    kern = functools.partial(_combine_kernel, tm=tm, tiles=tiles, n_steps=bsz * tiles, final_norm=final_norm)
    row_buf = pltpu.VMEM((tm * SUBLANES, LANES), F32)
    return pl.pallas_call(
        kern,
        out_shape=jax.ShapeDtypeStruct((bsz, seq, d), F32),
        grid_spec=pltpu.PrefetchScalarGridSpec(
            num_scalar_prefetch=2,
            grid=(bsz, tiles),
            in_specs=[
                pl.BlockSpec(memory_space=pl.ANY),
                pl.BlockSpec((1, tm, ROUTE_LANES), lambda b, i, d1, d2: (b, i, 0)),
                pl.BlockSpec((1, tm, d), lambda b, i, d1, d2: (b, i, 0)),
                pl.BlockSpec((1, 1, d), lambda b, i, d1, d2: (b, 0, 0)),
                pl.BlockSpec((1, d), lambda b, i, d1, d2: (0, 0)),
            ],
            out_specs=pl.BlockSpec((1, tm, d), lambda b, i, d1, d2: (b, i, 0)),
            scratch_shapes=[row_buf, row_buf, row_buf, row_buf, pltpu.SemaphoreType.DMA((2,))],
        ),
        compiler_params=_params(("arbitrary", "arbitrary")),
        name="moe_combine",
    )(dest1, dest2, y_buf, slab, x, g_ffn, norm_w.reshape(1, d))


def _block_diag(w):
    heads, hd, _ = w.shape
    n = heads * hd
    tiled = jnp.tile(w.reshape(n, hd), (1, heads))
    blk_r = lax.broadcasted_iota(jnp.int32, (n, n), 0) // hd
    blk_c = lax.broadcasted_iota(jnp.int32, (n, n), 1) // hd
    return jnp.where(blk_r == blk_c, tiled, 0.0)


def kernel(x, c, ada_w, ada_b, norm_mix_w, w_in, conv_w, conv_b, lru_wa, lru_ba, lru_wx, lru_bx, lru_lambda, norm_lru_w, hgrn_lb, norm_hgrn_w, w_out, norm_ffn_w, router_group_w, router_group_b, router_expert_w, router_expert_b, expert_w_gate, expert_w_up, expert_w_down, final_norm_w):
    bsz, seq, d = x.shape
    assert d == SUBLANES * LANES, "the MoE row movement keeps one (8, 128) tile per token"
    depth = ada_w.shape[0]
    d_lru = conv_w.shape[-1]
    d_hgrn = hgrn_lb.shape[-1]
    m = bsz * seq
    n_rows = m * 2 + N_EXPERTS * EXPERT_BLOCK
    n_blocks = n_rows // EXPERT_BLOCK

    mod = _modulation(c, ada_w, ada_b)
    lb_cum = jnp.cumsum(jax.nn.softmax(hgrn_lb.astype(F32), axis=0), axis=0)
    lb_all = lb_cum - lb_cum[0:1]

    for l in range(depth):
        sh_mix, sc_mix, g_mix, sh_ffn, sc_ffn, g_ffn = [
            mod[l, :, i * d:(i + 1) * d].reshape(bsz, 1, d) for i in range(6)]
        proj = _in_proj(x, norm_mix_w[l], sc_mix, sh_mix, w_in, l)
        wa_bd = jnp.stack([_block_diag(lru_wa[l, 0]), _block_diag(lru_wa[l, 1])]).astype(BF16)
        wx_bd = jnp.stack([_block_diag(lru_wx[l, 0]), _block_diag(lru_wx[l, 1])]).astype(BF16)
        lru = [
            _lru_scan(proj, conv_w[l], conv_b[l], wa_bd, lru_ba[l], wx_bd, lru_bx[l], lru_lambda[l],
                      reverse=rv)
            for rv in (False, True)]
        hg_f, hg_b, zero_tiles = _hgrn(proj, lb_all[l], n_rows * SUBLANES, d_lru=d_lru, d_hgrn=d_hgrn)

        lane_pad = ROUTE_LANES - N_GROUPS - N_EXPERTS
        wr = jnp.pad(jnp.concatenate([router_group_w[l], router_expert_w[l]], axis=1), ((0, 0), (0, lane_pad)))
        br = jnp.pad(jnp.concatenate([router_group_b[l], router_expert_b[l]]), (0, lane_pad)).reshape(1, ROUTE_LANES)
        x_mid, h_ffn, slab, counts, route = _post_mixer(
            lru[0], lru[1], proj, hg_f, hg_b, x, norm_lru_w[l], norm_hgrn_w[l], w_out[l].astype(BF16), g_mix,
            norm_ffn_w[l], sc_ffn, sh_ffn, wr.astype(BF16), br)

        cnt = counts[0, :N_EXPERTS].astype(jnp.int32)
        padded = ((cnt + EXPERT_BLOCK - 1) // EXPERT_BLOCK) * EXPERT_BLOCK
        pend = jnp.cumsum(padded)
        pstart = pend - padded
        blk_start = jnp.arange(n_blocks, dtype=jnp.int32) * EXPERT_BLOCK
        blk_expert = jnp.minimum(jnp.sum(pend[None, :] <= blk_start[:, None], axis=1), N_EXPERTS - 1)
        plan = _expert_plan(blk_expert.astype(jnp.int32), (pend[N_EXPERTS - 1] // EXPERT_BLOCK).astype(jnp.int32))
        dest = _dest_rows(pstart.astype(jnp.int32), route)
        dest1, dest2 = dest[0], dest[1]

        x_buf = _dispatch(dest1, dest2, h_ffn.reshape(m * SUBLANES, LANES), zero_tiles)
        y_buf = _experts(plan, x_buf, expert_w_gate, expert_w_up, expert_w_down, l)
        x = _combine(dest1, dest2, y_buf, slab, x_mid, g_ffn, final_norm_w, final_norm=(l == depth - 1))

    return x
```

```python
import functools

import jax
import jax.numpy as jnp
from jax import lax
from jax.experimental import pallas as pl
from jax.experimental.pallas import tpu as pltpu

F32 = jnp.float32
BF16 = jnp.bfloat16

LRU_HEADS = 8
HGRN_HEADS = 8
CONV_WIDTH = 4
LRU_C = 8.0
N_GROUPS = 4
EXPERTS_PER_GROUP = 8
N_EXPERTS = N_GROUPS * EXPERTS_PER_GROUP
NORM_EPS = 1e-6

LANES = 128
SUBLANES = 8
VMEM_LIMIT = 56 * 1024 * 1024

HGRN_CHUNK = 64
HGRN_SUB = 8
LOG2E = 1.4426950408889634
ROUTE_LANES = LANES
EXPERT_BLOCK = 512
DMA_ISSUE_UNROLL = 8
NEG_BIG = -3.0e38


def _params(sem):
    return pltpu.CompilerParams(dimension_semantics=sem, vmem_limit_bytes=VMEM_LIMIT)


def _dot(a, b):
    return jnp.dot(a, b, preferred_element_type=F32)


def _dot_nt(a, b):
    return lax.dot_general(a, b, (((1,), (1,)), ((), ())), preferred_element_type=F32)


def _dot_tn(a, b):
    return lax.dot_general(a, b, (((0,), (0,)), ((), ())), preferred_element_type=F32)


def _dot01_exact(m01, x):
    hi = x.astype(BF16)
    r1 = x - hi.astype(F32)
    mid = r1.astype(BF16)
    lo = (r1 - mid.astype(F32)).astype(BF16)
    return _dot(m01, hi) + _dot(m01, mid) + _dot(m01, lo)


def _sigmoid(x):
    return 1.0 / (1.0 + jnp.exp(-x))


def _mod_kernel(c_ref, w_ref, b_ref, o_ref):
    c = c_ref[...]
    cond = c * _sigmoid(c)
    o_ref[0] = _dot(cond.astype(BF16), w_ref[0].astype(BF16)) + b_ref[0]


def _modulation(c, ada_w, ada_b):
    depth, d, n = ada_w.shape
    bsz = c.shape[0]
    rows = -(-bsz // SUBLANES) * SUBLANES
    c_pad = jnp.pad(c, ((0, rows - bsz), (0, 0)))
    tn = n // 6
    out = pl.pallas_call(
        _mod_kernel,
        out_shape=jax.ShapeDtypeStruct((depth, rows, n), F32),
        grid=(depth, n // tn),
        in_specs=[
            pl.BlockSpec((rows, d), lambda l, j: (0, 0)),
            pl.BlockSpec((1, d, tn), lambda l, j: (l, 0, j)),
            pl.BlockSpec((1, 1, tn), lambda l, j: (l, 0, j)),
        ],
        out_specs=pl.BlockSpec((1, rows, tn), lambda l, j: (l, 0, j)),
        compiler_params=_params(("arbitrary", "arbitrary")),
        name="adaln_mod",
    )(c_pad, ada_w, ada_b.reshape(depth, 1, n))
    return out[:, :bsz]


def _rms_mod(x, nw, sc, sh):
    ms = jnp.mean(x * x, axis=-1, keepdims=True)
    return (x * lax.rsqrt(ms + NORM_EPS) * nw) * (1.0 + sc) + sh


def _inproj_kernel(x_ref, nw_ref, sc_ref, sh_ref, w_ref, o_ref, w_s):
    @pl.when((pl.program_id(0) == 0) & (pl.program_id(1) == 0))
    def _():
        w_s[...] = w_ref[0].astype(BF16)

    h = _rms_mod(x_ref[0], nw_ref[...], sc_ref[0], sh_ref[0])
    o_ref[0] = _dot(h.astype(BF16), w_s[...])


def _in_proj(x, nw, sc, sh, w_in, layer, tm=512):
    bsz, seq, d = x.shape
    n = w_in.shape[-1]
    return pl.pallas_call(
        _inproj_kernel,
        out_shape=jax.ShapeDtypeStruct((bsz, seq, n), F32),
        grid=(bsz, seq // tm),
        in_specs=[
            pl.BlockSpec((1, tm, d), lambda b, i: (b, i, 0)),
            pl.BlockSpec((1, d), lambda b, i: (0, 0)),
            pl.BlockSpec((1, 1, d), lambda b, i: (b, 0, 0)),
            pl.BlockSpec((1, 1, d), lambda b, i: (b, 0, 0)),
            pl.BlockSpec((1, d, n), lambda b, i: (layer, 0, 0), pipeline_mode=pl.Buffered(1)),
        ],
        out_specs=pl.BlockSpec((1, tm, n), lambda b, i: (b, i, 0)),
        scratch_shapes=[pltpu.VMEM((d, n), BF16)],
        compiler_params=_params(("arbitrary", "arbitrary")),
        name="in_proj",
    )(x, nw.reshape(1, d), sc, sh, w_in)


def _lru_kernel(x_ref, xp_ref, xn_ref, cw_ref, cb_ref, wa_ref, ba_ref, wx_ref, bx_ref, lam_ref,
                o_ref, carry_ref, sa_ref, sb_ref, cin_ref, *, reverse, n_chunks, rows):
    c = pl.program_id(1)
    chunk = (n_chunks - 1 - c) if reverse else c

    @pl.when(c == 0)
    def _():
        carry_ref[...] = jnp.zeros_like(carry_ref)

    x = x_ref[0]
    width = x.shape[1]
    row = lax.broadcasted_iota(jnp.int32, (rows, width), 0)
    has_prev = jnp.where(chunk > 0, 1.0, 0.0)
    has_next = jnp.where(chunk < n_chunks - 1, 1.0, 0.0)
    xp = xp_ref[0] * has_prev
    xn = xn_ref[0] * has_next
    xe = jnp.concatenate([xp, x, xn], axis=0)
    h8 = SUBLANES
    cw = cw_ref[...]
    xc = (cw[0:1] * xe[h8 - 2:h8 - 2 + rows] + cw[1:2] * xe[h8 - 1:h8 - 1 + rows] + cw[2:3] * x
          + cw[3:4] * xe[h8 + 1:h8 + 1 + rows] + cb_ref[...])

    xcb = xc.astype(BF16)
    r = _sigmoid(_dot(xcb, wa_ref[0]) + ba_ref[0])
    gate_i = _sigmoid(_dot(xcb, wx_ref[0]) + bx_ref[0])
    lam = lam_ref[0]
    softplus_neg_lam = jnp.maximum(-lam, 0.0) + jnp.log1p(jnp.exp(-jnp.abs(lam)))
    log_a = (-LRU_C) * r * softplus_neg_lam
    a = jnp.exp(log_a)
    t = jnp.tanh(-log_a)
    u = jnp.sqrt(2.0 * t / (1.0 + t)) * (gate_i * xc)

    groups = rows // SUBLANES
    acc_a = a.reshape(groups, SUBLANES, width)
    acc_b = u.reshape(groups, SUBLANES, width)
    sub = lax.broadcasted_iota(jnp.int32, (groups, SUBLANES, width), 1)
    s = 1
    while s < SUBLANES:
        if reverse:
            valid = sub < SUBLANES - s
            sh_a, sh_b = pltpu.roll(acc_a, SUBLANES - s, 1), pltpu.roll(acc_b, SUBLANES - s, 1)
        else:
            valid = sub >= s
            sh_a, sh_b = pltpu.roll(acc_a, s, 1), pltpu.roll(acc_b, s, 1)
        acc_b = jnp.where(valid, acc_a * sh_b + acc_b, acc_b)
        acc_a = jnp.where(valid, acc_a * sh_a, acc_a)
        s *= 2
    acc_a = acc_a.reshape(rows, width)
    acc_b = acc_b.reshape(rows, width)
    groups = rows // SUBLANES
    edge = 0 if reverse else SUBLANES - 1
    n_tiles = width // LANES
    for j in range(n_tiles):
        sa_ref[j] = acc_a[:, j * LANES:(j + 1) * LANES]
        sb_ref[j] = acc_b[:, j * LANES:(j + 1) * LANES]
    ea = jnp.concatenate([sa_ref[j, pl.ds(edge, groups, stride=SUBLANES), :] for j in range(n_tiles)], axis=1)
    eb = jnp.concatenate([sb_ref[j, pl.ds(edge, groups, stride=SUBLANES), :] for j in range(n_tiles)], axis=1)
    grow = lax.broadcasted_iota(jnp.int32, (groups, width), 0)
    s = 1
    while s < groups:
        if reverse:
            valid = grow < groups - s
            sh_a, sh_b = pltpu.roll(ea, groups - s, 0), pltpu.roll(eb, groups - s, 0)
        else:
            valid = grow >= s
            sh_a, sh_b = pltpu.roll(ea, s, 0), pltpu.roll(eb, s, 0)
        eb = jnp.where(valid, ea * sh_b + eb, eb)
        ea = jnp.where(valid, ea * sh_a, ea)
        s *= 2
    carry = carry_ref[...]
    group_out = eb + ea * carry
    if reverse:
        carry_in = jnp.where(grow == groups - 1, carry, pltpu.roll(group_out, groups - 1, 0))
        carry_ref[...] = group_out[0:1]
    else:
        carry_in = jnp.where(grow == 0, carry, pltpu.roll(group_out, 1, 0))
        carry_ref[...] = group_out[groups - 1:groups]
    cin_ref[...] = carry_in
    for g in range(groups):
        rs = slice(g * SUBLANES, (g + 1) * SUBLANES)
        o_ref[0, rs, :] = acc_b[rs] + acc_a[rs] * cin_ref[g:g + 1, :]


def _lru_scan(proj, conv_w, conv_b, wa_bd, ba, wx_bd, bx, lam, *, reverse, rows=512):
    bsz, seq, _ = proj.shape
    d_lru = conv_w.shape[1]
    n_chunks = seq // rows
    halo = rows // SUBLANES
    last_halo = seq // SUBLANES - 1
    dirn = 1 if reverse else 0

    def chunk_of(c):
        return (n_chunks - 1 - c) if reverse else c

    vec = lambda: pl.BlockSpec((1, 1, d_lru), lambda b, c: (dirn, 0, 0))
    mat = lambda: pl.BlockSpec((1, d_lru, d_lru), lambda b, c: (dirn, 0, 0))
    kern = functools.partial(_lru_kernel, reverse=reverse, n_chunks=n_chunks, rows=rows)
    return pl.pallas_call(
        kern,
        out_shape=jax.ShapeDtypeStruct((bsz, seq, d_lru), F32),
        grid=(bsz, n_chunks),
        in_specs=[
            pl.BlockSpec((1, rows, d_lru), lambda b, c: (b, chunk_of(c), 0)),
            pl.BlockSpec((1, SUBLANES, d_lru),
                         lambda b, c: (b, jnp.maximum(chunk_of(c) * halo - 1, 0), 0)),
            pl.BlockSpec((1, SUBLANES, d_lru),
                         lambda b, c: (b, jnp.minimum((chunk_of(c) + 1) * halo, last_halo), 0)),
            pl.BlockSpec((CONV_WIDTH, d_lru), lambda b, c: (0, 0)),
            pl.BlockSpec((1, d_lru), lambda b, c: (0, 0)),
            mat(), vec(), mat(), vec(), vec(),
        ],
        out_specs=pl.BlockSpec((1, rows, d_lru), lambda b, c: (b, chunk_of(c), 0)),
        scratch_shapes=[pltpu.VMEM((1, d_lru), F32), pltpu.VMEM((d_lru // LANES, rows, LANES), F32),
                        pltpu.VMEM((d_lru // LANES, rows, LANES), F32),
                        pltpu.VMEM((rows // SUBLANES, d_lru), F32)],
        compiler_params=_params(("arbitrary", "arbitrary")),
        name="lru_bwd" if reverse else "lru_fwd",
    )(proj, proj, proj, conv_w, conv_b.reshape(1, d_lru), wa_bd, ba.reshape(2, 1, d_lru),
      wx_bd, bx.reshape(2, 1, d_lru), lam.reshape(2, 1, d_lru))


def _hgrn_direction(rev, q_ref, f_ref, v_ref, lb_ref, o_ref, st_ref, diag_s, lvl_s, upd_s, qe_s, btot_s, *, rows):
    ck, sb = HGRN_CHUNK, HGRN_SUB
    n_blk = ck // sb
    sb_shift = sb.bit_length() - 1
    n_sub = rows // ck
    width = q_ref.shape[-1]
    n_pairs = width // LANES
    half = LANES // 2

    def flip(idx, n):
        return (n - 1 - idx) if rev else idx

    n_lvl = n_blk.bit_length() - 1
    tf = flip(lax.broadcasted_iota(jnp.int32, (ck, ck), 0), ck)
    uf = flip(lax.broadcasted_iota(jnp.int32, (ck, ck), 1), ck)
    tb, ub = tf >> sb_shift, uf >> sb_shift
    pb = flip(lax.broadcasted_iota(jnp.int32, (n_blk, ck), 0), n_blk)
    pub = flip(lax.broadcasted_iota(jnp.int32, (n_blk, ck), 1), ck) >> sb_shift
    mats = [jnp.where((tb == ub) & (uf <= tf), 1.0, 0.0),
            jnp.where(pub < pb, 1.0, 0.0)]
    for lvl in range(n_lvl):
        mid = ((pb >> (lvl + 1)) << (lvl + 1)) + (1 << lvl)
        mats.append(jnp.where(pub < mid, 1.0, 0.0))
    mats.append(jnp.ones((SUBLANES, ck), F32))
    m_cum = jnp.concatenate(mats, axis=0).astype(BF16)

    def per_block(rows8):
        return jnp.concatenate(
            [jnp.broadcast_to(rows8[jb:jb + 1], (sb, rows8.shape[1])) for jb in range(n_blk)], axis=0)
    row_blk = flip(lax.broadcasted_iota(jnp.int32, (ck, width), 0), ck) >> sb_shift
    upper = [((row_blk >> lvl) & 1) == 1 for lvl in range(n_lvl)]
    pr = flip(lax.broadcasted_iota(jnp.int32, (ck, LANES), 0), ck) >> sb_shift
    pc = flip(lax.broadcasted_iota(jnp.int32, (ck, LANES), 1) & (ck - 1), ck) >> sb_shift
    group_mask = [(pr >> (lvl + 1)) == (pc >> (lvl + 1)) for lvl in range(n_lvl)]
    lane = lax.broadcasted_iota(jnp.int32, (1, LANES), 1)
    head0 = lane < half

    def split_heads(x):
        xb = x.astype(BF16)
        zero = jnp.zeros_like(xb)
        return jnp.concatenate([jnp.where(head0, xb, zero), jnp.where(head0, zero, xb)], axis=0)

    sr = lax.broadcasted_iota(jnp.int32, (LANES, LANES), 0)
    sc = lax.broadcasted_iota(jnp.int32, (LANES, LANES), 1)
    same_head = (sr < half) == (sc < half)
    er = lax.broadcasted_iota(jnp.int32, (sb * LANES, LANES), 0)
    ec = lax.broadcasted_iota(jnp.int32, (sb * LANES, LANES), 1)
    sel = jnp.where(ec == (((er & (LANES - 1)) >> (half.bit_length() - 1)) * half + (er >> (LANES.bit_length() - 1))),
                    1.0, 0.0).astype(BF16)
    sub_row = flip(lax.broadcasted_iota(jnp.int32, (sb, LANES), 0), sb)
    lbv = lb_ref[...]

    def row_start(j):
        return pl.multiple_of(flip(j, n_sub) * ck, ck)

    def stage1a(j):
        r0 = row_start(j)
        q = q_ref[0, pl.ds(r0, ck), :]
        z = f_ref[0, pl.ds(r0, ck), :]
        v = v_ref[0, pl.ds(r0, ck), :]
        f = lbv + (1.0 - lbv) * _sigmoid(z)
        lf2 = jnp.log(f) * LOG2E
        k = 1.0 - f
        return q, v, k, _dot01_exact(m_cum, lf2)

    def stage1b(q, v, k, cums):
        bl2 = cums[0:ck]
        b2 = bl2 + per_block(cums[ck:ck + n_blk])
        tot_row = ck + (1 + n_lvl) * n_blk
        btot2 = cums[tot_row:tot_row + 1]
        log2_k = jnp.log(k) * LOG2E
        kb = b2 - log2_k
        kbl = bl2 - log2_k
        qe = q * jnp.exp2(b2)
        ke = jnp.exp2(btot2 - kb)
        q_lvl, k_lvl = [], []
        for lvl in range(n_lvl):
            split2 = per_block(cums[ck + (1 + lvl) * n_blk:ck + (2 + lvl) * n_blk])
            q_lvl.append(q * jnp.exp2(jnp.where(upper[lvl], b2 - split2, NEG_BIG)))
            k_lvl.append(jnp.exp2(jnp.where(upper[lvl], NEG_BIG, split2 - kb)))

        qe_s[...] = qe.astype(BF16)
        btot_s[...] = btot2
        for p in range(n_pairs):
            sl = slice(p * LANES, (p + 1) * LANES)
            diag_rows = []
            for jb in range(n_blk):
                rs = slice(jb * sb, (jb + 1) * sb)
                bl_b, kbl_b, q_b = bl2[rs, sl], kbl[rs, sl], q[rs, sl]
                terms = []
                for s in range(sb):
                    arg = jnp.where(sub_row >= flip(s, sb), bl_b - kbl_b[s:s + 1], NEG_BIG)
                    terms.append(q_b * jnp.exp2(arg))
                diag_rows.append(jnp.concatenate(terms, axis=1))
            diag_s[p] = _dot(jnp.concatenate(diag_rows, axis=0).astype(BF16), sel)
            for lvl in range(n_lvl):
                k_p = k_lvl[lvl][:, sl]
                lvl_s[p * n_lvl + lvl] = _dot_nt(q_lvl[lvl][:, sl].astype(BF16), split_heads(k_p))
            upd_s[p] = _dot_tn(v[:, sl].astype(BF16), ke[:, sl].astype(BF16))

    def stage2_issue(j):
        r0 = row_start(j)
        v = v_ref[0, pl.ds(r0, ck), :]
        out = []
        for p in range(n_pairs):
            sl = slice(p * LANES, (p + 1) * LANES)
            parts = []
            for jb in range(n_blk):
                blk = diag_s[p, jb * sb:(jb + 1) * sb, :]
                parts.append(pltpu.roll(blk, jb * sb, 1) if jb else blk)
            scores = jnp.concatenate(parts, axis=0)
            for lvl in range(n_lvl):
                s_lvl = lvl_s[p * n_lvl + lvl]
                scores = scores + (s_lvl if lvl == n_lvl - 1 else jnp.where(group_mask[lvl], s_lvl, 0.0))
            intra = _dot(scores.astype(BF16), split_heads(v[:, sl]))
            st = st_ref[p]
            inter = _dot_nt(qe_s[:, sl], st.astype(BF16))
            new_st = jnp.where(same_head, st * jnp.exp2(btot_s[:, sl]) + upd_s[p], 0.0)
            out.append((inter + intra, new_st))
        return r0, out

    def stage2_finish(r0, out):
        for p in range(n_pairs):
            o_ref[0, pl.ds(r0, ck), p * LANES:(p + 1) * LANES] = out[p][0]
            st_ref[p] = out[p][1]

    return stage1a, stage1b, stage2_issue, stage2_finish


N_HGRN_SCRATCH = 6


def _hgrn_kernel(qf_ref, ff_ref, vf_ref, qb_ref, fb_ref, vb_ref, lb_ref, of_ref, ob_ref, zero_ref, *scratch,
                 rows):
    fwd_scratch, bwd_scratch = scratch[:N_HGRN_SCRATCH], scratch[N_HGRN_SCRATCH:]

    @pl.when(pl.program_id(1) == 0)
    def _():
        fwd_scratch[0][...] = jnp.zeros_like(fwd_scratch[0])
        bwd_scratch[0][...] = jnp.zeros_like(bwd_scratch[0])

    f1a, f1b, f2, f3 = _hgrn_direction(False, qf_ref, ff_ref, vf_ref, lb_ref, of_ref, *fwd_scratch, rows=rows)
    b1a, b1b, b2, b3 = _hgrn_direction(True, qb_ref, fb_ref, vb_ref, lb_ref, ob_ref, *bwd_scratch, rows=rows)
    n_sub = rows // HGRN_CHUNK

    def stage1_both(j):
        fa = f1a(j)
        ba = b1a(j)
        f1b(*fa)
        b1b(*ba)

    zero_part = zero_ref.shape[0] // n_sub

    def store_zeros(j):
        zero_ref[pl.ds(pl.multiple_of(j * zero_part, SUBLANES), zero_part), :] = jnp.zeros(
            (zero_part, zero_ref.shape[1]), zero_ref.dtype)

    stage1_both(0)

    def pipelined(j, carry):
        fo = f2(j)
        bo = b2(j)
        store_zeros(j)
        stage1_both(j + 1)
        f3(*fo)
        b3(*bo)
        return carry

    lax.fori_loop(0, n_sub - 1, pipelined, 0)
    fo = f2(n_sub - 1)
    bo = b2(n_sub - 1)
    store_zeros(n_sub - 1)
    f3(*fo)
    b3(*bo)


def _hgrn(proj, lb, zero_rows, *, d_lru, d_hgrn, rows=512):
    bsz, seq, _ = proj.shape
    n_chunks = seq // rows
    zero_blk = zero_rows // (bsz * n_chunks)
    assert zero_blk * bsz * n_chunks == zero_rows and zero_blk % (SUBLANES * (rows // HGRN_CHUNK)) == 0
    col0 = (2 * d_lru) // d_hgrn
    n_pairs = d_hgrn // LANES
    n_lvl = (HGRN_CHUNK // HGRN_SUB).bit_length() - 1
    fwd = lambda col: pl.BlockSpec((1, rows, d_hgrn), lambda b, c: (b, c, col))
    bwd = lambda col: pl.BlockSpec((1, rows, d_hgrn), lambda b, c: (b, n_chunks - 1 - c, col))
    direction_scratch = [
        pltpu.VMEM((n_pairs, LANES, LANES), F32),
        pltpu.VMEM((n_pairs, HGRN_CHUNK, LANES), F32),
        pltpu.VMEM((n_pairs * n_lvl, HGRN_CHUNK, LANES), F32),
        pltpu.VMEM((n_pairs, LANES, LANES), F32),
        pltpu.VMEM((HGRN_CHUNK, d_hgrn), BF16),
        pltpu.VMEM((1, d_hgrn), F32),
    ]
    assert len(direction_scratch) == N_HGRN_SCRATCH
    kern = functools.partial(_hgrn_kernel, rows=rows)
    out = jax.ShapeDtypeStruct((bsz, seq, d_hgrn), F32)
    return pl.pallas_call(
        kern,
        out_shape=(out, out, jax.ShapeDtypeStruct((zero_rows, LANES), F32)),
        grid=(bsz, n_chunks),
        in_specs=[fwd(col0), fwd(col0 + 1), fwd(col0 + 3), bwd(col0), bwd(col0 + 2), bwd(col0 + 3),
                  pl.BlockSpec((1, d_hgrn), lambda b, c: (0, 0))],
        out_specs=(pl.BlockSpec((1, rows, d_hgrn), lambda b, c: (b, c, 0)),
                   pl.BlockSpec((1, rows, d_hgrn), lambda b, c: (b, n_chunks - 1 - c, 0)),
                   pl.BlockSpec((zero_blk, LANES), lambda b, c: (b * n_chunks + c, 0))),
        scratch_shapes=direction_scratch + direction_scratch,
        compiler_params=_params(("arbitrary", "arbitrary")),
        name="hgrn2",
    )(proj, proj, proj, proj, proj, proj, lb.reshape(1, d_hgrn))


def _interleave(*streams):
    done = object()
    live = list(streams)
    while live:
        live = [s for s in live if next(s, done) is not done]


def _gelu_tanh(y):
    return 0.5 * y * (1.0 + jnp.tanh(0.7978845608028654 * (y + 0.044715 * (y * y * y))))


def _post_kernel(lf_ref, lb_ref, y_ref, of_ref, ob_ref, g_ref, x_ref, nlw_ref, nhw_ref, wo_ref,
                 gm_ref, nfw_ref, scf_ref, shf_ref, wr_ref, br_ref,
                 xo_ref, h_ref, slab_ref, cnt_ref, route_ref, carry_ref, logits_s, *, tm, d_lru):
    step = pl.program_id(0)

    @pl.when(step == 0)
    def _():
        carry_ref[...] = jnp.zeros_like(carry_ref)
        logits_s[...] = jnp.zeros_like(logits_s)

    def mixer():
        lru = (lf_ref[0] + lb_ref[0]) * _gelu_tanh(y_ref[0])
        ms = jnp.mean(lru * lru, axis=-1, keepdims=True)
        yield
        lru = lru * lax.rsqrt(ms + NORM_EPS) * nlw_ref[...]

        hg = of_ref[0] + ob_ref[0]
        width = hg.shape[1]
        hd = width // HGRN_HEADS
        hd_shift = hd.bit_length() - 1
        er = lax.broadcasted_iota(jnp.int32, (width, width), 0) >> hd_shift
        ec = lax.broadcasted_iota(jnp.int32, (width, width), 1) >> hd_shift
        head_sum = jnp.where(er == ec, 1.0, 0.0).astype(BF16)
        sq = hg * hg
        sq_hi = sq.astype(BF16)
        sq_lo = (sq - sq_hi.astype(F32)).astype(BF16)
        ms_h = (_dot(sq_hi, head_sum) + _dot(sq_lo, head_sum)) * (1.0 / hd)
        yield
        g = g_ref[0]
        hg = (hg * lax.rsqrt(ms_h + NORM_EPS) * nhw_ref[...]) * (g * _sigmoid(g))
        yield
        mixed = _dot(lru.astype(BF16), wo_ref[0:d_lru, :])
        yield
        mixed = mixed + _dot(hg.astype(BF16), wo_ref[d_lru:, :])
        yield
        x_new = x_ref[0] + gm_ref[0] * mixed
        xo_ref[0] = x_new
        ms_f = jnp.mean(x_new * x_new, axis=-1, keepdims=True)
        yield
        h = (x_new * lax.rsqrt(ms_f + NORM_EPS) * nfw_ref[...]) * (1.0 + scf_ref[0]) + shf_ref[0]
        _tiles_store(h_ref, h, tm, lead=(0,))
        yield
        logits_s[...] = _dot(h.astype(BF16), wr_ref[...]) + br_ref[...]

    def routing():
        yield from _routing_steps(logits_s[...], jnp.where(step > 0, 1.0, 0.0), slab_ref, cnt_ref, route_ref,
                                  carry_ref, tm)

    _interleave(routing(), mixer())


def _routing_steps(logits, live, slab_ref, cnt_ref, route_ref, carry_ref, tm):
    lane = lax.broadcasted_iota(jnp.int32, (tm, ROUTE_LANES), 1)
    lane_f = lane.astype(F32)
    far = float(ROUTE_LANES)
    is_g = lane < N_GROUPS
    gl = jnp.where(is_g, logits, NEG_BIG)
    gmax = jnp.max(gl, axis=-1, keepdims=True)
    yield
    g_idx = jnp.min(jnp.where(gl == gmax, lane_f, far), axis=-1, keepdims=True)
    p_group = 1.0 / jnp.sum(jnp.where(is_g, jnp.exp(gl - gmax), 0.0), axis=-1, keepdims=True)
    yield
    e_lane = lane - N_GROUPS
    in_group = (e_lane >= 0) & (e_lane < N_EXPERTS) & ((e_lane >> (EXPERTS_PER_GROUP.bit_length() - 1)).astype(F32) == g_idx)
    ev = jnp.where(in_group, logits, NEG_BIG)
    top1 = jnp.max(ev, axis=-1, keepdims=True)
    yield
    i1 = jnp.min(jnp.where(in_group & (ev == top1), lane_f, far), axis=-1, keepdims=True)
    yield
    rest = in_group & (lane_f != i1)
    ev2 = jnp.where(rest, logits, NEG_BIG)
    top2 = jnp.max(ev2, axis=-1, keepdims=True)
    yield
    i2 = jnp.min(jnp.where(rest & (ev2 == top2), lane_f, far), axis=-1, keepdims=True)
    yield
    e1 = i1 - float(N_GROUPS)
    e2 = i2 - float(N_GROUPS)
    ex = jnp.exp(top2 - top1)
    w1 = p_group / (1.0 + ex)
    w2 = p_group * ex / (1.0 + ex)

    sel1 = lane_f == e1
    sel2 = lane_f == e2
    onehot = jnp.where(sel1 | sel2, live, 0.0)
    tr = lax.broadcasted_iota(jnp.int32, (tm, tm), 0)
    tc = lax.broadcasted_iota(jnp.int32, (tm, tm), 1)
    before = jnp.where(tc < tr, 1.0, 0.0).astype(BF16)
    cnt = _dot(before, onehot.astype(BF16)) + carry_ref[0:1]
    yield
    rank1 = jnp.sum(jnp.where(sel1, cnt, 0.0), axis=-1, keepdims=True)
    rank2 = jnp.sum(jnp.where(sel2, cnt, 0.0), axis=-1, keepdims=True)
    total = carry_ref[0:1] + jnp.sum(onehot, axis=0, keepdims=True)
    carry_ref[...] = jnp.broadcast_to(total, carry_ref.shape)
    cnt_ref[...] = jnp.broadcast_to(total, cnt_ref.shape)
    yield

    slab = jnp.where(lane == 0, e1, 0.0)
    slab = jnp.where(lane == 1, e2, slab)
    slab = jnp.where(lane == 2, w1, slab)
    slab = jnp.where(lane == 3, w2, slab)
    slab = jnp.where(lane == 4, rank1, slab)
    slab = jnp.where(lane == 5, rank2, slab)
    slab_ref[0] = slab
    route_ref[...] = slab.T[0:SUBLANES]


def _post_mixer(lru_f, lru_b, proj, hg_f, hg_b, x, nlw, nhw, wo_bf16, g_mix, nfw, sc_ffn, sh_ffn, wr_bf16, br,
                *, tm=512):
    bsz, seq, d = x.shape
    d_lru = lru_f.shape[-1]
    d_hgrn = hg_f.shape[-1]
    y_col = 1
    g_col = (2 * d_lru) // d_hgrn + 4
    tiles = seq // tm
    n_tiles = bsz * tiles
    cur = lambda s: jnp.minimum(s, n_tiles - 1)
    prev = lambda s: jnp.maximum(s - 1, 0)
    row = lambda w, col=0: pl.BlockSpec((1, tm, w), lambda s: (cur(s) // tiles, cur(s) % tiles, col))
    vec = lambda w: pl.BlockSpec((1, w), lambda s: (0, 0))
    per_b = lambda: pl.BlockSpec((1, 1, d), lambda s: (cur(s) // tiles, 0, 0))
    kern = functools.partial(_post_kernel, tm=tm, d_lru=d_lru)
    return pl.pallas_call(
        kern,
        out_shape=(
            jax.ShapeDtypeStruct((bsz, seq, d), F32),
            jax.ShapeDtypeStruct((bsz, seq * SUBLANES, LANES), F32),
            jax.ShapeDtypeStruct((bsz, seq, ROUTE_LANES), F32),
            jax.ShapeDtypeStruct((SUBLANES, ROUTE_LANES), F32),
            jax.ShapeDtypeStruct((SUBLANES, bsz * seq), F32),
        ),
        grid=(n_tiles + 1,),
        in_specs=[
            row(d_lru), row(d_lru), row(d_lru, y_col),
            row(d_hgrn), row(d_hgrn), row(d_hgrn, g_col),
            row(d), vec(d_lru), vec(d_hgrn),
            pl.BlockSpec((d, d), lambda s: (0, 0)),
            per_b(), vec(d), per_b(), per_b(),
            pl.BlockSpec((d, ROUTE_LANES), lambda s: (0, 0)),
            vec(ROUTE_LANES),
        ],
        out_specs=(
            row(d),
            pl.BlockSpec((1, tm * SUBLANES, LANES), lambda s: (cur(s) // tiles, cur(s) % tiles, 0)),
            pl.BlockSpec((1, tm, ROUTE_LANES), lambda s: (prev(s) // tiles, prev(s) % tiles, 0)),
            pl.BlockSpec((SUBLANES, ROUTE_LANES), lambda s: (0, 0)),
            pl.BlockSpec((SUBLANES, tm), lambda s: (0, prev(s))),
        ),
        scratch_shapes=[pltpu.VMEM((SUBLANES, ROUTE_LANES), F32), pltpu.VMEM((tm, ROUTE_LANES), F32)],
        compiler_params=_params(("arbitrary",)),
        name="post_mixer_router",
    )(lru_f, lru_b, proj, hg_f, hg_b, proj, x, nlw.reshape(1, d_lru), nhw.reshape(1, d_hgrn), wo_bf16,
      g_mix, nfw.reshape(1, d), sc_ffn, sh_ffn, wr_bf16, br)


def _tiles_load(ref, n, lead=()):
    return jnp.concatenate(
        [ref[(*lead, pl.ds(j, n, stride=SUBLANES), slice(None))] for j in range(SUBLANES)], axis=1)


def _tiles_store(ref, val, n, lead=()):
    for j in range(SUBLANES):
        ref[(*lead, pl.ds(j, n, stride=SUBLANES), slice(None))] = val[:, j * LANES:(j + 1) * LANES]


def _token_tile(ref, t):
    return ref.at[pl.ds(pl.multiple_of(t * SUBLANES, SUBLANES), SUBLANES)]


def _dest_kernel(start_ref, route_ref, o_ref):
    route = route_ref[...].astype(jnp.int32)
    start = jnp.zeros_like(route)
    for e in range(N_EXPERTS):
        start = jnp.where(route == e, start_ref[e], start)
    o_ref[...] = start + pltpu.roll(route, SUBLANES // 2, 0)


def _dest_rows(expert_start, route):
    return pl.pallas_call(
        _dest_kernel,
        out_shape=jax.ShapeDtypeStruct(route.shape, jnp.int32),
        grid_spec=pltpu.PrefetchScalarGridSpec(
            num_scalar_prefetch=1,
            grid=(1,),
            in_specs=[pl.BlockSpec(route.shape, lambda i, s: (0, 0))],
            out_specs=pl.BlockSpec(route.shape, lambda i, s: (0, 0)),
        ),
        compiler_params=pltpu.CompilerParams(dimension_semantics=("arbitrary",)),
        name="moe_dest_rows",
    )(expert_start, route)


def _dispatch_kernel(d1_ref, d2_ref, h_ref, z_ref, o_ref, sem, *, tb):
    del z_ref
    base = pl.program_id(0) * tb

    def issue(r, carry):
        t = base + r
        pltpu.make_async_copy(_token_tile(h_ref, r), _token_tile(o_ref, d1_ref[t]), sem).start(priority=0)
        pltpu.make_async_copy(_token_tile(h_ref, r), _token_tile(o_ref, d2_ref[t]), sem).start(priority=1)
        return carry

    lax.fori_loop(0, tb, issue, 0, unroll=DMA_ISSUE_UNROLL)
    for _ in range(2):
        pltpu.make_async_copy(h_ref, o_ref.at[pl.ds(0, tb * SUBLANES)], sem).wait()


def _dispatch(dest1, dest2, h_tiles, zero_tiles, *, tb=512):
    m = h_tiles.shape[0] // SUBLANES
    n_rows = zero_tiles.shape[0] // SUBLANES
    kern = functools.partial(_dispatch_kernel, tb=tb)
    return pl.pallas_call(
        kern,
        out_shape=jax.ShapeDtypeStruct((n_rows * SUBLANES, LANES), h_tiles.dtype),
        grid_spec=pltpu.PrefetchScalarGridSpec(
            num_scalar_prefetch=2,
            grid=(m // tb,),
            in_specs=[pl.BlockSpec((tb * SUBLANES, LANES), lambda i, d1, d2: (i, 0)),
                      pl.BlockSpec(memory_space=pl.ANY)],
            out_specs=pl.BlockSpec(memory_space=pl.ANY),
            scratch_shapes=[pltpu.SemaphoreType.DMA(())],
        ),
        input_output_aliases={3: 0},
        compiler_params=pltpu.CompilerParams(dimension_semantics=("arbitrary",), has_side_effects=True),
        name="moe_dispatch",
    )(dest1, dest2, h_tiles, zero_tiles)


def _expert_kernel(plan_ref, x_ref, wg_hbm, wu_hbm, wd_hbm, o_ref, wg_f, wu_f, wd_f, wg_s, wu_s, wd_s, sem,
                   *, layer):
    i = pl.program_id(0)
    n_blocks = pl.num_programs(0)
    n_used = plan_ref[n_blocks]
    expert = plan_ref[i]
    next_expert = plan_ref[n_blocks + 1 + i]
    slot = plan_ref[2 * n_blocks + 1 + i]
    first_block = ((i == 0) | (plan_ref[jnp.maximum(i - 1, 0)] != expert)) & (i < n_used)

    def copies(e, s):
        return (pltpu.make_async_copy(wg_hbm.at[layer, e], wg_f.at[s], sem.at[s]),
                pltpu.make_async_copy(wu_hbm.at[layer, e], wu_f.at[s], sem.at[s]),
                pltpu.make_async_copy(wd_hbm.at[layer, e], wd_f.at[s], sem.at[s]))

    @pl.when(i == 0)
    def _():
        for c in copies(expert, slot):
            c.start()

    @pl.when(first_block)
    def _():
        for c in copies(expert, slot):
            c.wait()
        wg_s[...] = wg_f[slot].astype(BF16)
        wu_s[...] = wu_f[slot].astype(BF16)
        wd_s[...] = wd_f[slot].astype(BF16)

        @pl.when(next_expert >= 0)
        def _():
            for c in copies(next_expert, 1 - slot):
                c.start()

    blk = x_ref.shape[0] // SUBLANES

    @pl.when(i < n_used)
    def _():
        x = _tiles_load(x_ref, blk).astype(BF16)
        gate = _dot(x, wg_s[...])
        up = _dot(x, wu_s[...])
        act = (gate * _sigmoid(gate)) * up
        _tiles_store(o_ref, _dot(act.astype(BF16), wd_s[...]), blk)

    @pl.when(i >= n_used)
    def _():
        o_ref[...] = jnp.zeros_like(o_ref)


def _expert_plan(blk_expert, blocks_used):
    n_blocks = blk_expert.shape[0]
    idx = jnp.arange(n_blocks, dtype=jnp.int32)
    change = jnp.concatenate([jnp.ones((1,), bool), blk_expert[1:] != blk_expert[:-1]])
    slot = (jnp.cumsum(change.astype(jnp.int32)) - 1) & 1
    change_at = jnp.where(change, idx, n_blocks)
    from_here = lax.cummin(change_at[::-1])[::-1]
    next_change = jnp.concatenate([from_here[1:], jnp.full((1,), n_blocks, jnp.int32)])
    next_expert = jnp.where(next_change < blocks_used, blk_expert[jnp.minimum(next_change, n_blocks - 1)], -1)
    return jnp.concatenate([blk_expert, blocks_used.reshape(1), next_expert, slot]).astype(jnp.int32)


def _experts(plan, x_tiles, wg, wu, wd, layer):
    n_rows = x_tiles.shape[0] // SUBLANES
    d, de = wg.shape[-2:]
    blk = EXPERT_BLOCK
    n_blocks = n_rows // blk
    kern = functools.partial(_expert_kernel, layer=layer)
    return pl.pallas_call(
        kern,
        out_shape=jax.ShapeDtypeStruct((n_rows * SUBLANES, LANES), F32),
        grid_spec=pltpu.PrefetchScalarGridSpec(
            num_scalar_prefetch=1,
            grid=(n_blocks,),
            in_specs=[
                pl.BlockSpec((blk * SUBLANES, LANES), lambda i, plan: (jnp.minimum(i, plan[n_blocks] - 1), 0)),
                pl.BlockSpec(memory_space=pl.ANY),
                pl.BlockSpec(memory_space=pl.ANY),
                pl.BlockSpec(memory_space=pl.ANY),
            ],
            out_specs=pl.BlockSpec((blk * SUBLANES, LANES), lambda i, plan: (i, 0)),
            scratch_shapes=[
                pltpu.VMEM((2, d, de), F32), pltpu.VMEM((2, d, de), F32), pltpu.VMEM((2, de, d), F32),
                pltpu.VMEM((d, de), BF16), pltpu.VMEM((d, de), BF16), pltpu.VMEM((de, d), BF16),
                pltpu.SemaphoreType.DMA((2,)),
            ],
        ),
        compiler_params=_params(("arbitrary",)),
        name="moe_experts",
    )(plan, x_tiles, wg, wu, wd)


def _combine_kernel(d1_ref, d2_ref, y_ref, slab_ref, x_ref, g_ref, nw_ref, o_ref, ra0, rb0, ra1, rb1, sem,
                    *, tm, tiles, n_steps, final_norm):
    step = pl.program_id(0) * tiles + pl.program_id(1)
    bufs = ((ra0, rb0), (ra1, rb1))

    def gather(tile, slot):
        base = tile * tm
        r1_ref, r2_ref = bufs[slot]

        def issue(r, carry):
            t = base + r
            pltpu.make_async_copy(_token_tile(y_ref, d1_ref[t]), _token_tile(r1_ref, r),
                                  sem.at[slot]).start(priority=0)
            pltpu.make_async_copy(_token_tile(y_ref, d2_ref[t]), _token_tile(r2_ref, r),
                                  sem.at[slot]).start(priority=1)
            return carry

        lax.fori_loop(0, tm, issue, 0, unroll=DMA_ISSUE_UNROLL)

    @pl.when(step == 0)
    def _():
        gather(0, 0)

    for slot in range(2):
        @pl.when((step & 1) == slot)
        def _():
            @pl.when(step + 1 < n_steps)
            def _():
                gather(step + 1, 1 - slot)

            r1_ref, r2_ref = bufs[slot]
            pltpu.make_async_copy(y_ref.at[pl.ds(0, tm * SUBLANES)], r1_ref, sem.at[slot]).wait()
            pltpu.make_async_copy(y_ref.at[pl.ds(0, tm * SUBLANES)], r2_ref, sem.at[slot]).wait()
            slab = slab_ref[0]
            y = slab[:, 2:3] * _tiles_load(r1_ref, tm) + slab[:, 3:4] * _tiles_load(r2_ref, tm)
            out = x_ref[0] + g_ref[0] * y
            if final_norm:
                ms = jnp.mean(out * out, axis=-1, keepdims=True)
                out = out * lax.rsqrt(ms + NORM_EPS) * nw_ref[...]
            o_ref[0] = out


def _combine(dest1, dest2, y_buf, slab, x, g_ffn, norm_w, *, final_norm, tm=512):
    bsz, seq, d = x.shape
    tiles = seq // tm
    kern = functools.partial(_combine_kernel, tm=tm, tiles=tiles, n_steps=bsz * tiles, final_norm=final_norm)
    row_buf = pltpu.VMEM((tm * SUBLANES, LANES), F32)
    return pl.pallas_call(
        kern,
        out_shape=jax.ShapeDtypeStruct((bsz, seq, d), F32),
        grid_spec=pltpu.PrefetchScalarGridSpec(
            num_scalar_prefetch=2,
            grid=(bsz, tiles),
            in_specs=[
                pl.BlockSpec(memory_space=pl.ANY),
                pl.BlockSpec((1, tm, ROUTE_LANES), lambda b, i, d1, d2: (b, i, 0)),
                pl.BlockSpec((1, tm, d), lambda b, i, d1, d2: (b, i, 0)),
                pl.BlockSpec((1, 1, d), lambda b, i, d1, d2: (b, 0, 0)),
                pl.BlockSpec((1, d), lambda b, i, d1, d2: (0, 0)),
            ],
            out_specs=pl.BlockSpec((1, tm, d), lambda b, i, d1, d2: (b, i, 0)),
            scratch_shapes=[row_buf, row_buf, row_buf, row_buf, pltpu.SemaphoreType.DMA((2,))],
        ),
        compiler_params=_params(("arbitrary", "arbitrary")),
        name="moe_combine",
    )(dest1, dest2, y_buf, slab, x, g_ffn, norm_w.reshape(1, d))


def _block_diag(w):
    heads, hd, _ = w.shape
    n = heads * hd
    tiled = jnp.tile(w.reshape(n, hd), (1, heads))
    blk_r = lax.broadcasted_iota(jnp.int32, (n, n), 0) // hd
    blk_c = lax.broadcasted_iota(jnp.int32, (n, n), 1) // hd
    return jnp.where(blk_r == blk_c, tiled, 0.0)


def kernel(x, c, ada_w, ada_b, norm_mix_w, w_in, conv_w, conv_b, lru_wa, lru_ba, lru_wx, lru_bx, lru_lambda, norm_lru_w, hgrn_lb, norm_hgrn_w, w_out, norm_ffn_w, router_group_w, router_group_b, router_expert_w, router_expert_b, expert_w_gate, expert_w_up, expert_w_down, final_norm_w):
    bsz, seq, d = x.shape
    assert d == SUBLANES * LANES, "the MoE row movement keeps one (8, 128) tile per token"
    depth = ada_w.shape[0]
    d_lru = conv_w.shape[-1]
    d_hgrn = hgrn_lb.shape[-1]
    m = bsz * seq
    n_rows = m * 2 + N_EXPERTS * EXPERT_BLOCK
    n_blocks = n_rows // EXPERT_BLOCK

    mod = _modulation(c, ada_w, ada_b)
    lb_cum = jnp.cumsum(jax.nn.softmax(hgrn_lb.astype(F32), axis=0), axis=0)
    lb_all = lb_cum - lb_cum[0:1]

    for l in range(depth):
        sh_mix, sc_mix, g_mix, sh_ffn, sc_ffn, g_ffn = [
            mod[l, :, i * d:(i + 1) * d].reshape(bsz, 1, d) for i in range(6)]
        proj = _in_proj(x, norm_mix_w[l], sc_mix, sh_mix, w_in, l)
        wa_bd = jnp.stack([_block_diag(lru_wa[l, 0]), _block_diag(lru_wa[l, 1])]).astype(BF16)
        wx_bd = jnp.stack([_block_diag(lru_wx[l, 0]), _block_diag(lru_wx[l, 1])]).astype(BF16)
        lru = [
            _lru_scan(proj, conv_w[l], conv_b[l], wa_bd, lru_ba[l], wx_bd, lru_bx[l], lru_lambda[l],
                      reverse=rv)
            for rv in (False, True)]
        hg_f, hg_b, zero_tiles = _hgrn(proj, lb_all[l], n_rows * SUBLANES, d_lru=d_lru, d_hgrn=d_hgrn)

        lane_pad = ROUTE_LANES - N_GROUPS - N_EXPERTS
        wr = jnp.pad(jnp.concatenate([router_group_w[l], router_expert_w[l]], axis=1), ((0, 0), (0, lane_pad)))
        br = jnp.pad(jnp.concatenate([router_group_b[l], router_expert_b[l]]), (0, lane_pad)).reshape(1, ROUTE_LANES)
        x_mid, h_ffn, slab, counts, route = _post_mixer(
            lru[0], lru[1], proj, hg_f, hg_b, x, norm_lru_w[l], norm_hgrn_w[l], w_out[l].astype(BF16), g_mix,
            norm_ffn_w[l], sc_ffn, sh_ffn, wr.astype(BF16), br)

        cnt = counts[0, :N_EXPERTS].astype(jnp.int32)
        padded = ((cnt + EXPERT_BLOCK - 1) // EXPERT_BLOCK) * EXPERT_BLOCK
        pend = jnp.cumsum(padded)
        pstart = pend - padded
        blk_start = jnp.arange(n_blocks, dtype=jnp.int32) * EXPERT_BLOCK
        blk_expert = jnp.minimum(jnp.sum(pend[None, :] <= blk_start[:, None], axis=1), N_EXPERTS - 1)
        plan = _expert_plan(blk_expert.astype(jnp.int32), (pend[N_EXPERTS - 1] // EXPERT_BLOCK).astype(jnp.int32))
        dest = _dest_rows(pstart.astype(jnp.int32), route)
        dest1, dest2 = dest[0], dest[1]

        x_buf = _dispatch(dest1, dest2, h_ffn.reshape(m * SUBLANES, LANES), zero_tiles)
        y_buf = _experts(plan, x_buf, expert_w_gate, expert_w_up, expert_w_down, l)
        x = _combine(dest1, dest2, y_buf, slab, x_mid, g_ffn, final_norm_w, final_norm=(l == depth - 1))

    return x
```

```python
import functools

import jax
import jax.numpy as jnp
from jax import lax
from jax.experimental import pallas as pl
from jax.experimental.pallas import tpu as pltpu

F32 = jnp.float32
BF16 = jnp.bfloat16

LRU_HEADS = 8
HGRN_HEADS = 8
CONV_WIDTH = 4
LRU_C = 8.0
N_GROUPS = 4
EXPERTS_PER_GROUP = 8
N_EXPERTS = N_GROUPS * EXPERTS_PER_GROUP
NORM_EPS = 1e-6

LANES = 128
SUBLANES = 8
VMEM_LIMIT = 56 * 1024 * 1024

HGRN_CHUNK = 64
HGRN_SUB = 8
LOG2E = 1.4426950408889634
ROUTE_LANES = LANES
EXPERT_BLOCK = 512
DMA_ISSUE_UNROLL = 8
COMBINE_PIECES = 8
NEG_BIG = -3.0e38


def _params(sem):
    return pltpu.CompilerParams(dimension_semantics=sem, vmem_limit_bytes=VMEM_LIMIT)


def _dot(a, b):
    return jnp.dot(a, b, preferred_element_type=F32)


def _dot_nt(a, b):
    return lax.dot_general(a, b, (((1,), (1,)), ((), ())), preferred_element_type=F32)


def _dot_tn(a, b):
    return lax.dot_general(a, b, (((0,), (0,)), ((), ())), preferred_element_type=F32)


def _dot01_exact(m01, x):
    hi = x.astype(BF16)
    r1 = x - hi.astype(F32)
    mid = r1.astype(BF16)
    lo = (r1 - mid.astype(F32)).astype(BF16)
    return _dot(m01, hi) + _dot(m01, mid) + _dot(m01, lo)


def _sigmoid(x):
    return 1.0 / (1.0 + jnp.exp(-x))


def _mod_kernel(c_ref, w_ref, b_ref, o_ref):
    c = c_ref[...]
    cond = c * _sigmoid(c)
    o_ref[0] = _dot(cond.astype(BF16), w_ref[0].astype(BF16)) + b_ref[0]


def _modulation(c, ada_w, ada_b):
    depth, d, n = ada_w.shape
    bsz = c.shape[0]
    rows = -(-bsz // SUBLANES) * SUBLANES
    c_pad = jnp.pad(c, ((0, rows - bsz), (0, 0)))
    tn = n // 6
    out = pl.pallas_call(
        _mod_kernel,
        out_shape=jax.ShapeDtypeStruct((depth, rows, n), F32),
        grid=(depth, n // tn),
        in_specs=[
            pl.BlockSpec((rows, d), lambda l, j: (0, 0)),
            pl.BlockSpec((1, d, tn), lambda l, j: (l, 0, j)),
            pl.BlockSpec((1, 1, tn), lambda l, j: (l, 0, j)),
        ],
        out_specs=pl.BlockSpec((1, rows, tn), lambda l, j: (l, 0, j)),
        compiler_params=_params(("arbitrary", "arbitrary")),
        name="adaln_mod",
    )(c_pad, ada_w, ada_b.reshape(depth, 1, n))
    return out[:, :bsz]


def _rms_mod(x, nw, sc, sh):
    ms = jnp.mean(x * x, axis=-1, keepdims=True)
    return (x * lax.rsqrt(ms + NORM_EPS) * nw) * (1.0 + sc) + sh


def _inproj_kernel(x_ref, nw_ref, sc_ref, sh_ref, w_ref, o_ref, w_s):
    @pl.when((pl.program_id(0) == 0) & (pl.program_id(1) == 0))
    def _():
        w_s[...] = w_ref[0].astype(BF16)

    h = _rms_mod(x_ref[0], nw_ref[...], sc_ref[0], sh_ref[0])
    o_ref[0] = _dot(h.astype(BF16), w_s[...])


def _in_proj(x, nw, sc, sh, w_in, layer, tm=512):
    bsz, seq, d = x.shape
    n = w_in.shape[-1]
    return pl.pallas_call(
        _inproj_kernel,
        out_shape=jax.ShapeDtypeStruct((bsz, seq, n), F32),
        grid=(bsz, seq // tm),
        in_specs=[
            pl.BlockSpec((1, tm, d), lambda b, i: (b, i, 0)),
            pl.BlockSpec((1, d), lambda b, i: (0, 0)),
            pl.BlockSpec((1, 1, d), lambda b, i: (b, 0, 0)),
            pl.BlockSpec((1, 1, d), lambda b, i: (b, 0, 0)),
            pl.BlockSpec((1, d, n), lambda b, i: (layer, 0, 0), pipeline_mode=pl.Buffered(1)),
        ],
        out_specs=pl.BlockSpec((1, tm, n), lambda b, i: (b, i, 0)),
        scratch_shapes=[pltpu.VMEM((d, n), BF16)],
        compiler_params=_params(("arbitrary", "arbitrary")),
        name="in_proj",
    )(x, nw.reshape(1, d), sc, sh, w_in)


def _lru_kernel(x_ref, xp_ref, xn_ref, cw_ref, cb_ref, wa_ref, ba_ref, wx_ref, bx_ref, lam_ref,
                o_ref, carry_ref, sa_ref, sb_ref, cin_ref, *, reverse, n_chunks, rows):
    c = pl.program_id(1)
    chunk = (n_chunks - 1 - c) if reverse else c

    @pl.when(c == 0)
    def _():
        carry_ref[...] = jnp.zeros_like(carry_ref)

    x = x_ref[0]
    width = x.shape[1]
    row = lax.broadcasted_iota(jnp.int32, (rows, width), 0)
    has_prev = jnp.where(chunk > 0, 1.0, 0.0)
    has_next = jnp.where(chunk < n_chunks - 1, 1.0, 0.0)
    xp = xp_ref[0] * has_prev
    xn = xn_ref[0] * has_next
    xe = jnp.concatenate([xp, x, xn], axis=0)
    h8 = SUBLANES
    cw = cw_ref[...]
    xc = (cw[0:1] * xe[h8 - 2:h8 - 2 + rows] + cw[1:2] * xe[h8 - 1:h8 - 1 + rows] + cw[2:3] * x
          + cw[3:4] * xe[h8 + 1:h8 + 1 + rows] + cb_ref[...])

    xcb = xc.astype(BF16)
    r = _sigmoid(_dot(xcb, wa_ref[0]) + ba_ref[0])
    gate_i = _sigmoid(_dot(xcb, wx_ref[0]) + bx_ref[0])
    lam = lam_ref[0]
    softplus_neg_lam = jnp.maximum(-lam, 0.0) + jnp.log1p(jnp.exp(-jnp.abs(lam)))
    log_a = (-LRU_C) * r * softplus_neg_lam
    a = jnp.exp(log_a)
    t = jnp.tanh(-log_a)
    u = jnp.sqrt(2.0 * t / (1.0 + t)) * (gate_i * xc)

    groups = rows // SUBLANES
    acc_a = a.reshape(groups, SUBLANES, width)
    acc_b = u.reshape(groups, SUBLANES, width)
    sub = lax.broadcasted_iota(jnp.int32, (groups, SUBLANES, width), 1)
    s = 1
    while s < SUBLANES:
        if reverse:
            valid = sub < SUBLANES - s
            sh_a, sh_b = pltpu.roll(acc_a, SUBLANES - s, 1), pltpu.roll(acc_b, SUBLANES - s, 1)
        else:
            valid = sub >= s
            sh_a, sh_b = pltpu.roll(acc_a, s, 1), pltpu.roll(acc_b, s, 1)
        acc_b = jnp.where(valid, acc_a * sh_b + acc_b, acc_b)
        acc_a = jnp.where(valid, acc_a * sh_a, acc_a)
        s *= 2
    acc_a = acc_a.reshape(rows, width)
    acc_b = acc_b.reshape(rows, width)
    groups = rows // SUBLANES
    edge = 0 if reverse else SUBLANES - 1
    n_tiles = width // LANES
    for j in range(n_tiles):
        sa_ref[j] = acc_a[:, j * LANES:(j + 1) * LANES]
        sb_ref[j] = acc_b[:, j * LANES:(j + 1) * LANES]
    ea = jnp.concatenate([sa_ref[j, pl.ds(edge, groups, stride=SUBLANES), :] for j in range(n_tiles)], axis=1)
    eb = jnp.concatenate([sb_ref[j, pl.ds(edge, groups, stride=SUBLANES), :] for j in range(n_tiles)], axis=1)
    grow = lax.broadcasted_iota(jnp.int32, (groups, width), 0)
    s = 1
    while s < groups:
        if reverse:
            valid = grow < groups - s
            sh_a, sh_b = pltpu.roll(ea, groups - s, 0), pltpu.roll(eb, groups - s, 0)
        else:
            valid = grow >= s
            sh_a, sh_b = pltpu.roll(ea, s, 0), pltpu.roll(eb, s, 0)
        eb = jnp.where(valid, ea * sh_b + eb, eb)
        ea = jnp.where(valid, ea * sh_a, ea)
        s *= 2
    carry = carry_ref[...]
    group_out = eb + ea * carry
    if reverse:
        carry_in = jnp.where(grow == groups - 1, carry, pltpu.roll(group_out, groups - 1, 0))
        carry_ref[...] = group_out[0:1]
    else:
        carry_in = jnp.where(grow == 0, carry, pltpu.roll(group_out, 1, 0))
        carry_ref[...] = group_out[groups - 1:groups]
    cin_ref[...] = carry_in
    for g in range(groups):
        rs = slice(g * SUBLANES, (g + 1) * SUBLANES)
        o_ref[0, rs, :] = acc_b[rs] + acc_a[rs] * cin_ref[g:g + 1, :]


def _lru_scan(proj, conv_w, conv_b, wa_bd, ba, wx_bd, bx, lam, *, reverse, rows=512):
    bsz, seq, _ = proj.shape
    d_lru = conv_w.shape[1]
    n_chunks = seq // rows
    halo = rows // SUBLANES
    last_halo = seq // SUBLANES - 1
    dirn = 1 if reverse else 0

    def chunk_of(c):
        return (n_chunks - 1 - c) if reverse else c

    vec = lambda: pl.BlockSpec((1, 1, d_lru), lambda b, c: (dirn, 0, 0))
    mat = lambda: pl.BlockSpec((1, d_lru, d_lru), lambda b, c: (dirn, 0, 0))
    kern = functools.partial(_lru_kernel, reverse=reverse, n_chunks=n_chunks, rows=rows)
    return pl.pallas_call(
        kern,
        out_shape=jax.ShapeDtypeStruct((bsz, seq, d_lru), F32),
        grid=(bsz, n_chunks),
        in_specs=[
            pl.BlockSpec((1, rows, d_lru), lambda b, c: (b, chunk_of(c), 0)),
            pl.BlockSpec((1, SUBLANES, d_lru),
                         lambda b, c: (b, jnp.maximum(chunk_of(c) * halo - 1, 0), 0)),
            pl.BlockSpec((1, SUBLANES, d_lru),
                         lambda b, c: (b, jnp.minimum((chunk_of(c) + 1) * halo, last_halo), 0)),
            pl.BlockSpec((CONV_WIDTH, d_lru), lambda b, c: (0, 0)),
            pl.BlockSpec((1, d_lru), lambda b, c: (0, 0)),
            mat(), vec(), mat(), vec(), vec(),
        ],
        out_specs=pl.BlockSpec((1, rows, d_lru), lambda b, c: (b, chunk_of(c), 0)),
        scratch_shapes=[pltpu.VMEM((1, d_lru), F32), pltpu.VMEM((d_lru // LANES, rows, LANES), F32),
                        pltpu.VMEM((d_lru // LANES, rows, LANES), F32),
                        pltpu.VMEM((rows // SUBLANES, d_lru), F32)],
        compiler_params=_params(("arbitrary", "arbitrary")),
        name="lru_bwd" if reverse else "lru_fwd",
    )(proj, proj, proj, conv_w, conv_b.reshape(1, d_lru), wa_bd, ba.reshape(2, 1, d_lru),
      wx_bd, bx.reshape(2, 1, d_lru), lam.reshape(2, 1, d_lru))


def _hgrn_direction(rev, q_ref, f_ref, v_ref, lb_ref, o_ref, st_ref, diag_s, lvl_s, upd_s, qe_s, btot_s, *, rows):
    ck, sb = HGRN_CHUNK, HGRN_SUB
    n_blk = ck // sb
    sb_shift = sb.bit_length() - 1
    n_sub = rows // ck
    width = q_ref.shape[-1]
    n_pairs = width // LANES
    half = LANES // 2

    def flip(idx, n):
        return (n - 1 - idx) if rev else idx

    n_lvl = n_blk.bit_length() - 1
    tf = flip(lax.broadcasted_iota(jnp.int32, (ck, ck), 0), ck)
    uf = flip(lax.broadcasted_iota(jnp.int32, (ck, ck), 1), ck)
    tb, ub = tf >> sb_shift, uf >> sb_shift
    pb = flip(lax.broadcasted_iota(jnp.int32, (n_blk, ck), 0), n_blk)
    pub = flip(lax.broadcasted_iota(jnp.int32, (n_blk, ck), 1), ck) >> sb_shift
    mats = [jnp.where((tb == ub) & (uf <= tf), 1.0, 0.0),
            jnp.where(pub < pb, 1.0, 0.0)]
    for lvl in range(n_lvl):
        mid = ((pb >> (lvl + 1)) << (lvl + 1)) + (1 << lvl)
        mats.append(jnp.where(pub < mid, 1.0, 0.0))
    mats.append(jnp.ones((SUBLANES, ck), F32))
    m_cum = jnp.concatenate(mats, axis=0).astype(BF16)

    def per_block(rows8):
        return jnp.concatenate(
            [jnp.broadcast_to(rows8[jb:jb + 1], (sb, rows8.shape[1])) for jb in range(n_blk)], axis=0)
    row_blk = flip(lax.broadcasted_iota(jnp.int32, (ck, width), 0), ck) >> sb_shift
    upper = [((row_blk >> lvl) & 1) == 1 for lvl in range(n_lvl)]
    pr = flip(lax.broadcasted_iota(jnp.int32, (ck, LANES), 0), ck) >> sb_shift
    pc = flip(lax.broadcasted_iota(jnp.int32, (ck, LANES), 1) & (ck - 1), ck) >> sb_shift
    group_mask = [(pr >> (lvl + 1)) == (pc >> (lvl + 1)) for lvl in range(n_lvl)]
    lane = lax.broadcasted_iota(jnp.int32, (1, LANES), 1)
    head0 = lane < half

    def split_heads(x):
        xb = x.astype(BF16)
        zero = jnp.zeros_like(xb)
        return jnp.concatenate([jnp.where(head0, xb, zero), jnp.where(head0, zero, xb)], axis=0)

    sr = lax.broadcasted_iota(jnp.int32, (LANES, LANES), 0)
    sc = lax.broadcasted_iota(jnp.int32, (LANES, LANES), 1)
    same_head = (sr < half) == (sc < half)
    er = lax.broadcasted_iota(jnp.int32, (sb * LANES, LANES), 0)
    ec = lax.broadcasted_iota(jnp.int32, (sb * LANES, LANES), 1)
    sel = jnp.where(ec == (((er & (LANES - 1)) >> (half.bit_length() - 1)) * half + (er >> (LANES.bit_length() - 1))),
                    1.0, 0.0).astype(BF16)
    sub_row = flip(lax.broadcasted_iota(jnp.int32, (sb, LANES), 0), sb)
    lbv = lb_ref[...]

    def row_start(j):
        return pl.multiple_of(flip(j, n_sub) * ck, ck)

    def stage1a(j):
        r0 = row_start(j)
        q = q_ref[0, pl.ds(r0, ck), :]
        z = f_ref[0, pl.ds(r0, ck), :]
        v = v_ref[0, pl.ds(r0, ck), :]
        f = lbv + (1.0 - lbv) * _sigmoid(z)
        lf2 = jnp.log(f) * LOG2E
        k = 1.0 - f
        return q, v, k, _dot01_exact(m_cum, lf2)

    def stage1b(q, v, k, cums):
        bl2 = cums[0:ck]
        b2 = bl2 + per_block(cums[ck:ck + n_blk])
        tot_row = ck + (1 + n_lvl) * n_blk
        btot2 = cums[tot_row:tot_row + 1]
        log2_k = jnp.log(k) * LOG2E
        kb = b2 - log2_k
        kbl = bl2 - log2_k
        qe = q * jnp.exp2(b2)
        ke = jnp.exp2(btot2 - kb)
        q_lvl, k_lvl = [], []
        for lvl in range(n_lvl):
            split2 = per_block(cums[ck + (1 + lvl) * n_blk:ck + (2 + lvl) * n_blk])
            q_lvl.append(q * jnp.exp2(jnp.where(upper[lvl], b2 - split2, NEG_BIG)))
            k_lvl.append(jnp.exp2(jnp.where(upper[lvl], NEG_BIG, split2 - kb)))

        qe_s[...] = qe.astype(BF16)
        btot_s[...] = btot2
        for p in range(n_pairs):
            sl = slice(p * LANES, (p + 1) * LANES)
            diag_rows = []
            for jb in range(n_blk):
                rs = slice(jb * sb, (jb + 1) * sb)
                bl_b, kbl_b, q_b = bl2[rs, sl], kbl[rs, sl], q[rs, sl]
                terms = []
                for s in range(sb):
                    arg = jnp.where(sub_row >= flip(s, sb), bl_b - kbl_b[s:s + 1], NEG_BIG)
                    terms.append(q_b * jnp.exp2(arg))
                diag_rows.append(jnp.concatenate(terms, axis=1))
            diag_s[p] = _dot(jnp.concatenate(diag_rows, axis=0).astype(BF16), sel)
            for lvl in range(n_lvl):
                k_p = k_lvl[lvl][:, sl]
                lvl_s[p * n_lvl + lvl] = _dot_nt(q_lvl[lvl][:, sl].astype(BF16), split_heads(k_p))
            upd_s[p] = _dot_tn(v[:, sl].astype(BF16), ke[:, sl].astype(BF16))

    def stage2_issue(j):
        r0 = row_start(j)
        v = v_ref[0, pl.ds(r0, ck), :]
        out = []
        for p in range(n_pairs):
            sl = slice(p * LANES, (p + 1) * LANES)
            parts = []
            for jb in range(n_blk):
                blk = diag_s[p, jb * sb:(jb + 1) * sb, :]
                parts.append(pltpu.roll(blk, jb * sb, 1) if jb else blk)
            scores = jnp.concatenate(parts, axis=0)
            for lvl in range(n_lvl):
                s_lvl = lvl_s[p * n_lvl + lvl]
                scores = scores + (s_lvl if lvl == n_lvl - 1 else jnp.where(group_mask[lvl], s_lvl, 0.0))
            intra = _dot(scores.astype(BF16), split_heads(v[:, sl]))
            st = st_ref[p]
            inter = _dot_nt(qe_s[:, sl], st.astype(BF16))
            new_st = jnp.where(same_head, st * jnp.exp2(btot_s[:, sl]) + upd_s[p], 0.0)
            out.append((inter + intra, new_st))
        return r0, out

    def stage2_finish(r0, out):
        for p in range(n_pairs):
            o_ref[0, pl.ds(r0, ck), p * LANES:(p + 1) * LANES] = out[p][0]
            st_ref[p] = out[p][1]

    return stage1a, stage1b, stage2_issue, stage2_finish


N_HGRN_SCRATCH = 6


def _hgrn_kernel(qf_ref, ff_ref, vf_ref, qb_ref, fb_ref, vb_ref, lb_ref, of_ref, ob_ref, zero_ref, *scratch,
                 rows):
    fwd_scratch, bwd_scratch = scratch[:N_HGRN_SCRATCH], scratch[N_HGRN_SCRATCH:]

    @pl.when(pl.program_id(1) == 0)
    def _():
        fwd_scratch[0][...] = jnp.zeros_like(fwd_scratch[0])
        bwd_scratch[0][...] = jnp.zeros_like(bwd_scratch[0])

    f1a, f1b, f2, f3 = _hgrn_direction(False, qf_ref, ff_ref, vf_ref, lb_ref, of_ref, *fwd_scratch, rows=rows)
    b1a, b1b, b2, b3 = _hgrn_direction(True, qb_ref, fb_ref, vb_ref, lb_ref, ob_ref, *bwd_scratch, rows=rows)
    n_sub = rows // HGRN_CHUNK

    def stage1_both(j):
        fa = f1a(j)
        ba = b1a(j)
        f1b(*fa)
        b1b(*ba)

    zero_part = zero_ref.shape[0] // n_sub

    def store_zeros(j):
        zero_ref[pl.ds(pl.multiple_of(j * zero_part, SUBLANES), zero_part), :] = jnp.zeros(
            (zero_part, zero_ref.shape[1]), zero_ref.dtype)

    stage1_both(0)

    def pipelined(j, carry):
        fo = f2(j)
        bo = b2(j)
        store_zeros(j)
        stage1_both(j + 1)
        f3(*fo)
        b3(*bo)
        return carry

    lax.fori_loop(0, n_sub - 1, pipelined, 0)
    fo = f2(n_sub - 1)
    bo = b2(n_sub - 1)
    store_zeros(n_sub - 1)
    f3(*fo)
    b3(*bo)


def _hgrn(proj, lb, zero_rows, *, d_lru, d_hgrn, rows=512):
    bsz, seq, _ = proj.shape
    n_chunks = seq // rows
    zero_blk = zero_rows // (bsz * n_chunks)
    assert zero_blk * bsz * n_chunks == zero_rows and zero_blk % (SUBLANES * (rows // HGRN_CHUNK)) == 0
    col0 = (2 * d_lru) // d_hgrn
    n_pairs = d_hgrn // LANES
    n_lvl = (HGRN_CHUNK // HGRN_SUB).bit_length() - 1
    fwd = lambda col: pl.BlockSpec((1, rows, d_hgrn), lambda b, c: (b, c, col))
    bwd = lambda col: pl.BlockSpec((1, rows, d_hgrn), lambda b, c: (b, n_chunks - 1 - c, col))
    direction_scratch = [
        pltpu.VMEM((n_pairs, LANES, LANES), F32),
        pltpu.VMEM((n_pairs, HGRN_CHUNK, LANES), F32),
        pltpu.VMEM((n_pairs * n_lvl, HGRN_CHUNK, LANES), F32),
        pltpu.VMEM((n_pairs, LANES, LANES), F32),
        pltpu.VMEM((HGRN_CHUNK, d_hgrn), BF16),
        pltpu.VMEM((1, d_hgrn), F32),
    ]
    assert len(direction_scratch) == N_HGRN_SCRATCH
    kern = functools.partial(_hgrn_kernel, rows=rows)
    out = jax.ShapeDtypeStruct((bsz, seq, d_hgrn), F32)
    return pl.pallas_call(
        kern,
        out_shape=(out, out, jax.ShapeDtypeStruct((zero_rows, LANES), F32)),
        grid=(bsz, n_chunks),
        in_specs=[fwd(col0), fwd(col0 + 1), fwd(col0 + 3), bwd(col0), bwd(col0 + 2), bwd(col0 + 3),
                  pl.BlockSpec((1, d_hgrn), lambda b, c: (0, 0))],
        out_specs=(pl.BlockSpec((1, rows, d_hgrn), lambda b, c: (b, c, 0)),
                   pl.BlockSpec((1, rows, d_hgrn), lambda b, c: (b, n_chunks - 1 - c, 0)),
                   pl.BlockSpec((zero_blk, LANES), lambda b, c: (b * n_chunks + c, 0))),
        scratch_shapes=direction_scratch + direction_scratch,
        compiler_params=_params(("arbitrary", "arbitrary")),
        name="hgrn2",
    )(proj, proj, proj, proj, proj, proj, lb.reshape(1, d_hgrn))


def _interleave(*streams):
    done = object()
    live = list(streams)
    while live:
        live = [s for s in live if next(s, done) is not done]


def _gelu_tanh(y):
    return 0.5 * y * (1.0 + jnp.tanh(0.7978845608028654 * (y + 0.044715 * (y * y * y))))


def _post_kernel(lf_ref, lb_ref, y_ref, of_ref, ob_ref, g_ref, x_ref, nlw_ref, nhw_ref, wo_ref,
                 gm_ref, nfw_ref, scf_ref, shf_ref, wr_ref, br_ref,
                 xo_ref, h_ref, slab_ref, cnt_ref, route_ref, carry_ref, logits_s, *, tm, d_lru):
    step = pl.program_id(0)

    @pl.when(step == 0)
    def _():
        carry_ref[...] = jnp.zeros_like(carry_ref)
        logits_s[...] = jnp.zeros_like(logits_s)

    def mixer():
        lru = (lf_ref[0] + lb_ref[0]) * _gelu_tanh(y_ref[0])
        ms = jnp.mean(lru * lru, axis=-1, keepdims=True)
        yield
        lru = lru * lax.rsqrt(ms + NORM_EPS) * nlw_ref[...]

        hg = of_ref[0] + ob_ref[0]
        width = hg.shape[1]
        hd = width // HGRN_HEADS
        hd_shift = hd.bit_length() - 1
        er = lax.broadcasted_iota(jnp.int32, (width, width), 0) >> hd_shift
        ec = lax.broadcasted_iota(jnp.int32, (width, width), 1) >> hd_shift
        head_sum = jnp.where(er == ec, 1.0, 0.0).astype(BF16)
        sq = hg * hg
        sq_hi = sq.astype(BF16)
        sq_lo = (sq - sq_hi.astype(F32)).astype(BF16)
        ms_h = (_dot(sq_hi, head_sum) + _dot(sq_lo, head_sum)) * (1.0 / hd)
        yield
        g = g_ref[0]
        hg = (hg * lax.rsqrt(ms_h + NORM_EPS) * nhw_ref[...]) * (g * _sigmoid(g))
        yield
        mixed = _dot(lru.astype(BF16), wo_ref[0:d_lru, :])
        yield
        mixed = mixed + _dot(hg.astype(BF16), wo_ref[d_lru:, :])
        yield
        x_new = x_ref[0] + gm_ref[0] * mixed
        xo_ref[0] = x_new
        ms_f = jnp.mean(x_new * x_new, axis=-1, keepdims=True)
        yield
        h = (x_new * lax.rsqrt(ms_f + NORM_EPS) * nfw_ref[...]) * (1.0 + scf_ref[0]) + shf_ref[0]
        _tiles_store(h_ref, h, tm, lead=(0,))
        yield
        logits_s[...] = _dot(h.astype(BF16), wr_ref[...]) + br_ref[...]

    def routing():
        yield from _routing_steps(logits_s[...], jnp.where(step > 0, 1.0, 0.0), slab_ref, cnt_ref, route_ref,
                                  carry_ref, tm)

    _interleave(routing(), mixer())


def _routing_steps(logits, live, slab_ref, cnt_ref, route_ref, carry_ref, tm):
    lane = lax.broadcasted_iota(jnp.int32, (tm, ROUTE_LANES), 1)
    lane_f = lane.astype(F32)
    far = float(ROUTE_LANES)
    is_g = lane < N_GROUPS
    gl = jnp.where(is_g, logits, NEG_BIG)
    gmax = jnp.max(gl, axis=-1, keepdims=True)
    yield
    g_idx = jnp.min(jnp.where(gl == gmax, lane_f, far), axis=-1, keepdims=True)
    p_group = 1.0 / jnp.sum(jnp.where(is_g, jnp.exp(gl - gmax), 0.0), axis=-1, keepdims=True)
    yield
    e_lane = lane - N_GROUPS
    in_group = (e_lane >= 0) & (e_lane < N_EXPERTS) & ((e_lane >> (EXPERTS_PER_GROUP.bit_length() - 1)).astype(F32) == g_idx)
    ev = jnp.where(in_group, logits, NEG_BIG)
    top1 = jnp.max(ev, axis=-1, keepdims=True)
    yield
    i1 = jnp.min(jnp.where(in_group & (ev == top1), lane_f, far), axis=-1, keepdims=True)
    yield
    rest = in_group & (lane_f != i1)
    ev2 = jnp.where(rest, logits, NEG_BIG)
    top2 = jnp.max(ev2, axis=-1, keepdims=True)
    yield
    i2 = jnp.min(jnp.where(rest & (ev2 == top2), lane_f, far), axis=-1, keepdims=True)
    yield
    e1 = i1 - float(N_GROUPS)
    e2 = i2 - float(N_GROUPS)
    ex = jnp.exp(top2 - top1)
    w1 = p_group / (1.0 + ex)
    w2 = p_group * ex / (1.0 + ex)

    sel1 = lane_f == e1
    sel2 = lane_f == e2
    onehot = jnp.where(sel1 | sel2, live, 0.0)
    tr = lax.broadcasted_iota(jnp.int32, (tm, tm), 0)
    tc = lax.broadcasted_iota(jnp.int32, (tm, tm), 1)
    before = jnp.where(tc < tr, 1.0, 0.0).astype(BF16)
    cnt = _dot(before, onehot.astype(BF16)) + carry_ref[0:1]
    yield
    rank1 = jnp.sum(jnp.where(sel1, cnt, 0.0), axis=-1, keepdims=True)
    rank2 = jnp.sum(jnp.where(sel2, cnt, 0.0), axis=-1, keepdims=True)
    total = carry_ref[0:1] + jnp.sum(onehot, axis=0, keepdims=True)
    carry_ref[...] = jnp.broadcast_to(total, carry_ref.shape)
    cnt_ref[...] = jnp.broadcast_to(total, cnt_ref.shape)
    yield

    slab = jnp.where(lane == 0, e1, 0.0)
    slab = jnp.where(lane == 1, e2, slab)
    slab = jnp.where(lane == 2, w1, slab)
    slab = jnp.where(lane == 3, w2, slab)
    slab = jnp.where(lane == 4, rank1, slab)
    slab = jnp.where(lane == 5, rank2, slab)
    slab_ref[0] = slab
    route_ref[...] = slab.T[0:SUBLANES]


def _post_mixer(lru_f, lru_b, proj, hg_f, hg_b, x, nlw, nhw, wo_bf16, g_mix, nfw, sc_ffn, sh_ffn, wr_bf16, br,
                *, tm=512):
    bsz, seq, d = x.shape
    d_lru = lru_f.shape[-1]
    d_hgrn = hg_f.shape[-1]
    y_col = 1
    g_col = (2 * d_lru) // d_hgrn + 4
    tiles = seq // tm
    n_tiles = bsz * tiles
    cur = lambda s: jnp.minimum(s, n_tiles - 1)
    prev = lambda s: jnp.maximum(s - 1, 0)
    row = lambda w, col=0: pl.BlockSpec((1, tm, w), lambda s: (cur(s) // tiles, cur(s) % tiles, col))
    vec = lambda w: pl.BlockSpec((1, w), lambda s: (0, 0))
    per_b = lambda: pl.BlockSpec((1, 1, d), lambda s: (cur(s) // tiles, 0, 0))
    kern = functools.partial(_post_kernel, tm=tm, d_lru=d_lru)
    return pl.pallas_call(
        kern,
        out_shape=(
            jax.ShapeDtypeStruct((bsz, seq, d), F32),
            jax.ShapeDtypeStruct((bsz, seq * SUBLANES, LANES), F32),
            jax.ShapeDtypeStruct((bsz, seq, ROUTE_LANES), F32),
            jax.ShapeDtypeStruct((SUBLANES, ROUTE_LANES), F32),
            jax.ShapeDtypeStruct((SUBLANES, bsz * seq), F32),
        ),
        grid=(n_tiles + 1,),
        in_specs=[
            row(d_lru), row(d_lru), row(d_lru, y_col),
            row(d_hgrn), row(d_hgrn), row(d_hgrn, g_col),
            row(d), vec(d_lru), vec(d_hgrn),
            pl.BlockSpec((d, d), lambda s: (0, 0)),
            per_b(), vec(d), per_b(), per_b(),
            pl.BlockSpec((d, ROUTE_LANES), lambda s: (0, 0)),
            vec(ROUTE_LANES),
        ],
        out_specs=(
            row(d),
            pl.BlockSpec((1, tm * SUBLANES, LANES), lambda s: (cur(s) // tiles, cur(s) % tiles, 0)),
            pl.BlockSpec((1, tm, ROUTE_LANES), lambda s: (prev(s) // tiles, prev(s) % tiles, 0)),
            pl.BlockSpec((SUBLANES, ROUTE_LANES), lambda s: (0, 0)),
            pl.BlockSpec((SUBLANES, tm), lambda s: (0, prev(s))),
        ),
        scratch_shapes=[pltpu.VMEM((SUBLANES, ROUTE_LANES), F32), pltpu.VMEM((tm, ROUTE_LANES), F32)],
        compiler_params=_params(("arbitrary",)),
        name="post_mixer_router",
    )(lru_f, lru_b, proj, hg_f, hg_b, proj, x, nlw.reshape(1, d_lru), nhw.reshape(1, d_hgrn), wo_bf16,
      g_mix, nfw.reshape(1, d), sc_ffn, sh_ffn, wr_bf16, br)


def _tiles_load(ref, n, lead=()):
    return jnp.concatenate(
        [ref[(*lead, pl.ds(j, n, stride=SUBLANES), slice(None))] for j in range(SUBLANES)], axis=1)


def _tiles_store(ref, val, n, lead=()):
    for j in range(SUBLANES):
        ref[(*lead, pl.ds(j, n, stride=SUBLANES), slice(None))] = val[:, j * LANES:(j + 1) * LANES]


def _token_tile(ref, t):
    return ref.at[pl.ds(pl.multiple_of(t * SUBLANES, SUBLANES), SUBLANES)]


def _dest_kernel(start_ref, route_ref, o_ref):
    route = route_ref[...].astype(jnp.int32)
    start = jnp.zeros_like(route)
    for e in range(N_EXPERTS):
        start = jnp.where(route == e, start_ref[e], start)
    o_ref[...] = start + pltpu.roll(route, SUBLANES // 2, 0)


def _dest_rows(expert_start, route):
    return pl.pallas_call(
        _dest_kernel,
        out_shape=jax.ShapeDtypeStruct(route.shape, jnp.int32),
        grid_spec=pltpu.PrefetchScalarGridSpec(
            num_scalar_prefetch=1,
            grid=(1,),
            in_specs=[pl.BlockSpec(route.shape, lambda i, s: (0, 0))],
            out_specs=pl.BlockSpec(route.shape, lambda i, s: (0, 0)),
        ),
        compiler_params=pltpu.CompilerParams(dimension_semantics=("arbitrary",)),
        name="moe_dest_rows",
    )(expert_start, route)


def _dispatch_kernel(d1_ref, d2_ref, h_ref, z_ref, o_ref, sem, *, tb):
    del z_ref
    base = pl.program_id(0) * tb

    def issue(r, carry):
        t = base + r
        pltpu.make_async_copy(_token_tile(h_ref, r), _token_tile(o_ref, d1_ref[t]), sem).start(priority=0)
        pltpu.make_async_copy(_token_tile(h_ref, r), _token_tile(o_ref, d2_ref[t]), sem).start(priority=1)
        return carry

    lax.fori_loop(0, tb, issue, 0, unroll=DMA_ISSUE_UNROLL)
    for _ in range(2):
        pltpu.make_async_copy(h_ref, o_ref.at[pl.ds(0, tb * SUBLANES)], sem).wait()


def _dispatch(dest1, dest2, h_tiles, zero_tiles, *, tb=512):
    m = h_tiles.shape[0] // SUBLANES
    n_rows = zero_tiles.shape[0] // SUBLANES
    kern = functools.partial(_dispatch_kernel, tb=tb)
    return pl.pallas_call(
        kern,
        out_shape=jax.ShapeDtypeStruct((n_rows * SUBLANES, LANES), h_tiles.dtype),
        grid_spec=pltpu.PrefetchScalarGridSpec(
            num_scalar_prefetch=2,
            grid=(m // tb,),
            in_specs=[pl.BlockSpec((tb * SUBLANES, LANES), lambda i, d1, d2: (i, 0)),
                      pl.BlockSpec(memory_space=pl.ANY)],
            out_specs=pl.BlockSpec(memory_space=pl.ANY),
            scratch_shapes=[pltpu.SemaphoreType.DMA(())],
        ),
        input_output_aliases={3: 0},
        compiler_params=pltpu.CompilerParams(dimension_semantics=("arbitrary",), has_side_effects=True),
        name="moe_dispatch",
    )(dest1, dest2, h_tiles, zero_tiles)


def _expert_kernel(plan_ref, x_ref, wg_hbm, wu_hbm, wd_hbm, o_ref, wg_f, wu_f, wd_f, wg_s, wu_s, wd_s, sem,
                   *, layer):
    i = pl.program_id(0)
    n_blocks = pl.num_programs(0)
    n_used = plan_ref[n_blocks]
    expert = plan_ref[i]
    next_expert = plan_ref[n_blocks + 1 + i]
    slot = plan_ref[2 * n_blocks + 1 + i]
    first_block = ((i == 0) | (plan_ref[jnp.maximum(i - 1, 0)] != expert)) & (i < n_used)

    def copies(e, s):
        return (pltpu.make_async_copy(wg_hbm.at[layer, e], wg_f.at[s], sem.at[s]),
                pltpu.make_async_copy(wu_hbm.at[layer, e], wu_f.at[s], sem.at[s]),
                pltpu.make_async_copy(wd_hbm.at[layer, e], wd_f.at[s], sem.at[s]))

    @pl.when(i == 0)
    def _():
        for c in copies(expert, slot):
            c.start()

    @pl.when(first_block)
    def _():
        for c in copies(expert, slot):
            c.wait()
        wg_s[...] = wg_f[slot].astype(BF16)
        wu_s[...] = wu_f[slot].astype(BF16)
        wd_s[...] = wd_f[slot].astype(BF16)

        @pl.when(next_expert >= 0)
        def _():
            for c in copies(next_expert, 1 - slot):
                c.start()

    blk = x_ref.shape[0] // SUBLANES

    @pl.when(i < n_used)
    def _():
        x = _tiles_load(x_ref, blk).astype(BF16)
        gate = _dot(x, wg_s[...])
        up = _dot(x, wu_s[...])
        act = (gate * _sigmoid(gate)) * up
        _tiles_store(o_ref, _dot(act.astype(BF16), wd_s[...]), blk)

    @pl.when(i >= n_used)
    def _():
        o_ref[...] = jnp.zeros_like(o_ref)


def _expert_plan(blk_expert, blocks_used):
    n_blocks = blk_expert.shape[0]
    idx = jnp.arange(n_blocks, dtype=jnp.int32)
    change = jnp.concatenate([jnp.ones((1,), bool), blk_expert[1:] != blk_expert[:-1]])
    slot = (jnp.cumsum(change.astype(jnp.int32)) - 1) & 1
    change_at = jnp.where(change, idx, n_blocks)
    from_here = lax.cummin(change_at[::-1])[::-1]
    next_change = jnp.concatenate([from_here[1:], jnp.full((1,), n_blocks, jnp.int32)])
    next_expert = jnp.where(next_change < blocks_used, blk_expert[jnp.minimum(next_change, n_blocks - 1)], -1)
    return jnp.concatenate([blk_expert, blocks_used.reshape(1), next_expert, slot]).astype(jnp.int32)


def _experts(plan, x_tiles, wg, wu, wd, layer):
    n_rows = x_tiles.shape[0] // SUBLANES
    d, de = wg.shape[-2:]
    blk = EXPERT_BLOCK
    n_blocks = n_rows // blk
    kern = functools.partial(_expert_kernel, layer=layer)
    return pl.pallas_call(
        kern,
        out_shape=jax.ShapeDtypeStruct((n_rows * SUBLANES, LANES), F32),
        grid_spec=pltpu.PrefetchScalarGridSpec(
            num_scalar_prefetch=1,
            grid=(n_blocks,),
            in_specs=[
                pl.BlockSpec((blk * SUBLANES, LANES), lambda i, plan: (jnp.minimum(i, plan[n_blocks] - 1), 0)),
                pl.BlockSpec(memory_space=pl.ANY),
                pl.BlockSpec(memory_space=pl.ANY),
                pl.BlockSpec(memory_space=pl.ANY),
            ],
            out_specs=pl.BlockSpec((blk * SUBLANES, LANES), lambda i, plan: (i, 0)),
            scratch_shapes=[
                pltpu.VMEM((2, d, de), F32), pltpu.VMEM((2, d, de), F32), pltpu.VMEM((2, de, d), F32),
                pltpu.VMEM((d, de), BF16), pltpu.VMEM((d, de), BF16), pltpu.VMEM((de, d), BF16),
                pltpu.SemaphoreType.DMA((2,)),
            ],
        ),
        compiler_params=_params(("arbitrary",)),
        name="moe_experts",
    )(plan, x_tiles, wg, wu, wd)


def _combine_kernel(d1_ref, d2_ref, y_ref, slab_ref, x_ref, g_ref, nw_ref, o_ref, ra0, rb0, ra1, rb1, sem,
                    *, tm, tiles, n_steps, final_norm):
    step = pl.program_id(0) * tiles + pl.program_id(1)
    bufs = ((ra0, rb0), (ra1, rb1))

    def gather(tile, slot):
        base = tile * tm
        r1_ref, r2_ref = bufs[slot]

        def issue(r, carry):
            t = base + r
            pltpu.make_async_copy(_token_tile(y_ref, d1_ref[t]), _token_tile(r1_ref, r),
                                  sem.at[slot]).start(priority=0)
            pltpu.make_async_copy(_token_tile(y_ref, d2_ref[t]), _token_tile(r2_ref, r),
                                  sem.at[slot]).start(priority=1)
            return carry

        lax.fori_loop(0, tm, issue, 0, unroll=DMA_ISSUE_UNROLL)

    def wait(slot):
        r1_ref, r2_ref = bufs[slot]
        pltpu.make_async_copy(y_ref.at[pl.ds(0, tm * SUBLANES)], r1_ref, sem.at[slot]).wait()
        pltpu.make_async_copy(y_ref.at[pl.ds(0, tm * SUBLANES)], r2_ref, sem.at[slot]).wait()

    def gather_pieces(tile, slot):
        base = tile * tm
        r1_ref, r2_ref = bufs[slot]
        for r in range(tm):
            t = base + r
            pltpu.make_async_copy(_token_tile(y_ref, d1_ref[t]), _token_tile(r1_ref, r),
                                  sem.at[slot]).start(priority=0)
            pltpu.make_async_copy(_token_tile(y_ref, d2_ref[t]), _token_tile(r2_ref, r),
                                  sem.at[slot]).start(priority=1)
            if r % (tm // COMBINE_PIECES) == tm // COMBINE_PIECES - 1:
                yield

    def combine_pieces(slot):
        r1_ref, r2_ref = bufs[slot]
        slab = slab_ref[0]
        w1, w2, gate = slab[:, 2:3], slab[:, 3:4], g_ref[0]
        cols = []
        for j in range(SUBLANES):
            cs = slice(j * LANES, (j + 1) * LANES)
            y = (w1 * r1_ref[pl.ds(j, tm, stride=SUBLANES), :] + w2 * r2_ref[pl.ds(j, tm, stride=SUBLANES), :])
            cols.append(x_ref[0, :, cs] + gate[:, cs] * y)
            yield
        out = jnp.concatenate(cols, axis=1)
        if final_norm:
            ms = jnp.mean(out * out, axis=-1, keepdims=True)
            out = out * lax.rsqrt(ms + NORM_EPS) * nw_ref[...]
        o_ref[0] = out

    @pl.when(step == 0)
    def _():
        gather(0, 0)

    nxt = jnp.minimum(step + 1, n_steps - 1)
    for slot in range(2):
        @pl.when((step & 1) == slot)
        def _():
            wait(slot)
            _interleave(gather_pieces(nxt, 1 - slot), combine_pieces(slot))

            @pl.when(step == n_steps - 1)
            def _():
                wait(1 - slot)


def _combine(dest1, dest2, y_buf, slab, x, g_ffn, norm_w, *, final_norm, tm=512):
    bsz, seq, d = x.shape
    tiles = seq // tm
    kern = functools.partial(_combine_kernel, tm=tm, tiles=tiles, n_steps=bsz * tiles, final_norm=final_norm)
    row_buf = pltpu.VMEM((tm * SUBLANES, LANES), F32)
    return pl.pallas_call(
        kern,
        out_shape=jax.ShapeDtypeStruct((bsz, seq, d), F32),
        grid_spec=pltpu.PrefetchScalarGridSpec(
            num_scalar_prefetch=2,
            grid=(bsz, tiles),
            in_specs=[
                pl.BlockSpec(memory_space=pl.ANY),
                pl.BlockSpec((1, tm, ROUTE_LANES), lambda b, i, d1, d2: (b, i, 0)),
                pl.BlockSpec((1, tm, d), lambda b, i, d1, d2: (b, i, 0)),
                pl.BlockSpec((1, 1, d), lambda b, i, d1, d2: (b, 0, 0)),
                pl.BlockSpec((1, d), lambda b, i, d1, d2: (0, 0)),
            ],
            out_specs=pl.BlockSpec((1, tm, d), lambda b, i, d1, d2: (b, i, 0)),
            scratch_shapes=[row_buf, row_buf, row_buf, row_buf, pltpu.SemaphoreType.DMA((2,))],
        ),
        compiler_params=_params(("arbitrary", "arbitrary")),
        name="moe_combine",
    )(dest1, dest2, y_buf, slab, x, g_ffn, norm_w.reshape(1, d))


def _block_diag(w):
    heads, hd, _ = w.shape
    n = heads * hd
    tiled = jnp.tile(w.reshape(n, hd), (1, heads))
    blk_r = lax.broadcasted_iota(jnp.int32, (n, n), 0) // hd
    blk_c = lax.broadcasted_iota(jnp.int32, (n, n), 1) // hd
    return jnp.where(blk_r == blk_c, tiled, 0.0)


def kernel(x, c, ada_w, ada_b, norm_mix_w, w_in, conv_w, conv_b, lru_wa, lru_ba, lru_wx, lru_bx, lru_lambda, norm_lru_w, hgrn_lb, norm_hgrn_w, w_out, norm_ffn_w, router_group_w, router_group_b, router_expert_w, router_expert_b, expert_w_gate, expert_w_up, expert_w_down, final_norm_w):
    bsz, seq, d = x.shape
    assert d == SUBLANES * LANES, "the MoE row movement keeps one (8, 128) tile per token"
    depth = ada_w.shape[0]
    d_lru = conv_w.shape[-1]
    d_hgrn = hgrn_lb.shape[-1]
    m = bsz * seq
    n_rows = m * 2 + N_EXPERTS * EXPERT_BLOCK
    n_blocks = n_rows // EXPERT_BLOCK

    mod = _modulation(c, ada_w, ada_b)
    lb_cum = jnp.cumsum(jax.nn.softmax(hgrn_lb.astype(F32), axis=0), axis=0)
    lb_all = lb_cum - lb_cum[0:1]

    for l in range(depth):
        sh_mix, sc_mix, g_mix, sh_ffn, sc_ffn, g_ffn = [
            mod[l, :, i * d:(i + 1) * d].reshape(bsz, 1, d) for i in range(6)]
        proj = _in_proj(x, norm_mix_w[l], sc_mix, sh_mix, w_in, l)
        wa_bd = jnp.stack([_block_diag(lru_wa[l, 0]), _block_diag(lru_wa[l, 1])]).astype(BF16)
        wx_bd = jnp.stack([_block_diag(lru_wx[l, 0]), _block_diag(lru_wx[l, 1])]).astype(BF16)
        lru = [
            _lru_scan(proj, conv_w[l], conv_b[l], wa_bd, lru_ba[l], wx_bd, lru_bx[l], lru_lambda[l],
                      reverse=rv)
            for rv in (False, True)]
        hg_f, hg_b, zero_tiles = _hgrn(proj, lb_all[l], n_rows * SUBLANES, d_lru=d_lru, d_hgrn=d_hgrn)

        lane_pad = ROUTE_LANES - N_GROUPS - N_EXPERTS
        wr = jnp.pad(jnp.concatenate([router_group_w[l], router_expert_w[l]], axis=1), ((0, 0), (0, lane_pad)))
        br = jnp.pad(jnp.concatenate([router_group_b[l], router_expert_b[l]]), (0, lane_pad)).reshape(1, ROUTE_LANES)
        x_mid, h_ffn, slab, counts, route = _post_mixer(
            lru[0], lru[1], proj, hg_f, hg_b, x, norm_lru_w[l], norm_hgrn_w[l], w_out[l].astype(BF16), g_mix,
            norm_ffn_w[l], sc_ffn, sh_ffn, wr.astype(BF16), br)

        cnt = counts[0, :N_EXPERTS].astype(jnp.int32)
        padded = ((cnt + EXPERT_BLOCK - 1) // EXPERT_BLOCK) * EXPERT_BLOCK
        pend = jnp.cumsum(padded)
        pstart = pend - padded
        blk_start = jnp.arange(n_blocks, dtype=jnp.int32) * EXPERT_BLOCK
        blk_expert = jnp.minimum(jnp.sum(pend[None, :] <= blk_start[:, None], axis=1), N_EXPERTS - 1)
        plan = _expert_plan(blk_expert.astype(jnp.int32), (pend[N_EXPERTS - 1] // EXPERT_BLOCK).astype(jnp.int32))
        dest = _dest_rows(pstart.astype(jnp.int32), route)
        dest1, dest2 = dest[0], dest[1]

        x_buf = _dispatch(dest1, dest2, h_ffn.reshape(m * SUBLANES, LANES), zero_tiles)
        y_buf = _experts(plan, x_buf, expert_w_gate, expert_w_up, expert_w_down, l)
        x = _combine(dest1, dest2, y_buf, slab, x_mid, g_ffn, final_norm_w, final_norm=(l == depth - 1))

    return x
```

```python
import functools

import jax
import jax.numpy as jnp
from jax import lax
from jax.experimental import pallas as pl
from jax.experimental.pallas import tpu as pltpu

F32 = jnp.float32
BF16 = jnp.bfloat16

LRU_HEADS = 8
HGRN_HEADS = 8
CONV_WIDTH = 4
LRU_C = 8.0
N_GROUPS = 4
EXPERTS_PER_GROUP = 8
N_EXPERTS = N_GROUPS * EXPERTS_PER_GROUP
NORM_EPS = 1e-6

LANES = 128
SUBLANES = 8
VMEM_LIMIT = 56 * 1024 * 1024

HGRN_CHUNK = 64
HGRN_SUB = 8
LOG2E = 1.4426950408889634
ROUTE_LANES = LANES
EXPERT_BLOCK = 512
DMA_ISSUE_UNROLL = 8
NEG_BIG = -3.0e38


def _params(sem):
    return pltpu.CompilerParams(dimension_semantics=sem, vmem_limit_bytes=VMEM_LIMIT)


def _dot(a, b):
    return jnp.dot(a, b, preferred_element_type=F32)


def _dot_nt(a, b):
    return lax.dot_general(a, b, (((1,), (1,)), ((), ())), preferred_element_type=F32)


def _dot_tn(a, b):
    return lax.dot_general(a, b, (((0,), (0,)), ((), ())), preferred_element_type=F32)


def _dot01_exact(m01, x):
    hi = x.astype(BF16)
    r1 = x - hi.astype(F32)
    mid = r1.astype(BF16)
    lo = (r1 - mid.astype(F32)).astype(BF16)
    return _dot(m01, hi) + _dot(m01, mid) + _dot(m01, lo)


def _sigmoid(x):
    return 1.0 / (1.0 + jnp.exp(-x))


def _mod_kernel(c_ref, w_ref, b_ref, o_ref):
    c = c_ref[...]
    cond = c * _sigmoid(c)
    o_ref[0] = _dot(cond.astype(BF16), w_ref[0].astype(BF16)) + b_ref[0]


def _modulation(c, ada_w, ada_b):
    depth, d, n = ada_w.shape
    bsz = c.shape[0]
    rows = -(-bsz // SUBLANES) * SUBLANES
    c_pad = jnp.pad(c, ((0, rows - bsz), (0, 0)))
    tn = n // 6
    out = pl.pallas_call(
        _mod_kernel,
        out_shape=jax.ShapeDtypeStruct((depth, rows, n), F32),
        grid=(depth, n // tn),
        in_specs=[
            pl.BlockSpec((rows, d), lambda l, j: (0, 0)),
            pl.BlockSpec((1, d, tn), lambda l, j: (l, 0, j)),
            pl.BlockSpec((1, 1, tn), lambda l, j: (l, 0, j)),
        ],
        out_specs=pl.BlockSpec((1, rows, tn), lambda l, j: (l, 0, j)),
        compiler_params=_params(("arbitrary", "arbitrary")),
        name="adaln_mod",
    )(c_pad, ada_w, ada_b.reshape(depth, 1, n))
    return out[:, :bsz]


def _rms_mod(x, nw, sc, sh):
    ms = jnp.mean(x * x, axis=-1, keepdims=True)
    return (x * lax.rsqrt(ms + NORM_EPS) * nw) * (1.0 + sc) + sh


def _inproj_kernel(x_ref, nw_ref, sc_ref, sh_ref, w_ref, o_ref, w_s):
    @pl.when((pl.program_id(0) == 0) & (pl.program_id(1) == 0))
    def _():
        w_s[...] = w_ref[0].astype(BF16)

    h = _rms_mod(x_ref[0], nw_ref[...], sc_ref[0], sh_ref[0])
    o_ref[0] = _dot(h.astype(BF16), w_s[...])


def _in_proj(x, nw, sc, sh, w_in, layer, tm=512):
    bsz, seq, d = x.shape
    n = w_in.shape[-1]
    return pl.pallas_call(
        _inproj_kernel,
        out_shape=jax.ShapeDtypeStruct((bsz, seq, n), F32),
        grid=(bsz, seq // tm),
        in_specs=[
            pl.BlockSpec((1, tm, d), lambda b, i: (b, i, 0)),
            pl.BlockSpec((1, d), lambda b, i: (0, 0)),
            pl.BlockSpec((1, 1, d), lambda b, i: (b, 0, 0)),
            pl.BlockSpec((1, 1, d), lambda b, i: (b, 0, 0)),
            pl.BlockSpec((1, d, n), lambda b, i: (layer, 0, 0), pipeline_mode=pl.Buffered(1)),
        ],
        out_specs=pl.BlockSpec((1, tm, n), lambda b, i: (b, i, 0)),
        scratch_shapes=[pltpu.VMEM((d, n), BF16)],
        compiler_params=_params(("arbitrary", "arbitrary")),
        name="in_proj",
    )(x, nw.reshape(1, d), sc, sh, w_in)


def _lru_kernel(x_ref, xp_ref, xn_ref, cw_ref, cb_ref, wa_ref, ba_ref, wx_ref, bx_ref, lam_ref, *rest,
                reverse, n_chunks, rows, accumulate):
    add_ref = rest[0] if accumulate else None
    o_ref, carry_ref, sa_ref, sb_ref, cin_ref = rest[1:] if accumulate else rest
    c = pl.program_id(1)
    chunk = (n_chunks - 1 - c) if reverse else c

    @pl.when(c == 0)
    def _():
        carry_ref[...] = jnp.zeros_like(carry_ref)

    x = x_ref[0]
    width = x.shape[1]
    row = lax.broadcasted_iota(jnp.int32, (rows, width), 0)
    has_prev = jnp.where(chunk > 0, 1.0, 0.0)
    has_next = jnp.where(chunk < n_chunks - 1, 1.0, 0.0)
    xp = xp_ref[0] * has_prev
    xn = xn_ref[0] * has_next
    xe = jnp.concatenate([xp, x, xn], axis=0)
    h8 = SUBLANES
    cw = cw_ref[...]
    xc = (cw[0:1] * xe[h8 - 2:h8 - 2 + rows] + cw[1:2] * xe[h8 - 1:h8 - 1 + rows] + cw[2:3] * x
          + cw[3:4] * xe[h8 + 1:h8 + 1 + rows] + cb_ref[...])

    xcb = xc.astype(BF16)
    r = _sigmoid(_dot(xcb, wa_ref[0]) + ba_ref[0])
    gate_i = _sigmoid(_dot(xcb, wx_ref[0]) + bx_ref[0])
    lam = lam_ref[0]
    softplus_neg_lam = jnp.maximum(-lam, 0.0) + jnp.log1p(jnp.exp(-jnp.abs(lam)))
    log_a = (-LRU_C) * r * softplus_neg_lam
    a = jnp.exp(log_a)
    t = jnp.tanh(-log_a)
    u = jnp.sqrt(2.0 * t / (1.0 + t)) * (gate_i * xc)

    groups = rows // SUBLANES
    acc_a = a.reshape(groups, SUBLANES, width)
    acc_b = u.reshape(groups, SUBLANES, width)
    sub = lax.broadcasted_iota(jnp.int32, (groups, SUBLANES, width), 1)
    s = 1
    while s < SUBLANES:
        if reverse:
            valid = sub < SUBLANES - s
            sh_a, sh_b = pltpu.roll(acc_a, SUBLANES - s, 1), pltpu.roll(acc_b, SUBLANES - s, 1)
        else:
            valid = sub >= s
            sh_a, sh_b = pltpu.roll(acc_a, s, 1), pltpu.roll(acc_b, s, 1)
        acc_b = jnp.where(valid, acc_a * sh_b + acc_b, acc_b)
        acc_a = jnp.where(valid, acc_a * sh_a, acc_a)
        s *= 2
    acc_a = acc_a.reshape(rows, width)
    acc_b = acc_b.reshape(rows, width)
    groups = rows // SUBLANES
    edge = 0 if reverse else SUBLANES - 1
    n_tiles = width // LANES
    for j in range(n_tiles):
        sa_ref[j] = acc_a[:, j * LANES:(j + 1) * LANES]
        sb_ref[j] = acc_b[:, j * LANES:(j + 1) * LANES]
    ea = jnp.concatenate([sa_ref[j, pl.ds(edge, groups, stride=SUBLANES), :] for j in range(n_tiles)], axis=1)
    eb = jnp.concatenate([sb_ref[j, pl.ds(edge, groups, stride=SUBLANES), :] for j in range(n_tiles)], axis=1)
    grow = lax.broadcasted_iota(jnp.int32, (groups, width), 0)
    s = 1
    while s < groups:
        if reverse:
            valid = grow < groups - s
            sh_a, sh_b = pltpu.roll(ea, groups - s, 0), pltpu.roll(eb, groups - s, 0)
        else:
            valid = grow >= s
            sh_a, sh_b = pltpu.roll(ea, s, 0), pltpu.roll(eb, s, 0)
        eb = jnp.where(valid, ea * sh_b + eb, eb)
        ea = jnp.where(valid, ea * sh_a, ea)
        s *= 2
    carry = carry_ref[...]
    group_out = eb + ea * carry
    if reverse:
        carry_in = jnp.where(grow == groups - 1, carry, pltpu.roll(group_out, groups - 1, 0))
        carry_ref[...] = group_out[0:1]
    else:
        carry_in = jnp.where(grow == 0, carry, pltpu.roll(group_out, 1, 0))
        carry_ref[...] = group_out[groups - 1:groups]
    cin_ref[...] = carry_in
    for g in range(groups):
        rs = slice(g * SUBLANES, (g + 1) * SUBLANES)
        h_rows = acc_b[rs] + acc_a[rs] * cin_ref[g:g + 1, :]
        o_ref[0, rs, :] = (add_ref[0, rs, :] + h_rows) if accumulate else h_rows


def _lru_scan(proj, conv_w, conv_b, wa_bd, ba, wx_bd, bx, lam, *, reverse, add_to=None, rows=512):
    bsz, seq, _ = proj.shape
    d_lru = conv_w.shape[1]
    n_chunks = seq // rows
    halo = rows // SUBLANES
    last_halo = seq // SUBLANES - 1
    dirn = 1 if reverse else 0

    def chunk_of(c):
        return (n_chunks - 1 - c) if reverse else c

    vec = lambda: pl.BlockSpec((1, 1, d_lru), lambda b, c: (dirn, 0, 0))
    mat = lambda: pl.BlockSpec((1, d_lru, d_lru), lambda b, c: (dirn, 0, 0))
    accumulate = add_to is not None
    kern = functools.partial(_lru_kernel, reverse=reverse, n_chunks=n_chunks, rows=rows, accumulate=accumulate)
    tile = pl.BlockSpec((1, rows, d_lru), lambda b, c: (b, chunk_of(c), 0))
    return pl.pallas_call(
        kern,
        out_shape=jax.ShapeDtypeStruct((bsz, seq, d_lru), F32),
        grid=(bsz, n_chunks),
        in_specs=[
            pl.BlockSpec((1, rows, d_lru), lambda b, c: (b, chunk_of(c), 0)),
            pl.BlockSpec((1, SUBLANES, d_lru),
                         lambda b, c: (b, jnp.maximum(chunk_of(c) * halo - 1, 0), 0)),
            pl.BlockSpec((1, SUBLANES, d_lru),
                         lambda b, c: (b, jnp.minimum((chunk_of(c) + 1) * halo, last_halo), 0)),
            pl.BlockSpec((CONV_WIDTH, d_lru), lambda b, c: (0, 0)),
            pl.BlockSpec((1, d_lru), lambda b, c: (0, 0)),
            mat(), vec(), mat(), vec(), vec(),
        ] + ([tile] if accumulate else []),
        out_specs=tile,
        scratch_shapes=[pltpu.VMEM((1, d_lru), F32), pltpu.VMEM((d_lru // LANES, rows, LANES), F32),
                        pltpu.VMEM((d_lru // LANES, rows, LANES), F32),
                        pltpu.VMEM((rows // SUBLANES, d_lru), F32)],
        compiler_params=_params(("arbitrary", "arbitrary")),
        name="lru_bwd" if reverse else "lru_fwd",
    )(proj, proj, proj, conv_w, conv_b.reshape(1, d_lru), wa_bd, ba.reshape(2, 1, d_lru),
      wx_bd, bx.reshape(2, 1, d_lru), lam.reshape(2, 1, d_lru), *([add_to] if accumulate else []))


def _hgrn_direction(rev, q_ref, f_ref, v_ref, lb_ref, o_ref, st_ref, diag_s, lvl_s, upd_s, qe_s, btot_s, *, rows):
    ck, sb = HGRN_CHUNK, HGRN_SUB
    n_blk = ck // sb
    sb_shift = sb.bit_length() - 1
    n_sub = rows // ck
    width = q_ref.shape[-1]
    n_pairs = width // LANES
    half = LANES // 2

    def flip(idx, n):
        return (n - 1 - idx) if rev else idx

    n_lvl = n_blk.bit_length() - 1
    tf = flip(lax.broadcasted_iota(jnp.int32, (ck, ck), 0), ck)
    uf = flip(lax.broadcasted_iota(jnp.int32, (ck, ck), 1), ck)
    tb, ub = tf >> sb_shift, uf >> sb_shift
    pb = flip(lax.broadcasted_iota(jnp.int32, (n_blk, ck), 0), n_blk)
    pub = flip(lax.broadcasted_iota(jnp.int32, (n_blk, ck), 1), ck) >> sb_shift
    mats = [jnp.where((tb == ub) & (uf <= tf), 1.0, 0.0),
            jnp.where(pub < pb, 1.0, 0.0)]
    for lvl in range(n_lvl):
        mid = ((pb >> (lvl + 1)) << (lvl + 1)) + (1 << lvl)
        mats.append(jnp.where(pub < mid, 1.0, 0.0))
    mats.append(jnp.ones((SUBLANES, ck), F32))
    m_cum = jnp.concatenate(mats, axis=0).astype(BF16)

    def per_block(rows8):
        return jnp.concatenate(
            [jnp.broadcast_to(rows8[jb:jb + 1], (sb, rows8.shape[1])) for jb in range(n_blk)], axis=0)
    row_blk = flip(lax.broadcasted_iota(jnp.int32, (ck, width), 0), ck) >> sb_shift
    upper = [((row_blk >> lvl) & 1) == 1 for lvl in range(n_lvl)]
    pr = flip(lax.broadcasted_iota(jnp.int32, (ck, LANES), 0), ck) >> sb_shift
    pc = flip(lax.broadcasted_iota(jnp.int32, (ck, LANES), 1) & (ck - 1), ck) >> sb_shift
    group_mask = [(pr >> (lvl + 1)) == (pc >> (lvl + 1)) for lvl in range(n_lvl)]
    lane = lax.broadcasted_iota(jnp.int32, (1, LANES), 1)
    head0 = lane < half

    def split_heads(x):
        xb = x.astype(BF16)
        zero = jnp.zeros_like(xb)
        return jnp.concatenate([jnp.where(head0, xb, zero), jnp.where(head0, zero, xb)], axis=0)

    sr = lax.broadcasted_iota(jnp.int32, (LANES, LANES), 0)
    sc = lax.broadcasted_iota(jnp.int32, (LANES, LANES), 1)
    same_head = (sr < half) == (sc < half)
    er = lax.broadcasted_iota(jnp.int32, (sb * LANES, LANES), 0)
    ec = lax.broadcasted_iota(jnp.int32, (sb * LANES, LANES), 1)
    sel = jnp.where(ec == (((er & (LANES - 1)) >> (half.bit_length() - 1)) * half + (er >> (LANES.bit_length() - 1))),
                    1.0, 0.0).astype(BF16)
    sub_row = flip(lax.broadcasted_iota(jnp.int32, (sb, LANES), 0), sb)
    lbv = lb_ref[...]

    def row_start(j):
        return pl.multiple_of(flip(j, n_sub) * ck, ck)

    def stage1a(j):
        r0 = row_start(j)
        q = q_ref[0, pl.ds(r0, ck), :]
        z = f_ref[0, pl.ds(r0, ck), :]
        v = v_ref[0, pl.ds(r0, ck), :]
        f = lbv + (1.0 - lbv) * _sigmoid(z)
        lf2 = jnp.log(f) * LOG2E
        k = 1.0 - f
        return q, v, k, _dot01_exact(m_cum, lf2)

    def stage1b(q, v, k, cums):
        bl2 = cums[0:ck]
        b2 = bl2 + per_block(cums[ck:ck + n_blk])
        tot_row = ck + (1 + n_lvl) * n_blk
        btot2 = cums[tot_row:tot_row + 1]
        log2_k = jnp.log(k) * LOG2E
        kb = b2 - log2_k
        kbl = bl2 - log2_k
        qe = q * jnp.exp2(b2)
        ke = jnp.exp2(btot2 - kb)
        q_lvl, k_lvl = [], []
        for lvl in range(n_lvl):
            split2 = per_block(cums[ck + (1 + lvl) * n_blk:ck + (2 + lvl) * n_blk])
            q_lvl.append(q * jnp.exp2(jnp.where(upper[lvl], b2 - split2, NEG_BIG)))
            k_lvl.append(jnp.exp2(jnp.where(upper[lvl], NEG_BIG, split2 - kb)))

        qe_s[...] = qe.astype(BF16)
        btot_s[...] = btot2
        for p in range(n_pairs):
            sl = slice(p * LANES, (p + 1) * LANES)
            diag_rows = []
            for jb in range(n_blk):
                rs = slice(jb * sb, (jb + 1) * sb)
                bl_b, kbl_b, q_b = bl2[rs, sl], kbl[rs, sl], q[rs, sl]
                terms = []
                for s in range(sb):
                    arg = jnp.where(sub_row >= flip(s, sb), bl_b - kbl_b[s:s + 1], NEG_BIG)
                    terms.append(q_b * jnp.exp2(arg))
                diag_rows.append(jnp.concatenate(terms, axis=1))
            diag_s[p] = _dot(jnp.concatenate(diag_rows, axis=0).astype(BF16), sel)
            for lvl in range(n_lvl):
                k_p = k_lvl[lvl][:, sl]
                lvl_s[p * n_lvl + lvl] = _dot_nt(q_lvl[lvl][:, sl].astype(BF16), split_heads(k_p))
            upd_s[p] = _dot_tn(v[:, sl].astype(BF16), ke[:, sl].astype(BF16))

    def stage2_issue(j):
        r0 = row_start(j)
        v = v_ref[0, pl.ds(r0, ck), :]
        out = []
        for p in range(n_pairs):
            sl = slice(p * LANES, (p + 1) * LANES)
            parts = []
            for jb in range(n_blk):
                blk = diag_s[p, jb * sb:(jb + 1) * sb, :]
                parts.append(pltpu.roll(blk, jb * sb, 1) if jb else blk)
            scores = jnp.concatenate(parts, axis=0)
            for lvl in range(n_lvl):
                s_lvl = lvl_s[p * n_lvl + lvl]
                scores = scores + (s_lvl if lvl == n_lvl - 1 else jnp.where(group_mask[lvl], s_lvl, 0.0))
            intra = _dot(scores.astype(BF16), split_heads(v[:, sl]))
            st = st_ref[p]
            inter = _dot_nt(qe_s[:, sl], st.astype(BF16))
            new_st = jnp.where(same_head, st * jnp.exp2(btot_s[:, sl]) + upd_s[p], 0.0)
            out.append((inter + intra, new_st))
        return r0, out

    def stage2_finish(r0, out):
        for p in range(n_pairs):
            o_ref[0, pl.ds(r0, ck), p * LANES:(p + 1) * LANES] = out[p][0]
            st_ref[p] = out[p][1]

    return stage1a, stage1b, stage2_issue, stage2_finish


N_HGRN_SCRATCH = 6


def _hgrn_kernel(qf_ref, ff_ref, vf_ref, qb_ref, fb_ref, vb_ref, lb_ref, of_ref, ob_ref, zero_ref, *scratch,
                 rows):
    fwd_scratch, bwd_scratch = scratch[:N_HGRN_SCRATCH], scratch[N_HGRN_SCRATCH:]

    @pl.when(pl.program_id(1) == 0)
    def _():
        fwd_scratch[0][...] = jnp.zeros_like(fwd_scratch[0])
        bwd_scratch[0][...] = jnp.zeros_like(bwd_scratch[0])

    f1a, f1b, f2, f3 = _hgrn_direction(False, qf_ref, ff_ref, vf_ref, lb_ref, of_ref, *fwd_scratch, rows=rows)
    b1a, b1b, b2, b3 = _hgrn_direction(True, qb_ref, fb_ref, vb_ref, lb_ref, ob_ref, *bwd_scratch, rows=rows)
    n_sub = rows // HGRN_CHUNK

    def stage1_both(j):
        fa = f1a(j)
        ba = b1a(j)
        f1b(*fa)
        b1b(*ba)

    zero_part = zero_ref.shape[0] // n_sub

    def store_zeros(j):
        zero_ref[pl.ds(pl.multiple_of(j * zero_part, SUBLANES), zero_part), :] = jnp.zeros(
            (zero_part, zero_ref.shape[1]), zero_ref.dtype)

    stage1_both(0)

    def pipelined(j, carry):
        fo = f2(j)
        bo = b2(j)
        store_zeros(j)
        stage1_both(j + 1)
        f3(*fo)
        b3(*bo)
        return carry

    lax.fori_loop(0, n_sub - 1, pipelined, 0)
    fo = f2(n_sub - 1)
    bo = b2(n_sub - 1)
    store_zeros(n_sub - 1)
    f3(*fo)
    b3(*bo)


def _hgrn(proj, lb, zero_rows, *, d_lru, d_hgrn, rows=512):
    bsz, seq, _ = proj.shape
    n_chunks = seq // rows
    zero_blk = zero_rows // (bsz * n_chunks)
    assert zero_blk * bsz * n_chunks == zero_rows and zero_blk % (SUBLANES * (rows // HGRN_CHUNK)) == 0
    col0 = (2 * d_lru) // d_hgrn
    n_pairs = d_hgrn // LANES
    n_lvl = (HGRN_CHUNK // HGRN_SUB).bit_length() - 1
    fwd = lambda col: pl.BlockSpec((1, rows, d_hgrn), lambda b, c: (b, c, col))
    bwd = lambda col: pl.BlockSpec((1, rows, d_hgrn), lambda b, c: (b, n_chunks - 1 - c, col))
    direction_scratch = [
        pltpu.VMEM((n_pairs, LANES, LANES), F32),
        pltpu.VMEM((n_pairs, HGRN_CHUNK, LANES), F32),
        pltpu.VMEM((n_pairs * n_lvl, HGRN_CHUNK, LANES), F32),
        pltpu.VMEM((n_pairs, LANES, LANES), F32),
        pltpu.VMEM((HGRN_CHUNK, d_hgrn), BF16),
        pltpu.VMEM((1, d_hgrn), F32),
    ]
    assert len(direction_scratch) == N_HGRN_SCRATCH
    kern = functools.partial(_hgrn_kernel, rows=rows)
    out = jax.ShapeDtypeStruct((bsz, seq, d_hgrn), F32)
    return pl.pallas_call(
        kern,
        out_shape=(out, out, jax.ShapeDtypeStruct((zero_rows, LANES), F32)),
        grid=(bsz, n_chunks),
        in_specs=[fwd(col0), fwd(col0 + 1), fwd(col0 + 3), bwd(col0), bwd(col0 + 2), bwd(col0 + 3),
                  pl.BlockSpec((1, d_hgrn), lambda b, c: (0, 0))],
        out_specs=(pl.BlockSpec((1, rows, d_hgrn), lambda b, c: (b, c, 0)),
                   pl.BlockSpec((1, rows, d_hgrn), lambda b, c: (b, n_chunks - 1 - c, 0)),
                   pl.BlockSpec((zero_blk, LANES), lambda b, c: (b * n_chunks + c, 0))),
        scratch_shapes=direction_scratch + direction_scratch,
        compiler_params=_params(("arbitrary", "arbitrary")),
        name="hgrn2",
    )(proj, proj, proj, proj, proj, proj, lb.reshape(1, d_hgrn))


def _interleave(*streams):
    done = object()
    live = list(streams)
    while live:
        live = [s for s in live if next(s, done) is not done]


def _gelu_tanh(y):
    return 0.5 * y * (1.0 + jnp.tanh(0.7978845608028654 * (y + 0.044715 * (y * y * y))))


def _post_kernel(lru_ref, y_ref, of_ref, ob_ref, g_ref, x_ref, nlw_ref, nhw_ref, wo_ref,
                 gm_ref, nfw_ref, scf_ref, shf_ref, wr_ref, br_ref,
                 xo_ref, h_ref, slab_ref, cnt_ref, route_ref, carry_ref, logits_s, *, tm, d_lru):
    step = pl.program_id(0)

    @pl.when(step == 0)
    def _():
        carry_ref[...] = jnp.zeros_like(carry_ref)
        logits_s[...] = jnp.zeros_like(logits_s)

    def mixer():
        lru = lru_ref[0] * _gelu_tanh(y_ref[0])
        ms = jnp.mean(lru * lru, axis=-1, keepdims=True)
        yield
        lru = lru * lax.rsqrt(ms + NORM_EPS) * nlw_ref[...]

        hg = of_ref[0] + ob_ref[0]
        width = hg.shape[1]
        hd = width // HGRN_HEADS
        hd_shift = hd.bit_length() - 1
        er = lax.broadcasted_iota(jnp.int32, (width, width), 0) >> hd_shift
        ec = lax.broadcasted_iota(jnp.int32, (width, width), 1) >> hd_shift
        head_sum = jnp.where(er == ec, 1.0, 0.0).astype(BF16)
        sq = hg * hg
        sq_hi = sq.astype(BF16)
        sq_lo = (sq - sq_hi.astype(F32)).astype(BF16)
        ms_h = (_dot(sq_hi, head_sum) + _dot(sq_lo, head_sum)) * (1.0 / hd)
        yield
        g = g_ref[0]
        hg = (hg * lax.rsqrt(ms_h + NORM_EPS) * nhw_ref[...]) * (g * _sigmoid(g))
        yield
        mixed = _dot(lru.astype(BF16), wo_ref[0:d_lru, :])
        yield
        mixed = mixed + _dot(hg.astype(BF16), wo_ref[d_lru:, :])
        yield
        x_new = x_ref[0] + gm_ref[0] * mixed
        xo_ref[0] = x_new
        ms_f = jnp.mean(x_new * x_new, axis=-1, keepdims=True)
        yield
        h = (x_new * lax.rsqrt(ms_f + NORM_EPS) * nfw_ref[...]) * (1.0 + scf_ref[0]) + shf_ref[0]
        _tiles_store(h_ref, h, tm, lead=(0,))
        yield
        logits_s[...] = _dot(h.astype(BF16), wr_ref[...]) + br_ref[...]

    def routing():
        yield from _routing_steps(logits_s[...], jnp.where(step > 0, 1.0, 0.0), slab_ref, cnt_ref, route_ref,
                                  carry_ref, tm)

    _interleave(routing(), mixer())


def _routing_steps(logits, live, slab_ref, cnt_ref, route_ref, carry_ref, tm):
    lane = lax.broadcasted_iota(jnp.int32, (tm, ROUTE_LANES), 1)
    lane_f = lane.astype(F32)
    far = float(ROUTE_LANES)
    is_g = lane < N_GROUPS
    gl = jnp.where(is_g, logits, NEG_BIG)
    gmax = jnp.max(gl, axis=-1, keepdims=True)
    yield
    g_idx = jnp.min(jnp.where(gl == gmax, lane_f, far), axis=-1, keepdims=True)
    p_group = 1.0 / jnp.sum(jnp.where(is_g, jnp.exp(gl - gmax), 0.0), axis=-1, keepdims=True)
    yield
    e_lane = lane - N_GROUPS
    in_group = (e_lane >= 0) & (e_lane < N_EXPERTS) & ((e_lane >> (EXPERTS_PER_GROUP.bit_length() - 1)).astype(F32) == g_idx)
    ev = jnp.where(in_group, logits, NEG_BIG)
    top1 = jnp.max(ev, axis=-1, keepdims=True)
    yield
    i1 = jnp.min(jnp.where(in_group & (ev == top1), lane_f, far), axis=-1, keepdims=True)
    yield
    rest = in_group & (lane_f != i1)
    ev2 = jnp.where(rest, logits, NEG_BIG)
    top2 = jnp.max(ev2, axis=-1, keepdims=True)
    yield
    i2 = jnp.min(jnp.where(rest & (ev2 == top2), lane_f, far), axis=-1, keepdims=True)
    yield
    e1 = i1 - float(N_GROUPS)
    e2 = i2 - float(N_GROUPS)
    ex = jnp.exp(top2 - top1)
    w1 = p_group / (1.0 + ex)
    w2 = p_group * ex / (1.0 + ex)

    sel1 = lane_f == e1
    sel2 = lane_f == e2
    onehot = jnp.where(sel1 | sel2, live, 0.0)
    tr = lax.broadcasted_iota(jnp.int32, (tm, tm), 0)
    tc = lax.broadcasted_iota(jnp.int32, (tm, tm), 1)
    before = jnp.where(tc < tr, 1.0, 0.0).astype(BF16)
    cnt = _dot(before, onehot.astype(BF16)) + carry_ref[0:1]
    yield
    rank1 = jnp.sum(jnp.where(sel1, cnt, 0.0), axis=-1, keepdims=True)
    rank2 = jnp.sum(jnp.where(sel2, cnt, 0.0), axis=-1, keepdims=True)
    total = carry_ref[0:1] + jnp.sum(onehot, axis=0, keepdims=True)
    carry_ref[...] = jnp.broadcast_to(total, carry_ref.shape)
    cnt_ref[...] = jnp.broadcast_to(total, cnt_ref.shape)
    yield

    slab = jnp.where(lane == 0, e1, 0.0)
    slab = jnp.where(lane == 1, e2, slab)
    slab = jnp.where(lane == 2, w1, slab)
    slab = jnp.where(lane == 3, w2, slab)
    slab = jnp.where(lane == 4, rank1, slab)
    slab = jnp.where(lane == 5, rank2, slab)
    slab_ref[0] = slab
    route_ref[...] = slab.T[0:SUBLANES]


def _post_mixer(lru_sum, proj, hg_f, hg_b, x, nlw, nhw, wo_bf16, g_mix, nfw, sc_ffn, sh_ffn, wr_bf16, br,
                *, tm=512):
    bsz, seq, d = x.shape
    d_lru = lru_sum.shape[-1]
    d_hgrn = hg_f.shape[-1]
    y_col = 1
    g_col = (2 * d_lru) // d_hgrn + 4
    tiles = seq // tm
    n_tiles = bsz * tiles
    cur = lambda s: jnp.minimum(s, n_tiles - 1)
    prev = lambda s: jnp.maximum(s - 1, 0)
    row = lambda w, col=0: pl.BlockSpec((1, tm, w), lambda s: (cur(s) // tiles, cur(s) % tiles, col))
    vec = lambda w: pl.BlockSpec((1, w), lambda s: (0, 0))
    per_b = lambda: pl.BlockSpec((1, 1, d), lambda s: (cur(s) // tiles, 0, 0))
    kern = functools.partial(_post_kernel, tm=tm, d_lru=d_lru)
    return pl.pallas_call(
        kern,
        out_shape=(
            jax.ShapeDtypeStruct((bsz, seq, d), F32),
            jax.ShapeDtypeStruct((bsz, seq * SUBLANES, LANES), F32),
            jax.ShapeDtypeStruct((bsz, seq, ROUTE_LANES), F32),
            jax.ShapeDtypeStruct((SUBLANES, ROUTE_LANES), F32),
            jax.ShapeDtypeStruct((SUBLANES, bsz * seq), F32),
        ),
        grid=(n_tiles + 1,),
        in_specs=[
            row(d_lru), row(d_lru, y_col),
            row(d_hgrn), row(d_hgrn), row(d_hgrn, g_col),
            row(d), vec(d_lru), vec(d_hgrn),
            pl.BlockSpec((d, d), lambda s: (0, 0)),
            per_b(), vec(d), per_b(), per_b(),
            pl.BlockSpec((d, ROUTE_LANES), lambda s: (0, 0)),
            vec(ROUTE_LANES),
        ],
        out_specs=(
            row(d),
            pl.BlockSpec((1, tm * SUBLANES, LANES), lambda s: (cur(s) // tiles, cur(s) % tiles, 0)),
            pl.BlockSpec((1, tm, ROUTE_LANES), lambda s: (prev(s) // tiles, prev(s) % tiles, 0)),
            pl.BlockSpec((SUBLANES, ROUTE_LANES), lambda s: (0, 0)),
            pl.BlockSpec((SUBLANES, tm), lambda s: (0, prev(s))),
        ),
        scratch_shapes=[pltpu.VMEM((SUBLANES, ROUTE_LANES), F32), pltpu.VMEM((tm, ROUTE_LANES), F32)],
        compiler_params=_params(("arbitrary",)),
        name="post_mixer_router",
    )(lru_sum, proj, hg_f, hg_b, proj, x, nlw.reshape(1, d_lru), nhw.reshape(1, d_hgrn), wo_bf16,
      g_mix, nfw.reshape(1, d), sc_ffn, sh_ffn, wr_bf16, br)


def _tiles_load(ref, n, lead=()):
    return jnp.concatenate(
        [ref[(*lead, pl.ds(j, n, stride=SUBLANES), slice(None))] for j in range(SUBLANES)], axis=1)


def _tiles_store(ref, val, n, lead=()):
    for j in range(SUBLANES):
        ref[(*lead, pl.ds(j, n, stride=SUBLANES), slice(None))] = val[:, j * LANES:(j + 1) * LANES]


def _token_tile(ref, t):
    return ref.at[pl.ds(pl.multiple_of(t * SUBLANES, SUBLANES), SUBLANES)]


def _dest_kernel(start_ref, route_ref, o_ref):
    route = route_ref[...].astype(jnp.int32)
    start = jnp.zeros_like(route)
    for e in range(N_EXPERTS):
        start = jnp.where(route == e, start_ref[e], start)
    o_ref[...] = start + pltpu.roll(route, SUBLANES // 2, 0)


def _dest_rows(expert_start, route):
    return pl.pallas_call(
        _dest_kernel,
        out_shape=jax.ShapeDtypeStruct(route.shape, jnp.int32),
        grid_spec=pltpu.PrefetchScalarGridSpec(
            num_scalar_prefetch=1,
            grid=(1,),
            in_specs=[pl.BlockSpec(route.shape, lambda i, s: (0, 0))],
            out_specs=pl.BlockSpec(route.shape, lambda i, s: (0, 0)),
        ),
        compiler_params=pltpu.CompilerParams(dimension_semantics=("arbitrary",)),
        name="moe_dest_rows",
    )(expert_start, route)


def _dispatch_kernel(d1_ref, d2_ref, h_ref, z_ref, o_ref, sem, *, tb):
    del z_ref
    base = pl.program_id(0) * tb

    def issue(r, carry):
        t = base + r
        pltpu.make_async_copy(_token_tile(h_ref, r), _token_tile(o_ref, d1_ref[t]), sem).start(priority=0)
        pltpu.make_async_copy(_token_tile(h_ref, r), _token_tile(o_ref, d2_ref[t]), sem).start(priority=1)
        return carry

    lax.fori_loop(0, tb, issue, 0, unroll=DMA_ISSUE_UNROLL)
    for _ in range(2):
        pltpu.make_async_copy(h_ref, o_ref.at[pl.ds(0, tb * SUBLANES)], sem).wait()


def _dispatch(dest1, dest2, h_tiles, zero_tiles, *, tb=512):
    m = h_tiles.shape[0] // SUBLANES
    n_rows = zero_tiles.shape[0] // SUBLANES
    kern = functools.partial(_dispatch_kernel, tb=tb)
    return pl.pallas_call(
        kern,
        out_shape=jax.ShapeDtypeStruct((n_rows * SUBLANES, LANES), h_tiles.dtype),
        grid_spec=pltpu.PrefetchScalarGridSpec(
            num_scalar_prefetch=2,
            grid=(m // tb,),
            in_specs=[pl.BlockSpec((tb * SUBLANES, LANES), lambda i, d1, d2: (i, 0)),
                      pl.BlockSpec(memory_space=pl.ANY)],
            out_specs=pl.BlockSpec(memory_space=pl.ANY),
            scratch_shapes=[pltpu.SemaphoreType.DMA(())],
        ),
        input_output_aliases={3: 0},
        compiler_params=pltpu.CompilerParams(dimension_semantics=("arbitrary",), has_side_effects=True),
        name="moe_dispatch",
    )(dest1, dest2, h_tiles, zero_tiles)


def _expert_kernel(plan_ref, x_ref, wg_hbm, wu_hbm, wd_hbm, o_ref, wg_f, wu_f, wd_f, wg_s, wu_s, wd_s, sem,
                   *, layer):
    i = pl.program_id(0)
    n_blocks = pl.num_programs(0)
    n_used = plan_ref[n_blocks]
    expert = plan_ref[i]
    next_expert = plan_ref[n_blocks + 1 + i]
    slot = plan_ref[2 * n_blocks + 1 + i]
    first_block = ((i == 0) | (plan_ref[jnp.maximum(i - 1, 0)] != expert)) & (i < n_used)

    def copies(e, s):
        return (pltpu.make_async_copy(wg_hbm.at[layer, e], wg_f.at[s], sem.at[s]),
                pltpu.make_async_copy(wu_hbm.at[layer, e], wu_f.at[s], sem.at[s]),
                pltpu.make_async_copy(wd_hbm.at[layer, e], wd_f.at[s], sem.at[s]))

    @pl.when(i == 0)
    def _():
        for c in copies(expert, slot):
            c.start()

    @pl.when(first_block)
    def _():
        for c in copies(expert, slot):
            c.wait()
        wg_s[...] = wg_f[slot].astype(BF16)
        wu_s[...] = wu_f[slot].astype(BF16)
        wd_s[...] = wd_f[slot].astype(BF16)

        @pl.when(next_expert >= 0)
        def _():
            for c in copies(next_expert, 1 - slot):
                c.start()

    blk = x_ref.shape[0] // SUBLANES

    @pl.when(i < n_used)
    def _():
        x = _tiles_load(x_ref, blk).astype(BF16)
        gate = _dot(x, wg_s[...])
        up = _dot(x, wu_s[...])
        act = (gate * _sigmoid(gate)) * up
        _tiles_store(o_ref, _dot(act.astype(BF16), wd_s[...]), blk)

    @pl.when(i >= n_used)
    def _():
        o_ref[...] = jnp.zeros_like(o_ref)


def _expert_plan(blk_expert, blocks_used):
    n_blocks = blk_expert.shape[0]
    idx = jnp.arange(n_blocks, dtype=jnp.int32)
    change = jnp.concatenate([jnp.ones((1,), bool), blk_expert[1:] != blk_expert[:-1]])
    slot = (jnp.cumsum(change.astype(jnp.int32)) - 1) & 1
    change_at = jnp.where(change, idx, n_blocks)
    from_here = lax.cummin(change_at[::-1])[::-1]
    next_change = jnp.concatenate([from_here[1:], jnp.full((1,), n_blocks, jnp.int32)])
    next_expert = jnp.where(next_change < blocks_used, blk_expert[jnp.minimum(next_change, n_blocks - 1)], -1)
    return jnp.concatenate([blk_expert, blocks_used.reshape(1), next_expert, slot]).astype(jnp.int32)


def _experts(plan, x_tiles, wg, wu, wd, layer):
    n_rows = x_tiles.shape[0] // SUBLANES
    d, de = wg.shape[-2:]
    blk = EXPERT_BLOCK
    n_blocks = n_rows // blk
    kern = functools.partial(_expert_kernel, layer=layer)
    return pl.pallas_call(
        kern,
        out_shape=jax.ShapeDtypeStruct((n_rows * SUBLANES, LANES), F32),
        grid_spec=pltpu.PrefetchScalarGridSpec(
            num_scalar_prefetch=1,
            grid=(n_blocks,),
            in_specs=[
                pl.BlockSpec((blk * SUBLANES, LANES), lambda i, plan: (jnp.minimum(i, plan[n_blocks] - 1), 0)),
                pl.BlockSpec(memory_space=pl.ANY),
                pl.BlockSpec(memory_space=pl.ANY),
                pl.BlockSpec(memory_space=pl.ANY),
            ],
            out_specs=pl.BlockSpec((blk * SUBLANES, LANES), lambda i, plan: (i, 0)),
            scratch_shapes=[
                pltpu.VMEM((2, d, de), F32), pltpu.VMEM((2, d, de), F32), pltpu.VMEM((2, de, d), F32),
                pltpu.VMEM((d, de), BF16), pltpu.VMEM((d, de), BF16), pltpu.VMEM((de, d), BF16),
                pltpu.SemaphoreType.DMA((2,)),
            ],
        ),
        compiler_params=_params(("arbitrary",)),
        name="moe_experts",
    )(plan, x_tiles, wg, wu, wd)


def _combine_kernel(d1_ref, d2_ref, y_ref, slab_ref, x_ref, g_ref, nw_ref, o_ref, ra0, rb0, ra1, rb1, sem,
                    *, tm, tiles, n_steps, final_norm):
    step = pl.program_id(0) * tiles + pl.program_id(1)
    bufs = ((ra0, rb0), (ra1, rb1))

    def gather(tile, slot):
        base = tile * tm
        r1_ref, r2_ref = bufs[slot]

        def issue(r, carry):
            t = base + r
            pltpu.make_async_copy(_token_tile(y_ref, d1_ref[t]), _token_tile(r1_ref, r),
                                  sem.at[slot]).start(priority=0)
            pltpu.make_async_copy(_token_tile(y_ref, d2_ref[t]), _token_tile(r2_ref, r),
                                  sem.at[slot]).start(priority=1)
            return carry

        lax.fori_loop(0, tm, issue, 0, unroll=DMA_ISSUE_UNROLL)

    @pl.when(step == 0)
    def _():
        gather(0, 0)

    for slot in range(2):
        @pl.when((step & 1) == slot)
        def _():
            @pl.when(step + 1 < n_steps)
            def _():
                gather(step + 1, 1 - slot)

            r1_ref, r2_ref = bufs[slot]
            pltpu.make_async_copy(y_ref.at[pl.ds(0, tm * SUBLANES)], r1_ref, sem.at[slot]).wait()
            pltpu.make_async_copy(y_ref.at[pl.ds(0, tm * SUBLANES)], r2_ref, sem.at[slot]).wait()
            slab = slab_ref[0]
            y = slab[:, 2:3] * _tiles_load(r1_ref, tm) + slab[:, 3:4] * _tiles_load(r2_ref, tm)
            out = x_ref[0] + g_ref[0] * y
            if final_norm:
                ms = jnp.mean(out * out, axis=-1, keepdims=True)
                out = out * lax.rsqrt(ms + NORM_EPS) * nw_ref[...]
            o_ref[0] = out


def _combine(dest1, dest2, y_buf, slab, x, g_ffn, norm_w, *, final_norm, tm=512):
    bsz, seq, d = x.shape
    tiles = seq // tm
    kern = functools.partial(_combine_kernel, tm=tm, tiles=tiles, n_steps=bsz * tiles, final_norm=final_norm)
    row_buf = pltpu.VMEM((tm * SUBLANES, LANES), F32)
    return pl.pallas_call(
        kern,
        out_shape=jax.ShapeDtypeStruct((bsz, seq, d), F32),
        grid_spec=pltpu.PrefetchScalarGridSpec(
            num_scalar_prefetch=2,
            grid=(bsz, tiles),
            in_specs=[
                pl.BlockSpec(memory_space=pl.ANY),
                pl.BlockSpec((1, tm, ROUTE_LANES), lambda b, i, d1, d2: (b, i, 0)),
                pl.BlockSpec((1, tm, d), lambda b, i, d1, d2: (b, i, 0)),
                pl.BlockSpec((1, 1, d), lambda b, i, d1, d2: (b, 0, 0)),
                pl.BlockSpec((1, d), lambda b, i, d1, d2: (0, 0)),
            ],
            out_specs=pl.BlockSpec((1, tm, d), lambda b, i, d1, d2: (b, i, 0)),
            scratch_shapes=[row_buf, row_buf, row_buf, row_buf, pltpu.SemaphoreType.DMA((2,))],
        ),
        compiler_params=_params(("arbitrary", "arbitrary")),
        name="moe_combine",
    )(dest1, dest2, y_buf, slab, x, g_ffn, norm_w.reshape(1, d))


def _block_diag(w):
    heads, hd, _ = w.shape
    n = heads * hd
    tiled = jnp.tile(w.reshape(n, hd), (1, heads))
    blk_r = lax.broadcasted_iota(jnp.int32, (n, n), 0) // hd
    blk_c = lax.broadcasted_iota(jnp.int32, (n, n), 1) // hd
    return jnp.where(blk_r == blk_c, tiled, 0.0)


def kernel(x, c, ada_w, ada_b, norm_mix_w, w_in, conv_w, conv_b, lru_wa, lru_ba, lru_wx, lru_bx, lru_lambda, norm_lru_w, hgrn_lb, norm_hgrn_w, w_out, norm_ffn_w, router_group_w, router_group_b, router_expert_w, router_expert_b, expert_w_gate, expert_w_up, expert_w_down, final_norm_w):
    bsz, seq, d = x.shape
    assert d == SUBLANES * LANES, "the MoE row movement keeps one (8, 128) tile per token"
    depth = ada_w.shape[0]
    d_lru = conv_w.shape[-1]
    d_hgrn = hgrn_lb.shape[-1]
    m = bsz * seq
    n_rows = m * 2 + N_EXPERTS * EXPERT_BLOCK
    n_blocks = n_rows // EXPERT_BLOCK

    mod = _modulation(c, ada_w, ada_b)
    lb_cum = jnp.cumsum(jax.nn.softmax(hgrn_lb.astype(F32), axis=0), axis=0)
    lb_all = lb_cum - lb_cum[0:1]

    for l in range(depth):
        sh_mix, sc_mix, g_mix, sh_ffn, sc_ffn, g_ffn = [
            mod[l, :, i * d:(i + 1) * d].reshape(bsz, 1, d) for i in range(6)]
        proj = _in_proj(x, norm_mix_w[l], sc_mix, sh_mix, w_in, l)
        wa_bd = jnp.stack([_block_diag(lru_wa[l, 0]), _block_diag(lru_wa[l, 1])]).astype(BF16)
        wx_bd = jnp.stack([_block_diag(lru_wx[l, 0]), _block_diag(lru_wx[l, 1])]).astype(BF16)
        lru_args = (proj, conv_w[l], conv_b[l], wa_bd, lru_ba[l], wx_bd, lru_bx[l], lru_lambda[l])
        lru_sum = _lru_scan(*lru_args, reverse=True, add_to=_lru_scan(*lru_args, reverse=False))
        hg_f, hg_b, zero_tiles = _hgrn(proj, lb_all[l], n_rows * SUBLANES, d_lru=d_lru, d_hgrn=d_hgrn)

        lane_pad = ROUTE_LANES - N_GROUPS - N_EXPERTS
        wr = jnp.pad(jnp.concatenate([router_group_w[l], router_expert_w[l]], axis=1), ((0, 0), (0, lane_pad)))
        br = jnp.pad(jnp.concatenate([router_group_b[l], router_expert_b[l]]), (0, lane_pad)).reshape(1, ROUTE_LANES)
        x_mid, h_ffn, slab, counts, route = _post_mixer(
            lru_sum, proj, hg_f, hg_b, x, norm_lru_w[l], norm_hgrn_w[l], w_out[l].astype(BF16), g_mix,
            norm_ffn_w[l], sc_ffn, sh_ffn, wr.astype(BF16), br)

        cnt = counts[0, :N_EXPERTS].astype(jnp.int32)
        padded = ((cnt + EXPERT_BLOCK - 1) // EXPERT_BLOCK) * EXPERT_BLOCK
        pend = jnp.cumsum(padded)
        pstart = pend - padded
        blk_start = jnp.arange(n_blocks, dtype=jnp.int32) * EXPERT_BLOCK
        blk_expert = jnp.minimum(jnp.sum(pend[None, :] <= blk_start[:, None], axis=1), N_EXPERTS - 1)
        plan = _expert_plan(blk_expert.astype(jnp.int32), (pend[N_EXPERTS - 1] // EXPERT_BLOCK).astype(jnp.int32))
        dest = _dest_rows(pstart.astype(jnp.int32), route)
        dest1, dest2 = dest[0], dest[1]

        x_buf = _dispatch(dest1, dest2, h_ffn.reshape(m * SUBLANES, LANES), zero_tiles)
        y_buf = _experts(plan, x_buf, expert_w_gate, expert_w_up, expert_w_down, l)
        x = _combine(dest1, dest2, y_buf, slab, x_mid, g_ffn, final_norm_w, final_norm=(l == depth - 1))

    return x
```

```python
import functools

import jax
import jax.numpy as jnp
from jax import lax
from jax.experimental import pallas as pl
from jax.experimental.pallas import tpu as pltpu

F32 = jnp.float32
BF16 = jnp.bfloat16

HGRN_HEADS = 8
N_MODULATIONS = 6
CONV_WIDTH = 4
LRU_C = 8.0
N_GROUPS = 4
EXPERTS_PER_GROUP = 8
N_EXPERTS = N_GROUPS * EXPERTS_PER_GROUP
NORM_EPS = 1e-6

LANES = 128
SUBLANES = 8
VMEM_LIMIT = 56 * 1024 * 1024

HGRN_CHUNK = 64
HGRN_SUB = 8
LOG2E = 1.4426950408889634
ROUTE_LANES = LANES
EXPERT_BLOCK = 512
DMA_ISSUE_UNROLL = 8
NEG_BIG = -3.0e38


def _params(sem):
    return pltpu.CompilerParams(dimension_semantics=sem, vmem_limit_bytes=VMEM_LIMIT)


def _dot(a, b):
    return jnp.dot(a, b, preferred_element_type=F32)


def _dot_nt(a, b):
    return lax.dot_general(a, b, (((1,), (1,)), ((), ())), preferred_element_type=F32)


def _dot_tn(a, b):
    return lax.dot_general(a, b, (((0,), (0,)), ((), ())), preferred_element_type=F32)


def _dot01_exact(m01, x):
    hi = x.astype(BF16)
    r1 = x - hi.astype(F32)
    mid = r1.astype(BF16)
    lo = (r1 - mid.astype(F32)).astype(BF16)
    return _dot(m01, hi) + _dot(m01, mid) + _dot(m01, lo)


def _sigmoid(x):
    return 1.0 / (1.0 + jnp.exp(-x))


def _mod_kernel(c_ref, w_ref, b_ref, o_ref):
    c = c_ref[...]
    cond = c * _sigmoid(c)
    o_ref[0] = _dot(cond.astype(BF16), w_ref[0].astype(BF16)) + b_ref[0]


def _modulation(c, ada_w, ada_b):
    depth, d, n = ada_w.shape
    bsz = c.shape[0]
    rows = -(-bsz // SUBLANES) * SUBLANES
    c_pad = jnp.pad(c, ((0, rows - bsz), (0, 0)))
    tn = n // N_MODULATIONS
    out = pl.pallas_call(
        _mod_kernel,
        out_shape=jax.ShapeDtypeStruct((depth, rows, n), F32),
        grid=(depth, n // tn),
        in_specs=[
            pl.BlockSpec((rows, d), lambda l, j: (0, 0)),
            pl.BlockSpec((1, d, tn), lambda l, j: (l, 0, j)),
            pl.BlockSpec((1, 1, tn), lambda l, j: (l, 0, j)),
        ],
        out_specs=pl.BlockSpec((1, rows, tn), lambda l, j: (l, 0, j)),
        compiler_params=_params(("arbitrary", "arbitrary")),
        name="adaln_mod",
    )(c_pad, ada_w, ada_b.reshape(depth, 1, n))
    return out[:, :bsz]


def _rms_mod(x, nw, sc, sh):
    ms = jnp.mean(x * x, axis=-1, keepdims=True)
    return (x * lax.rsqrt(ms + NORM_EPS) * nw) * (1.0 + sc) + sh


def _inproj_kernel(x_ref, nw_ref, sc_ref, sh_ref, w_ref, o_ref, w_s):
    @pl.when((pl.program_id(0) == 0) & (pl.program_id(1) == 0))
    def _():
        w_s[...] = w_ref[0].astype(BF16)

    h = _rms_mod(x_ref[0], nw_ref[...], sc_ref[0], sh_ref[0])
    o_ref[0] = _dot(h.astype(BF16), w_s[...])


def _in_proj(x, nw, sc, sh, w_in, layer, tm=512):
    bsz, seq, d = x.shape
    n = w_in.shape[-1]
    return pl.pallas_call(
        _inproj_kernel,
        out_shape=jax.ShapeDtypeStruct((bsz, seq, n), F32),
        grid=(bsz, seq // tm),
        in_specs=[
            pl.BlockSpec((1, tm, d), lambda b, i: (b, i, 0)),
            pl.BlockSpec((1, d), lambda b, i: (0, 0)),
            pl.BlockSpec((1, 1, d), lambda b, i: (b, 0, 0)),
            pl.BlockSpec((1, 1, d), lambda b, i: (b, 0, 0)),
            pl.BlockSpec((1, d, n), lambda b, i: (layer, 0, 0), pipeline_mode=pl.Buffered(1)),
        ],
        out_specs=pl.BlockSpec((1, tm, n), lambda b, i: (b, i, 0)),
        scratch_shapes=[pltpu.VMEM((d, n), BF16)],
        compiler_params=_params(("arbitrary", "arbitrary")),
        name="in_proj",
    )(x, nw.reshape(1, d), sc, sh, w_in)


def _lru_kernel(x_ref, xp_ref, xn_ref, cw_ref, cb_ref, wa_ref, ba_ref, wx_ref, bx_ref, lam_ref, *rest,
                reverse, n_chunks, rows, accumulate):
    add_ref = rest[0] if accumulate else None
    o_ref, carry_ref, sa_ref, sb_ref, cin_ref = rest[1:] if accumulate else rest
    c = pl.program_id(1)
    chunk = (n_chunks - 1 - c) if reverse else c

    @pl.when(c == 0)
    def _():
        carry_ref[...] = jnp.zeros_like(carry_ref)

    x = x_ref[0]
    width = x.shape[1]
    has_prev = jnp.where(chunk > 0, 1.0, 0.0)
    has_next = jnp.where(chunk < n_chunks - 1, 1.0, 0.0)
    xp = xp_ref[0] * has_prev
    xn = xn_ref[0] * has_next
    xe = jnp.concatenate([xp, x, xn], axis=0)
    cw = cw_ref[...]
    xc = cb_ref[...]
    for k in range(CONV_WIDTH):
        lo = SUBLANES + k - CONV_WIDTH // 2
        xc = xc + cw[k:k + 1] * xe[lo:lo + rows]

    xcb = xc.astype(BF16)
    r = _sigmoid(_dot(xcb, wa_ref[0]) + ba_ref[0])
    gate_i = _sigmoid(_dot(xcb, wx_ref[0]) + bx_ref[0])
    lam = lam_ref[0]
    softplus_neg_lam = jnp.maximum(-lam, 0.0) + jnp.log1p(jnp.exp(-jnp.abs(lam)))
    log_a = (-LRU_C) * r * softplus_neg_lam
    a = jnp.exp(log_a)
    t = jnp.tanh(-log_a)
    u = jnp.sqrt(2.0 * t / (1.0 + t)) * (gate_i * xc)

    groups = rows // SUBLANES
    acc_a = a.reshape(groups, SUBLANES, width)
    acc_b = u.reshape(groups, SUBLANES, width)
    sub = lax.broadcasted_iota(jnp.int32, (groups, SUBLANES, width), 1)
    s = 1
    while s < SUBLANES:
        if reverse:
            valid = sub < SUBLANES - s
            sh_a, sh_b = pltpu.roll(acc_a, SUBLANES - s, 1), pltpu.roll(acc_b, SUBLANES - s, 1)
        else:
            valid = sub >= s
            sh_a, sh_b = pltpu.roll(acc_a, s, 1), pltpu.roll(acc_b, s, 1)
        acc_b = jnp.where(valid, acc_a * sh_b + acc_b, acc_b)
        acc_a = jnp.where(valid, acc_a * sh_a, acc_a)
        s *= 2
    acc_a = acc_a.reshape(rows, width)
    acc_b = acc_b.reshape(rows, width)
    edge = 0 if reverse else SUBLANES - 1
    n_tiles = width // LANES
    for j in range(n_tiles):
        sa_ref[j] = acc_a[:, j * LANES:(j + 1) * LANES]
        sb_ref[j] = acc_b[:, j * LANES:(j + 1) * LANES]
    ea = jnp.concatenate([sa_ref[j, pl.ds(edge, groups, stride=SUBLANES), :] for j in range(n_tiles)], axis=1)
    eb = jnp.concatenate([sb_ref[j, pl.ds(edge, groups, stride=SUBLANES), :] for j in range(n_tiles)], axis=1)
    grow = lax.broadcasted_iota(jnp.int32, (groups, width), 0)
    s = 1
    while s < groups:
        if reverse:
            valid = grow < groups - s
            sh_a, sh_b = pltpu.roll(ea, groups - s, 0), pltpu.roll(eb, groups - s, 0)
        else:
            valid = grow >= s
            sh_a, sh_b = pltpu.roll(ea, s, 0), pltpu.roll(eb, s, 0)
        eb = jnp.where(valid, ea * sh_b + eb, eb)
        ea = jnp.where(valid, ea * sh_a, ea)
        s *= 2
    carry = carry_ref[...]
    group_out = eb + ea * carry
    if reverse:
        carry_in = jnp.where(grow == groups - 1, carry, pltpu.roll(group_out, groups - 1, 0))
        carry_ref[...] = group_out[0:1]
    else:
        carry_in = jnp.where(grow == 0, carry, pltpu.roll(group_out, 1, 0))
        carry_ref[...] = group_out[groups - 1:groups]
    cin_ref[...] = carry_in
    for g in range(groups):
        rs = slice(g * SUBLANES, (g + 1) * SUBLANES)
        h_rows = acc_b[rs] + acc_a[rs] * cin_ref[g:g + 1, :]
        o_ref[0, rs, :] = (add_ref[0, rs, :] + h_rows) if accumulate else h_rows


def _lru_scan(proj, conv_w, conv_b, wa_bd, ba, wx_bd, bx, lam, *, reverse, add_to=None, rows=512):
    bsz, seq, _ = proj.shape
    d_lru = conv_w.shape[1]
    n_chunks = seq // rows
    halo = rows // SUBLANES
    last_halo = seq // SUBLANES - 1
    dirn = 1 if reverse else 0

    def chunk_of(c):
        return (n_chunks - 1 - c) if reverse else c

    vec = lambda: pl.BlockSpec((1, 1, d_lru), lambda b, c: (dirn, 0, 0))
    mat = lambda: pl.BlockSpec((1, d_lru, d_lru), lambda b, c: (dirn, 0, 0))
    accumulate = add_to is not None
    kern = functools.partial(_lru_kernel, reverse=reverse, n_chunks=n_chunks, rows=rows, accumulate=accumulate)
    tile = pl.BlockSpec((1, rows, d_lru), lambda b, c: (b, chunk_of(c), 0))
    return pl.pallas_call(
        kern,
        out_shape=jax.ShapeDtypeStruct((bsz, seq, d_lru), F32),
        grid=(bsz, n_chunks),
        in_specs=[
            pl.BlockSpec((1, rows, d_lru), lambda b, c: (b, chunk_of(c), 0)),
            pl.BlockSpec((1, SUBLANES, d_lru),
                         lambda b, c: (b, jnp.maximum(chunk_of(c) * halo - 1, 0), 0)),
            pl.BlockSpec((1, SUBLANES, d_lru),
                         lambda b, c: (b, jnp.minimum((chunk_of(c) + 1) * halo, last_halo), 0)),
            pl.BlockSpec((CONV_WIDTH, d_lru), lambda b, c: (0, 0)),
            pl.BlockSpec((1, d_lru), lambda b, c: (0, 0)),
            mat(), vec(), mat(), vec(), vec(),
        ] + ([tile] if accumulate else []),
        out_specs=tile,
        scratch_shapes=[pltpu.VMEM((1, d_lru), F32), pltpu.VMEM((d_lru // LANES, rows, LANES), F32),
                        pltpu.VMEM((d_lru // LANES, rows, LANES), F32),
                        pltpu.VMEM((rows // SUBLANES, d_lru), F32)],
        compiler_params=_params(("arbitrary", "arbitrary")),
        name="lru_bwd" if reverse else "lru_fwd",
    )(proj, proj, proj, conv_w, conv_b.reshape(1, d_lru), wa_bd, ba.reshape(2, 1, d_lru),
      wx_bd, bx.reshape(2, 1, d_lru), lam.reshape(2, 1, d_lru), *([add_to] if accumulate else []))


def _hgrn_direction(rev, q_ref, f_ref, v_ref, lb_ref, o_ref, st_ref, diag_s, lvl_s, upd_s, qe_s, btot_s, *, rows):
    ck, sb = HGRN_CHUNK, HGRN_SUB
    n_blk = ck // sb
    sb_shift = sb.bit_length() - 1
    n_sub = rows // ck
    width = q_ref.shape[-1]
    n_pairs = width // LANES
    half = LANES // 2

    def flip(idx, n):
        return (n - 1 - idx) if rev else idx

    n_lvl = n_blk.bit_length() - 1
    tf = flip(lax.broadcasted_iota(jnp.int32, (ck, ck), 0), ck)
    uf = flip(lax.broadcasted_iota(jnp.int32, (ck, ck), 1), ck)
    tb, ub = tf >> sb_shift, uf >> sb_shift
    pb = flip(lax.broadcasted_iota(jnp.int32, (n_blk, ck), 0), n_blk)
    pub = flip(lax.broadcasted_iota(jnp.int32, (n_blk, ck), 1), ck) >> sb_shift
    mats = [jnp.where((tb == ub) & (uf <= tf), 1.0, 0.0),
            jnp.where(pub < pb, 1.0, 0.0)]
    for lvl in range(n_lvl):
        mid = ((pb >> (lvl + 1)) << (lvl + 1)) + (1 << lvl)
        mats.append(jnp.where(pub < mid, 1.0, 0.0))
    mats.append(jnp.ones((SUBLANES, ck), F32))
    m_cum = jnp.concatenate(mats, axis=0).astype(BF16)

    def per_block(rows8):
        return jnp.concatenate(
            [jnp.broadcast_to(rows8[jb:jb + 1], (sb, rows8.shape[1])) for jb in range(n_blk)], axis=0)
    row_blk = flip(lax.broadcasted_iota(jnp.int32, (ck, width), 0), ck) >> sb_shift
    upper = [((row_blk >> lvl) & 1) == 1 for lvl in range(n_lvl)]
    pr = flip(lax.broadcasted_iota(jnp.int32, (ck, LANES), 0), ck) >> sb_shift
    pc = flip(lax.broadcasted_iota(jnp.int32, (ck, LANES), 1) & (ck - 1), ck) >> sb_shift
    group_mask = [(pr >> (lvl + 1)) == (pc >> (lvl + 1)) for lvl in range(n_lvl)]
    lane = lax.broadcasted_iota(jnp.int32, (1, LANES), 1)
    head0 = lane < half

    def split_heads(x):
        xb = x.astype(BF16)
        zero = jnp.zeros_like(xb)
        return jnp.concatenate([jnp.where(head0, xb, zero), jnp.where(head0, zero, xb)], axis=0)

    sr = lax.broadcasted_iota(jnp.int32, (LANES, LANES), 0)
    sc = lax.broadcasted_iota(jnp.int32, (LANES, LANES), 1)
    same_head = (sr < half) == (sc < half)
    er = lax.broadcasted_iota(jnp.int32, (sb * LANES, LANES), 0)
    ec = lax.broadcasted_iota(jnp.int32, (sb * LANES, LANES), 1)
    sel = jnp.where(ec == (((er & (LANES - 1)) >> (half.bit_length() - 1)) * half + (er >> (LANES.bit_length() - 1))),
                    1.0, 0.0).astype(BF16)
    sub_row = flip(lax.broadcasted_iota(jnp.int32, (sb, LANES), 0), sb)
    lbv = lb_ref[...]

    def row_start(j):
        return pl.multiple_of(flip(j, n_sub) * ck, ck)

    def stage1a(j):
        r0 = row_start(j)
        q = q_ref[0, pl.ds(r0, ck), :]
        z = f_ref[0, pl.ds(r0, ck), :]
        v = v_ref[0, pl.ds(r0, ck), :]
        f = lbv + (1.0 - lbv) * _sigmoid(z)
        lf2 = jnp.log(f) * LOG2E
        k = 1.0 - f
        return q, v, k, _dot01_exact(m_cum, lf2)

    def stage1b(q, v, k, cums):
        bl2 = cums[0:ck]
        b2 = bl2 + per_block(cums[ck:ck + n_blk])
        tot_row = ck + (1 + n_lvl) * n_blk
        btot2 = cums[tot_row:tot_row + 1]
        log2_k = jnp.log(k) * LOG2E
        kb = b2 - log2_k
        kbl = bl2 - log2_k
        qe = q * jnp.exp2(b2)
        ke = jnp.exp2(btot2 - kb)
        q_lvl, k_lvl = [], []
        for lvl in range(n_lvl):
            split2 = per_block(cums[ck + (1 + lvl) * n_blk:ck + (2 + lvl) * n_blk])
            q_lvl.append(q * jnp.exp2(jnp.where(upper[lvl], b2 - split2, NEG_BIG)))
            k_lvl.append(jnp.exp2(jnp.where(upper[lvl], NEG_BIG, split2 - kb)))

        qe_s[...] = qe.astype(BF16)
        btot_s[...] = btot2
        for p in range(n_pairs):
            sl = slice(p * LANES, (p + 1) * LANES)
            diag_rows = []
            for jb in range(n_blk):
                rs = slice(jb * sb, (jb + 1) * sb)
                bl_b, kbl_b, q_b = bl2[rs, sl], kbl[rs, sl], q[rs, sl]
                terms = []
                for s in range(sb):
                    arg = jnp.where(sub_row >= flip(s, sb), bl_b - kbl_b[s:s + 1], NEG_BIG)
                    terms.append(q_b * jnp.exp2(arg))
                diag_rows.append(jnp.concatenate(terms, axis=1))
            diag_s[p] = _dot(jnp.concatenate(diag_rows, axis=0).astype(BF16), sel)
            for lvl in range(n_lvl):
                k_p = k_lvl[lvl][:, sl]
                lvl_s[p * n_lvl + lvl] = _dot_nt(q_lvl[lvl][:, sl].astype(BF16), split_heads(k_p))
            upd_s[p] = _dot_tn(v[:, sl].astype(BF16), ke[:, sl].astype(BF16))

    def stage2_issue(j):
        r0 = row_start(j)
        v = v_ref[0, pl.ds(r0, ck), :]
        out = []
        for p in range(n_pairs):
            sl = slice(p * LANES, (p + 1) * LANES)
            parts = []
            for jb in range(n_blk):
                blk = diag_s[p, jb * sb:(jb + 1) * sb, :]
                parts.append(pltpu.roll(blk, jb * sb, 1) if jb else blk)
            scores = jnp.concatenate(parts, axis=0)
            for lvl in range(n_lvl):
                s_lvl = lvl_s[p * n_lvl + lvl]
                scores = scores + (s_lvl if lvl == n_lvl - 1 else jnp.where(group_mask[lvl], s_lvl, 0.0))
            intra = _dot(scores.astype(BF16), split_heads(v[:, sl]))
            st = st_ref[p]
            inter = _dot_nt(qe_s[:, sl], st.astype(BF16))
            new_st = jnp.where(same_head, st * jnp.exp2(btot_s[:, sl]) + upd_s[p], 0.0)
            out.append((inter + intra, new_st))
        return r0, out

    def stage2_finish(r0, out):
        for p in range(n_pairs):
            o_ref[0, pl.ds(r0, ck), p * LANES:(p + 1) * LANES] = out[p][0]
            st_ref[p] = out[p][1]

    return stage1a, stage1b, stage2_issue, stage2_finish


N_HGRN_SCRATCH = 6


def _hgrn_kernel(qf_ref, ff_ref, vf_ref, qb_ref, fb_ref, vb_ref, lb_ref, of_ref, ob_ref, zero_ref, *scratch,
                 rows):
    fwd_scratch, bwd_scratch = scratch[:N_HGRN_SCRATCH], scratch[N_HGRN_SCRATCH:]

    @pl.when(pl.program_id(1) == 0)
    def _():
        fwd_scratch[0][...] = jnp.zeros_like(fwd_scratch[0])
        bwd_scratch[0][...] = jnp.zeros_like(bwd_scratch[0])

    f1a, f1b, f2, f3 = _hgrn_direction(False, qf_ref, ff_ref, vf_ref, lb_ref, of_ref, *fwd_scratch, rows=rows)
    b1a, b1b, b2, b3 = _hgrn_direction(True, qb_ref, fb_ref, vb_ref, lb_ref, ob_ref, *bwd_scratch, rows=rows)
    n_sub = rows // HGRN_CHUNK

    def stage1_both(j):
        fa = f1a(j)
        ba = b1a(j)
        f1b(*fa)
        b1b(*ba)

    zero_part = zero_ref.shape[0] // n_sub

    def store_zeros(j):
        zero_ref[pl.ds(pl.multiple_of(j * zero_part, SUBLANES), zero_part), :] = jnp.zeros(
            (zero_part, zero_ref.shape[1]), zero_ref.dtype)

    stage1_both(0)

    def pipelined(j, carry):
        fo = f2(j)
        bo = b2(j)
        store_zeros(j)
        stage1_both(j + 1)
        f3(*fo)
        b3(*bo)
        return carry

    lax.fori_loop(0, n_sub - 1, pipelined, 0)
    fo = f2(n_sub - 1)
    bo = b2(n_sub - 1)
    store_zeros(n_sub - 1)
    f3(*fo)
    b3(*bo)


def _hgrn(proj, lb, zero_rows, *, d_lru, d_hgrn, rows=512):
    bsz, seq, _ = proj.shape
    n_chunks = seq // rows
    zero_blk = zero_rows // (bsz * n_chunks)
    assert zero_blk * bsz * n_chunks == zero_rows and zero_blk % (SUBLANES * (rows // HGRN_CHUNK)) == 0
    assert 2 * (d_hgrn // HGRN_HEADS) == LANES and HGRN_SUB == SUBLANES and HGRN_CHUNK == HGRN_SUB * SUBLANES
    col0 = (2 * d_lru) // d_hgrn
    n_pairs = d_hgrn // LANES
    n_lvl = (HGRN_CHUNK // HGRN_SUB).bit_length() - 1
    fwd = lambda col: pl.BlockSpec((1, rows, d_hgrn), lambda b, c: (b, c, col))
    bwd = lambda col: pl.BlockSpec((1, rows, d_hgrn), lambda b, c: (b, n_chunks - 1 - c, col))
    direction_scratch = [
        pltpu.VMEM((n_pairs, LANES, LANES), F32),
        pltpu.VMEM((n_pairs, HGRN_CHUNK, LANES), F32),
        pltpu.VMEM((n_pairs * n_lvl, HGRN_CHUNK, LANES), F32),
        pltpu.VMEM((n_pairs, LANES, LANES), F32),
        pltpu.VMEM((HGRN_CHUNK, d_hgrn), BF16),
        pltpu.VMEM((1, d_hgrn), F32),
    ]
    assert len(direction_scratch) == N_HGRN_SCRATCH
    kern = functools.partial(_hgrn_kernel, rows=rows)
    out = jax.ShapeDtypeStruct((bsz, seq, d_hgrn), F32)
    return pl.pallas_call(
        kern,
        out_shape=(out, out, jax.ShapeDtypeStruct((zero_rows, LANES), F32)),
        grid=(bsz, n_chunks),
        in_specs=[fwd(col0), fwd(col0 + 1), fwd(col0 + 3), bwd(col0), bwd(col0 + 2), bwd(col0 + 3),
                  pl.BlockSpec((1, d_hgrn), lambda b, c: (0, 0))],
        out_specs=(pl.BlockSpec((1, rows, d_hgrn), lambda b, c: (b, c, 0)),
                   pl.BlockSpec((1, rows, d_hgrn), lambda b, c: (b, n_chunks - 1 - c, 0)),
                   pl.BlockSpec((zero_blk, LANES), lambda b, c: (b * n_chunks + c, 0))),
        scratch_shapes=direction_scratch + direction_scratch,
        compiler_params=_params(("arbitrary", "arbitrary")),
        name="hgrn2",
    )(proj, proj, proj, proj, proj, proj, lb.reshape(1, d_hgrn))


def _interleave(*streams):
    done = object()
    live = list(streams)
    while live:
        live = [s for s in live if next(s, done) is not done]


def _gelu_tanh(y):
    return 0.5 * y * (1.0 + jnp.tanh(0.7978845608028654 * (y + 0.044715 * (y * y * y))))


def _post_kernel(lru_ref, y_ref, of_ref, ob_ref, g_ref, x_ref, nlw_ref, nhw_ref, wo_ref,
                 gm_ref, nfw_ref, scf_ref, shf_ref, wr_ref, br_ref,
                 xo_ref, h_ref, slab_ref, cnt_ref, route_ref, carry_ref, logits_s, *, tm, d_lru):
    step = pl.program_id(0)

    @pl.when(step == 0)
    def _():
        carry_ref[...] = jnp.zeros_like(carry_ref)
        logits_s[...] = jnp.zeros_like(logits_s)

    def mixer():
        lru = lru_ref[0] * _gelu_tanh(y_ref[0])
        ms = jnp.mean(lru * lru, axis=-1, keepdims=True)
        yield
        lru = lru * lax.rsqrt(ms + NORM_EPS) * nlw_ref[...]

        hg = of_ref[0] + ob_ref[0]
        width = hg.shape[1]
        hd = width // HGRN_HEADS
        hd_shift = hd.bit_length() - 1
        er = lax.broadcasted_iota(jnp.int32, (width, width), 0) >> hd_shift
        ec = lax.broadcasted_iota(jnp.int32, (width, width), 1) >> hd_shift
        head_sum = jnp.where(er == ec, 1.0, 0.0).astype(BF16)
        sq = hg * hg
        sq_hi = sq.astype(BF16)
        sq_lo = (sq - sq_hi.astype(F32)).astype(BF16)
        ms_h = (_dot(sq_hi, head_sum) + _dot(sq_lo, head_sum)) * (1.0 / hd)
        yield
        g = g_ref[0]
        hg = (hg * lax.rsqrt(ms_h + NORM_EPS) * nhw_ref[...]) * (g * _sigmoid(g))
        yield
        mixed = _dot(lru.astype(BF16), wo_ref[0:d_lru, :])
        yield
        mixed = mixed + _dot(hg.astype(BF16), wo_ref[d_lru:, :])
        yield
        x_new = x_ref[0] + gm_ref[0] * mixed
        xo_ref[0] = x_new
        ms_f = jnp.mean(x_new * x_new, axis=-1, keepdims=True)
        yield
        h = (x_new * lax.rsqrt(ms_f + NORM_EPS) * nfw_ref[...]) * (1.0 + scf_ref[0]) + shf_ref[0]
        _tiles_store(h_ref, h, tm, lead=(0,))
        yield
        logits_s[...] = _dot(h.astype(BF16), wr_ref[...]) + br_ref[...]

    def routing():
        yield from _routing_steps(logits_s[...], jnp.where(step > 0, 1.0, 0.0), slab_ref, cnt_ref, route_ref,
                                  carry_ref, tm)

    _interleave(routing(), mixer())


def _routing_steps(logits, live, slab_ref, cnt_ref, route_ref, carry_ref, tm):
    lane = lax.broadcasted_iota(jnp.int32, (tm, ROUTE_LANES), 1)
    lane_f = lane.astype(F32)
    far = float(ROUTE_LANES)
    is_g = lane < N_GROUPS
    gl = jnp.where(is_g, logits, NEG_BIG)
    gmax = jnp.max(gl, axis=-1, keepdims=True)
    yield
    g_idx = jnp.min(jnp.where(gl == gmax, lane_f, far), axis=-1, keepdims=True)
    p_group = 1.0 / jnp.sum(jnp.where(is_g, jnp.exp(gl - gmax), 0.0), axis=-1, keepdims=True)
    yield
    e_lane = lane - N_GROUPS
    in_group = (e_lane >= 0) & (e_lane < N_EXPERTS) & ((e_lane >> (EXPERTS_PER_GROUP.bit_length() - 1)).astype(F32) == g_idx)
    ev = jnp.where(in_group, logits, NEG_BIG)
    top1 = jnp.max(ev, axis=-1, keepdims=True)
    yield
    i1 = jnp.min(jnp.where(in_group & (ev == top1), lane_f, far), axis=-1, keepdims=True)
    yield
    rest = in_group & (lane_f != i1)
    ev2 = jnp.where(rest, logits, NEG_BIG)
    top2 = jnp.max(ev2, axis=-1, keepdims=True)
    yield
    i2 = jnp.min(jnp.where(rest & (ev2 == top2), lane_f, far), axis=-1, keepdims=True)
    yield
    e1 = i1 - float(N_GROUPS)
    e2 = i2 - float(N_GROUPS)
    ex = jnp.exp(top2 - top1)
    w1 = p_group / (1.0 + ex)
    w2 = p_group * ex / (1.0 + ex)

    sel1 = lane_f == e1
    sel2 = lane_f == e2
    onehot = jnp.where(sel1 | sel2, live, 0.0)
    tr = lax.broadcasted_iota(jnp.int32, (tm, tm), 0)
    tc = lax.broadcasted_iota(jnp.int32, (tm, tm), 1)
    before = jnp.where(tc < tr, 1.0, 0.0).astype(BF16)
    cnt = _dot(before, onehot.astype(BF16)) + carry_ref[0:1]
    yield
    rank1 = jnp.sum(jnp.where(sel1, cnt, 0.0), axis=-1, keepdims=True)
    rank2 = jnp.sum(jnp.where(sel2, cnt, 0.0), axis=-1, keepdims=True)
    total = carry_ref[0:1] + jnp.sum(onehot, axis=0, keepdims=True)
    carry_ref[...] = jnp.broadcast_to(total, carry_ref.shape)
    cnt_ref[...] = jnp.broadcast_to(total, cnt_ref.shape)
    yield

    slab = jnp.where(lane == 0, e1, 0.0)
    slab = jnp.where(lane == 1, e2, slab)
    slab = jnp.where(lane == 2, w1, slab)
    slab = jnp.where(lane == 3, w2, slab)
    slab = jnp.where(lane == 4, rank1, slab)
    slab = jnp.where(lane == 5, rank2, slab)
    slab_ref[0] = slab
    route_ref[...] = slab.T[0:SUBLANES]


def _post_mixer(lru_sum, proj, hg_f, hg_b, x, nlw, nhw, wo_bf16, g_mix, nfw, sc_ffn, sh_ffn, wr_bf16, br,
                *, tm=512):
    bsz, seq, d = x.shape
    d_lru = lru_sum.shape[-1]
    d_hgrn = hg_f.shape[-1]
    y_col = 1
    g_col = (2 * d_lru) // d_hgrn + 4
    tiles = seq // tm
    n_tiles = bsz * tiles
    cur = lambda s: jnp.minimum(s, n_tiles - 1)
    prev = lambda s: jnp.maximum(s - 1, 0)
    row = lambda w, col=0: pl.BlockSpec((1, tm, w), lambda s: (cur(s) // tiles, cur(s) % tiles, col))
    vec = lambda w: pl.BlockSpec((1, w), lambda s: (0, 0))
    per_b = lambda: pl.BlockSpec((1, 1, d), lambda s: (cur(s) // tiles, 0, 0))
    kern = functools.partial(_post_kernel, tm=tm, d_lru=d_lru)
    return pl.pallas_call(
        kern,
        out_shape=(
            jax.ShapeDtypeStruct((bsz, seq, d), F32),
            jax.ShapeDtypeStruct((bsz, seq * SUBLANES, LANES), F32),
            jax.ShapeDtypeStruct((bsz, seq, ROUTE_LANES), F32),
            jax.ShapeDtypeStruct((SUBLANES, ROUTE_LANES), F32),
            jax.ShapeDtypeStruct((SUBLANES, bsz * seq), F32),
        ),
        grid=(n_tiles + 1,),
        in_specs=[
            row(d_lru), row(d_lru, y_col),
            row(d_hgrn), row(d_hgrn), row(d_hgrn, g_col),
            row(d), vec(d_lru), vec(d_hgrn),
            pl.BlockSpec((d, d), lambda s: (0, 0)),
            per_b(), vec(d), per_b(), per_b(),
            pl.BlockSpec((d, ROUTE_LANES), lambda s: (0, 0)),
            vec(ROUTE_LANES),
        ],
        out_specs=(
            row(d),
            pl.BlockSpec((1, tm * SUBLANES, LANES), lambda s: (cur(s) // tiles, cur(s) % tiles, 0)),
            pl.BlockSpec((1, tm, ROUTE_LANES), lambda s: (prev(s) // tiles, prev(s) % tiles, 0)),
            pl.BlockSpec((SUBLANES, ROUTE_LANES), lambda s: (0, 0)),
            pl.BlockSpec((SUBLANES, tm), lambda s: (0, prev(s))),
        ),
        scratch_shapes=[pltpu.VMEM((SUBLANES, ROUTE_LANES), F32), pltpu.VMEM((tm, ROUTE_LANES), F32)],
        compiler_params=_params(("arbitrary",)),
        name="post_mixer_router",
    )(lru_sum, proj, hg_f, hg_b, proj, x, nlw.reshape(1, d_lru), nhw.reshape(1, d_hgrn), wo_bf16,
      g_mix, nfw.reshape(1, d), sc_ffn, sh_ffn, wr_bf16, br)


def _tiles_load(ref, n, lead=()):
    return jnp.concatenate(
        [ref[(*lead, pl.ds(j, n, stride=SUBLANES), slice(None))] for j in range(SUBLANES)], axis=1)


def _tiles_store(ref, val, n, lead=()):
    for j in range(SUBLANES):
        ref[(*lead, pl.ds(j, n, stride=SUBLANES), slice(None))] = val[:, j * LANES:(j + 1) * LANES]


def _token_tile(ref, t):
    return ref.at[pl.ds(pl.multiple_of(t * SUBLANES, SUBLANES), SUBLANES)]


def _dest_kernel(start_ref, route_ref, o_ref):
    route = route_ref[...].astype(jnp.int32)
    start = jnp.zeros_like(route)
    for e in range(N_EXPERTS):
        start = jnp.where(route == e, start_ref[e], start)
    o_ref[...] = start + pltpu.roll(route, SUBLANES // 2, 0)


def _dest_rows(expert_start, route):
    return pl.pallas_call(
        _dest_kernel,
        out_shape=jax.ShapeDtypeStruct(route.shape, jnp.int32),
        grid_spec=pltpu.PrefetchScalarGridSpec(
            num_scalar_prefetch=1,
            grid=(1,),
            in_specs=[pl.BlockSpec(route.shape, lambda i, s: (0, 0))],
            out_specs=pl.BlockSpec(route.shape, lambda i, s: (0, 0)),
        ),
        compiler_params=pltpu.CompilerParams(dimension_semantics=("arbitrary",)),
        name="moe_dest_rows",
    )(expert_start, route)


def _dispatch_kernel(d1_ref, d2_ref, h_ref, z_ref, o_ref, sem, *, tb):
    del z_ref
    base = pl.program_id(0) * tb

    def issue(r, carry):
        t = base + r
        pltpu.make_async_copy(_token_tile(h_ref, r), _token_tile(o_ref, d1_ref[t]), sem).start(priority=0)
        pltpu.make_async_copy(_token_tile(h_ref, r), _token_tile(o_ref, d2_ref[t]), sem).start(priority=1)
        return carry

    lax.fori_loop(0, tb, issue, 0, unroll=DMA_ISSUE_UNROLL)
    for _ in range(2):
        pltpu.make_async_copy(h_ref, o_ref.at[pl.ds(0, tb * SUBLANES)], sem).wait()


def _dispatch(dest1, dest2, h_tiles, zero_tiles, *, tb=512):
    m = h_tiles.shape[0] // SUBLANES
    n_rows = zero_tiles.shape[0] // SUBLANES
    kern = functools.partial(_dispatch_kernel, tb=tb)
    return pl.pallas_call(
        kern,
        out_shape=jax.ShapeDtypeStruct((n_rows * SUBLANES, LANES), h_tiles.dtype),
        grid_spec=pltpu.PrefetchScalarGridSpec(
            num_scalar_prefetch=2,
            grid=(m // tb,),
            in_specs=[pl.BlockSpec((tb * SUBLANES, LANES), lambda i, d1, d2: (i, 0)),
                      pl.BlockSpec(memory_space=pl.ANY)],
            out_specs=pl.BlockSpec(memory_space=pl.ANY),
            scratch_shapes=[pltpu.SemaphoreType.DMA(())],
        ),
        input_output_aliases={3: 0},
        compiler_params=pltpu.CompilerParams(dimension_semantics=("arbitrary",), has_side_effects=True),
        name="moe_dispatch",
    )(dest1, dest2, h_tiles, zero_tiles)


def _expert_kernel(plan_ref, x_ref, wg_hbm, wu_hbm, wd_hbm, o_ref, wg_f, wu_f, wd_f, wg_s, wu_s, wd_s, sem,
                   *, layer):
    i = pl.program_id(0)
    n_blocks = pl.num_programs(0)
    n_used = plan_ref[n_blocks]
    expert = plan_ref[i]
    next_expert = plan_ref[n_blocks + 1 + i]
    slot = plan_ref[2 * n_blocks + 1 + i]
    first_block = ((i == 0) | (plan_ref[jnp.maximum(i - 1, 0)] != expert)) & (i < n_used)

    def copies(e, s):
        return (pltpu.make_async_copy(wg_hbm.at[layer, e], wg_f.at[s], sem.at[s]),
                pltpu.make_async_copy(wu_hbm.at[layer, e], wu_f.at[s], sem.at[s]),
                pltpu.make_async_copy(wd_hbm.at[layer, e], wd_f.at[s], sem.at[s]))

    @pl.when(i == 0)
    def _():
        for c in copies(expert, slot):
            c.start()

    @pl.when(first_block)
    def _():
        for c in copies(expert, slot):
            c.wait()
        wg_s[...] = wg_f[slot].astype(BF16)
        wu_s[...] = wu_f[slot].astype(BF16)
        wd_s[...] = wd_f[slot].astype(BF16)

        @pl.when(next_expert >= 0)
        def _():
            for c in copies(next_expert, 1 - slot):
                c.start()

    blk = x_ref.shape[0] // SUBLANES

    @pl.when(i < n_used)
    def _():
        x = _tiles_load(x_ref, blk).astype(BF16)
        gate = _dot(x, wg_s[...])
        up = _dot(x, wu_s[...])
        act = (gate * _sigmoid(gate)) * up
        _tiles_store(o_ref, _dot(act.astype(BF16), wd_s[...]), blk)

    @pl.when(i >= n_used)
    def _():
        o_ref[...] = jnp.zeros_like(o_ref)


def _expert_plan(blk_expert, blocks_used):
    n_blocks = blk_expert.shape[0]
    idx = jnp.arange(n_blocks, dtype=jnp.int32)
    change = jnp.concatenate([jnp.ones((1,), bool), blk_expert[1:] != blk_expert[:-1]])
    slot = (jnp.cumsum(change.astype(jnp.int32)) - 1) & 1
    change_at = jnp.where(change, idx, n_blocks)
    from_here = lax.cummin(change_at[::-1])[::-1]
    next_change = jnp.concatenate([from_here[1:], jnp.full((1,), n_blocks, jnp.int32)])
    next_expert = jnp.where(next_change < blocks_used, blk_expert[jnp.minimum(next_change, n_blocks - 1)], -1)
    return jnp.concatenate([blk_expert, blocks_used.reshape(1), next_expert, slot]).astype(jnp.int32)


def _experts(plan, x_tiles, wg, wu, wd, layer):
    n_rows = x_tiles.shape[0] // SUBLANES
    d, de = wg.shape[-2:]
    blk = EXPERT_BLOCK
    n_blocks = n_rows // blk
    kern = functools.partial(_expert_kernel, layer=layer)
    return pl.pallas_call(
        kern,
        out_shape=jax.ShapeDtypeStruct((n_rows * SUBLANES, LANES), F32),
        grid_spec=pltpu.PrefetchScalarGridSpec(
            num_scalar_prefetch=1,
            grid=(n_blocks,),
            in_specs=[
                pl.BlockSpec((blk * SUBLANES, LANES), lambda i, plan: (jnp.minimum(i, plan[n_blocks] - 1), 0)),
                pl.BlockSpec(memory_space=pl.ANY),
                pl.BlockSpec(memory_space=pl.ANY),
                pl.BlockSpec(memory_space=pl.ANY),
            ],
            out_specs=pl.BlockSpec((blk * SUBLANES, LANES), lambda i, plan: (i, 0)),
            scratch_shapes=[
                pltpu.VMEM((2, d, de), F32), pltpu.VMEM((2, d, de), F32), pltpu.VMEM((2, de, d), F32),
                pltpu.VMEM((d, de), BF16), pltpu.VMEM((d, de), BF16), pltpu.VMEM((de, d), BF16),
                pltpu.SemaphoreType.DMA((2,)),
            ],
        ),
        compiler_params=_params(("arbitrary",)),
        name="moe_experts",
    )(plan, x_tiles, wg, wu, wd)


def _combine_kernel(d1_ref, d2_ref, y_ref, slab_ref, x_ref, g_ref, nw_ref, o_ref, ra0, rb0, ra1, rb1, sem,
                    *, tm, tiles, n_steps, final_norm):
    step = pl.program_id(0) * tiles + pl.program_id(1)
    bufs = ((ra0, rb0), (ra1, rb1))

    def gather(tile, slot):
        base = tile * tm
        r1_ref, r2_ref = bufs[slot]

        def issue(r, carry):
            t = base + r
            pltpu.make_async_copy(_token_tile(y_ref, d1_ref[t]), _token_tile(r1_ref, r),
                                  sem.at[slot]).start(priority=0)
            pltpu.make_async_copy(_token_tile(y_ref, d2_ref[t]), _token_tile(r2_ref, r),
                                  sem.at[slot]).start(priority=1)
            return carry

        lax.fori_loop(0, tm, issue, 0, unroll=DMA_ISSUE_UNROLL)

    @pl.when(step == 0)
    def _():
        gather(0, 0)

    for slot in range(2):
        @pl.when((step & 1) == slot)
        def _():
            @pl.when(step + 1 < n_steps)
            def _():
                gather(step + 1, 1 - slot)

            r1_ref, r2_ref = bufs[slot]
            pltpu.make_async_copy(y_ref.at[pl.ds(0, tm * SUBLANES)], r1_ref, sem.at[slot]).wait()
            pltpu.make_async_copy(y_ref.at[pl.ds(0, tm * SUBLANES)], r2_ref, sem.at[slot]).wait()
            slab = slab_ref[0]
            y = slab[:, 2:3] * _tiles_load(r1_ref, tm) + slab[:, 3:4] * _tiles_load(r2_ref, tm)
            out = x_ref[0] + g_ref[0] * y
            if final_norm:
                ms = jnp.mean(out * out, axis=-1, keepdims=True)
                out = out * lax.rsqrt(ms + NORM_EPS) * nw_ref[...]
            o_ref[0] = out


def _combine(dest1, dest2, y_buf, slab, x, g_ffn, norm_w, *, final_norm, tm=512):
    bsz, seq, d = x.shape
    tiles = seq // tm
    kern = functools.partial(_combine_kernel, tm=tm, tiles=tiles, n_steps=bsz * tiles, final_norm=final_norm)
    row_buf = pltpu.VMEM((tm * SUBLANES, LANES), F32)
    return pl.pallas_call(
        kern,
        out_shape=jax.ShapeDtypeStruct((bsz, seq, d), F32),
        grid_spec=pltpu.PrefetchScalarGridSpec(
            num_scalar_prefetch=2,
            grid=(bsz, tiles),
            in_specs=[
                pl.BlockSpec(memory_space=pl.ANY),
                pl.BlockSpec((1, tm, ROUTE_LANES), lambda b, i, d1, d2: (b, i, 0)),
                pl.BlockSpec((1, tm, d), lambda b, i, d1, d2: (b, i, 0)),
                pl.BlockSpec((1, 1, d), lambda b, i, d1, d2: (b, 0, 0)),
                pl.BlockSpec((1, d), lambda b, i, d1, d2: (0, 0)),
            ],
            out_specs=pl.BlockSpec((1, tm, d), lambda b, i, d1, d2: (b, i, 0)),
            scratch_shapes=[row_buf, row_buf, row_buf, row_buf, pltpu.SemaphoreType.DMA((2,))],
        ),
        compiler_params=_params(("arbitrary", "arbitrary")),
        name="moe_combine",
    )(dest1, dest2, y_buf, slab, x, g_ffn, norm_w.reshape(1, d))


def _block_diag(w):
    heads, hd, _ = w.shape
    n = heads * hd
    tiled = jnp.tile(w.reshape(n, hd), (1, heads))
    blk_r = lax.broadcasted_iota(jnp.int32, (n, n), 0) // hd
    blk_c = lax.broadcasted_iota(jnp.int32, (n, n), 1) // hd
    return jnp.where(blk_r == blk_c, tiled, 0.0)


def kernel(x, c, ada_w, ada_b, norm_mix_w, w_in, conv_w, conv_b, lru_wa, lru_ba, lru_wx, lru_bx, lru_lambda, norm_lru_w, hgrn_lb, norm_hgrn_w, w_out, norm_ffn_w, router_group_w, router_group_b, router_expert_w, router_expert_b, expert_w_gate, expert_w_up, expert_w_down, final_norm_w):
    bsz, seq, d = x.shape
    assert d == SUBLANES * LANES, "the MoE row movement keeps one (8, 128) tile per token"
    depth = ada_w.shape[0]
    d_lru = conv_w.shape[-1]
    d_hgrn = hgrn_lb.shape[-1]
    m = bsz * seq
    n_rows = m * 2 + N_EXPERTS * EXPERT_BLOCK
    n_blocks = n_rows // EXPERT_BLOCK

    mod = _modulation(c, ada_w, ada_b)
    lb_cum = jnp.cumsum(jax.nn.softmax(hgrn_lb.astype(F32), axis=0), axis=0)
    lb_all = lb_cum - lb_cum[0:1]

    for l in range(depth):
        sh_mix, sc_mix, g_mix, sh_ffn, sc_ffn, g_ffn = [
            mod[l, :, i * d:(i + 1) * d].reshape(bsz, 1, d) for i in range(N_MODULATIONS)]
        proj = _in_proj(x, norm_mix_w[l], sc_mix, sh_mix, w_in, l)
        wa_bd = jnp.stack([_block_diag(lru_wa[l, 0]), _block_diag(lru_wa[l, 1])]).astype(BF16)
        wx_bd = jnp.stack([_block_diag(lru_wx[l, 0]), _block_diag(lru_wx[l, 1])]).astype(BF16)
        lru_args = (proj, conv_w[l], conv_b[l], wa_bd, lru_ba[l], wx_bd, lru_bx[l], lru_lambda[l])
        lru_sum = _lru_scan(*lru_args, reverse=True, add_to=_lru_scan(*lru_args, reverse=False))
        hg_f, hg_b, zero_tiles = _hgrn(proj, lb_all[l], n_rows * SUBLANES, d_lru=d_lru, d_hgrn=d_hgrn)

        lane_pad = ROUTE_LANES - N_GROUPS - N_EXPERTS
        wr = jnp.pad(jnp.concatenate([router_group_w[l], router_expert_w[l]], axis=1), ((0, 0), (0, lane_pad)))
        br = jnp.pad(jnp.concatenate([router_group_b[l], router_expert_b[l]]), (0, lane_pad)).reshape(1, ROUTE_LANES)
        x_mid, h_ffn, slab, counts, route = _post_mixer(
            lru_sum, proj, hg_f, hg_b, x, norm_lru_w[l], norm_hgrn_w[l], w_out[l].astype(BF16), g_mix,
            norm_ffn_w[l], sc_ffn, sh_ffn, wr.astype(BF16), br)

        cnt = counts[0, :N_EXPERTS].astype(jnp.int32)
        padded = ((cnt + EXPERT_BLOCK - 1) // EXPERT_BLOCK) * EXPERT_BLOCK
        pend = jnp.cumsum(padded)
        pstart = pend - padded
        blk_start = jnp.arange(n_blocks, dtype=jnp.int32) * EXPERT_BLOCK
        blk_expert = jnp.minimum(jnp.sum(pend[None, :] <= blk_start[:, None], axis=1), N_EXPERTS - 1)
        plan = _expert_plan(blk_expert.astype(jnp.int32), (pend[N_EXPERTS - 1] // EXPERT_BLOCK).astype(jnp.int32))
        dest = _dest_rows(pstart.astype(jnp.int32), route)
        dest1, dest2 = dest[0], dest[1]

        x_buf = _dispatch(dest1, dest2, h_ffn.reshape(m * SUBLANES, LANES), zero_tiles)
        y_buf = _experts(plan, x_buf, expert_w_gate, expert_w_up, expert_w_down, l)
        x = _combine(dest1, dest2, y_buf, slab, x_mid, g_ffn, final_norm_w, final_norm=(l == depth - 1))

    return x
```

```python
import functools

import jax
import jax.numpy as jnp
from jax import lax
from jax.experimental import pallas as pl
from jax.experimental.pallas import tpu as pltpu

F32 = jnp.float32
BF16 = jnp.bfloat16

HGRN_HEADS = 8
N_MODULATIONS = 6
CONV_WIDTH = 4
LRU_C = 8.0
N_GROUPS = 4
EXPERTS_PER_GROUP = 8
N_EXPERTS = N_GROUPS * EXPERTS_PER_GROUP
NORM_EPS = 1e-6

LANES = 128
SUBLANES = 8
VMEM_LIMIT = 56 * 1024 * 1024

HGRN_CHUNK = 64
HGRN_SUB = 8
LOG2E = 1.4426950408889634
ROUTE_LANES = LANES
EXPERT_BLOCK = 512
DMA_ISSUE_UNROLL = 8
NEG_BIG = -3.0e38


def _params(sem):
    return pltpu.CompilerParams(dimension_semantics=sem, vmem_limit_bytes=VMEM_LIMIT)


def _dot(a, b):
    return jnp.dot(a, b, preferred_element_type=F32)


def _dot_nt(a, b):
    return lax.dot_general(a, b, (((1,), (1,)), ((), ())), preferred_element_type=F32)


def _dot_tn(a, b):
    return lax.dot_general(a, b, (((0,), (0,)), ((), ())), preferred_element_type=F32)


def _dot01_exact(m01, x):
    hi = x.astype(BF16)
    r1 = x - hi.astype(F32)
    mid = r1.astype(BF16)
    lo = (r1 - mid.astype(F32)).astype(BF16)
    return _dot(m01, hi) + _dot(m01, mid) + _dot(m01, lo)


def _sigmoid(x):
    return 1.0 / (1.0 + jnp.exp(-x))


def _sigmoid_tanh(x):
    return 0.5 * jnp.tanh(0.5 * x) + 0.5


def _mod_kernel(c_ref, w_ref, b_ref, o_ref):
    c = c_ref[...]
    cond = c * _sigmoid(c)
    o_ref[0] = _dot(cond.astype(BF16), w_ref[0].astype(BF16)) + b_ref[0]


def _modulation(c, ada_w, ada_b):
    depth, d, n = ada_w.shape
    bsz = c.shape[0]
    rows = -(-bsz // SUBLANES) * SUBLANES
    c_pad = jnp.pad(c, ((0, rows - bsz), (0, 0)))
    tn = n // N_MODULATIONS
    out = pl.pallas_call(
        _mod_kernel,
        out_shape=jax.ShapeDtypeStruct((depth, rows, n), F32),
        grid=(depth, n // tn),
        in_specs=[
            pl.BlockSpec((rows, d), lambda l, j: (0, 0)),
            pl.BlockSpec((1, d, tn), lambda l, j: (l, 0, j)),
            pl.BlockSpec((1, 1, tn), lambda l, j: (l, 0, j)),
        ],
        out_specs=pl.BlockSpec((1, rows, tn), lambda l, j: (l, 0, j)),
        compiler_params=_params(("arbitrary", "arbitrary")),
        name="adaln_mod",
    )(c_pad, ada_w, ada_b.reshape(depth, 1, n))
    return out[:, :bsz]


def _rms_mod(x, nw, sc, sh):
    ms = jnp.mean(x * x, axis=-1, keepdims=True)
    return (x * lax.rsqrt(ms + NORM_EPS) * nw) * (1.0 + sc) + sh


def _inproj_kernel(x_ref, nw_ref, sc_ref, sh_ref, w_ref, o_ref, w_s):
    @pl.when((pl.program_id(0) == 0) & (pl.program_id(1) == 0))
    def _():
        w_s[...] = w_ref[0].astype(BF16)

    h = _rms_mod(x_ref[0], nw_ref[...], sc_ref[0], sh_ref[0])
    o_ref[0] = _dot(h.astype(BF16), w_s[...])


def _in_proj(x, nw, sc, sh, w_in, layer, tm=512):
    bsz, seq, d = x.shape
    n = w_in.shape[-1]
    return pl.pallas_call(
        _inproj_kernel,
        out_shape=jax.ShapeDtypeStruct((bsz, seq, n), F32),
        grid=(bsz, seq // tm),
        in_specs=[
            pl.BlockSpec((1, tm, d), lambda b, i: (b, i, 0)),
            pl.BlockSpec((1, d), lambda b, i: (0, 0)),
            pl.BlockSpec((1, 1, d), lambda b, i: (b, 0, 0)),
            pl.BlockSpec((1, 1, d), lambda b, i: (b, 0, 0)),
            pl.BlockSpec((1, d, n), lambda b, i: (layer, 0, 0), pipeline_mode=pl.Buffered(1)),
        ],
        out_specs=pl.BlockSpec((1, tm, n), lambda b, i: (b, i, 0)),
        scratch_shapes=[pltpu.VMEM((d, n), BF16)],
        compiler_params=_params(("arbitrary", "arbitrary")),
        name="in_proj",
    )(x, nw.reshape(1, d), sc, sh, w_in)


def _lru_kernel(x_ref, xp_ref, xn_ref, cw_ref, cb_ref, wa_ref, ba_ref, wx_ref, bx_ref, lam_ref, *rest,
                reverse, n_chunks, rows, accumulate):
    add_ref = rest[0] if accumulate else None
    o_ref, carry_ref, sa_ref, sb_ref, cin_ref = rest[1:] if accumulate else rest
    c = pl.program_id(1)
    chunk = (n_chunks - 1 - c) if reverse else c

    @pl.when(c == 0)
    def _():
        carry_ref[...] = jnp.zeros_like(carry_ref)

    x = x_ref[0]
    width = x.shape[1]
    has_prev = jnp.where(chunk > 0, 1.0, 0.0)
    has_next = jnp.where(chunk < n_chunks - 1, 1.0, 0.0)
    xp = xp_ref[0] * has_prev
    xn = xn_ref[0] * has_next
    xe = jnp.concatenate([xp, x, xn], axis=0)
    cw = cw_ref[...]
    xc = cb_ref[...]
    for k in range(CONV_WIDTH):
        lo = SUBLANES + k - CONV_WIDTH // 2
        xc = xc + cw[k:k + 1] * xe[lo:lo + rows]

    xcb = xc.astype(BF16)
    r = _sigmoid_tanh(_dot(xcb, wa_ref[0]) + ba_ref[0])
    gate_i = _sigmoid_tanh(_dot(xcb, wx_ref[0]) + bx_ref[0])
    lam = lam_ref[0]
    softplus_neg_lam = jnp.maximum(-lam, 0.0) + jnp.log1p(jnp.exp(-jnp.abs(lam)))
    log_a = (-LRU_C) * r * softplus_neg_lam
    a = jnp.exp(log_a)
    t = jnp.tanh(-log_a)
    u = jnp.sqrt(2.0 * t / (1.0 + t)) * (gate_i * xc)

    groups = rows // SUBLANES
    acc_a = a.reshape(groups, SUBLANES, width)
    acc_b = u.reshape(groups, SUBLANES, width)
    sub = lax.broadcasted_iota(jnp.int32, (groups, SUBLANES, width), 1)
    s = 1
    while s < SUBLANES:
        if reverse:
            valid = sub < SUBLANES - s
            sh_a, sh_b = pltpu.roll(acc_a, SUBLANES - s, 1), pltpu.roll(acc_b, SUBLANES - s, 1)
        else:
            valid = sub >= s
            sh_a, sh_b = pltpu.roll(acc_a, s, 1), pltpu.roll(acc_b, s, 1)
        acc_b = jnp.where(valid, acc_a * sh_b + acc_b, acc_b)
        acc_a = jnp.where(valid, acc_a * sh_a, acc_a)
        s *= 2
    acc_a = acc_a.reshape(rows, width)
    acc_b = acc_b.reshape(rows, width)
    edge = 0 if reverse else SUBLANES - 1
    n_tiles = width // LANES
    for j in range(n_tiles):
        sa_ref[j] = acc_a[:, j * LANES:(j + 1) * LANES]
        sb_ref[j] = acc_b[:, j * LANES:(j + 1) * LANES]
    ea = jnp.concatenate([sa_ref[j, pl.ds(edge, groups, stride=SUBLANES), :] for j in range(n_tiles)], axis=1)
    eb = jnp.concatenate([sb_ref[j, pl.ds(edge, groups, stride=SUBLANES), :] for j in range(n_tiles)], axis=1)
    grow = lax.broadcasted_iota(jnp.int32, (groups, width), 0)
    s = 1
    while s < groups:
        if reverse:
            valid = grow < groups - s
            sh_a, sh_b = pltpu.roll(ea, groups - s, 0), pltpu.roll(eb, groups - s, 0)
        else:
            valid = grow >= s
            sh_a, sh_b = pltpu.roll(ea, s, 0), pltpu.roll(eb, s, 0)
        eb = jnp.where(valid, ea * sh_b + eb, eb)
        ea = jnp.where(valid, ea * sh_a, ea)
        s *= 2
    carry = carry_ref[...]
    group_out = eb + ea * carry
    if reverse:
        carry_in = jnp.where(grow == groups - 1, carry, pltpu.roll(group_out, groups - 1, 0))
        carry_ref[...] = group_out[0:1]
    else:
        carry_in = jnp.where(grow == 0, carry, pltpu.roll(group_out, 1, 0))
        carry_ref[...] = group_out[groups - 1:groups]
    cin_ref[...] = carry_in
    for g in range(groups):
        rs = slice(g * SUBLANES, (g + 1) * SUBLANES)
        h_rows = acc_b[rs] + acc_a[rs] * cin_ref[g:g + 1, :]
        o_ref[0, rs, :] = (add_ref[0, rs, :] + h_rows) if accumulate else h_rows


def _lru_scan(proj, conv_w, conv_b, wa_bd, ba, wx_bd, bx, lam, *, reverse, add_to=None, rows=512):
    bsz, seq, _ = proj.shape
    d_lru = conv_w.shape[1]
    n_chunks = seq // rows
    halo = rows // SUBLANES
    last_halo = seq // SUBLANES - 1
    dirn = 1 if reverse else 0

    def chunk_of(c):
        return (n_chunks - 1 - c) if reverse else c

    vec = lambda: pl.BlockSpec((1, 1, d_lru), lambda b, c: (dirn, 0, 0))
    mat = lambda: pl.BlockSpec((1, d_lru, d_lru), lambda b, c: (dirn, 0, 0))
    accumulate = add_to is not None
    kern = functools.partial(_lru_kernel, reverse=reverse, n_chunks=n_chunks, rows=rows, accumulate=accumulate)
    tile = pl.BlockSpec((1, rows, d_lru), lambda b, c: (b, chunk_of(c), 0))
    return pl.pallas_call(
        kern,
        out_shape=jax.ShapeDtypeStruct((bsz, seq, d_lru), F32),
        grid=(bsz, n_chunks),
        in_specs=[
            pl.BlockSpec((1, rows, d_lru), lambda b, c: (b, chunk_of(c), 0)),
            pl.BlockSpec((1, SUBLANES, d_lru),
                         lambda b, c: (b, jnp.maximum(chunk_of(c) * halo - 1, 0), 0)),
            pl.BlockSpec((1, SUBLANES, d_lru),
                         lambda b, c: (b, jnp.minimum((chunk_of(c) + 1) * halo, last_halo), 0)),
            pl.BlockSpec((CONV_WIDTH, d_lru), lambda b, c: (0, 0)),
            pl.BlockSpec((1, d_lru), lambda b, c: (0, 0)),
            mat(), vec(), mat(), vec(), vec(),
        ] + ([tile] if accumulate else []),
        out_specs=tile,
        scratch_shapes=[pltpu.VMEM((1, d_lru), F32), pltpu.VMEM((d_lru // LANES, rows, LANES), F32),
                        pltpu.VMEM((d_lru // LANES, rows, LANES), F32),
                        pltpu.VMEM((rows // SUBLANES, d_lru), F32)],
        compiler_params=_params(("arbitrary", "arbitrary")),
        name="lru_bwd" if reverse else "lru_fwd",
    )(proj, proj, proj, conv_w, conv_b.reshape(1, d_lru), wa_bd, ba.reshape(2, 1, d_lru),
      wx_bd, bx.reshape(2, 1, d_lru), lam.reshape(2, 1, d_lru), *([add_to] if accumulate else []))


def _hgrn_direction(rev, q_ref, f_ref, v_ref, lb_ref, o_ref, st_ref, diag_s, lvl_s, upd_s, qe_s, btot_s, *, rows):
    ck, sb = HGRN_CHUNK, HGRN_SUB
    n_blk = ck // sb
    sb_shift = sb.bit_length() - 1
    n_sub = rows // ck
    width = q_ref.shape[-1]
    n_pairs = width // LANES
    half = LANES // 2

    def flip(idx, n):
        return (n - 1 - idx) if rev else idx

    n_lvl = n_blk.bit_length() - 1
    tf = flip(lax.broadcasted_iota(jnp.int32, (ck, ck), 0), ck)
    uf = flip(lax.broadcasted_iota(jnp.int32, (ck, ck), 1), ck)
    tb, ub = tf >> sb_shift, uf >> sb_shift
    pb = flip(lax.broadcasted_iota(jnp.int32, (n_blk, ck), 0), n_blk)
    pub = flip(lax.broadcasted_iota(jnp.int32, (n_blk, ck), 1), ck) >> sb_shift
    mats = [jnp.where((tb == ub) & (uf <= tf), 1.0, 0.0),
            jnp.where(pub < pb, 1.0, 0.0)]
    for lvl in range(n_lvl):
        mid = ((pb >> (lvl + 1)) << (lvl + 1)) + (1 << lvl)
        mats.append(jnp.where(pub < mid, 1.0, 0.0))
    mats.append(jnp.ones((SUBLANES, ck), F32))
    m_cum = jnp.concatenate(mats, axis=0).astype(BF16)

    def per_block(rows8):
        return jnp.concatenate(
            [jnp.broadcast_to(rows8[jb:jb + 1], (sb, rows8.shape[1])) for jb in range(n_blk)], axis=0)
    row_blk = flip(lax.broadcasted_iota(jnp.int32, (ck, width), 0), ck) >> sb_shift
    upper = [((row_blk >> lvl) & 1) == 1 for lvl in range(n_lvl)]
    pr = flip(lax.broadcasted_iota(jnp.int32, (ck, LANES), 0), ck) >> sb_shift
    pc = flip(lax.broadcasted_iota(jnp.int32, (ck, LANES), 1) & (ck - 1), ck) >> sb_shift
    group_mask = [(pr >> (lvl + 1)) == (pc >> (lvl + 1)) for lvl in range(n_lvl)]
    lane = lax.broadcasted_iota(jnp.int32, (1, LANES), 1)
    head0 = lane < half

    def split_heads(x):
        xb = x.astype(BF16)
        zero = jnp.zeros_like(xb)
        return jnp.concatenate([jnp.where(head0, xb, zero), jnp.where(head0, zero, xb)], axis=0)

    sr = lax.broadcasted_iota(jnp.int32, (LANES, LANES), 0)
    sc = lax.broadcasted_iota(jnp.int32, (LANES, LANES), 1)
    same_head = (sr < half) == (sc < half)
    er = lax.broadcasted_iota(jnp.int32, (sb * LANES, LANES), 0)
    ec = lax.broadcasted_iota(jnp.int32, (sb * LANES, LANES), 1)
    sel = jnp.where(ec == (((er & (LANES - 1)) >> (half.bit_length() - 1)) * half + (er >> (LANES.bit_length() - 1))),
                    1.0, 0.0).astype(BF16)
    sub_row = flip(lax.broadcasted_iota(jnp.int32, (sb, LANES), 0), sb)
    lbv = lb_ref[...]

    def row_start(j):
        return pl.multiple_of(flip(j, n_sub) * ck, ck)

    def stage1a(j):
        r0 = row_start(j)
        q = q_ref[0, pl.ds(r0, ck), :]
        z = f_ref[0, pl.ds(r0, ck), :]
        v = v_ref[0, pl.ds(r0, ck), :]
        f = lbv + (1.0 - lbv) * _sigmoid(z)
        lf2 = jnp.log(f) * LOG2E
        k = 1.0 - f
        return q, v, k, _dot01_exact(m_cum, lf2)

    def stage1b(q, v, k, cums):
        bl2 = cums[0:ck]
        b2 = bl2 + per_block(cums[ck:ck + n_blk])
        tot_row = ck + (1 + n_lvl) * n_blk
        btot2 = cums[tot_row:tot_row + 1]
        log2_k = jnp.log(k) * LOG2E
        kb = b2 - log2_k
        kbl = bl2 - log2_k
        qe = q * jnp.exp2(b2)
        ke = jnp.exp2(btot2 - kb)
        q_lvl, k_lvl = [], []
        for lvl in range(n_lvl):
            split2 = per_block(cums[ck + (1 + lvl) * n_blk:ck + (2 + lvl) * n_blk])
            q_lvl.append(q * jnp.exp2(jnp.where(upper[lvl], b2 - split2, NEG_BIG)))
            k_lvl.append(jnp.exp2(jnp.where(upper[lvl], NEG_BIG, split2 - kb)))

        qe_s[...] = qe.astype(BF16)
        btot_s[...] = btot2
        for p in range(n_pairs):
            sl = slice(p * LANES, (p + 1) * LANES)
            diag_rows = []
            for jb in range(n_blk):
                rs = slice(jb * sb, (jb + 1) * sb)
                bl_b, kbl_b, q_b = bl2[rs, sl], kbl[rs, sl], q[rs, sl]
                terms = []
                for s in range(sb):
                    arg = jnp.where(sub_row >= flip(s, sb), bl_b - kbl_b[s:s + 1], NEG_BIG)
                    terms.append(q_b * jnp.exp2(arg))
                diag_rows.append(jnp.concatenate(terms, axis=1))
            diag_s[p] = _dot(jnp.concatenate(diag_rows, axis=0).astype(BF16), sel)
            for lvl in range(n_lvl):
                k_p = k_lvl[lvl][:, sl]
                lvl_s[p * n_lvl + lvl] = _dot_nt(q_lvl[lvl][:, sl].astype(BF16), split_heads(k_p))
            upd_s[p] = _dot_tn(v[:, sl].astype(BF16), ke[:, sl].astype(BF16))

    def stage2_issue(j):
        r0 = row_start(j)
        v = v_ref[0, pl.ds(r0, ck), :]
        out = []
        for p in range(n_pairs):
            sl = slice(p * LANES, (p + 1) * LANES)
            parts = []
            for jb in range(n_blk):
                blk = diag_s[p, jb * sb:(jb + 1) * sb, :]
                parts.append(pltpu.roll(blk, jb * sb, 1) if jb else blk)
            scores = jnp.concatenate(parts, axis=0)
            for lvl in range(n_lvl):
                s_lvl = lvl_s[p * n_lvl + lvl]
                scores = scores + (s_lvl if lvl == n_lvl - 1 else jnp.where(group_mask[lvl], s_lvl, 0.0))
            intra = _dot(scores.astype(BF16), split_heads(v[:, sl]))
            st = st_ref[p]
            inter = _dot_nt(qe_s[:, sl], st.astype(BF16))
            new_st = jnp.where(same_head, st * jnp.exp2(btot_s[:, sl]) + upd_s[p], 0.0)
            out.append((inter + intra, new_st))
        return r0, out

    def stage2_finish(r0, out):
        for p in range(n_pairs):
            o_ref[0, pl.ds(r0, ck), p * LANES:(p + 1) * LANES] = out[p][0]
            st_ref[p] = out[p][1]

    return stage1a, stage1b, stage2_issue, stage2_finish


N_HGRN_SCRATCH = 6


def _hgrn_kernel(qf_ref, ff_ref, vf_ref, qb_ref, fb_ref, vb_ref, lb_ref, of_ref, ob_ref, zero_ref, *scratch,
                 rows):
    fwd_scratch, bwd_scratch = scratch[:N_HGRN_SCRATCH], scratch[N_HGRN_SCRATCH:]

    @pl.when(pl.program_id(1) == 0)
    def _():
        fwd_scratch[0][...] = jnp.zeros_like(fwd_scratch[0])
        bwd_scratch[0][...] = jnp.zeros_like(bwd_scratch[0])

    f1a, f1b, f2, f3 = _hgrn_direction(False, qf_ref, ff_ref, vf_ref, lb_ref, of_ref, *fwd_scratch, rows=rows)
    b1a, b1b, b2, b3 = _hgrn_direction(True, qb_ref, fb_ref, vb_ref, lb_ref, ob_ref, *bwd_scratch, rows=rows)
    n_sub = rows // HGRN_CHUNK

    def stage1_both(j):
        fa = f1a(j)
        ba = b1a(j)
        f1b(*fa)
        b1b(*ba)

    zero_part = zero_ref.shape[0] // n_sub

    def store_zeros(j):
        zero_ref[pl.ds(pl.multiple_of(j * zero_part, SUBLANES), zero_part), :] = jnp.zeros(
            (zero_part, zero_ref.shape[1]), zero_ref.dtype)

    stage1_both(0)

    def pipelined(j, carry):
        fo = f2(j)
        bo = b2(j)
        store_zeros(j)
        stage1_both(j + 1)
        f3(*fo)
        b3(*bo)
        return carry

    lax.fori_loop(0, n_sub - 1, pipelined, 0)
    fo = f2(n_sub - 1)
    bo = b2(n_sub - 1)
    store_zeros(n_sub - 1)
    f3(*fo)
    b3(*bo)


def _hgrn(proj, lb, zero_rows, *, d_lru, d_hgrn, rows=512):
    bsz, seq, _ = proj.shape
    n_chunks = seq // rows
    zero_blk = zero_rows // (bsz * n_chunks)
    assert zero_blk * bsz * n_chunks == zero_rows and zero_blk % (SUBLANES * (rows // HGRN_CHUNK)) == 0
    assert 2 * (d_hgrn // HGRN_HEADS) == LANES and HGRN_SUB == SUBLANES and HGRN_CHUNK == HGRN_SUB * SUBLANES
    col0 = (2 * d_lru) // d_hgrn
    n_pairs = d_hgrn // LANES
    n_lvl = (HGRN_CHUNK // HGRN_SUB).bit_length() - 1
    fwd = lambda col: pl.BlockSpec((1, rows, d_hgrn), lambda b, c: (b, c, col))
    bwd = lambda col: pl.BlockSpec((1, rows, d_hgrn), lambda b, c: (b, n_chunks - 1 - c, col))
    direction_scratch = [
        pltpu.VMEM((n_pairs, LANES, LANES), F32),
        pltpu.VMEM((n_pairs, HGRN_CHUNK, LANES), F32),
        pltpu.VMEM((n_pairs * n_lvl, HGRN_CHUNK, LANES), F32),
        pltpu.VMEM((n_pairs, LANES, LANES), F32),
        pltpu.VMEM((HGRN_CHUNK, d_hgrn), BF16),
        pltpu.VMEM((1, d_hgrn), F32),
    ]
    assert len(direction_scratch) == N_HGRN_SCRATCH
    kern = functools.partial(_hgrn_kernel, rows=rows)
    out = jax.ShapeDtypeStruct((bsz, seq, d_hgrn), F32)
    return pl.pallas_call(
        kern,
        out_shape=(out, out, jax.ShapeDtypeStruct((zero_rows, LANES), F32)),
        grid=(bsz, n_chunks),
        in_specs=[fwd(col0), fwd(col0 + 1), fwd(col0 + 3), bwd(col0), bwd(col0 + 2), bwd(col0 + 3),
                  pl.BlockSpec((1, d_hgrn), lambda b, c: (0, 0))],
        out_specs=(pl.BlockSpec((1, rows, d_hgrn), lambda b, c: (b, c, 0)),
                   pl.BlockSpec((1, rows, d_hgrn), lambda b, c: (b, n_chunks - 1 - c, 0)),
                   pl.BlockSpec((zero_blk, LANES), lambda b, c: (b * n_chunks + c, 0))),
        scratch_shapes=direction_scratch + direction_scratch,
        compiler_params=_params(("arbitrary", "arbitrary")),
        name="hgrn2",
    )(proj, proj, proj, proj, proj, proj, lb.reshape(1, d_hgrn))


def _interleave(*streams):
    done = object()
    live = list(streams)
    while live:
        live = [s for s in live if next(s, done) is not done]


def _gelu_tanh(y):
    return 0.5 * y * (1.0 + jnp.tanh(0.7978845608028654 * (y + 0.044715 * (y * y * y))))


def _post_kernel(lru_ref, y_ref, of_ref, ob_ref, g_ref, x_ref, nlw_ref, nhw_ref, wo_ref,
                 gm_ref, nfw_ref, scf_ref, shf_ref, wr_ref, br_ref,
                 xo_ref, h_ref, slab_ref, cnt_ref, route_ref, carry_ref, logits_s, *, tm, d_lru):
    step = pl.program_id(0)

    @pl.when(step == 0)
    def _():
        carry_ref[...] = jnp.zeros_like(carry_ref)
        logits_s[...] = jnp.zeros_like(logits_s)

    def mixer():
        lru = lru_ref[0] * _gelu_tanh(y_ref[0])
        ms = jnp.mean(lru * lru, axis=-1, keepdims=True)
        yield
        lru = lru * lax.rsqrt(ms + NORM_EPS) * nlw_ref[...]

        hg = of_ref[0] + ob_ref[0]
        width = hg.shape[1]
        hd = width // HGRN_HEADS
        hd_shift = hd.bit_length() - 1
        er = lax.broadcasted_iota(jnp.int32, (width, width), 0) >> hd_shift
        ec = lax.broadcasted_iota(jnp.int32, (width, width), 1) >> hd_shift
        head_sum = jnp.where(er == ec, 1.0, 0.0).astype(BF16)
        sq = hg * hg
        sq_hi = sq.astype(BF16)
        sq_lo = (sq - sq_hi.astype(F32)).astype(BF16)
        ms_h = (_dot(sq_hi, head_sum) + _dot(sq_lo, head_sum)) * (1.0 / hd)
        yield
        g = g_ref[0]
        hg = (hg * lax.rsqrt(ms_h + NORM_EPS) * nhw_ref[...]) * (g * _sigmoid(g))
        yield
        mixed = _dot(lru.astype(BF16), wo_ref[0:d_lru, :])
        yield
        mixed = mixed + _dot(hg.astype(BF16), wo_ref[d_lru:, :])
        yield
        x_new = x_ref[0] + gm_ref[0] * mixed
        xo_ref[0] = x_new
        ms_f = jnp.mean(x_new * x_new, axis=-1, keepdims=True)
        yield
        h = (x_new * lax.rsqrt(ms_f + NORM_EPS) * nfw_ref[...]) * (1.0 + scf_ref[0]) + shf_ref[0]
        _tiles_store(h_ref, h, tm, lead=(0,))
        yield
        logits_s[...] = _dot(h.astype(BF16), wr_ref[...]) + br_ref[...]

    def routing():
        yield from _routing_steps(logits_s[...], jnp.where(step > 0, 1.0, 0.0), slab_ref, cnt_ref, route_ref,
                                  carry_ref, tm)

    _interleave(routing(), mixer())


def _routing_steps(logits, live, slab_ref, cnt_ref, route_ref, carry_ref, tm):
    lane = lax.broadcasted_iota(jnp.int32, (tm, ROUTE_LANES), 1)
    lane_f = lane.astype(F32)
    far = float(ROUTE_LANES)
    is_g = lane < N_GROUPS
    gl = jnp.where(is_g, logits, NEG_BIG)
    gmax = jnp.max(gl, axis=-1, keepdims=True)
    yield
    g_idx = jnp.min(jnp.where(gl == gmax, lane_f, far), axis=-1, keepdims=True)
    p_group = 1.0 / jnp.sum(jnp.where(is_g, jnp.exp(gl - gmax), 0.0), axis=-1, keepdims=True)
    yield
    e_lane = lane - N_GROUPS
    in_group = (e_lane >= 0) & (e_lane < N_EXPERTS) & ((e_lane >> (EXPERTS_PER_GROUP.bit_length() - 1)).astype(F32) == g_idx)
    ev = jnp.where(in_group, logits, NEG_BIG)
    top1 = jnp.max(ev, axis=-1, keepdims=True)
    yield
    i1 = jnp.min(jnp.where(in_group & (ev == top1), lane_f, far), axis=-1, keepdims=True)
    yield
    rest = in_group & (lane_f != i1)
    ev2 = jnp.where(rest, logits, NEG_BIG)
    top2 = jnp.max(ev2, axis=-1, keepdims=True)
    yield
    i2 = jnp.min(jnp.where(rest & (ev2 == top2), lane_f, far), axis=-1, keepdims=True)
    yield
    e1 = i1 - float(N_GROUPS)
    e2 = i2 - float(N_GROUPS)
    ex = jnp.exp(top2 - top1)
    w1 = p_group / (1.0 + ex)
    w2 = p_group * ex / (1.0 + ex)

    sel1 = lane_f == e1
    sel2 = lane_f == e2
    onehot = jnp.where(sel1 | sel2, live, 0.0)
    tr = lax.broadcasted_iota(jnp.int32, (tm, tm), 0)
    tc = lax.broadcasted_iota(jnp.int32, (tm, tm), 1)
    before = jnp.where(tc < tr, 1.0, 0.0).astype(BF16)
    cnt = _dot(before, onehot.astype(BF16)) + carry_ref[0:1]
    yield
    rank1 = jnp.sum(jnp.where(sel1, cnt, 0.0), axis=-1, keepdims=True)
    rank2 = jnp.sum(jnp.where(sel2, cnt, 0.0), axis=-1, keepdims=True)
    total = carry_ref[0:1] + jnp.sum(onehot, axis=0, keepdims=True)
    carry_ref[...] = jnp.broadcast_to(total, carry_ref.shape)
    cnt_ref[...] = jnp.broadcast_to(total, cnt_ref.shape)
    yield

    slab = jnp.where(lane == 0, e1, 0.0)
    slab = jnp.where(lane == 1, e2, slab)
    slab = jnp.where(lane == 2, w1, slab)
    slab = jnp.where(lane == 3, w2, slab)
    slab = jnp.where(lane == 4, rank1, slab)
    slab = jnp.where(lane == 5, rank2, slab)
    slab_ref[0] = slab
    route_ref[...] = slab.T[0:SUBLANES]


def _post_mixer(lru_sum, proj, hg_f, hg_b, x, nlw, nhw, wo_bf16, g_mix, nfw, sc_ffn, sh_ffn, wr_bf16, br,
                *, tm=512):
    bsz, seq, d = x.shape
    d_lru = lru_sum.shape[-1]
    d_hgrn = hg_f.shape[-1]
    y_col = 1
    g_col = (2 * d_lru) // d_hgrn + 4
    tiles = seq // tm
    n_tiles = bsz * tiles
    cur = lambda s: jnp.minimum(s, n_tiles - 1)
    prev = lambda s: jnp.maximum(s - 1, 0)
    row = lambda w, col=0: pl.BlockSpec((1, tm, w), lambda s: (cur(s) // tiles, cur(s) % tiles, col))
    vec = lambda w: pl.BlockSpec((1, w), lambda s: (0, 0))
    per_b = lambda: pl.BlockSpec((1, 1, d), lambda s: (cur(s) // tiles, 0, 0))
    kern = functools.partial(_post_kernel, tm=tm, d_lru=d_lru)
    return pl.pallas_call(
        kern,
        out_shape=(
            jax.ShapeDtypeStruct((bsz, seq, d), F32),
            jax.ShapeDtypeStruct((bsz, seq * SUBLANES, LANES), F32),
            jax.ShapeDtypeStruct((bsz, seq, ROUTE_LANES), F32),
            jax.ShapeDtypeStruct((SUBLANES, ROUTE_LANES), F32),
            jax.ShapeDtypeStruct((SUBLANES, bsz * seq), F32),
        ),
        grid=(n_tiles + 1,),
        in_specs=[
            row(d_lru), row(d_lru, y_col),
            row(d_hgrn), row(d_hgrn), row(d_hgrn, g_col),
            row(d), vec(d_lru), vec(d_hgrn),
            pl.BlockSpec((d, d), lambda s: (0, 0)),
            per_b(), vec(d), per_b(), per_b(),
            pl.BlockSpec((d, ROUTE_LANES), lambda s: (0, 0)),
            vec(ROUTE_LANES),
        ],
        out_specs=(
            row(d),
            pl.BlockSpec((1, tm * SUBLANES, LANES), lambda s: (cur(s) // tiles, cur(s) % tiles, 0)),
            pl.BlockSpec((1, tm, ROUTE_LANES), lambda s: (prev(s) // tiles, prev(s) % tiles, 0)),
            pl.BlockSpec((SUBLANES, ROUTE_LANES), lambda s: (0, 0)),
            pl.BlockSpec((SUBLANES, tm), lambda s: (0, prev(s))),
        ),
        scratch_shapes=[pltpu.VMEM((SUBLANES, ROUTE_LANES), F32), pltpu.VMEM((tm, ROUTE_LANES), F32)],
        compiler_params=_params(("arbitrary",)),
        name="post_mixer_router",
    )(lru_sum, proj, hg_f, hg_b, proj, x, nlw.reshape(1, d_lru), nhw.reshape(1, d_hgrn), wo_bf16,
      g_mix, nfw.reshape(1, d), sc_ffn, sh_ffn, wr_bf16, br)


def _tiles_load(ref, n, lead=()):
    return jnp.concatenate(
        [ref[(*lead, pl.ds(j, n, stride=SUBLANES), slice(None))] for j in range(SUBLANES)], axis=1)


def _tiles_store(ref, val, n, lead=()):
    for j in range(SUBLANES):
        ref[(*lead, pl.ds(j, n, stride=SUBLANES), slice(None))] = val[:, j * LANES:(j + 1) * LANES]


def _token_tile(ref, t):
    return ref.at[pl.ds(pl.multiple_of(t * SUBLANES, SUBLANES), SUBLANES)]


def _dest_kernel(start_ref, route_ref, o_ref):
    route = route_ref[...].astype(jnp.int32)
    start = jnp.zeros_like(route)
    for e in range(N_EXPERTS):
        start = jnp.where(route == e, start_ref[e], start)
    o_ref[...] = start + pltpu.roll(route, SUBLANES // 2, 0)


def _dest_rows(expert_start, route):
    return pl.pallas_call(
        _dest_kernel,
        out_shape=jax.ShapeDtypeStruct(route.shape, jnp.int32),
        grid_spec=pltpu.PrefetchScalarGridSpec(
            num_scalar_prefetch=1,
            grid=(1,),
            in_specs=[pl.BlockSpec(route.shape, lambda i, s: (0, 0))],
            out_specs=pl.BlockSpec(route.shape, lambda i, s: (0, 0)),
        ),
        compiler_params=pltpu.CompilerParams(dimension_semantics=("arbitrary",)),
        name="moe_dest_rows",
    )(expert_start, route)


def _dispatch_kernel(d1_ref, d2_ref, h_ref, z_ref, o_ref, sem, *, tb):
    del z_ref
    base = pl.program_id(0) * tb

    def issue(r, carry):
        t = base + r
        pltpu.make_async_copy(_token_tile(h_ref, r), _token_tile(o_ref, d1_ref[t]), sem).start(priority=0)
        pltpu.make_async_copy(_token_tile(h_ref, r), _token_tile(o_ref, d2_ref[t]), sem).start(priority=1)
        return carry

    lax.fori_loop(0, tb, issue, 0, unroll=DMA_ISSUE_UNROLL)
    for _ in range(2):
        pltpu.make_async_copy(h_ref, o_ref.at[pl.ds(0, tb * SUBLANES)], sem).wait()


def _dispatch(dest1, dest2, h_tiles, zero_tiles, *, tb=512):
    m = h_tiles.shape[0] // SUBLANES
    n_rows = zero_tiles.shape[0] // SUBLANES
    kern = functools.partial(_dispatch_kernel, tb=tb)
    return pl.pallas_call(
        kern,
        out_shape=jax.ShapeDtypeStruct((n_rows * SUBLANES, LANES), h_tiles.dtype),
        grid_spec=pltpu.PrefetchScalarGridSpec(
            num_scalar_prefetch=2,
            grid=(m // tb,),
            in_specs=[pl.BlockSpec((tb * SUBLANES, LANES), lambda i, d1, d2: (i, 0)),
                      pl.BlockSpec(memory_space=pl.ANY)],
            out_specs=pl.BlockSpec(memory_space=pl.ANY),
            scratch_shapes=[pltpu.SemaphoreType.DMA(())],
        ),
        input_output_aliases={3: 0},
        compiler_params=pltpu.CompilerParams(dimension_semantics=("arbitrary",), has_side_effects=True),
        name="moe_dispatch",
    )(dest1, dest2, h_tiles, zero_tiles)


def _expert_kernel(plan_ref, x_ref, wg_hbm, wu_hbm, wd_hbm, o_ref, wg_f, wu_f, wd_f, wg_s, wu_s, wd_s, sem,
                   *, layer):
    i = pl.program_id(0)
    n_blocks = pl.num_programs(0)
    n_used = plan_ref[n_blocks]
    expert = plan_ref[i]
    next_expert = plan_ref[n_blocks + 1 + i]
    slot = plan_ref[2 * n_blocks + 1 + i]
    first_block = ((i == 0) | (plan_ref[jnp.maximum(i - 1, 0)] != expert)) & (i < n_used)

    def copies(e, s):
        return (pltpu.make_async_copy(wg_hbm.at[layer, e], wg_f.at[s], sem.at[s]),
                pltpu.make_async_copy(wu_hbm.at[layer, e], wu_f.at[s], sem.at[s]),
                pltpu.make_async_copy(wd_hbm.at[layer, e], wd_f.at[s], sem.at[s]))

    @pl.when(i == 0)
    def _():
        for c in copies(expert, slot):
            c.start()

    @pl.when(first_block)
    def _():
        for c in copies(expert, slot):
            c.wait()
        wg_s[...] = wg_f[slot].astype(BF16)
        wu_s[...] = wu_f[slot].astype(BF16)
        wd_s[...] = wd_f[slot].astype(BF16)

        @pl.when(next_expert >= 0)
        def _():
            for c in copies(next_expert, 1 - slot):
                c.start()

    blk = x_ref.shape[0] // SUBLANES

    @pl.when(i < n_used)
    def _():
        x = _tiles_load(x_ref, blk).astype(BF16)
        gate = _dot(x, wg_s[...])
        up = _dot(x, wu_s[...])
        act = (gate * _sigmoid(gate)) * up
        _tiles_store(o_ref, _dot(act.astype(BF16), wd_s[...]), blk)

    @pl.when(i >= n_used)
    def _():
        o_ref[...] = jnp.zeros_like(o_ref)


def _expert_plan(blk_expert, blocks_used):
    n_blocks = blk_expert.shape[0]
    idx = jnp.arange(n_blocks, dtype=jnp.int32)
    change = jnp.concatenate([jnp.ones((1,), bool), blk_expert[1:] != blk_expert[:-1]])
    slot = (jnp.cumsum(change.astype(jnp.int32)) - 1) & 1
    change_at = jnp.where(change, idx, n_blocks)
    from_here = lax.cummin(change_at[::-1])[::-1]
    next_change = jnp.concatenate([from_here[1:], jnp.full((1,), n_blocks, jnp.int32)])
    next_expert = jnp.where(next_change < blocks_used, blk_expert[jnp.minimum(next_change, n_blocks - 1)], -1)
    return jnp.concatenate([blk_expert, blocks_used.reshape(1), next_expert, slot]).astype(jnp.int32)


def _experts(plan, x_tiles, wg, wu, wd, layer):
    n_rows = x_tiles.shape[0] // SUBLANES
    d, de = wg.shape[-2:]
    blk = EXPERT_BLOCK
    n_blocks = n_rows // blk
    kern = functools.partial(_expert_kernel, layer=layer)
    return pl.pallas_call(
        kern,
        out_shape=jax.ShapeDtypeStruct((n_rows * SUBLANES, LANES), F32),
        grid_spec=pltpu.PrefetchScalarGridSpec(
            num_scalar_prefetch=1,
            grid=(n_blocks,),
            in_specs=[
                pl.BlockSpec((blk * SUBLANES, LANES), lambda i, plan: (jnp.minimum(i, plan[n_blocks] - 1), 0)),
                pl.BlockSpec(memory_space=pl.ANY),
                pl.BlockSpec(memory_space=pl.ANY),
                pl.BlockSpec(memory_space=pl.ANY),
            ],
            out_specs=pl.BlockSpec((blk * SUBLANES, LANES), lambda i, plan: (i, 0)),
            scratch_shapes=[
                pltpu.VMEM((2, d, de), F32), pltpu.VMEM((2, d, de), F32), pltpu.VMEM((2, de, d), F32),
                pltpu.VMEM((d, de), BF16), pltpu.VMEM((d, de), BF16), pltpu.VMEM((de, d), BF16),
                pltpu.SemaphoreType.DMA((2,)),
            ],
        ),
        compiler_params=_params(("arbitrary",)),
        name="moe_experts",
    )(plan, x_tiles, wg, wu, wd)


def _combine_kernel(d1_ref, d2_ref, y_ref, slab_ref, x_ref, g_ref, nw_ref, o_ref, ra0, rb0, ra1, rb1, sem,
                    *, tm, tiles, n_steps, final_norm):
    step = pl.program_id(0) * tiles + pl.program_id(1)
    bufs = ((ra0, rb0), (ra1, rb1))

    def gather(tile, slot):
        base = tile * tm
        r1_ref, r2_ref = bufs[slot]

        def issue(r, carry):
            t = base + r
            pltpu.make_async_copy(_token_tile(y_ref, d1_ref[t]), _token_tile(r1_ref, r),
                                  sem.at[slot]).start(priority=0)
            pltpu.make_async_copy(_token_tile(y_ref, d2_ref[t]), _token_tile(r2_ref, r),
                                  sem.at[slot]).start(priority=1)
            return carry

        lax.fori_loop(0, tm, issue, 0, unroll=DMA_ISSUE_UNROLL)

    @pl.when(step == 0)
    def _():
        gather(0, 0)

    for slot in range(2):
        @pl.when((step & 1) == slot)
        def _():
            @pl.when(step + 1 < n_steps)
            def _():
                gather(step + 1, 1 - slot)

            r1_ref, r2_ref = bufs[slot]
            pltpu.make_async_copy(y_ref.at[pl.ds(0, tm * SUBLANES)], r1_ref, sem.at[slot]).wait()
            pltpu.make_async_copy(y_ref.at[pl.ds(0, tm * SUBLANES)], r2_ref, sem.at[slot]).wait()
            slab = slab_ref[0]
            y = slab[:, 2:3] * _tiles_load(r1_ref, tm) + slab[:, 3:4] * _tiles_load(r2_ref, tm)
            out = x_ref[0] + g_ref[0] * y
            if final_norm:
                ms = jnp.mean(out * out, axis=-1, keepdims=True)
                out = out * lax.rsqrt(ms + NORM_EPS) * nw_ref[...]
            o_ref[0] = out


def _combine(dest1, dest2, y_buf, slab, x, g_ffn, norm_w, *, final_norm, tm=512):
    bsz, seq, d = x.shape
    tiles = seq // tm
    kern = functools.partial(_combine_kernel, tm=tm, tiles=tiles, n_steps=bsz * tiles, final_norm=final_norm)
    row_buf = pltpu.VMEM((tm * SUBLANES, LANES), F32)
    return pl.pallas_call(
        kern,
        out_shape=jax.ShapeDtypeStruct((bsz, seq, d), F32),
        grid_spec=pltpu.PrefetchScalarGridSpec(
            num_scalar_prefetch=2,
            grid=(bsz, tiles),
            in_specs=[
                pl.BlockSpec(memory_space=pl.ANY),
                pl.BlockSpec((1, tm, ROUTE_LANES), lambda b, i, d1, d2: (b, i, 0)),
                pl.BlockSpec((1, tm, d), lambda b, i, d1, d2: (b, i, 0)),
                pl.BlockSpec((1, 1, d), lambda b, i, d1, d2: (b, 0, 0)),
                pl.BlockSpec((1, d), lambda b, i, d1, d2: (0, 0)),
            ],
            out_specs=pl.BlockSpec((1, tm, d), lambda b, i, d1, d2: (b, i, 0)),
            scratch_shapes=[row_buf, row_buf, row_buf, row_buf, pltpu.SemaphoreType.DMA((2,))],
        ),
        compiler_params=_params(("arbitrary", "arbitrary")),
        name="moe_combine",
    )(dest1, dest2, y_buf, slab, x, g_ffn, norm_w.reshape(1, d))


def _block_diag(w):
    heads, hd, _ = w.shape
    n = heads * hd
    tiled = jnp.tile(w.reshape(n, hd), (1, heads))
    blk_r = lax.broadcasted_iota(jnp.int32, (n, n), 0) // hd
    blk_c = lax.broadcasted_iota(jnp.int32, (n, n), 1) // hd
    return jnp.where(blk_r == blk_c, tiled, 0.0)


def kernel(x, c, ada_w, ada_b, norm_mix_w, w_in, conv_w, conv_b, lru_wa, lru_ba, lru_wx, lru_bx, lru_lambda, norm_lru_w, hgrn_lb, norm_hgrn_w, w_out, norm_ffn_w, router_group_w, router_group_b, router_expert_w, router_expert_b, expert_w_gate, expert_w_up, expert_w_down, final_norm_w):
    bsz, seq, d = x.shape
    assert d == SUBLANES * LANES, "the MoE row movement keeps one (8, 128) tile per token"
    depth = ada_w.shape[0]
    d_lru = conv_w.shape[-1]
    d_hgrn = hgrn_lb.shape[-1]
    m = bsz * seq
    n_rows = m * 2 + N_EXPERTS * EXPERT_BLOCK
    n_blocks = n_rows // EXPERT_BLOCK

    mod = _modulation(c, ada_w, ada_b)
    lb_cum = jnp.cumsum(jax.nn.softmax(hgrn_lb.astype(F32), axis=0), axis=0)
    lb_all = lb_cum - lb_cum[0:1]

    for l in range(depth):
        sh_mix, sc_mix, g_mix, sh_ffn, sc_ffn, g_ffn = [
            mod[l, :, i * d:(i + 1) * d].reshape(bsz, 1, d) for i in range(N_MODULATIONS)]
        proj = _in_proj(x, norm_mix_w[l], sc_mix, sh_mix, w_in, l)
        wa_bd = jnp.stack([_block_diag(lru_wa[l, 0]), _block_diag(lru_wa[l, 1])]).astype(BF16)
        wx_bd = jnp.stack([_block_diag(lru_wx[l, 0]), _block_diag(lru_wx[l, 1])]).astype(BF16)
        lru_args = (proj, conv_w[l], conv_b[l], wa_bd, lru_ba[l], wx_bd, lru_bx[l], lru_lambda[l])
        lru_sum = _lru_scan(*lru_args, reverse=True, add_to=_lru_scan(*lru_args, reverse=False))
        hg_f, hg_b, zero_tiles = _hgrn(proj, lb_all[l], n_rows * SUBLANES, d_lru=d_lru, d_hgrn=d_hgrn)

        lane_pad = ROUTE_LANES - N_GROUPS - N_EXPERTS
        wr = jnp.pad(jnp.concatenate([router_group_w[l], router_expert_w[l]], axis=1), ((0, 0), (0, lane_pad)))
        br = jnp.pad(jnp.concatenate([router_group_b[l], router_expert_b[l]]), (0, lane_pad)).reshape(1, ROUTE_LANES)
        x_mid, h_ffn, slab, counts, route = _post_mixer(
            lru_sum, proj, hg_f, hg_b, x, norm_lru_w[l], norm_hgrn_w[l], w_out[l].astype(BF16), g_mix,
            norm_ffn_w[l], sc_ffn, sh_ffn, wr.astype(BF16), br)

        cnt = counts[0, :N_EXPERTS].astype(jnp.int32)
        padded = ((cnt + EXPERT_BLOCK - 1) // EXPERT_BLOCK) * EXPERT_BLOCK
        pend = jnp.cumsum(padded)
        pstart = pend - padded
        blk_start = jnp.arange(n_blocks, dtype=jnp.int32) * EXPERT_BLOCK
        blk_expert = jnp.minimum(jnp.sum(pend[None, :] <= blk_start[:, None], axis=1), N_EXPERTS - 1)
        plan = _expert_plan(blk_expert.astype(jnp.int32), (pend[N_EXPERTS - 1] // EXPERT_BLOCK).astype(jnp.int32))
        dest = _dest_rows(pstart.astype(jnp.int32), route)
        dest1, dest2 = dest[0], dest[1]

        x_buf = _dispatch(dest1, dest2, h_ffn.reshape(m * SUBLANES, LANES), zero_tiles)
        y_buf = _experts(plan, x_buf, expert_w_gate, expert_w_up, expert_w_down, l)
        x = _combine(dest1, dest2, y_buf, slab, x_mid, g_ffn, final_norm_w, final_norm=(l == depth - 1))

    return x
```

```python
import functools

import jax
import jax.numpy as jnp
from jax import lax
from jax.experimental import pallas as pl
from jax.experimental.pallas import tpu as pltpu

F32 = jnp.float32
BF16 = jnp.bfloat16

HGRN_HEADS = 8
N_MODULATIONS = 6
CONV_WIDTH = 4
LRU_C = 8.0
N_GROUPS = 4
EXPERTS_PER_GROUP = 8
N_EXPERTS = N_GROUPS * EXPERTS_PER_GROUP
NORM_EPS = 1e-6

LANES = 128
SUBLANES = 8
VMEM_LIMIT = 56 * 1024 * 1024

HGRN_CHUNK = 64
HGRN_SUB = 8
LOG2E = 1.4426950408889634
ROUTE_LANES = LANES
EXPERT_BLOCK = 256
DMA_ISSUE_UNROLL = 8
NEG_BIG = -3.0e38


def _params(sem):
    return pltpu.CompilerParams(dimension_semantics=sem, vmem_limit_bytes=VMEM_LIMIT)


def _dot(a, b):
    return jnp.dot(a, b, preferred_element_type=F32)


def _dot_nt(a, b):
    return lax.dot_general(a, b, (((1,), (1,)), ((), ())), preferred_element_type=F32)


def _dot_tn(a, b):
    return lax.dot_general(a, b, (((0,), (0,)), ((), ())), preferred_element_type=F32)


def _dot01_exact(m01, x):
    hi = x.astype(BF16)
    r1 = x - hi.astype(F32)
    mid = r1.astype(BF16)
    lo = (r1 - mid.astype(F32)).astype(BF16)
    return _dot(m01, hi) + _dot(m01, mid) + _dot(m01, lo)


def _sigmoid(x):
    return 1.0 / (1.0 + jnp.exp(-x))


def _sigmoid_tanh(x):
    return 0.5 * jnp.tanh(0.5 * x) + 0.5


def _mod_kernel(c_ref, w_ref, b_ref, o_ref):
    c = c_ref[...]
    cond = c * _sigmoid(c)
    o_ref[0] = _dot(cond.astype(BF16), w_ref[0].astype(BF16)) + b_ref[0]


def _modulation(c, ada_w, ada_b):
    depth, d, n = ada_w.shape
    bsz = c.shape[0]
    rows = -(-bsz // SUBLANES) * SUBLANES
    c_pad = jnp.pad(c, ((0, rows - bsz), (0, 0)))
    tn = n // N_MODULATIONS
    out = pl.pallas_call(
        _mod_kernel,
        out_shape=jax.ShapeDtypeStruct((depth, rows, n), F32),
        grid=(depth, n // tn),
        in_specs=[
            pl.BlockSpec((rows, d), lambda l, j: (0, 0)),
            pl.BlockSpec((1, d, tn), lambda l, j: (l, 0, j)),
            pl.BlockSpec((1, 1, tn), lambda l, j: (l, 0, j)),
        ],
        out_specs=pl.BlockSpec((1, rows, tn), lambda l, j: (l, 0, j)),
        compiler_params=_params(("arbitrary", "arbitrary")),
        name="adaln_mod",
    )(c_pad, ada_w, ada_b.reshape(depth, 1, n))
    return out[:, :bsz]


def _rms_mod(x, nw, sc, sh):
    ms = jnp.mean(x * x, axis=-1, keepdims=True)
    return (x * lax.rsqrt(ms + NORM_EPS) * nw) * (1.0 + sc) + sh


def _inproj_kernel(x_ref, nw_ref, sc_ref, sh_ref, w_ref, o_ref, w_s):
    @pl.when((pl.program_id(0) == 0) & (pl.program_id(1) == 0))
    def _():
        w_s[...] = w_ref[0].astype(BF16)

    h = _rms_mod(x_ref[0], nw_ref[...], sc_ref[0], sh_ref[0])
    o_ref[0] = _dot(h.astype(BF16), w_s[...])


def _in_proj(x, nw, sc, sh, w_in, layer, tm=512):
    bsz, seq, d = x.shape
    n = w_in.shape[-1]
    return pl.pallas_call(
        _inproj_kernel,
        out_shape=jax.ShapeDtypeStruct((bsz, seq, n), F32),
        grid=(bsz, seq // tm),
        in_specs=[
            pl.BlockSpec((1, tm, d), lambda b, i: (b, i, 0)),
            pl.BlockSpec((1, d), lambda b, i: (0, 0)),
            pl.BlockSpec((1, 1, d), lambda b, i: (b, 0, 0)),
            pl.BlockSpec((1, 1, d), lambda b, i: (b, 0, 0)),
            pl.BlockSpec((1, d, n), lambda b, i: (layer, 0, 0), pipeline_mode=pl.Buffered(1)),
        ],
        out_specs=pl.BlockSpec((1, tm, n), lambda b, i: (b, i, 0)),
        scratch_shapes=[pltpu.VMEM((d, n), BF16)],
        compiler_params=_params(("arbitrary", "arbitrary")),
        name="in_proj",
    )(x, nw.reshape(1, d), sc, sh, w_in)


def _lru_kernel(x_ref, xp_ref, xn_ref, cw_ref, cb_ref, wa_ref, ba_ref, wx_ref, bx_ref, lam_ref, *rest,
                reverse, n_chunks, rows, accumulate):
    add_ref = rest[0] if accumulate else None
    o_ref, carry_ref, sa_ref, sb_ref, cin_ref = rest[1:] if accumulate else rest
    c = pl.program_id(1)
    chunk = (n_chunks - 1 - c) if reverse else c

    @pl.when(c == 0)
    def _():
        carry_ref[...] = jnp.zeros_like(carry_ref)

    x = x_ref[0]
    width = x.shape[1]
    has_prev = jnp.where(chunk > 0, 1.0, 0.0)
    has_next = jnp.where(chunk < n_chunks - 1, 1.0, 0.0)
    xp = xp_ref[0] * has_prev
    xn = xn_ref[0] * has_next
    xe = jnp.concatenate([xp, x, xn], axis=0)
    cw = cw_ref[...]
    xc = cb_ref[...]
    for k in range(CONV_WIDTH):
        lo = SUBLANES + k - CONV_WIDTH // 2
        xc = xc + cw[k:k + 1] * xe[lo:lo + rows]

    xcb = xc.astype(BF16)
    r = _sigmoid_tanh(_dot(xcb, wa_ref[0]) + ba_ref[0])
    gate_i = _sigmoid_tanh(_dot(xcb, wx_ref[0]) + bx_ref[0])
    lam = lam_ref[0]
    softplus_neg_lam = jnp.maximum(-lam, 0.0) + jnp.log1p(jnp.exp(-jnp.abs(lam)))
    log_a = (-LRU_C) * r * softplus_neg_lam
    a = jnp.exp(log_a)
    t = jnp.tanh(-log_a)
    u = jnp.sqrt(2.0 * t / (1.0 + t)) * (gate_i * xc)

    groups = rows // SUBLANES
    acc_a = a.reshape(groups, SUBLANES, width)
    acc_b = u.reshape(groups, SUBLANES, width)
    sub = lax.broadcasted_iota(jnp.int32, (groups, SUBLANES, width), 1)
    s = 1
    while s < SUBLANES:
        if reverse:
            valid = sub < SUBLANES - s
            sh_a, sh_b = pltpu.roll(acc_a, SUBLANES - s, 1), pltpu.roll(acc_b, SUBLANES - s, 1)
        else:
            valid = sub >= s
            sh_a, sh_b = pltpu.roll(acc_a, s, 1), pltpu.roll(acc_b, s, 1)
        acc_b = jnp.where(valid, acc_a * sh_b + acc_b, acc_b)
        acc_a = jnp.where(valid, acc_a * sh_a, acc_a)
        s *= 2
    acc_a = acc_a.reshape(rows, width)
    acc_b = acc_b.reshape(rows, width)
    edge = 0 if reverse else SUBLANES - 1
    n_tiles = width // LANES
    for j in range(n_tiles):
        sa_ref[j] = acc_a[:, j * LANES:(j + 1) * LANES]
        sb_ref[j] = acc_b[:, j * LANES:(j + 1) * LANES]
    ea = jnp.concatenate([sa_ref[j, pl.ds(edge, groups, stride=SUBLANES), :] for j in range(n_tiles)], axis=1)
    eb = jnp.concatenate([sb_ref[j, pl.ds(edge, groups, stride=SUBLANES), :] for j in range(n_tiles)], axis=1)
    grow = lax.broadcasted_iota(jnp.int32, (groups, width), 0)
    s = 1
    while s < groups:
        if reverse:
            valid = grow < groups - s
            sh_a, sh_b = pltpu.roll(ea, groups - s, 0), pltpu.roll(eb, groups - s, 0)
        else:
            valid = grow >= s
            sh_a, sh_b = pltpu.roll(ea, s, 0), pltpu.roll(eb, s, 0)
        eb = jnp.where(valid, ea * sh_b + eb, eb)
        ea = jnp.where(valid, ea * sh_a, ea)
        s *= 2
    carry = carry_ref[...]
    group_out = eb + ea * carry
    if reverse:
        carry_in = jnp.where(grow == groups - 1, carry, pltpu.roll(group_out, groups - 1, 0))
        carry_ref[...] = group_out[0:1]
    else:
        carry_in = jnp.where(grow == 0, carry, pltpu.roll(group_out, 1, 0))
        carry_ref[...] = group_out[groups - 1:groups]
    cin_ref[...] = carry_in
    for g in range(groups):
        rs = slice(g * SUBLANES, (g + 1) * SUBLANES)
        h_rows = acc_b[rs] + acc_a[rs] * cin_ref[g:g + 1, :]
        o_ref[0, rs, :] = (add_ref[0, rs, :] + h_rows) if accumulate else h_rows


def _lru_scan(proj, conv_w, conv_b, wa_bd, ba, wx_bd, bx, lam, *, reverse, add_to=None, rows=512):
    bsz, seq, _ = proj.shape
    d_lru = conv_w.shape[1]
    n_chunks = seq // rows
    halo = rows // SUBLANES
    last_halo = seq // SUBLANES - 1
    dirn = 1 if reverse else 0

    def chunk_of(c):
        return (n_chunks - 1 - c) if reverse else c

    vec = lambda: pl.BlockSpec((1, 1, d_lru), lambda b, c: (dirn, 0, 0))
    mat = lambda: pl.BlockSpec((1, d_lru, d_lru), lambda b, c: (dirn, 0, 0))
    accumulate = add_to is not None
    kern = functools.partial(_lru_kernel, reverse=reverse, n_chunks=n_chunks, rows=rows, accumulate=accumulate)
    tile = pl.BlockSpec((1, rows, d_lru), lambda b, c: (b, chunk_of(c), 0))
    return pl.pallas_call(
        kern,
        out_shape=jax.ShapeDtypeStruct((bsz, seq, d_lru), F32),
        grid=(bsz, n_chunks),
        in_specs=[
            pl.BlockSpec((1, rows, d_lru), lambda b, c: (b, chunk_of(c), 0)),
            pl.BlockSpec((1, SUBLANES, d_lru),
                         lambda b, c: (b, jnp.maximum(chunk_of(c) * halo - 1, 0), 0)),
            pl.BlockSpec((1, SUBLANES, d_lru),
                         lambda b, c: (b, jnp.minimum((chunk_of(c) + 1) * halo, last_halo), 0)),
            pl.BlockSpec((CONV_WIDTH, d_lru), lambda b, c: (0, 0)),
            pl.BlockSpec((1, d_lru), lambda b, c: (0, 0)),
            mat(), vec(), mat(), vec(), vec(),
        ] + ([tile] if accumulate else []),
        out_specs=tile,
        scratch_shapes=[pltpu.VMEM((1, d_lru), F32), pltpu.VMEM((d_lru // LANES, rows, LANES), F32),
                        pltpu.VMEM((d_lru // LANES, rows, LANES), F32),
                        pltpu.VMEM((rows // SUBLANES, d_lru), F32)],
        compiler_params=_params(("arbitrary", "arbitrary")),
        name="lru_bwd" if reverse else "lru_fwd",
    )(proj, proj, proj, conv_w, conv_b.reshape(1, d_lru), wa_bd, ba.reshape(2, 1, d_lru),
      wx_bd, bx.reshape(2, 1, d_lru), lam.reshape(2, 1, d_lru), *([add_to] if accumulate else []))


def _hgrn_direction(rev, q_ref, f_ref, v_ref, lb_ref, o_ref, st_ref, diag_s, lvl_s, upd_s, qe_s, btot_s, *, rows):
    ck, sb = HGRN_CHUNK, HGRN_SUB
    n_blk = ck // sb
    sb_shift = sb.bit_length() - 1
    n_sub = rows // ck
    width = q_ref.shape[-1]
    n_pairs = width // LANES
    half = LANES // 2

    def flip(idx, n):
        return (n - 1 - idx) if rev else idx

    n_lvl = n_blk.bit_length() - 1
    tf = flip(lax.broadcasted_iota(jnp.int32, (ck, ck), 0), ck)
    uf = flip(lax.broadcasted_iota(jnp.int32, (ck, ck), 1), ck)
    tb, ub = tf >> sb_shift, uf >> sb_shift
    pb = flip(lax.broadcasted_iota(jnp.int32, (n_blk, ck), 0), n_blk)
    pub = flip(lax.broadcasted_iota(jnp.int32, (n_blk, ck), 1), ck) >> sb_shift
    mats = [jnp.where((tb == ub) & (uf <= tf), 1.0, 0.0),
            jnp.where(pub < pb, 1.0, 0.0)]
    for lvl in range(n_lvl):
        mid = ((pb >> (lvl + 1)) << (lvl + 1)) + (1 << lvl)
        mats.append(jnp.where(pub < mid, 1.0, 0.0))
    mats.append(jnp.ones((SUBLANES, ck), F32))
    m_cum = jnp.concatenate(mats, axis=0).astype(BF16)

    def per_block(rows8):
        return jnp.concatenate(
            [jnp.broadcast_to(rows8[jb:jb + 1], (sb, rows8.shape[1])) for jb in range(n_blk)], axis=0)
    row_blk = flip(lax.broadcasted_iota(jnp.int32, (ck, width), 0), ck) >> sb_shift
    upper = [((row_blk >> lvl) & 1) == 1 for lvl in range(n_lvl)]
    pr = flip(lax.broadcasted_iota(jnp.int32, (ck, LANES), 0), ck) >> sb_shift
    pc = flip(lax.broadcasted_iota(jnp.int32, (ck, LANES), 1) & (ck - 1), ck) >> sb_shift
    group_mask = [(pr >> (lvl + 1)) == (pc >> (lvl + 1)) for lvl in range(n_lvl)]
    lane = lax.broadcasted_iota(jnp.int32, (1, LANES), 1)
    head0 = lane < half

    def split_heads(x):
        xb = x.astype(BF16)
        zero = jnp.zeros_like(xb)
        return jnp.concatenate([jnp.where(head0, xb, zero), jnp.where(head0, zero, xb)], axis=0)

    sr = lax.broadcasted_iota(jnp.int32, (LANES, LANES), 0)
    sc = lax.broadcasted_iota(jnp.int32, (LANES, LANES), 1)
    same_head = (sr < half) == (sc < half)
    er = lax.broadcasted_iota(jnp.int32, (sb * LANES, LANES), 0)
    ec = lax.broadcasted_iota(jnp.int32, (sb * LANES, LANES), 1)
    sel = jnp.where(ec == (((er & (LANES - 1)) >> (half.bit_length() - 1)) * half + (er >> (LANES.bit_length() - 1))),
                    1.0, 0.0).astype(BF16)
    sub_row = flip(lax.broadcasted_iota(jnp.int32, (sb, LANES), 0), sb)
    lbv = lb_ref[...]

    def row_start(j):
        return pl.multiple_of(flip(j, n_sub) * ck, ck)

    def stage1a(j):
        r0 = row_start(j)
        q = q_ref[0, pl.ds(r0, ck), :]
        z = f_ref[0, pl.ds(r0, ck), :]
        v = v_ref[0, pl.ds(r0, ck), :]
        f = lbv + (1.0 - lbv) * _sigmoid(z)
        lf2 = jnp.log(f) * LOG2E
        k = 1.0 - f
        return q, v, k, _dot01_exact(m_cum, lf2)

    def stage1b(q, v, k, cums):
        bl2 = cums[0:ck]
        b2 = bl2 + per_block(cums[ck:ck + n_blk])
        tot_row = ck + (1 + n_lvl) * n_blk
        btot2 = cums[tot_row:tot_row + 1]
        log2_k = jnp.log(k) * LOG2E
        kb = b2 - log2_k
        kbl = bl2 - log2_k
        qe = q * jnp.exp2(b2)
        ke = jnp.exp2(btot2 - kb)
        q_lvl, k_lvl = [], []
        for lvl in range(n_lvl):
            split2 = per_block(cums[ck + (1 + lvl) * n_blk:ck + (2 + lvl) * n_blk])
            q_lvl.append(q * jnp.exp2(jnp.where(upper[lvl], b2 - split2, NEG_BIG)))
            k_lvl.append(jnp.exp2(jnp.where(upper[lvl], NEG_BIG, split2 - kb)))

        qe_s[...] = qe.astype(BF16)
        btot_s[...] = btot2
        for p in range(n_pairs):
            sl = slice(p * LANES, (p + 1) * LANES)
            diag_rows = []
            for jb in range(n_blk):
                rs = slice(jb * sb, (jb + 1) * sb)
                bl_b, kbl_b, q_b = bl2[rs, sl], kbl[rs, sl], q[rs, sl]
                terms = []
                for s in range(sb):
                    arg = jnp.where(sub_row >= flip(s, sb), bl_b - kbl_b[s:s + 1], NEG_BIG)
                    terms.append(q_b * jnp.exp2(arg))
                diag_rows.append(jnp.concatenate(terms, axis=1))
            diag_s[p] = _dot(jnp.concatenate(diag_rows, axis=0).astype(BF16), sel)
            for lvl in range(n_lvl):
                k_p = k_lvl[lvl][:, sl]
                lvl_s[p * n_lvl + lvl] = _dot_nt(q_lvl[lvl][:, sl].astype(BF16), split_heads(k_p))
            upd_s[p] = _dot_tn(v[:, sl].astype(BF16), ke[:, sl].astype(BF16))

    def stage2_issue(j):
        r0 = row_start(j)
        v = v_ref[0, pl.ds(r0, ck), :]
        out = []
        for p in range(n_pairs):
            sl = slice(p * LANES, (p + 1) * LANES)
            parts = []
            for jb in range(n_blk):
                blk = diag_s[p, jb * sb:(jb + 1) * sb, :]
                parts.append(pltpu.roll(blk, jb * sb, 1) if jb else blk)
            scores = jnp.concatenate(parts, axis=0)
            for lvl in range(n_lvl):
                s_lvl = lvl_s[p * n_lvl + lvl]
                scores = scores + (s_lvl if lvl == n_lvl - 1 else jnp.where(group_mask[lvl], s_lvl, 0.0))
            intra = _dot(scores.astype(BF16), split_heads(v[:, sl]))
            st = st_ref[p]
            inter = _dot_nt(qe_s[:, sl], st.astype(BF16))
            new_st = jnp.where(same_head, st * jnp.exp2(btot_s[:, sl]) + upd_s[p], 0.0)
            out.append((inter + intra, new_st))
        return r0, out

    def stage2_finish(r0, out):
        for p in range(n_pairs):
            o_ref[0, pl.ds(r0, ck), p * LANES:(p + 1) * LANES] = out[p][0]
            st_ref[p] = out[p][1]

    return stage1a, stage1b, stage2_issue, stage2_finish


N_HGRN_SCRATCH = 6


def _hgrn_kernel(qf_ref, ff_ref, vf_ref, qb_ref, fb_ref, vb_ref, lb_ref, of_ref, ob_ref, zero_ref, *scratch,
                 rows):
    fwd_scratch, bwd_scratch = scratch[:N_HGRN_SCRATCH], scratch[N_HGRN_SCRATCH:]

    @pl.when(pl.program_id(1) == 0)
    def _():
        fwd_scratch[0][...] = jnp.zeros_like(fwd_scratch[0])
        bwd_scratch[0][...] = jnp.zeros_like(bwd_scratch[0])

    f1a, f1b, f2, f3 = _hgrn_direction(False, qf_ref, ff_ref, vf_ref, lb_ref, of_ref, *fwd_scratch, rows=rows)
    b1a, b1b, b2, b3 = _hgrn_direction(True, qb_ref, fb_ref, vb_ref, lb_ref, ob_ref, *bwd_scratch, rows=rows)
    n_sub = rows // HGRN_CHUNK

    def stage1_both(j):
        fa = f1a(j)
        ba = b1a(j)
        f1b(*fa)
        b1b(*ba)

    zero_part = zero_ref.shape[0] // n_sub

    def store_zeros(j):
        zero_ref[pl.ds(pl.multiple_of(j * zero_part, SUBLANES), zero_part), :] = jnp.zeros(
            (zero_part, zero_ref.shape[1]), zero_ref.dtype)

    stage1_both(0)

    def pipelined(j, carry):
        fo = f2(j)
        bo = b2(j)
        store_zeros(j)
        stage1_both(j + 1)
        f3(*fo)
        b3(*bo)
        return carry

    lax.fori_loop(0, n_sub - 1, pipelined, 0)
    fo = f2(n_sub - 1)
    bo = b2(n_sub - 1)
    store_zeros(n_sub - 1)
    f3(*fo)
    b3(*bo)


def _hgrn(proj, lb, zero_rows, *, d_lru, d_hgrn, rows=512):
    bsz, seq, _ = proj.shape
    n_chunks = seq // rows
    zero_blk = zero_rows // (bsz * n_chunks)
    assert zero_blk * bsz * n_chunks == zero_rows and zero_blk % (SUBLANES * (rows // HGRN_CHUNK)) == 0
    assert 2 * (d_hgrn // HGRN_HEADS) == LANES and HGRN_SUB == SUBLANES and HGRN_CHUNK == HGRN_SUB * SUBLANES
    col0 = (2 * d_lru) // d_hgrn
    n_pairs = d_hgrn // LANES
    n_lvl = (HGRN_CHUNK // HGRN_SUB).bit_length() - 1
    fwd = lambda col: pl.BlockSpec((1, rows, d_hgrn), lambda b, c: (b, c, col))
    bwd = lambda col: pl.BlockSpec((1, rows, d_hgrn), lambda b, c: (b, n_chunks - 1 - c, col))
    direction_scratch = [
        pltpu.VMEM((n_pairs, LANES, LANES), F32),
        pltpu.VMEM((n_pairs, HGRN_CHUNK, LANES), F32),
        pltpu.VMEM((n_pairs * n_lvl, HGRN_CHUNK, LANES), F32),
        pltpu.VMEM((n_pairs, LANES, LANES), F32),
        pltpu.VMEM((HGRN_CHUNK, d_hgrn), BF16),
        pltpu.VMEM((1, d_hgrn), F32),
    ]
    assert len(direction_scratch) == N_HGRN_SCRATCH
    kern = functools.partial(_hgrn_kernel, rows=rows)
    out = jax.ShapeDtypeStruct((bsz, seq, d_hgrn), F32)
    return pl.pallas_call(
        kern,
        out_shape=(out, out, jax.ShapeDtypeStruct((zero_rows, LANES), F32)),
        grid=(bsz, n_chunks),
        in_specs=[fwd(col0), fwd(col0 + 1), fwd(col0 + 3), bwd(col0), bwd(col0 + 2), bwd(col0 + 3),
                  pl.BlockSpec((1, d_hgrn), lambda b, c: (0, 0))],
        out_specs=(pl.BlockSpec((1, rows, d_hgrn), lambda b, c: (b, c, 0)),
                   pl.BlockSpec((1, rows, d_hgrn), lambda b, c: (b, n_chunks - 1 - c, 0)),
                   pl.BlockSpec((zero_blk, LANES), lambda b, c: (b * n_chunks + c, 0))),
        scratch_shapes=direction_scratch + direction_scratch,
        compiler_params=_params(("arbitrary", "arbitrary")),
        name="hgrn2",
    )(proj, proj, proj, proj, proj, proj, lb.reshape(1, d_hgrn))


def _interleave(*streams):
    done = object()
    live = list(streams)
    while live:
        live = [s for s in live if next(s, done) is not done]


def _gelu_tanh(y):
    return 0.5 * y * (1.0 + jnp.tanh(0.7978845608028654 * (y + 0.044715 * (y * y * y))))


def _post_kernel(lru_ref, y_ref, of_ref, ob_ref, g_ref, x_ref, nlw_ref, nhw_ref, wo_ref,
                 gm_ref, nfw_ref, scf_ref, shf_ref, wr_ref, br_ref,
                 xo_ref, h_ref, slab_ref, cnt_ref, route_ref, carry_ref, logits_s, *, tm, d_lru):
    step = pl.program_id(0)

    @pl.when(step == 0)
    def _():
        carry_ref[...] = jnp.zeros_like(carry_ref)
        logits_s[...] = jnp.zeros_like(logits_s)

    def mixer():
        lru = lru_ref[0] * _gelu_tanh(y_ref[0])
        ms = jnp.mean(lru * lru, axis=-1, keepdims=True)
        yield
        lru = lru * lax.rsqrt(ms + NORM_EPS) * nlw_ref[...]

        hg = of_ref[0] + ob_ref[0]
        width = hg.shape[1]
        hd = width // HGRN_HEADS
        hd_shift = hd.bit_length() - 1
        er = lax.broadcasted_iota(jnp.int32, (width, width), 0) >> hd_shift
        ec = lax.broadcasted_iota(jnp.int32, (width, width), 1) >> hd_shift
        head_sum = jnp.where(er == ec, 1.0, 0.0).astype(BF16)
        sq = hg * hg
        sq_hi = sq.astype(BF16)
        sq_lo = (sq - sq_hi.astype(F32)).astype(BF16)
        ms_h = (_dot(sq_hi, head_sum) + _dot(sq_lo, head_sum)) * (1.0 / hd)
        yield
        g = g_ref[0]
        hg = (hg * lax.rsqrt(ms_h + NORM_EPS) * nhw_ref[...]) * (g * _sigmoid(g))
        yield
        mixed = _dot(lru.astype(BF16), wo_ref[0:d_lru, :])
        yield
        mixed = mixed + _dot(hg.astype(BF16), wo_ref[d_lru:, :])
        yield
        x_new = x_ref[0] + gm_ref[0] * mixed
        xo_ref[0] = x_new
        ms_f = jnp.mean(x_new * x_new, axis=-1, keepdims=True)
        yield
        h = (x_new * lax.rsqrt(ms_f + NORM_EPS) * nfw_ref[...]) * (1.0 + scf_ref[0]) + shf_ref[0]
        _tiles_store(h_ref, h, tm, lead=(0,))
        yield
        logits_s[...] = _dot(h.astype(BF16), wr_ref[...]) + br_ref[...]

    def routing():
        yield from _routing_steps(logits_s[...], jnp.where(step > 0, 1.0, 0.0), slab_ref, cnt_ref, route_ref,
                                  carry_ref, tm)

    _interleave(routing(), mixer())


def _routing_steps(logits, live, slab_ref, cnt_ref, route_ref, carry_ref, tm):
    lane = lax.broadcasted_iota(jnp.int32, (tm, ROUTE_LANES), 1)
    lane_f = lane.astype(F32)
    far = float(ROUTE_LANES)
    is_g = lane < N_GROUPS
    gl = jnp.where(is_g, logits, NEG_BIG)
    gmax = jnp.max(gl, axis=-1, keepdims=True)
    yield
    g_idx = jnp.min(jnp.where(gl == gmax, lane_f, far), axis=-1, keepdims=True)
    p_group = 1.0 / jnp.sum(jnp.where(is_g, jnp.exp(gl - gmax), 0.0), axis=-1, keepdims=True)
    yield
    e_lane = lane - N_GROUPS
    in_group = (e_lane >= 0) & (e_lane < N_EXPERTS) & ((e_lane >> (EXPERTS_PER_GROUP.bit_length() - 1)).astype(F32) == g_idx)
    ev = jnp.where(in_group, logits, NEG_BIG)
    top1 = jnp.max(ev, axis=-1, keepdims=True)
    yield
    i1 = jnp.min(jnp.where(in_group & (ev == top1), lane_f, far), axis=-1, keepdims=True)
    yield
    rest = in_group & (lane_f != i1)
    ev2 = jnp.where(rest, logits, NEG_BIG)
    top2 = jnp.max(ev2, axis=-1, keepdims=True)
    yield
    i2 = jnp.min(jnp.where(rest & (ev2 == top2), lane_f, far), axis=-1, keepdims=True)
    yield
    e1 = i1 - float(N_GROUPS)
    e2 = i2 - float(N_GROUPS)
    ex = jnp.exp(top2 - top1)
    w1 = p_group / (1.0 + ex)
    w2 = p_group * ex / (1.0 + ex)

    sel1 = lane_f == e1
    sel2 = lane_f == e2
    onehot = jnp.where(sel1 | sel2, live, 0.0)
    tr = lax.broadcasted_iota(jnp.int32, (tm, tm), 0)
    tc = lax.broadcasted_iota(jnp.int32, (tm, tm), 1)
    before = jnp.where(tc < tr, 1.0, 0.0).astype(BF16)
    cnt = _dot(before, onehot.astype(BF16)) + carry_ref[0:1]
    yield
    rank1 = jnp.sum(jnp.where(sel1, cnt, 0.0), axis=-1, keepdims=True)
    rank2 = jnp.sum(jnp.where(sel2, cnt, 0.0), axis=-1, keepdims=True)
    total = carry_ref[0:1] + jnp.sum(onehot, axis=0, keepdims=True)
    carry_ref[...] = jnp.broadcast_to(total, carry_ref.shape)
    cnt_ref[...] = jnp.broadcast_to(total, cnt_ref.shape)
    yield

    slab = jnp.where(lane == 0, e1, 0.0)
    slab = jnp.where(lane == 1, e2, slab)
    slab = jnp.where(lane == 2, w1, slab)
    slab = jnp.where(lane == 3, w2, slab)
    slab = jnp.where(lane == 4, rank1, slab)
    slab = jnp.where(lane == 5, rank2, slab)
    slab_ref[0] = slab
    route_ref[...] = slab.T[0:SUBLANES]


def _post_mixer(lru_sum, proj, hg_f, hg_b, x, nlw, nhw, wo_bf16, g_mix, nfw, sc_ffn, sh_ffn, wr_bf16, br,
                *, tm=512):
    bsz, seq, d = x.shape
    d_lru = lru_sum.shape[-1]
    d_hgrn = hg_f.shape[-1]
    y_col = 1
    g_col = (2 * d_lru) // d_hgrn + 4
    tiles = seq // tm
    n_tiles = bsz * tiles
    cur = lambda s: jnp.minimum(s, n_tiles - 1)
    prev = lambda s: jnp.maximum(s - 1, 0)
    row = lambda w, col=0: pl.BlockSpec((1, tm, w), lambda s: (cur(s) // tiles, cur(s) % tiles, col))
    vec = lambda w: pl.BlockSpec((1, w), lambda s: (0, 0))
    per_b = lambda: pl.BlockSpec((1, 1, d), lambda s: (cur(s) // tiles, 0, 0))
    kern = functools.partial(_post_kernel, tm=tm, d_lru=d_lru)
    return pl.pallas_call(
        kern,
        out_shape=(
            jax.ShapeDtypeStruct((bsz, seq, d), F32),
            jax.ShapeDtypeStruct((bsz, seq * SUBLANES, LANES), F32),
            jax.ShapeDtypeStruct((bsz, seq, ROUTE_LANES), F32),
            jax.ShapeDtypeStruct((SUBLANES, ROUTE_LANES), F32),
            jax.ShapeDtypeStruct((SUBLANES, bsz * seq), F32),
        ),
        grid=(n_tiles + 1,),
        in_specs=[
            row(d_lru), row(d_lru, y_col),
            row(d_hgrn), row(d_hgrn), row(d_hgrn, g_col),
            row(d), vec(d_lru), vec(d_hgrn),
            pl.BlockSpec((d, d), lambda s: (0, 0)),
            per_b(), vec(d), per_b(), per_b(),
            pl.BlockSpec((d, ROUTE_LANES), lambda s: (0, 0)),
            vec(ROUTE_LANES),
        ],
        out_specs=(
            row(d),
            pl.BlockSpec((1, tm * SUBLANES, LANES), lambda s: (cur(s) // tiles, cur(s) % tiles, 0)),
            pl.BlockSpec((1, tm, ROUTE_LANES), lambda s: (prev(s) // tiles, prev(s) % tiles, 0)),
            pl.BlockSpec((SUBLANES, ROUTE_LANES), lambda s: (0, 0)),
            pl.BlockSpec((SUBLANES, tm), lambda s: (0, prev(s))),
        ),
        scratch_shapes=[pltpu.VMEM((SUBLANES, ROUTE_LANES), F32), pltpu.VMEM((tm, ROUTE_LANES), F32)],
        compiler_params=_params(("arbitrary",)),
        name="post_mixer_router",
    )(lru_sum, proj, hg_f, hg_b, proj, x, nlw.reshape(1, d_lru), nhw.reshape(1, d_hgrn), wo_bf16,
      g_mix, nfw.reshape(1, d), sc_ffn, sh_ffn, wr_bf16, br)


def _tiles_load(ref, n, lead=()):
    return jnp.concatenate(
        [ref[(*lead, pl.ds(j, n, stride=SUBLANES), slice(None))] for j in range(SUBLANES)], axis=1)


def _tiles_store(ref, val, n, lead=()):
    for j in range(SUBLANES):
        ref[(*lead, pl.ds(j, n, stride=SUBLANES), slice(None))] = val[:, j * LANES:(j + 1) * LANES]


def _token_tile(ref, t):
    return ref.at[pl.ds(pl.multiple_of(t * SUBLANES, SUBLANES), SUBLANES)]


def _dest_kernel(start_ref, route_ref, o_ref):
    route = route_ref[...].astype(jnp.int32)
    start = jnp.zeros_like(route)
    for e in range(N_EXPERTS):
        start = jnp.where(route == e, start_ref[e], start)
    o_ref[...] = start + pltpu.roll(route, SUBLANES // 2, 0)


def _dest_rows(expert_start, route):
    return pl.pallas_call(
        _dest_kernel,
        out_shape=jax.ShapeDtypeStruct(route.shape, jnp.int32),
        grid_spec=pltpu.PrefetchScalarGridSpec(
            num_scalar_prefetch=1,
            grid=(1,),
            in_specs=[pl.BlockSpec(route.shape, lambda i, s: (0, 0))],
            out_specs=pl.BlockSpec(route.shape, lambda i, s: (0, 0)),
        ),
        compiler_params=pltpu.CompilerParams(dimension_semantics=("arbitrary",)),
        name="moe_dest_rows",
    )(expert_start, route)


def _dispatch_kernel(d1_ref, d2_ref, h_ref, z_ref, o_ref, sem, *, tb):
    del z_ref
    base = pl.program_id(0) * tb

    def issue(r, carry):
        t = base + r
        pltpu.make_async_copy(_token_tile(h_ref, r), _token_tile(o_ref, d1_ref[t]), sem).start(priority=0)
        pltpu.make_async_copy(_token_tile(h_ref, r), _token_tile(o_ref, d2_ref[t]), sem).start(priority=1)
        return carry

    lax.fori_loop(0, tb, issue, 0, unroll=DMA_ISSUE_UNROLL)
    for _ in range(2):
        pltpu.make_async_copy(h_ref, o_ref.at[pl.ds(0, tb * SUBLANES)], sem).wait()


def _dispatch(dest1, dest2, h_tiles, zero_tiles, *, tb=512):
    m = h_tiles.shape[0] // SUBLANES
    n_rows = zero_tiles.shape[0] // SUBLANES
    kern = functools.partial(_dispatch_kernel, tb=tb)
    return pl.pallas_call(
        kern,
        out_shape=jax.ShapeDtypeStruct((n_rows * SUBLANES, LANES), h_tiles.dtype),
        grid_spec=pltpu.PrefetchScalarGridSpec(
            num_scalar_prefetch=2,
            grid=(m // tb,),
            in_specs=[pl.BlockSpec((tb * SUBLANES, LANES), lambda i, d1, d2: (i, 0)),
                      pl.BlockSpec(memory_space=pl.ANY)],
            out_specs=pl.BlockSpec(memory_space=pl.ANY),
            scratch_shapes=[pltpu.SemaphoreType.DMA(())],
        ),
        input_output_aliases={3: 0},
        compiler_params=pltpu.CompilerParams(dimension_semantics=("arbitrary",), has_side_effects=True),
        name="moe_dispatch",
    )(dest1, dest2, h_tiles, zero_tiles)


def _expert_kernel(plan_ref, x_ref, wg_hbm, wu_hbm, wd_hbm, o_ref, wg_f, wu_f, wd_f, wg_s, wu_s, wd_s, sem,
                   *, layer):
    i = pl.program_id(0)
    n_blocks = pl.num_programs(0)
    n_used = plan_ref[n_blocks]
    expert = plan_ref[i]
    next_expert = plan_ref[n_blocks + 1 + i]
    slot = plan_ref[2 * n_blocks + 1 + i]
    first_block = ((i == 0) | (plan_ref[jnp.maximum(i - 1, 0)] != expert)) & (i < n_used)

    def copies(e, s):
        return (pltpu.make_async_copy(wg_hbm.at[layer, e], wg_f.at[s], sem.at[s]),
                pltpu.make_async_copy(wu_hbm.at[layer, e], wu_f.at[s], sem.at[s]),
                pltpu.make_async_copy(wd_hbm.at[layer, e], wd_f.at[s], sem.at[s]))

    @pl.when(i == 0)
    def _():
        for c in copies(expert, slot):
            c.start()

    @pl.when(first_block)
    def _():
        for c in copies(expert, slot):
            c.wait()
        wg_s[...] = wg_f[slot].astype(BF16)
        wu_s[...] = wu_f[slot].astype(BF16)
        wd_s[...] = wd_f[slot].astype(BF16)

        @pl.when(next_expert >= 0)
        def _():
            for c in copies(next_expert, 1 - slot):
                c.start()

    blk = x_ref.shape[0] // SUBLANES

    @pl.when(i < n_used)
    def _():
        x = _tiles_load(x_ref, blk).astype(BF16)
        gate = _dot(x, wg_s[...])
        up = _dot(x, wu_s[...])
        act = (gate * _sigmoid(gate)) * up
        _tiles_store(o_ref, _dot(act.astype(BF16), wd_s[...]), blk)

    @pl.when(i >= n_used)
    def _():
        o_ref[...] = jnp.zeros_like(o_ref)


def _expert_plan(blk_expert, blocks_used):
    n_blocks = blk_expert.shape[0]
    idx = jnp.arange(n_blocks, dtype=jnp.int32)
    change = jnp.concatenate([jnp.ones((1,), bool), blk_expert[1:] != blk_expert[:-1]])
    slot = (jnp.cumsum(change.astype(jnp.int32)) - 1) & 1
    change_at = jnp.where(change, idx, n_blocks)
    from_here = lax.cummin(change_at[::-1])[::-1]
    next_change = jnp.concatenate([from_here[1:], jnp.full((1,), n_blocks, jnp.int32)])
    next_expert = jnp.where(next_change < blocks_used, blk_expert[jnp.minimum(next_change, n_blocks - 1)], -1)
    return jnp.concatenate([blk_expert, blocks_used.reshape(1), next_expert, slot]).astype(jnp.int32)


def _experts(plan, x_tiles, wg, wu, wd, layer):
    n_rows = x_tiles.shape[0] // SUBLANES
    d, de = wg.shape[-2:]
    blk = EXPERT_BLOCK
    n_blocks = n_rows // blk
    kern = functools.partial(_expert_kernel, layer=layer)
    return pl.pallas_call(
        kern,
        out_shape=jax.ShapeDtypeStruct((n_rows * SUBLANES, LANES), F32),
        grid_spec=pltpu.PrefetchScalarGridSpec(
            num_scalar_prefetch=1,
            grid=(n_blocks,),
            in_specs=[
                pl.BlockSpec((blk * SUBLANES, LANES), lambda i, plan: (jnp.minimum(i, plan[n_blocks] - 1), 0)),
                pl.BlockSpec(memory_space=pl.ANY),
                pl.BlockSpec(memory_space=pl.ANY),
                pl.BlockSpec(memory_space=pl.ANY),
            ],
            out_specs=pl.BlockSpec((blk * SUBLANES, LANES), lambda i, plan: (i, 0)),
            scratch_shapes=[
                pltpu.VMEM((2, d, de), F32), pltpu.VMEM((2, d, de), F32), pltpu.VMEM((2, de, d), F32),
                pltpu.VMEM((d, de), BF16), pltpu.VMEM((d, de), BF16), pltpu.VMEM((de, d), BF16),
                pltpu.SemaphoreType.DMA((2,)),
            ],
        ),
        compiler_params=_params(("arbitrary",)),
        name="moe_experts",
    )(plan, x_tiles, wg, wu, wd)


def _combine_kernel(d1_ref, d2_ref, y_ref, slab_ref, x_ref, g_ref, nw_ref, o_ref, ra0, rb0, ra1, rb1, sem,
                    *, tm, tiles, n_steps, final_norm):
    step = pl.program_id(0) * tiles + pl.program_id(1)
    bufs = ((ra0, rb0), (ra1, rb1))

    def gather(tile, slot):
        base = tile * tm
        r1_ref, r2_ref = bufs[slot]

        def issue(r, carry):
            t = base + r
            pltpu.make_async_copy(_token_tile(y_ref, d1_ref[t]), _token_tile(r1_ref, r),
                                  sem.at[slot]).start(priority=0)
            pltpu.make_async_copy(_token_tile(y_ref, d2_ref[t]), _token_tile(r2_ref, r),
                                  sem.at[slot]).start(priority=1)
            return carry

        lax.fori_loop(0, tm, issue, 0, unroll=DMA_ISSUE_UNROLL)

    @pl.when(step == 0)
    def _():
        gather(0, 0)

    for slot in range(2):
        @pl.when((step & 1) == slot)
        def _():
            @pl.when(step + 1 < n_steps)
            def _():
                gather(step + 1, 1 - slot)

            r1_ref, r2_ref = bufs[slot]
            pltpu.make_async_copy(y_ref.at[pl.ds(0, tm * SUBLANES)], r1_ref, sem.at[slot]).wait()
            pltpu.make_async_copy(y_ref.at[pl.ds(0, tm * SUBLANES)], r2_ref, sem.at[slot]).wait()
            slab = slab_ref[0]
            y = slab[:, 2:3] * _tiles_load(r1_ref, tm) + slab[:, 3:4] * _tiles_load(r2_ref, tm)
            out = x_ref[0] + g_ref[0] * y
            if final_norm:
                ms = jnp.mean(out * out, axis=-1, keepdims=True)
                out = out * lax.rsqrt(ms + NORM_EPS) * nw_ref[...]
            o_ref[0] = out


def _combine(dest1, dest2, y_buf, slab, x, g_ffn, norm_w, *, final_norm, tm=512):
    bsz, seq, d = x.shape
    tiles = seq // tm
    kern = functools.partial(_combine_kernel, tm=tm, tiles=tiles, n_steps=bsz * tiles, final_norm=final_norm)
    row_buf = pltpu.VMEM((tm * SUBLANES, LANES), F32)
    return pl.pallas_call(
        kern,
        out_shape=jax.ShapeDtypeStruct((bsz, seq, d), F32),
        grid_spec=pltpu.PrefetchScalarGridSpec(
            num_scalar_prefetch=2,
            grid=(bsz, tiles),
            in_specs=[
                pl.BlockSpec(memory_space=pl.ANY),
                pl.BlockSpec((1, tm, ROUTE_LANES), lambda b, i, d1, d2: (b, i, 0)),
                pl.BlockSpec((1, tm, d), lambda b, i, d1, d2: (b, i, 0)),
                pl.BlockSpec((1, 1, d), lambda b, i, d1, d2: (b, 0, 0)),
                pl.BlockSpec((1, d), lambda b, i, d1, d2: (0, 0)),
            ],
            out_specs=pl.BlockSpec((1, tm, d), lambda b, i, d1, d2: (b, i, 0)),
            scratch_shapes=[row_buf, row_buf, row_buf, row_buf, pltpu.SemaphoreType.DMA((2,))],
        ),
        compiler_params=_params(("arbitrary", "arbitrary")),
        name="moe_combine",
    )(dest1, dest2, y_buf, slab, x, g_ffn, norm_w.reshape(1, d))


def _block_diag(w):
    heads, hd, _ = w.shape
    n = heads * hd
    tiled = jnp.tile(w.reshape(n, hd), (1, heads))
    blk_r = lax.broadcasted_iota(jnp.int32, (n, n), 0) // hd
    blk_c = lax.broadcasted_iota(jnp.int32, (n, n), 1) // hd
    return jnp.where(blk_r == blk_c, tiled, 0.0)


def kernel(x, c, ada_w, ada_b, norm_mix_w, w_in, conv_w, conv_b, lru_wa, lru_ba, lru_wx, lru_bx, lru_lambda, norm_lru_w, hgrn_lb, norm_hgrn_w, w_out, norm_ffn_w, router_group_w, router_group_b, router_expert_w, router_expert_b, expert_w_gate, expert_w_up, expert_w_down, final_norm_w):
    bsz, seq, d = x.shape
    assert d == SUBLANES * LANES, "the MoE row movement keeps one (8, 128) tile per token"
    depth = ada_w.shape[0]
    d_lru = conv_w.shape[-1]
    d_hgrn = hgrn_lb.shape[-1]
    m = bsz * seq
    n_rows = m * 2 + N_EXPERTS * EXPERT_BLOCK
    n_blocks = n_rows // EXPERT_BLOCK

    mod = _modulation(c, ada_w, ada_b)
    lb_cum = jnp.cumsum(jax.nn.softmax(hgrn_lb.astype(F32), axis=0), axis=0)
    lb_all = lb_cum - lb_cum[0:1]

    for l in range(depth):
        sh_mix, sc_mix, g_mix, sh_ffn, sc_ffn, g_ffn = [
            mod[l, :, i * d:(i + 1) * d].reshape(bsz, 1, d) for i in range(N_MODULATIONS)]
        proj = _in_proj(x, norm_mix_w[l], sc_mix, sh_mix, w_in, l)
        wa_bd = jnp.stack([_block_diag(lru_wa[l, 0]), _block_diag(lru_wa[l, 1])]).astype(BF16)
        wx_bd = jnp.stack([_block_diag(lru_wx[l, 0]), _block_diag(lru_wx[l, 1])]).astype(BF16)
        lru_args = (proj, conv_w[l], conv_b[l], wa_bd, lru_ba[l], wx_bd, lru_bx[l], lru_lambda[l])
        lru_sum = _lru_scan(*lru_args, reverse=True, add_to=_lru_scan(*lru_args, reverse=False))
        hg_f, hg_b, zero_tiles = _hgrn(proj, lb_all[l], n_rows * SUBLANES, d_lru=d_lru, d_hgrn=d_hgrn)

        lane_pad = ROUTE_LANES - N_GROUPS - N_EXPERTS
        wr = jnp.pad(jnp.concatenate([router_group_w[l], router_expert_w[l]], axis=1), ((0, 0), (0, lane_pad)))
        br = jnp.pad(jnp.concatenate([router_group_b[l], router_expert_b[l]]), (0, lane_pad)).reshape(1, ROUTE_LANES)
        x_mid, h_ffn, slab, counts, route = _post_mixer(
            lru_sum, proj, hg_f, hg_b, x, norm_lru_w[l], norm_hgrn_w[l], w_out[l].astype(BF16), g_mix,
            norm_ffn_w[l], sc_ffn, sh_ffn, wr.astype(BF16), br)

        cnt = counts[0, :N_EXPERTS].astype(jnp.int32)
        padded = ((cnt + EXPERT_BLOCK - 1) // EXPERT_BLOCK) * EXPERT_BLOCK
        pend = jnp.cumsum(padded)
        pstart = pend - padded
        blk_start = jnp.arange(n_blocks, dtype=jnp.int32) * EXPERT_BLOCK
        blk_expert = jnp.minimum(jnp.sum(pend[None, :] <= blk_start[:, None], axis=1), N_EXPERTS - 1)
        plan = _expert_plan(blk_expert.astype(jnp.int32), (pend[N_EXPERTS - 1] // EXPERT_BLOCK).astype(jnp.int32))
        dest = _dest_rows(pstart.astype(jnp.int32), route)
        dest1, dest2 = dest[0], dest[1]

        x_buf = _dispatch(dest1, dest2, h_ffn.reshape(m * SUBLANES, LANES), zero_tiles)
        y_buf = _experts(plan, x_buf, expert_w_gate, expert_w_up, expert_w_down, l)
        x = _combine(dest1, dest2, y_buf, slab, x_mid, g_ffn, final_norm_w, final_norm=(l == depth - 1))

    return x
```

```python
import functools

import jax
import jax.numpy as jnp
from jax import lax
from jax.experimental import pallas as pl
from jax.experimental.pallas import tpu as pltpu

F32 = jnp.float32
BF16 = jnp.bfloat16

HGRN_HEADS = 8
N_MODULATIONS = 6
CONV_WIDTH = 4
LRU_C = 8.0
N_GROUPS = 4
EXPERTS_PER_GROUP = 8
N_EXPERTS = N_GROUPS * EXPERTS_PER_GROUP
NORM_EPS = 1e-6

LANES = 128
SUBLANES = 8
VMEM_LIMIT = 56 * 1024 * 1024

HGRN_CHUNK = 64
HGRN_SUB = 8
LOG2E = 1.4426950408889634
ROUTE_LANES = LANES
EXPERT_BLOCK = 512
DMA_ISSUE_UNROLL = 8
NEG_BIG = -3.0e38


def _params(sem):
    return pltpu.CompilerParams(dimension_semantics=sem, vmem_limit_bytes=VMEM_LIMIT)


def _dot(a, b):
    return jnp.dot(a, b, preferred_element_type=F32)


def _dot_nt(a, b):
    return lax.dot_general(a, b, (((1,), (1,)), ((), ())), preferred_element_type=F32)


def _dot_tn(a, b):
    return lax.dot_general(a, b, (((0,), (0,)), ((), ())), preferred_element_type=F32)


def _dot01_exact(m01, x):
    hi = x.astype(BF16)
    r1 = x - hi.astype(F32)
    mid = r1.astype(BF16)
    lo = (r1 - mid.astype(F32)).astype(BF16)
    return _dot(m01, hi) + _dot(m01, mid) + _dot(m01, lo)


def _sigmoid(x):
    return 1.0 / (1.0 + jnp.exp(-x))


def _sigmoid_tanh(x):
    return 0.5 * jnp.tanh(0.5 * x) + 0.5


def _mod_kernel(c_ref, w_ref, b_ref, o_ref):
    c = c_ref[...]
    cond = c * _sigmoid(c)
    o_ref[0] = _dot(cond.astype(BF16), w_ref[0].astype(BF16)) + b_ref[0]


def _modulation(c, ada_w, ada_b):
    depth, d, n = ada_w.shape
    bsz = c.shape[0]
    rows = -(-bsz // SUBLANES) * SUBLANES
    c_pad = jnp.pad(c, ((0, rows - bsz), (0, 0)))
    tn = n // N_MODULATIONS
    out = pl.pallas_call(
        _mod_kernel,
        out_shape=jax.ShapeDtypeStruct((depth, rows, n), F32),
        grid=(depth, n // tn),
        in_specs=[
            pl.BlockSpec((rows, d), lambda l, j: (0, 0)),
            pl.BlockSpec((1, d, tn), lambda l, j: (l, 0, j)),
            pl.BlockSpec((1, 1, tn), lambda l, j: (l, 0, j)),
        ],
        out_specs=pl.BlockSpec((1, rows, tn), lambda l, j: (l, 0, j)),
        compiler_params=_params(("arbitrary", "arbitrary")),
        name="adaln_mod",
    )(c_pad, ada_w, ada_b.reshape(depth, 1, n))
    return out[:, :bsz]


def _rms_mod(x, nw, sc, sh):
    ms = jnp.mean(x * x, axis=-1, keepdims=True)
    return (x * lax.rsqrt(ms + NORM_EPS) * nw) * (1.0 + sc) + sh


def _inproj_lru_kernel(x_ref, nw_ref, sc_ref, sh_ref, w_ref, cw_ref, cb_ref, wa_ref, ba_ref, wx_ref, bx_ref,
                       lam_ref, proj_ref, lru_ref, w_s, xl_s, xnew_s, carry_ref, sa_ref, sb_ref, cin_ref,
                       *, tm, tiles, d_lru):
    s = pl.program_id(0)
    n_cols = proj_ref.shape[-1]

    @pl.when(s == 0)
    def _():
        w_s[...] = w_ref[0].astype(BF16)
        xl_s[...] = jnp.zeros_like(xl_s)
        carry_ref[...] = jnp.zeros_like(carry_ref)

    prev = jnp.maximum(s - 1, 0)
    chunk = prev % tiles
    has_prev = jnp.where(chunk > 0, 1.0, 0.0)
    has_next = jnp.where(chunk < tiles - 1, 1.0, 0.0)
    hb = _rms_mod(x_ref[0], nw_ref[...], sc_ref[0], sh_ref[0]).astype(BF16)

    def project():
        step = 2 * LANES
        for c0 in range(0, n_cols, step):
            block = _dot(hb, w_s[:, c0:c0 + step])
            proj_ref[0, :, c0:c0 + step] = block
            if c0 < d_lru:
                xnew_s[:, c0:c0 + step] = block
            if c0 + step >= d_lru:
                yield

    def scan():
        yield

        def store(rs, h_rows):
            lru_ref[0, rs, :] = h_rows

        yield from _lru_steps(xl_s[SUBLANES:, :], xl_s[0:SUBLANES, :] * has_prev,
                              xnew_s[0:SUBLANES, :] * has_next, carry_ref[...] * has_prev,
                              cw_ref[...], cb_ref[...], wa_ref[0], ba_ref[0], wx_ref[0], bx_ref[0], lam_ref[0],
                              store, carry_ref, sa_ref, sb_ref, cin_ref, reverse=False)

    _interleave(project(), scan())
    xl_s[0:SUBLANES, :] = xl_s[tm:tm + SUBLANES, :]
    xl_s[SUBLANES:, :] = xnew_s[...]


def _in_proj_lru(x, nw, sc, sh, w_in, layer, conv_w, conv_b, wa_bd, ba, wx_bd, bx, lam, tm=512):
    bsz, seq, d = x.shape
    n = w_in.shape[-1]
    d_lru = conv_w.shape[1]
    tiles = seq // tm
    n_tiles = bsz * tiles
    cur = lambda s: jnp.minimum(s, n_tiles - 1)
    prev = lambda s: jnp.maximum(s - 1, 0)
    vec = lambda: pl.BlockSpec((1, 1, d_lru), lambda s: (0, 0, 0))
    mat = lambda: pl.BlockSpec((1, d_lru, d_lru), lambda s: (0, 0, 0))
    kern = functools.partial(_inproj_lru_kernel, tm=tm, tiles=tiles, d_lru=d_lru)
    return pl.pallas_call(
        kern,
        out_shape=(jax.ShapeDtypeStruct((bsz, seq, n), F32), jax.ShapeDtypeStruct((bsz, seq, d_lru), F32)),
        grid=(n_tiles + 1,),
        in_specs=[
            pl.BlockSpec((1, tm, d), lambda s: (cur(s) // tiles, cur(s) % tiles, 0)),
            pl.BlockSpec((1, d), lambda s: (0, 0)),
            pl.BlockSpec((1, 1, d), lambda s: (cur(s) // tiles, 0, 0)),
            pl.BlockSpec((1, 1, d), lambda s: (cur(s) // tiles, 0, 0)),
            pl.BlockSpec((1, d, n), lambda s: (layer, 0, 0), pipeline_mode=pl.Buffered(1)),
            pl.BlockSpec((CONV_WIDTH, d_lru), lambda s: (0, 0)),
            pl.BlockSpec((1, d_lru), lambda s: (0, 0)),
            mat(), vec(), mat(), vec(), vec(),
        ],
        out_specs=(pl.BlockSpec((1, tm, n), lambda s: (cur(s) // tiles, cur(s) % tiles, 0)),
                   pl.BlockSpec((1, tm, d_lru), lambda s: (prev(s) // tiles, prev(s) % tiles, 0))),
        scratch_shapes=[
            pltpu.VMEM((d, n), BF16),
            pltpu.VMEM((tm + SUBLANES, d_lru), F32),
            pltpu.VMEM((tm, d_lru), F32),
            pltpu.VMEM((1, d_lru), F32),
            pltpu.VMEM((d_lru // LANES, tm, LANES), F32), pltpu.VMEM((d_lru // LANES, tm, LANES), F32),
            pltpu.VMEM((tm // SUBLANES, d_lru), F32),
        ],
        compiler_params=_params(("arbitrary",)),
        name="in_proj_lru_fwd",
    )(x, nw.reshape(1, d), sc, sh, w_in, conv_w, conv_b.reshape(1, d_lru), wa_bd, ba.reshape(2, 1, d_lru),
      wx_bd, bx.reshape(2, 1, d_lru), lam.reshape(2, 1, d_lru))


def _lru_kernel(x_ref, xp_ref, xn_ref, cw_ref, cb_ref, wa_ref, ba_ref, wx_ref, bx_ref, lam_ref, *rest,
                reverse, n_chunks, rows, accumulate):
    add_ref = rest[0] if accumulate else None
    o_ref, carry_ref, sa_ref, sb_ref, cin_ref = rest[1:] if accumulate else rest
    c = pl.program_id(1)
    chunk = (n_chunks - 1 - c) if reverse else c

    @pl.when(c == 0)
    def _():
        carry_ref[...] = jnp.zeros_like(carry_ref)

    has_prev = jnp.where(chunk > 0, 1.0, 0.0)
    has_next = jnp.where(chunk < n_chunks - 1, 1.0, 0.0)

    def store(rs, h_rows):
        o_ref[0, rs, :] = (add_ref[0, rs, :] + h_rows) if accumulate else h_rows

    _interleave(_lru_steps(x_ref[0], xp_ref[0] * has_prev, xn_ref[0] * has_next, carry_ref[...],
                           cw_ref[...], cb_ref[...], wa_ref[0], ba_ref[0], wx_ref[0], bx_ref[0], lam_ref[0],
                           store, carry_ref, sa_ref, sb_ref, cin_ref, reverse=reverse))


def _lru_steps(x, xp, xn, carry, cw, cb, wa, ba, wx, bx, lam, store, carry_ref, sa_ref, sb_ref, cin_ref, *,
               reverse):
    rows, width = x.shape
    xe = jnp.concatenate([xp, x, xn], axis=0)
    xc = cb
    for k in range(CONV_WIDTH):
        lo = SUBLANES + k - CONV_WIDTH // 2
        xc = xc + cw[k:k + 1] * xe[lo:lo + rows]
    yield

    xcb = xc.astype(BF16)
    r = _sigmoid_tanh(_dot(xcb, wa) + ba)
    yield
    gate_i = _sigmoid_tanh(_dot(xcb, wx) + bx)
    yield
    softplus_neg_lam = jnp.maximum(-lam, 0.0) + jnp.log1p(jnp.exp(-jnp.abs(lam)))
    log_a = (-LRU_C) * r * softplus_neg_lam
    a = jnp.exp(log_a)
    t = jnp.tanh(-log_a)
    u = jnp.sqrt(2.0 * t / (1.0 + t)) * (gate_i * xc)
    yield

    groups = rows // SUBLANES
    acc_a = a.reshape(groups, SUBLANES, width)
    acc_b = u.reshape(groups, SUBLANES, width)
    sub = lax.broadcasted_iota(jnp.int32, (groups, SUBLANES, width), 1)
    s = 1
    while s < SUBLANES:
        if reverse:
            valid = sub < SUBLANES - s
            sh_a, sh_b = pltpu.roll(acc_a, SUBLANES - s, 1), pltpu.roll(acc_b, SUBLANES - s, 1)
        else:
            valid = sub >= s
            sh_a, sh_b = pltpu.roll(acc_a, s, 1), pltpu.roll(acc_b, s, 1)
        acc_b = jnp.where(valid, acc_a * sh_b + acc_b, acc_b)
        acc_a = jnp.where(valid, acc_a * sh_a, acc_a)
        s *= 2
        yield
    acc_a = acc_a.reshape(rows, width)
    acc_b = acc_b.reshape(rows, width)
    edge = 0 if reverse else SUBLANES - 1
    n_tiles = width // LANES
    for j in range(n_tiles):
        sa_ref[j] = acc_a[:, j * LANES:(j + 1) * LANES]
        sb_ref[j] = acc_b[:, j * LANES:(j + 1) * LANES]
    ea = jnp.concatenate([sa_ref[j, pl.ds(edge, groups, stride=SUBLANES), :] for j in range(n_tiles)], axis=1)
    eb = jnp.concatenate([sb_ref[j, pl.ds(edge, groups, stride=SUBLANES), :] for j in range(n_tiles)], axis=1)
    grow = lax.broadcasted_iota(jnp.int32, (groups, width), 0)
    s = 1
    while s < groups:
        if reverse:
            valid = grow < groups - s
            sh_a, sh_b = pltpu.roll(ea, groups - s, 0), pltpu.roll(eb, groups - s, 0)
        else:
            valid = grow >= s
            sh_a, sh_b = pltpu.roll(ea, s, 0), pltpu.roll(eb, s, 0)
        eb = jnp.where(valid, ea * sh_b + eb, eb)
        ea = jnp.where(valid, ea * sh_a, ea)
        s *= 2
    yield
    group_out = eb + ea * carry
    if reverse:
        carry_in = jnp.where(grow == groups - 1, carry, pltpu.roll(group_out, groups - 1, 0))
        carry_ref[...] = group_out[0:1]
    else:
        carry_in = jnp.where(grow == 0, carry, pltpu.roll(group_out, 1, 0))
        carry_ref[...] = group_out[groups - 1:groups]
    cin_ref[...] = carry_in
    for g in range(groups):
        rs = slice(g * SUBLANES, (g + 1) * SUBLANES)
        store(rs, acc_b[rs] + acc_a[rs] * cin_ref[g:g + 1, :])
        if g % (groups // 4) == groups // 4 - 1:
            yield


def _lru_scan(proj, conv_w, conv_b, wa_bd, ba, wx_bd, bx, lam, *, reverse, add_to=None, rows=512):
    bsz, seq, _ = proj.shape
    d_lru = conv_w.shape[1]
    n_chunks = seq // rows
    halo = rows // SUBLANES
    last_halo = seq // SUBLANES - 1
    dirn = 1 if reverse else 0

    def chunk_of(c):
        return (n_chunks - 1 - c) if reverse else c

    vec = lambda: pl.BlockSpec((1, 1, d_lru), lambda b, c: (dirn, 0, 0))
    mat = lambda: pl.BlockSpec((1, d_lru, d_lru), lambda b, c: (dirn, 0, 0))
    accumulate = add_to is not None
    kern = functools.partial(_lru_kernel, reverse=reverse, n_chunks=n_chunks, rows=rows, accumulate=accumulate)
    tile = pl.BlockSpec((1, rows, d_lru), lambda b, c: (b, chunk_of(c), 0))
    return pl.pallas_call(
        kern,
        out_shape=jax.ShapeDtypeStruct((bsz, seq, d_lru), F32),
        grid=(bsz, n_chunks),
        in_specs=[
            pl.BlockSpec((1, rows, d_lru), lambda b, c: (b, chunk_of(c), 0)),
            pl.BlockSpec((1, SUBLANES, d_lru),
                         lambda b, c: (b, jnp.maximum(chunk_of(c) * halo - 1, 0), 0)),
            pl.BlockSpec((1, SUBLANES, d_lru),
                         lambda b, c: (b, jnp.minimum((chunk_of(c) + 1) * halo, last_halo), 0)),
            pl.BlockSpec((CONV_WIDTH, d_lru), lambda b, c: (0, 0)),
            pl.BlockSpec((1, d_lru), lambda b, c: (0, 0)),
            mat(), vec(), mat(), vec(), vec(),
        ] + ([tile] if accumulate else []),
        out_specs=tile,
        scratch_shapes=[pltpu.VMEM((1, d_lru), F32), pltpu.VMEM((d_lru // LANES, rows, LANES), F32),
                        pltpu.VMEM((d_lru // LANES, rows, LANES), F32),
                        pltpu.VMEM((rows // SUBLANES, d_lru), F32)],
        compiler_params=_params(("arbitrary", "arbitrary")),
        name="lru_bwd" if reverse else "lru_fwd",
    )(proj, proj, proj, conv_w, conv_b.reshape(1, d_lru), wa_bd, ba.reshape(2, 1, d_lru),
      wx_bd, bx.reshape(2, 1, d_lru), lam.reshape(2, 1, d_lru), *([add_to] if accumulate else []))


def _hgrn_direction(rev, q_ref, f_ref, v_ref, lb_ref, o_ref, st_ref, diag_s, lvl_s, upd_s, qe_s, btot_s, *, rows):
    ck, sb = HGRN_CHUNK, HGRN_SUB
    n_blk = ck // sb
    sb_shift = sb.bit_length() - 1
    n_sub = rows // ck
    width = q_ref.shape[-1]
    n_pairs = width // LANES
    half = LANES // 2

    def flip(idx, n):
        return (n - 1 - idx) if rev else idx

    n_lvl = n_blk.bit_length() - 1
    tf = flip(lax.broadcasted_iota(jnp.int32, (ck, ck), 0), ck)
    uf = flip(lax.broadcasted_iota(jnp.int32, (ck, ck), 1), ck)
    tb, ub = tf >> sb_shift, uf >> sb_shift
    pb = flip(lax.broadcasted_iota(jnp.int32, (n_blk, ck), 0), n_blk)
    pub = flip(lax.broadcasted_iota(jnp.int32, (n_blk, ck), 1), ck) >> sb_shift
    mats = [jnp.where((tb == ub) & (uf <= tf), 1.0, 0.0),
            jnp.where(pub < pb, 1.0, 0.0)]
    for lvl in range(n_lvl):
        mid = ((pb >> (lvl + 1)) << (lvl + 1)) + (1 << lvl)
        mats.append(jnp.where(pub < mid, 1.0, 0.0))
    mats.append(jnp.ones((SUBLANES, ck), F32))
    m_cum = jnp.concatenate(mats, axis=0).astype(BF16)

    def per_block(rows8):
        return jnp.concatenate(
            [jnp.broadcast_to(rows8[jb:jb + 1], (sb, rows8.shape[1])) for jb in range(n_blk)], axis=0)
    row_blk = flip(lax.broadcasted_iota(jnp.int32, (ck, width), 0), ck) >> sb_shift
    upper = [((row_blk >> lvl) & 1) == 1 for lvl in range(n_lvl)]
    pr = flip(lax.broadcasted_iota(jnp.int32, (ck, LANES), 0), ck) >> sb_shift
    pc = flip(lax.broadcasted_iota(jnp.int32, (ck, LANES), 1) & (ck - 1), ck) >> sb_shift
    group_mask = [(pr >> (lvl + 1)) == (pc >> (lvl + 1)) for lvl in range(n_lvl)]
    lane = lax.broadcasted_iota(jnp.int32, (1, LANES), 1)
    head0 = lane < half

    def split_heads(x):
        xb = x.astype(BF16)
        zero = jnp.zeros_like(xb)
        return jnp.concatenate([jnp.where(head0, xb, zero), jnp.where(head0, zero, xb)], axis=0)

    sr = lax.broadcasted_iota(jnp.int32, (LANES, LANES), 0)
    sc = lax.broadcasted_iota(jnp.int32, (LANES, LANES), 1)
    same_head = (sr < half) == (sc < half)
    er = lax.broadcasted_iota(jnp.int32, (sb * LANES, LANES), 0)
    ec = lax.broadcasted_iota(jnp.int32, (sb * LANES, LANES), 1)
    sel = jnp.where(ec == (((er & (LANES - 1)) >> (half.bit_length() - 1)) * half + (er >> (LANES.bit_length() - 1))),
                    1.0, 0.0).astype(BF16)
    sub_row = flip(lax.broadcasted_iota(jnp.int32, (sb, LANES), 0), sb)
    lbv = lb_ref[...]

    def row_start(j):
        return pl.multiple_of(flip(j, n_sub) * ck, ck)

    def stage1a(j):
        r0 = row_start(j)
        q = q_ref[0, pl.ds(r0, ck), :]
        z = f_ref[0, pl.ds(r0, ck), :]
        v = v_ref[0, pl.ds(r0, ck), :]
        f = lbv + (1.0 - lbv) * _sigmoid(z)
        lf2 = jnp.log(f) * LOG2E
        k = 1.0 - f
        return q, v, k, _dot01_exact(m_cum, lf2)

    def stage1b(q, v, k, cums):
        bl2 = cums[0:ck]
        b2 = bl2 + per_block(cums[ck:ck + n_blk])
        tot_row = ck + (1 + n_lvl) * n_blk
        btot2 = cums[tot_row:tot_row + 1]
        log2_k = jnp.log(k) * LOG2E
        kb = b2 - log2_k
        kbl = bl2 - log2_k
        qe = q * jnp.exp2(b2)
        ke = jnp.exp2(btot2 - kb)
        q_lvl, k_lvl = [], []
        for lvl in range(n_lvl):
            split2 = per_block(cums[ck + (1 + lvl) * n_blk:ck + (2 + lvl) * n_blk])
            q_lvl.append(q * jnp.exp2(jnp.where(upper[lvl], b2 - split2, NEG_BIG)))
            k_lvl.append(jnp.exp2(jnp.where(upper[lvl], NEG_BIG, split2 - kb)))

        qe_s[...] = qe.astype(BF16)
        btot_s[...] = btot2
        for p in range(n_pairs):
            sl = slice(p * LANES, (p + 1) * LANES)
            diag_rows = []
            for jb in range(n_blk):
                rs = slice(jb * sb, (jb + 1) * sb)
                bl_b, kbl_b, q_b = bl2[rs, sl], kbl[rs, sl], q[rs, sl]
                terms = []
                for s in range(sb):
                    arg = jnp.where(sub_row >= flip(s, sb), bl_b - kbl_b[s:s + 1], NEG_BIG)
                    terms.append(q_b * jnp.exp2(arg))
                diag_rows.append(jnp.concatenate(terms, axis=1))
            diag_s[p] = _dot(jnp.concatenate(diag_rows, axis=0).astype(BF16), sel)
            for lvl in range(n_lvl):
                k_p = k_lvl[lvl][:, sl]
                lvl_s[p * n_lvl + lvl] = _dot_nt(q_lvl[lvl][:, sl].astype(BF16), split_heads(k_p))
            upd_s[p] = _dot_tn(v[:, sl].astype(BF16), ke[:, sl].astype(BF16))

    def stage2_issue(j):
        r0 = row_start(j)
        v = v_ref[0, pl.ds(r0, ck), :]
        out = []
        for p in range(n_pairs):
            sl = slice(p * LANES, (p + 1) * LANES)
            parts = []
            for jb in range(n_blk):
                blk = diag_s[p, jb * sb:(jb + 1) * sb, :]
                parts.append(pltpu.roll(blk, jb * sb, 1) if jb else blk)
            scores = jnp.concatenate(parts, axis=0)
            for lvl in range(n_lvl):
                s_lvl = lvl_s[p * n_lvl + lvl]
                scores = scores + (s_lvl if lvl == n_lvl - 1 else jnp.where(group_mask[lvl], s_lvl, 0.0))
            intra = _dot(scores.astype(BF16), split_heads(v[:, sl]))
            st = st_ref[p]
            inter = _dot_nt(qe_s[:, sl], st.astype(BF16))
            new_st = jnp.where(same_head, st * jnp.exp2(btot_s[:, sl]) + upd_s[p], 0.0)
            out.append((inter + intra, new_st))
        return r0, out

    def stage2_finish(r0, out):
        for p in range(n_pairs):
            o_ref[0, pl.ds(r0, ck), p * LANES:(p + 1) * LANES] = out[p][0]
            st_ref[p] = out[p][1]

    return stage1a, stage1b, stage2_issue, stage2_finish


N_HGRN_SCRATCH = 6


def _hgrn_kernel(qf_ref, ff_ref, vf_ref, qb_ref, fb_ref, vb_ref, lb_ref, of_ref, ob_ref, zero_ref, *scratch,
                 rows):
    fwd_scratch, bwd_scratch = scratch[:N_HGRN_SCRATCH], scratch[N_HGRN_SCRATCH:]

    @pl.when(pl.program_id(1) == 0)
    def _():
        fwd_scratch[0][...] = jnp.zeros_like(fwd_scratch[0])
        bwd_scratch[0][...] = jnp.zeros_like(bwd_scratch[0])

    f1a, f1b, f2, f3 = _hgrn_direction(False, qf_ref, ff_ref, vf_ref, lb_ref, of_ref, *fwd_scratch, rows=rows)
    b1a, b1b, b2, b3 = _hgrn_direction(True, qb_ref, fb_ref, vb_ref, lb_ref, ob_ref, *bwd_scratch, rows=rows)
    n_sub = rows // HGRN_CHUNK

    def stage1_both(j):
        fa = f1a(j)
        ba = b1a(j)
        f1b(*fa)
        b1b(*ba)

    zero_part = zero_ref.shape[0] // n_sub

    def store_zeros(j):
        zero_ref[pl.ds(pl.multiple_of(j * zero_part, SUBLANES), zero_part), :] = jnp.zeros(
            (zero_part, zero_ref.shape[1]), zero_ref.dtype)

    stage1_both(0)

    def pipelined(j, carry):
        fo = f2(j)
        bo = b2(j)
        store_zeros(j)
        stage1_both(j + 1)
        f3(*fo)
        b3(*bo)
        return carry

    lax.fori_loop(0, n_sub - 1, pipelined, 0)
    fo = f2(n_sub - 1)
    bo = b2(n_sub - 1)
    store_zeros(n_sub - 1)
    f3(*fo)
    b3(*bo)


def _hgrn(proj, lb, zero_rows, *, d_lru, d_hgrn, rows=512):
    bsz, seq, _ = proj.shape
    n_chunks = seq // rows
    zero_blk = zero_rows // (bsz * n_chunks)
    assert zero_blk * bsz * n_chunks == zero_rows and zero_blk % (SUBLANES * (rows // HGRN_CHUNK)) == 0
    assert 2 * (d_hgrn // HGRN_HEADS) == LANES and HGRN_SUB == SUBLANES and HGRN_CHUNK == HGRN_SUB * SUBLANES
    col0 = (2 * d_lru) // d_hgrn
    n_pairs = d_hgrn // LANES
    n_lvl = (HGRN_CHUNK // HGRN_SUB).bit_length() - 1
    fwd = lambda col: pl.BlockSpec((1, rows, d_hgrn), lambda b, c: (b, c, col))
    bwd = lambda col: pl.BlockSpec((1, rows, d_hgrn), lambda b, c: (b, n_chunks - 1 - c, col))
    direction_scratch = [
        pltpu.VMEM((n_pairs, LANES, LANES), F32),
        pltpu.VMEM((n_pairs, HGRN_CHUNK, LANES), F32),
        pltpu.VMEM((n_pairs * n_lvl, HGRN_CHUNK, LANES), F32),
        pltpu.VMEM((n_pairs, LANES, LANES), F32),
        pltpu.VMEM((HGRN_CHUNK, d_hgrn), BF16),
        pltpu.VMEM((1, d_hgrn), F32),
    ]
    assert len(direction_scratch) == N_HGRN_SCRATCH
    kern = functools.partial(_hgrn_kernel, rows=rows)
    out = jax.ShapeDtypeStruct((bsz, seq, d_hgrn), F32)
    return pl.pallas_call(
        kern,
        out_shape=(out, out, jax.ShapeDtypeStruct((zero_rows, LANES), F32)),
        grid=(bsz, n_chunks),
        in_specs=[fwd(col0), fwd(col0 + 1), fwd(col0 + 3), bwd(col0), bwd(col0 + 2), bwd(col0 + 3),
                  pl.BlockSpec((1, d_hgrn), lambda b, c: (0, 0))],
        out_specs=(pl.BlockSpec((1, rows, d_hgrn), lambda b, c: (b, c, 0)),
                   pl.BlockSpec((1, rows, d_hgrn), lambda b, c: (b, n_chunks - 1 - c, 0)),
                   pl.BlockSpec((zero_blk, LANES), lambda b, c: (b * n_chunks + c, 0))),
        scratch_shapes=direction_scratch + direction_scratch,
        compiler_params=_params(("arbitrary", "arbitrary")),
        name="hgrn2",
    )(proj, proj, proj, proj, proj, proj, lb.reshape(1, d_hgrn))


def _interleave(*streams):
    done = object()
    live = list(streams)
    while live:
        live = [s for s in live if next(s, done) is not done]


def _gelu_tanh(y):
    return 0.5 * y * (1.0 + jnp.tanh(0.7978845608028654 * (y + 0.044715 * (y * y * y))))


def _post_kernel(lru_ref, y_ref, of_ref, ob_ref, g_ref, x_ref, nlw_ref, nhw_ref, wo_ref,
                 gm_ref, nfw_ref, scf_ref, shf_ref, wr_ref, br_ref,
                 xo_ref, h_ref, slab_ref, cnt_ref, route_ref, carry_ref, logits_s, *, tm, d_lru):
    step = pl.program_id(0)

    @pl.when(step == 0)
    def _():
        carry_ref[...] = jnp.zeros_like(carry_ref)
        logits_s[...] = jnp.zeros_like(logits_s)

    def mixer():
        lru = lru_ref[0] * _gelu_tanh(y_ref[0])
        ms = jnp.mean(lru * lru, axis=-1, keepdims=True)
        yield
        lru = lru * lax.rsqrt(ms + NORM_EPS) * nlw_ref[...]

        hg = of_ref[0] + ob_ref[0]
        width = hg.shape[1]
        hd = width // HGRN_HEADS
        hd_shift = hd.bit_length() - 1
        er = lax.broadcasted_iota(jnp.int32, (width, width), 0) >> hd_shift
        ec = lax.broadcasted_iota(jnp.int32, (width, width), 1) >> hd_shift
        head_sum = jnp.where(er == ec, 1.0, 0.0).astype(BF16)
        sq = hg * hg
        sq_hi = sq.astype(BF16)
        sq_lo = (sq - sq_hi.astype(F32)).astype(BF16)
        ms_h = (_dot(sq_hi, head_sum) + _dot(sq_lo, head_sum)) * (1.0 / hd)
        yield
        g = g_ref[0]
        hg = (hg * lax.rsqrt(ms_h + NORM_EPS) * nhw_ref[...]) * (g * _sigmoid(g))
        yield
        mixed = _dot(lru.astype(BF16), wo_ref[0:d_lru, :])
        yield
        mixed = mixed + _dot(hg.astype(BF16), wo_ref[d_lru:, :])
        yield
        x_new = x_ref[0] + gm_ref[0] * mixed
        xo_ref[0] = x_new
        ms_f = jnp.mean(x_new * x_new, axis=-1, keepdims=True)
        yield
        h = (x_new * lax.rsqrt(ms_f + NORM_EPS) * nfw_ref[...]) * (1.0 + scf_ref[0]) + shf_ref[0]
        _tiles_store(h_ref, h, tm, lead=(0,))
        yield
        logits_s[...] = _dot(h.astype(BF16), wr_ref[...]) + br_ref[...]

    def routing():
        yield from _routing_steps(logits_s[...], jnp.where(step > 0, 1.0, 0.0), slab_ref, cnt_ref, route_ref,
                                  carry_ref, tm)

    _interleave(routing(), mixer())


def _routing_steps(logits, live, slab_ref, cnt_ref, route_ref, carry_ref, tm):
    lane = lax.broadcasted_iota(jnp.int32, (tm, ROUTE_LANES), 1)
    lane_f = lane.astype(F32)
    far = float(ROUTE_LANES)
    is_g = lane < N_GROUPS
    gl = jnp.where(is_g, logits, NEG_BIG)
    gmax = jnp.max(gl, axis=-1, keepdims=True)
    yield
    g_idx = jnp.min(jnp.where(gl == gmax, lane_f, far), axis=-1, keepdims=True)
    p_group = 1.0 / jnp.sum(jnp.where(is_g, jnp.exp(gl - gmax), 0.0), axis=-1, keepdims=True)
    yield
    e_lane = lane - N_GROUPS
    in_group = (e_lane >= 0) & (e_lane < N_EXPERTS) & ((e_lane >> (EXPERTS_PER_GROUP.bit_length() - 1)).astype(F32) == g_idx)
    ev = jnp.where(in_group, logits, NEG_BIG)
    top1 = jnp.max(ev, axis=-1, keepdims=True)
    yield
    i1 = jnp.min(jnp.where(in_group & (ev == top1), lane_f, far), axis=-1, keepdims=True)
    yield
    rest = in_group & (lane_f != i1)
    ev2 = jnp.where(rest, logits, NEG_BIG)
    top2 = jnp.max(ev2, axis=-1, keepdims=True)
    yield
    i2 = jnp.min(jnp.where(rest & (ev2 == top2), lane_f, far), axis=-1, keepdims=True)
    yield
    e1 = i1 - float(N_GROUPS)
    e2 = i2 - float(N_GROUPS)
    ex = jnp.exp(top2 - top1)
    w1 = p_group / (1.0 + ex)
    w2 = p_group * ex / (1.0 + ex)

    sel1 = lane_f == e1
    sel2 = lane_f == e2
    onehot = jnp.where(sel1 | sel2, live, 0.0)
    tr = lax.broadcasted_iota(jnp.int32, (tm, tm), 0)
    tc = lax.broadcasted_iota(jnp.int32, (tm, tm), 1)
    before = jnp.where(tc < tr, 1.0, 0.0).astype(BF16)
    cnt = _dot(before, onehot.astype(BF16)) + carry_ref[0:1]
    yield
    rank1 = jnp.sum(jnp.where(sel1, cnt, 0.0), axis=-1, keepdims=True)
    rank2 = jnp.sum(jnp.where(sel2, cnt, 0.0), axis=-1, keepdims=True)
    total = carry_ref[0:1] + jnp.sum(onehot, axis=0, keepdims=True)
    carry_ref[...] = jnp.broadcast_to(total, carry_ref.shape)
    cnt_ref[...] = jnp.broadcast_to(total, cnt_ref.shape)
    yield

    slab = jnp.where(lane == 0, e1, 0.0)
    slab = jnp.where(lane == 1, e2, slab)
    slab = jnp.where(lane == 2, w1, slab)
    slab = jnp.where(lane == 3, w2, slab)
    slab = jnp.where(lane == 4, rank1, slab)
    slab = jnp.where(lane == 5, rank2, slab)
    slab_ref[0] = slab
    route_ref[...] = slab.T[0:SUBLANES]


def _post_mixer(lru_sum, proj, hg_f, hg_b, x, nlw, nhw, wo_bf16, g_mix, nfw, sc_ffn, sh_ffn, wr_bf16, br,
                *, tm=512):
    bsz, seq, d = x.shape
    d_lru = lru_sum.shape[-1]
    d_hgrn = hg_f.shape[-1]
    y_col = 1
    g_col = (2 * d_lru) // d_hgrn + 4
    tiles = seq // tm
    n_tiles = bsz * tiles
    cur = lambda s: jnp.minimum(s, n_tiles - 1)
    prev = lambda s: jnp.maximum(s - 1, 0)
    row = lambda w, col=0: pl.BlockSpec((1, tm, w), lambda s: (cur(s) // tiles, cur(s) % tiles, col))
    vec = lambda w: pl.BlockSpec((1, w), lambda s: (0, 0))
    per_b = lambda: pl.BlockSpec((1, 1, d), lambda s: (cur(s) // tiles, 0, 0))
    kern = functools.partial(_post_kernel, tm=tm, d_lru=d_lru)
    return pl.pallas_call(
        kern,
        out_shape=(
            jax.ShapeDtypeStruct((bsz, seq, d), F32),
            jax.ShapeDtypeStruct((bsz, seq * SUBLANES, LANES), F32),
            jax.ShapeDtypeStruct((bsz, seq, ROUTE_LANES), F32),
            jax.ShapeDtypeStruct((SUBLANES, ROUTE_LANES), F32),
            jax.ShapeDtypeStruct((SUBLANES, bsz * seq), F32),
        ),
        grid=(n_tiles + 1,),
        in_specs=[
            row(d_lru), row(d_lru, y_col),
            row(d_hgrn), row(d_hgrn), row(d_hgrn, g_col),
            row(d), vec(d_lru), vec(d_hgrn),
            pl.BlockSpec((d, d), lambda s: (0, 0)),
            per_b(), vec(d), per_b(), per_b(),
            pl.BlockSpec((d, ROUTE_LANES), lambda s: (0, 0)),
            vec(ROUTE_LANES),
        ],
        out_specs=(
            row(d),
            pl.BlockSpec((1, tm * SUBLANES, LANES), lambda s: (cur(s) // tiles, cur(s) % tiles, 0)),
            pl.BlockSpec((1, tm, ROUTE_LANES), lambda s: (prev(s) // tiles, prev(s) % tiles, 0)),
            pl.BlockSpec((SUBLANES, ROUTE_LANES), lambda s: (0, 0)),
            pl.BlockSpec((SUBLANES, tm), lambda s: (0, prev(s))),
        ),
        scratch_shapes=[pltpu.VMEM((SUBLANES, ROUTE_LANES), F32), pltpu.VMEM((tm, ROUTE_LANES), F32)],
        compiler_params=_params(("arbitrary",)),
        name="post_mixer_router",
    )(lru_sum, proj, hg_f, hg_b, proj, x, nlw.reshape(1, d_lru), nhw.reshape(1, d_hgrn), wo_bf16,
      g_mix, nfw.reshape(1, d), sc_ffn, sh_ffn, wr_bf16, br)


def _tiles_load(ref, n, lead=()):
    return jnp.concatenate(
        [ref[(*lead, pl.ds(j, n, stride=SUBLANES), slice(None))] for j in range(SUBLANES)], axis=1)


def _tiles_store(ref, val, n, lead=()):
    for j in range(SUBLANES):
        ref[(*lead, pl.ds(j, n, stride=SUBLANES), slice(None))] = val[:, j * LANES:(j + 1) * LANES]


def _token_tile(ref, t):
    return ref.at[pl.ds(pl.multiple_of(t * SUBLANES, SUBLANES), SUBLANES)]


def _dest_kernel(start_ref, route_ref, o_ref):
    route = route_ref[...].astype(jnp.int32)
    start = jnp.zeros_like(route)
    for e in range(N_EXPERTS):
        start = jnp.where(route == e, start_ref[e], start)
    o_ref[...] = start + pltpu.roll(route, SUBLANES // 2, 0)


def _dest_rows(expert_start, route):
    return pl.pallas_call(
        _dest_kernel,
        out_shape=jax.ShapeDtypeStruct(route.shape, jnp.int32),
        grid_spec=pltpu.PrefetchScalarGridSpec(
            num_scalar_prefetch=1,
            grid=(1,),
            in_specs=[pl.BlockSpec(route.shape, lambda i, s: (0, 0))],
            out_specs=pl.BlockSpec(route.shape, lambda i, s: (0, 0)),
        ),
        compiler_params=pltpu.CompilerParams(dimension_semantics=("arbitrary",)),
        name="moe_dest_rows",
    )(expert_start, route)


def _dispatch_kernel(d1_ref, d2_ref, h_ref, z_ref, o_ref, sem, *, tb):
    del z_ref
    base = pl.program_id(0) * tb

    def issue(r, carry):
        t = base + r
        pltpu.make_async_copy(_token_tile(h_ref, r), _token_tile(o_ref, d1_ref[t]), sem).start(priority=0)
        pltpu.make_async_copy(_token_tile(h_ref, r), _token_tile(o_ref, d2_ref[t]), sem).start(priority=1)
        return carry

    lax.fori_loop(0, tb, issue, 0, unroll=DMA_ISSUE_UNROLL)
    for _ in range(2):
        pltpu.make_async_copy(h_ref, o_ref.at[pl.ds(0, tb * SUBLANES)], sem).wait()


def _dispatch(dest1, dest2, h_tiles, zero_tiles, *, tb=512):
    m = h_tiles.shape[0] // SUBLANES
    n_rows = zero_tiles.shape[0] // SUBLANES
    kern = functools.partial(_dispatch_kernel, tb=tb)
    return pl.pallas_call(
        kern,
        out_shape=jax.ShapeDtypeStruct((n_rows * SUBLANES, LANES), h_tiles.dtype),
        grid_spec=pltpu.PrefetchScalarGridSpec(
            num_scalar_prefetch=2,
            grid=(m // tb,),
            in_specs=[pl.BlockSpec((tb * SUBLANES, LANES), lambda i, d1, d2: (i, 0)),
                      pl.BlockSpec(memory_space=pl.ANY)],
            out_specs=pl.BlockSpec(memory_space=pl.ANY),
            scratch_shapes=[pltpu.SemaphoreType.DMA(())],
        ),
        input_output_aliases={3: 0},
        compiler_params=pltpu.CompilerParams(dimension_semantics=("arbitrary",), has_side_effects=True),
        name="moe_dispatch",
    )(dest1, dest2, h_tiles, zero_tiles)


def _expert_kernel(plan_ref, x_ref, wg_hbm, wu_hbm, wd_hbm, o_ref, wg_f, wu_f, wd_f, wg_s, wu_s, wd_s, sem,
                   *, layer):
    i = pl.program_id(0)
    n_blocks = pl.num_programs(0)
    n_used = plan_ref[n_blocks]
    expert = plan_ref[i]
    next_expert = plan_ref[n_blocks + 1 + i]
    slot = plan_ref[2 * n_blocks + 1 + i]
    first_block = ((i == 0) | (plan_ref[jnp.maximum(i - 1, 0)] != expert)) & (i < n_used)

    def copies(e, s):
        return (pltpu.make_async_copy(wg_hbm.at[layer, e], wg_f.at[s], sem.at[s]),
                pltpu.make_async_copy(wu_hbm.at[layer, e], wu_f.at[s], sem.at[s]),
                pltpu.make_async_copy(wd_hbm.at[layer, e], wd_f.at[s], sem.at[s]))

    @pl.when(i == 0)
    def _():
        for c in copies(expert, slot):
            c.start()

    @pl.when(first_block)
    def _():
        for c in copies(expert, slot):
            c.wait()
        wg_s[...] = wg_f[slot].astype(BF16)
        wu_s[...] = wu_f[slot].astype(BF16)
        wd_s[...] = wd_f[slot].astype(BF16)

        @pl.when(next_expert >= 0)
        def _():
            for c in copies(next_expert, 1 - slot):
                c.start()

    blk = x_ref.shape[0] // SUBLANES

    @pl.when(i < n_used)
    def _():
        x = _tiles_load(x_ref, blk).astype(BF16)
        gate = _dot(x, wg_s[...])
        up = _dot(x, wu_s[...])
        act = (gate * _sigmoid(gate)) * up
        _tiles_store(o_ref, _dot(act.astype(BF16), wd_s[...]), blk)

    @pl.when(i >= n_used)
    def _():
        o_ref[...] = jnp.zeros_like(o_ref)


def _expert_plan(blk_expert, blocks_used):
    n_blocks = blk_expert.shape[0]
    idx = jnp.arange(n_blocks, dtype=jnp.int32)
    change = jnp.concatenate([jnp.ones((1,), bool), blk_expert[1:] != blk_expert[:-1]])
    slot = (jnp.cumsum(change.astype(jnp.int32)) - 1) & 1
    change_at = jnp.where(change, idx, n_blocks)
    from_here = lax.cummin(change_at[::-1])[::-1]
    next_change = jnp.concatenate([from_here[1:], jnp.full((1,), n_blocks, jnp.int32)])
    next_expert = jnp.where(next_change < blocks_used, blk_expert[jnp.minimum(next_change, n_blocks - 1)], -1)
    return jnp.concatenate([blk_expert, blocks_used.reshape(1), next_expert, slot]).astype(jnp.int32)


def _experts(plan, x_tiles, wg, wu, wd, layer):
    n_rows = x_tiles.shape[0] // SUBLANES
    d, de = wg.shape[-2:]
    blk = EXPERT_BLOCK
    n_blocks = n_rows // blk
    kern = functools.partial(_expert_kernel, layer=layer)
    return pl.pallas_call(
        kern,
        out_shape=jax.ShapeDtypeStruct((n_rows * SUBLANES, LANES), F32),
        grid_spec=pltpu.PrefetchScalarGridSpec(
            num_scalar_prefetch=1,
            grid=(n_blocks,),
            in_specs=[
                pl.BlockSpec((blk * SUBLANES, LANES), lambda i, plan: (jnp.minimum(i, plan[n_blocks] - 1), 0)),
                pl.BlockSpec(memory_space=pl.ANY),
                pl.BlockSpec(memory_space=pl.ANY),
                pl.BlockSpec(memory_space=pl.ANY),
            ],
            out_specs=pl.BlockSpec((blk * SUBLANES, LANES), lambda i, plan: (i, 0)),
            scratch_shapes=[
                pltpu.VMEM((2, d, de), F32), pltpu.VMEM((2, d, de), F32), pltpu.VMEM((2, de, d), F32),
                pltpu.VMEM((d, de), BF16), pltpu.VMEM((d, de), BF16), pltpu.VMEM((de, d), BF16),
                pltpu.SemaphoreType.DMA((2,)),
            ],
        ),
        compiler_params=_params(("arbitrary",)),
        name="moe_experts",
    )(plan, x_tiles, wg, wu, wd)


def _combine_kernel(d1_ref, d2_ref, y_ref, slab_ref, x_ref, g_ref, nw_ref, o_ref, ra0, rb0, ra1, rb1, sem,
                    *, tm, tiles, n_steps, final_norm):
    step = pl.program_id(0) * tiles + pl.program_id(1)
    bufs = ((ra0, rb0), (ra1, rb1))

    def gather(tile, slot):
        base = tile * tm
        r1_ref, r2_ref = bufs[slot]

        def issue(r, carry):
            t = base + r
            pltpu.make_async_copy(_token_tile(y_ref, d1_ref[t]), _token_tile(r1_ref, r),
                                  sem.at[slot]).start(priority=0)
            pltpu.make_async_copy(_token_tile(y_ref, d2_ref[t]), _token_tile(r2_ref, r),
                                  sem.at[slot]).start(priority=1)
            return carry

        lax.fori_loop(0, tm, issue, 0, unroll=DMA_ISSUE_UNROLL)

    @pl.when(step == 0)
    def _():
        gather(0, 0)

    for slot in range(2):
        @pl.when((step & 1) == slot)
        def _():
            @pl.when(step + 1 < n_steps)
            def _():
                gather(step + 1, 1 - slot)

            r1_ref, r2_ref = bufs[slot]
            pltpu.make_async_copy(y_ref.at[pl.ds(0, tm * SUBLANES)], r1_ref, sem.at[slot]).wait()
            pltpu.make_async_copy(y_ref.at[pl.ds(0, tm * SUBLANES)], r2_ref, sem.at[slot]).wait()
            slab = slab_ref[0]
            y = slab[:, 2:3] * _tiles_load(r1_ref, tm) + slab[:, 3:4] * _tiles_load(r2_ref, tm)
            out = x_ref[0] + g_ref[0] * y
            if final_norm:
                ms = jnp.mean(out * out, axis=-1, keepdims=True)
                out = out * lax.rsqrt(ms + NORM_EPS) * nw_ref[...]
            o_ref[0] = out


def _combine(dest1, dest2, y_buf, slab, x, g_ffn, norm_w, *, final_norm, tm=512):
    bsz, seq, d = x.shape
    tiles = seq // tm
    kern = functools.partial(_combine_kernel, tm=tm, tiles=tiles, n_steps=bsz * tiles, final_norm=final_norm)
    row_buf = pltpu.VMEM((tm * SUBLANES, LANES), F32)
    return pl.pallas_call(
        kern,
        out_shape=jax.ShapeDtypeStruct((bsz, seq, d), F32),
        grid_spec=pltpu.PrefetchScalarGridSpec(
            num_scalar_prefetch=2,
            grid=(bsz, tiles),
            in_specs=[
                pl.BlockSpec(memory_space=pl.ANY),
                pl.BlockSpec((1, tm, ROUTE_LANES), lambda b, i, d1, d2: (b, i, 0)),
                pl.BlockSpec((1, tm, d), lambda b, i, d1, d2: (b, i, 0)),
                pl.BlockSpec((1, 1, d), lambda b, i, d1, d2: (b, 0, 0)),
                pl.BlockSpec((1, d), lambda b, i, d1, d2: (0, 0)),
            ],
            out_specs=pl.BlockSpec((1, tm, d), lambda b, i, d1, d2: (b, i, 0)),
            scratch_shapes=[row_buf, row_buf, row_buf, row_buf, pltpu.SemaphoreType.DMA((2,))],
        ),
        compiler_params=_params(("arbitrary", "arbitrary")),
        name="moe_combine",
    )(dest1, dest2, y_buf, slab, x, g_ffn, norm_w.reshape(1, d))


def _block_diag(w):
    heads, hd, _ = w.shape
    n = heads * hd
    tiled = jnp.tile(w.reshape(n, hd), (1, heads))
    blk_r = lax.broadcasted_iota(jnp.int32, (n, n), 0) // hd
    blk_c = lax.broadcasted_iota(jnp.int32, (n, n), 1) // hd
    return jnp.where(blk_r == blk_c, tiled, 0.0)


def kernel(x, c, ada_w, ada_b, norm_mix_w, w_in, conv_w, conv_b, lru_wa, lru_ba, lru_wx, lru_bx, lru_lambda, norm_lru_w, hgrn_lb, norm_hgrn_w, w_out, norm_ffn_w, router_group_w, router_group_b, router_expert_w, router_expert_b, expert_w_gate, expert_w_up, expert_w_down, final_norm_w):
    bsz, seq, d = x.shape
    assert d == SUBLANES * LANES, "the MoE row movement keeps one (8, 128) tile per token"
    depth = ada_w.shape[0]
    d_lru = conv_w.shape[-1]
    d_hgrn = hgrn_lb.shape[-1]
    m = bsz * seq
    n_rows = m * 2 + N_EXPERTS * EXPERT_BLOCK
    n_blocks = n_rows // EXPERT_BLOCK

    mod = _modulation(c, ada_w, ada_b)
    lb_cum = jnp.cumsum(jax.nn.softmax(hgrn_lb.astype(F32), axis=0), axis=0)
    lb_all = lb_cum - lb_cum[0:1]

    for l in range(depth):
        sh_mix, sc_mix, g_mix, sh_ffn, sc_ffn, g_ffn = [
            mod[l, :, i * d:(i + 1) * d].reshape(bsz, 1, d) for i in range(N_MODULATIONS)]
        wa_bd = jnp.stack([_block_diag(lru_wa[l, 0]), _block_diag(lru_wa[l, 1])]).astype(BF16)
        wx_bd = jnp.stack([_block_diag(lru_wx[l, 0]), _block_diag(lru_wx[l, 1])]).astype(BF16)
        lru_w = (conv_w[l], conv_b[l], wa_bd, lru_ba[l], wx_bd, lru_bx[l], lru_lambda[l])
        proj, lru_fwd = _in_proj_lru(x, norm_mix_w[l], sc_mix, sh_mix, w_in, l, *lru_w)
        lru_sum = _lru_scan(proj, *lru_w, reverse=True, add_to=lru_fwd)
        hg_f, hg_b, zero_tiles = _hgrn(proj, lb_all[l], n_rows * SUBLANES, d_lru=d_lru, d_hgrn=d_hgrn)

        lane_pad = ROUTE_LANES - N_GROUPS - N_EXPERTS
        wr = jnp.pad(jnp.concatenate([router_group_w[l], router_expert_w[l]], axis=1), ((0, 0), (0, lane_pad)))
        br = jnp.pad(jnp.concatenate([router_group_b[l], router_expert_b[l]]), (0, lane_pad)).reshape(1, ROUTE_LANES)
        x_mid, h_ffn, slab, counts, route = _post_mixer(
            lru_sum, proj, hg_f, hg_b, x, norm_lru_w[l], norm_hgrn_w[l], w_out[l].astype(BF16), g_mix,
            norm_ffn_w[l], sc_ffn, sh_ffn, wr.astype(BF16), br)

        cnt = counts[0, :N_EXPERTS].astype(jnp.int32)
        padded = ((cnt + EXPERT_BLOCK - 1) // EXPERT_BLOCK) * EXPERT_BLOCK
        pend = jnp.cumsum(padded)
        pstart = pend - padded
        blk_start = jnp.arange(n_blocks, dtype=jnp.int32) * EXPERT_BLOCK
        blk_expert = jnp.minimum(jnp.sum(pend[None, :] <= blk_start[:, None], axis=1), N_EXPERTS - 1)
        plan = _expert_plan(blk_expert.astype(jnp.int32), (pend[N_EXPERTS - 1] // EXPERT_BLOCK).astype(jnp.int32))
        dest = _dest_rows(pstart.astype(jnp.int32), route)
        dest1, dest2 = dest[0], dest[1]

        x_buf = _dispatch(dest1, dest2, h_ffn.reshape(m * SUBLANES, LANES), zero_tiles)
        y_buf = _experts(plan, x_buf, expert_w_gate, expert_w_up, expert_w_down, l)
        x = _combine(dest1, dest2, y_buf, slab, x_mid, g_ffn, final_norm_w, final_norm=(l == depth - 1))

    return x
```

```python
import functools

import jax
import jax.numpy as jnp
from jax import lax
from jax.experimental import pallas as pl
from jax.experimental.pallas import tpu as pltpu

F32 = jnp.float32
BF16 = jnp.bfloat16

HGRN_HEADS = 8
N_MODULATIONS = 6
CONV_WIDTH = 4
LRU_C = 8.0
N_GROUPS = 4
EXPERTS_PER_GROUP = 8
N_EXPERTS = N_GROUPS * EXPERTS_PER_GROUP
NORM_EPS = 1e-6

LANES = 128
SUBLANES = 8
VMEM_LIMIT = 56 * 1024 * 1024

HGRN_CHUNK = 64
HGRN_SUB = 8
LOG2E = 1.4426950408889634
ROUTE_LANES = LANES
EXPERT_BLOCK = 512
X_RING = 3
DMA_ISSUE_UNROLL = 8
NEG_BIG = -3.0e38


def _params(sem):
    return pltpu.CompilerParams(dimension_semantics=sem, vmem_limit_bytes=VMEM_LIMIT)


def _dot(a, b):
    return jnp.dot(a, b, preferred_element_type=F32)


def _dot_nt(a, b):
    return lax.dot_general(a, b, (((1,), (1,)), ((), ())), preferred_element_type=F32)


def _dot_tn(a, b):
    return lax.dot_general(a, b, (((0,), (0,)), ((), ())), preferred_element_type=F32)


def _dot01_exact(m01, x):
    hi = x.astype(BF16)
    r1 = x - hi.astype(F32)
    mid = r1.astype(BF16)
    lo = (r1 - mid.astype(F32)).astype(BF16)
    return _dot(m01, hi) + _dot(m01, mid) + _dot(m01, lo)


def _sigmoid(x):
    return 1.0 / (1.0 + jnp.exp(-x))


def _sigmoid_tanh(x):
    return 0.5 * jnp.tanh(0.5 * x) + 0.5


def _mod_kernel(c_ref, w_ref, b_ref, o_ref):
    c = c_ref[...]
    cond = c * _sigmoid(c)
    o_ref[0] = _dot(cond.astype(BF16), w_ref[0].astype(BF16)) + b_ref[0]


def _modulation(c, ada_w, ada_b):
    depth, d, n = ada_w.shape
    bsz = c.shape[0]
    rows = -(-bsz // SUBLANES) * SUBLANES
    c_pad = jnp.pad(c, ((0, rows - bsz), (0, 0)))
    tn = n // N_MODULATIONS
    out = pl.pallas_call(
        _mod_kernel,
        out_shape=jax.ShapeDtypeStruct((depth, rows, n), F32),
        grid=(depth, n // tn),
        in_specs=[
            pl.BlockSpec((rows, d), lambda l, j: (0, 0)),
            pl.BlockSpec((1, d, tn), lambda l, j: (l, 0, j)),
            pl.BlockSpec((1, 1, tn), lambda l, j: (l, 0, j)),
        ],
        out_specs=pl.BlockSpec((1, rows, tn), lambda l, j: (l, 0, j)),
        compiler_params=_params(("arbitrary", "arbitrary")),
        name="adaln_mod",
    )(c_pad, ada_w, ada_b.reshape(depth, 1, n))
    return out[:, :bsz]


def _rms_mod(x, nw, sc, sh):
    ms = jnp.mean(x * x, axis=-1, keepdims=True)
    return (x * lax.rsqrt(ms + NORM_EPS) * nw) * (1.0 + sc) + sh


def _inproj_lru_kernel(x_ref, nw_ref, sc_ref, sh_ref, w_ref, cw_ref, cb_ref, wa_ref, ba_ref, wx_ref, bx_ref,
                       lam_ref, proj_ref, lru_ref, w_s, xl_s, xnew_s, carry_ref, sa_ref, sb_ref, cin_ref,
                       *, tm, tiles, d_lru):
    s = pl.program_id(0)
    n_cols = proj_ref.shape[-1]

    @pl.when(s == 0)
    def _():
        w_s[...] = w_ref[0].astype(BF16)
        xl_s[...] = jnp.zeros_like(xl_s)
        carry_ref[...] = jnp.zeros_like(carry_ref)

    prev = jnp.maximum(s - 1, 0)
    chunk = prev % tiles
    has_prev = jnp.where(chunk > 0, 1.0, 0.0)
    has_next = jnp.where(chunk < tiles - 1, 1.0, 0.0)
    hb = _rms_mod(x_ref[0], nw_ref[...], sc_ref[0], sh_ref[0]).astype(BF16)

    def project():
        step = 2 * LANES
        for c0 in range(0, n_cols, step):
            block = _dot(hb, w_s[:, c0:c0 + step])
            proj_ref[0, :, c0:c0 + step] = block
            if c0 < d_lru:
                xnew_s[:, c0:c0 + step] = block
            if c0 + step >= d_lru:
                yield

    def scan():
        yield

        def store(rs, h_rows):
            lru_ref[0, rs, :] = h_rows

        yield from _lru_steps(xl_s[SUBLANES:, :], xl_s[0:SUBLANES, :] * has_prev,
                              xnew_s[0:SUBLANES, :] * has_next, carry_ref[...] * has_prev,
                              cw_ref[...], cb_ref[...], wa_ref[0], ba_ref[0], wx_ref[0], bx_ref[0], lam_ref[0],
                              store, carry_ref, sa_ref, sb_ref, cin_ref, reverse=False)

    _interleave(project(), scan())
    xl_s[0:SUBLANES, :] = xl_s[tm:tm + SUBLANES, :]
    xl_s[SUBLANES:, :] = xnew_s[...]


def _in_proj_lru(x, nw, sc, sh, w_in, layer, conv_w, conv_b, wa_bd, ba, wx_bd, bx, lam, tm=512):
    bsz, seq, d = x.shape
    n = w_in.shape[-1]
    d_lru = conv_w.shape[1]
    tiles = seq // tm
    n_tiles = bsz * tiles
    cur = lambda s: jnp.minimum(s, n_tiles - 1)
    prev = lambda s: jnp.maximum(s - 1, 0)
    vec = lambda: pl.BlockSpec((1, 1, d_lru), lambda s: (0, 0, 0))
    mat = lambda: pl.BlockSpec((1, d_lru, d_lru), lambda s: (0, 0, 0))
    kern = functools.partial(_inproj_lru_kernel, tm=tm, tiles=tiles, d_lru=d_lru)
    return pl.pallas_call(
        kern,
        out_shape=(jax.ShapeDtypeStruct((bsz, seq, n), F32), jax.ShapeDtypeStruct((bsz, seq, d_lru), F32)),
        grid=(n_tiles + 1,),
        in_specs=[
            pl.BlockSpec((1, tm, d), lambda s: (cur(s) // tiles, cur(s) % tiles, 0)),
            pl.BlockSpec((1, d), lambda s: (0, 0)),
            pl.BlockSpec((1, 1, d), lambda s: (cur(s) // tiles, 0, 0)),
            pl.BlockSpec((1, 1, d), lambda s: (cur(s) // tiles, 0, 0)),
            pl.BlockSpec((1, d, n), lambda s: (layer, 0, 0), pipeline_mode=pl.Buffered(1)),
            pl.BlockSpec((CONV_WIDTH, d_lru), lambda s: (0, 0)),
            pl.BlockSpec((1, d_lru), lambda s: (0, 0)),
            mat(), vec(), mat(), vec(), vec(),
        ],
        out_specs=(pl.BlockSpec((1, tm, n), lambda s: (cur(s) // tiles, cur(s) % tiles, 0)),
                   pl.BlockSpec((1, tm, d_lru), lambda s: (prev(s) // tiles, prev(s) % tiles, 0))),
        scratch_shapes=[
            pltpu.VMEM((d, n), BF16),
            pltpu.VMEM((tm + SUBLANES, d_lru), F32),
            pltpu.VMEM((tm, d_lru), F32),
            pltpu.VMEM((1, d_lru), F32),
            pltpu.VMEM((d_lru // LANES, tm, LANES), F32), pltpu.VMEM((d_lru // LANES, tm, LANES), F32),
            pltpu.VMEM((tm // SUBLANES, d_lru), F32),
        ],
        compiler_params=_params(("arbitrary",)),
        name="in_proj_lru_fwd",
    )(x, nw.reshape(1, d), sc, sh, w_in, conv_w, conv_b.reshape(1, d_lru), wa_bd, ba.reshape(2, 1, d_lru),
      wx_bd, bx.reshape(2, 1, d_lru), lam.reshape(2, 1, d_lru))


def _lru_kernel(x_ref, xp_ref, xn_ref, cw_ref, cb_ref, wa_ref, ba_ref, wx_ref, bx_ref, lam_ref, *rest,
                reverse, n_chunks, rows, accumulate):
    add_ref = rest[0] if accumulate else None
    o_ref, carry_ref, sa_ref, sb_ref, cin_ref = rest[1:] if accumulate else rest
    c = pl.program_id(1)
    chunk = (n_chunks - 1 - c) if reverse else c

    @pl.when(c == 0)
    def _():
        carry_ref[...] = jnp.zeros_like(carry_ref)

    has_prev = jnp.where(chunk > 0, 1.0, 0.0)
    has_next = jnp.where(chunk < n_chunks - 1, 1.0, 0.0)

    def store(rs, h_rows):
        o_ref[0, rs, :] = (add_ref[0, rs, :] + h_rows) if accumulate else h_rows

    _interleave(_lru_steps(x_ref[0], xp_ref[0] * has_prev, xn_ref[0] * has_next, carry_ref[...],
                           cw_ref[...], cb_ref[...], wa_ref[0], ba_ref[0], wx_ref[0], bx_ref[0], lam_ref[0],
                           store, carry_ref, sa_ref, sb_ref, cin_ref, reverse=reverse))


def _lru_steps(x, xp, xn, carry, cw, cb, wa, ba, wx, bx, lam, store, carry_ref, sa_ref, sb_ref, cin_ref, *,
               reverse):
    rows, width = x.shape
    xe = jnp.concatenate([xp, x, xn], axis=0)
    xc = cb
    for k in range(CONV_WIDTH):
        lo = SUBLANES + k - CONV_WIDTH // 2
        xc = xc + cw[k:k + 1] * xe[lo:lo + rows]
    yield

    xcb = xc.astype(BF16)
    r = _sigmoid_tanh(_dot(xcb, wa) + ba)
    yield
    gate_i = _sigmoid_tanh(_dot(xcb, wx) + bx)
    yield
    softplus_neg_lam = jnp.maximum(-lam, 0.0) + jnp.log1p(jnp.exp(-jnp.abs(lam)))
    log_a = (-LRU_C) * r * softplus_neg_lam
    a = jnp.exp(log_a)
    t = jnp.tanh(-log_a)
    u = jnp.sqrt(2.0 * t / (1.0 + t)) * (gate_i * xc)
    yield

    groups = rows // SUBLANES
    acc_a = a.reshape(groups, SUBLANES, width)
    acc_b = u.reshape(groups, SUBLANES, width)
    sub = lax.broadcasted_iota(jnp.int32, (groups, SUBLANES, width), 1)
    s = 1
    while s < SUBLANES:
        if reverse:
            valid = sub < SUBLANES - s
            sh_a, sh_b = pltpu.roll(acc_a, SUBLANES - s, 1), pltpu.roll(acc_b, SUBLANES - s, 1)
        else:
            valid = sub >= s
            sh_a, sh_b = pltpu.roll(acc_a, s, 1), pltpu.roll(acc_b, s, 1)
        acc_b = jnp.where(valid, acc_a * sh_b + acc_b, acc_b)
        acc_a = jnp.where(valid, acc_a * sh_a, acc_a)
        s *= 2
        yield
    acc_a = acc_a.reshape(rows, width)
    acc_b = acc_b.reshape(rows, width)
    edge = 0 if reverse else SUBLANES - 1
    n_tiles = width // LANES
    for j in range(n_tiles):
        sa_ref[j] = acc_a[:, j * LANES:(j + 1) * LANES]
        sb_ref[j] = acc_b[:, j * LANES:(j + 1) * LANES]
    ea = jnp.concatenate([sa_ref[j, pl.ds(edge, groups, stride=SUBLANES), :] for j in range(n_tiles)], axis=1)
    eb = jnp.concatenate([sb_ref[j, pl.ds(edge, groups, stride=SUBLANES), :] for j in range(n_tiles)], axis=1)
    grow = lax.broadcasted_iota(jnp.int32, (groups, width), 0)
    s = 1
    while s < groups:
        if reverse:
            valid = grow < groups - s
            sh_a, sh_b = pltpu.roll(ea, groups - s, 0), pltpu.roll(eb, groups - s, 0)
        else:
            valid = grow >= s
            sh_a, sh_b = pltpu.roll(ea, s, 0), pltpu.roll(eb, s, 0)
        eb = jnp.where(valid, ea * sh_b + eb, eb)
        ea = jnp.where(valid, ea * sh_a, ea)
        s *= 2
    yield
    group_out = eb + ea * carry
    if reverse:
        carry_in = jnp.where(grow == groups - 1, carry, pltpu.roll(group_out, groups - 1, 0))
        carry_ref[...] = group_out[0:1]
    else:
        carry_in = jnp.where(grow == 0, carry, pltpu.roll(group_out, 1, 0))
        carry_ref[...] = group_out[groups - 1:groups]
    cin_ref[...] = carry_in
    for g in range(groups):
        rs = slice(g * SUBLANES, (g + 1) * SUBLANES)
        store(rs, acc_b[rs] + acc_a[rs] * cin_ref[g:g + 1, :])
        if g % (groups // 4) == groups // 4 - 1:
            yield


def _lru_scan(proj, conv_w, conv_b, wa_bd, ba, wx_bd, bx, lam, *, reverse, add_to=None, rows=512):
    bsz, seq, _ = proj.shape
    d_lru = conv_w.shape[1]
    n_chunks = seq // rows
    halo = rows // SUBLANES
    last_halo = seq // SUBLANES - 1
    dirn = 1 if reverse else 0

    def chunk_of(c):
        return (n_chunks - 1 - c) if reverse else c

    vec = lambda: pl.BlockSpec((1, 1, d_lru), lambda b, c: (dirn, 0, 0))
    mat = lambda: pl.BlockSpec((1, d_lru, d_lru), lambda b, c: (dirn, 0, 0))
    accumulate = add_to is not None
    kern = functools.partial(_lru_kernel, reverse=reverse, n_chunks=n_chunks, rows=rows, accumulate=accumulate)
    tile = pl.BlockSpec((1, rows, d_lru), lambda b, c: (b, chunk_of(c), 0))
    return pl.pallas_call(
        kern,
        out_shape=jax.ShapeDtypeStruct((bsz, seq, d_lru), F32),
        grid=(bsz, n_chunks),
        in_specs=[
            pl.BlockSpec((1, rows, d_lru), lambda b, c: (b, chunk_of(c), 0)),
            pl.BlockSpec((1, SUBLANES, d_lru),
                         lambda b, c: (b, jnp.maximum(chunk_of(c) * halo - 1, 0), 0)),
            pl.BlockSpec((1, SUBLANES, d_lru),
                         lambda b, c: (b, jnp.minimum((chunk_of(c) + 1) * halo, last_halo), 0)),
            pl.BlockSpec((CONV_WIDTH, d_lru), lambda b, c: (0, 0)),
            pl.BlockSpec((1, d_lru), lambda b, c: (0, 0)),
            mat(), vec(), mat(), vec(), vec(),
        ] + ([tile] if accumulate else []),
        out_specs=tile,
        scratch_shapes=[pltpu.VMEM((1, d_lru), F32), pltpu.VMEM((d_lru // LANES, rows, LANES), F32),
                        pltpu.VMEM((d_lru // LANES, rows, LANES), F32),
                        pltpu.VMEM((rows // SUBLANES, d_lru), F32)],
        compiler_params=_params(("arbitrary", "arbitrary")),
        name="lru_bwd" if reverse else "lru_fwd",
    )(proj, proj, proj, conv_w, conv_b.reshape(1, d_lru), wa_bd, ba.reshape(2, 1, d_lru),
      wx_bd, bx.reshape(2, 1, d_lru), lam.reshape(2, 1, d_lru), *([add_to] if accumulate else []))


def _hgrn_direction(rev, q_ref, f_ref, v_ref, lb_ref, o_ref, st_ref, diag_s, lvl_s, upd_s, qe_s, btot_s, *, rows):
    ck, sb = HGRN_CHUNK, HGRN_SUB
    n_blk = ck // sb
    sb_shift = sb.bit_length() - 1
    n_sub = rows // ck
    width = q_ref.shape[-1]
    n_pairs = width // LANES
    half = LANES // 2

    def flip(idx, n):
        return (n - 1 - idx) if rev else idx

    n_lvl = n_blk.bit_length() - 1
    tf = flip(lax.broadcasted_iota(jnp.int32, (ck, ck), 0), ck)
    uf = flip(lax.broadcasted_iota(jnp.int32, (ck, ck), 1), ck)
    tb, ub = tf >> sb_shift, uf >> sb_shift
    pb = flip(lax.broadcasted_iota(jnp.int32, (n_blk, ck), 0), n_blk)
    pub = flip(lax.broadcasted_iota(jnp.int32, (n_blk, ck), 1), ck) >> sb_shift
    mats = [jnp.where((tb == ub) & (uf <= tf), 1.0, 0.0),
            jnp.where(pub < pb, 1.0, 0.0)]
    for lvl in range(n_lvl):
        mid = ((pb >> (lvl + 1)) << (lvl + 1)) + (1 << lvl)
        mats.append(jnp.where(pub < mid, 1.0, 0.0))
    mats.append(jnp.ones((SUBLANES, ck), F32))
    m_cum = jnp.concatenate(mats, axis=0).astype(BF16)

    def per_block(rows8):
        return jnp.concatenate(
            [jnp.broadcast_to(rows8[jb:jb + 1], (sb, rows8.shape[1])) for jb in range(n_blk)], axis=0)
    row_blk = flip(lax.broadcasted_iota(jnp.int32, (ck, width), 0), ck) >> sb_shift
    upper = [((row_blk >> lvl) & 1) == 1 for lvl in range(n_lvl)]
    pr = flip(lax.broadcasted_iota(jnp.int32, (ck, LANES), 0), ck) >> sb_shift
    pc = flip(lax.broadcasted_iota(jnp.int32, (ck, LANES), 1) & (ck - 1), ck) >> sb_shift
    group_mask = [(pr >> (lvl + 1)) == (pc >> (lvl + 1)) for lvl in range(n_lvl)]
    lane = lax.broadcasted_iota(jnp.int32, (1, LANES), 1)
    head0 = lane < half

    def split_heads(x):
        xb = x.astype(BF16)
        zero = jnp.zeros_like(xb)
        return jnp.concatenate([jnp.where(head0, xb, zero), jnp.where(head0, zero, xb)], axis=0)

    sr = lax.broadcasted_iota(jnp.int32, (LANES, LANES), 0)
    sc = lax.broadcasted_iota(jnp.int32, (LANES, LANES), 1)
    same_head = (sr < half) == (sc < half)
    er = lax.broadcasted_iota(jnp.int32, (sb * LANES, LANES), 0)
    ec = lax.broadcasted_iota(jnp.int32, (sb * LANES, LANES), 1)
    sel = jnp.where(ec == (((er & (LANES - 1)) >> (half.bit_length() - 1)) * half + (er >> (LANES.bit_length() - 1))),
                    1.0, 0.0).astype(BF16)
    sub_row = flip(lax.broadcasted_iota(jnp.int32, (sb, LANES), 0), sb)
    lbv = lb_ref[...]

    def row_start(j):
        return pl.multiple_of(flip(j, n_sub) * ck, ck)

    def stage1a(j):
        r0 = row_start(j)
        q = q_ref[0, pl.ds(r0, ck), :]
        z = f_ref[0, pl.ds(r0, ck), :]
        v = v_ref[0, pl.ds(r0, ck), :]
        f = lbv + (1.0 - lbv) * _sigmoid(z)
        lf2 = jnp.log(f) * LOG2E
        k = 1.0 - f
        return q, v, k, _dot01_exact(m_cum, lf2)

    def stage1b(q, v, k, cums):
        bl2 = cums[0:ck]
        b2 = bl2 + per_block(cums[ck:ck + n_blk])
        tot_row = ck + (1 + n_lvl) * n_blk
        btot2 = cums[tot_row:tot_row + 1]
        log2_k = jnp.log(k) * LOG2E
        kb = b2 - log2_k
        kbl = bl2 - log2_k
        qe = q * jnp.exp2(b2)
        ke = jnp.exp2(btot2 - kb)
        q_lvl, k_lvl = [], []
        for lvl in range(n_lvl):
            split2 = per_block(cums[ck + (1 + lvl) * n_blk:ck + (2 + lvl) * n_blk])
            q_lvl.append(q * jnp.exp2(jnp.where(upper[lvl], b2 - split2, NEG_BIG)))
            k_lvl.append(jnp.exp2(jnp.where(upper[lvl], NEG_BIG, split2 - kb)))

        qe_s[...] = qe.astype(BF16)
        btot_s[...] = btot2
        for p in range(n_pairs):
            sl = slice(p * LANES, (p + 1) * LANES)
            diag_rows = []
            for jb in range(n_blk):
                rs = slice(jb * sb, (jb + 1) * sb)
                bl_b, kbl_b, q_b = bl2[rs, sl], kbl[rs, sl], q[rs, sl]
                terms = []
                for s in range(sb):
                    arg = jnp.where(sub_row >= flip(s, sb), bl_b - kbl_b[s:s + 1], NEG_BIG)
                    terms.append(q_b * jnp.exp2(arg))
                diag_rows.append(jnp.concatenate(terms, axis=1))
            diag_s[p] = _dot(jnp.concatenate(diag_rows, axis=0).astype(BF16), sel)
            for lvl in range(n_lvl):
                k_p = k_lvl[lvl][:, sl]
                lvl_s[p * n_lvl + lvl] = _dot_nt(q_lvl[lvl][:, sl].astype(BF16), split_heads(k_p))
            upd_s[p] = _dot_tn(v[:, sl].astype(BF16), ke[:, sl].astype(BF16))

    def stage2_issue(j):
        r0 = row_start(j)
        v = v_ref[0, pl.ds(r0, ck), :]
        out = []
        for p in range(n_pairs):
            sl = slice(p * LANES, (p + 1) * LANES)
            parts = []
            for jb in range(n_blk):
                blk = diag_s[p, jb * sb:(jb + 1) * sb, :]
                parts.append(pltpu.roll(blk, jb * sb, 1) if jb else blk)
            scores = jnp.concatenate(parts, axis=0)
            for lvl in range(n_lvl):
                s_lvl = lvl_s[p * n_lvl + lvl]
                scores = scores + (s_lvl if lvl == n_lvl - 1 else jnp.where(group_mask[lvl], s_lvl, 0.0))
            intra = _dot(scores.astype(BF16), split_heads(v[:, sl]))
            st = st_ref[p]
            inter = _dot_nt(qe_s[:, sl], st.astype(BF16))
            new_st = jnp.where(same_head, st * jnp.exp2(btot_s[:, sl]) + upd_s[p], 0.0)
            out.append((inter + intra, new_st))
        return r0, out

    def stage2_finish(r0, out):
        for p in range(n_pairs):
            o_ref[0, pl.ds(r0, ck), p * LANES:(p + 1) * LANES] = out[p][0]
            st_ref[p] = out[p][1]

    return stage1a, stage1b, stage2_issue, stage2_finish


N_HGRN_SCRATCH = 6


def _hgrn_kernel(qf_ref, ff_ref, vf_ref, qb_ref, fb_ref, vb_ref, lb_ref, of_ref, ob_ref, zero_ref, *scratch,
                 rows):
    fwd_scratch, bwd_scratch = scratch[:N_HGRN_SCRATCH], scratch[N_HGRN_SCRATCH:]

    @pl.when(pl.program_id(1) == 0)
    def _():
        fwd_scratch[0][...] = jnp.zeros_like(fwd_scratch[0])
        bwd_scratch[0][...] = jnp.zeros_like(bwd_scratch[0])

    f1a, f1b, f2, f3 = _hgrn_direction(False, qf_ref, ff_ref, vf_ref, lb_ref, of_ref, *fwd_scratch, rows=rows)
    b1a, b1b, b2, b3 = _hgrn_direction(True, qb_ref, fb_ref, vb_ref, lb_ref, ob_ref, *bwd_scratch, rows=rows)
    n_sub = rows // HGRN_CHUNK

    def stage1_both(j):
        fa = f1a(j)
        ba = b1a(j)
        f1b(*fa)
        b1b(*ba)

    zero_part = zero_ref.shape[0] // n_sub

    def store_zeros(j):
        zero_ref[pl.ds(pl.multiple_of(j * zero_part, SUBLANES), zero_part), :] = jnp.zeros(
            (zero_part, zero_ref.shape[1]), zero_ref.dtype)

    stage1_both(0)

    def pipelined(j, carry):
        fo = f2(j)
        bo = b2(j)
        store_zeros(j)
        stage1_both(j + 1)
        f3(*fo)
        b3(*bo)
        return carry

    lax.fori_loop(0, n_sub - 1, pipelined, 0)
    fo = f2(n_sub - 1)
    bo = b2(n_sub - 1)
    store_zeros(n_sub - 1)
    f3(*fo)
    b3(*bo)


def _hgrn(proj, lb, zero_rows, *, d_lru, d_hgrn, rows=512):
    bsz, seq, _ = proj.shape
    n_chunks = seq // rows
    zero_blk = zero_rows // (bsz * n_chunks)
    assert zero_blk * bsz * n_chunks == zero_rows and zero_blk % (SUBLANES * (rows // HGRN_CHUNK)) == 0
    assert 2 * (d_hgrn // HGRN_HEADS) == LANES and HGRN_SUB == SUBLANES and HGRN_CHUNK == HGRN_SUB * SUBLANES
    col0 = (2 * d_lru) // d_hgrn
    n_pairs = d_hgrn // LANES
    n_lvl = (HGRN_CHUNK // HGRN_SUB).bit_length() - 1
    fwd = lambda col: pl.BlockSpec((1, rows, d_hgrn), lambda b, c: (b, c, col))
    bwd = lambda col: pl.BlockSpec((1, rows, d_hgrn), lambda b, c: (b, n_chunks - 1 - c, col))
    direction_scratch = [
        pltpu.VMEM((n_pairs, LANES, LANES), F32),
        pltpu.VMEM((n_pairs, HGRN_CHUNK, LANES), F32),
        pltpu.VMEM((n_pairs * n_lvl, HGRN_CHUNK, LANES), F32),
        pltpu.VMEM((n_pairs, LANES, LANES), F32),
        pltpu.VMEM((HGRN_CHUNK, d_hgrn), BF16),
        pltpu.VMEM((1, d_hgrn), F32),
    ]
    assert len(direction_scratch) == N_HGRN_SCRATCH
    kern = functools.partial(_hgrn_kernel, rows=rows)
    out = jax.ShapeDtypeStruct((bsz, seq, d_hgrn), F32)
    return pl.pallas_call(
        kern,
        out_shape=(out, out, jax.ShapeDtypeStruct((zero_rows, LANES), F32)),
        grid=(bsz, n_chunks),
        in_specs=[fwd(col0), fwd(col0 + 1), fwd(col0 + 3), bwd(col0), bwd(col0 + 2), bwd(col0 + 3),
                  pl.BlockSpec((1, d_hgrn), lambda b, c: (0, 0))],
        out_specs=(pl.BlockSpec((1, rows, d_hgrn), lambda b, c: (b, c, 0)),
                   pl.BlockSpec((1, rows, d_hgrn), lambda b, c: (b, n_chunks - 1 - c, 0)),
                   pl.BlockSpec((zero_blk, LANES), lambda b, c: (b * n_chunks + c, 0))),
        scratch_shapes=direction_scratch + direction_scratch,
        compiler_params=_params(("arbitrary", "arbitrary")),
        name="hgrn2",
    )(proj, proj, proj, proj, proj, proj, lb.reshape(1, d_hgrn))


def _interleave(*streams):
    done = object()
    live = list(streams)
    while live:
        live = [s for s in live if next(s, done) is not done]


def _gelu_tanh(y):
    return 0.5 * y * (1.0 + jnp.tanh(0.7978845608028654 * (y + 0.044715 * (y * y * y))))


def _post_kernel(lru_ref, y_ref, of_ref, ob_ref, g_ref, x_ref, nlw_ref, nhw_ref, wo_ref,
                 gm_ref, nfw_ref, scf_ref, shf_ref, wr_ref, br_ref,
                 xo_ref, h_ref, slab_ref, cnt_ref, route_ref, carry_ref, logits_s, *, tm, d_lru):
    step = pl.program_id(0)

    @pl.when(step == 0)
    def _():
        carry_ref[...] = jnp.zeros_like(carry_ref)
        logits_s[...] = jnp.zeros_like(logits_s)

    def mixer():
        lru = lru_ref[0] * _gelu_tanh(y_ref[0])
        ms = jnp.mean(lru * lru, axis=-1, keepdims=True)
        yield
        lru = lru * lax.rsqrt(ms + NORM_EPS) * nlw_ref[...]

        hg = of_ref[0] + ob_ref[0]
        width = hg.shape[1]
        hd = width // HGRN_HEADS
        hd_shift = hd.bit_length() - 1
        er = lax.broadcasted_iota(jnp.int32, (width, width), 0) >> hd_shift
        ec = lax.broadcasted_iota(jnp.int32, (width, width), 1) >> hd_shift
        head_sum = jnp.where(er == ec, 1.0, 0.0).astype(BF16)
        sq = hg * hg
        sq_hi = sq.astype(BF16)
        sq_lo = (sq - sq_hi.astype(F32)).astype(BF16)
        ms_h = (_dot(sq_hi, head_sum) + _dot(sq_lo, head_sum)) * (1.0 / hd)
        yield
        g = g_ref[0]
        hg = (hg * lax.rsqrt(ms_h + NORM_EPS) * nhw_ref[...]) * (g * _sigmoid(g))
        yield
        mixed = _dot(lru.astype(BF16), wo_ref[0:d_lru, :])
        yield
        mixed = mixed + _dot(hg.astype(BF16), wo_ref[d_lru:, :])
        yield
        x_new = x_ref[0] + gm_ref[0] * mixed
        xo_ref[0] = x_new
        ms_f = jnp.mean(x_new * x_new, axis=-1, keepdims=True)
        yield
        h = (x_new * lax.rsqrt(ms_f + NORM_EPS) * nfw_ref[...]) * (1.0 + scf_ref[0]) + shf_ref[0]
        _tiles_store(h_ref, h, tm, lead=(0,))
        yield
        logits_s[...] = _dot(h.astype(BF16), wr_ref[...]) + br_ref[...]

    def routing():
        yield from _routing_steps(logits_s[...], jnp.where(step > 0, 1.0, 0.0), slab_ref, cnt_ref, route_ref,
                                  carry_ref, tm)

    _interleave(routing(), mixer())


def _routing_steps(logits, live, slab_ref, cnt_ref, route_ref, carry_ref, tm):
    lane = lax.broadcasted_iota(jnp.int32, (tm, ROUTE_LANES), 1)
    lane_f = lane.astype(F32)
    far = float(ROUTE_LANES)
    is_g = lane < N_GROUPS
    gl = jnp.where(is_g, logits, NEG_BIG)
    gmax = jnp.max(gl, axis=-1, keepdims=True)
    yield
    g_idx = jnp.min(jnp.where(gl == gmax, lane_f, far), axis=-1, keepdims=True)
    p_group = 1.0 / jnp.sum(jnp.where(is_g, jnp.exp(gl - gmax), 0.0), axis=-1, keepdims=True)
    yield
    e_lane = lane - N_GROUPS
    in_group = (e_lane >= 0) & (e_lane < N_EXPERTS) & ((e_lane >> (EXPERTS_PER_GROUP.bit_length() - 1)).astype(F32) == g_idx)
    ev = jnp.where(in_group, logits, NEG_BIG)
    top1 = jnp.max(ev, axis=-1, keepdims=True)
    yield
    i1 = jnp.min(jnp.where(in_group & (ev == top1), lane_f, far), axis=-1, keepdims=True)
    yield
    rest = in_group & (lane_f != i1)
    ev2 = jnp.where(rest, logits, NEG_BIG)
    top2 = jnp.max(ev2, axis=-1, keepdims=True)
    yield
    i2 = jnp.min(jnp.where(rest & (ev2 == top2), lane_f, far), axis=-1, keepdims=True)
    yield
    e1 = i1 - float(N_GROUPS)
    e2 = i2 - float(N_GROUPS)
    ex = jnp.exp(top2 - top1)
    w1 = p_group / (1.0 + ex)
    w2 = p_group * ex / (1.0 + ex)

    sel1 = lane_f == e1
    sel2 = lane_f == e2
    onehot = jnp.where(sel1 | sel2, live, 0.0)
    tr = lax.broadcasted_iota(jnp.int32, (tm, tm), 0)
    tc = lax.broadcasted_iota(jnp.int32, (tm, tm), 1)
    before = jnp.where(tc < tr, 1.0, 0.0).astype(BF16)
    cnt = _dot(before, onehot.astype(BF16)) + carry_ref[0:1]
    yield
    rank1 = jnp.sum(jnp.where(sel1, cnt, 0.0), axis=-1, keepdims=True)
    rank2 = jnp.sum(jnp.where(sel2, cnt, 0.0), axis=-1, keepdims=True)
    total = carry_ref[0:1] + jnp.sum(onehot, axis=0, keepdims=True)
    carry_ref[...] = jnp.broadcast_to(total, carry_ref.shape)
    cnt_ref[...] = jnp.broadcast_to(total, cnt_ref.shape)
    yield

    slab = jnp.where(lane == 0, e1, 0.0)
    slab = jnp.where(lane == 1, e2, slab)
    slab = jnp.where(lane == 2, w1, slab)
    slab = jnp.where(lane == 3, w2, slab)
    slab = jnp.where(lane == 4, rank1, slab)
    slab = jnp.where(lane == 5, rank2, slab)
    slab_ref[0] = slab
    route_ref[...] = slab.T[0:SUBLANES]


def _post_mixer(lru_sum, proj, hg_f, hg_b, x, nlw, nhw, wo_bf16, g_mix, nfw, sc_ffn, sh_ffn, wr_bf16, br,
                *, tm=512):
    bsz, seq, d = x.shape
    d_lru = lru_sum.shape[-1]
    d_hgrn = hg_f.shape[-1]
    y_col = 1
    g_col = (2 * d_lru) // d_hgrn + 4
    tiles = seq // tm
    n_tiles = bsz * tiles
    cur = lambda s: jnp.minimum(s, n_tiles - 1)
    prev = lambda s: jnp.maximum(s - 1, 0)
    row = lambda w, col=0: pl.BlockSpec((1, tm, w), lambda s: (cur(s) // tiles, cur(s) % tiles, col))
    vec = lambda w: pl.BlockSpec((1, w), lambda s: (0, 0))
    per_b = lambda: pl.BlockSpec((1, 1, d), lambda s: (cur(s) // tiles, 0, 0))
    kern = functools.partial(_post_kernel, tm=tm, d_lru=d_lru)
    return pl.pallas_call(
        kern,
        out_shape=(
            jax.ShapeDtypeStruct((bsz, seq, d), F32),
            jax.ShapeDtypeStruct((bsz, seq * SUBLANES, LANES), F32),
            jax.ShapeDtypeStruct((bsz, seq, ROUTE_LANES), F32),
            jax.ShapeDtypeStruct((SUBLANES, ROUTE_LANES), F32),
            jax.ShapeDtypeStruct((SUBLANES, bsz * seq), F32),
        ),
        grid=(n_tiles + 1,),
        in_specs=[
            row(d_lru), row(d_lru, y_col),
            row(d_hgrn), row(d_hgrn), row(d_hgrn, g_col),
            row(d), vec(d_lru), vec(d_hgrn),
            pl.BlockSpec((d, d), lambda s: (0, 0)),
            per_b(), vec(d), per_b(), per_b(),
            pl.BlockSpec((d, ROUTE_LANES), lambda s: (0, 0)),
            vec(ROUTE_LANES),
        ],
        out_specs=(
            row(d),
            pl.BlockSpec((1, tm * SUBLANES, LANES), lambda s: (cur(s) // tiles, cur(s) % tiles, 0)),
            pl.BlockSpec((1, tm, ROUTE_LANES), lambda s: (prev(s) // tiles, prev(s) % tiles, 0)),
            pl.BlockSpec((SUBLANES, ROUTE_LANES), lambda s: (0, 0)),
            pl.BlockSpec((SUBLANES, tm), lambda s: (0, prev(s))),
        ),
        scratch_shapes=[pltpu.VMEM((SUBLANES, ROUTE_LANES), F32), pltpu.VMEM((tm, ROUTE_LANES), F32)],
        compiler_params=_params(("arbitrary",)),
        name="post_mixer_router",
    )(lru_sum, proj, hg_f, hg_b, proj, x, nlw.reshape(1, d_lru), nhw.reshape(1, d_hgrn), wo_bf16,
      g_mix, nfw.reshape(1, d), sc_ffn, sh_ffn, wr_bf16, br)


def _tiles_load(ref, n, lead=()):
    return jnp.concatenate(
        [ref[(*lead, pl.ds(j, n, stride=SUBLANES), slice(None))] for j in range(SUBLANES)], axis=1)


def _tiles_store(ref, val, n, lead=()):
    for j in range(SUBLANES):
        ref[(*lead, pl.ds(j, n, stride=SUBLANES), slice(None))] = val[:, j * LANES:(j + 1) * LANES]


def _token_tile(ref, t):
    return ref.at[pl.ds(pl.multiple_of(t * SUBLANES, SUBLANES), SUBLANES)]


def _dest_kernel(start_ref, route_ref, o_ref):
    route = route_ref[...].astype(jnp.int32)
    start = jnp.zeros_like(route)
    for e in range(N_EXPERTS):
        start = jnp.where(route == e, start_ref[e], start)
    o_ref[...] = start + pltpu.roll(route, SUBLANES // 2, 0)


def _dest_rows(expert_start, route):
    return pl.pallas_call(
        _dest_kernel,
        out_shape=jax.ShapeDtypeStruct(route.shape, jnp.int32),
        grid_spec=pltpu.PrefetchScalarGridSpec(
            num_scalar_prefetch=1,
            grid=(1,),
            in_specs=[pl.BlockSpec(route.shape, lambda i, s: (0, 0))],
            out_specs=pl.BlockSpec(route.shape, lambda i, s: (0, 0)),
        ),
        compiler_params=pltpu.CompilerParams(dimension_semantics=("arbitrary",)),
        name="moe_dest_rows",
    )(expert_start, route)


def _dispatch_kernel(d1_ref, d2_ref, h_ref, z_ref, o_ref, sem, *, tb):
    del z_ref
    base = pl.program_id(0) * tb

    def issue(r, carry):
        t = base + r
        pltpu.make_async_copy(_token_tile(h_ref, r), _token_tile(o_ref, d1_ref[t]), sem).start(priority=0)
        pltpu.make_async_copy(_token_tile(h_ref, r), _token_tile(o_ref, d2_ref[t]), sem).start(priority=1)
        return carry

    lax.fori_loop(0, tb, issue, 0, unroll=DMA_ISSUE_UNROLL)
    for _ in range(2):
        pltpu.make_async_copy(h_ref, o_ref.at[pl.ds(0, tb * SUBLANES)], sem).wait()


def _dispatch(dest1, dest2, h_tiles, zero_tiles, *, tb=512):
    m = h_tiles.shape[0] // SUBLANES
    n_rows = zero_tiles.shape[0] // SUBLANES
    kern = functools.partial(_dispatch_kernel, tb=tb)
    return pl.pallas_call(
        kern,
        out_shape=jax.ShapeDtypeStruct((n_rows * SUBLANES, LANES), h_tiles.dtype),
        grid_spec=pltpu.PrefetchScalarGridSpec(
            num_scalar_prefetch=2,
            grid=(m // tb,),
            in_specs=[pl.BlockSpec((tb * SUBLANES, LANES), lambda i, d1, d2: (i, 0)),
                      pl.BlockSpec(memory_space=pl.ANY)],
            out_specs=pl.BlockSpec(memory_space=pl.ANY),
            scratch_shapes=[pltpu.SemaphoreType.DMA(())],
        ),
        input_output_aliases={3: 0},
        compiler_params=pltpu.CompilerParams(dimension_semantics=("arbitrary",), has_side_effects=True),
        name="moe_dispatch",
    )(dest1, dest2, h_tiles, zero_tiles)


def _expert_kernel(plan_ref, x_hbm, wg_hbm, wu_hbm, wd_hbm, o_ref, wg_f, wu_f, wd_f, wg_s, wu_s, wd_s, sem,
                   x_ring, x_sem, *, layer, blk):
    i = pl.program_id(0)
    n_blocks = pl.num_programs(0)
    n_used = plan_ref[n_blocks]
    expert = plan_ref[i]
    next_expert = plan_ref[n_blocks + 1 + i]
    slot = plan_ref[2 * n_blocks + 1 + i]
    first_block = ((i == 0) | (plan_ref[jnp.maximum(i - 1, 0)] != expert)) & (i < n_used)

    def copies(e, s):
        return (pltpu.make_async_copy(wg_hbm.at[layer, e], wg_f.at[s], sem.at[s]),
                pltpu.make_async_copy(wu_hbm.at[layer, e], wu_f.at[s], sem.at[s]),
                pltpu.make_async_copy(wd_hbm.at[layer, e], wd_f.at[s], sem.at[s]))

    @pl.when(i == 0)
    def _():
        for c in copies(expert, slot):
            c.start()

    @pl.when(first_block)
    def _():
        for c in copies(expert, slot):
            c.wait()
        wg_s[...] = wg_f[slot].astype(BF16)
        wu_s[...] = wu_f[slot].astype(BF16)
        wd_s[...] = wd_f[slot].astype(BF16)

        @pl.when(next_expert >= 0)
        def _():
            for c in copies(next_expert, 1 - slot):
                c.start()

    def x_copy(block, ring_slot):
        rows = blk * SUBLANES
        return pltpu.make_async_copy(x_hbm.at[pl.ds(pl.multiple_of(block * rows, rows), rows)],
                                     x_ring.at[ring_slot], x_sem.at[ring_slot])

    @pl.when(i == 0)
    def _():
        for b in range(X_RING - 1):
            @pl.when(b < n_used)
            def _():
                x_copy(b, b).start()

    ahead = i + X_RING - 1

    @pl.when(ahead < n_used)
    def _():
        x_copy(ahead, ahead % X_RING).start()

    for ring_slot in range(X_RING):
        @pl.when((i < n_used) & (i % X_RING == ring_slot))
        def _():
            x_copy(i, ring_slot).wait()
            x = _tiles_load(x_ring, blk, lead=(ring_slot,)).astype(BF16)
            gate = _dot(x, wg_s[...])
            up = _dot(x, wu_s[...])
            act = (gate * _sigmoid(gate)) * up
            _tiles_store(o_ref, _dot(act.astype(BF16), wd_s[...]), blk)

    @pl.when(i >= n_used)
    def _():
        o_ref[...] = jnp.zeros_like(o_ref)


def _expert_plan(blk_expert, blocks_used):
    n_blocks = blk_expert.shape[0]
    idx = jnp.arange(n_blocks, dtype=jnp.int32)
    change = jnp.concatenate([jnp.ones((1,), bool), blk_expert[1:] != blk_expert[:-1]])
    slot = (jnp.cumsum(change.astype(jnp.int32)) - 1) & 1
    change_at = jnp.where(change, idx, n_blocks)
    from_here = lax.cummin(change_at[::-1])[::-1]
    next_change = jnp.concatenate([from_here[1:], jnp.full((1,), n_blocks, jnp.int32)])
    next_expert = jnp.where(next_change < blocks_used, blk_expert[jnp.minimum(next_change, n_blocks - 1)], -1)
    return jnp.concatenate([blk_expert, blocks_used.reshape(1), next_expert, slot]).astype(jnp.int32)


def _experts(plan, x_tiles, wg, wu, wd, layer):
    n_rows = x_tiles.shape[0] // SUBLANES
    d, de = wg.shape[-2:]
    blk = EXPERT_BLOCK
    n_blocks = n_rows // blk
    kern = functools.partial(_expert_kernel, layer=layer, blk=blk)
    return pl.pallas_call(
        kern,
        out_shape=jax.ShapeDtypeStruct((n_rows * SUBLANES, LANES), F32),
        grid_spec=pltpu.PrefetchScalarGridSpec(
            num_scalar_prefetch=1,
            grid=(n_blocks,),
            in_specs=[
                pl.BlockSpec(memory_space=pl.ANY),
                pl.BlockSpec(memory_space=pl.ANY),
                pl.BlockSpec(memory_space=pl.ANY),
                pl.BlockSpec(memory_space=pl.ANY),
            ],
            out_specs=pl.BlockSpec((blk * SUBLANES, LANES), lambda i, plan: (i, 0)),
            scratch_shapes=[
                pltpu.VMEM((2, d, de), F32), pltpu.VMEM((2, d, de), F32), pltpu.VMEM((2, de, d), F32),
                pltpu.VMEM((d, de), BF16), pltpu.VMEM((d, de), BF16), pltpu.VMEM((de, d), BF16),
                pltpu.SemaphoreType.DMA((2,)),
                pltpu.VMEM((X_RING, blk * SUBLANES, LANES), F32), pltpu.SemaphoreType.DMA((X_RING,)),
            ],
        ),
        compiler_params=_params(("arbitrary",)),
        name="moe_experts",
    )(plan, x_tiles, wg, wu, wd)


def _combine_kernel(d1_ref, d2_ref, y_ref, slab_ref, x_ref, g_ref, nw_ref, o_ref, ra0, rb0, ra1, rb1, sem,
                    *, tm, tiles, n_steps, final_norm):
    step = pl.program_id(0) * tiles + pl.program_id(1)
    bufs = ((ra0, rb0), (ra1, rb1))

    def gather(tile, slot):
        base = tile * tm
        r1_ref, r2_ref = bufs[slot]

        def issue(r, carry):
            t = base + r
            pltpu.make_async_copy(_token_tile(y_ref, d1_ref[t]), _token_tile(r1_ref, r),
                                  sem.at[slot]).start(priority=0)
            pltpu.make_async_copy(_token_tile(y_ref, d2_ref[t]), _token_tile(r2_ref, r),
                                  sem.at[slot]).start(priority=1)
            return carry

        lax.fori_loop(0, tm, issue, 0, unroll=DMA_ISSUE_UNROLL)

    @pl.when(step == 0)
    def _():
        gather(0, 0)

    for slot in range(2):
        @pl.when((step & 1) == slot)
        def _():
            @pl.when(step + 1 < n_steps)
            def _():
                gather(step + 1, 1 - slot)

            r1_ref, r2_ref = bufs[slot]
            pltpu.make_async_copy(y_ref.at[pl.ds(0, tm * SUBLANES)], r1_ref, sem.at[slot]).wait()
            pltpu.make_async_copy(y_ref.at[pl.ds(0, tm * SUBLANES)], r2_ref, sem.at[slot]).wait()
            slab = slab_ref[0]
            y = slab[:, 2:3] * _tiles_load(r1_ref, tm) + slab[:, 3:4] * _tiles_load(r2_ref, tm)
            out = x_ref[0] + g_ref[0] * y
            if final_norm:
                ms = jnp.mean(out * out, axis=-1, keepdims=True)
                out = out * lax.rsqrt(ms + NORM_EPS) * nw_ref[...]
            o_ref[0] = out


def _combine(dest1, dest2, y_buf, slab, x, g_ffn, norm_w, *, final_norm, tm=512):
    bsz, seq, d = x.shape
    tiles = seq // tm
    kern = functools.partial(_combine_kernel, tm=tm, tiles=tiles, n_steps=bsz * tiles, final_norm=final_norm)
    row_buf = pltpu.VMEM((tm * SUBLANES, LANES), F32)
    return pl.pallas_call(
        kern,
        out_shape=jax.ShapeDtypeStruct((bsz, seq, d), F32),
        grid_spec=pltpu.PrefetchScalarGridSpec(
            num_scalar_prefetch=2,
            grid=(bsz, tiles),
            in_specs=[
                pl.BlockSpec(memory_space=pl.ANY),
                pl.BlockSpec((1, tm, ROUTE_LANES), lambda b, i, d1, d2: (b, i, 0)),
                pl.BlockSpec((1, tm, d), lambda b, i, d1, d2: (b, i, 0)),
                pl.BlockSpec((1, 1, d), lambda b, i, d1, d2: (b, 0, 0)),
                pl.BlockSpec((1, d), lambda b, i, d1, d2: (0, 0)),
            ],
            out_specs=pl.BlockSpec((1, tm, d), lambda b, i, d1, d2: (b, i, 0)),
            scratch_shapes=[row_buf, row_buf, row_buf, row_buf, pltpu.SemaphoreType.DMA((2,))],
        ),
        compiler_params=_params(("arbitrary", "arbitrary")),
        name="moe_combine",
    )(dest1, dest2, y_buf, slab, x, g_ffn, norm_w.reshape(1, d))


def _block_diag(w):
    heads, hd, _ = w.shape
    n = heads * hd
    tiled = jnp.tile(w.reshape(n, hd), (1, heads))
    blk_r = lax.broadcasted_iota(jnp.int32, (n, n), 0) // hd
    blk_c = lax.broadcasted_iota(jnp.int32, (n, n), 1) // hd
    return jnp.where(blk_r == blk_c, tiled, 0.0)


def kernel(x, c, ada_w, ada_b, norm_mix_w, w_in, conv_w, conv_b, lru_wa, lru_ba, lru_wx, lru_bx, lru_lambda, norm_lru_w, hgrn_lb, norm_hgrn_w, w_out, norm_ffn_w, router_group_w, router_group_b, router_expert_w, router_expert_b, expert_w_gate, expert_w_up, expert_w_down, final_norm_w):
    bsz, seq, d = x.shape
    assert d == SUBLANES * LANES, "the MoE row movement keeps one (8, 128) tile per token"
    depth = ada_w.shape[0]
    d_lru = conv_w.shape[-1]
    d_hgrn = hgrn_lb.shape[-1]
    m = bsz * seq
    n_rows = m * 2 + N_EXPERTS * EXPERT_BLOCK
    n_blocks = n_rows // EXPERT_BLOCK

    mod = _modulation(c, ada_w, ada_b)
    lb_cum = jnp.cumsum(jax.nn.softmax(hgrn_lb.astype(F32), axis=0), axis=0)
    lb_all = lb_cum - lb_cum[0:1]

    for l in range(depth):
        sh_mix, sc_mix, g_mix, sh_ffn, sc_ffn, g_ffn = [
            mod[l, :, i * d:(i + 1) * d].reshape(bsz, 1, d) for i in range(N_MODULATIONS)]
        wa_bd = jnp.stack([_block_diag(lru_wa[l, 0]), _block_diag(lru_wa[l, 1])]).astype(BF16)
        wx_bd = jnp.stack([_block_diag(lru_wx[l, 0]), _block_diag(lru_wx[l, 1])]).astype(BF16)
        lru_w = (conv_w[l], conv_b[l], wa_bd, lru_ba[l], wx_bd, lru_bx[l], lru_lambda[l])
        proj, lru_fwd = _in_proj_lru(x, norm_mix_w[l], sc_mix, sh_mix, w_in, l, *lru_w)
        lru_sum = _lru_scan(proj, *lru_w, reverse=True, add_to=lru_fwd)
        hg_f, hg_b, zero_tiles = _hgrn(proj, lb_all[l], n_rows * SUBLANES, d_lru=d_lru, d_hgrn=d_hgrn)

        lane_pad = ROUTE_LANES - N_GROUPS - N_EXPERTS
        wr = jnp.pad(jnp.concatenate([router_group_w[l], router_expert_w[l]], axis=1), ((0, 0), (0, lane_pad)))
        br = jnp.pad(jnp.concatenate([router_group_b[l], router_expert_b[l]]), (0, lane_pad)).reshape(1, ROUTE_LANES)
        x_mid, h_ffn, slab, counts, route = _post_mixer(
            lru_sum, proj, hg_f, hg_b, x, norm_lru_w[l], norm_hgrn_w[l], w_out[l].astype(BF16), g_mix,
            norm_ffn_w[l], sc_ffn, sh_ffn, wr.astype(BF16), br)

        cnt = counts[0, :N_EXPERTS].astype(jnp.int32)
        padded = ((cnt + EXPERT_BLOCK - 1) // EXPERT_BLOCK) * EXPERT_BLOCK
        pend = jnp.cumsum(padded)
        pstart = pend - padded
        blk_start = jnp.arange(n_blocks, dtype=jnp.int32) * EXPERT_BLOCK
        blk_expert = jnp.minimum(jnp.sum(pend[None, :] <= blk_start[:, None], axis=1), N_EXPERTS - 1)
        plan = _expert_plan(blk_expert.astype(jnp.int32), (pend[N_EXPERTS - 1] // EXPERT_BLOCK).astype(jnp.int32))
        dest = _dest_rows(pstart.astype(jnp.int32), route)
        dest1, dest2 = dest[0], dest[1]

        x_buf = _dispatch(dest1, dest2, h_ffn.reshape(m * SUBLANES, LANES), zero_tiles)
        y_buf = _experts(plan, x_buf, expert_w_gate, expert_w_up, expert_w_down, l)
        x = _combine(dest1, dest2, y_buf, slab, x_mid, g_ffn, final_norm_w, final_norm=(l == depth - 1))

    return x
```

```python
import functools

import jax
import jax.numpy as jnp
from jax import lax
from jax.experimental import pallas as pl
from jax.experimental.pallas import tpu as pltpu

F32 = jnp.float32
BF16 = jnp.bfloat16

HGRN_HEADS = 8
N_MODULATIONS = 6
CONV_WIDTH = 4
LRU_C = 8.0
N_GROUPS = 4
EXPERTS_PER_GROUP = 8
N_EXPERTS = N_GROUPS * EXPERTS_PER_GROUP
NORM_EPS = 1e-6

LANES = 128
SUBLANES = 8
VMEM_LIMIT = 56 * 1024 * 1024

HGRN_CHUNK = 64
HGRN_SUB = 8
LOG2E = 1.4426950408889634
ROUTE_LANES = LANES
EXPERT_BLOCK = 512
X_RING = 3
IN_RING = 3
DMA_ISSUE_UNROLL = 8
NEG_BIG = -3.0e38


def _params(sem):
    return pltpu.CompilerParams(dimension_semantics=sem, vmem_limit_bytes=VMEM_LIMIT)


def _dot(a, b):
    return jnp.dot(a, b, preferred_element_type=F32)


def _dot_nt(a, b):
    return lax.dot_general(a, b, (((1,), (1,)), ((), ())), preferred_element_type=F32)


def _dot_tn(a, b):
    return lax.dot_general(a, b, (((0,), (0,)), ((), ())), preferred_element_type=F32)


def _dot01_exact(m01, x):
    hi = x.astype(BF16)
    r1 = x - hi.astype(F32)
    mid = r1.astype(BF16)
    lo = (r1 - mid.astype(F32)).astype(BF16)
    return _dot(m01, hi) + _dot(m01, mid) + _dot(m01, lo)


def _sigmoid(x):
    return 1.0 / (1.0 + jnp.exp(-x))


def _sigmoid_tanh(x):
    return 0.5 * jnp.tanh(0.5 * x) + 0.5


def _mod_kernel(c_ref, w_ref, b_ref, o_ref):
    c = c_ref[...]
    cond = c * _sigmoid(c)
    o_ref[0] = _dot(cond.astype(BF16), w_ref[0].astype(BF16)) + b_ref[0]


def _modulation(c, ada_w, ada_b):
    depth, d, n = ada_w.shape
    bsz = c.shape[0]
    rows = -(-bsz // SUBLANES) * SUBLANES
    c_pad = jnp.pad(c, ((0, rows - bsz), (0, 0)))
    tn = n // N_MODULATIONS
    out = pl.pallas_call(
        _mod_kernel,
        out_shape=jax.ShapeDtypeStruct((depth, rows, n), F32),
        grid=(depth, n // tn),
        in_specs=[
            pl.BlockSpec((rows, d), lambda l, j: (0, 0)),
            pl.BlockSpec((1, d, tn), lambda l, j: (l, 0, j)),
            pl.BlockSpec((1, 1, tn), lambda l, j: (l, 0, j)),
        ],
        out_specs=pl.BlockSpec((1, rows, tn), lambda l, j: (l, 0, j)),
        compiler_params=_params(("arbitrary", "arbitrary")),
        name="adaln_mod",
    )(c_pad, ada_w, ada_b.reshape(depth, 1, n))
    return out[:, :bsz]


def _rms_mod(x, nw, sc, sh):
    ms = jnp.mean(x * x, axis=-1, keepdims=True)
    return (x * lax.rsqrt(ms + NORM_EPS) * nw) * (1.0 + sc) + sh


def _inproj_lru_kernel(x_ref, nw_ref, sc_ref, sh_ref, w_ref, cw_ref, cb_ref, wa_ref, ba_ref, wx_ref, bx_ref,
                       lam_ref, proj_ref, lru_ref, w_s, xl_s, xnew_s, carry_ref, sa_ref, sb_ref, cin_ref,
                       *, tm, tiles, d_lru):
    s = pl.program_id(0)
    n_cols = proj_ref.shape[-1]

    @pl.when(s == 0)
    def _():
        w_s[...] = w_ref[0].astype(BF16)
        xl_s[...] = jnp.zeros_like(xl_s)
        carry_ref[...] = jnp.zeros_like(carry_ref)

    prev = jnp.maximum(s - 1, 0)
    chunk = prev % tiles
    has_prev = jnp.where(chunk > 0, 1.0, 0.0)
    has_next = jnp.where(chunk < tiles - 1, 1.0, 0.0)
    hb = _rms_mod(x_ref[0], nw_ref[...], sc_ref[0], sh_ref[0]).astype(BF16)

    def project():
        step = 2 * LANES
        for c0 in range(0, n_cols, step):
            block = _dot(hb, w_s[:, c0:c0 + step])
            proj_ref[0, :, c0:c0 + step] = block
            if c0 < d_lru:
                xnew_s[:, c0:c0 + step] = block
            if c0 + step >= d_lru:
                yield

    def scan():
        yield

        def store(rs, h_rows):
            lru_ref[0, rs, :] = h_rows

        yield from _lru_steps(xl_s[SUBLANES:, :], xl_s[0:SUBLANES, :] * has_prev,
                              xnew_s[0:SUBLANES, :] * has_next, carry_ref[...] * has_prev,
                              cw_ref[...], cb_ref[...], wa_ref[0], ba_ref[0], wx_ref[0], bx_ref[0], lam_ref[0],
                              store, carry_ref, sa_ref, sb_ref, cin_ref, reverse=False)

    _interleave(project(), scan())
    xl_s[0:SUBLANES, :] = xl_s[tm:tm + SUBLANES, :]
    xl_s[SUBLANES:, :] = xnew_s[...]


def _in_proj_lru(x, nw, sc, sh, w_in, layer, conv_w, conv_b, wa_bd, ba, wx_bd, bx, lam, tm=512):
    bsz, seq, d = x.shape
    n = w_in.shape[-1]
    d_lru = conv_w.shape[1]
    tiles = seq // tm
    n_tiles = bsz * tiles
    cur = lambda s: jnp.minimum(s, n_tiles - 1)
    prev = lambda s: jnp.maximum(s - 1, 0)
    vec = lambda: pl.BlockSpec((1, 1, d_lru), lambda s: (0, 0, 0))
    mat = lambda: pl.BlockSpec((1, d_lru, d_lru), lambda s: (0, 0, 0))
    kern = functools.partial(_inproj_lru_kernel, tm=tm, tiles=tiles, d_lru=d_lru)
    return pl.pallas_call(
        kern,
        out_shape=(jax.ShapeDtypeStruct((bsz, seq, n), F32), jax.ShapeDtypeStruct((bsz, seq, d_lru), F32)),
        grid=(n_tiles + 1,),
        in_specs=[
            pl.BlockSpec((1, tm, d), lambda s: (cur(s) // tiles, cur(s) % tiles, 0)),
            pl.BlockSpec((1, d), lambda s: (0, 0)),
            pl.BlockSpec((1, 1, d), lambda s: (cur(s) // tiles, 0, 0)),
            pl.BlockSpec((1, 1, d), lambda s: (cur(s) // tiles, 0, 0)),
            pl.BlockSpec((1, d, n), lambda s: (layer, 0, 0), pipeline_mode=pl.Buffered(1)),
            pl.BlockSpec((CONV_WIDTH, d_lru), lambda s: (0, 0)),
            pl.BlockSpec((1, d_lru), lambda s: (0, 0)),
            mat(), vec(), mat(), vec(), vec(),
        ],
        out_specs=(pl.BlockSpec((1, tm, n), lambda s: (cur(s) // tiles, cur(s) % tiles, 0)),
                   pl.BlockSpec((1, tm, d_lru), lambda s: (prev(s) // tiles, prev(s) % tiles, 0))),
        scratch_shapes=[
            pltpu.VMEM((d, n), BF16),
            pltpu.VMEM((tm + SUBLANES, d_lru), F32),
            pltpu.VMEM((tm, d_lru), F32),
            pltpu.VMEM((1, d_lru), F32),
            pltpu.VMEM((d_lru // LANES, tm, LANES), F32), pltpu.VMEM((d_lru // LANES, tm, LANES), F32),
            pltpu.VMEM((tm // SUBLANES, d_lru), F32),
        ],
        compiler_params=_params(("arbitrary",)),
        name="in_proj_lru_fwd",
    )(x, nw.reshape(1, d), sc, sh, w_in, conv_w, conv_b.reshape(1, d_lru), wa_bd, ba.reshape(2, 1, d_lru),
      wx_bd, bx.reshape(2, 1, d_lru), lam.reshape(2, 1, d_lru))


def _lru_kernel(x_ref, xp_ref, xn_ref, cw_ref, cb_ref, wa_ref, ba_ref, wx_ref, bx_ref, lam_ref, *rest,
                reverse, n_chunks, rows, accumulate):
    add_ref = rest[0] if accumulate else None
    o_ref, carry_ref, sa_ref, sb_ref, cin_ref = rest[1:] if accumulate else rest
    c = pl.program_id(1)
    chunk = (n_chunks - 1 - c) if reverse else c

    @pl.when(c == 0)
    def _():
        carry_ref[...] = jnp.zeros_like(carry_ref)

    has_prev = jnp.where(chunk > 0, 1.0, 0.0)
    has_next = jnp.where(chunk < n_chunks - 1, 1.0, 0.0)

    def store(rs, h_rows):
        o_ref[0, rs, :] = (add_ref[0, rs, :] + h_rows) if accumulate else h_rows

    _interleave(_lru_steps(x_ref[0], xp_ref[0] * has_prev, xn_ref[0] * has_next, carry_ref[...],
                           cw_ref[...], cb_ref[...], wa_ref[0], ba_ref[0], wx_ref[0], bx_ref[0], lam_ref[0],
                           store, carry_ref, sa_ref, sb_ref, cin_ref, reverse=reverse))


def _lru_steps(x, xp, xn, carry, cw, cb, wa, ba, wx, bx, lam, store, carry_ref, sa_ref, sb_ref, cin_ref, *,
               reverse):
    rows, width = x.shape
    xe = jnp.concatenate([xp, x, xn], axis=0)
    xc = cb
    for k in range(CONV_WIDTH):
        lo = SUBLANES + k - CONV_WIDTH // 2
        xc = xc + cw[k:k + 1] * xe[lo:lo + rows]
    yield

    xcb = xc.astype(BF16)
    r = _sigmoid_tanh(_dot(xcb, wa) + ba)
    yield
    gate_i = _sigmoid_tanh(_dot(xcb, wx) + bx)
    yield
    softplus_neg_lam = jnp.maximum(-lam, 0.0) + jnp.log1p(jnp.exp(-jnp.abs(lam)))
    log_a = (-LRU_C) * r * softplus_neg_lam
    a = jnp.exp(log_a)
    t = jnp.tanh(-log_a)
    u = jnp.sqrt(2.0 * t / (1.0 + t)) * (gate_i * xc)
    yield

    groups = rows // SUBLANES
    acc_a = a.reshape(groups, SUBLANES, width)
    acc_b = u.reshape(groups, SUBLANES, width)
    sub = lax.broadcasted_iota(jnp.int32, (groups, SUBLANES, width), 1)
    s = 1
    while s < SUBLANES:
        if reverse:
            valid = sub < SUBLANES - s
            sh_a, sh_b = pltpu.roll(acc_a, SUBLANES - s, 1), pltpu.roll(acc_b, SUBLANES - s, 1)
        else:
            valid = sub >= s
            sh_a, sh_b = pltpu.roll(acc_a, s, 1), pltpu.roll(acc_b, s, 1)
        acc_b = jnp.where(valid, acc_a * sh_b + acc_b, acc_b)
        acc_a = jnp.where(valid, acc_a * sh_a, acc_a)
        s *= 2
        yield
    acc_a = acc_a.reshape(rows, width)
    acc_b = acc_b.reshape(rows, width)
    edge = 0 if reverse else SUBLANES - 1
    n_tiles = width // LANES
    for j in range(n_tiles):
        sa_ref[j] = acc_a[:, j * LANES:(j + 1) * LANES]
        sb_ref[j] = acc_b[:, j * LANES:(j + 1) * LANES]
    ea = jnp.concatenate([sa_ref[j, pl.ds(edge, groups, stride=SUBLANES), :] for j in range(n_tiles)], axis=1)
    eb = jnp.concatenate([sb_ref[j, pl.ds(edge, groups, stride=SUBLANES), :] for j in range(n_tiles)], axis=1)
    grow = lax.broadcasted_iota(jnp.int32, (groups, width), 0)
    s = 1
    while s < groups:
        if reverse:
            valid = grow < groups - s
            sh_a, sh_b = pltpu.roll(ea, groups - s, 0), pltpu.roll(eb, groups - s, 0)
        else:
            valid = grow >= s
            sh_a, sh_b = pltpu.roll(ea, s, 0), pltpu.roll(eb, s, 0)
        eb = jnp.where(valid, ea * sh_b + eb, eb)
        ea = jnp.where(valid, ea * sh_a, ea)
        s *= 2
    yield
    group_out = eb + ea * carry
    if reverse:
        carry_in = jnp.where(grow == groups - 1, carry, pltpu.roll(group_out, groups - 1, 0))
        carry_ref[...] = group_out[0:1]
    else:
        carry_in = jnp.where(grow == 0, carry, pltpu.roll(group_out, 1, 0))
        carry_ref[...] = group_out[groups - 1:groups]
    cin_ref[...] = carry_in
    for g in range(groups):
        rs = slice(g * SUBLANES, (g + 1) * SUBLANES)
        store(rs, acc_b[rs] + acc_a[rs] * cin_ref[g:g + 1, :])
        if g % (groups // 4) == groups // 4 - 1:
            yield


def _lru_scan(proj, conv_w, conv_b, wa_bd, ba, wx_bd, bx, lam, *, reverse, add_to=None, rows=512):
    bsz, seq, _ = proj.shape
    d_lru = conv_w.shape[1]
    n_chunks = seq // rows
    halo = rows // SUBLANES
    last_halo = seq // SUBLANES - 1
    dirn = 1 if reverse else 0

    def chunk_of(c):
        return (n_chunks - 1 - c) if reverse else c

    vec = lambda: pl.BlockSpec((1, 1, d_lru), lambda b, c: (dirn, 0, 0))
    mat = lambda: pl.BlockSpec((1, d_lru, d_lru), lambda b, c: (dirn, 0, 0))
    accumulate = add_to is not None
    kern = functools.partial(_lru_kernel, reverse=reverse, n_chunks=n_chunks, rows=rows, accumulate=accumulate)
    tile = pl.BlockSpec((1, rows, d_lru), lambda b, c: (b, chunk_of(c), 0))
    return pl.pallas_call(
        kern,
        out_shape=jax.ShapeDtypeStruct((bsz, seq, d_lru), F32),
        grid=(bsz, n_chunks),
        in_specs=[
            pl.BlockSpec((1, rows, d_lru), lambda b, c: (b, chunk_of(c), 0)),
            pl.BlockSpec((1, SUBLANES, d_lru),
                         lambda b, c: (b, jnp.maximum(chunk_of(c) * halo - 1, 0), 0)),
            pl.BlockSpec((1, SUBLANES, d_lru),
                         lambda b, c: (b, jnp.minimum((chunk_of(c) + 1) * halo, last_halo), 0)),
            pl.BlockSpec((CONV_WIDTH, d_lru), lambda b, c: (0, 0)),
            pl.BlockSpec((1, d_lru), lambda b, c: (0, 0)),
            mat(), vec(), mat(), vec(), vec(),
        ] + ([tile] if accumulate else []),
        out_specs=tile,
        scratch_shapes=[pltpu.VMEM((1, d_lru), F32), pltpu.VMEM((d_lru // LANES, rows, LANES), F32),
                        pltpu.VMEM((d_lru // LANES, rows, LANES), F32),
                        pltpu.VMEM((rows // SUBLANES, d_lru), F32)],
        compiler_params=_params(("arbitrary", "arbitrary")),
        name="lru_bwd" if reverse else "lru_fwd",
    )(proj, proj, proj, conv_w, conv_b.reshape(1, d_lru), wa_bd, ba.reshape(2, 1, d_lru),
      wx_bd, bx.reshape(2, 1, d_lru), lam.reshape(2, 1, d_lru), *([add_to] if accumulate else []))


def _hgrn_direction(rev, q_ref, f_ref, v_ref, lb_ref, o_ref, st_ref, diag_s, lvl_s, upd_s, qe_s, btot_s, *, rows):
    ck, sb = HGRN_CHUNK, HGRN_SUB
    n_blk = ck // sb
    sb_shift = sb.bit_length() - 1
    n_sub = rows // ck
    width = q_ref.shape[-1]
    n_pairs = width // LANES
    half = LANES // 2

    def flip(idx, n):
        return (n - 1 - idx) if rev else idx

    n_lvl = n_blk.bit_length() - 1
    tf = flip(lax.broadcasted_iota(jnp.int32, (ck, ck), 0), ck)
    uf = flip(lax.broadcasted_iota(jnp.int32, (ck, ck), 1), ck)
    tb, ub = tf >> sb_shift, uf >> sb_shift
    pb = flip(lax.broadcasted_iota(jnp.int32, (n_blk, ck), 0), n_blk)
    pub = flip(lax.broadcasted_iota(jnp.int32, (n_blk, ck), 1), ck) >> sb_shift
    mats = [jnp.where((tb == ub) & (uf <= tf), 1.0, 0.0),
            jnp.where(pub < pb, 1.0, 0.0)]
    for lvl in range(n_lvl):
        mid = ((pb >> (lvl + 1)) << (lvl + 1)) + (1 << lvl)
        mats.append(jnp.where(pub < mid, 1.0, 0.0))
    mats.append(jnp.ones((SUBLANES, ck), F32))
    m_cum = jnp.concatenate(mats, axis=0).astype(BF16)

    def per_block(rows8):
        return jnp.concatenate(
            [jnp.broadcast_to(rows8[jb:jb + 1], (sb, rows8.shape[1])) for jb in range(n_blk)], axis=0)
    row_blk = flip(lax.broadcasted_iota(jnp.int32, (ck, width), 0), ck) >> sb_shift
    upper = [((row_blk >> lvl) & 1) == 1 for lvl in range(n_lvl)]
    pr = flip(lax.broadcasted_iota(jnp.int32, (ck, LANES), 0), ck) >> sb_shift
    pc = flip(lax.broadcasted_iota(jnp.int32, (ck, LANES), 1) & (ck - 1), ck) >> sb_shift
    group_mask = [(pr >> (lvl + 1)) == (pc >> (lvl + 1)) for lvl in range(n_lvl)]
    lane = lax.broadcasted_iota(jnp.int32, (1, LANES), 1)
    head0 = lane < half

    def split_heads(x):
        xb = x.astype(BF16)
        zero = jnp.zeros_like(xb)
        return jnp.concatenate([jnp.where(head0, xb, zero), jnp.where(head0, zero, xb)], axis=0)

    sr = lax.broadcasted_iota(jnp.int32, (LANES, LANES), 0)
    sc = lax.broadcasted_iota(jnp.int32, (LANES, LANES), 1)
    same_head = (sr < half) == (sc < half)
    er = lax.broadcasted_iota(jnp.int32, (sb * LANES, LANES), 0)
    ec = lax.broadcasted_iota(jnp.int32, (sb * LANES, LANES), 1)
    sel = jnp.where(ec == (((er & (LANES - 1)) >> (half.bit_length() - 1)) * half + (er >> (LANES.bit_length() - 1))),
                    1.0, 0.0).astype(BF16)
    sub_row = flip(lax.broadcasted_iota(jnp.int32, (sb, LANES), 0), sb)
    lbv = lb_ref[...]

    def row_start(j):
        return pl.multiple_of(flip(j, n_sub) * ck, ck)

    def stage1a(j):
        r0 = row_start(j)
        q = q_ref[0, pl.ds(r0, ck), :]
        z = f_ref[0, pl.ds(r0, ck), :]
        v = v_ref[0, pl.ds(r0, ck), :]
        f = lbv + (1.0 - lbv) * _sigmoid(z)
        lf2 = jnp.log(f) * LOG2E
        k = 1.0 - f
        return q, v, k, _dot01_exact(m_cum, lf2)

    def stage1b(q, v, k, cums):
        bl2 = cums[0:ck]
        b2 = bl2 + per_block(cums[ck:ck + n_blk])
        tot_row = ck + (1 + n_lvl) * n_blk
        btot2 = cums[tot_row:tot_row + 1]
        log2_k = jnp.log(k) * LOG2E
        kb = b2 - log2_k
        kbl = bl2 - log2_k
        qe = q * jnp.exp2(b2)
        ke = jnp.exp2(btot2 - kb)
        q_lvl, k_lvl = [], []
        for lvl in range(n_lvl):
            split2 = per_block(cums[ck + (1 + lvl) * n_blk:ck + (2 + lvl) * n_blk])
            q_lvl.append(q * jnp.exp2(jnp.where(upper[lvl], b2 - split2, NEG_BIG)))
            k_lvl.append(jnp.exp2(jnp.where(upper[lvl], NEG_BIG, split2 - kb)))

        qe_s[...] = qe.astype(BF16)
        btot_s[...] = btot2
        for p in range(n_pairs):
            sl = slice(p * LANES, (p + 1) * LANES)
            diag_rows = []
            for jb in range(n_blk):
                rs = slice(jb * sb, (jb + 1) * sb)
                bl_b, kbl_b, q_b = bl2[rs, sl], kbl[rs, sl], q[rs, sl]
                terms = []
                for s in range(sb):
                    arg = jnp.where(sub_row >= flip(s, sb), bl_b - kbl_b[s:s + 1], NEG_BIG)
                    terms.append(q_b * jnp.exp2(arg))
                diag_rows.append(jnp.concatenate(terms, axis=1))
            diag_s[p] = _dot(jnp.concatenate(diag_rows, axis=0).astype(BF16), sel)
            for lvl in range(n_lvl):
                k_p = k_lvl[lvl][:, sl]
                lvl_s[p * n_lvl + lvl] = _dot_nt(q_lvl[lvl][:, sl].astype(BF16), split_heads(k_p))
            upd_s[p] = _dot_tn(v[:, sl].astype(BF16), ke[:, sl].astype(BF16))

    def stage2_issue(j):
        r0 = row_start(j)
        v = v_ref[0, pl.ds(r0, ck), :]
        out = []
        for p in range(n_pairs):
            sl = slice(p * LANES, (p + 1) * LANES)
            parts = []
            for jb in range(n_blk):
                blk = diag_s[p, jb * sb:(jb + 1) * sb, :]
                parts.append(pltpu.roll(blk, jb * sb, 1) if jb else blk)
            scores = jnp.concatenate(parts, axis=0)
            for lvl in range(n_lvl):
                s_lvl = lvl_s[p * n_lvl + lvl]
                scores = scores + (s_lvl if lvl == n_lvl - 1 else jnp.where(group_mask[lvl], s_lvl, 0.0))
            intra = _dot(scores.astype(BF16), split_heads(v[:, sl]))
            st = st_ref[p]
            inter = _dot_nt(qe_s[:, sl], st.astype(BF16))
            new_st = jnp.where(same_head, st * jnp.exp2(btot_s[:, sl]) + upd_s[p], 0.0)
            out.append((inter + intra, new_st))
        return r0, out

    def stage2_finish(r0, out):
        for p in range(n_pairs):
            o_ref[0, pl.ds(r0, ck), p * LANES:(p + 1) * LANES] = out[p][0]
            st_ref[p] = out[p][1]

    return stage1a, stage1b, stage2_issue, stage2_finish


N_HGRN_SCRATCH = 6


def _hgrn_kernel(qf_ref, ff_ref, vf_ref, qb_ref, fb_ref, vb_ref, lb_ref, of_ref, ob_ref, zero_ref, *scratch,
                 rows):
    fwd_scratch, bwd_scratch = scratch[:N_HGRN_SCRATCH], scratch[N_HGRN_SCRATCH:]

    @pl.when(pl.program_id(1) == 0)
    def _():
        fwd_scratch[0][...] = jnp.zeros_like(fwd_scratch[0])
        bwd_scratch[0][...] = jnp.zeros_like(bwd_scratch[0])

    f1a, f1b, f2, f3 = _hgrn_direction(False, qf_ref, ff_ref, vf_ref, lb_ref, of_ref, *fwd_scratch, rows=rows)
    b1a, b1b, b2, b3 = _hgrn_direction(True, qb_ref, fb_ref, vb_ref, lb_ref, ob_ref, *bwd_scratch, rows=rows)
    n_sub = rows // HGRN_CHUNK

    def stage1_both(j):
        fa = f1a(j)
        ba = b1a(j)
        f1b(*fa)
        b1b(*ba)

    zero_part = zero_ref.shape[0] // n_sub

    def store_zeros(j):
        zero_ref[pl.ds(pl.multiple_of(j * zero_part, SUBLANES), zero_part), :] = jnp.zeros(
            (zero_part, zero_ref.shape[1]), zero_ref.dtype)

    stage1_both(0)

    def pipelined(j, carry):
        fo = f2(j)
        bo = b2(j)
        store_zeros(j)
        stage1_both(j + 1)
        f3(*fo)
        b3(*bo)
        return carry

    lax.fori_loop(0, n_sub - 1, pipelined, 0)
    fo = f2(n_sub - 1)
    bo = b2(n_sub - 1)
    store_zeros(n_sub - 1)
    f3(*fo)
    b3(*bo)


def _hgrn(proj, lb, zero_rows, *, d_lru, d_hgrn, rows=512):
    bsz, seq, _ = proj.shape
    n_chunks = seq // rows
    zero_blk = zero_rows // (bsz * n_chunks)
    assert zero_blk * bsz * n_chunks == zero_rows and zero_blk % (SUBLANES * (rows // HGRN_CHUNK)) == 0
    assert 2 * (d_hgrn // HGRN_HEADS) == LANES and HGRN_SUB == SUBLANES and HGRN_CHUNK == HGRN_SUB * SUBLANES
    col0 = (2 * d_lru) // d_hgrn
    n_pairs = d_hgrn // LANES
    n_lvl = (HGRN_CHUNK // HGRN_SUB).bit_length() - 1
    fwd = lambda col: pl.BlockSpec((1, rows, d_hgrn), lambda b, c: (b, c, col))
    bwd = lambda col: pl.BlockSpec((1, rows, d_hgrn), lambda b, c: (b, n_chunks - 1 - c, col))
    direction_scratch = [
        pltpu.VMEM((n_pairs, LANES, LANES), F32),
        pltpu.VMEM((n_pairs, HGRN_CHUNK, LANES), F32),
        pltpu.VMEM((n_pairs * n_lvl, HGRN_CHUNK, LANES), F32),
        pltpu.VMEM((n_pairs, LANES, LANES), F32),
        pltpu.VMEM((HGRN_CHUNK, d_hgrn), BF16),
        pltpu.VMEM((1, d_hgrn), F32),
    ]
    assert len(direction_scratch) == N_HGRN_SCRATCH
    kern = functools.partial(_hgrn_kernel, rows=rows)
    out = jax.ShapeDtypeStruct((bsz, seq, d_hgrn), F32)
    return pl.pallas_call(
        kern,
        out_shape=(out, out, jax.ShapeDtypeStruct((zero_rows, LANES), F32)),
        grid=(bsz, n_chunks),
        in_specs=[fwd(col0), fwd(col0 + 1), fwd(col0 + 3), bwd(col0), bwd(col0 + 2), bwd(col0 + 3),
                  pl.BlockSpec((1, d_hgrn), lambda b, c: (0, 0))],
        out_specs=(pl.BlockSpec((1, rows, d_hgrn), lambda b, c: (b, c, 0)),
                   pl.BlockSpec((1, rows, d_hgrn), lambda b, c: (b, n_chunks - 1 - c, 0)),
                   pl.BlockSpec((zero_blk, LANES), lambda b, c: (b * n_chunks + c, 0))),
        scratch_shapes=direction_scratch + direction_scratch,
        compiler_params=_params(("arbitrary", "arbitrary")),
        name="hgrn2",
    )(proj, proj, proj, proj, proj, proj, lb.reshape(1, d_hgrn))


def _interleave(*streams):
    done = object()
    live = list(streams)
    while live:
        live = [s for s in live if next(s, done) is not done]


def _gelu_tanh(y):
    return 0.5 * y * (1.0 + jnp.tanh(0.7978845608028654 * (y + 0.044715 * (y * y * y))))


def _post_kernel(lru_hbm, proj_hbm, of_hbm, ob_hbm, x_hbm, nlw_ref, nhw_ref, wo_ref,
                 gm_ref, nfw_ref, scf_ref, shf_ref, wr_ref, br_ref,
                 xo_ref, h_ref, slab_ref, cnt_ref, route_ref, carry_ref, logits_s,
                 r_lru, r_y, r_of, r_ob, r_g, r_x, in_sem, *, tm, d_lru, tiles, n_tiles, y_col, g_col):
    step = pl.program_id(0)
    d_hgrn = r_of.shape[-1]

    @pl.when(step == 0)
    def _():
        carry_ref[...] = jnp.zeros_like(carry_ref)
        logits_s[...] = jnp.zeros_like(logits_s)

    def tile_copies(t, ring_slot):
        b, r0 = t // tiles, pl.multiple_of((t % tiles) * tm, tm)

        def cp(hbm, col, width, ring):
            return pltpu.make_async_copy(hbm.at[b, pl.ds(r0, tm), pl.ds(col, width)], ring.at[ring_slot],
                                         in_sem.at[ring_slot])

        return (cp(lru_hbm, 0, d_lru, r_lru), cp(proj_hbm, y_col * d_lru, d_lru, r_y),
                cp(of_hbm, 0, d_hgrn, r_of), cp(ob_hbm, 0, d_hgrn, r_ob),
                cp(proj_hbm, g_col * d_hgrn, d_hgrn, r_g), cp(x_hbm, 0, x_hbm.shape[-1], r_x))

    @pl.when(step == 0)
    def _():
        for t in range(min(IN_RING - 1, n_tiles)):
            for c in tile_copies(t, t):
                c.start()

    ahead = step + IN_RING - 1

    @pl.when(ahead < n_tiles)
    def _():
        for c in tile_copies(ahead, ahead % IN_RING):
            c.start()

    @pl.when(step < n_tiles)
    def _():
        for c in tile_copies(step, step % IN_RING):
            c.wait()

    slot = jnp.minimum(step, n_tiles - 1) % IN_RING
    lru_ref, y_ref, of_ref, ob_ref, g_ref, x_ref = (r.at[slot] for r in (r_lru, r_y, r_of, r_ob, r_g, r_x))

    def mixer():
        lru = lru_ref[...] * _gelu_tanh(y_ref[...])
        ms = jnp.mean(lru * lru, axis=-1, keepdims=True)
        yield
        lru = lru * lax.rsqrt(ms + NORM_EPS) * nlw_ref[...]

        hg = of_ref[...] + ob_ref[...]
        width = hg.shape[1]
        hd = width // HGRN_HEADS
        hd_shift = hd.bit_length() - 1
        er = lax.broadcasted_iota(jnp.int32, (width, width), 0) >> hd_shift
        ec = lax.broadcasted_iota(jnp.int32, (width, width), 1) >> hd_shift
        head_sum = jnp.where(er == ec, 1.0, 0.0).astype(BF16)
        sq = hg * hg
        sq_hi = sq.astype(BF16)
        sq_lo = (sq - sq_hi.astype(F32)).astype(BF16)
        ms_h = (_dot(sq_hi, head_sum) + _dot(sq_lo, head_sum)) * (1.0 / hd)
        yield
        g = g_ref[...]
        hg = (hg * lax.rsqrt(ms_h + NORM_EPS) * nhw_ref[...]) * (g * _sigmoid(g))
        yield
        mixed = _dot(lru.astype(BF16), wo_ref[0:d_lru, :])
        yield
        mixed = mixed + _dot(hg.astype(BF16), wo_ref[d_lru:, :])
        yield
        x_new = x_ref[...] + gm_ref[0] * mixed
        xo_ref[0] = x_new
        ms_f = jnp.mean(x_new * x_new, axis=-1, keepdims=True)
        yield
        h = (x_new * lax.rsqrt(ms_f + NORM_EPS) * nfw_ref[...]) * (1.0 + scf_ref[0]) + shf_ref[0]
        _tiles_store(h_ref, h, tm, lead=(0,))
        yield
        logits_s[...] = _dot(h.astype(BF16), wr_ref[...]) + br_ref[...]

    def routing():
        yield from _routing_steps(logits_s[...], jnp.where(step > 0, 1.0, 0.0), slab_ref, cnt_ref, route_ref,
                                  carry_ref, tm)

    _interleave(routing(), mixer())


def _routing_steps(logits, live, slab_ref, cnt_ref, route_ref, carry_ref, tm):
    lane = lax.broadcasted_iota(jnp.int32, (tm, ROUTE_LANES), 1)
    lane_f = lane.astype(F32)
    far = float(ROUTE_LANES)
    is_g = lane < N_GROUPS
    gl = jnp.where(is_g, logits, NEG_BIG)
    gmax = jnp.max(gl, axis=-1, keepdims=True)
    yield
    g_idx = jnp.min(jnp.where(gl == gmax, lane_f, far), axis=-1, keepdims=True)
    p_group = 1.0 / jnp.sum(jnp.where(is_g, jnp.exp(gl - gmax), 0.0), axis=-1, keepdims=True)
    yield
    e_lane = lane - N_GROUPS
    in_group = (e_lane >= 0) & (e_lane < N_EXPERTS) & ((e_lane >> (EXPERTS_PER_GROUP.bit_length() - 1)).astype(F32) == g_idx)
    ev = jnp.where(in_group, logits, NEG_BIG)
    top1 = jnp.max(ev, axis=-1, keepdims=True)
    yield
    i1 = jnp.min(jnp.where(in_group & (ev == top1), lane_f, far), axis=-1, keepdims=True)
    yield
    rest = in_group & (lane_f != i1)
    ev2 = jnp.where(rest, logits, NEG_BIG)
    top2 = jnp.max(ev2, axis=-1, keepdims=True)
    yield
    i2 = jnp.min(jnp.where(rest & (ev2 == top2), lane_f, far), axis=-1, keepdims=True)
    yield
    e1 = i1 - float(N_GROUPS)
    e2 = i2 - float(N_GROUPS)
    ex = jnp.exp(top2 - top1)
    w1 = p_group / (1.0 + ex)
    w2 = p_group * ex / (1.0 + ex)

    sel1 = lane_f == e1
    sel2 = lane_f == e2
    onehot = jnp.where(sel1 | sel2, live, 0.0)
    tr = lax.broadcasted_iota(jnp.int32, (tm, tm), 0)
    tc = lax.broadcasted_iota(jnp.int32, (tm, tm), 1)
    before = jnp.where(tc < tr, 1.0, 0.0).astype(BF16)
    cnt = _dot(before, onehot.astype(BF16)) + carry_ref[0:1]
    yield
    rank1 = jnp.sum(jnp.where(sel1, cnt, 0.0), axis=-1, keepdims=True)
    rank2 = jnp.sum(jnp.where(sel2, cnt, 0.0), axis=-1, keepdims=True)
    total = carry_ref[0:1] + jnp.sum(onehot, axis=0, keepdims=True)
    carry_ref[...] = jnp.broadcast_to(total, carry_ref.shape)
    cnt_ref[...] = jnp.broadcast_to(total, cnt_ref.shape)
    yield

    slab = jnp.where(lane == 0, e1, 0.0)
    slab = jnp.where(lane == 1, e2, slab)
    slab = jnp.where(lane == 2, w1, slab)
    slab = jnp.where(lane == 3, w2, slab)
    slab = jnp.where(lane == 4, rank1, slab)
    slab = jnp.where(lane == 5, rank2, slab)
    slab_ref[0] = slab
    route_ref[...] = slab.T[0:SUBLANES]


def _post_mixer(lru_sum, proj, hg_f, hg_b, x, nlw, nhw, wo_bf16, g_mix, nfw, sc_ffn, sh_ffn, wr_bf16, br,
                *, tm=512):
    bsz, seq, d = x.shape
    d_lru = lru_sum.shape[-1]
    d_hgrn = hg_f.shape[-1]
    y_col = 1
    g_col = (2 * d_lru) // d_hgrn + 4
    tiles = seq // tm
    n_tiles = bsz * tiles
    cur = lambda s: jnp.minimum(s, n_tiles - 1)
    prev = lambda s: jnp.maximum(s - 1, 0)
    row = lambda w, col=0: pl.BlockSpec((1, tm, w), lambda s: (cur(s) // tiles, cur(s) % tiles, col))
    vec = lambda w: pl.BlockSpec((1, w), lambda s: (0, 0))
    per_b = lambda: pl.BlockSpec((1, 1, d), lambda s: (cur(s) // tiles, 0, 0))
    hbm = lambda: pl.BlockSpec(memory_space=pl.ANY)
    ring = lambda w: pltpu.VMEM((IN_RING, tm, w), F32)
    kern = functools.partial(_post_kernel, tm=tm, d_lru=d_lru, tiles=tiles, n_tiles=n_tiles, y_col=y_col,
                             g_col=g_col)
    return pl.pallas_call(
        kern,
        out_shape=(
            jax.ShapeDtypeStruct((bsz, seq, d), F32),
            jax.ShapeDtypeStruct((bsz, seq * SUBLANES, LANES), F32),
            jax.ShapeDtypeStruct((bsz, seq, ROUTE_LANES), F32),
            jax.ShapeDtypeStruct((SUBLANES, ROUTE_LANES), F32),
            jax.ShapeDtypeStruct((SUBLANES, bsz * seq), F32),
        ),
        grid=(n_tiles + 1,),
        in_specs=[
            hbm(), hbm(), hbm(), hbm(), hbm(),
            vec(d_lru), vec(d_hgrn),
            pl.BlockSpec((d, d), lambda s: (0, 0)),
            per_b(), vec(d), per_b(), per_b(),
            pl.BlockSpec((d, ROUTE_LANES), lambda s: (0, 0)),
            vec(ROUTE_LANES),
        ],
        out_specs=(
            row(d),
            pl.BlockSpec((1, tm * SUBLANES, LANES), lambda s: (cur(s) // tiles, cur(s) % tiles, 0)),
            pl.BlockSpec((1, tm, ROUTE_LANES), lambda s: (prev(s) // tiles, prev(s) % tiles, 0)),
            pl.BlockSpec((SUBLANES, ROUTE_LANES), lambda s: (0, 0)),
            pl.BlockSpec((SUBLANES, tm), lambda s: (0, prev(s))),
        ),
        scratch_shapes=[pltpu.VMEM((SUBLANES, ROUTE_LANES), F32), pltpu.VMEM((tm, ROUTE_LANES), F32),
                        ring(d_lru), ring(d_lru), ring(d_hgrn), ring(d_hgrn), ring(d_hgrn), ring(d),
                        pltpu.SemaphoreType.DMA((IN_RING,))],
        compiler_params=_params(("arbitrary",)),
        name="post_mixer_router",
    )(lru_sum, proj, hg_f, hg_b, x, nlw.reshape(1, d_lru), nhw.reshape(1, d_hgrn), wo_bf16,
      g_mix, nfw.reshape(1, d), sc_ffn, sh_ffn, wr_bf16, br)


def _tiles_load(ref, n, lead=()):
    return jnp.concatenate(
        [ref[(*lead, pl.ds(j, n, stride=SUBLANES), slice(None))] for j in range(SUBLANES)], axis=1)


def _tiles_store(ref, val, n, lead=()):
    for j in range(SUBLANES):
        ref[(*lead, pl.ds(j, n, stride=SUBLANES), slice(None))] = val[:, j * LANES:(j + 1) * LANES]


def _token_tile(ref, t):
    return ref.at[pl.ds(pl.multiple_of(t * SUBLANES, SUBLANES), SUBLANES)]


def _dest_kernel(start_ref, route_ref, o_ref):
    route = route_ref[...].astype(jnp.int32)
    start = jnp.zeros_like(route)
    for e in range(N_EXPERTS):
        start = jnp.where(route == e, start_ref[e], start)
    o_ref[...] = start + pltpu.roll(route, SUBLANES // 2, 0)


def _dest_rows(expert_start, route):
    return pl.pallas_call(
        _dest_kernel,
        out_shape=jax.ShapeDtypeStruct(route.shape, jnp.int32),
        grid_spec=pltpu.PrefetchScalarGridSpec(
            num_scalar_prefetch=1,
            grid=(1,),
            in_specs=[pl.BlockSpec(route.shape, lambda i, s: (0, 0))],
            out_specs=pl.BlockSpec(route.shape, lambda i, s: (0, 0)),
        ),
        compiler_params=pltpu.CompilerParams(dimension_semantics=("arbitrary",)),
        name="moe_dest_rows",
    )(expert_start, route)


def _dispatch_kernel(d1_ref, d2_ref, h_ref, z_ref, o_ref, sem, *, tb):
    del z_ref
    base = pl.program_id(0) * tb

    def issue(r, carry):
        t = base + r
        pltpu.make_async_copy(_token_tile(h_ref, r), _token_tile(o_ref, d1_ref[t]), sem).start(priority=0)
        pltpu.make_async_copy(_token_tile(h_ref, r), _token_tile(o_ref, d2_ref[t]), sem).start(priority=1)
        return carry

    lax.fori_loop(0, tb, issue, 0, unroll=DMA_ISSUE_UNROLL)
    for _ in range(2):
        pltpu.make_async_copy(h_ref, o_ref.at[pl.ds(0, tb * SUBLANES)], sem).wait()


def _dispatch(dest1, dest2, h_tiles, zero_tiles, *, tb=512):
    m = h_tiles.shape[0] // SUBLANES
    n_rows = zero_tiles.shape[0] // SUBLANES
    kern = functools.partial(_dispatch_kernel, tb=tb)
    return pl.pallas_call(
        kern,
        out_shape=jax.ShapeDtypeStruct((n_rows * SUBLANES, LANES), h_tiles.dtype),
        grid_spec=pltpu.PrefetchScalarGridSpec(
            num_scalar_prefetch=2,
            grid=(m // tb,),
            in_specs=[pl.BlockSpec((tb * SUBLANES, LANES), lambda i, d1, d2: (i, 0)),
                      pl.BlockSpec(memory_space=pl.ANY)],
            out_specs=pl.BlockSpec(memory_space=pl.ANY),
            scratch_shapes=[pltpu.SemaphoreType.DMA(())],
        ),
        input_output_aliases={3: 0},
        compiler_params=pltpu.CompilerParams(dimension_semantics=("arbitrary",), has_side_effects=True),
        name="moe_dispatch",
    )(dest1, dest2, h_tiles, zero_tiles)


def _expert_kernel(plan_ref, x_hbm, wg_hbm, wu_hbm, wd_hbm, o_ref, wg_f, wu_f, wd_f, wg_s, wu_s, wd_s, sem,
                   x_ring, x_sem, *, layer, blk):
    i = pl.program_id(0)
    n_blocks = pl.num_programs(0)
    n_used = plan_ref[n_blocks]
    expert = plan_ref[i]
    next_expert = plan_ref[n_blocks + 1 + i]
    slot = plan_ref[2 * n_blocks + 1 + i]
    first_block = ((i == 0) | (plan_ref[jnp.maximum(i - 1, 0)] != expert)) & (i < n_used)

    def copies(e, s):
        return (pltpu.make_async_copy(wg_hbm.at[layer, e], wg_f.at[s], sem.at[s]),
                pltpu.make_async_copy(wu_hbm.at[layer, e], wu_f.at[s], sem.at[s]),
                pltpu.make_async_copy(wd_hbm.at[layer, e], wd_f.at[s], sem.at[s]))

    @pl.when(i == 0)
    def _():
        for c in copies(expert, slot):
            c.start()

    @pl.when(first_block)
    def _():
        for c in copies(expert, slot):
            c.wait()
        wg_s[...] = wg_f[slot].astype(BF16)
        wu_s[...] = wu_f[slot].astype(BF16)
        wd_s[...] = wd_f[slot].astype(BF16)

        @pl.when(next_expert >= 0)
        def _():
            for c in copies(next_expert, 1 - slot):
                c.start()

    def x_copy(block, ring_slot):
        rows = blk * SUBLANES
        return pltpu.make_async_copy(x_hbm.at[pl.ds(pl.multiple_of(block * rows, rows), rows)],
                                     x_ring.at[ring_slot], x_sem.at[ring_slot])

    @pl.when(i == 0)
    def _():
        for b in range(X_RING - 1):
            @pl.when(b < n_used)
            def _():
                x_copy(b, b).start()

    ahead = i + X_RING - 1

    @pl.when(ahead < n_used)
    def _():
        x_copy(ahead, ahead % X_RING).start()

    for ring_slot in range(X_RING):
        @pl.when((i < n_used) & (i % X_RING == ring_slot))
        def _():
            x_copy(i, ring_slot).wait()
            x = _tiles_load(x_ring, blk, lead=(ring_slot,)).astype(BF16)
            gate = _dot(x, wg_s[...])
            up = _dot(x, wu_s[...])
            act = (gate * _sigmoid(gate)) * up
            _tiles_store(o_ref, _dot(act.astype(BF16), wd_s[...]), blk)

    @pl.when(i >= n_used)
    def _():
        o_ref[...] = jnp.zeros_like(o_ref)


def _expert_plan(blk_expert, blocks_used):
    n_blocks = blk_expert.shape[0]
    idx = jnp.arange(n_blocks, dtype=jnp.int32)
    change = jnp.concatenate([jnp.ones((1,), bool), blk_expert[1:] != blk_expert[:-1]])
    slot = (jnp.cumsum(change.astype(jnp.int32)) - 1) & 1
    change_at = jnp.where(change, idx, n_blocks)
    from_here = lax.cummin(change_at[::-1])[::-1]
    next_change = jnp.concatenate([from_here[1:], jnp.full((1,), n_blocks, jnp.int32)])
    next_expert = jnp.where(next_change < blocks_used, blk_expert[jnp.minimum(next_change, n_blocks - 1)], -1)
    return jnp.concatenate([blk_expert, blocks_used.reshape(1), next_expert, slot]).astype(jnp.int32)


def _experts(plan, x_tiles, wg, wu, wd, layer):
    n_rows = x_tiles.shape[0] // SUBLANES
    d, de = wg.shape[-2:]
    blk = EXPERT_BLOCK
    n_blocks = n_rows // blk
    kern = functools.partial(_expert_kernel, layer=layer, blk=blk)
    return pl.pallas_call(
        kern,
        out_shape=jax.ShapeDtypeStruct((n_rows * SUBLANES, LANES), F32),
        grid_spec=pltpu.PrefetchScalarGridSpec(
            num_scalar_prefetch=1,
            grid=(n_blocks,),
            in_specs=[
                pl.BlockSpec(memory_space=pl.ANY),
                pl.BlockSpec(memory_space=pl.ANY),
                pl.BlockSpec(memory_space=pl.ANY),
                pl.BlockSpec(memory_space=pl.ANY),
            ],
            out_specs=pl.BlockSpec((blk * SUBLANES, LANES), lambda i, plan: (i, 0)),
            scratch_shapes=[
                pltpu.VMEM((2, d, de), F32), pltpu.VMEM((2, d, de), F32), pltpu.VMEM((2, de, d), F32),
                pltpu.VMEM((d, de), BF16), pltpu.VMEM((d, de), BF16), pltpu.VMEM((de, d), BF16),
                pltpu.SemaphoreType.DMA((2,)),
                pltpu.VMEM((X_RING, blk * SUBLANES, LANES), F32), pltpu.SemaphoreType.DMA((X_RING,)),
            ],
        ),
        compiler_params=_params(("arbitrary",)),
        name="moe_experts",
    )(plan, x_tiles, wg, wu, wd)


def _combine_kernel(d1_ref, d2_ref, y_ref, slab_ref, x_ref, g_ref, nw_ref, o_ref, ra0, rb0, ra1, rb1, sem,
                    *, tm, tiles, n_steps, final_norm):
    step = pl.program_id(0) * tiles + pl.program_id(1)
    bufs = ((ra0, rb0), (ra1, rb1))

    def gather(tile, slot):
        base = tile * tm
        r1_ref, r2_ref = bufs[slot]

        def issue(r, carry):
            t = base + r
            pltpu.make_async_copy(_token_tile(y_ref, d1_ref[t]), _token_tile(r1_ref, r),
                                  sem.at[slot]).start(priority=0)
            pltpu.make_async_copy(_token_tile(y_ref, d2_ref[t]), _token_tile(r2_ref, r),
                                  sem.at[slot]).start(priority=1)
            return carry

        lax.fori_loop(0, tm, issue, 0, unroll=DMA_ISSUE_UNROLL)

    @pl.when(step == 0)
    def _():
        gather(0, 0)

    for slot in range(2):
        @pl.when((step & 1) == slot)
        def _():
            @pl.when(step + 1 < n_steps)
            def _():
                gather(step + 1, 1 - slot)

            r1_ref, r2_ref = bufs[slot]
            pltpu.make_async_copy(y_ref.at[pl.ds(0, tm * SUBLANES)], r1_ref, sem.at[slot]).wait()
            pltpu.make_async_copy(y_ref.at[pl.ds(0, tm * SUBLANES)], r2_ref, sem.at[slot]).wait()
            slab = slab_ref[0]
            y = slab[:, 2:3] * _tiles_load(r1_ref, tm) + slab[:, 3:4] * _tiles_load(r2_ref, tm)
            out = x_ref[0] + g_ref[0] * y
            if final_norm:
                ms = jnp.mean(out * out, axis=-1, keepdims=True)
                out = out * lax.rsqrt(ms + NORM_EPS) * nw_ref[...]
            o_ref[0] = out


def _combine(dest1, dest2, y_buf, slab, x, g_ffn, norm_w, *, final_norm, tm=512):
    bsz, seq, d = x.shape
    tiles = seq // tm
    kern = functools.partial(_combine_kernel, tm=tm, tiles=tiles, n_steps=bsz * tiles, final_norm=final_norm)
    row_buf = pltpu.VMEM((tm * SUBLANES, LANES), F32)
    return pl.pallas_call(
        kern,
        out_shape=jax.ShapeDtypeStruct((bsz, seq, d), F32),
        grid_spec=pltpu.PrefetchScalarGridSpec(
            num_scalar_prefetch=2,
            grid=(bsz, tiles),
            in_specs=[
                pl.BlockSpec(memory_space=pl.ANY),
                pl.BlockSpec((1, tm, ROUTE_LANES), lambda b, i, d1, d2: (b, i, 0)),
                pl.BlockSpec((1, tm, d), lambda b, i, d1, d2: (b, i, 0)),
                pl.BlockSpec((1, 1, d), lambda b, i, d1, d2: (b, 0, 0)),
                pl.BlockSpec((1, d), lambda b, i, d1, d2: (0, 0)),
            ],
            out_specs=pl.BlockSpec((1, tm, d), lambda b, i, d1, d2: (b, i, 0)),
            scratch_shapes=[row_buf, row_buf, row_buf, row_buf, pltpu.SemaphoreType.DMA((2,))],
        ),
        compiler_params=_params(("arbitrary", "arbitrary")),
        name="moe_combine",
    )(dest1, dest2, y_buf, slab, x, g_ffn, norm_w.reshape(1, d))


def _block_diag(w):
    heads, hd, _ = w.shape
    n = heads * hd
    tiled = jnp.tile(w.reshape(n, hd), (1, heads))
    blk_r = lax.broadcasted_iota(jnp.int32, (n, n), 0) // hd
    blk_c = lax.broadcasted_iota(jnp.int32, (n, n), 1) // hd
    return jnp.where(blk_r == blk_c, tiled, 0.0)


def kernel(x, c, ada_w, ada_b, norm_mix_w, w_in, conv_w, conv_b, lru_wa, lru_ba, lru_wx, lru_bx, lru_lambda, norm_lru_w, hgrn_lb, norm_hgrn_w, w_out, norm_ffn_w, router_group_w, router_group_b, router_expert_w, router_expert_b, expert_w_gate, expert_w_up, expert_w_down, final_norm_w):
    bsz, seq, d = x.shape
    assert d == SUBLANES * LANES, "the MoE row movement keeps one (8, 128) tile per token"
    depth = ada_w.shape[0]
    d_lru = conv_w.shape[-1]
    d_hgrn = hgrn_lb.shape[-1]
    m = bsz * seq
    n_rows = m * 2 + N_EXPERTS * EXPERT_BLOCK
    n_blocks = n_rows // EXPERT_BLOCK

    mod = _modulation(c, ada_w, ada_b)
    lb_cum = jnp.cumsum(jax.nn.softmax(hgrn_lb.astype(F32), axis=0), axis=0)
    lb_all = lb_cum - lb_cum[0:1]

    for l in range(depth):
        sh_mix, sc_mix, g_mix, sh_ffn, sc_ffn, g_ffn = [
            mod[l, :, i * d:(i + 1) * d].reshape(bsz, 1, d) for i in range(N_MODULATIONS)]
        wa_bd = jnp.stack([_block_diag(lru_wa[l, 0]), _block_diag(lru_wa[l, 1])]).astype(BF16)
        wx_bd = jnp.stack([_block_diag(lru_wx[l, 0]), _block_diag(lru_wx[l, 1])]).astype(BF16)
        lru_w = (conv_w[l], conv_b[l], wa_bd, lru_ba[l], wx_bd, lru_bx[l], lru_lambda[l])
        proj, lru_fwd = _in_proj_lru(x, norm_mix_w[l], sc_mix, sh_mix, w_in, l, *lru_w)
        lru_sum = _lru_scan(proj, *lru_w, reverse=True, add_to=lru_fwd)
        hg_f, hg_b, zero_tiles = _hgrn(proj, lb_all[l], n_rows * SUBLANES, d_lru=d_lru, d_hgrn=d_hgrn)

        lane_pad = ROUTE_LANES - N_GROUPS - N_EXPERTS
        wr = jnp.pad(jnp.concatenate([router_group_w[l], router_expert_w[l]], axis=1), ((0, 0), (0, lane_pad)))
        br = jnp.pad(jnp.concatenate([router_group_b[l], router_expert_b[l]]), (0, lane_pad)).reshape(1, ROUTE_LANES)
        x_mid, h_ffn, slab, counts, route = _post_mixer(
            lru_sum, proj, hg_f, hg_b, x, norm_lru_w[l], norm_hgrn_w[l], w_out[l].astype(BF16), g_mix,
            norm_ffn_w[l], sc_ffn, sh_ffn, wr.astype(BF16), br)

        cnt = counts[0, :N_EXPERTS].astype(jnp.int32)
        padded = ((cnt + EXPERT_BLOCK - 1) // EXPERT_BLOCK) * EXPERT_BLOCK
        pend = jnp.cumsum(padded)
        pstart = pend - padded
        blk_start = jnp.arange(n_blocks, dtype=jnp.int32) * EXPERT_BLOCK
        blk_expert = jnp.minimum(jnp.sum(pend[None, :] <= blk_start[:, None], axis=1), N_EXPERTS - 1)
        plan = _expert_plan(blk_expert.astype(jnp.int32), (pend[N_EXPERTS - 1] // EXPERT_BLOCK).astype(jnp.int32))
        dest = _dest_rows(pstart.astype(jnp.int32), route)
        dest1, dest2 = dest[0], dest[1]

        x_buf = _dispatch(dest1, dest2, h_ffn.reshape(m * SUBLANES, LANES), zero_tiles)
        y_buf = _experts(plan, x_buf, expert_w_gate, expert_w_up, expert_w_down, l)
        x = _combine(dest1, dest2, y_buf, slab, x_mid, g_ffn, final_norm_w, final_norm=(l == depth - 1))

    return x
```

```python
import functools

import jax
import jax.numpy as jnp
from jax import lax
from jax.experimental import pallas as pl
from jax.experimental.pallas import tpu as pltpu

F32 = jnp.float32
BF16 = jnp.bfloat16

HGRN_HEADS = 8
N_MODULATIONS = 6
CONV_WIDTH = 4
LRU_C = 8.0
N_GROUPS = 4
EXPERTS_PER_GROUP = 8
N_EXPERTS = N_GROUPS * EXPERTS_PER_GROUP
NORM_EPS = 1e-6

LANES = 128
SUBLANES = 8
VMEM_LIMIT = 56 * 1024 * 1024

HGRN_CHUNK = 64
HGRN_SUB = 8
LOG2E = 1.4426950408889634
ROUTE_LANES = LANES
EXPERT_BLOCK = 512
X_RING = 3
IN_RING = 3
DMA_ISSUE_UNROLL = 8
NEG_BIG = -3.0e38


def _params(sem):
    return pltpu.CompilerParams(dimension_semantics=sem, vmem_limit_bytes=VMEM_LIMIT)


def _dot(a, b):
    return jnp.dot(a, b, preferred_element_type=F32)


def _dot_nt(a, b):
    return lax.dot_general(a, b, (((1,), (1,)), ((), ())), preferred_element_type=F32)


def _dot_tn(a, b):
    return lax.dot_general(a, b, (((0,), (0,)), ((), ())), preferred_element_type=F32)


def _dot01_exact(m01, x):
    hi = x.astype(BF16)
    r1 = x - hi.astype(F32)
    mid = r1.astype(BF16)
    lo = (r1 - mid.astype(F32)).astype(BF16)
    return _dot(m01, hi) + _dot(m01, mid) + _dot(m01, lo)


def _sigmoid(x):
    return 1.0 / (1.0 + jnp.exp(-x))


def _sigmoid_tanh(x):
    return 0.5 * jnp.tanh(0.5 * x) + 0.5


def _mod_kernel(c_ref, w_ref, b_ref, o_ref):
    c = c_ref[...]
    cond = c * _sigmoid(c)
    o_ref[0] = _dot(cond.astype(BF16), w_ref[0].astype(BF16)) + b_ref[0]


def _modulation(c, ada_w, ada_b):
    depth, d, n = ada_w.shape
    bsz = c.shape[0]
    rows = -(-bsz // SUBLANES) * SUBLANES
    c_pad = jnp.pad(c, ((0, rows - bsz), (0, 0)))
    tn = n // N_MODULATIONS
    out = pl.pallas_call(
        _mod_kernel,
        out_shape=jax.ShapeDtypeStruct((depth, rows, n), F32),
        grid=(depth, n // tn),
        in_specs=[
            pl.BlockSpec((rows, d), lambda l, j: (0, 0)),
            pl.BlockSpec((1, d, tn), lambda l, j: (l, 0, j)),
            pl.BlockSpec((1, 1, tn), lambda l, j: (l, 0, j)),
        ],
        out_specs=pl.BlockSpec((1, rows, tn), lambda l, j: (l, 0, j)),
        compiler_params=_params(("arbitrary", "arbitrary")),
        name="adaln_mod",
    )(c_pad, ada_w, ada_b.reshape(depth, 1, n))
    return out[:, :bsz]


def _rms_mod(x, nw, sc, sh):
    ms = jnp.mean(x * x, axis=-1, keepdims=True)
    return (x * lax.rsqrt(ms + NORM_EPS) * nw) * (1.0 + sc) + sh


def _input_ring(step, n, depth, copies):
    @pl.when(step == 0)
    def _():
        for t in range(depth - 1):
            @pl.when(t < n)
            def _():
                for c in copies(t, t):
                    c.start()

    ahead = step + depth - 1

    @pl.when(ahead < n)
    def _():
        for c in copies(ahead, ahead % depth):
            c.start()

    @pl.when(step < n)
    def _():
        for c in copies(step, step % depth):
            c.wait()


def _inproj_lru_kernel(x_hbm, nw_ref, sc_ref, sh_ref, w_ref, cw_ref, cb_ref, wa_ref, ba_ref, wx_ref, bx_ref,
                       lam_ref, proj_ref, lru_ref, w_s, xl_s, xnew_s, carry_ref, sa_ref, sb_ref, cin_ref,
                       x_ring, x_sem, *, tm, tiles, n_tiles, d_lru):
    s = pl.program_id(0)
    n_cols = proj_ref.shape[-1]

    @pl.when(s == 0)
    def _():
        w_s[...] = w_ref[0].astype(BF16)
        xl_s[...] = jnp.zeros_like(xl_s)
        carry_ref[...] = jnp.zeros_like(carry_ref)

    prev = jnp.maximum(s - 1, 0)
    chunk = prev % tiles
    has_prev = jnp.where(chunk > 0, 1.0, 0.0)
    has_next = jnp.where(chunk < tiles - 1, 1.0, 0.0)

    def x_copies(t, ring_slot):
        rows = x_hbm.at[t // tiles, pl.ds(pl.multiple_of((t % tiles) * tm, tm), tm)]
        return (pltpu.make_async_copy(rows, x_ring.at[ring_slot], x_sem.at[ring_slot]),)

    _input_ring(s, n_tiles, IN_RING, x_copies)
    x_tile = x_ring[jnp.minimum(s, n_tiles - 1) % IN_RING]
    hb = _rms_mod(x_tile, nw_ref[...], sc_ref[0], sh_ref[0]).astype(BF16)

    def project():
        step = 2 * LANES
        for c0 in range(0, n_cols, step):
            block = _dot(hb, w_s[:, c0:c0 + step])
            proj_ref[0, :, c0:c0 + step] = block
            if c0 < d_lru:
                xnew_s[:, c0:c0 + step] = block
            if c0 + step >= d_lru:
                yield

    def scan():
        yield

        def store(rs, h_rows):
            lru_ref[0, rs, :] = h_rows

        yield from _lru_steps(xl_s[SUBLANES:, :], xl_s[0:SUBLANES, :] * has_prev,
                              xnew_s[0:SUBLANES, :] * has_next, carry_ref[...] * has_prev,
                              cw_ref[...], cb_ref[...], wa_ref[0], ba_ref[0], wx_ref[0], bx_ref[0], lam_ref[0],
                              store, carry_ref, sa_ref, sb_ref, cin_ref, reverse=False)

    _interleave(project(), scan())
    xl_s[0:SUBLANES, :] = xl_s[tm:tm + SUBLANES, :]
    xl_s[SUBLANES:, :] = xnew_s[...]


def _in_proj_lru(x, nw, sc, sh, w_in, layer, conv_w, conv_b, wa_bd, ba, wx_bd, bx, lam, tm=512):
    bsz, seq, d = x.shape
    n = w_in.shape[-1]
    d_lru = conv_w.shape[1]
    tiles = seq // tm
    n_tiles = bsz * tiles
    cur = lambda s: jnp.minimum(s, n_tiles - 1)
    prev = lambda s: jnp.maximum(s - 1, 0)
    vec = lambda: pl.BlockSpec((1, 1, d_lru), lambda s: (0, 0, 0))
    mat = lambda: pl.BlockSpec((1, d_lru, d_lru), lambda s: (0, 0, 0))
    kern = functools.partial(_inproj_lru_kernel, tm=tm, tiles=tiles, n_tiles=n_tiles, d_lru=d_lru)
    return pl.pallas_call(
        kern,
        out_shape=(jax.ShapeDtypeStruct((bsz, seq, n), F32), jax.ShapeDtypeStruct((bsz, seq, d_lru), F32)),
        grid=(n_tiles + 1,),
        in_specs=[
            pl.BlockSpec(memory_space=pl.ANY),
            pl.BlockSpec((1, d), lambda s: (0, 0)),
            pl.BlockSpec((1, 1, d), lambda s: (cur(s) // tiles, 0, 0)),
            pl.BlockSpec((1, 1, d), lambda s: (cur(s) // tiles, 0, 0)),
            pl.BlockSpec((1, d, n), lambda s: (layer, 0, 0), pipeline_mode=pl.Buffered(1)),
            pl.BlockSpec((CONV_WIDTH, d_lru), lambda s: (0, 0)),
            pl.BlockSpec((1, d_lru), lambda s: (0, 0)),
            mat(), vec(), mat(), vec(), vec(),
        ],
        out_specs=(pl.BlockSpec((1, tm, n), lambda s: (cur(s) // tiles, cur(s) % tiles, 0)),
                   pl.BlockSpec((1, tm, d_lru), lambda s: (prev(s) // tiles, prev(s) % tiles, 0))),
        scratch_shapes=[
            pltpu.VMEM((d, n), BF16),
            pltpu.VMEM((tm + SUBLANES, d_lru), F32),
            pltpu.VMEM((tm, d_lru), F32),
            pltpu.VMEM((1, d_lru), F32),
            pltpu.VMEM((d_lru // LANES, tm, LANES), F32), pltpu.VMEM((d_lru // LANES, tm, LANES), F32),
            pltpu.VMEM((tm // SUBLANES, d_lru), F32),
            pltpu.VMEM((IN_RING, tm, d), F32), pltpu.SemaphoreType.DMA((IN_RING,)),
        ],
        compiler_params=_params(("arbitrary",)),
        name="in_proj_lru_fwd",
    )(x, nw.reshape(1, d), sc, sh, w_in, conv_w, conv_b.reshape(1, d_lru), wa_bd, ba.reshape(2, 1, d_lru),
      wx_bd, bx.reshape(2, 1, d_lru), lam.reshape(2, 1, d_lru))


def _lru_kernel(x_ref, xp_ref, xn_ref, cw_ref, cb_ref, wa_ref, ba_ref, wx_ref, bx_ref, lam_ref, *rest,
                reverse, n_chunks, rows, accumulate):
    add_ref = rest[0] if accumulate else None
    o_ref, carry_ref, sa_ref, sb_ref, cin_ref = rest[1:] if accumulate else rest
    c = pl.program_id(1)
    chunk = (n_chunks - 1 - c) if reverse else c

    @pl.when(c == 0)
    def _():
        carry_ref[...] = jnp.zeros_like(carry_ref)

    has_prev = jnp.where(chunk > 0, 1.0, 0.0)
    has_next = jnp.where(chunk < n_chunks - 1, 1.0, 0.0)

    def store(rs, h_rows):
        o_ref[0, rs, :] = (add_ref[0, rs, :] + h_rows) if accumulate else h_rows

    _interleave(_lru_steps(x_ref[0], xp_ref[0] * has_prev, xn_ref[0] * has_next, carry_ref[...],
                           cw_ref[...], cb_ref[...], wa_ref[0], ba_ref[0], wx_ref[0], bx_ref[0], lam_ref[0],
                           store, carry_ref, sa_ref, sb_ref, cin_ref, reverse=reverse))


def _lru_steps(x, xp, xn, carry, cw, cb, wa, ba, wx, bx, lam, store, carry_ref, sa_ref, sb_ref, cin_ref, *,
               reverse):
    rows, width = x.shape
    xe = jnp.concatenate([xp, x, xn], axis=0)
    xc = cb
    for k in range(CONV_WIDTH):
        lo = SUBLANES + k - CONV_WIDTH // 2
        xc = xc + cw[k:k + 1] * xe[lo:lo + rows]
    yield

    xcb = xc.astype(BF16)
    r = _sigmoid_tanh(_dot(xcb, wa) + ba)
    yield
    gate_i = _sigmoid_tanh(_dot(xcb, wx) + bx)
    yield
    softplus_neg_lam = jnp.maximum(-lam, 0.0) + jnp.log1p(jnp.exp(-jnp.abs(lam)))
    log_a = (-LRU_C) * r * softplus_neg_lam
    a = jnp.exp(log_a)
    t = jnp.tanh(-log_a)
    u = jnp.sqrt(2.0 * t / (1.0 + t)) * (gate_i * xc)
    yield

    groups = rows // SUBLANES
    acc_a = a.reshape(groups, SUBLANES, width)
    acc_b = u.reshape(groups, SUBLANES, width)
    sub = lax.broadcasted_iota(jnp.int32, (groups, SUBLANES, width), 1)
    s = 1
    while s < SUBLANES:
        if reverse:
            valid = sub < SUBLANES - s
            sh_a, sh_b = pltpu.roll(acc_a, SUBLANES - s, 1), pltpu.roll(acc_b, SUBLANES - s, 1)
        else:
            valid = sub >= s
            sh_a, sh_b = pltpu.roll(acc_a, s, 1), pltpu.roll(acc_b, s, 1)
        acc_b = jnp.where(valid, acc_a * sh_b + acc_b, acc_b)
        acc_a = jnp.where(valid, acc_a * sh_a, acc_a)
        s *= 2
        yield
    acc_a = acc_a.reshape(rows, width)
    acc_b = acc_b.reshape(rows, width)
    edge = 0 if reverse else SUBLANES - 1
    n_tiles = width // LANES
    for j in range(n_tiles):
        sa_ref[j] = acc_a[:, j * LANES:(j + 1) * LANES]
        sb_ref[j] = acc_b[:, j * LANES:(j + 1) * LANES]
    ea = jnp.concatenate([sa_ref[j, pl.ds(edge, groups, stride=SUBLANES), :] for j in range(n_tiles)], axis=1)
    eb = jnp.concatenate([sb_ref[j, pl.ds(edge, groups, stride=SUBLANES), :] for j in range(n_tiles)], axis=1)
    grow = lax.broadcasted_iota(jnp.int32, (groups, width), 0)
    s = 1
    while s < groups:
        if reverse:
            valid = grow < groups - s
            sh_a, sh_b = pltpu.roll(ea, groups - s, 0), pltpu.roll(eb, groups - s, 0)
        else:
            valid = grow >= s
            sh_a, sh_b = pltpu.roll(ea, s, 0), pltpu.roll(eb, s, 0)
        eb = jnp.where(valid, ea * sh_b + eb, eb)
        ea = jnp.where(valid, ea * sh_a, ea)
        s *= 2
    yield
    group_out = eb + ea * carry
    if reverse:
        carry_in = jnp.where(grow == groups - 1, carry, pltpu.roll(group_out, groups - 1, 0))
        carry_ref[...] = group_out[0:1]
    else:
        carry_in = jnp.where(grow == 0, carry, pltpu.roll(group_out, 1, 0))
        carry_ref[...] = group_out[groups - 1:groups]
    cin_ref[...] = carry_in
    for g in range(groups):
        rs = slice(g * SUBLANES, (g + 1) * SUBLANES)
        store(rs, acc_b[rs] + acc_a[rs] * cin_ref[g:g + 1, :])
        if g % (groups // 4) == groups // 4 - 1:
            yield


def _lru_scan(proj, conv_w, conv_b, wa_bd, ba, wx_bd, bx, lam, *, reverse, add_to=None, rows=512):
    bsz, seq, _ = proj.shape
    d_lru = conv_w.shape[1]
    n_chunks = seq // rows
    halo = rows // SUBLANES
    last_halo = seq // SUBLANES - 1
    dirn = 1 if reverse else 0

    def chunk_of(c):
        return (n_chunks - 1 - c) if reverse else c

    vec = lambda: pl.BlockSpec((1, 1, d_lru), lambda b, c: (dirn, 0, 0))
    mat = lambda: pl.BlockSpec((1, d_lru, d_lru), lambda b, c: (dirn, 0, 0))
    accumulate = add_to is not None
    kern = functools.partial(_lru_kernel, reverse=reverse, n_chunks=n_chunks, rows=rows, accumulate=accumulate)
    tile = pl.BlockSpec((1, rows, d_lru), lambda b, c: (b, chunk_of(c), 0))
    return pl.pallas_call(
        kern,
        out_shape=jax.ShapeDtypeStruct((bsz, seq, d_lru), F32),
        grid=(bsz, n_chunks),
        in_specs=[
            pl.BlockSpec((1, rows, d_lru), lambda b, c: (b, chunk_of(c), 0)),
            pl.BlockSpec((1, SUBLANES, d_lru),
                         lambda b, c: (b, jnp.maximum(chunk_of(c) * halo - 1, 0), 0)),
            pl.BlockSpec((1, SUBLANES, d_lru),
                         lambda b, c: (b, jnp.minimum((chunk_of(c) + 1) * halo, last_halo), 0)),
            pl.BlockSpec((CONV_WIDTH, d_lru), lambda b, c: (0, 0)),
            pl.BlockSpec((1, d_lru), lambda b, c: (0, 0)),
            mat(), vec(), mat(), vec(), vec(),
        ] + ([tile] if accumulate else []),
        out_specs=tile,
        scratch_shapes=[pltpu.VMEM((1, d_lru), F32), pltpu.VMEM((d_lru // LANES, rows, LANES), F32),
                        pltpu.VMEM((d_lru // LANES, rows, LANES), F32),
                        pltpu.VMEM((rows // SUBLANES, d_lru), F32)],
        compiler_params=_params(("arbitrary", "arbitrary")),
        name="lru_bwd" if reverse else "lru_fwd",
    )(proj, proj, proj, conv_w, conv_b.reshape(1, d_lru), wa_bd, ba.reshape(2, 1, d_lru),
      wx_bd, bx.reshape(2, 1, d_lru), lam.reshape(2, 1, d_lru), *([add_to] if accumulate else []))


def _hgrn_direction(rev, q_ref, f_ref, v_ref, lb_ref, o_ref, st_ref, diag_s, lvl_s, upd_s, qe_s, btot_s, *, rows):
    ck, sb = HGRN_CHUNK, HGRN_SUB
    n_blk = ck // sb
    sb_shift = sb.bit_length() - 1
    n_sub = rows // ck
    width = q_ref.shape[-1]
    n_pairs = width // LANES
    half = LANES // 2

    def flip(idx, n):
        return (n - 1 - idx) if rev else idx

    n_lvl = n_blk.bit_length() - 1
    tf = flip(lax.broadcasted_iota(jnp.int32, (ck, ck), 0), ck)
    uf = flip(lax.broadcasted_iota(jnp.int32, (ck, ck), 1), ck)
    tb, ub = tf >> sb_shift, uf >> sb_shift
    pb = flip(lax.broadcasted_iota(jnp.int32, (n_blk, ck), 0), n_blk)
    pub = flip(lax.broadcasted_iota(jnp.int32, (n_blk, ck), 1), ck) >> sb_shift
    mats = [jnp.where((tb == ub) & (uf <= tf), 1.0, 0.0),
            jnp.where(pub < pb, 1.0, 0.0)]
    for lvl in range(n_lvl):
        mid = ((pb >> (lvl + 1)) << (lvl + 1)) + (1 << lvl)
        mats.append(jnp.where(pub < mid, 1.0, 0.0))
    mats.append(jnp.ones((SUBLANES, ck), F32))
    m_cum = jnp.concatenate(mats, axis=0).astype(BF16)

    def per_block(rows8):
        return jnp.concatenate(
            [jnp.broadcast_to(rows8[jb:jb + 1], (sb, rows8.shape[1])) for jb in range(n_blk)], axis=0)
    row_blk = flip(lax.broadcasted_iota(jnp.int32, (ck, width), 0), ck) >> sb_shift
    upper = [((row_blk >> lvl) & 1) == 1 for lvl in range(n_lvl)]
    pr = flip(lax.broadcasted_iota(jnp.int32, (ck, LANES), 0), ck) >> sb_shift
    pc = flip(lax.broadcasted_iota(jnp.int32, (ck, LANES), 1) & (ck - 1), ck) >> sb_shift
    group_mask = [(pr >> (lvl + 1)) == (pc >> (lvl + 1)) for lvl in range(n_lvl)]
    lane = lax.broadcasted_iota(jnp.int32, (1, LANES), 1)
    head0 = lane < half

    def split_heads(x):
        xb = x.astype(BF16)
        zero = jnp.zeros_like(xb)
        return jnp.concatenate([jnp.where(head0, xb, zero), jnp.where(head0, zero, xb)], axis=0)

    sr = lax.broadcasted_iota(jnp.int32, (LANES, LANES), 0)
    sc = lax.broadcasted_iota(jnp.int32, (LANES, LANES), 1)
    same_head = (sr < half) == (sc < half)
    er = lax.broadcasted_iota(jnp.int32, (sb * LANES, LANES), 0)
    ec = lax.broadcasted_iota(jnp.int32, (sb * LANES, LANES), 1)
    sel = jnp.where(ec == (((er & (LANES - 1)) >> (half.bit_length() - 1)) * half + (er >> (LANES.bit_length() - 1))),
                    1.0, 0.0).astype(BF16)
    sub_row = flip(lax.broadcasted_iota(jnp.int32, (sb, LANES), 0), sb)
    lbv = lb_ref[...]

    def row_start(j):
        return pl.multiple_of(flip(j, n_sub) * ck, ck)

    def stage1a(j):
        r0 = row_start(j)
        q = q_ref[0, pl.ds(r0, ck), :]
        z = f_ref[0, pl.ds(r0, ck), :]
        v = v_ref[0, pl.ds(r0, ck), :]
        f = lbv + (1.0 - lbv) * _sigmoid(z)
        lf2 = jnp.log(f) * LOG2E
        k = 1.0 - f
        return q, v, k, _dot01_exact(m_cum, lf2)

    def stage1b(q, v, k, cums):
        bl2 = cums[0:ck]
        b2 = bl2 + per_block(cums[ck:ck + n_blk])
        tot_row = ck + (1 + n_lvl) * n_blk
        btot2 = cums[tot_row:tot_row + 1]
        log2_k = jnp.log(k) * LOG2E
        kb = b2 - log2_k
        kbl = bl2 - log2_k
        qe = q * jnp.exp2(b2)
        ke = jnp.exp2(btot2 - kb)
        q_lvl, k_lvl = [], []
        for lvl in range(n_lvl):
            split2 = per_block(cums[ck + (1 + lvl) * n_blk:ck + (2 + lvl) * n_blk])
            q_lvl.append(q * jnp.exp2(jnp.where(upper[lvl], b2 - split2, NEG_BIG)))
            k_lvl.append(jnp.exp2(jnp.where(upper[lvl], NEG_BIG, split2 - kb)))

        qe_s[...] = qe.astype(BF16)
        btot_s[...] = btot2
        for p in range(n_pairs):
            sl = slice(p * LANES, (p + 1) * LANES)
            diag_rows = []
            for jb in range(n_blk):
                rs = slice(jb * sb, (jb + 1) * sb)
                bl_b, kbl_b, q_b = bl2[rs, sl], kbl[rs, sl], q[rs, sl]
                terms = []
                for s in range(sb):
                    arg = jnp.where(sub_row >= flip(s, sb), bl_b - kbl_b[s:s + 1], NEG_BIG)
                    terms.append(q_b * jnp.exp2(arg))
                diag_rows.append(jnp.concatenate(terms, axis=1))
            diag_s[p] = _dot(jnp.concatenate(diag_rows, axis=0).astype(BF16), sel)
            for lvl in range(n_lvl):
                k_p = k_lvl[lvl][:, sl]
                lvl_s[p * n_lvl + lvl] = _dot_nt(q_lvl[lvl][:, sl].astype(BF16), split_heads(k_p))
            upd_s[p] = _dot_tn(v[:, sl].astype(BF16), ke[:, sl].astype(BF16))

    def stage2_issue(j):
        r0 = row_start(j)
        v = v_ref[0, pl.ds(r0, ck), :]
        out = []
        for p in range(n_pairs):
            sl = slice(p * LANES, (p + 1) * LANES)
            parts = []
            for jb in range(n_blk):
                blk = diag_s[p, jb * sb:(jb + 1) * sb, :]
                parts.append(pltpu.roll(blk, jb * sb, 1) if jb else blk)
            scores = jnp.concatenate(parts, axis=0)
            for lvl in range(n_lvl):
                s_lvl = lvl_s[p * n_lvl + lvl]
                scores = scores + (s_lvl if lvl == n_lvl - 1 else jnp.where(group_mask[lvl], s_lvl, 0.0))
            intra = _dot(scores.astype(BF16), split_heads(v[:, sl]))
            st = st_ref[p]
            inter = _dot_nt(qe_s[:, sl], st.astype(BF16))
            new_st = jnp.where(same_head, st * jnp.exp2(btot_s[:, sl]) + upd_s[p], 0.0)
            out.append((inter + intra, new_st))
        return r0, out

    def stage2_finish(r0, out):
        for p in range(n_pairs):
            o_ref[0, pl.ds(r0, ck), p * LANES:(p + 1) * LANES] = out[p][0]
            st_ref[p] = out[p][1]

    return stage1a, stage1b, stage2_issue, stage2_finish


N_HGRN_SCRATCH = 6


def _hgrn_kernel(proj_hbm, lb_ref, of_ref, ob_ref, zero_ref, *scratch, rows, n_chunks, n_steps, col0, d_hgrn):
    fwd_scratch, bwd_scratch = scratch[:N_HGRN_SCRATCH], scratch[N_HGRN_SCRATCH:2 * N_HGRN_SCRATCH]
    rings, ring_sem = scratch[2 * N_HGRN_SCRATCH:-1], scratch[-1]
    ring_cols = (col0, col0 + 1, col0 + 3, col0, col0 + 2, col0 + 3)

    def in_copies(t, ring_slot):
        b, c = t // n_chunks, t % n_chunks
        out = []
        for i, (ring, col) in enumerate(zip(rings, ring_cols)):
            tile = c if i < 3 else n_chunks - 1 - c
            src = proj_hbm.at[pl.ds(b, 1), pl.ds(pl.multiple_of(tile * rows, rows), rows),
                              pl.ds(col * d_hgrn, d_hgrn)]
            out.append(pltpu.make_async_copy(src, ring.at[ring_slot], ring_sem.at[ring_slot, i]))
        return out

    step = pl.program_id(0) * n_chunks + pl.program_id(1)
    _input_ring(step, n_steps, IN_RING, in_copies)
    qf_ref, ff_ref, vf_ref, qb_ref, fb_ref, vb_ref = (r.at[step % IN_RING] for r in rings)

    @pl.when(pl.program_id(1) == 0)
    def _():
        fwd_scratch[0][...] = jnp.zeros_like(fwd_scratch[0])
        bwd_scratch[0][...] = jnp.zeros_like(bwd_scratch[0])

    f1a, f1b, f2, f3 = _hgrn_direction(False, qf_ref, ff_ref, vf_ref, lb_ref, of_ref, *fwd_scratch, rows=rows)
    b1a, b1b, b2, b3 = _hgrn_direction(True, qb_ref, fb_ref, vb_ref, lb_ref, ob_ref, *bwd_scratch, rows=rows)
    n_sub = rows // HGRN_CHUNK

    def stage1_both(j):
        fa = f1a(j)
        ba = b1a(j)
        f1b(*fa)
        b1b(*ba)

    zero_part = zero_ref.shape[0] // n_sub

    def store_zeros(j):
        zero_ref[pl.ds(pl.multiple_of(j * zero_part, SUBLANES), zero_part), :] = jnp.zeros(
            (zero_part, zero_ref.shape[1]), zero_ref.dtype)

    stage1_both(0)

    def pipelined(j, carry):
        fo = f2(j)
        bo = b2(j)
        store_zeros(j)
        stage1_both(j + 1)
        f3(*fo)
        b3(*bo)
        return carry

    lax.fori_loop(0, n_sub - 1, pipelined, 0)
    fo = f2(n_sub - 1)
    bo = b2(n_sub - 1)
    store_zeros(n_sub - 1)
    f3(*fo)
    b3(*bo)


def _hgrn(proj, lb, zero_rows, *, d_lru, d_hgrn, rows=512):
    bsz, seq, _ = proj.shape
    n_chunks = seq // rows
    zero_blk = zero_rows // (bsz * n_chunks)
    assert zero_blk * bsz * n_chunks == zero_rows and zero_blk % (SUBLANES * (rows // HGRN_CHUNK)) == 0
    assert 2 * (d_hgrn // HGRN_HEADS) == LANES and HGRN_SUB == SUBLANES and HGRN_CHUNK == HGRN_SUB * SUBLANES
    col0 = (2 * d_lru) // d_hgrn
    n_pairs = d_hgrn // LANES
    n_lvl = (HGRN_CHUNK // HGRN_SUB).bit_length() - 1
    direction_scratch = [
        pltpu.VMEM((n_pairs, LANES, LANES), F32),
        pltpu.VMEM((n_pairs, HGRN_CHUNK, LANES), F32),
        pltpu.VMEM((n_pairs * n_lvl, HGRN_CHUNK, LANES), F32),
        pltpu.VMEM((n_pairs, LANES, LANES), F32),
        pltpu.VMEM((HGRN_CHUNK, d_hgrn), BF16),
        pltpu.VMEM((1, d_hgrn), F32),
    ]
    assert len(direction_scratch) == N_HGRN_SCRATCH
    n_rings = 6
    rings = [pltpu.VMEM((IN_RING, 1, rows, d_hgrn), F32)] * n_rings + [pltpu.SemaphoreType.DMA((IN_RING, n_rings))]
    kern = functools.partial(_hgrn_kernel, rows=rows, n_chunks=n_chunks, n_steps=bsz * n_chunks, col0=col0,
                             d_hgrn=d_hgrn)
    out = jax.ShapeDtypeStruct((bsz, seq, d_hgrn), F32)
    return pl.pallas_call(
        kern,
        out_shape=(out, out, jax.ShapeDtypeStruct((zero_rows, LANES), F32)),
        grid=(bsz, n_chunks),
        in_specs=[pl.BlockSpec(memory_space=pl.ANY), pl.BlockSpec((1, d_hgrn), lambda b, c: (0, 0))],
        out_specs=(pl.BlockSpec((1, rows, d_hgrn), lambda b, c: (b, c, 0)),
                   pl.BlockSpec((1, rows, d_hgrn), lambda b, c: (b, n_chunks - 1 - c, 0)),
                   pl.BlockSpec((zero_blk, LANES), lambda b, c: (b * n_chunks + c, 0))),
        scratch_shapes=direction_scratch + direction_scratch + rings,
        compiler_params=_params(("arbitrary", "arbitrary")),
        name="hgrn2",
    )(proj, lb.reshape(1, d_hgrn))


def _interleave(*streams):
    done = object()
    live = list(streams)
    while live:
        live = [s for s in live if next(s, done) is not done]


def _gelu_tanh(y):
    return 0.5 * y * (1.0 + jnp.tanh(0.7978845608028654 * (y + 0.044715 * (y * y * y))))


def _post_kernel(lru_hbm, proj_hbm, of_hbm, ob_hbm, x_hbm, nlw_ref, nhw_ref, wo_ref,
                 gm_ref, nfw_ref, scf_ref, shf_ref, wr_ref, br_ref,
                 xo_ref, h_ref, slab_ref, cnt_ref, route_ref, carry_ref, logits_s,
                 r_lru, r_y, r_of, r_ob, r_g, r_x, in_sem, *, tm, d_lru, tiles, n_tiles, y_col, g_col):
    step = pl.program_id(0)
    d_hgrn = r_of.shape[-1]

    @pl.when(step == 0)
    def _():
        carry_ref[...] = jnp.zeros_like(carry_ref)
        logits_s[...] = jnp.zeros_like(logits_s)

    def tile_copies(t, ring_slot):
        b, r0 = t // tiles, pl.multiple_of((t % tiles) * tm, tm)

        def cp(hbm, col, width, ring):
            return pltpu.make_async_copy(hbm.at[b, pl.ds(r0, tm), pl.ds(col, width)], ring.at[ring_slot],
                                         in_sem.at[ring_slot])

        return (cp(lru_hbm, 0, d_lru, r_lru), cp(proj_hbm, y_col * d_lru, d_lru, r_y),
                cp(of_hbm, 0, d_hgrn, r_of), cp(ob_hbm, 0, d_hgrn, r_ob),
                cp(proj_hbm, g_col * d_hgrn, d_hgrn, r_g), cp(x_hbm, 0, x_hbm.shape[-1], r_x))

    @pl.when(step == 0)
    def _():
        for t in range(min(IN_RING - 1, n_tiles)):
            for c in tile_copies(t, t):
                c.start()

    ahead = step + IN_RING - 1

    @pl.when(ahead < n_tiles)
    def _():
        for c in tile_copies(ahead, ahead % IN_RING):
            c.start()

    @pl.when(step < n_tiles)
    def _():
        for c in tile_copies(step, step % IN_RING):
            c.wait()

    slot = jnp.minimum(step, n_tiles - 1) % IN_RING
    lru_ref, y_ref, of_ref, ob_ref, g_ref, x_ref = (r.at[slot] for r in (r_lru, r_y, r_of, r_ob, r_g, r_x))

    def mixer():
        lru = lru_ref[...] * _gelu_tanh(y_ref[...])
        ms = jnp.mean(lru * lru, axis=-1, keepdims=True)
        yield
        lru = lru * lax.rsqrt(ms + NORM_EPS) * nlw_ref[...]

        hg = of_ref[...] + ob_ref[...]
        width = hg.shape[1]
        hd = width // HGRN_HEADS
        hd_shift = hd.bit_length() - 1
        er = lax.broadcasted_iota(jnp.int32, (width, width), 0) >> hd_shift
        ec = lax.broadcasted_iota(jnp.int32, (width, width), 1) >> hd_shift
        head_sum = jnp.where(er == ec, 1.0, 0.0).astype(BF16)
        sq = hg * hg
        sq_hi = sq.astype(BF16)
        sq_lo = (sq - sq_hi.astype(F32)).astype(BF16)
        ms_h = (_dot(sq_hi, head_sum) + _dot(sq_lo, head_sum)) * (1.0 / hd)
        yield
        g = g_ref[...]
        hg = (hg * lax.rsqrt(ms_h + NORM_EPS) * nhw_ref[...]) * (g * _sigmoid(g))
        yield
        mixed = _dot(lru.astype(BF16), wo_ref[0:d_lru, :])
        yield
        mixed = mixed + _dot(hg.astype(BF16), wo_ref[d_lru:, :])
        yield
        x_new = x_ref[...] + gm_ref[0] * mixed
        xo_ref[0] = x_new
        ms_f = jnp.mean(x_new * x_new, axis=-1, keepdims=True)
        yield
        h = (x_new * lax.rsqrt(ms_f + NORM_EPS) * nfw_ref[...]) * (1.0 + scf_ref[0]) + shf_ref[0]
        _tiles_store(h_ref, h, tm, lead=(0,))
        yield
        logits_s[...] = _dot(h.astype(BF16), wr_ref[...]) + br_ref[...]

    def routing():
        yield from _routing_steps(logits_s[...], jnp.where(step > 0, 1.0, 0.0), slab_ref, cnt_ref, route_ref,
                                  carry_ref, tm)

    _interleave(routing(), mixer())


def _routing_steps(logits, live, slab_ref, cnt_ref, route_ref, carry_ref, tm):
    lane = lax.broadcasted_iota(jnp.int32, (tm, ROUTE_LANES), 1)
    lane_f = lane.astype(F32)
    far = float(ROUTE_LANES)
    is_g = lane < N_GROUPS
    gl = jnp.where(is_g, logits, NEG_BIG)
    gmax = jnp.max(gl, axis=-1, keepdims=True)
    yield
    g_idx = jnp.min(jnp.where(gl == gmax, lane_f, far), axis=-1, keepdims=True)
    p_group = 1.0 / jnp.sum(jnp.where(is_g, jnp.exp(gl - gmax), 0.0), axis=-1, keepdims=True)
    yield
    e_lane = lane - N_GROUPS
    in_group = (e_lane >= 0) & (e_lane < N_EXPERTS) & ((e_lane >> (EXPERTS_PER_GROUP.bit_length() - 1)).astype(F32) == g_idx)
    ev = jnp.where(in_group, logits, NEG_BIG)
    top1 = jnp.max(ev, axis=-1, keepdims=True)
    yield
    i1 = jnp.min(jnp.where(in_group & (ev == top1), lane_f, far), axis=-1, keepdims=True)
    yield
    rest = in_group & (lane_f != i1)
    ev2 = jnp.where(rest, logits, NEG_BIG)
    top2 = jnp.max(ev2, axis=-1, keepdims=True)
    yield
    i2 = jnp.min(jnp.where(rest & (ev2 == top2), lane_f, far), axis=-1, keepdims=True)
    yield
    e1 = i1 - float(N_GROUPS)
    e2 = i2 - float(N_GROUPS)
    ex = jnp.exp(top2 - top1)
    w1 = p_group / (1.0 + ex)
    w2 = p_group * ex / (1.0 + ex)

    sel1 = lane_f == e1
    sel2 = lane_f == e2
    onehot = jnp.where(sel1 | sel2, live, 0.0)
    tr = lax.broadcasted_iota(jnp.int32, (tm, tm), 0)
    tc = lax.broadcasted_iota(jnp.int32, (tm, tm), 1)
    before = jnp.where(tc < tr, 1.0, 0.0).astype(BF16)
    cnt = _dot(before, onehot.astype(BF16)) + carry_ref[0:1]
    yield
    rank1 = jnp.sum(jnp.where(sel1, cnt, 0.0), axis=-1, keepdims=True)
    rank2 = jnp.sum(jnp.where(sel2, cnt, 0.0), axis=-1, keepdims=True)
    total = carry_ref[0:1] + jnp.sum(onehot, axis=0, keepdims=True)
    carry_ref[...] = jnp.broadcast_to(total, carry_ref.shape)
    cnt_ref[...] = jnp.broadcast_to(total, cnt_ref.shape)
    yield

    slab = jnp.where(lane == 0, e1, 0.0)
    slab = jnp.where(lane == 1, e2, slab)
    slab = jnp.where(lane == 2, w1, slab)
    slab = jnp.where(lane == 3, w2, slab)
    slab = jnp.where(lane == 4, rank1, slab)
    slab = jnp.where(lane == 5, rank2, slab)
    slab_ref[0] = slab
    route_ref[...] = slab.T[0:SUBLANES]


def _post_mixer(lru_sum, proj, hg_f, hg_b, x, nlw, nhw, wo_bf16, g_mix, nfw, sc_ffn, sh_ffn, wr_bf16, br,
                *, tm=512):
    bsz, seq, d = x.shape
    d_lru = lru_sum.shape[-1]
    d_hgrn = hg_f.shape[-1]
    y_col = 1
    g_col = (2 * d_lru) // d_hgrn + 4
    tiles = seq // tm
    n_tiles = bsz * tiles
    cur = lambda s: jnp.minimum(s, n_tiles - 1)
    prev = lambda s: jnp.maximum(s - 1, 0)
    row = lambda w, col=0: pl.BlockSpec((1, tm, w), lambda s: (cur(s) // tiles, cur(s) % tiles, col))
    vec = lambda w: pl.BlockSpec((1, w), lambda s: (0, 0))
    per_b = lambda: pl.BlockSpec((1, 1, d), lambda s: (cur(s) // tiles, 0, 0))
    hbm = lambda: pl.BlockSpec(memory_space=pl.ANY)
    ring = lambda w: pltpu.VMEM((IN_RING, tm, w), F32)
    kern = functools.partial(_post_kernel, tm=tm, d_lru=d_lru, tiles=tiles, n_tiles=n_tiles, y_col=y_col,
                             g_col=g_col)
    return pl.pallas_call(
        kern,
        out_shape=(
            jax.ShapeDtypeStruct((bsz, seq, d), F32),
            jax.ShapeDtypeStruct((bsz, seq * SUBLANES, LANES), F32),
            jax.ShapeDtypeStruct((bsz, seq, ROUTE_LANES), F32),
            jax.ShapeDtypeStruct((SUBLANES, ROUTE_LANES), F32),
            jax.ShapeDtypeStruct((SUBLANES, bsz * seq), F32),
        ),
        grid=(n_tiles + 1,),
        in_specs=[
            hbm(), hbm(), hbm(), hbm(), hbm(),
            vec(d_lru), vec(d_hgrn),
            pl.BlockSpec((d, d), lambda s: (0, 0)),
            per_b(), vec(d), per_b(), per_b(),
            pl.BlockSpec((d, ROUTE_LANES), lambda s: (0, 0)),
            vec(ROUTE_LANES),
        ],
        out_specs=(
            row(d),
            pl.BlockSpec((1, tm * SUBLANES, LANES), lambda s: (cur(s) // tiles, cur(s) % tiles, 0)),
            pl.BlockSpec((1, tm, ROUTE_LANES), lambda s: (prev(s) // tiles, prev(s) % tiles, 0)),
            pl.BlockSpec((SUBLANES, ROUTE_LANES), lambda s: (0, 0)),
            pl.BlockSpec((SUBLANES, tm), lambda s: (0, prev(s))),
        ),
        scratch_shapes=[pltpu.VMEM((SUBLANES, ROUTE_LANES), F32), pltpu.VMEM((tm, ROUTE_LANES), F32),
                        ring(d_lru), ring(d_lru), ring(d_hgrn), ring(d_hgrn), ring(d_hgrn), ring(d),
                        pltpu.SemaphoreType.DMA((IN_RING,))],
        compiler_params=_params(("arbitrary",)),
        name="post_mixer_router",
    )(lru_sum, proj, hg_f, hg_b, x, nlw.reshape(1, d_lru), nhw.reshape(1, d_hgrn), wo_bf16,
      g_mix, nfw.reshape(1, d), sc_ffn, sh_ffn, wr_bf16, br)


def _tiles_load(ref, n, lead=()):
    return jnp.concatenate(
        [ref[(*lead, pl.ds(j, n, stride=SUBLANES), slice(None))] for j in range(SUBLANES)], axis=1)


def _tiles_store(ref, val, n, lead=()):
    for j in range(SUBLANES):
        ref[(*lead, pl.ds(j, n, stride=SUBLANES), slice(None))] = val[:, j * LANES:(j + 1) * LANES]


def _token_tile(ref, t):
    return ref.at[pl.ds(pl.multiple_of(t * SUBLANES, SUBLANES), SUBLANES)]


def _dest_kernel(start_ref, route_ref, o_ref):
    route = route_ref[...].astype(jnp.int32)
    start = jnp.zeros_like(route)
    for e in range(N_EXPERTS):
        start = jnp.where(route == e, start_ref[e], start)
    o_ref[...] = start + pltpu.roll(route, SUBLANES // 2, 0)


def _dest_rows(expert_start, route):
    return pl.pallas_call(
        _dest_kernel,
        out_shape=jax.ShapeDtypeStruct(route.shape, jnp.int32),
        grid_spec=pltpu.PrefetchScalarGridSpec(
            num_scalar_prefetch=1,
            grid=(1,),
            in_specs=[pl.BlockSpec(route.shape, lambda i, s: (0, 0))],
            out_specs=pl.BlockSpec(route.shape, lambda i, s: (0, 0)),
        ),
        compiler_params=pltpu.CompilerParams(dimension_semantics=("arbitrary",)),
        name="moe_dest_rows",
    )(expert_start, route)


def _dispatch_kernel(d1_ref, d2_ref, h_ref, z_ref, o_ref, sem, *, tb):
    del z_ref
    base = pl.program_id(0) * tb

    def issue(r, carry):
        t = base + r
        pltpu.make_async_copy(_token_tile(h_ref, r), _token_tile(o_ref, d1_ref[t]), sem).start(priority=0)
        pltpu.make_async_copy(_token_tile(h_ref, r), _token_tile(o_ref, d2_ref[t]), sem).start(priority=1)
        return carry

    lax.fori_loop(0, tb, issue, 0, unroll=DMA_ISSUE_UNROLL)
    for _ in range(2):
        pltpu.make_async_copy(h_ref, o_ref.at[pl.ds(0, tb * SUBLANES)], sem).wait()


def _dispatch(dest1, dest2, h_tiles, zero_tiles, *, tb=512):
    m = h_tiles.shape[0] // SUBLANES
    n_rows = zero_tiles.shape[0] // SUBLANES
    kern = functools.partial(_dispatch_kernel, tb=tb)
    return pl.pallas_call(
        kern,
        out_shape=jax.ShapeDtypeStruct((n_rows * SUBLANES, LANES), h_tiles.dtype),
        grid_spec=pltpu.PrefetchScalarGridSpec(
            num_scalar_prefetch=2,
            grid=(m // tb,),
            in_specs=[pl.BlockSpec((tb * SUBLANES, LANES), lambda i, d1, d2: (i, 0)),
                      pl.BlockSpec(memory_space=pl.ANY)],
            out_specs=pl.BlockSpec(memory_space=pl.ANY),
            scratch_shapes=[pltpu.SemaphoreType.DMA(())],
        ),
        input_output_aliases={3: 0},
        compiler_params=pltpu.CompilerParams(dimension_semantics=("arbitrary",), has_side_effects=True),
        name="moe_dispatch",
    )(dest1, dest2, h_tiles, zero_tiles)


def _expert_kernel(plan_ref, x_hbm, wg_hbm, wu_hbm, wd_hbm, o_ref, wg_f, wu_f, wd_f, wg_s, wu_s, wd_s, sem,
                   x_ring, x_sem, *, layer, blk):
    i = pl.program_id(0)
    n_blocks = pl.num_programs(0)
    n_used = plan_ref[n_blocks]
    expert = plan_ref[i]
    next_expert = plan_ref[n_blocks + 1 + i]
    slot = plan_ref[2 * n_blocks + 1 + i]
    first_block = ((i == 0) | (plan_ref[jnp.maximum(i - 1, 0)] != expert)) & (i < n_used)

    def copies(e, s):
        return (pltpu.make_async_copy(wg_hbm.at[layer, e], wg_f.at[s], sem.at[s]),
                pltpu.make_async_copy(wu_hbm.at[layer, e], wu_f.at[s], sem.at[s]),
                pltpu.make_async_copy(wd_hbm.at[layer, e], wd_f.at[s], sem.at[s]))

    @pl.when(i == 0)
    def _():
        for c in copies(expert, slot):
            c.start()

    @pl.when(first_block)
    def _():
        for c in copies(expert, slot):
            c.wait()
        wg_s[...] = wg_f[slot].astype(BF16)
        wu_s[...] = wu_f[slot].astype(BF16)
        wd_s[...] = wd_f[slot].astype(BF16)

        @pl.when(next_expert >= 0)
        def _():
            for c in copies(next_expert, 1 - slot):
                c.start()

    def x_copy(block, ring_slot):
        rows = blk * SUBLANES
        return pltpu.make_async_copy(x_hbm.at[pl.ds(pl.multiple_of(block * rows, rows), rows)],
                                     x_ring.at[ring_slot], x_sem.at[ring_slot])

    @pl.when(i == 0)
    def _():
        for b in range(X_RING - 1):
            @pl.when(b < n_used)
            def _():
                x_copy(b, b).start()

    ahead = i + X_RING - 1

    @pl.when(ahead < n_used)
    def _():
        x_copy(ahead, ahead % X_RING).start()

    for ring_slot in range(X_RING):
        @pl.when((i < n_used) & (i % X_RING == ring_slot))
        def _():
            x_copy(i, ring_slot).wait()
            x = _tiles_load(x_ring, blk, lead=(ring_slot,)).astype(BF16)
            gate = _dot(x, wg_s[...])
            up = _dot(x, wu_s[...])
            act = (gate * _sigmoid(gate)) * up
            _tiles_store(o_ref, _dot(act.astype(BF16), wd_s[...]), blk)

    @pl.when(i >= n_used)
    def _():
        o_ref[...] = jnp.zeros_like(o_ref)


def _expert_plan(blk_expert, blocks_used):
    n_blocks = blk_expert.shape[0]
    idx = jnp.arange(n_blocks, dtype=jnp.int32)
    change = jnp.concatenate([jnp.ones((1,), bool), blk_expert[1:] != blk_expert[:-1]])
    slot = (jnp.cumsum(change.astype(jnp.int32)) - 1) & 1
    change_at = jnp.where(change, idx, n_blocks)
    from_here = lax.cummin(change_at[::-1])[::-1]
    next_change = jnp.concatenate([from_here[1:], jnp.full((1,), n_blocks, jnp.int32)])
    next_expert = jnp.where(next_change < blocks_used, blk_expert[jnp.minimum(next_change, n_blocks - 1)], -1)
    return jnp.concatenate([blk_expert, blocks_used.reshape(1), next_expert, slot]).astype(jnp.int32)


def _experts(plan, x_tiles, wg, wu, wd, layer):
    n_rows = x_tiles.shape[0] // SUBLANES
    d, de = wg.shape[-2:]
    blk = EXPERT_BLOCK
    n_blocks = n_rows // blk
    kern = functools.partial(_expert_kernel, layer=layer, blk=blk)
    return pl.pallas_call(
        kern,
        out_shape=jax.ShapeDtypeStruct((n_rows * SUBLANES, LANES), F32),
        grid_spec=pltpu.PrefetchScalarGridSpec(
            num_scalar_prefetch=1,
            grid=(n_blocks,),
            in_specs=[
                pl.BlockSpec(memory_space=pl.ANY),
                pl.BlockSpec(memory_space=pl.ANY),
                pl.BlockSpec(memory_space=pl.ANY),
                pl.BlockSpec(memory_space=pl.ANY),
            ],
            out_specs=pl.BlockSpec((blk * SUBLANES, LANES), lambda i, plan: (i, 0)),
            scratch_shapes=[
                pltpu.VMEM((2, d, de), F32), pltpu.VMEM((2, d, de), F32), pltpu.VMEM((2, de, d), F32),
                pltpu.VMEM((d, de), BF16), pltpu.VMEM((d, de), BF16), pltpu.VMEM((de, d), BF16),
                pltpu.SemaphoreType.DMA((2,)),
                pltpu.VMEM((X_RING, blk * SUBLANES, LANES), F32), pltpu.SemaphoreType.DMA((X_RING,)),
            ],
        ),
        compiler_params=_params(("arbitrary",)),
        name="moe_experts",
    )(plan, x_tiles, wg, wu, wd)


def _combine_kernel(d1_ref, d2_ref, y_ref, slab_ref, x_ref, g_ref, nw_ref, o_ref, ra0, rb0, ra1, rb1, sem,
                    *, tm, tiles, n_steps, final_norm):
    step = pl.program_id(0) * tiles + pl.program_id(1)
    bufs = ((ra0, rb0), (ra1, rb1))

    def gather(tile, slot):
        base = tile * tm
        r1_ref, r2_ref = bufs[slot]

        def issue(r, carry):
            t = base + r
            pltpu.make_async_copy(_token_tile(y_ref, d1_ref[t]), _token_tile(r1_ref, r),
                                  sem.at[slot]).start(priority=0)
            pltpu.make_async_copy(_token_tile(y_ref, d2_ref[t]), _token_tile(r2_ref, r),
                                  sem.at[slot]).start(priority=1)
            return carry

        lax.fori_loop(0, tm, issue, 0, unroll=DMA_ISSUE_UNROLL)

    @pl.when(step == 0)
    def _():
        gather(0, 0)

    for slot in range(2):
        @pl.when((step & 1) == slot)
        def _():
            @pl.when(step + 1 < n_steps)
            def _():
                gather(step + 1, 1 - slot)

            r1_ref, r2_ref = bufs[slot]
            pltpu.make_async_copy(y_ref.at[pl.ds(0, tm * SUBLANES)], r1_ref, sem.at[slot]).wait()
            pltpu.make_async_copy(y_ref.at[pl.ds(0, tm * SUBLANES)], r2_ref, sem.at[slot]).wait()
            slab = slab_ref[0]
            y = slab[:, 2:3] * _tiles_load(r1_ref, tm) + slab[:, 3:4] * _tiles_load(r2_ref, tm)
            out = x_ref[0] + g_ref[0] * y
            if final_norm:
                ms = jnp.mean(out * out, axis=-1, keepdims=True)
                out = out * lax.rsqrt(ms + NORM_EPS) * nw_ref[...]
            o_ref[0] = out


def _combine(dest1, dest2, y_buf, slab, x, g_ffn, norm_w, *, final_norm, tm=512):
    bsz, seq, d = x.shape
    tiles = seq // tm
    kern = functools.partial(_combine_kernel, tm=tm, tiles=tiles, n_steps=bsz * tiles, final_norm=final_norm)
    row_buf = pltpu.VMEM((tm * SUBLANES, LANES), F32)
    return pl.pallas_call(
        kern,
        out_shape=jax.ShapeDtypeStruct((bsz, seq, d), F32),
        grid_spec=pltpu.PrefetchScalarGridSpec(
            num_scalar_prefetch=2,
            grid=(bsz, tiles),
            in_specs=[
                pl.BlockSpec(memory_space=pl.ANY),
                pl.BlockSpec((1, tm, ROUTE_LANES), lambda b, i, d1, d2: (b, i, 0)),
                pl.BlockSpec((1, tm, d), lambda b, i, d1, d2: (b, i, 0)),
                pl.BlockSpec((1, 1, d), lambda b, i, d1, d2: (b, 0, 0)),
                pl.BlockSpec((1, d), lambda b, i, d1, d2: (0, 0)),
            ],
            out_specs=pl.BlockSpec((1, tm, d), lambda b, i, d1, d2: (b, i, 0)),
            scratch_shapes=[row_buf, row_buf, row_buf, row_buf, pltpu.SemaphoreType.DMA((2,))],
        ),
        compiler_params=_params(("arbitrary", "arbitrary")),
        name="moe_combine",
    )(dest1, dest2, y_buf, slab, x, g_ffn, norm_w.reshape(1, d))


def _block_diag(w):
    heads, hd, _ = w.shape
    n = heads * hd
    tiled = jnp.tile(w.reshape(n, hd), (1, heads))
    blk_r = lax.broadcasted_iota(jnp.int32, (n, n), 0) // hd
    blk_c = lax.broadcasted_iota(jnp.int32, (n, n), 1) // hd
    return jnp.where(blk_r == blk_c, tiled, 0.0)


def kernel(x, c, ada_w, ada_b, norm_mix_w, w_in, conv_w, conv_b, lru_wa, lru_ba, lru_wx, lru_bx, lru_lambda, norm_lru_w, hgrn_lb, norm_hgrn_w, w_out, norm_ffn_w, router_group_w, router_group_b, router_expert_w, router_expert_b, expert_w_gate, expert_w_up, expert_w_down, final_norm_w):
    bsz, seq, d = x.shape
    assert d == SUBLANES * LANES, "the MoE row movement keeps one (8, 128) tile per token"
    depth = ada_w.shape[0]
    d_lru = conv_w.shape[-1]
    d_hgrn = hgrn_lb.shape[-1]
    m = bsz * seq
    n_rows = m * 2 + N_EXPERTS * EXPERT_BLOCK
    n_blocks = n_rows // EXPERT_BLOCK

    mod = _modulation(c, ada_w, ada_b)
    lb_cum = jnp.cumsum(jax.nn.softmax(hgrn_lb.astype(F32), axis=0), axis=0)
    lb_all = lb_cum - lb_cum[0:1]

    for l in range(depth):
        sh_mix, sc_mix, g_mix, sh_ffn, sc_ffn, g_ffn = [
            mod[l, :, i * d:(i + 1) * d].reshape(bsz, 1, d) for i in range(N_MODULATIONS)]
        wa_bd = jnp.stack([_block_diag(lru_wa[l, 0]), _block_diag(lru_wa[l, 1])]).astype(BF16)
        wx_bd = jnp.stack([_block_diag(lru_wx[l, 0]), _block_diag(lru_wx[l, 1])]).astype(BF16)
        lru_w = (conv_w[l], conv_b[l], wa_bd, lru_ba[l], wx_bd, lru_bx[l], lru_lambda[l])
        proj, lru_fwd = _in_proj_lru(x, norm_mix_w[l], sc_mix, sh_mix, w_in, l, *lru_w)
        lru_sum = _lru_scan(proj, *lru_w, reverse=True, add_to=lru_fwd)
        hg_f, hg_b, zero_tiles = _hgrn(proj, lb_all[l], n_rows * SUBLANES, d_lru=d_lru, d_hgrn=d_hgrn)

        lane_pad = ROUTE_LANES - N_GROUPS - N_EXPERTS
        wr = jnp.pad(jnp.concatenate([router_group_w[l], router_expert_w[l]], axis=1), ((0, 0), (0, lane_pad)))
        br = jnp.pad(jnp.concatenate([router_group_b[l], router_expert_b[l]]), (0, lane_pad)).reshape(1, ROUTE_LANES)
        x_mid, h_ffn, slab, counts, route = _post_mixer(
            lru_sum, proj, hg_f, hg_b, x, norm_lru_w[l], norm_hgrn_w[l], w_out[l].astype(BF16), g_mix,
            norm_ffn_w[l], sc_ffn, sh_ffn, wr.astype(BF16), br)

        cnt = counts[0, :N_EXPERTS].astype(jnp.int32)
        padded = ((cnt + EXPERT_BLOCK - 1) // EXPERT_BLOCK) * EXPERT_BLOCK
        pend = jnp.cumsum(padded)
        pstart = pend - padded
        blk_start = jnp.arange(n_blocks, dtype=jnp.int32) * EXPERT_BLOCK
        blk_expert = jnp.minimum(jnp.sum(pend[None, :] <= blk_start[:, None], axis=1), N_EXPERTS - 1)
        plan = _expert_plan(blk_expert.astype(jnp.int32), (pend[N_EXPERTS - 1] // EXPERT_BLOCK).astype(jnp.int32))
        dest = _dest_rows(pstart.astype(jnp.int32), route)
        dest1, dest2 = dest[0], dest[1]

        x_buf = _dispatch(dest1, dest2, h_ffn.reshape(m * SUBLANES, LANES), zero_tiles)
        y_buf = _experts(plan, x_buf, expert_w_gate, expert_w_up, expert_w_down, l)
        x = _combine(dest1, dest2, y_buf, slab, x_mid, g_ffn, final_norm_w, final_norm=(l == depth - 1))

    return x
```

```python
import functools

import jax
import jax.numpy as jnp
from jax import lax
from jax.experimental import pallas as pl
from jax.experimental.pallas import tpu as pltpu

F32 = jnp.float32
BF16 = jnp.bfloat16

HGRN_HEADS = 8
N_MODULATIONS = 6
CONV_WIDTH = 4
LRU_C = 8.0
N_GROUPS = 4
EXPERTS_PER_GROUP = 8
N_EXPERTS = N_GROUPS * EXPERTS_PER_GROUP
NORM_EPS = 1e-6

LANES = 128
SUBLANES = 8
VMEM_LIMIT = 56 * 1024 * 1024

HGRN_CHUNK = 64
HGRN_SUB = 8
LOG2E = 1.4426950408889634
ROUTE_LANES = LANES
EXPERT_BLOCK = 512
X_RING = 3
IN_RING = 3
DMA_ISSUE_UNROLL = 8
NEG_BIG = -3.0e38


def _params(sem):
    return pltpu.CompilerParams(dimension_semantics=sem, vmem_limit_bytes=VMEM_LIMIT)


def _dot(a, b):
    return jnp.dot(a, b, preferred_element_type=F32)


def _dot_nt(a, b):
    return lax.dot_general(a, b, (((1,), (1,)), ((), ())), preferred_element_type=F32)


def _dot_tn(a, b):
    return lax.dot_general(a, b, (((0,), (0,)), ((), ())), preferred_element_type=F32)


def _dot01_exact(m01, x):
    hi = x.astype(BF16)
    r1 = x - hi.astype(F32)
    mid = r1.astype(BF16)
    lo = (r1 - mid.astype(F32)).astype(BF16)
    return _dot(m01, hi) + _dot(m01, mid) + _dot(m01, lo)


def _sigmoid(x):
    return 1.0 / (1.0 + jnp.exp(-x))


def _sigmoid_tanh(x):
    return 0.5 * jnp.tanh(0.5 * x) + 0.5


def _mod_kernel(c_ref, w_ref, b_ref, o_ref):
    c = c_ref[...]
    cond = c * _sigmoid(c)
    o_ref[0] = _dot(cond.astype(BF16), w_ref[0].astype(BF16)) + b_ref[0]


def _modulation(c, ada_w, ada_b):
    depth, d, n = ada_w.shape
    bsz = c.shape[0]
    rows = -(-bsz // SUBLANES) * SUBLANES
    c_pad = jnp.pad(c, ((0, rows - bsz), (0, 0)))
    tn = n // N_MODULATIONS
    out = pl.pallas_call(
        _mod_kernel,
        out_shape=jax.ShapeDtypeStruct((depth, rows, n), F32),
        grid=(depth, n // tn),
        in_specs=[
            pl.BlockSpec((rows, d), lambda l, j: (0, 0)),
            pl.BlockSpec((1, d, tn), lambda l, j: (l, 0, j)),
            pl.BlockSpec((1, 1, tn), lambda l, j: (l, 0, j)),
        ],
        out_specs=pl.BlockSpec((1, rows, tn), lambda l, j: (l, 0, j)),
        compiler_params=_params(("arbitrary", "arbitrary")),
        name="adaln_mod",
    )(c_pad, ada_w, ada_b.reshape(depth, 1, n))
    return out[:, :bsz]


def _rms_mod(x, nw, sc, sh):
    ms = jnp.mean(x * x, axis=-1, keepdims=True)
    return (x * lax.rsqrt(ms + NORM_EPS) * nw) * (1.0 + sc) + sh


def _inproj_lru_kernel(x_ref, nw_ref, sc_ref, sh_ref, w_ref, cw_ref, cb_ref, wa_ref, ba_ref, wx_ref, bx_ref,
                       lam_ref, proj_ref, lru_ref, w_s, xl_s, xnew_s, carry_ref, sa_ref, sb_ref, cin_ref,
                       *, tm, tiles, d_lru):
    s = pl.program_id(0)
    n_cols = proj_ref.shape[-1]

    @pl.when(s == 0)
    def _():
        w_s[...] = w_ref[0].astype(BF16)
        xl_s[...] = jnp.zeros_like(xl_s)
        carry_ref[...] = jnp.zeros_like(carry_ref)

    prev = jnp.maximum(s - 1, 0)
    chunk = prev % tiles
    has_prev = jnp.where(chunk > 0, 1.0, 0.0)
    has_next = jnp.where(chunk < tiles - 1, 1.0, 0.0)
    hb = _rms_mod(x_ref[0], nw_ref[...], sc_ref[0], sh_ref[0]).astype(BF16)

    def project():
        step = 2 * LANES
        for c0 in range(0, n_cols, step):
            block = _dot(hb, w_s[:, c0:c0 + step])
            proj_ref[0, :, c0:c0 + step] = block
            if c0 < d_lru:
                xnew_s[:, c0:c0 + step] = block
            if c0 + step >= d_lru:
                yield

    def scan():
        yield

        def store(rs, h_rows):
            lru_ref[0, rs, :] = h_rows

        yield from _lru_steps(xl_s[SUBLANES:, :], xl_s[0:SUBLANES, :] * has_prev,
                              xnew_s[0:SUBLANES, :] * has_next, carry_ref[...] * has_prev,
                              cw_ref[...], cb_ref[...], wa_ref[0], ba_ref[0], wx_ref[0], bx_ref[0], lam_ref[0],
                              store, carry_ref, sa_ref, sb_ref, cin_ref, reverse=False)

    _interleave(project(), scan())
    xl_s[0:SUBLANES, :] = xl_s[tm:tm + SUBLANES, :]
    xl_s[SUBLANES:, :] = xnew_s[...]


def _in_proj_lru(x, nw, sc, sh, w_in, layer, conv_w, conv_b, wa_bd, ba, wx_bd, bx, lam, tm=512):
    bsz, seq, d = x.shape
    n = w_in.shape[-1]
    d_lru = conv_w.shape[1]
    tiles = seq // tm
    n_tiles = bsz * tiles
    cur = lambda s: jnp.minimum(s, n_tiles - 1)
    prev = lambda s: jnp.maximum(s - 1, 0)
    vec = lambda: pl.BlockSpec((1, 1, d_lru), lambda s: (0, 0, 0))
    mat = lambda: pl.BlockSpec((1, d_lru, d_lru), lambda s: (0, 0, 0))
    kern = functools.partial(_inproj_lru_kernel, tm=tm, tiles=tiles, d_lru=d_lru)
    return pl.pallas_call(
        kern,
        out_shape=(jax.ShapeDtypeStruct((bsz, seq, n), F32), jax.ShapeDtypeStruct((bsz, seq, d_lru), F32)),
        grid=(n_tiles + 1,),
        in_specs=[
            pl.BlockSpec((1, tm, d), lambda s: (cur(s) // tiles, cur(s) % tiles, 0)),
            pl.BlockSpec((1, d), lambda s: (0, 0)),
            pl.BlockSpec((1, 1, d), lambda s: (cur(s) // tiles, 0, 0)),
            pl.BlockSpec((1, 1, d), lambda s: (cur(s) // tiles, 0, 0)),
            pl.BlockSpec((1, d, n), lambda s: (layer, 0, 0), pipeline_mode=pl.Buffered(1)),
            pl.BlockSpec((CONV_WIDTH, d_lru), lambda s: (0, 0)),
            pl.BlockSpec((1, d_lru), lambda s: (0, 0)),
            mat(), vec(), mat(), vec(), vec(),
        ],
        out_specs=(pl.BlockSpec((1, tm, n), lambda s: (cur(s) // tiles, cur(s) % tiles, 0)),
                   pl.BlockSpec((1, tm, d_lru), lambda s: (prev(s) // tiles, prev(s) % tiles, 0))),
        scratch_shapes=[
            pltpu.VMEM((d, n), BF16),
            pltpu.VMEM((tm + SUBLANES, d_lru), F32),
            pltpu.VMEM((tm, d_lru), F32),
            pltpu.VMEM((1, d_lru), F32),
            pltpu.VMEM((d_lru // LANES, tm, LANES), F32), pltpu.VMEM((d_lru // LANES, tm, LANES), F32),
            pltpu.VMEM((tm // SUBLANES, d_lru), F32),
        ],
        compiler_params=_params(("arbitrary",)),
        name="in_proj_lru_fwd",
    )(x, nw.reshape(1, d), sc, sh, w_in, conv_w, conv_b.reshape(1, d_lru), wa_bd, ba.reshape(2, 1, d_lru),
      wx_bd, bx.reshape(2, 1, d_lru), lam.reshape(2, 1, d_lru))


def _lru_kernel(x_ref, xp_ref, xn_ref, cw_ref, cb_ref, wa_ref, ba_ref, wx_ref, bx_ref, lam_ref, *rest,
                reverse, n_chunks, rows, accumulate):
    add_ref = rest[0] if accumulate else None
    o_ref, carry_ref, sa_ref, sb_ref, cin_ref = rest[1:] if accumulate else rest
    c = pl.program_id(1)
    chunk = (n_chunks - 1 - c) if reverse else c

    @pl.when(c == 0)
    def _():
        carry_ref[...] = jnp.zeros_like(carry_ref)

    has_prev = jnp.where(chunk > 0, 1.0, 0.0)
    has_next = jnp.where(chunk < n_chunks - 1, 1.0, 0.0)

    def store(rs, h_rows):
        o_ref[0, rs, :] = (add_ref[0, rs, :] + h_rows) if accumulate else h_rows

    _interleave(_lru_steps(x_ref[0], xp_ref[0] * has_prev, xn_ref[0] * has_next, carry_ref[...],
                           cw_ref[...], cb_ref[...], wa_ref[0], ba_ref[0], wx_ref[0], bx_ref[0], lam_ref[0],
                           store, carry_ref, sa_ref, sb_ref, cin_ref, reverse=reverse))


def _lru_steps(x, xp, xn, carry, cw, cb, wa, ba, wx, bx, lam, store, carry_ref, sa_ref, sb_ref, cin_ref, *,
               reverse):
    rows, width = x.shape
    xe = jnp.concatenate([xp, x, xn], axis=0)
    xc = cb
    for k in range(CONV_WIDTH):
        lo = SUBLANES + k - CONV_WIDTH // 2
        xc = xc + cw[k:k + 1] * xe[lo:lo + rows]
    yield

    xcb = xc.astype(BF16)
    r = _sigmoid_tanh(_dot(xcb, wa) + ba)
    yield
    gate_i = _sigmoid_tanh(_dot(xcb, wx) + bx)
    yield
    softplus_neg_lam = jnp.maximum(-lam, 0.0) + jnp.log1p(jnp.exp(-jnp.abs(lam)))
    log_a = (-LRU_C) * r * softplus_neg_lam
    a = jnp.exp(log_a)
    t = jnp.tanh(-log_a)
    u = jnp.sqrt(2.0 * t / (1.0 + t)) * (gate_i * xc)
    yield

    groups = rows // SUBLANES
    acc_a = a.reshape(groups, SUBLANES, width)
    acc_b = u.reshape(groups, SUBLANES, width)
    sub = lax.broadcasted_iota(jnp.int32, (groups, SUBLANES, width), 1)
    s = 1
    while s < SUBLANES:
        if reverse:
            valid = sub < SUBLANES - s
            sh_a, sh_b = pltpu.roll(acc_a, SUBLANES - s, 1), pltpu.roll(acc_b, SUBLANES - s, 1)
        else:
            valid = sub >= s
            sh_a, sh_b = pltpu.roll(acc_a, s, 1), pltpu.roll(acc_b, s, 1)
        acc_b = jnp.where(valid, acc_a * sh_b + acc_b, acc_b)
        acc_a = jnp.where(valid, acc_a * sh_a, acc_a)
        s *= 2
        yield
    acc_a = acc_a.reshape(rows, width)
    acc_b = acc_b.reshape(rows, width)
    edge = 0 if reverse else SUBLANES - 1
    n_tiles = width // LANES
    for j in range(n_tiles):
        sa_ref[j] = acc_a[:, j * LANES:(j + 1) * LANES]
        sb_ref[j] = acc_b[:, j * LANES:(j + 1) * LANES]
    ea = jnp.concatenate([sa_ref[j, pl.ds(edge, groups, stride=SUBLANES), :] for j in range(n_tiles)], axis=1)
    eb = jnp.concatenate([sb_ref[j, pl.ds(edge, groups, stride=SUBLANES), :] for j in range(n_tiles)], axis=1)
    grow = lax.broadcasted_iota(jnp.int32, (groups, width), 0)
    s = 1
    while s < groups:
        if reverse:
            valid = grow < groups - s
            sh_a, sh_b = pltpu.roll(ea, groups - s, 0), pltpu.roll(eb, groups - s, 0)
        else:
            valid = grow >= s
            sh_a, sh_b = pltpu.roll(ea, s, 0), pltpu.roll(eb, s, 0)
        eb = jnp.where(valid, ea * sh_b + eb, eb)
        ea = jnp.where(valid, ea * sh_a, ea)
        s *= 2
    yield
    group_out = eb + ea * carry
    if reverse:
        carry_in = jnp.where(grow == groups - 1, carry, pltpu.roll(group_out, groups - 1, 0))
        carry_ref[...] = group_out[0:1]
    else:
        carry_in = jnp.where(grow == 0, carry, pltpu.roll(group_out, 1, 0))
        carry_ref[...] = group_out[groups - 1:groups]
    cin_ref[...] = carry_in
    for g in range(groups):
        rs = slice(g * SUBLANES, (g + 1) * SUBLANES)
        store(rs, acc_b[rs] + acc_a[rs] * cin_ref[g:g + 1, :])
        if g % (groups // 4) == groups // 4 - 1:
            yield


def _lru_scan(proj, conv_w, conv_b, wa_bd, ba, wx_bd, bx, lam, *, reverse, add_to=None, rows=512):
    bsz, seq, _ = proj.shape
    d_lru = conv_w.shape[1]
    n_chunks = seq // rows
    halo = rows // SUBLANES
    last_halo = seq // SUBLANES - 1
    dirn = 1 if reverse else 0

    def chunk_of(c):
        return (n_chunks - 1 - c) if reverse else c

    vec = lambda: pl.BlockSpec((1, 1, d_lru), lambda b, c: (dirn, 0, 0))
    mat = lambda: pl.BlockSpec((1, d_lru, d_lru), lambda b, c: (dirn, 0, 0))
    accumulate = add_to is not None
    kern = functools.partial(_lru_kernel, reverse=reverse, n_chunks=n_chunks, rows=rows, accumulate=accumulate)
    tile = pl.BlockSpec((1, rows, d_lru), lambda b, c: (b, chunk_of(c), 0))
    return pl.pallas_call(
        kern,
        out_shape=jax.ShapeDtypeStruct((bsz, seq, d_lru), F32),
        grid=(bsz, n_chunks),
        in_specs=[
            pl.BlockSpec((1, rows, d_lru), lambda b, c: (b, chunk_of(c), 0)),
            pl.BlockSpec((1, SUBLANES, d_lru),
                         lambda b, c: (b, jnp.maximum(chunk_of(c) * halo - 1, 0), 0)),
            pl.BlockSpec((1, SUBLANES, d_lru),
                         lambda b, c: (b, jnp.minimum((chunk_of(c) + 1) * halo, last_halo), 0)),
            pl.BlockSpec((CONV_WIDTH, d_lru), lambda b, c: (0, 0)),
            pl.BlockSpec((1, d_lru), lambda b, c: (0, 0)),
            mat(), vec(), mat(), vec(), vec(),
        ] + ([tile] if accumulate else []),
        out_specs=tile,
        scratch_shapes=[pltpu.VMEM((1, d_lru), F32), pltpu.VMEM((d_lru // LANES, rows, LANES), F32),
                        pltpu.VMEM((d_lru // LANES, rows, LANES), F32),
                        pltpu.VMEM((rows // SUBLANES, d_lru), F32)],
        compiler_params=_params(("arbitrary", "arbitrary")),
        name="lru_bwd" if reverse else "lru_fwd",
    )(proj, proj, proj, conv_w, conv_b.reshape(1, d_lru), wa_bd, ba.reshape(2, 1, d_lru),
      wx_bd, bx.reshape(2, 1, d_lru), lam.reshape(2, 1, d_lru), *([add_to] if accumulate else []))


def _hgrn_direction(rev, q_ref, f_ref, v_ref, lb_ref, o_ref, st_ref, diag_s, lvl_s, upd_s, qe_s, btot_s, *, rows):
    ck, sb = HGRN_CHUNK, HGRN_SUB
    n_blk = ck // sb
    sb_shift = sb.bit_length() - 1
    n_sub = rows // ck
    width = q_ref.shape[-1]
    n_pairs = width // LANES
    half = LANES // 2

    def flip(idx, n):
        return (n - 1 - idx) if rev else idx

    n_lvl = n_blk.bit_length() - 1
    tf = flip(lax.broadcasted_iota(jnp.int32, (ck, ck), 0), ck)
    uf = flip(lax.broadcasted_iota(jnp.int32, (ck, ck), 1), ck)
    tb, ub = tf >> sb_shift, uf >> sb_shift
    pb = flip(lax.broadcasted_iota(jnp.int32, (n_blk, ck), 0), n_blk)
    pub = flip(lax.broadcasted_iota(jnp.int32, (n_blk, ck), 1), ck) >> sb_shift
    mats = [jnp.where((tb == ub) & (uf <= tf), 1.0, 0.0),
            jnp.where(pub < pb, 1.0, 0.0)]
    for lvl in range(n_lvl):
        mid = ((pb >> (lvl + 1)) << (lvl + 1)) + (1 << lvl)
        mats.append(jnp.where(pub < mid, 1.0, 0.0))
    mats.append(jnp.ones((SUBLANES, ck), F32))
    m_cum = jnp.concatenate(mats, axis=0).astype(BF16)

    def per_block(rows8):
        return jnp.concatenate(
            [jnp.broadcast_to(rows8[jb:jb + 1], (sb, rows8.shape[1])) for jb in range(n_blk)], axis=0)
    row_blk = flip(lax.broadcasted_iota(jnp.int32, (ck, width), 0), ck) >> sb_shift
    upper = [((row_blk >> lvl) & 1) == 1 for lvl in range(n_lvl)]
    pr = flip(lax.broadcasted_iota(jnp.int32, (ck, LANES), 0), ck) >> sb_shift
    pc = flip(lax.broadcasted_iota(jnp.int32, (ck, LANES), 1) & (ck - 1), ck) >> sb_shift
    group_mask = [(pr >> (lvl + 1)) == (pc >> (lvl + 1)) for lvl in range(n_lvl)]
    lane = lax.broadcasted_iota(jnp.int32, (1, LANES), 1)
    head0 = lane < half

    def split_heads(x):
        xb = x.astype(BF16)
        zero = jnp.zeros_like(xb)
        return jnp.concatenate([jnp.where(head0, xb, zero), jnp.where(head0, zero, xb)], axis=0)

    sr = lax.broadcasted_iota(jnp.int32, (LANES, LANES), 0)
    sc = lax.broadcasted_iota(jnp.int32, (LANES, LANES), 1)
    same_head = (sr < half) == (sc < half)
    er = lax.broadcasted_iota(jnp.int32, (sb * LANES, LANES), 0)
    ec = lax.broadcasted_iota(jnp.int32, (sb * LANES, LANES), 1)
    sel = jnp.where(ec == (((er & (LANES - 1)) >> (half.bit_length() - 1)) * half + (er >> (LANES.bit_length() - 1))),
                    1.0, 0.0).astype(BF16)
    sub_row = flip(lax.broadcasted_iota(jnp.int32, (sb, LANES), 0), sb)
    lbv = lb_ref[...]

    def row_start(j):
        return pl.multiple_of(flip(j, n_sub) * ck, ck)

    def stage1a(j):
        r0 = row_start(j)
        q = q_ref[0, pl.ds(r0, ck), :]
        z = f_ref[0, pl.ds(r0, ck), :]
        v = v_ref[0, pl.ds(r0, ck), :]
        f = lbv + (1.0 - lbv) * _sigmoid(z)
        lf2 = jnp.log(f) * LOG2E
        k = 1.0 - f
        return q, v, k, _dot01_exact(m_cum, lf2)

    def stage1b(q, v, k, cums):
        bl2 = cums[0:ck]
        b2 = bl2 + per_block(cums[ck:ck + n_blk])
        tot_row = ck + (1 + n_lvl) * n_blk
        btot2 = cums[tot_row:tot_row + 1]
        log2_k = jnp.log(k) * LOG2E
        kb = b2 - log2_k
        kbl = bl2 - log2_k
        qe = q * jnp.exp2(b2)
        ke = jnp.exp2(btot2 - kb)
        q_lvl, k_lvl = [], []
        for lvl in range(n_lvl):
            split2 = per_block(cums[ck + (1 + lvl) * n_blk:ck + (2 + lvl) * n_blk])
            q_lvl.append(q * jnp.exp2(jnp.where(upper[lvl], b2 - split2, NEG_BIG)))
            k_lvl.append(jnp.exp2(jnp.where(upper[lvl], NEG_BIG, split2 - kb)))

        qe_s[...] = qe.astype(BF16)
        btot_s[...] = btot2
        for p in range(n_pairs):
            sl = slice(p * LANES, (p + 1) * LANES)
            diag_rows = []
            for jb in range(n_blk):
                rs = slice(jb * sb, (jb + 1) * sb)
                bl_b, kbl_b, q_b = bl2[rs, sl], kbl[rs, sl], q[rs, sl]
                terms = []
                for s in range(sb):
                    arg = jnp.where(sub_row >= flip(s, sb), bl_b - kbl_b[s:s + 1], NEG_BIG)
                    terms.append(q_b * jnp.exp2(arg))
                diag_rows.append(jnp.concatenate(terms, axis=1))
            diag_s[p] = _dot(jnp.concatenate(diag_rows, axis=0).astype(BF16), sel)
            for lvl in range(n_lvl):
                k_p = k_lvl[lvl][:, sl]
                lvl_s[p * n_lvl + lvl] = _dot_nt(q_lvl[lvl][:, sl].astype(BF16), split_heads(k_p))
            upd_s[p] = _dot_tn(v[:, sl].astype(BF16), ke[:, sl].astype(BF16))

    def stage2_issue(j):
        r0 = row_start(j)
        v = v_ref[0, pl.ds(r0, ck), :]
        out = []
        for p in range(n_pairs):
            sl = slice(p * LANES, (p + 1) * LANES)
            parts = []
            for jb in range(n_blk):
                blk = diag_s[p, jb * sb:(jb + 1) * sb, :]
                parts.append(pltpu.roll(blk, jb * sb, 1) if jb else blk)
            scores = jnp.concatenate(parts, axis=0)
            for lvl in range(n_lvl):
                s_lvl = lvl_s[p * n_lvl + lvl]
                scores = scores + (s_lvl if lvl == n_lvl - 1 else jnp.where(group_mask[lvl], s_lvl, 0.0))
            intra = _dot(scores.astype(BF16), split_heads(v[:, sl]))
            st = st_ref[p]
            inter = _dot_nt(qe_s[:, sl], st.astype(BF16))
            new_st = jnp.where(same_head, st * jnp.exp2(btot_s[:, sl]) + upd_s[p], 0.0)
            out.append((inter + intra, new_st))
        return r0, out

    def stage2_finish(r0, out):
        for p in range(n_pairs):
            o_ref[0, pl.ds(r0, ck), p * LANES:(p + 1) * LANES] = out[p][0]
            st_ref[p] = out[p][1]

    return stage1a, stage1b, stage2_issue, stage2_finish


N_HGRN_SCRATCH = 6


def _hgrn_kernel(qf_ref, ff_ref, vf_ref, qb_ref, fb_ref, vb_ref, lb_ref, of_ref, ob_ref, zero_ref, *scratch,
                 rows):
    fwd_scratch, bwd_scratch = scratch[:N_HGRN_SCRATCH], scratch[N_HGRN_SCRATCH:]

    @pl.when(pl.program_id(1) == 0)
    def _():
        fwd_scratch[0][...] = jnp.zeros_like(fwd_scratch[0])
        bwd_scratch[0][...] = jnp.zeros_like(bwd_scratch[0])

    f1a, f1b, f2, f3 = _hgrn_direction(False, qf_ref, ff_ref, vf_ref, lb_ref, of_ref, *fwd_scratch, rows=rows)
    b1a, b1b, b2, b3 = _hgrn_direction(True, qb_ref, fb_ref, vb_ref, lb_ref, ob_ref, *bwd_scratch, rows=rows)
    n_sub = rows // HGRN_CHUNK

    def stage1_both(j):
        fa = f1a(j)
        ba = b1a(j)
        f1b(*fa)
        b1b(*ba)

    zero_part = zero_ref.shape[0] // n_sub

    def store_zeros(j):
        zero_ref[pl.ds(pl.multiple_of(j * zero_part, SUBLANES), zero_part), :] = jnp.zeros(
            (zero_part, zero_ref.shape[1]), zero_ref.dtype)

    stage1_both(0)

    def pipelined(j, carry):
        fo = f2(j)
        bo = b2(j)
        store_zeros(j)
        stage1_both(j + 1)
        f3(*fo)
        b3(*bo)
        return carry

    lax.fori_loop(0, n_sub - 1, pipelined, 0)
    fo = f2(n_sub - 1)
    bo = b2(n_sub - 1)
    store_zeros(n_sub - 1)
    f3(*fo)
    b3(*bo)


def _hgrn(proj, lb, zero_rows, *, d_lru, d_hgrn, rows=512):
    bsz, seq, _ = proj.shape
    n_chunks = seq // rows
    zero_blk = zero_rows // (bsz * n_chunks)
    assert zero_blk * bsz * n_chunks == zero_rows and zero_blk % (SUBLANES * (rows // HGRN_CHUNK)) == 0
    assert 2 * (d_hgrn // HGRN_HEADS) == LANES and HGRN_SUB == SUBLANES and HGRN_CHUNK == HGRN_SUB * SUBLANES
    col0 = (2 * d_lru) // d_hgrn
    n_pairs = d_hgrn // LANES
    n_lvl = (HGRN_CHUNK // HGRN_SUB).bit_length() - 1
    fwd = lambda col: pl.BlockSpec((1, rows, d_hgrn), lambda b, c: (b, c, col))
    bwd = lambda col: pl.BlockSpec((1, rows, d_hgrn), lambda b, c: (b, n_chunks - 1 - c, col))
    direction_scratch = [
        pltpu.VMEM((n_pairs, LANES, LANES), F32),
        pltpu.VMEM((n_pairs, HGRN_CHUNK, LANES), F32),
        pltpu.VMEM((n_pairs * n_lvl, HGRN_CHUNK, LANES), F32),
        pltpu.VMEM((n_pairs, LANES, LANES), F32),
        pltpu.VMEM((HGRN_CHUNK, d_hgrn), BF16),
        pltpu.VMEM((1, d_hgrn), F32),
    ]
    assert len(direction_scratch) == N_HGRN_SCRATCH
    kern = functools.partial(_hgrn_kernel, rows=rows)
    out = jax.ShapeDtypeStruct((bsz, seq, d_hgrn), F32)
    return pl.pallas_call(
        kern,
        out_shape=(out, out, jax.ShapeDtypeStruct((zero_rows, LANES), F32)),
        grid=(bsz, n_chunks),
        in_specs=[fwd(col0), fwd(col0 + 1), fwd(col0 + 3), bwd(col0), bwd(col0 + 2), bwd(col0 + 3),
                  pl.BlockSpec((1, d_hgrn), lambda b, c: (0, 0))],
        out_specs=(pl.BlockSpec((1, rows, d_hgrn), lambda b, c: (b, c, 0)),
                   pl.BlockSpec((1, rows, d_hgrn), lambda b, c: (b, n_chunks - 1 - c, 0)),
                   pl.BlockSpec((zero_blk, LANES), lambda b, c: (b * n_chunks + c, 0))),
        scratch_shapes=direction_scratch + direction_scratch,
        compiler_params=_params(("arbitrary", "arbitrary")),
        name="hgrn2",
    )(proj, proj, proj, proj, proj, proj, lb.reshape(1, d_hgrn))


def _interleave(*streams):
    done = object()
    live = list(streams)
    while live:
        live = [s for s in live if next(s, done) is not done]


def _gelu_tanh(y):
    return 0.5 * y * (1.0 + jnp.tanh(0.7978845608028654 * (y + 0.044715 * (y * y * y))))


def _post_kernel(lru_hbm, proj_hbm, of_hbm, ob_hbm, x_hbm, nlw_ref, nhw_ref, wo_ref,
                 gm_ref, nfw_ref, scf_ref, shf_ref, wr_ref, br_ref,
                 xo_ref, h_ref, slab_ref, cnt_ref, route_ref, carry_ref, logits_s,
                 r_lru, r_y, r_of, r_ob, r_g, r_x, in_sem, *, tm, d_lru, tiles, n_tiles, y_col, g_col):
    step = pl.program_id(0)
    d_hgrn = r_of.shape[-1]

    @pl.when(step == 0)
    def _():
        carry_ref[...] = jnp.zeros_like(carry_ref)
        logits_s[...] = jnp.zeros_like(logits_s)

    def tile_copies(t, ring_slot):
        b, r0 = t // tiles, pl.multiple_of((t % tiles) * tm, tm)

        def cp(hbm, col, width, ring):
            return pltpu.make_async_copy(hbm.at[b, pl.ds(r0, tm), pl.ds(col, width)], ring.at[ring_slot],
                                         in_sem.at[ring_slot])

        return (cp(lru_hbm, 0, d_lru, r_lru), cp(proj_hbm, y_col * d_lru, d_lru, r_y),
                cp(of_hbm, 0, d_hgrn, r_of), cp(ob_hbm, 0, d_hgrn, r_ob),
                cp(proj_hbm, g_col * d_hgrn, d_hgrn, r_g), cp(x_hbm, 0, x_hbm.shape[-1], r_x))

    @pl.when(step == 0)
    def _():
        for t in range(min(IN_RING - 1, n_tiles)):
            for c in tile_copies(t, t):
                c.start()

    ahead = step + IN_RING - 1

    @pl.when(ahead < n_tiles)
    def _():
        for c in tile_copies(ahead, ahead % IN_RING):
            c.start()

    @pl.when(step < n_tiles)
    def _():
        for c in tile_copies(step, step % IN_RING):
            c.wait()

    slot = jnp.minimum(step, n_tiles - 1) % IN_RING
    lru_ref, y_ref, of_ref, ob_ref, g_ref, x_ref = (r.at[slot] for r in (r_lru, r_y, r_of, r_ob, r_g, r_x))

    def mixer():
        lru = lru_ref[...] * _gelu_tanh(y_ref[...])
        ms = jnp.mean(lru * lru, axis=-1, keepdims=True)
        yield
        lru = lru * lax.rsqrt(ms + NORM_EPS) * nlw_ref[...]

        hg = of_ref[...] + ob_ref[...]
        width = hg.shape[1]
        hd = width // HGRN_HEADS
        hd_shift = hd.bit_length() - 1
        er = lax.broadcasted_iota(jnp.int32, (width, width), 0) >> hd_shift
        ec = lax.broadcasted_iota(jnp.int32, (width, width), 1) >> hd_shift
        head_sum = jnp.where(er == ec, 1.0, 0.0).astype(BF16)
        sq = hg * hg
        sq_hi = sq.astype(BF16)
        sq_lo = (sq - sq_hi.astype(F32)).astype(BF16)
        ms_h = (_dot(sq_hi, head_sum) + _dot(sq_lo, head_sum)) * (1.0 / hd)
        yield
        g = g_ref[...]
        hg = (hg * lax.rsqrt(ms_h + NORM_EPS) * nhw_ref[...]) * (g * _sigmoid(g))
        yield
        mixed = _dot(lru.astype(BF16), wo_ref[0:d_lru, :])
        yield
        mixed = mixed + _dot(hg.astype(BF16), wo_ref[d_lru:, :])
        yield
        x_new = x_ref[...] + gm_ref[0] * mixed
        xo_ref[0] = x_new
        ms_f = jnp.mean(x_new * x_new, axis=-1, keepdims=True)
        yield
        h = (x_new * lax.rsqrt(ms_f + NORM_EPS) * nfw_ref[...]) * (1.0 + scf_ref[0]) + shf_ref[0]
        _tiles_store(h_ref, h, tm, lead=(0,))
        yield
        logits_s[...] = _dot(h.astype(BF16), wr_ref[...]) + br_ref[...]

    def routing():
        yield from _routing_steps(logits_s[...], jnp.where(step > 0, 1.0, 0.0), slab_ref, cnt_ref, route_ref,
                                  carry_ref, tm)

    _interleave(routing(), mixer())


def _routing_steps(logits, live, slab_ref, cnt_ref, route_ref, carry_ref, tm):
    lane = lax.broadcasted_iota(jnp.int32, (tm, ROUTE_LANES), 1)
    lane_f = lane.astype(F32)
    far = float(ROUTE_LANES)
    is_g = lane < N_GROUPS
    gl = jnp.where(is_g, logits, NEG_BIG)
    gmax = jnp.max(gl, axis=-1, keepdims=True)
    yield
    g_idx = jnp.min(jnp.where(gl == gmax, lane_f, far), axis=-1, keepdims=True)
    p_group = 1.0 / jnp.sum(jnp.where(is_g, jnp.exp(gl - gmax), 0.0), axis=-1, keepdims=True)
    yield
    e_lane = lane - N_GROUPS
    in_group = (e_lane >= 0) & (e_lane < N_EXPERTS) & ((e_lane >> (EXPERTS_PER_GROUP.bit_length() - 1)).astype(F32) == g_idx)
    ev = jnp.where(in_group, logits, NEG_BIG)
    top1 = jnp.max(ev, axis=-1, keepdims=True)
    yield
    i1 = jnp.min(jnp.where(in_group & (ev == top1), lane_f, far), axis=-1, keepdims=True)
    yield
    rest = in_group & (lane_f != i1)
    ev2 = jnp.where(rest, logits, NEG_BIG)
    top2 = jnp.max(ev2, axis=-1, keepdims=True)
    yield
    i2 = jnp.min(jnp.where(rest & (ev2 == top2), lane_f, far), axis=-1, keepdims=True)
    yield
    e1 = i1 - float(N_GROUPS)
    e2 = i2 - float(N_GROUPS)
    ex = jnp.exp(top2 - top1)
    w1 = p_group / (1.0 + ex)
    w2 = p_group * ex / (1.0 + ex)

    sel1 = lane_f == e1
    sel2 = lane_f == e2
    onehot = jnp.where(sel1 | sel2, live, 0.0)
    tr = lax.broadcasted_iota(jnp.int32, (tm, tm), 0)
    tc = lax.broadcasted_iota(jnp.int32, (tm, tm), 1)
    before = jnp.where(tc < tr, 1.0, 0.0).astype(BF16)
    cnt = _dot(before, onehot.astype(BF16)) + carry_ref[0:1]
    yield
    rank1 = jnp.sum(jnp.where(sel1, cnt, 0.0), axis=-1, keepdims=True)
    rank2 = jnp.sum(jnp.where(sel2, cnt, 0.0), axis=-1, keepdims=True)
    total = carry_ref[0:1] + jnp.sum(onehot, axis=0, keepdims=True)
    carry_ref[...] = jnp.broadcast_to(total, carry_ref.shape)
    cnt_ref[...] = jnp.broadcast_to(total, cnt_ref.shape)
    yield

    slab = jnp.where(lane == 0, e1, 0.0)
    slab = jnp.where(lane == 1, e2, slab)
    slab = jnp.where(lane == 2, w1, slab)
    slab = jnp.where(lane == 3, w2, slab)
    slab = jnp.where(lane == 4, rank1, slab)
    slab = jnp.where(lane == 5, rank2, slab)
    slab_ref[0] = slab
    route_ref[...] = slab.T[0:SUBLANES]


def _post_mixer(lru_sum, proj, hg_f, hg_b, x, nlw, nhw, wo_bf16, g_mix, nfw, sc_ffn, sh_ffn, wr_bf16, br,
                *, tm=512):
    bsz, seq, d = x.shape
    d_lru = lru_sum.shape[-1]
    d_hgrn = hg_f.shape[-1]
    y_col = 1
    g_col = (2 * d_lru) // d_hgrn + 4
    tiles = seq // tm
    n_tiles = bsz * tiles
    cur = lambda s: jnp.minimum(s, n_tiles - 1)
    prev = lambda s: jnp.maximum(s - 1, 0)
    row = lambda w, col=0: pl.BlockSpec((1, tm, w), lambda s: (cur(s) // tiles, cur(s) % tiles, col))
    vec = lambda w: pl.BlockSpec((1, w), lambda s: (0, 0))
    per_b = lambda: pl.BlockSpec((1, 1, d), lambda s: (cur(s) // tiles, 0, 0))
    hbm = lambda: pl.BlockSpec(memory_space=pl.ANY)
    ring = lambda w: pltpu.VMEM((IN_RING, tm, w), F32)
    kern = functools.partial(_post_kernel, tm=tm, d_lru=d_lru, tiles=tiles, n_tiles=n_tiles, y_col=y_col,
                             g_col=g_col)
    return pl.pallas_call(
        kern,
        out_shape=(
            jax.ShapeDtypeStruct((bsz, seq, d), F32),
            jax.ShapeDtypeStruct((bsz, seq * SUBLANES, LANES), F32),
            jax.ShapeDtypeStruct((bsz, seq, ROUTE_LANES), F32),
            jax.ShapeDtypeStruct((SUBLANES, ROUTE_LANES), F32),
            jax.ShapeDtypeStruct((SUBLANES, bsz * seq), F32),
        ),
        grid=(n_tiles + 1,),
        in_specs=[
            hbm(), hbm(), hbm(), hbm(), hbm(),
            vec(d_lru), vec(d_hgrn),
            pl.BlockSpec((d, d), lambda s: (0, 0)),
            per_b(), vec(d), per_b(), per_b(),
            pl.BlockSpec((d, ROUTE_LANES), lambda s: (0, 0)),
            vec(ROUTE_LANES),
        ],
        out_specs=(
            row(d),
            pl.BlockSpec((1, tm * SUBLANES, LANES), lambda s: (cur(s) // tiles, cur(s) % tiles, 0)),
            pl.BlockSpec((1, tm, ROUTE_LANES), lambda s: (prev(s) // tiles, prev(s) % tiles, 0)),
            pl.BlockSpec((SUBLANES, ROUTE_LANES), lambda s: (0, 0)),
            pl.BlockSpec((SUBLANES, tm), lambda s: (0, prev(s))),
        ),
        scratch_shapes=[pltpu.VMEM((SUBLANES, ROUTE_LANES), F32), pltpu.VMEM((tm, ROUTE_LANES), F32),
                        ring(d_lru), ring(d_lru), ring(d_hgrn), ring(d_hgrn), ring(d_hgrn), ring(d),
                        pltpu.SemaphoreType.DMA((IN_RING,))],
        compiler_params=_params(("arbitrary",)),
        name="post_mixer_router",
    )(lru_sum, proj, hg_f, hg_b, x, nlw.reshape(1, d_lru), nhw.reshape(1, d_hgrn), wo_bf16,
      g_mix, nfw.reshape(1, d), sc_ffn, sh_ffn, wr_bf16, br)


def _tiles_load(ref, n, lead=(), first=0):
    return jnp.concatenate(
        [ref[(*lead, pl.ds(first * SUBLANES + j, n, stride=SUBLANES), slice(None))] for j in range(SUBLANES)],
        axis=1)


def _tiles_store(ref, val, n, lead=(), first=0):
    for j in range(SUBLANES):
        ref[(*lead, pl.ds(first * SUBLANES + j, n, stride=SUBLANES), slice(None))] = val[:, j * LANES:(j + 1) * LANES]


def _token_tile(ref, t):
    return ref.at[pl.ds(pl.multiple_of(t * SUBLANES, SUBLANES), SUBLANES)]


def _dest_kernel(start_ref, route_ref, o_ref):
    route = route_ref[...].astype(jnp.int32)
    start = jnp.zeros_like(route)
    for e in range(N_EXPERTS):
        start = jnp.where(route == e, start_ref[e], start)
    o_ref[...] = start + pltpu.roll(route, SUBLANES // 2, 0)


def _dest_rows(expert_start, route):
    return pl.pallas_call(
        _dest_kernel,
        out_shape=jax.ShapeDtypeStruct(route.shape, jnp.int32),
        grid_spec=pltpu.PrefetchScalarGridSpec(
            num_scalar_prefetch=1,
            grid=(1,),
            in_specs=[pl.BlockSpec(route.shape, lambda i, s: (0, 0))],
            out_specs=pl.BlockSpec(route.shape, lambda i, s: (0, 0)),
        ),
        compiler_params=pltpu.CompilerParams(dimension_semantics=("arbitrary",)),
        name="moe_dest_rows",
    )(expert_start, route)


def _dispatch_kernel(d1_ref, d2_ref, h_ref, z_ref, o_ref, sem, *, tb):
    del z_ref
    base = pl.program_id(0) * tb

    def issue(r, carry):
        t = base + r
        pltpu.make_async_copy(_token_tile(h_ref, r), _token_tile(o_ref, d1_ref[t]), sem).start(priority=0)
        pltpu.make_async_copy(_token_tile(h_ref, r), _token_tile(o_ref, d2_ref[t]), sem).start(priority=1)
        return carry

    lax.fori_loop(0, tb, issue, 0, unroll=DMA_ISSUE_UNROLL)
    for _ in range(2):
        pltpu.make_async_copy(h_ref, o_ref.at[pl.ds(0, tb * SUBLANES)], sem).wait()


def _dispatch(dest1, dest2, h_tiles, zero_tiles, *, tb=512):
    m = h_tiles.shape[0] // SUBLANES
    n_rows = zero_tiles.shape[0] // SUBLANES
    kern = functools.partial(_dispatch_kernel, tb=tb)
    return pl.pallas_call(
        kern,
        out_shape=jax.ShapeDtypeStruct((n_rows * SUBLANES, LANES), h_tiles.dtype),
        grid_spec=pltpu.PrefetchScalarGridSpec(
            num_scalar_prefetch=2,
            grid=(m // tb,),
            in_specs=[pl.BlockSpec((tb * SUBLANES, LANES), lambda i, d1, d2: (i, 0)),
                      pl.BlockSpec(memory_space=pl.ANY)],
            out_specs=pl.BlockSpec(memory_space=pl.ANY),
            scratch_shapes=[pltpu.SemaphoreType.DMA(())],
        ),
        input_output_aliases={3: 0},
        compiler_params=pltpu.CompilerParams(dimension_semantics=("arbitrary",), has_side_effects=True),
        name="moe_dispatch",
    )(dest1, dest2, h_tiles, zero_tiles)


def _expert_kernel(plan_ref, x_hbm, wg_hbm, wu_hbm, wd_hbm, o_ref, wg_f, wu_f, wd_f, wg_s, wu_s, wd_s, sem,
                   x_ring, x_sem, *, layer, blk):
    i = pl.program_id(0)
    n_blocks = pl.num_programs(0)
    n_used = plan_ref[n_blocks]
    expert = plan_ref[i]
    next_expert = plan_ref[n_blocks + 1 + i]
    slot = plan_ref[2 * n_blocks + 1 + i]
    first_block = ((i == 0) | (plan_ref[jnp.maximum(i - 1, 0)] != expert)) & (i < n_used)

    def copies(e, s):
        return (pltpu.make_async_copy(wg_hbm.at[layer, e], wg_f.at[s], sem.at[s]),
                pltpu.make_async_copy(wu_hbm.at[layer, e], wu_f.at[s], sem.at[s]),
                pltpu.make_async_copy(wd_hbm.at[layer, e], wd_f.at[s], sem.at[s]))

    @pl.when(i == 0)
    def _():
        for c in copies(expert, slot):
            c.start()

    @pl.when(first_block)
    def _():
        for c in copies(expert, slot):
            c.wait()
        wg_s[...] = wg_f[slot].astype(BF16)
        wu_s[...] = wu_f[slot].astype(BF16)
        wd_s[...] = wd_f[slot].astype(BF16)

        @pl.when(next_expert >= 0)
        def _():
            for c in copies(next_expert, 1 - slot):
                c.start()

    def x_copy(block, ring_slot):
        rows = blk * SUBLANES
        return pltpu.make_async_copy(x_hbm.at[pl.ds(pl.multiple_of(block * rows, rows), rows)],
                                     x_ring.at[ring_slot], x_sem.at[ring_slot])

    @pl.when(i == 0)
    def _():
        for b in range(X_RING - 1):
            @pl.when(b < n_used)
            def _():
                x_copy(b, b).start()

    ahead = i + X_RING - 1

    @pl.when(ahead < n_used)
    def _():
        x_copy(ahead, ahead % X_RING).start()

    fill = plan_ref[3 * n_blocks + 1 + i]
    half = blk // 2

    def ffn(ring_slot, first, n):
        x = _tiles_load(x_ring, n, lead=(ring_slot,), first=first).astype(BF16)
        gate = _dot(x, wg_s[...])
        up = _dot(x, wu_s[...])
        act = (gate * _sigmoid(gate)) * up
        _tiles_store(o_ref, _dot(act.astype(BF16), wd_s[...]), n, first=first)

    for ring_slot in range(X_RING):
        in_slot = (i < n_used) & (i % X_RING == ring_slot)

        @pl.when(in_slot & (fill > half))
        def _():
            x_copy(i, ring_slot).wait()
            ffn(ring_slot, 0, blk)

        @pl.when(in_slot & (fill <= half))
        def _():
            x_copy(i, ring_slot).wait()
            ffn(ring_slot, 0, half)
            o_ref[pl.ds(half * SUBLANES, half * SUBLANES), :] = jnp.zeros((half * SUBLANES, LANES), o_ref.dtype)

    @pl.when(i >= n_used)
    def _():
        o_ref[...] = jnp.zeros_like(o_ref)


def _expert_plan(blk_expert, blocks_used, blk_fill):
    n_blocks = blk_expert.shape[0]
    idx = jnp.arange(n_blocks, dtype=jnp.int32)
    change = jnp.concatenate([jnp.ones((1,), bool), blk_expert[1:] != blk_expert[:-1]])
    slot = (jnp.cumsum(change.astype(jnp.int32)) - 1) & 1
    change_at = jnp.where(change, idx, n_blocks)
    from_here = lax.cummin(change_at[::-1])[::-1]
    next_change = jnp.concatenate([from_here[1:], jnp.full((1,), n_blocks, jnp.int32)])
    next_expert = jnp.where(next_change < blocks_used, blk_expert[jnp.minimum(next_change, n_blocks - 1)], -1)
    return jnp.concatenate([blk_expert, blocks_used.reshape(1), next_expert, slot, blk_fill]).astype(jnp.int32)


def _experts(plan, x_tiles, wg, wu, wd, layer):
    n_rows = x_tiles.shape[0] // SUBLANES
    d, de = wg.shape[-2:]
    blk = EXPERT_BLOCK
    n_blocks = n_rows // blk
    kern = functools.partial(_expert_kernel, layer=layer, blk=blk)
    return pl.pallas_call(
        kern,
        out_shape=jax.ShapeDtypeStruct((n_rows * SUBLANES, LANES), F32),
        grid_spec=pltpu.PrefetchScalarGridSpec(
            num_scalar_prefetch=1,
            grid=(n_blocks,),
            in_specs=[
                pl.BlockSpec(memory_space=pl.ANY),
                pl.BlockSpec(memory_space=pl.ANY),
                pl.BlockSpec(memory_space=pl.ANY),
                pl.BlockSpec(memory_space=pl.ANY),
            ],
            out_specs=pl.BlockSpec((blk * SUBLANES, LANES), lambda i, plan: (i, 0)),
            scratch_shapes=[
                pltpu.VMEM((2, d, de), F32), pltpu.VMEM((2, d, de), F32), pltpu.VMEM((2, de, d), F32),
                pltpu.VMEM((d, de), BF16), pltpu.VMEM((d, de), BF16), pltpu.VMEM((de, d), BF16),
                pltpu.SemaphoreType.DMA((2,)),
                pltpu.VMEM((X_RING, blk * SUBLANES, LANES), F32), pltpu.SemaphoreType.DMA((X_RING,)),
            ],
        ),
        compiler_params=_params(("arbitrary",)),
        name="moe_experts",
    )(plan, x_tiles, wg, wu, wd)


def _combine_kernel(d1_ref, d2_ref, y_ref, slab_ref, x_ref, g_ref, nw_ref, o_ref, ra0, rb0, ra1, rb1, sem,
                    *, tm, tiles, n_steps, final_norm):
    step = pl.program_id(0) * tiles + pl.program_id(1)
    bufs = ((ra0, rb0), (ra1, rb1))

    def gather(tile, slot):
        base = tile * tm
        r1_ref, r2_ref = bufs[slot]

        def issue(r, carry):
            t = base + r
            pltpu.make_async_copy(_token_tile(y_ref, d1_ref[t]), _token_tile(r1_ref, r),
                                  sem.at[slot]).start(priority=0)
            pltpu.make_async_copy(_token_tile(y_ref, d2_ref[t]), _token_tile(r2_ref, r),
                                  sem.at[slot]).start(priority=1)
            return carry

        lax.fori_loop(0, tm, issue, 0, unroll=DMA_ISSUE_UNROLL)

    @pl.when(step == 0)
    def _():
        gather(0, 0)

    for slot in range(2):
        @pl.when((step & 1) == slot)
        def _():
            @pl.when(step + 1 < n_steps)
            def _():
                gather(step + 1, 1 - slot)

            r1_ref, r2_ref = bufs[slot]
            pltpu.make_async_copy(y_ref.at[pl.ds(0, tm * SUBLANES)], r1_ref, sem.at[slot]).wait()
            pltpu.make_async_copy(y_ref.at[pl.ds(0, tm * SUBLANES)], r2_ref, sem.at[slot]).wait()
            slab = slab_ref[0]
            y = slab[:, 2:3] * _tiles_load(r1_ref, tm) + slab[:, 3:4] * _tiles_load(r2_ref, tm)
            out = x_ref[0] + g_ref[0] * y
            if final_norm:
                ms = jnp.mean(out * out, axis=-1, keepdims=True)
                out = out * lax.rsqrt(ms + NORM_EPS) * nw_ref[...]
            o_ref[0] = out


def _combine(dest1, dest2, y_buf, slab, x, g_ffn, norm_w, *, final_norm, tm=512):
    bsz, seq, d = x.shape
    tiles = seq // tm
    kern = functools.partial(_combine_kernel, tm=tm, tiles=tiles, n_steps=bsz * tiles, final_norm=final_norm)
    row_buf = pltpu.VMEM((tm * SUBLANES, LANES), F32)
    return pl.pallas_call(
        kern,
        out_shape=jax.ShapeDtypeStruct((bsz, seq, d), F32),
        grid_spec=pltpu.PrefetchScalarGridSpec(
            num_scalar_prefetch=2,
            grid=(bsz, tiles),
            in_specs=[
                pl.BlockSpec(memory_space=pl.ANY),
                pl.BlockSpec((1, tm, ROUTE_LANES), lambda b, i, d1, d2: (b, i, 0)),
                pl.BlockSpec((1, tm, d), lambda b, i, d1, d2: (b, i, 0)),
                pl.BlockSpec((1, 1, d), lambda b, i, d1, d2: (b, 0, 0)),
                pl.BlockSpec((1, d), lambda b, i, d1, d2: (0, 0)),
            ],
            out_specs=pl.BlockSpec((1, tm, d), lambda b, i, d1, d2: (b, i, 0)),
            scratch_shapes=[row_buf, row_buf, row_buf, row_buf, pltpu.SemaphoreType.DMA((2,))],
        ),
        compiler_params=_params(("arbitrary", "arbitrary")),
        name="moe_combine",
    )(dest1, dest2, y_buf, slab, x, g_ffn, norm_w.reshape(1, d))


def _block_diag(w):
    heads, hd, _ = w.shape
    n = heads * hd
    tiled = jnp.tile(w.reshape(n, hd), (1, heads))
    blk_r = lax.broadcasted_iota(jnp.int32, (n, n), 0) // hd
    blk_c = lax.broadcasted_iota(jnp.int32, (n, n), 1) // hd
    return jnp.where(blk_r == blk_c, tiled, 0.0)


def kernel(x, c, ada_w, ada_b, norm_mix_w, w_in, conv_w, conv_b, lru_wa, lru_ba, lru_wx, lru_bx, lru_lambda, norm_lru_w, hgrn_lb, norm_hgrn_w, w_out, norm_ffn_w, router_group_w, router_group_b, router_expert_w, router_expert_b, expert_w_gate, expert_w_up, expert_w_down, final_norm_w):
    bsz, seq, d = x.shape
    assert d == SUBLANES * LANES, "the MoE row movement keeps one (8, 128) tile per token"
    depth = ada_w.shape[0]
    d_lru = conv_w.shape[-1]
    d_hgrn = hgrn_lb.shape[-1]
    m = bsz * seq
    n_rows = m * 2 + N_EXPERTS * EXPERT_BLOCK
    n_blocks = n_rows // EXPERT_BLOCK

    mod = _modulation(c, ada_w, ada_b)
    lb_cum = jnp.cumsum(jax.nn.softmax(hgrn_lb.astype(F32), axis=0), axis=0)
    lb_all = lb_cum - lb_cum[0:1]

    for l in range(depth):
        sh_mix, sc_mix, g_mix, sh_ffn, sc_ffn, g_ffn = [
            mod[l, :, i * d:(i + 1) * d].reshape(bsz, 1, d) for i in range(N_MODULATIONS)]
        wa_bd = jnp.stack([_block_diag(lru_wa[l, 0]), _block_diag(lru_wa[l, 1])]).astype(BF16)
        wx_bd = jnp.stack([_block_diag(lru_wx[l, 0]), _block_diag(lru_wx[l, 1])]).astype(BF16)
        lru_w = (conv_w[l], conv_b[l], wa_bd, lru_ba[l], wx_bd, lru_bx[l], lru_lambda[l])
        proj, lru_fwd = _in_proj_lru(x, norm_mix_w[l], sc_mix, sh_mix, w_in, l, *lru_w)
        lru_sum = _lru_scan(proj, *lru_w, reverse=True, add_to=lru_fwd)
        hg_f, hg_b, zero_tiles = _hgrn(proj, lb_all[l], n_rows * SUBLANES, d_lru=d_lru, d_hgrn=d_hgrn)

        lane_pad = ROUTE_LANES - N_GROUPS - N_EXPERTS
        wr = jnp.pad(jnp.concatenate([router_group_w[l], router_expert_w[l]], axis=1), ((0, 0), (0, lane_pad)))
        br = jnp.pad(jnp.concatenate([router_group_b[l], router_expert_b[l]]), (0, lane_pad)).reshape(1, ROUTE_LANES)
        x_mid, h_ffn, slab, counts, route = _post_mixer(
            lru_sum, proj, hg_f, hg_b, x, norm_lru_w[l], norm_hgrn_w[l], w_out[l].astype(BF16), g_mix,
            norm_ffn_w[l], sc_ffn, sh_ffn, wr.astype(BF16), br)

        cnt = counts[0, :N_EXPERTS].astype(jnp.int32)
        padded = ((cnt + EXPERT_BLOCK - 1) // EXPERT_BLOCK) * EXPERT_BLOCK
        pend = jnp.cumsum(padded)
        pstart = pend - padded
        blk_start = jnp.arange(n_blocks, dtype=jnp.int32) * EXPERT_BLOCK
        blk_expert = jnp.minimum(jnp.sum(pend[None, :] <= blk_start[:, None], axis=1), N_EXPERTS - 1)
        blk_fill = jnp.clip((pstart + cnt)[blk_expert] - blk_start, 0, EXPERT_BLOCK)
        plan = _expert_plan(blk_expert.astype(jnp.int32), (pend[N_EXPERTS - 1] // EXPERT_BLOCK).astype(jnp.int32),
                            blk_fill)
        dest = _dest_rows(pstart.astype(jnp.int32), route)
        dest1, dest2 = dest[0], dest[1]

        x_buf = _dispatch(dest1, dest2, h_ffn.reshape(m * SUBLANES, LANES), zero_tiles)
        y_buf = _experts(plan, x_buf, expert_w_gate, expert_w_up, expert_w_down, l)
        x = _combine(dest1, dest2, y_buf, slab, x_mid, g_ffn, final_norm_w, final_norm=(l == depth - 1))

    return x
```

```python
import functools

import jax
import jax.numpy as jnp
from jax import lax
from jax.experimental import pallas as pl
from jax.experimental.pallas import tpu as pltpu

F32 = jnp.float32
BF16 = jnp.bfloat16

HGRN_HEADS = 8
N_MODULATIONS = 6
CONV_WIDTH = 4
LRU_C = 8.0
N_GROUPS = 4
EXPERTS_PER_GROUP = 8
N_EXPERTS = N_GROUPS * EXPERTS_PER_GROUP
NORM_EPS = 1e-6

LANES = 128
SUBLANES = 8
VMEM_LIMIT = 56 * 1024 * 1024

HGRN_CHUNK = 64
HGRN_SUB = 8
LOG2E = 1.4426950408889634
ROUTE_LANES = LANES
EXPERT_BLOCK = 512
EXPERT_TAIL_PARTS = 4
X_RING = 3
IN_RING = 3
DMA_ISSUE_UNROLL = 8
NEG_BIG = -3.0e38


def _params(sem):
    return pltpu.CompilerParams(dimension_semantics=sem, vmem_limit_bytes=VMEM_LIMIT)


def _dot(a, b):
    return jnp.dot(a, b, preferred_element_type=F32)


def _dot_nt(a, b):
    return lax.dot_general(a, b, (((1,), (1,)), ((), ())), preferred_element_type=F32)


def _dot_tn(a, b):
    return lax.dot_general(a, b, (((0,), (0,)), ((), ())), preferred_element_type=F32)


def _dot01_exact(m01, x):
    hi = x.astype(BF16)
    r1 = x - hi.astype(F32)
    mid = r1.astype(BF16)
    lo = (r1 - mid.astype(F32)).astype(BF16)
    return _dot(m01, hi) + _dot(m01, mid) + _dot(m01, lo)


def _sigmoid(x):
    return 1.0 / (1.0 + jnp.exp(-x))


def _sigmoid_tanh(x):
    return 0.5 * jnp.tanh(0.5 * x) + 0.5


def _mod_kernel(c_ref, w_ref, b_ref, o_ref):
    c = c_ref[...]
    cond = c * _sigmoid(c)
    o_ref[0] = _dot(cond.astype(BF16), w_ref[0].astype(BF16)) + b_ref[0]


def _modulation(c, ada_w, ada_b):
    depth, d, n = ada_w.shape
    bsz = c.shape[0]
    rows = -(-bsz // SUBLANES) * SUBLANES
    c_pad = jnp.pad(c, ((0, rows - bsz), (0, 0)))
    tn = n // N_MODULATIONS
    out = pl.pallas_call(
        _mod_kernel,
        out_shape=jax.ShapeDtypeStruct((depth, rows, n), F32),
        grid=(depth, n // tn),
        in_specs=[
            pl.BlockSpec((rows, d), lambda l, j: (0, 0)),
            pl.BlockSpec((1, d, tn), lambda l, j: (l, 0, j)),
            pl.BlockSpec((1, 1, tn), lambda l, j: (l, 0, j)),
        ],
        out_specs=pl.BlockSpec((1, rows, tn), lambda l, j: (l, 0, j)),
        compiler_params=_params(("arbitrary", "arbitrary")),
        name="adaln_mod",
    )(c_pad, ada_w, ada_b.reshape(depth, 1, n))
    return out[:, :bsz]


def _rms_mod(x, nw, sc, sh):
    ms = jnp.mean(x * x, axis=-1, keepdims=True)
    return (x * lax.rsqrt(ms + NORM_EPS) * nw) * (1.0 + sc) + sh


def _inproj_lru_kernel(x_ref, nw_ref, sc_ref, sh_ref, w_ref, cw_ref, cb_ref, wa_ref, ba_ref, wx_ref, bx_ref,
                       lam_ref, proj_ref, lru_ref, w_s, xl_s, xnew_s, carry_ref, sa_ref, sb_ref, cin_ref,
                       *, tm, tiles, d_lru):
    s = pl.program_id(0)
    n_cols = proj_ref.shape[-1]

    @pl.when(s == 0)
    def _():
        w_s[...] = w_ref[0].astype(BF16)
        xl_s[...] = jnp.zeros_like(xl_s)
        carry_ref[...] = jnp.zeros_like(carry_ref)

    prev = jnp.maximum(s - 1, 0)
    chunk = prev % tiles
    has_prev = jnp.where(chunk > 0, 1.0, 0.0)
    has_next = jnp.where(chunk < tiles - 1, 1.0, 0.0)
    hb = _rms_mod(x_ref[0], nw_ref[...], sc_ref[0], sh_ref[0]).astype(BF16)

    def project():
        step = 2 * LANES
        for c0 in range(0, n_cols, step):
            block = _dot(hb, w_s[:, c0:c0 + step])
            proj_ref[0, :, c0:c0 + step] = block
            if c0 < d_lru:
                xnew_s[:, c0:c0 + step] = block
            if c0 + step >= d_lru:
                yield

    def scan():
        yield

        def store(rs, h_rows):
            lru_ref[0, rs, :] = h_rows

        yield from _lru_steps(xl_s[SUBLANES:, :], xl_s[0:SUBLANES, :] * has_prev,
                              xnew_s[0:SUBLANES, :] * has_next, carry_ref[...] * has_prev,
                              cw_ref[...], cb_ref[...], wa_ref[0], ba_ref[0], wx_ref[0], bx_ref[0], lam_ref[0],
                              store, carry_ref, sa_ref, sb_ref, cin_ref, reverse=False)

    _interleave(project(), scan())
    xl_s[0:SUBLANES, :] = xl_s[tm:tm + SUBLANES, :]
    xl_s[SUBLANES:, :] = xnew_s[...]


def _in_proj_lru(x, nw, sc, sh, w_in, layer, conv_w, conv_b, wa_bd, ba, wx_bd, bx, lam, tm=512):
    bsz, seq, d = x.shape
    n = w_in.shape[-1]
    d_lru = conv_w.shape[1]
    tiles = seq // tm
    n_tiles = bsz * tiles
    cur = lambda s: jnp.minimum(s, n_tiles - 1)
    prev = lambda s: jnp.maximum(s - 1, 0)
    vec = lambda: pl.BlockSpec((1, 1, d_lru), lambda s: (0, 0, 0))
    mat = lambda: pl.BlockSpec((1, d_lru, d_lru), lambda s: (0, 0, 0))
    kern = functools.partial(_inproj_lru_kernel, tm=tm, tiles=tiles, d_lru=d_lru)
    return pl.pallas_call(
        kern,
        out_shape=(jax.ShapeDtypeStruct((bsz, seq, n), F32), jax.ShapeDtypeStruct((bsz, seq, d_lru), F32)),
        grid=(n_tiles + 1,),
        in_specs=[
            pl.BlockSpec((1, tm, d), lambda s: (cur(s) // tiles, cur(s) % tiles, 0)),
            pl.BlockSpec((1, d), lambda s: (0, 0)),
            pl.BlockSpec((1, 1, d), lambda s: (cur(s) // tiles, 0, 0)),
            pl.BlockSpec((1, 1, d), lambda s: (cur(s) // tiles, 0, 0)),
            pl.BlockSpec((1, d, n), lambda s: (layer, 0, 0), pipeline_mode=pl.Buffered(1)),
            pl.BlockSpec((CONV_WIDTH, d_lru), lambda s: (0, 0)),
            pl.BlockSpec((1, d_lru), lambda s: (0, 0)),
            mat(), vec(), mat(), vec(), vec(),
        ],
        out_specs=(pl.BlockSpec((1, tm, n), lambda s: (cur(s) // tiles, cur(s) % tiles, 0)),
                   pl.BlockSpec((1, tm, d_lru), lambda s: (prev(s) // tiles, prev(s) % tiles, 0))),
        scratch_shapes=[
            pltpu.VMEM((d, n), BF16),
            pltpu.VMEM((tm + SUBLANES, d_lru), F32),
            pltpu.VMEM((tm, d_lru), F32),
            pltpu.VMEM((1, d_lru), F32),
            pltpu.VMEM((d_lru // LANES, tm, LANES), F32), pltpu.VMEM((d_lru // LANES, tm, LANES), F32),
            pltpu.VMEM((tm // SUBLANES, d_lru), F32),
        ],
        compiler_params=_params(("arbitrary",)),
        name="in_proj_lru_fwd",
    )(x, nw.reshape(1, d), sc, sh, w_in, conv_w, conv_b.reshape(1, d_lru), wa_bd, ba.reshape(2, 1, d_lru),
      wx_bd, bx.reshape(2, 1, d_lru), lam.reshape(2, 1, d_lru))


def _lru_kernel(x_ref, xp_ref, xn_ref, cw_ref, cb_ref, wa_ref, ba_ref, wx_ref, bx_ref, lam_ref, *rest,
                reverse, n_chunks, rows, accumulate):
    add_ref = rest[0] if accumulate else None
    o_ref, carry_ref, sa_ref, sb_ref, cin_ref = rest[1:] if accumulate else rest
    c = pl.program_id(1)
    chunk = (n_chunks - 1 - c) if reverse else c

    @pl.when(c == 0)
    def _():
        carry_ref[...] = jnp.zeros_like(carry_ref)

    has_prev = jnp.where(chunk > 0, 1.0, 0.0)
    has_next = jnp.where(chunk < n_chunks - 1, 1.0, 0.0)

    def store(rs, h_rows):
        o_ref[0, rs, :] = (add_ref[0, rs, :] + h_rows) if accumulate else h_rows

    _interleave(_lru_steps(x_ref[0], xp_ref[0] * has_prev, xn_ref[0] * has_next, carry_ref[...],
                           cw_ref[...], cb_ref[...], wa_ref[0], ba_ref[0], wx_ref[0], bx_ref[0], lam_ref[0],
                           store, carry_ref, sa_ref, sb_ref, cin_ref, reverse=reverse))


def _lru_steps(x, xp, xn, carry, cw, cb, wa, ba, wx, bx, lam, store, carry_ref, sa_ref, sb_ref, cin_ref, *,
               reverse):
    rows, width = x.shape
    xe = jnp.concatenate([xp, x, xn], axis=0)
    xc = cb
    for k in range(CONV_WIDTH):
        lo = SUBLANES + k - CONV_WIDTH // 2
        xc = xc + cw[k:k + 1] * xe[lo:lo + rows]
    yield

    xcb = xc.astype(BF16)
    r = _sigmoid_tanh(_dot(xcb, wa) + ba)
    yield
    gate_i = _sigmoid_tanh(_dot(xcb, wx) + bx)
    yield
    softplus_neg_lam = jnp.maximum(-lam, 0.0) + jnp.log1p(jnp.exp(-jnp.abs(lam)))
    log_a = (-LRU_C) * r * softplus_neg_lam
    a = jnp.exp(log_a)
    t = jnp.tanh(-log_a)
    u = jnp.sqrt(2.0 * t / (1.0 + t)) * (gate_i * xc)
    yield

    groups = rows // SUBLANES
    acc_a = a.reshape(groups, SUBLANES, width)
    acc_b = u.reshape(groups, SUBLANES, width)
    sub = lax.broadcasted_iota(jnp.int32, (groups, SUBLANES, width), 1)
    s = 1
    while s < SUBLANES:
        if reverse:
            valid = sub < SUBLANES - s
            sh_a, sh_b = pltpu.roll(acc_a, SUBLANES - s, 1), pltpu.roll(acc_b, SUBLANES - s, 1)
        else:
            valid = sub >= s
            sh_a, sh_b = pltpu.roll(acc_a, s, 1), pltpu.roll(acc_b, s, 1)
        acc_b = jnp.where(valid, acc_a * sh_b + acc_b, acc_b)
        acc_a = jnp.where(valid, acc_a * sh_a, acc_a)
        s *= 2
        yield
    acc_a = acc_a.reshape(rows, width)
    acc_b = acc_b.reshape(rows, width)
    edge = 0 if reverse else SUBLANES - 1
    n_tiles = width // LANES
    for j in range(n_tiles):
        sa_ref[j] = acc_a[:, j * LANES:(j + 1) * LANES]
        sb_ref[j] = acc_b[:, j * LANES:(j + 1) * LANES]
    ea = jnp.concatenate([sa_ref[j, pl.ds(edge, groups, stride=SUBLANES), :] for j in range(n_tiles)], axis=1)
    eb = jnp.concatenate([sb_ref[j, pl.ds(edge, groups, stride=SUBLANES), :] for j in range(n_tiles)], axis=1)
    grow = lax.broadcasted_iota(jnp.int32, (groups, width), 0)
    s = 1
    while s < groups:
        if reverse:
            valid = grow < groups - s
            sh_a, sh_b = pltpu.roll(ea, groups - s, 0), pltpu.roll(eb, groups - s, 0)
        else:
            valid = grow >= s
            sh_a, sh_b = pltpu.roll(ea, s, 0), pltpu.roll(eb, s, 0)
        eb = jnp.where(valid, ea * sh_b + eb, eb)
        ea = jnp.where(valid, ea * sh_a, ea)
        s *= 2
    yield
    group_out = eb + ea * carry
    if reverse:
        carry_in = jnp.where(grow == groups - 1, carry, pltpu.roll(group_out, groups - 1, 0))
        carry_ref[...] = group_out[0:1]
    else:
        carry_in = jnp.where(grow == 0, carry, pltpu.roll(group_out, 1, 0))
        carry_ref[...] = group_out[groups - 1:groups]
    cin_ref[...] = carry_in
    for g in range(groups):
        rs = slice(g * SUBLANES, (g + 1) * SUBLANES)
        store(rs, acc_b[rs] + acc_a[rs] * cin_ref[g:g + 1, :])
        if g % (groups // 4) == groups // 4 - 1:
            yield


def _lru_scan(proj, conv_w, conv_b, wa_bd, ba, wx_bd, bx, lam, *, reverse, add_to=None, rows=512):
    bsz, seq, _ = proj.shape
    d_lru = conv_w.shape[1]
    n_chunks = seq // rows
    halo = rows // SUBLANES
    last_halo = seq // SUBLANES - 1
    dirn = 1 if reverse else 0

    def chunk_of(c):
        return (n_chunks - 1 - c) if reverse else c

    vec = lambda: pl.BlockSpec((1, 1, d_lru), lambda b, c: (dirn, 0, 0))
    mat = lambda: pl.BlockSpec((1, d_lru, d_lru), lambda b, c: (dirn, 0, 0))
    accumulate = add_to is not None
    kern = functools.partial(_lru_kernel, reverse=reverse, n_chunks=n_chunks, rows=rows, accumulate=accumulate)
    tile = pl.BlockSpec((1, rows, d_lru), lambda b, c: (b, chunk_of(c), 0))
    return pl.pallas_call(
        kern,
        out_shape=jax.ShapeDtypeStruct((bsz, seq, d_lru), F32),
        grid=(bsz, n_chunks),
        in_specs=[
            pl.BlockSpec((1, rows, d_lru), lambda b, c: (b, chunk_of(c), 0)),
            pl.BlockSpec((1, SUBLANES, d_lru),
                         lambda b, c: (b, jnp.maximum(chunk_of(c) * halo - 1, 0), 0)),
            pl.BlockSpec((1, SUBLANES, d_lru),
                         lambda b, c: (b, jnp.minimum((chunk_of(c) + 1) * halo, last_halo), 0)),
            pl.BlockSpec((CONV_WIDTH, d_lru), lambda b, c: (0, 0)),
            pl.BlockSpec((1, d_lru), lambda b, c: (0, 0)),
            mat(), vec(), mat(), vec(), vec(),
        ] + ([tile] if accumulate else []),
        out_specs=tile,
        scratch_shapes=[pltpu.VMEM((1, d_lru), F32), pltpu.VMEM((d_lru // LANES, rows, LANES), F32),
                        pltpu.VMEM((d_lru // LANES, rows, LANES), F32),
                        pltpu.VMEM((rows // SUBLANES, d_lru), F32)],
        compiler_params=_params(("arbitrary", "arbitrary")),
        name="lru_bwd" if reverse else "lru_fwd",
    )(proj, proj, proj, conv_w, conv_b.reshape(1, d_lru), wa_bd, ba.reshape(2, 1, d_lru),
      wx_bd, bx.reshape(2, 1, d_lru), lam.reshape(2, 1, d_lru), *([add_to] if accumulate else []))


def _hgrn_direction(rev, q_ref, f_ref, v_ref, lb_ref, o_ref, st_ref, diag_s, lvl_s, upd_s, qe_s, btot_s, *, rows):
    ck, sb = HGRN_CHUNK, HGRN_SUB
    n_blk = ck // sb
    sb_shift = sb.bit_length() - 1
    n_sub = rows // ck
    width = q_ref.shape[-1]
    n_pairs = width // LANES
    half = LANES // 2

    def flip(idx, n):
        return (n - 1 - idx) if rev else idx

    n_lvl = n_blk.bit_length() - 1
    tf = flip(lax.broadcasted_iota(jnp.int32, (ck, ck), 0), ck)
    uf = flip(lax.broadcasted_iota(jnp.int32, (ck, ck), 1), ck)
    tb, ub = tf >> sb_shift, uf >> sb_shift
    pb = flip(lax.broadcasted_iota(jnp.int32, (n_blk, ck), 0), n_blk)
    pub = flip(lax.broadcasted_iota(jnp.int32, (n_blk, ck), 1), ck) >> sb_shift
    mats = [jnp.where((tb == ub) & (uf <= tf), 1.0, 0.0),
            jnp.where(pub < pb, 1.0, 0.0)]
    for lvl in range(n_lvl):
        mid = ((pb >> (lvl + 1)) << (lvl + 1)) + (1 << lvl)
        mats.append(jnp.where(pub < mid, 1.0, 0.0))
    mats.append(jnp.ones((SUBLANES, ck), F32))
    m_cum = jnp.concatenate(mats, axis=0).astype(BF16)

    def per_block(rows8):
        return jnp.concatenate(
            [jnp.broadcast_to(rows8[jb:jb + 1], (sb, rows8.shape[1])) for jb in range(n_blk)], axis=0)
    row_blk = flip(lax.broadcasted_iota(jnp.int32, (ck, width), 0), ck) >> sb_shift
    upper = [((row_blk >> lvl) & 1) == 1 for lvl in range(n_lvl)]
    pr = flip(lax.broadcasted_iota(jnp.int32, (ck, LANES), 0), ck) >> sb_shift
    pc = flip(lax.broadcasted_iota(jnp.int32, (ck, LANES), 1) & (ck - 1), ck) >> sb_shift
    group_mask = [(pr >> (lvl + 1)) == (pc >> (lvl + 1)) for lvl in range(n_lvl)]
    lane = lax.broadcasted_iota(jnp.int32, (1, LANES), 1)
    head0 = lane < half

    def split_heads(x):
        xb = x.astype(BF16)
        zero = jnp.zeros_like(xb)
        return jnp.concatenate([jnp.where(head0, xb, zero), jnp.where(head0, zero, xb)], axis=0)

    sr = lax.broadcasted_iota(jnp.int32, (LANES, LANES), 0)
    sc = lax.broadcasted_iota(jnp.int32, (LANES, LANES), 1)
    same_head = (sr < half) == (sc < half)
    er = lax.broadcasted_iota(jnp.int32, (sb * LANES, LANES), 0)
    ec = lax.broadcasted_iota(jnp.int32, (sb * LANES, LANES), 1)
    sel = jnp.where(ec == (((er & (LANES - 1)) >> (half.bit_length() - 1)) * half + (er >> (LANES.bit_length() - 1))),
                    1.0, 0.0).astype(BF16)
    sub_row = flip(lax.broadcasted_iota(jnp.int32, (sb, LANES), 0), sb)
    lbv = lb_ref[...]

    def row_start(j):
        return pl.multiple_of(flip(j, n_sub) * ck, ck)

    def stage1a(j):
        r0 = row_start(j)
        q = q_ref[0, pl.ds(r0, ck), :]
        z = f_ref[0, pl.ds(r0, ck), :]
        v = v_ref[0, pl.ds(r0, ck), :]
        f = lbv + (1.0 - lbv) * _sigmoid(z)
        lf2 = jnp.log(f) * LOG2E
        k = 1.0 - f
        return q, v, k, _dot01_exact(m_cum, lf2)

    def stage1b(q, v, k, cums):
        bl2 = cums[0:ck]
        b2 = bl2 + per_block(cums[ck:ck + n_blk])
        tot_row = ck + (1 + n_lvl) * n_blk
        btot2 = cums[tot_row:tot_row + 1]
        log2_k = jnp.log(k) * LOG2E
        kb = b2 - log2_k
        kbl = bl2 - log2_k
        qe = q * jnp.exp2(b2)
        ke = jnp.exp2(btot2 - kb)
        q_lvl, k_lvl = [], []
        for lvl in range(n_lvl):
            split2 = per_block(cums[ck + (1 + lvl) * n_blk:ck + (2 + lvl) * n_blk])
            q_lvl.append(q * jnp.exp2(jnp.where(upper[lvl], b2 - split2, NEG_BIG)))
            k_lvl.append(jnp.exp2(jnp.where(upper[lvl], NEG_BIG, split2 - kb)))

        qe_s[...] = qe.astype(BF16)
        btot_s[...] = btot2
        for p in range(n_pairs):
            sl = slice(p * LANES, (p + 1) * LANES)
            diag_rows = []
            for jb in range(n_blk):
                rs = slice(jb * sb, (jb + 1) * sb)
                bl_b, kbl_b, q_b = bl2[rs, sl], kbl[rs, sl], q[rs, sl]
                terms = []
                for s in range(sb):
                    arg = jnp.where(sub_row >= flip(s, sb), bl_b - kbl_b[s:s + 1], NEG_BIG)
                    terms.append(q_b * jnp.exp2(arg))
                diag_rows.append(jnp.concatenate(terms, axis=1))
            diag_s[p] = _dot(jnp.concatenate(diag_rows, axis=0).astype(BF16), sel)
            for lvl in range(n_lvl):
                k_p = k_lvl[lvl][:, sl]
                lvl_s[p * n_lvl + lvl] = _dot_nt(q_lvl[lvl][:, sl].astype(BF16), split_heads(k_p))
            upd_s[p] = _dot_tn(v[:, sl].astype(BF16), ke[:, sl].astype(BF16))

    def stage2_issue(j):
        r0 = row_start(j)
        v = v_ref[0, pl.ds(r0, ck), :]
        out = []
        for p in range(n_pairs):
            sl = slice(p * LANES, (p + 1) * LANES)
            parts = []
            for jb in range(n_blk):
                blk = diag_s[p, jb * sb:(jb + 1) * sb, :]
                parts.append(pltpu.roll(blk, jb * sb, 1) if jb else blk)
            scores = jnp.concatenate(parts, axis=0)
            for lvl in range(n_lvl):
                s_lvl = lvl_s[p * n_lvl + lvl]
                scores = scores + (s_lvl if lvl == n_lvl - 1 else jnp.where(group_mask[lvl], s_lvl, 0.0))
            intra = _dot(scores.astype(BF16), split_heads(v[:, sl]))
            st = st_ref[p]
            inter = _dot_nt(qe_s[:, sl], st.astype(BF16))
            new_st = jnp.where(same_head, st * jnp.exp2(btot_s[:, sl]) + upd_s[p], 0.0)
            out.append((inter + intra, new_st))
        return r0, out

    def stage2_finish(r0, out):
        for p in range(n_pairs):
            o_ref[0, pl.ds(r0, ck), p * LANES:(p + 1) * LANES] = out[p][0]
            st_ref[p] = out[p][1]

    return stage1a, stage1b, stage2_issue, stage2_finish


N_HGRN_SCRATCH = 6


def _hgrn_kernel(qf_ref, ff_ref, vf_ref, qb_ref, fb_ref, vb_ref, lb_ref, of_ref, ob_ref, zero_ref, *scratch,
                 rows):
    fwd_scratch, bwd_scratch = scratch[:N_HGRN_SCRATCH], scratch[N_HGRN_SCRATCH:]

    @pl.when(pl.program_id(1) == 0)
    def _():
        fwd_scratch[0][...] = jnp.zeros_like(fwd_scratch[0])
        bwd_scratch[0][...] = jnp.zeros_like(bwd_scratch[0])

    f1a, f1b, f2, f3 = _hgrn_direction(False, qf_ref, ff_ref, vf_ref, lb_ref, of_ref, *fwd_scratch, rows=rows)
    b1a, b1b, b2, b3 = _hgrn_direction(True, qb_ref, fb_ref, vb_ref, lb_ref, ob_ref, *bwd_scratch, rows=rows)
    n_sub = rows // HGRN_CHUNK

    def stage1_both(j):
        fa = f1a(j)
        ba = b1a(j)
        f1b(*fa)
        b1b(*ba)

    zero_part = zero_ref.shape[0] // n_sub

    def store_zeros(j):
        zero_ref[pl.ds(pl.multiple_of(j * zero_part, SUBLANES), zero_part), :] = jnp.zeros(
            (zero_part, zero_ref.shape[1]), zero_ref.dtype)

    stage1_both(0)

    def pipelined(j, carry):
        fo = f2(j)
        bo = b2(j)
        store_zeros(j)
        stage1_both(j + 1)
        f3(*fo)
        b3(*bo)
        return carry

    lax.fori_loop(0, n_sub - 1, pipelined, 0)
    fo = f2(n_sub - 1)
    bo = b2(n_sub - 1)
    store_zeros(n_sub - 1)
    f3(*fo)
    b3(*bo)


def _hgrn(proj, lb, zero_rows, *, d_lru, d_hgrn, rows=512):
    bsz, seq, _ = proj.shape
    n_chunks = seq // rows
    zero_blk = zero_rows // (bsz * n_chunks)
    assert zero_blk * bsz * n_chunks == zero_rows and zero_blk % (SUBLANES * (rows // HGRN_CHUNK)) == 0
    assert 2 * (d_hgrn // HGRN_HEADS) == LANES and HGRN_SUB == SUBLANES and HGRN_CHUNK == HGRN_SUB * SUBLANES
    col0 = (2 * d_lru) // d_hgrn
    n_pairs = d_hgrn // LANES
    n_lvl = (HGRN_CHUNK // HGRN_SUB).bit_length() - 1
    fwd = lambda col: pl.BlockSpec((1, rows, d_hgrn), lambda b, c: (b, c, col))
    bwd = lambda col: pl.BlockSpec((1, rows, d_hgrn), lambda b, c: (b, n_chunks - 1 - c, col))
    direction_scratch = [
        pltpu.VMEM((n_pairs, LANES, LANES), F32),
        pltpu.VMEM((n_pairs, HGRN_CHUNK, LANES), F32),
        pltpu.VMEM((n_pairs * n_lvl, HGRN_CHUNK, LANES), F32),
        pltpu.VMEM((n_pairs, LANES, LANES), F32),
        pltpu.VMEM((HGRN_CHUNK, d_hgrn), BF16),
        pltpu.VMEM((1, d_hgrn), F32),
    ]
    assert len(direction_scratch) == N_HGRN_SCRATCH
    kern = functools.partial(_hgrn_kernel, rows=rows)
    out = jax.ShapeDtypeStruct((bsz, seq, d_hgrn), F32)
    return pl.pallas_call(
        kern,
        out_shape=(out, out, jax.ShapeDtypeStruct((zero_rows, LANES), F32)),
        grid=(bsz, n_chunks),
        in_specs=[fwd(col0), fwd(col0 + 1), fwd(col0 + 3), bwd(col0), bwd(col0 + 2), bwd(col0 + 3),
                  pl.BlockSpec((1, d_hgrn), lambda b, c: (0, 0))],
        out_specs=(pl.BlockSpec((1, rows, d_hgrn), lambda b, c: (b, c, 0)),
                   pl.BlockSpec((1, rows, d_hgrn), lambda b, c: (b, n_chunks - 1 - c, 0)),
                   pl.BlockSpec((zero_blk, LANES), lambda b, c: (b * n_chunks + c, 0))),
        scratch_shapes=direction_scratch + direction_scratch,
        compiler_params=_params(("arbitrary", "arbitrary")),
        name="hgrn2",
    )(proj, proj, proj, proj, proj, proj, lb.reshape(1, d_hgrn))


def _interleave(*streams):
    done = object()
    live = list(streams)
    while live:
        live = [s for s in live if next(s, done) is not done]


def _gelu_tanh(y):
    return 0.5 * y * (1.0 + jnp.tanh(0.7978845608028654 * (y + 0.044715 * (y * y * y))))


def _post_kernel(lru_hbm, proj_hbm, of_hbm, ob_hbm, x_hbm, nlw_ref, nhw_ref, wo_ref,
                 gm_ref, nfw_ref, scf_ref, shf_ref, wr_ref, br_ref,
                 xo_ref, h_ref, slab_ref, cnt_ref, route_ref, carry_ref, logits_s,
                 r_lru, r_y, r_of, r_ob, r_g, r_x, in_sem, *, tm, d_lru, tiles, n_tiles, y_col, g_col):
    step = pl.program_id(0)
    d_hgrn = r_of.shape[-1]

    @pl.when(step == 0)
    def _():
        carry_ref[...] = jnp.zeros_like(carry_ref)
        logits_s[...] = jnp.zeros_like(logits_s)

    def tile_copies(t, ring_slot):
        b, r0 = t // tiles, pl.multiple_of((t % tiles) * tm, tm)

        def cp(hbm, col, width, ring):
            return pltpu.make_async_copy(hbm.at[b, pl.ds(r0, tm), pl.ds(col, width)], ring.at[ring_slot],
                                         in_sem.at[ring_slot])

        return (cp(lru_hbm, 0, d_lru, r_lru), cp(proj_hbm, y_col * d_lru, d_lru, r_y),
                cp(of_hbm, 0, d_hgrn, r_of), cp(ob_hbm, 0, d_hgrn, r_ob),
                cp(proj_hbm, g_col * d_hgrn, d_hgrn, r_g), cp(x_hbm, 0, x_hbm.shape[-1], r_x))

    @pl.when(step == 0)
    def _():
        for t in range(min(IN_RING - 1, n_tiles)):
            for c in tile_copies(t, t):
                c.start()

    ahead = step + IN_RING - 1

    @pl.when(ahead < n_tiles)
    def _():
        for c in tile_copies(ahead, ahead % IN_RING):
            c.start()

    @pl.when(step < n_tiles)
    def _():
        for c in tile_copies(step, step % IN_RING):
            c.wait()

    slot = jnp.minimum(step, n_tiles - 1) % IN_RING
    lru_ref, y_ref, of_ref, ob_ref, g_ref, x_ref = (r.at[slot] for r in (r_lru, r_y, r_of, r_ob, r_g, r_x))

    def mixer():
        lru = lru_ref[...] * _gelu_tanh(y_ref[...])
        ms = jnp.mean(lru * lru, axis=-1, keepdims=True)
        yield
        lru = lru * lax.rsqrt(ms + NORM_EPS) * nlw_ref[...]

        hg = of_ref[...] + ob_ref[...]
        width = hg.shape[1]
        hd = width // HGRN_HEADS
        hd_shift = hd.bit_length() - 1
        er = lax.broadcasted_iota(jnp.int32, (width, width), 0) >> hd_shift
        ec = lax.broadcasted_iota(jnp.int32, (width, width), 1) >> hd_shift
        head_sum = jnp.where(er == ec, 1.0, 0.0).astype(BF16)
        sq = hg * hg
        sq_hi = sq.astype(BF16)
        sq_lo = (sq - sq_hi.astype(F32)).astype(BF16)
        ms_h = (_dot(sq_hi, head_sum) + _dot(sq_lo, head_sum)) * (1.0 / hd)
        yield
        g = g_ref[...]
        hg = (hg * lax.rsqrt(ms_h + NORM_EPS) * nhw_ref[...]) * (g * _sigmoid(g))
        yield
        mixed = _dot(lru.astype(BF16), wo_ref[0:d_lru, :])
        yield
        mixed = mixed + _dot(hg.astype(BF16), wo_ref[d_lru:, :])
        yield
        x_new = x_ref[...] + gm_ref[0] * mixed
        xo_ref[0] = x_new
        ms_f = jnp.mean(x_new * x_new, axis=-1, keepdims=True)
        yield
        h = (x_new * lax.rsqrt(ms_f + NORM_EPS) * nfw_ref[...]) * (1.0 + scf_ref[0]) + shf_ref[0]
        _tiles_store(h_ref, h, tm, lead=(0,))
        yield
        logits_s[...] = _dot(h.astype(BF16), wr_ref[...]) + br_ref[...]

    def routing():
        yield from _routing_steps(logits_s[...], jnp.where(step > 0, 1.0, 0.0), slab_ref, cnt_ref, route_ref,
                                  carry_ref, tm)

    _interleave(routing(), mixer())


def _routing_steps(logits, live, slab_ref, cnt_ref, route_ref, carry_ref, tm):
    lane = lax.broadcasted_iota(jnp.int32, (tm, ROUTE_LANES), 1)
    lane_f = lane.astype(F32)
    far = float(ROUTE_LANES)
    is_g = lane < N_GROUPS
    gl = jnp.where(is_g, logits, NEG_BIG)
    gmax = jnp.max(gl, axis=-1, keepdims=True)
    yield
    g_idx = jnp.min(jnp.where(gl == gmax, lane_f, far), axis=-1, keepdims=True)
    p_group = 1.0 / jnp.sum(jnp.where(is_g, jnp.exp(gl - gmax), 0.0), axis=-1, keepdims=True)
    yield
    e_lane = lane - N_GROUPS
    in_group = (e_lane >= 0) & (e_lane < N_EXPERTS) & ((e_lane >> (EXPERTS_PER_GROUP.bit_length() - 1)).astype(F32) == g_idx)
    ev = jnp.where(in_group, logits, NEG_BIG)
    top1 = jnp.max(ev, axis=-1, keepdims=True)
    yield
    i1 = jnp.min(jnp.where(in_group & (ev == top1), lane_f, far), axis=-1, keepdims=True)
    yield
    rest = in_group & (lane_f != i1)
    ev2 = jnp.where(rest, logits, NEG_BIG)
    top2 = jnp.max(ev2, axis=-1, keepdims=True)
    yield
    i2 = jnp.min(jnp.where(rest & (ev2 == top2), lane_f, far), axis=-1, keepdims=True)
    yield
    e1 = i1 - float(N_GROUPS)
    e2 = i2 - float(N_GROUPS)
    ex = jnp.exp(top2 - top1)
    w1 = p_group / (1.0 + ex)
    w2 = p_group * ex / (1.0 + ex)

    sel1 = lane_f == e1
    sel2 = lane_f == e2
    onehot = jnp.where(sel1 | sel2, live, 0.0)
    tr = lax.broadcasted_iota(jnp.int32, (tm, tm), 0)
    tc = lax.broadcasted_iota(jnp.int32, (tm, tm), 1)
    before = jnp.where(tc < tr, 1.0, 0.0).astype(BF16)
    cnt = _dot(before, onehot.astype(BF16)) + carry_ref[0:1]
    yield
    rank1 = jnp.sum(jnp.where(sel1, cnt, 0.0), axis=-1, keepdims=True)
    rank2 = jnp.sum(jnp.where(sel2, cnt, 0.0), axis=-1, keepdims=True)
    total = carry_ref[0:1] + jnp.sum(onehot, axis=0, keepdims=True)
    carry_ref[...] = jnp.broadcast_to(total, carry_ref.shape)
    cnt_ref[...] = jnp.broadcast_to(total, cnt_ref.shape)
    yield

    slab = jnp.where(lane == 0, e1, 0.0)
    slab = jnp.where(lane == 1, e2, slab)
    slab = jnp.where(lane == 2, w1, slab)
    slab = jnp.where(lane == 3, w2, slab)
    slab = jnp.where(lane == 4, rank1, slab)
    slab = jnp.where(lane == 5, rank2, slab)
    slab_ref[0] = slab
    route_ref[...] = slab.T[0:SUBLANES]


def _post_mixer(lru_sum, proj, hg_f, hg_b, x, nlw, nhw, wo_bf16, g_mix, nfw, sc_ffn, sh_ffn, wr_bf16, br,
                *, tm=512):
    bsz, seq, d = x.shape
    d_lru = lru_sum.shape[-1]
    d_hgrn = hg_f.shape[-1]
    y_col = 1
    g_col = (2 * d_lru) // d_hgrn + 4
    tiles = seq // tm
    n_tiles = bsz * tiles
    cur = lambda s: jnp.minimum(s, n_tiles - 1)
    prev = lambda s: jnp.maximum(s - 1, 0)
    row = lambda w, col=0: pl.BlockSpec((1, tm, w), lambda s: (cur(s) // tiles, cur(s) % tiles, col))
    vec = lambda w: pl.BlockSpec((1, w), lambda s: (0, 0))
    per_b = lambda: pl.BlockSpec((1, 1, d), lambda s: (cur(s) // tiles, 0, 0))
    hbm = lambda: pl.BlockSpec(memory_space=pl.ANY)
    ring = lambda w: pltpu.VMEM((IN_RING, tm, w), F32)
    kern = functools.partial(_post_kernel, tm=tm, d_lru=d_lru, tiles=tiles, n_tiles=n_tiles, y_col=y_col,
                             g_col=g_col)
    return pl.pallas_call(
        kern,
        out_shape=(
            jax.ShapeDtypeStruct((bsz, seq, d), F32),
            jax.ShapeDtypeStruct((bsz, seq * SUBLANES, LANES), F32),
            jax.ShapeDtypeStruct((bsz, seq, ROUTE_LANES), F32),
            jax.ShapeDtypeStruct((SUBLANES, ROUTE_LANES), F32),
            jax.ShapeDtypeStruct((SUBLANES, bsz * seq), F32),
        ),
        grid=(n_tiles + 1,),
        in_specs=[
            hbm(), hbm(), hbm(), hbm(), hbm(),
            vec(d_lru), vec(d_hgrn),
            pl.BlockSpec((d, d), lambda s: (0, 0)),
            per_b(), vec(d), per_b(), per_b(),
            pl.BlockSpec((d, ROUTE_LANES), lambda s: (0, 0)),
            vec(ROUTE_LANES),
        ],
        out_specs=(
            row(d),
            pl.BlockSpec((1, tm * SUBLANES, LANES), lambda s: (cur(s) // tiles, cur(s) % tiles, 0)),
            pl.BlockSpec((1, tm, ROUTE_LANES), lambda s: (prev(s) // tiles, prev(s) % tiles, 0)),
            pl.BlockSpec((SUBLANES, ROUTE_LANES), lambda s: (0, 0)),
            pl.BlockSpec((SUBLANES, tm), lambda s: (0, prev(s))),
        ),
        scratch_shapes=[pltpu.VMEM((SUBLANES, ROUTE_LANES), F32), pltpu.VMEM((tm, ROUTE_LANES), F32),
                        ring(d_lru), ring(d_lru), ring(d_hgrn), ring(d_hgrn), ring(d_hgrn), ring(d),
                        pltpu.SemaphoreType.DMA((IN_RING,))],
        compiler_params=_params(("arbitrary",)),
        name="post_mixer_router",
    )(lru_sum, proj, hg_f, hg_b, x, nlw.reshape(1, d_lru), nhw.reshape(1, d_hgrn), wo_bf16,
      g_mix, nfw.reshape(1, d), sc_ffn, sh_ffn, wr_bf16, br)


def _tiles_load(ref, n, lead=(), first=0):
    return jnp.concatenate(
        [ref[(*lead, pl.ds(first * SUBLANES + j, n, stride=SUBLANES), slice(None))] for j in range(SUBLANES)],
        axis=1)


def _tiles_store(ref, val, n, lead=(), first=0):
    for j in range(SUBLANES):
        ref[(*lead, pl.ds(first * SUBLANES + j, n, stride=SUBLANES), slice(None))] = val[:, j * LANES:(j + 1) * LANES]


def _token_tile(ref, t):
    return ref.at[pl.ds(pl.multiple_of(t * SUBLANES, SUBLANES), SUBLANES)]


def _dest_kernel(start_ref, route_ref, o_ref):
    route = route_ref[...].astype(jnp.int32)
    start = jnp.zeros_like(route)
    for e in range(N_EXPERTS):
        start = jnp.where(route == e, start_ref[e], start)
    o_ref[...] = start + pltpu.roll(route, SUBLANES // 2, 0)


def _dest_rows(expert_start, route):
    return pl.pallas_call(
        _dest_kernel,
        out_shape=jax.ShapeDtypeStruct(route.shape, jnp.int32),
        grid_spec=pltpu.PrefetchScalarGridSpec(
            num_scalar_prefetch=1,
            grid=(1,),
            in_specs=[pl.BlockSpec(route.shape, lambda i, s: (0, 0))],
            out_specs=pl.BlockSpec(route.shape, lambda i, s: (0, 0)),
        ),
        compiler_params=pltpu.CompilerParams(dimension_semantics=("arbitrary",)),
        name="moe_dest_rows",
    )(expert_start, route)


def _dispatch_kernel(d1_ref, d2_ref, h_ref, z_ref, o_ref, sem, *, tb):
    del z_ref
    base = pl.program_id(0) * tb

    def issue(r, carry):
        t = base + r
        pltpu.make_async_copy(_token_tile(h_ref, r), _token_tile(o_ref, d1_ref[t]), sem).start(priority=0)
        pltpu.make_async_copy(_token_tile(h_ref, r), _token_tile(o_ref, d2_ref[t]), sem).start(priority=1)
        return carry

    lax.fori_loop(0, tb, issue, 0, unroll=DMA_ISSUE_UNROLL)
    for _ in range(2):
        pltpu.make_async_copy(h_ref, o_ref.at[pl.ds(0, tb * SUBLANES)], sem).wait()


def _dispatch(dest1, dest2, h_tiles, zero_tiles, *, tb=512):
    m = h_tiles.shape[0] // SUBLANES
    n_rows = zero_tiles.shape[0] // SUBLANES
    kern = functools.partial(_dispatch_kernel, tb=tb)
    return pl.pallas_call(
        kern,
        out_shape=jax.ShapeDtypeStruct((n_rows * SUBLANES, LANES), h_tiles.dtype),
        grid_spec=pltpu.PrefetchScalarGridSpec(
            num_scalar_prefetch=2,
            grid=(m // tb,),
            in_specs=[pl.BlockSpec((tb * SUBLANES, LANES), lambda i, d1, d2: (i, 0)),
                      pl.BlockSpec(memory_space=pl.ANY)],
            out_specs=pl.BlockSpec(memory_space=pl.ANY),
            scratch_shapes=[pltpu.SemaphoreType.DMA(())],
        ),
        input_output_aliases={3: 0},
        compiler_params=pltpu.CompilerParams(dimension_semantics=("arbitrary",), has_side_effects=True),
        name="moe_dispatch",
    )(dest1, dest2, h_tiles, zero_tiles)


def _expert_kernel(plan_ref, x_hbm, wg_hbm, wu_hbm, wd_hbm, o_ref, wg_f, wu_f, wd_f, wg_s, wu_s, wd_s, sem,
                   x_ring, x_sem, *, layer, blk):
    i = pl.program_id(0)
    n_blocks = pl.num_programs(0)
    n_used = plan_ref[n_blocks]
    expert = plan_ref[i]
    next_expert = plan_ref[n_blocks + 1 + i]
    slot = plan_ref[2 * n_blocks + 1 + i]
    first_block = ((i == 0) | (plan_ref[jnp.maximum(i - 1, 0)] != expert)) & (i < n_used)

    def copies(e, s):
        return (pltpu.make_async_copy(wg_hbm.at[layer, e], wg_f.at[s], sem.at[s]),
                pltpu.make_async_copy(wu_hbm.at[layer, e], wu_f.at[s], sem.at[s]),
                pltpu.make_async_copy(wd_hbm.at[layer, e], wd_f.at[s], sem.at[s]))

    @pl.when(i == 0)
    def _():
        for c in copies(expert, slot):
            c.start()

    @pl.when(first_block)
    def _():
        for c in copies(expert, slot):
            c.wait()
        wg_s[...] = wg_f[slot].astype(BF16)
        wu_s[...] = wu_f[slot].astype(BF16)
        wd_s[...] = wd_f[slot].astype(BF16)

        @pl.when(next_expert >= 0)
        def _():
            for c in copies(next_expert, 1 - slot):
                c.start()

    def x_copy(block, ring_slot):
        rows = blk * SUBLANES
        return pltpu.make_async_copy(x_hbm.at[pl.ds(pl.multiple_of(block * rows, rows), rows)],
                                     x_ring.at[ring_slot], x_sem.at[ring_slot])

    @pl.when(i == 0)
    def _():
        for b in range(X_RING - 1):
            @pl.when(b < n_used)
            def _():
                x_copy(b, b).start()

    ahead = i + X_RING - 1

    @pl.when(ahead < n_used)
    def _():
        x_copy(ahead, ahead % X_RING).start()

    fill = plan_ref[3 * n_blocks + 1 + i]
    part = blk // EXPERT_TAIL_PARTS

    def ffn(ring_slot, first, n):
        x = _tiles_load(x_ring, n, lead=(ring_slot,), first=first).astype(BF16)
        gate = _dot(x, wg_s[...])
        up = _dot(x, wu_s[...])
        act = (gate * _sigmoid(gate)) * up
        _tiles_store(o_ref, _dot(act.astype(BF16), wd_s[...]), n, first=first)

    for ring_slot in range(X_RING):
        in_slot = (i < n_used) & (i % X_RING == ring_slot)

        for k in range(1, EXPERT_TAIL_PARTS + 1):
            @pl.when(in_slot & (fill > (k - 1) * part) & (fill <= k * part))
            def _():
                x_copy(i, ring_slot).wait()
                ffn(ring_slot, 0, k * part)
                if k < EXPERT_TAIL_PARTS:
                    rest = (blk - k * part) * SUBLANES
                    o_ref[pl.ds(k * part * SUBLANES, rest), :] = jnp.zeros((rest, LANES), o_ref.dtype)

    @pl.when(i >= n_used)
    def _():
        o_ref[...] = jnp.zeros_like(o_ref)


def _expert_plan(blk_expert, blocks_used, blk_fill):
    n_blocks = blk_expert.shape[0]
    idx = jnp.arange(n_blocks, dtype=jnp.int32)
    change = jnp.concatenate([jnp.ones((1,), bool), blk_expert[1:] != blk_expert[:-1]])
    slot = (jnp.cumsum(change.astype(jnp.int32)) - 1) & 1
    change_at = jnp.where(change, idx, n_blocks)
    from_here = lax.cummin(change_at[::-1])[::-1]
    next_change = jnp.concatenate([from_here[1:], jnp.full((1,), n_blocks, jnp.int32)])
    next_expert = jnp.where(next_change < blocks_used, blk_expert[jnp.minimum(next_change, n_blocks - 1)], -1)
    return jnp.concatenate([blk_expert, blocks_used.reshape(1), next_expert, slot, blk_fill]).astype(jnp.int32)


def _experts(plan, x_tiles, wg, wu, wd, layer):
    n_rows = x_tiles.shape[0] // SUBLANES
    d, de = wg.shape[-2:]
    blk = EXPERT_BLOCK
    n_blocks = n_rows // blk
    kern = functools.partial(_expert_kernel, layer=layer, blk=blk)
    return pl.pallas_call(
        kern,
        out_shape=jax.ShapeDtypeStruct((n_rows * SUBLANES, LANES), F32),
        grid_spec=pltpu.PrefetchScalarGridSpec(
            num_scalar_prefetch=1,
            grid=(n_blocks,),
            in_specs=[
                pl.BlockSpec(memory_space=pl.ANY),
                pl.BlockSpec(memory_space=pl.ANY),
                pl.BlockSpec(memory_space=pl.ANY),
                pl.BlockSpec(memory_space=pl.ANY),
            ],
            out_specs=pl.BlockSpec((blk * SUBLANES, LANES), lambda i, plan: (i, 0)),
            scratch_shapes=[
                pltpu.VMEM((2, d, de), F32), pltpu.VMEM((2, d, de), F32), pltpu.VMEM((2, de, d), F32),
                pltpu.VMEM((d, de), BF16), pltpu.VMEM((d, de), BF16), pltpu.VMEM((de, d), BF16),
                pltpu.SemaphoreType.DMA((2,)),
                pltpu.VMEM((X_RING, blk * SUBLANES, LANES), F32), pltpu.SemaphoreType.DMA((X_RING,)),
            ],
        ),
        compiler_params=_params(("arbitrary",)),
        name="moe_experts",
    )(plan, x_tiles, wg, wu, wd)


def _combine_kernel(d1_ref, d2_ref, y_ref, slab_ref, x_ref, g_ref, nw_ref, o_ref, ra0, rb0, ra1, rb1, sem,
                    *, tm, tiles, n_steps, final_norm):
    step = pl.program_id(0) * tiles + pl.program_id(1)
    bufs = ((ra0, rb0), (ra1, rb1))

    def gather(tile, slot):
        base = tile * tm
        r1_ref, r2_ref = bufs[slot]

        def issue(r, carry):
            t = base + r
            pltpu.make_async_copy(_token_tile(y_ref, d1_ref[t]), _token_tile(r1_ref, r),
                                  sem.at[slot]).start(priority=0)
            pltpu.make_async_copy(_token_tile(y_ref, d2_ref[t]), _token_tile(r2_ref, r),
                                  sem.at[slot]).start(priority=1)
            return carry

        lax.fori_loop(0, tm, issue, 0, unroll=DMA_ISSUE_UNROLL)

    @pl.when(step == 0)
    def _():
        gather(0, 0)

    for slot in range(2):
        @pl.when((step & 1) == slot)
        def _():
            @pl.when(step + 1 < n_steps)
            def _():
                gather(step + 1, 1 - slot)

            r1_ref, r2_ref = bufs[slot]
            pltpu.make_async_copy(y_ref.at[pl.ds(0, tm * SUBLANES)], r1_ref, sem.at[slot]).wait()
            pltpu.make_async_copy(y_ref.at[pl.ds(0, tm * SUBLANES)], r2_ref, sem.at[slot]).wait()
            slab = slab_ref[0]
            y = slab[:, 2:3] * _tiles_load(r1_ref, tm) + slab[:, 3:4] * _tiles_load(r2_ref, tm)
            out = x_ref[0] + g_ref[0] * y
            if final_norm:
                ms = jnp.mean(out * out, axis=-1, keepdims=True)
                out = out * lax.rsqrt(ms + NORM_EPS) * nw_ref[...]
            o_ref[0] = out


def _combine(dest1, dest2, y_buf, slab, x, g_ffn, norm_w, *, final_norm, tm=512):
    bsz, seq, d = x.shape
    tiles = seq // tm
    kern = functools.partial(_combine_kernel, tm=tm, tiles=tiles, n_steps=bsz * tiles, final_norm=final_norm)
    row_buf = pltpu.VMEM((tm * SUBLANES, LANES), F32)
    return pl.pallas_call(
        kern,
        out_shape=jax.ShapeDtypeStruct((bsz, seq, d), F32),
        grid_spec=pltpu.PrefetchScalarGridSpec(
            num_scalar_prefetch=2,
            grid=(bsz, tiles),
            in_specs=[
                pl.BlockSpec(memory_space=pl.ANY),
                pl.BlockSpec((1, tm, ROUTE_LANES), lambda b, i, d1, d2: (b, i, 0)),
                pl.BlockSpec((1, tm, d), lambda b, i, d1, d2: (b, i, 0)),
                pl.BlockSpec((1, 1, d), lambda b, i, d1, d2: (b, 0, 0)),
                pl.BlockSpec((1, d), lambda b, i, d1, d2: (0, 0)),
            ],
            out_specs=pl.BlockSpec((1, tm, d), lambda b, i, d1, d2: (b, i, 0)),
            scratch_shapes=[row_buf, row_buf, row_buf, row_buf, pltpu.SemaphoreType.DMA((2,))],
        ),
        compiler_params=_params(("arbitrary", "arbitrary")),
        name="moe_combine",
    )(dest1, dest2, y_buf, slab, x, g_ffn, norm_w.reshape(1, d))


def _block_diag(w):
    heads, hd, _ = w.shape
    n = heads * hd
    tiled = jnp.tile(w.reshape(n, hd), (1, heads))
    blk_r = lax.broadcasted_iota(jnp.int32, (n, n), 0) // hd
    blk_c = lax.broadcasted_iota(jnp.int32, (n, n), 1) // hd
    return jnp.where(blk_r == blk_c, tiled, 0.0)


def kernel(x, c, ada_w, ada_b, norm_mix_w, w_in, conv_w, conv_b, lru_wa, lru_ba, lru_wx, lru_bx, lru_lambda, norm_lru_w, hgrn_lb, norm_hgrn_w, w_out, norm_ffn_w, router_group_w, router_group_b, router_expert_w, router_expert_b, expert_w_gate, expert_w_up, expert_w_down, final_norm_w):
    bsz, seq, d = x.shape
    assert d == SUBLANES * LANES, "the MoE row movement keeps one (8, 128) tile per token"
    depth = ada_w.shape[0]
    d_lru = conv_w.shape[-1]
    d_hgrn = hgrn_lb.shape[-1]
    m = bsz * seq
    n_rows = m * 2 + N_EXPERTS * EXPERT_BLOCK
    n_blocks = n_rows // EXPERT_BLOCK

    mod = _modulation(c, ada_w, ada_b)
    lb_cum = jnp.cumsum(jax.nn.softmax(hgrn_lb.astype(F32), axis=0), axis=0)
    lb_all = lb_cum - lb_cum[0:1]

    for l in range(depth):
        sh_mix, sc_mix, g_mix, sh_ffn, sc_ffn, g_ffn = [
            mod[l, :, i * d:(i + 1) * d].reshape(bsz, 1, d) for i in range(N_MODULATIONS)]
        wa_bd = jnp.stack([_block_diag(lru_wa[l, 0]), _block_diag(lru_wa[l, 1])]).astype(BF16)
        wx_bd = jnp.stack([_block_diag(lru_wx[l, 0]), _block_diag(lru_wx[l, 1])]).astype(BF16)
        lru_w = (conv_w[l], conv_b[l], wa_bd, lru_ba[l], wx_bd, lru_bx[l], lru_lambda[l])
        proj, lru_fwd = _in_proj_lru(x, norm_mix_w[l], sc_mix, sh_mix, w_in, l, *lru_w)
        lru_sum = _lru_scan(proj, *lru_w, reverse=True, add_to=lru_fwd)
        hg_f, hg_b, zero_tiles = _hgrn(proj, lb_all[l], n_rows * SUBLANES, d_lru=d_lru, d_hgrn=d_hgrn)

        lane_pad = ROUTE_LANES - N_GROUPS - N_EXPERTS
        wr = jnp.pad(jnp.concatenate([router_group_w[l], router_expert_w[l]], axis=1), ((0, 0), (0, lane_pad)))
        br = jnp.pad(jnp.concatenate([router_group_b[l], router_expert_b[l]]), (0, lane_pad)).reshape(1, ROUTE_LANES)
        x_mid, h_ffn, slab, counts, route = _post_mixer(
            lru_sum, proj, hg_f, hg_b, x, norm_lru_w[l], norm_hgrn_w[l], w_out[l].astype(BF16), g_mix,
            norm_ffn_w[l], sc_ffn, sh_ffn, wr.astype(BF16), br)

        cnt = counts[0, :N_EXPERTS].astype(jnp.int32)
        padded = ((cnt + EXPERT_BLOCK - 1) // EXPERT_BLOCK) * EXPERT_BLOCK
        pend = jnp.cumsum(padded)
        pstart = pend - padded
        blk_start = jnp.arange(n_blocks, dtype=jnp.int32) * EXPERT_BLOCK
        blk_expert = jnp.minimum(jnp.sum(pend[None, :] <= blk_start[:, None], axis=1), N_EXPERTS - 1)
        blk_fill = jnp.clip((pstart + cnt)[blk_expert] - blk_start, 0, EXPERT_BLOCK)
        plan = _expert_plan(blk_expert.astype(jnp.int32), (pend[N_EXPERTS - 1] // EXPERT_BLOCK).astype(jnp.int32),
                            blk_fill)
        dest = _dest_rows(pstart.astype(jnp.int32), route)
        dest1, dest2 = dest[0], dest[1]

        x_buf = _dispatch(dest1, dest2, h_ffn.reshape(m * SUBLANES, LANES), zero_tiles)
        y_buf = _experts(plan, x_buf, expert_w_gate, expert_w_up, expert_w_down, l)
        x = _combine(dest1, dest2, y_buf, slab, x_mid, g_ffn, final_norm_w, final_norm=(l == depth - 1))

    return x
```

```python
import functools

import jax
import jax.numpy as jnp
from jax import lax
from jax.experimental import pallas as pl
from jax.experimental.pallas import tpu as pltpu

F32 = jnp.float32
BF16 = jnp.bfloat16

HGRN_HEADS = 8
N_MODULATIONS = 6
CONV_WIDTH = 4
LRU_C = 8.0
N_GROUPS = 4
EXPERTS_PER_GROUP = 8
N_EXPERTS = N_GROUPS * EXPERTS_PER_GROUP
NORM_EPS = 1e-6

LANES = 128
SUBLANES = 8
VMEM_LIMIT = 56 * 1024 * 1024

HGRN_CHUNK = 64
HGRN_SUB = 8
LOG2E = 1.4426950408889634
ROUTE_LANES = LANES
EXPERT_BLOCK = 512
EXPERT_TAIL_PARTS = 4
X_RING = 3
IN_RING = 3
DMA_ISSUE_UNROLL = 8
NEG_BIG = -3.0e38


def _params(sem):
    return pltpu.CompilerParams(dimension_semantics=sem, vmem_limit_bytes=VMEM_LIMIT)


def _dot(a, b):
    return jnp.dot(a, b, preferred_element_type=F32)


def _dot_nt(a, b):
    return lax.dot_general(a, b, (((1,), (1,)), ((), ())), preferred_element_type=F32)


def _dot_tn(a, b):
    return lax.dot_general(a, b, (((0,), (0,)), ((), ())), preferred_element_type=F32)


def _dot01_exact(m01, x):
    hi = x.astype(BF16)
    r1 = x - hi.astype(F32)
    mid = r1.astype(BF16)
    lo = (r1 - mid.astype(F32)).astype(BF16)
    return _dot(m01, hi) + _dot(m01, mid) + _dot(m01, lo)


def _sigmoid(x):
    return 1.0 / (1.0 + jnp.exp(-x))


def _sigmoid_tanh(x):
    return 0.5 * jnp.tanh(0.5 * x) + 0.5


def _mod_kernel(c_ref, w_ref, b_ref, o_ref):
    c = c_ref[...]
    cond = c * _sigmoid(c)
    o_ref[0] = _dot(cond.astype(BF16), w_ref[0].astype(BF16)) + b_ref[0]


def _modulation(c, ada_w, ada_b):
    depth, d, n = ada_w.shape
    bsz = c.shape[0]
    rows = -(-bsz // SUBLANES) * SUBLANES
    c_pad = jnp.pad(c, ((0, rows - bsz), (0, 0)))
    tn = n // N_MODULATIONS
    out = pl.pallas_call(
        _mod_kernel,
        out_shape=jax.ShapeDtypeStruct((depth, rows, n), F32),
        grid=(depth, n // tn),
        in_specs=[
            pl.BlockSpec((rows, d), lambda l, j: (0, 0)),
            pl.BlockSpec((1, d, tn), lambda l, j: (l, 0, j)),
            pl.BlockSpec((1, 1, tn), lambda l, j: (l, 0, j)),
        ],
        out_specs=pl.BlockSpec((1, rows, tn), lambda l, j: (l, 0, j)),
        compiler_params=_params(("arbitrary", "arbitrary")),
        name="adaln_mod",
    )(c_pad, ada_w, ada_b.reshape(depth, 1, n))
    return out[:, :bsz]


def _rms_mod(x, nw, sc, sh):
    ms = jnp.mean(x * x, axis=-1, keepdims=True)
    return (x * lax.rsqrt(ms + NORM_EPS) * nw) * (1.0 + sc) + sh


def _inproj_lru_kernel(x_ref, nw_ref, sc_ref, sh_ref, w_ref, cw_ref, cb_ref, wa_ref, ba_ref, wx_ref, bx_ref,
                       lam_ref, proj_ref, lru_ref, w_s, xl_s, xnew_s, carry_ref, sa_ref, sb_ref, cin_ref,
                       *, tm, tiles, d_lru):
    s = pl.program_id(0)
    n_cols = proj_ref.shape[-1]

    @pl.when(s == 0)
    def _():
        w_s[...] = w_ref[0].astype(BF16)
        xl_s[...] = jnp.zeros_like(xl_s)
        carry_ref[...] = jnp.zeros_like(carry_ref)

    prev = jnp.maximum(s - 1, 0)
    chunk = prev % tiles
    has_prev = jnp.where(chunk > 0, 1.0, 0.0)
    has_next = jnp.where(chunk < tiles - 1, 1.0, 0.0)
    hb = _rms_mod(x_ref[0], nw_ref[...], sc_ref[0], sh_ref[0]).astype(BF16)

    def project():
        step = 2 * LANES
        for c0 in range(0, n_cols, step):
            block = _dot(hb, w_s[:, c0:c0 + step])
            proj_ref[0, :, c0:c0 + step] = block
            if c0 < d_lru:
                xnew_s[:, c0:c0 + step] = block
            if c0 + step >= d_lru:
                yield

    def scan():
        yield

        def store(rs, h_rows):
            lru_ref[0, rs, :] = h_rows

        yield from _lru_steps(xl_s[SUBLANES:, :], xl_s[0:SUBLANES, :] * has_prev,
                              xnew_s[0:SUBLANES, :] * has_next, carry_ref[...] * has_prev,
                              cw_ref[...], cb_ref[...], wa_ref[0], ba_ref[0], wx_ref[0], bx_ref[0], lam_ref[0],
                              store, carry_ref, sa_ref, sb_ref, cin_ref, reverse=False)

    _interleave(project(), scan())
    xl_s[0:SUBLANES, :] = xl_s[tm:tm + SUBLANES, :]
    xl_s[SUBLANES:, :] = xnew_s[...]


def _in_proj_lru(x, nw, sc, sh, w_in, layer, conv_w, conv_b, wa_bd, ba, wx_bd, bx, lam, tm=512):
    bsz, seq, d = x.shape
    n = w_in.shape[-1]
    d_lru = conv_w.shape[1]
    tiles = seq // tm
    n_tiles = bsz * tiles
    cur = lambda s: jnp.minimum(s, n_tiles - 1)
    prev = lambda s: jnp.maximum(s - 1, 0)
    vec = lambda: pl.BlockSpec((1, 1, d_lru), lambda s: (0, 0, 0))
    mat = lambda: pl.BlockSpec((1, d_lru, d_lru), lambda s: (0, 0, 0))
    kern = functools.partial(_inproj_lru_kernel, tm=tm, tiles=tiles, d_lru=d_lru)
    return pl.pallas_call(
        kern,
        out_shape=(jax.ShapeDtypeStruct((bsz, seq, n), F32), jax.ShapeDtypeStruct((bsz, seq, d_lru), F32)),
        grid=(n_tiles + 1,),
        in_specs=[
            pl.BlockSpec((1, tm, d), lambda s: (cur(s) // tiles, cur(s) % tiles, 0)),
            pl.BlockSpec((1, d), lambda s: (0, 0)),
            pl.BlockSpec((1, 1, d), lambda s: (cur(s) // tiles, 0, 0)),
            pl.BlockSpec((1, 1, d), lambda s: (cur(s) // tiles, 0, 0)),
            pl.BlockSpec((1, d, n), lambda s: (layer, 0, 0), pipeline_mode=pl.Buffered(1)),
            pl.BlockSpec((CONV_WIDTH, d_lru), lambda s: (0, 0)),
            pl.BlockSpec((1, d_lru), lambda s: (0, 0)),
            mat(), vec(), mat(), vec(), vec(),
        ],
        out_specs=(pl.BlockSpec((1, tm, n), lambda s: (cur(s) // tiles, cur(s) % tiles, 0)),
                   pl.BlockSpec((1, tm, d_lru), lambda s: (prev(s) // tiles, prev(s) % tiles, 0))),
        scratch_shapes=[
            pltpu.VMEM((d, n), BF16),
            pltpu.VMEM((tm + SUBLANES, d_lru), F32),
            pltpu.VMEM((tm, d_lru), F32),
            pltpu.VMEM((1, d_lru), F32),
            pltpu.VMEM((d_lru // LANES, tm, LANES), F32), pltpu.VMEM((d_lru // LANES, tm, LANES), F32),
            pltpu.VMEM((tm // SUBLANES, d_lru), F32),
        ],
        compiler_params=_params(("arbitrary",)),
        name="in_proj_lru_fwd",
    )(x, nw.reshape(1, d), sc, sh, w_in, conv_w, conv_b.reshape(1, d_lru), wa_bd, ba.reshape(2, 1, d_lru),
      wx_bd, bx.reshape(2, 1, d_lru), lam.reshape(2, 1, d_lru))


def _lru_kernel(x_ref, xp_ref, xn_ref, cw_ref, cb_ref, wa_ref, ba_ref, wx_ref, bx_ref, lam_ref, *rest,
                reverse, n_chunks, rows, accumulate):
    add_ref = rest[0] if accumulate else None
    o_ref, carry_ref, sa_ref, sb_ref, cin_ref = rest[1:] if accumulate else rest
    c = pl.program_id(1)
    chunk = (n_chunks - 1 - c) if reverse else c

    @pl.when(c == 0)
    def _():
        carry_ref[...] = jnp.zeros_like(carry_ref)

    has_prev = jnp.where(chunk > 0, 1.0, 0.0)
    has_next = jnp.where(chunk < n_chunks - 1, 1.0, 0.0)

    def store(rs, h_rows):
        o_ref[0, rs, :] = (add_ref[0, rs, :] + h_rows) if accumulate else h_rows

    _interleave(_lru_steps(x_ref[0], xp_ref[0] * has_prev, xn_ref[0] * has_next, carry_ref[...],
                           cw_ref[...], cb_ref[...], wa_ref[0], ba_ref[0], wx_ref[0], bx_ref[0], lam_ref[0],
                           store, carry_ref, sa_ref, sb_ref, cin_ref, reverse=reverse))


def _lru_steps(x, xp, xn, carry, cw, cb, wa, ba, wx, bx, lam, store, carry_ref, sa_ref, sb_ref, cin_ref, *,
               reverse):
    rows, width = x.shape
    xe = jnp.concatenate([xp, x, xn], axis=0)
    xc = cb
    for k in range(CONV_WIDTH):
        lo = SUBLANES + k - CONV_WIDTH // 2
        xc = xc + cw[k:k + 1] * xe[lo:lo + rows]
    yield

    xcb = xc.astype(BF16)
    r = _sigmoid_tanh(_dot(xcb, wa) + ba)
    yield
    gate_i = _sigmoid_tanh(_dot(xcb, wx) + bx)
    yield
    softplus_neg_lam = jnp.maximum(-lam, 0.0) + jnp.log1p(jnp.exp(-jnp.abs(lam)))
    log_a = (-LRU_C) * r * softplus_neg_lam
    a = jnp.exp(log_a)
    t = jnp.tanh(-log_a)
    u = jnp.sqrt(2.0 * t / (1.0 + t)) * (gate_i * xc)
    yield

    groups = rows // SUBLANES
    acc_a = a.reshape(groups, SUBLANES, width)
    acc_b = u.reshape(groups, SUBLANES, width)
    sub = lax.broadcasted_iota(jnp.int32, (groups, SUBLANES, width), 1)
    s = 1
    while s < SUBLANES:
        if reverse:
            valid = sub < SUBLANES - s
            sh_a, sh_b = pltpu.roll(acc_a, SUBLANES - s, 1), pltpu.roll(acc_b, SUBLANES - s, 1)
        else:
            valid = sub >= s
            sh_a, sh_b = pltpu.roll(acc_a, s, 1), pltpu.roll(acc_b, s, 1)
        acc_b = jnp.where(valid, acc_a * sh_b + acc_b, acc_b)
        acc_a = jnp.where(valid, acc_a * sh_a, acc_a)
        s *= 2
        yield
    acc_a = acc_a.reshape(rows, width)
    acc_b = acc_b.reshape(rows, width)
    edge = 0 if reverse else SUBLANES - 1
    n_tiles = width // LANES
    for j in range(n_tiles):
        sa_ref[j] = acc_a[:, j * LANES:(j + 1) * LANES]
        sb_ref[j] = acc_b[:, j * LANES:(j + 1) * LANES]
    ea = jnp.concatenate([sa_ref[j, pl.ds(edge, groups, stride=SUBLANES), :] for j in range(n_tiles)], axis=1)
    eb = jnp.concatenate([sb_ref[j, pl.ds(edge, groups, stride=SUBLANES), :] for j in range(n_tiles)], axis=1)
    grow = lax.broadcasted_iota(jnp.int32, (groups, width), 0)
    s = 1
    while s < groups:
        if reverse:
            valid = grow < groups - s
            sh_a, sh_b = pltpu.roll(ea, groups - s, 0), pltpu.roll(eb, groups - s, 0)
        else:
            valid = grow >= s
            sh_a, sh_b = pltpu.roll(ea, s, 0), pltpu.roll(eb, s, 0)
        eb = jnp.where(valid, ea * sh_b + eb, eb)
        ea = jnp.where(valid, ea * sh_a, ea)
        s *= 2
    yield
    group_out = eb + ea * carry
    if reverse:
        carry_in = jnp.where(grow == groups - 1, carry, pltpu.roll(group_out, groups - 1, 0))
        carry_ref[...] = group_out[0:1]
    else:
        carry_in = jnp.where(grow == 0, carry, pltpu.roll(group_out, 1, 0))
        carry_ref[...] = group_out[groups - 1:groups]
    cin_ref[...] = carry_in
    for g in range(groups):
        rs = slice(g * SUBLANES, (g + 1) * SUBLANES)
        store(rs, acc_b[rs] + acc_a[rs] * cin_ref[g:g + 1, :])
        if g % (groups // 4) == groups // 4 - 1:
            yield


def _lru_scan(proj, conv_w, conv_b, wa_bd, ba, wx_bd, bx, lam, *, reverse, add_to=None, rows=512):
    bsz, seq, _ = proj.shape
    d_lru = conv_w.shape[1]
    n_chunks = seq // rows
    halo = rows // SUBLANES
    last_halo = seq // SUBLANES - 1
    dirn = 1 if reverse else 0

    def chunk_of(c):
        return (n_chunks - 1 - c) if reverse else c

    vec = lambda: pl.BlockSpec((1, 1, d_lru), lambda b, c: (dirn, 0, 0))
    mat = lambda: pl.BlockSpec((1, d_lru, d_lru), lambda b, c: (dirn, 0, 0))
    accumulate = add_to is not None
    kern = functools.partial(_lru_kernel, reverse=reverse, n_chunks=n_chunks, rows=rows, accumulate=accumulate)
    tile = pl.BlockSpec((1, rows, d_lru), lambda b, c: (b, chunk_of(c), 0))
    return pl.pallas_call(
        kern,
        out_shape=jax.ShapeDtypeStruct((bsz, seq, d_lru), F32),
        grid=(bsz, n_chunks),
        in_specs=[
            pl.BlockSpec((1, rows, d_lru), lambda b, c: (b, chunk_of(c), 0)),
            pl.BlockSpec((1, SUBLANES, d_lru),
                         lambda b, c: (b, jnp.maximum(chunk_of(c) * halo - 1, 0), 0)),
            pl.BlockSpec((1, SUBLANES, d_lru),
                         lambda b, c: (b, jnp.minimum((chunk_of(c) + 1) * halo, last_halo), 0)),
            pl.BlockSpec((CONV_WIDTH, d_lru), lambda b, c: (0, 0)),
            pl.BlockSpec((1, d_lru), lambda b, c: (0, 0)),
            mat(), vec(), mat(), vec(), vec(),
        ] + ([tile] if accumulate else []),
        out_specs=tile,
        scratch_shapes=[pltpu.VMEM((1, d_lru), F32), pltpu.VMEM((d_lru // LANES, rows, LANES), F32),
                        pltpu.VMEM((d_lru // LANES, rows, LANES), F32),
                        pltpu.VMEM((rows // SUBLANES, d_lru), F32)],
        compiler_params=_params(("arbitrary", "arbitrary")),
        name="lru_bwd" if reverse else "lru_fwd",
    )(proj, proj, proj, conv_w, conv_b.reshape(1, d_lru), wa_bd, ba.reshape(2, 1, d_lru),
      wx_bd, bx.reshape(2, 1, d_lru), lam.reshape(2, 1, d_lru), *([add_to] if accumulate else []))


def _hgrn_direction(rev, q_ref, f_ref, v_ref, lb_ref, o_ref, st_ref, diag_s, lvl_s, upd_s, qe_s, btot_s, *, rows):
    ck, sb = HGRN_CHUNK, HGRN_SUB
    n_blk = ck // sb
    sb_shift = sb.bit_length() - 1
    n_sub = rows // ck
    width = q_ref.shape[-1]
    n_pairs = width // LANES
    half = LANES // 2

    def flip(idx, n):
        return (n - 1 - idx) if rev else idx

    n_lvl = n_blk.bit_length() - 1
    tf = flip(lax.broadcasted_iota(jnp.int32, (ck, ck), 0), ck)
    uf = flip(lax.broadcasted_iota(jnp.int32, (ck, ck), 1), ck)
    tb, ub = tf >> sb_shift, uf >> sb_shift
    pb = flip(lax.broadcasted_iota(jnp.int32, (n_blk, ck), 0), n_blk)
    pub = flip(lax.broadcasted_iota(jnp.int32, (n_blk, ck), 1), ck) >> sb_shift
    mats = [jnp.where((tb == ub) & (uf <= tf), 1.0, 0.0),
            jnp.where(pub < pb, 1.0, 0.0)]
    for lvl in range(n_lvl):
        mid = ((pb >> (lvl + 1)) << (lvl + 1)) + (1 << lvl)
        mats.append(jnp.where(pub < mid, 1.0, 0.0))
    mats.append(jnp.ones((SUBLANES, ck), F32))
    m_cum = jnp.concatenate(mats, axis=0).astype(BF16)

    def per_block(rows8):
        return jnp.concatenate(
            [jnp.broadcast_to(rows8[jb:jb + 1], (sb, rows8.shape[1])) for jb in range(n_blk)], axis=0)
    row_blk = flip(lax.broadcasted_iota(jnp.int32, (ck, width), 0), ck) >> sb_shift
    upper = [((row_blk >> lvl) & 1) == 1 for lvl in range(n_lvl)]
    pr = flip(lax.broadcasted_iota(jnp.int32, (ck, LANES), 0), ck) >> sb_shift
    pc = flip(lax.broadcasted_iota(jnp.int32, (ck, LANES), 1) & (ck - 1), ck) >> sb_shift
    group_mask = [(pr >> (lvl + 1)) == (pc >> (lvl + 1)) for lvl in range(n_lvl)]
    lane = lax.broadcasted_iota(jnp.int32, (1, LANES), 1)
    head0 = lane < half

    def split_heads(x):
        xb = x.astype(BF16)
        zero = jnp.zeros_like(xb)
        return jnp.concatenate([jnp.where(head0, xb, zero), jnp.where(head0, zero, xb)], axis=0)

    sr = lax.broadcasted_iota(jnp.int32, (LANES, LANES), 0)
    sc = lax.broadcasted_iota(jnp.int32, (LANES, LANES), 1)
    same_head = (sr < half) == (sc < half)
    er = lax.broadcasted_iota(jnp.int32, (sb * LANES, LANES), 0)
    ec = lax.broadcasted_iota(jnp.int32, (sb * LANES, LANES), 1)
    sel = jnp.where(ec == (((er & (LANES - 1)) >> (half.bit_length() - 1)) * half + (er >> (LANES.bit_length() - 1))),
                    1.0, 0.0).astype(BF16)
    sub_row = flip(lax.broadcasted_iota(jnp.int32, (sb, LANES), 0), sb)
    lbv = lb_ref[...]

    def row_start(j):
        return pl.multiple_of(flip(j, n_sub) * ck, ck)

    def stage1a(j):
        r0 = row_start(j)
        q = q_ref[0, pl.ds(r0, ck), :]
        z = f_ref[0, pl.ds(r0, ck), :]
        v = v_ref[0, pl.ds(r0, ck), :]
        f = lbv + (1.0 - lbv) * _sigmoid(z)
        lf2 = jnp.log(f) * LOG2E
        k = 1.0 - f
        return q, v, k, _dot01_exact(m_cum, lf2)

    def stage1b(q, v, k, cums):
        bl2 = cums[0:ck]
        b2 = bl2 + per_block(cums[ck:ck + n_blk])
        tot_row = ck + (1 + n_lvl) * n_blk
        btot2 = cums[tot_row:tot_row + 1]
        log2_k = jnp.log(k) * LOG2E
        kb = b2 - log2_k
        kbl = bl2 - log2_k
        qe = q * jnp.exp2(b2)
        ke = jnp.exp2(btot2 - kb)
        q_lvl, k_lvl = [], []
        for lvl in range(n_lvl):
            split2 = per_block(cums[ck + (1 + lvl) * n_blk:ck + (2 + lvl) * n_blk])
            q_lvl.append(q * jnp.exp2(jnp.where(upper[lvl], b2 - split2, NEG_BIG)))
            k_lvl.append(jnp.exp2(jnp.where(upper[lvl], NEG_BIG, split2 - kb)))

        qe_s[...] = qe.astype(BF16)
        btot_s[...] = btot2
        for p in range(n_pairs):
            sl = slice(p * LANES, (p + 1) * LANES)
            diag_rows = []
            for jb in range(n_blk):
                rs = slice(jb * sb, (jb + 1) * sb)
                bl_b, kbl_b, q_b = bl2[rs, sl], kbl[rs, sl], q[rs, sl]
                terms = []
                for s in range(sb):
                    arg = jnp.where(sub_row >= flip(s, sb), bl_b - kbl_b[s:s + 1], NEG_BIG)
                    terms.append(q_b * jnp.exp2(arg))
                diag_rows.append(jnp.concatenate(terms, axis=1))
            diag_s[p] = _dot(jnp.concatenate(diag_rows, axis=0).astype(BF16), sel)
            for lvl in range(n_lvl):
                k_p = k_lvl[lvl][:, sl]
                lvl_s[p * n_lvl + lvl] = _dot_nt(q_lvl[lvl][:, sl].astype(BF16), split_heads(k_p))
            upd_s[p] = _dot_tn(v[:, sl].astype(BF16), ke[:, sl].astype(BF16))

    def stage2_issue(j):
        r0 = row_start(j)
        v = v_ref[0, pl.ds(r0, ck), :]
        out = []
        for p in range(n_pairs):
            sl = slice(p * LANES, (p + 1) * LANES)
            parts = []
            for jb in range(n_blk):
                blk = diag_s[p, jb * sb:(jb + 1) * sb, :]
                parts.append(pltpu.roll(blk, jb * sb, 1) if jb else blk)
            scores = jnp.concatenate(parts, axis=0)
            for lvl in range(n_lvl):
                s_lvl = lvl_s[p * n_lvl + lvl]
                scores = scores + (s_lvl if lvl == n_lvl - 1 else jnp.where(group_mask[lvl], s_lvl, 0.0))
            intra = _dot(scores.astype(BF16), split_heads(v[:, sl]))
            st = st_ref[p]
            inter = _dot_nt(qe_s[:, sl], st.astype(BF16))
            new_st = jnp.where(same_head, st * jnp.exp2(btot_s[:, sl]) + upd_s[p], 0.0)
            out.append((inter + intra, new_st))
        return r0, out

    def stage2_finish(r0, out):
        for p in range(n_pairs):
            o_ref[0, pl.ds(r0, ck), p * LANES:(p + 1) * LANES] = out[p][0]
            st_ref[p] = out[p][1]

    return stage1a, stage1b, stage2_issue, stage2_finish


N_HGRN_SCRATCH = 6


def _hgrn_kernel(qf_ref, ff_ref, vf_ref, qb_ref, fb_ref, vb_ref, lb_ref, of_ref, ob_ref, zero_ref, *scratch,
                 rows):
    fwd_scratch, bwd_scratch = scratch[:N_HGRN_SCRATCH], scratch[N_HGRN_SCRATCH:]

    @pl.when(pl.program_id(1) == 0)
    def _():
        fwd_scratch[0][...] = jnp.zeros_like(fwd_scratch[0])
        bwd_scratch[0][...] = jnp.zeros_like(bwd_scratch[0])

    f1a, f1b, f2, f3 = _hgrn_direction(False, qf_ref, ff_ref, vf_ref, lb_ref, of_ref, *fwd_scratch, rows=rows)
    b1a, b1b, b2, b3 = _hgrn_direction(True, qb_ref, fb_ref, vb_ref, lb_ref, ob_ref, *bwd_scratch, rows=rows)
    n_sub = rows // HGRN_CHUNK

    def stage1_both(j):
        fa = f1a(j)
        ba = b1a(j)
        f1b(*fa)
        b1b(*ba)

    zero_part = zero_ref.shape[0] // n_sub

    def store_zeros(j):
        zero_ref[pl.ds(pl.multiple_of(j * zero_part, SUBLANES), zero_part), :] = jnp.zeros(
            (zero_part, zero_ref.shape[1]), zero_ref.dtype)

    stage1_both(0)

    def pipelined(j, carry):
        fo = f2(j)
        bo = b2(j)
        store_zeros(j)
        stage1_both(j + 1)
        f3(*fo)
        b3(*bo)
        return carry

    lax.fori_loop(0, n_sub - 1, pipelined, 0)
    fo = f2(n_sub - 1)
    bo = b2(n_sub - 1)
    store_zeros(n_sub - 1)
    f3(*fo)
    b3(*bo)


def _hgrn(proj, lb, zero_rows, *, d_lru, d_hgrn, rows=512):
    bsz, seq, _ = proj.shape
    n_chunks = seq // rows
    zero_blk = zero_rows // (bsz * n_chunks)
    assert zero_blk * bsz * n_chunks == zero_rows and zero_blk % (SUBLANES * (rows // HGRN_CHUNK)) == 0
    assert 2 * (d_hgrn // HGRN_HEADS) == LANES and HGRN_SUB == SUBLANES and HGRN_CHUNK == HGRN_SUB * SUBLANES
    col0 = (2 * d_lru) // d_hgrn
    n_pairs = d_hgrn // LANES
    n_lvl = (HGRN_CHUNK // HGRN_SUB).bit_length() - 1
    fwd = lambda col: pl.BlockSpec((1, rows, d_hgrn), lambda b, c: (b, c, col))
    bwd = lambda col: pl.BlockSpec((1, rows, d_hgrn), lambda b, c: (b, n_chunks - 1 - c, col))
    direction_scratch = [
        pltpu.VMEM((n_pairs, LANES, LANES), F32),
        pltpu.VMEM((n_pairs, HGRN_CHUNK, LANES), F32),
        pltpu.VMEM((n_pairs * n_lvl, HGRN_CHUNK, LANES), F32),
        pltpu.VMEM((n_pairs, LANES, LANES), F32),
        pltpu.VMEM((HGRN_CHUNK, d_hgrn), BF16),
        pltpu.VMEM((1, d_hgrn), F32),
    ]
    assert len(direction_scratch) == N_HGRN_SCRATCH
    kern = functools.partial(_hgrn_kernel, rows=rows)
    out = jax.ShapeDtypeStruct((bsz, seq, d_hgrn), F32)
    return pl.pallas_call(
        kern,
        out_shape=(out, out, jax.ShapeDtypeStruct((zero_rows, LANES), F32)),
        grid=(bsz, n_chunks),
        in_specs=[fwd(col0), fwd(col0 + 1), fwd(col0 + 3), bwd(col0), bwd(col0 + 2), bwd(col0 + 3),
                  pl.BlockSpec((1, d_hgrn), lambda b, c: (0, 0))],
        out_specs=(pl.BlockSpec((1, rows, d_hgrn), lambda b, c: (b, c, 0)),
                   pl.BlockSpec((1, rows, d_hgrn), lambda b, c: (b, n_chunks - 1 - c, 0)),
                   pl.BlockSpec((zero_blk, LANES), lambda b, c: (b * n_chunks + c, 0))),
        scratch_shapes=direction_scratch + direction_scratch,
        compiler_params=_params(("arbitrary", "arbitrary")),
        name="hgrn2",
    )(proj, proj, proj, proj, proj, proj, lb.reshape(1, d_hgrn))


def _interleave(*streams):
    done = object()
    live = list(streams)
    while live:
        live = [s for s in live if next(s, done) is not done]


def _gelu_tanh(y):
    return 0.5 * y * (1.0 + jnp.tanh(0.7978845608028654 * (y + 0.044715 * (y * y * y))))


def _post_kernel(lru_hbm, proj_hbm, of_hbm, ob_hbm, x_hbm, nlw_ref, nhw_ref, wo_ref,
                 gm_ref, nfw_ref, scf_ref, shf_ref, wr_ref, br_ref,
                 xo_ref, h_ref, slab_ref, cnt_ref, route_ref, carry_ref, logits_s,
                 r_lru, r_y, r_of, r_ob, r_g, r_x, in_sem, *, tm, d_lru, tiles, n_tiles, y_col, g_col):
    step = pl.program_id(0)
    d_hgrn = r_of.shape[-1]

    @pl.when(step == 0)
    def _():
        carry_ref[...] = jnp.zeros_like(carry_ref)
        logits_s[...] = jnp.zeros_like(logits_s)

    def tile_copies(t, ring_slot):
        b, r0 = t // tiles, pl.multiple_of((t % tiles) * tm, tm)

        def cp(hbm, col, width, ring):
            return pltpu.make_async_copy(hbm.at[b, pl.ds(r0, tm), pl.ds(col, width)], ring.at[ring_slot],
                                         in_sem.at[ring_slot])

        return (cp(lru_hbm, 0, d_lru, r_lru), cp(proj_hbm, y_col * d_lru, d_lru, r_y),
                cp(of_hbm, 0, d_hgrn, r_of), cp(ob_hbm, 0, d_hgrn, r_ob),
                cp(proj_hbm, g_col * d_hgrn, d_hgrn, r_g), cp(x_hbm, 0, x_hbm.shape[-1], r_x))

    @pl.when(step == 0)
    def _():
        for t in range(min(IN_RING - 1, n_tiles)):
            for c in tile_copies(t, t):
                c.start()

    ahead = step + IN_RING - 1

    @pl.when(ahead < n_tiles)
    def _():
        for c in tile_copies(ahead, ahead % IN_RING):
            c.start()

    @pl.when(step < n_tiles)
    def _():
        for c in tile_copies(step, step % IN_RING):
            c.wait()

    slot = jnp.minimum(step, n_tiles - 1) % IN_RING
    lru_ref, y_ref, of_ref, ob_ref, g_ref, x_ref = (r.at[slot] for r in (r_lru, r_y, r_of, r_ob, r_g, r_x))

    def mixer():
        lru = lru_ref[...] * _gelu_tanh(y_ref[...])
        ms = jnp.mean(lru * lru, axis=-1, keepdims=True)
        yield
        lru = lru * lax.rsqrt(ms + NORM_EPS) * nlw_ref[...]

        hg = of_ref[...] + ob_ref[...]
        width = hg.shape[1]
        hd = width // HGRN_HEADS
        hd_shift = hd.bit_length() - 1
        er = lax.broadcasted_iota(jnp.int32, (width, width), 0) >> hd_shift
        ec = lax.broadcasted_iota(jnp.int32, (width, width), 1) >> hd_shift
        head_sum = jnp.where(er == ec, 1.0, 0.0).astype(BF16)
        sq = hg * hg
        sq_hi = sq.astype(BF16)
        sq_lo = (sq - sq_hi.astype(F32)).astype(BF16)
        ms_h = (_dot(sq_hi, head_sum) + _dot(sq_lo, head_sum)) * (1.0 / hd)
        yield
        g = g_ref[...]
        hg = (hg * lax.rsqrt(ms_h + NORM_EPS) * nhw_ref[...]) * (g * _sigmoid(g))
        yield
        mixed = _dot(lru.astype(BF16), wo_ref[0:d_lru, :])
        yield
        mixed = mixed + _dot(hg.astype(BF16), wo_ref[d_lru:, :])
        yield
        x_new = x_ref[...] + gm_ref[0] * mixed
        xo_ref[0] = x_new
        ms_f = jnp.mean(x_new * x_new, axis=-1, keepdims=True)
        yield
        h = (x_new * lax.rsqrt(ms_f + NORM_EPS) * nfw_ref[...]) * (1.0 + scf_ref[0]) + shf_ref[0]
        _tiles_store(h_ref, h, tm, lead=(0,))
        yield
        logits_s[...] = _dot(h.astype(BF16), wr_ref[...]) + br_ref[...]

    def routing():
        yield from _routing_steps(logits_s[...], jnp.where(step > 0, 1.0, 0.0), slab_ref, cnt_ref, route_ref,
                                  carry_ref, tm)

    _interleave(routing(), mixer())


def _routing_steps(logits, live, slab_ref, cnt_ref, route_ref, carry_ref, tm):
    lane = lax.broadcasted_iota(jnp.int32, (tm, ROUTE_LANES), 1)
    lane_f = lane.astype(F32)
    far = float(ROUTE_LANES)
    is_g = lane < N_GROUPS
    gl = jnp.where(is_g, logits, NEG_BIG)
    gmax = jnp.max(gl, axis=-1, keepdims=True)
    yield
    g_idx = jnp.min(jnp.where(gl == gmax, lane_f, far), axis=-1, keepdims=True)
    p_group = 1.0 / jnp.sum(jnp.where(is_g, jnp.exp(gl - gmax), 0.0), axis=-1, keepdims=True)
    yield
    e_lane = lane - N_GROUPS
    in_group = (e_lane >= 0) & (e_lane < N_EXPERTS) & ((e_lane >> (EXPERTS_PER_GROUP.bit_length() - 1)).astype(F32) == g_idx)
    ev = jnp.where(in_group, logits, NEG_BIG)
    top1 = jnp.max(ev, axis=-1, keepdims=True)
    yield
    i1 = jnp.min(jnp.where(in_group & (ev == top1), lane_f, far), axis=-1, keepdims=True)
    yield
    rest = in_group & (lane_f != i1)
    ev2 = jnp.where(rest, logits, NEG_BIG)
    top2 = jnp.max(ev2, axis=-1, keepdims=True)
    yield
    i2 = jnp.min(jnp.where(rest & (ev2 == top2), lane_f, far), axis=-1, keepdims=True)
    yield
    e1 = i1 - float(N_GROUPS)
    e2 = i2 - float(N_GROUPS)
    ex = jnp.exp(top2 - top1)
    w1 = p_group / (1.0 + ex)
    w2 = p_group * ex / (1.0 + ex)

    sel1 = lane_f == e1
    sel2 = lane_f == e2
    onehot = jnp.where(sel1 | sel2, live, 0.0)
    tr = lax.broadcasted_iota(jnp.int32, (tm, tm), 0)
    tc = lax.broadcasted_iota(jnp.int32, (tm, tm), 1)
    before = jnp.where(tc < tr, 1.0, 0.0).astype(BF16)
    cnt = _dot(before, onehot.astype(BF16)) + carry_ref[0:1]
    yield
    rank1 = jnp.sum(jnp.where(sel1, cnt, 0.0), axis=-1, keepdims=True)
    rank2 = jnp.sum(jnp.where(sel2, cnt, 0.0), axis=-1, keepdims=True)
    total = carry_ref[0:1] + jnp.sum(onehot, axis=0, keepdims=True)
    carry_ref[...] = jnp.broadcast_to(total, carry_ref.shape)
    cnt_ref[...] = jnp.broadcast_to(total, cnt_ref.shape)
    yield

    slab = jnp.where(lane == 0, e1, 0.0)
    slab = jnp.where(lane == 1, e2, slab)
    slab = jnp.where(lane == 2, w1, slab)
    slab = jnp.where(lane == 3, w2, slab)
    slab = jnp.where(lane == 4, rank1, slab)
    slab = jnp.where(lane == 5, rank2, slab)
    slab_ref[0] = slab
    route_ref[...] = slab.T[0:SUBLANES]


def _post_mixer(lru_sum, proj, hg_f, hg_b, x, nlw, nhw, wo_bf16, g_mix, nfw, sc_ffn, sh_ffn, wr_bf16, br,
                *, tm=512):
    bsz, seq, d = x.shape
    d_lru = lru_sum.shape[-1]
    d_hgrn = hg_f.shape[-1]
    y_col = 1
    g_col = (2 * d_lru) // d_hgrn + 4
    tiles = seq // tm
    n_tiles = bsz * tiles
    cur = lambda s: jnp.minimum(s, n_tiles - 1)
    prev = lambda s: jnp.maximum(s - 1, 0)
    row = lambda w, col=0: pl.BlockSpec((1, tm, w), lambda s: (cur(s) // tiles, cur(s) % tiles, col))
    vec = lambda w: pl.BlockSpec((1, w), lambda s: (0, 0))
    per_b = lambda: pl.BlockSpec((1, 1, d), lambda s: (cur(s) // tiles, 0, 0))
    hbm = lambda: pl.BlockSpec(memory_space=pl.ANY)
    ring = lambda w: pltpu.VMEM((IN_RING, tm, w), F32)
    kern = functools.partial(_post_kernel, tm=tm, d_lru=d_lru, tiles=tiles, n_tiles=n_tiles, y_col=y_col,
                             g_col=g_col)
    return pl.pallas_call(
        kern,
        out_shape=(
            jax.ShapeDtypeStruct((bsz, seq, d), F32),
            jax.ShapeDtypeStruct((bsz, seq * SUBLANES, LANES), F32),
            jax.ShapeDtypeStruct((bsz, seq, ROUTE_LANES), F32),
            jax.ShapeDtypeStruct((SUBLANES, ROUTE_LANES), F32),
            jax.ShapeDtypeStruct((SUBLANES, bsz * seq), F32),
        ),
        grid=(n_tiles + 1,),
        in_specs=[
            hbm(), hbm(), hbm(), hbm(), hbm(),
            vec(d_lru), vec(d_hgrn),
            pl.BlockSpec((d, d), lambda s: (0, 0)),
            per_b(), vec(d), per_b(), per_b(),
            pl.BlockSpec((d, ROUTE_LANES), lambda s: (0, 0)),
            vec(ROUTE_LANES),
        ],
        out_specs=(
            row(d),
            pl.BlockSpec((1, tm * SUBLANES, LANES), lambda s: (cur(s) // tiles, cur(s) % tiles, 0)),
            pl.BlockSpec((1, tm, ROUTE_LANES), lambda s: (prev(s) // tiles, prev(s) % tiles, 0)),
            pl.BlockSpec((SUBLANES, ROUTE_LANES), lambda s: (0, 0)),
            pl.BlockSpec((SUBLANES, tm), lambda s: (0, prev(s))),
        ),
        scratch_shapes=[pltpu.VMEM((SUBLANES, ROUTE_LANES), F32), pltpu.VMEM((tm, ROUTE_LANES), F32),
                        ring(d_lru), ring(d_lru), ring(d_hgrn), ring(d_hgrn), ring(d_hgrn), ring(d),
                        pltpu.SemaphoreType.DMA((IN_RING,))],
        compiler_params=_params(("arbitrary",)),
        name="post_mixer_router",
    )(lru_sum, proj, hg_f, hg_b, x, nlw.reshape(1, d_lru), nhw.reshape(1, d_hgrn), wo_bf16,
      g_mix, nfw.reshape(1, d), sc_ffn, sh_ffn, wr_bf16, br)


def _tiles_load(ref, n, lead=(), first=0):
    return jnp.concatenate(
        [ref[(*lead, pl.ds(first * SUBLANES + j, n, stride=SUBLANES), slice(None))] for j in range(SUBLANES)],
        axis=1)


def _tiles_store(ref, val, n, lead=(), first=0):
    for j in range(SUBLANES):
        ref[(*lead, pl.ds(first * SUBLANES + j, n, stride=SUBLANES), slice(None))] = val[:, j * LANES:(j + 1) * LANES]


def _token_tile(ref, t):
    return ref.at[pl.ds(pl.multiple_of(t * SUBLANES, SUBLANES), SUBLANES)]


def _dest_kernel(start_ref, route_ref, o_ref):
    route = route_ref[...].astype(jnp.int32)
    start = jnp.zeros_like(route)
    for e in range(N_EXPERTS):
        start = jnp.where(route == e, start_ref[e], start)
    o_ref[...] = start + pltpu.roll(route, SUBLANES // 2, 0)


def _dest_rows(expert_start, route):
    return pl.pallas_call(
        _dest_kernel,
        out_shape=jax.ShapeDtypeStruct(route.shape, jnp.int32),
        grid_spec=pltpu.PrefetchScalarGridSpec(
            num_scalar_prefetch=1,
            grid=(1,),
            in_specs=[pl.BlockSpec(route.shape, lambda i, s: (0, 0))],
            out_specs=pl.BlockSpec(route.shape, lambda i, s: (0, 0)),
        ),
        compiler_params=pltpu.CompilerParams(dimension_semantics=("arbitrary",)),
        name="moe_dest_rows",
    )(expert_start, route)


def _dispatch_kernel(d1_ref, d2_ref, h_ref, z_ref, o_ref, sem, *, tb):
    del z_ref
    base = pl.program_id(0) * tb

    def issue(r, carry):
        t = base + r
        pltpu.make_async_copy(_token_tile(h_ref, r), _token_tile(o_ref, d1_ref[t]), sem).start(priority=0)
        pltpu.make_async_copy(_token_tile(h_ref, r), _token_tile(o_ref, d2_ref[t]), sem).start(priority=1)
        return carry

    lax.fori_loop(0, tb, issue, 0, unroll=DMA_ISSUE_UNROLL)
    for _ in range(2):
        pltpu.make_async_copy(h_ref, o_ref.at[pl.ds(0, tb * SUBLANES)], sem).wait()


def _dispatch(dest1, dest2, h_tiles, zero_tiles, *, tb=512):
    m = h_tiles.shape[0] // SUBLANES
    n_rows = zero_tiles.shape[0] // SUBLANES
    kern = functools.partial(_dispatch_kernel, tb=tb)
    return pl.pallas_call(
        kern,
        out_shape=jax.ShapeDtypeStruct((n_rows * SUBLANES, LANES), h_tiles.dtype),
        grid_spec=pltpu.PrefetchScalarGridSpec(
            num_scalar_prefetch=2,
            grid=(m // tb,),
            in_specs=[pl.BlockSpec((tb * SUBLANES, LANES), lambda i, d1, d2: (i, 0)),
                      pl.BlockSpec(memory_space=pl.ANY)],
            out_specs=pl.BlockSpec(memory_space=pl.ANY),
            scratch_shapes=[pltpu.SemaphoreType.DMA(())],
        ),
        input_output_aliases={3: 0},
        compiler_params=pltpu.CompilerParams(dimension_semantics=("arbitrary",), has_side_effects=True),
        name="moe_dispatch",
    )(dest1, dest2, h_tiles, zero_tiles)


def _expert_kernel(plan_ref, x_hbm, wg_hbm, wu_hbm, wd_hbm, o_ref, wg_f, wu_f, wd_f, wg_s, wu_s, wd_s, sem,
                   x_ring, x_sem, *, layer, blk):
    i = pl.program_id(0)
    n_blocks = pl.num_programs(0)
    n_used = plan_ref[n_blocks]
    expert = plan_ref[i]
    next_expert = plan_ref[n_blocks + 1 + i]
    slot = plan_ref[2 * n_blocks + 1 + i]
    first_block = ((i == 0) | (plan_ref[jnp.maximum(i - 1, 0)] != expert)) & (i < n_used)

    def copies(e, s):
        return (pltpu.make_async_copy(wg_hbm.at[layer, e], wg_f.at[s], sem.at[s]),
                pltpu.make_async_copy(wu_hbm.at[layer, e], wu_f.at[s], sem.at[s]),
                pltpu.make_async_copy(wd_hbm.at[layer, e], wd_f.at[s], sem.at[s]))

    @pl.when(i == 0)
    def _():
        for c in copies(expert, slot):
            c.start()

    @pl.when(first_block)
    def _():
        for c in copies(expert, slot):
            c.wait()

        @pl.when(next_expert >= 0)
        def _():
            for c in copies(next_expert, 1 - slot):
                c.start()

        wg_s[...] = wg_f[slot].astype(BF16)
        wu_s[...] = wu_f[slot].astype(BF16)
        wd_s[...] = wd_f[slot].astype(BF16)

    def x_copy(block, ring_slot):
        rows = blk * SUBLANES
        return pltpu.make_async_copy(x_hbm.at[pl.ds(pl.multiple_of(block * rows, rows), rows)],
                                     x_ring.at[ring_slot], x_sem.at[ring_slot])

    @pl.when(i == 0)
    def _():
        for b in range(X_RING - 1):
            @pl.when(b < n_used)
            def _():
                x_copy(b, b).start()

    ahead = i + X_RING - 1

    @pl.when(ahead < n_used)
    def _():
        x_copy(ahead, ahead % X_RING).start()

    fill = plan_ref[3 * n_blocks + 1 + i]
    part = blk // EXPERT_TAIL_PARTS

    def ffn(ring_slot, first, n):
        x = _tiles_load(x_ring, n, lead=(ring_slot,), first=first).astype(BF16)
        gate = _dot(x, wg_s[...])
        up = _dot(x, wu_s[...])
        act = (gate * _sigmoid(gate)) * up
        _tiles_store(o_ref, _dot(act.astype(BF16), wd_s[...]), n, first=first)

    for ring_slot in range(X_RING):
        in_slot = (i < n_used) & (i % X_RING == ring_slot)

        for k in range(1, EXPERT_TAIL_PARTS + 1):
            @pl.when(in_slot & (fill > (k - 1) * part) & (fill <= k * part))
            def _():
                x_copy(i, ring_slot).wait()
                ffn(ring_slot, 0, k * part)
                if k < EXPERT_TAIL_PARTS:
                    rest = (blk - k * part) * SUBLANES
                    o_ref[pl.ds(k * part * SUBLANES, rest), :] = jnp.zeros((rest, LANES), o_ref.dtype)

    @pl.when(i >= n_used)
    def _():
        o_ref[...] = jnp.zeros_like(o_ref)


def _expert_plan(blk_expert, blocks_used, blk_fill):
    n_blocks = blk_expert.shape[0]
    idx = jnp.arange(n_blocks, dtype=jnp.int32)
    change = jnp.concatenate([jnp.ones((1,), bool), blk_expert[1:] != blk_expert[:-1]])
    slot = (jnp.cumsum(change.astype(jnp.int32)) - 1) & 1
    change_at = jnp.where(change, idx, n_blocks)
    from_here = lax.cummin(change_at[::-1])[::-1]
    next_change = jnp.concatenate([from_here[1:], jnp.full((1,), n_blocks, jnp.int32)])
    next_expert = jnp.where(next_change < blocks_used, blk_expert[jnp.minimum(next_change, n_blocks - 1)], -1)
    return jnp.concatenate([blk_expert, blocks_used.reshape(1), next_expert, slot, blk_fill]).astype(jnp.int32)


def _experts(plan, x_tiles, wg, wu, wd, layer):
    n_rows = x_tiles.shape[0] // SUBLANES
    d, de = wg.shape[-2:]
    blk = EXPERT_BLOCK
    n_blocks = n_rows // blk
    kern = functools.partial(_expert_kernel, layer=layer, blk=blk)
    return pl.pallas_call(
        kern,
        out_shape=jax.ShapeDtypeStruct((n_rows * SUBLANES, LANES), F32),
        grid_spec=pltpu.PrefetchScalarGridSpec(
            num_scalar_prefetch=1,
            grid=(n_blocks,),
            in_specs=[
                pl.BlockSpec(memory_space=pl.ANY),
                pl.BlockSpec(memory_space=pl.ANY),
                pl.BlockSpec(memory_space=pl.ANY),
                pl.BlockSpec(memory_space=pl.ANY),
            ],
            out_specs=pl.BlockSpec((blk * SUBLANES, LANES), lambda i, plan: (i, 0)),
            scratch_shapes=[
                pltpu.VMEM((2, d, de), F32), pltpu.VMEM((2, d, de), F32), pltpu.VMEM((2, de, d), F32),
                pltpu.VMEM((d, de), BF16), pltpu.VMEM((d, de), BF16), pltpu.VMEM((de, d), BF16),
                pltpu.SemaphoreType.DMA((2,)),
                pltpu.VMEM((X_RING, blk * SUBLANES, LANES), F32), pltpu.SemaphoreType.DMA((X_RING,)),
            ],
        ),
        compiler_params=_params(("arbitrary",)),
        name="moe_experts",
    )(plan, x_tiles, wg, wu, wd)


def _combine_kernel(d1_ref, d2_ref, y_ref, slab_ref, x_ref, g_ref, nw_ref, o_ref, ra0, rb0, ra1, rb1, sem,
                    *, tm, tiles, n_steps, final_norm):
    step = pl.program_id(0) * tiles + pl.program_id(1)
    bufs = ((ra0, rb0), (ra1, rb1))

    def gather(tile, slot):
        base = tile * tm
        r1_ref, r2_ref = bufs[slot]

        def issue(r, carry):
            t = base + r
            pltpu.make_async_copy(_token_tile(y_ref, d1_ref[t]), _token_tile(r1_ref, r),
                                  sem.at[slot]).start(priority=0)
            pltpu.make_async_copy(_token_tile(y_ref, d2_ref[t]), _token_tile(r2_ref, r),
                                  sem.at[slot]).start(priority=1)
            return carry

        lax.fori_loop(0, tm, issue, 0, unroll=DMA_ISSUE_UNROLL)

    @pl.when(step == 0)
    def _():
        gather(0, 0)

    for slot in range(2):
        @pl.when((step & 1) == slot)
        def _():
            @pl.when(step + 1 < n_steps)
            def _():
                gather(step + 1, 1 - slot)

            r1_ref, r2_ref = bufs[slot]
            pltpu.make_async_copy(y_ref.at[pl.ds(0, tm * SUBLANES)], r1_ref, sem.at[slot]).wait()
            pltpu.make_async_copy(y_ref.at[pl.ds(0, tm * SUBLANES)], r2_ref, sem.at[slot]).wait()
            slab = slab_ref[0]
            y = slab[:, 2:3] * _tiles_load(r1_ref, tm) + slab[:, 3:4] * _tiles_load(r2_ref, tm)
            out = x_ref[0] + g_ref[0] * y
            if final_norm:
                ms = jnp.mean(out * out, axis=-1, keepdims=True)
                out = out * lax.rsqrt(ms + NORM_EPS) * nw_ref[...]
            o_ref[0] = out


def _combine(dest1, dest2, y_buf, slab, x, g_ffn, norm_w, *, final_norm, tm=512):
    bsz, seq, d = x.shape
    tiles = seq // tm
    kern = functools.partial(_combine_kernel, tm=tm, tiles=tiles, n_steps=bsz * tiles, final_norm=final_norm)
    row_buf = pltpu.VMEM((tm * SUBLANES, LANES), F32)
    return pl.pallas_call(
        kern,
        out_shape=jax.ShapeDtypeStruct((bsz, seq, d), F32),
        grid_spec=pltpu.PrefetchScalarGridSpec(
            num_scalar_prefetch=2,
            grid=(bsz, tiles),
            in_specs=[
                pl.BlockSpec(memory_space=pl.ANY),
                pl.BlockSpec((1, tm, ROUTE_LANES), lambda b, i, d1, d2: (b, i, 0)),
                pl.BlockSpec((1, tm, d), lambda b, i, d1, d2: (b, i, 0)),
                pl.BlockSpec((1, 1, d), lambda b, i, d1, d2: (b, 0, 0)),
                pl.BlockSpec((1, d), lambda b, i, d1, d2: (0, 0)),
            ],
            out_specs=pl.BlockSpec((1, tm, d), lambda b, i, d1, d2: (b, i, 0)),
            scratch_shapes=[row_buf, row_buf, row_buf, row_buf, pltpu.SemaphoreType.DMA((2,))],
        ),
        compiler_params=_params(("arbitrary", "arbitrary")),
        name="moe_combine",
    )(dest1, dest2, y_buf, slab, x, g_ffn, norm_w.reshape(1, d))


def _block_diag(w):
    heads, hd, _ = w.shape
    n = heads * hd
    tiled = jnp.tile(w.reshape(n, hd), (1, heads))
    blk_r = lax.broadcasted_iota(jnp.int32, (n, n), 0) // hd
    blk_c = lax.broadcasted_iota(jnp.int32, (n, n), 1) // hd
    return jnp.where(blk_r == blk_c, tiled, 0.0)


def kernel(x, c, ada_w, ada_b, norm_mix_w, w_in, conv_w, conv_b, lru_wa, lru_ba, lru_wx, lru_bx, lru_lambda, norm_lru_w, hgrn_lb, norm_hgrn_w, w_out, norm_ffn_w, router_group_w, router_group_b, router_expert_w, router_expert_b, expert_w_gate, expert_w_up, expert_w_down, final_norm_w):
    bsz, seq, d = x.shape
    assert d == SUBLANES * LANES, "the MoE row movement keeps one (8, 128) tile per token"
    depth = ada_w.shape[0]
    d_lru = conv_w.shape[-1]
    d_hgrn = hgrn_lb.shape[-1]
    m = bsz * seq
    n_rows = m * 2 + N_EXPERTS * EXPERT_BLOCK
    n_blocks = n_rows // EXPERT_BLOCK

    mod = _modulation(c, ada_w, ada_b)
    lb_cum = jnp.cumsum(jax.nn.softmax(hgrn_lb.astype(F32), axis=0), axis=0)
    lb_all = lb_cum - lb_cum[0:1]

    for l in range(depth):
        sh_mix, sc_mix, g_mix, sh_ffn, sc_ffn, g_ffn = [
            mod[l, :, i * d:(i + 1) * d].reshape(bsz, 1, d) for i in range(N_MODULATIONS)]
        wa_bd = jnp.stack([_block_diag(lru_wa[l, 0]), _block_diag(lru_wa[l, 1])]).astype(BF16)
        wx_bd = jnp.stack([_block_diag(lru_wx[l, 0]), _block_diag(lru_wx[l, 1])]).astype(BF16)
        lru_w = (conv_w[l], conv_b[l], wa_bd, lru_ba[l], wx_bd, lru_bx[l], lru_lambda[l])
        proj, lru_fwd = _in_proj_lru(x, norm_mix_w[l], sc_mix, sh_mix, w_in, l, *lru_w)
        lru_sum = _lru_scan(proj, *lru_w, reverse=True, add_to=lru_fwd)
        hg_f, hg_b, zero_tiles = _hgrn(proj, lb_all[l], n_rows * SUBLANES, d_lru=d_lru, d_hgrn=d_hgrn)

        lane_pad = ROUTE_LANES - N_GROUPS - N_EXPERTS
        wr = jnp.pad(jnp.concatenate([router_group_w[l], router_expert_w[l]], axis=1), ((0, 0), (0, lane_pad)))
        br = jnp.pad(jnp.concatenate([router_group_b[l], router_expert_b[l]]), (0, lane_pad)).reshape(1, ROUTE_LANES)
        x_mid, h_ffn, slab, counts, route = _post_mixer(
            lru_sum, proj, hg_f, hg_b, x, norm_lru_w[l], norm_hgrn_w[l], w_out[l].astype(BF16), g_mix,
            norm_ffn_w[l], sc_ffn, sh_ffn, wr.astype(BF16), br)

        cnt = counts[0, :N_EXPERTS].astype(jnp.int32)
        padded = ((cnt + EXPERT_BLOCK - 1) // EXPERT_BLOCK) * EXPERT_BLOCK
        pend = jnp.cumsum(padded)
        pstart = pend - padded
        blk_start = jnp.arange(n_blocks, dtype=jnp.int32) * EXPERT_BLOCK
        blk_expert = jnp.minimum(jnp.sum(pend[None, :] <= blk_start[:, None], axis=1), N_EXPERTS - 1)
        blk_fill = jnp.clip((pstart + cnt)[blk_expert] - blk_start, 0, EXPERT_BLOCK)
        plan = _expert_plan(blk_expert.astype(jnp.int32), (pend[N_EXPERTS - 1] // EXPERT_BLOCK).astype(jnp.int32),
                            blk_fill)
        dest = _dest_rows(pstart.astype(jnp.int32), route)
        dest1, dest2 = dest[0], dest[1]

        x_buf = _dispatch(dest1, dest2, h_ffn.reshape(m * SUBLANES, LANES), zero_tiles)
        y_buf = _experts(plan, x_buf, expert_w_gate, expert_w_up, expert_w_down, l)
        x = _combine(dest1, dest2, y_buf, slab, x_mid, g_ffn, final_norm_w, final_norm=(l == depth - 1))

    return x
```
